```python
import math
import jax, jax.numpy as jnp
from jax import lax
import numpy as np

D_MODEL = 1024
BATCH = 8
SEQ = 4096
DEPTH = 2

CHUNK = 64
D_FF = 2816
FFN_RES = 0.5
D_CONV = D_MODEL // 2
CONV_A_WIDTH = 31
N_HEADS_B = 8
HEAD_DIM = 64
D_ATTN = N_HEADS_B * HEAD_DIM
Q_BLOCK = 128
CONV_C_WIDTH = 3
D_SHORT = D_MODEL
N_EVEN = (DEPTH + 1) // 2
N_ODD = DEPTH // 2
D_IN_EVEN = 2 * D_CONV + 3 * D_ATTN + N_HEADS_B
D_IN_ODD = 3 * D_SHORT
EPS = 1e-6

kernel_name = "hybrid_conformer_fox_shortconv_trunk"


def rmsnorm(x, g):
    xf = x.astype(jnp.float32)
    y = xf * lax.rsqrt(jnp.mean(xf * xf, axis=-1, keepdims=True) + EPS)
    return (y * g.astype(jnp.float32)).astype(x.dtype)


def swiglu(x, w_gate, w_up, w_down):
    return (jax.nn.silu(x @ w_gate) * (x @ w_up)) @ w_down


def causal_depthwise_conv(x, w):
    k_width = w.shape[0]
    return lax.conv_general_dilated(
        x, w[:, None, :].astype(x.dtype), window_strides=(1,),
        padding=[(k_width - 1, 0)],
        dimension_numbers=("NWC", "WIO", "NWC"),
        feature_group_count=x.shape[-1])


def forgetting_attention(q, k, v, log_f):
    b, s_len, h, dh = q.shape
    nb = s_len // Q_BLOCK
    scale = 1.0 / math.sqrt(dh)
    cum = jnp.cumsum(log_f, axis=1).transpose(0, 2, 1)
    qh = q.transpose(0, 2, 1, 3)
    kh = k.transpose(0, 2, 1, 3)
    vh = v.transpose(0, 2, 1, 3)
    q_blocks = qh.reshape(b, h, nb, Q_BLOCK, dh).transpose(2, 0, 1, 3, 4)
    f_blocks = cum.reshape(b, h, nb, Q_BLOCK).transpose(2, 0, 1, 3)
    q_pos = jnp.arange(s_len).reshape(nb, Q_BLOCK)
    k_pos = jnp.arange(s_len)

    def block(args):
        qi, fi, pi = args
        logits = jnp.einsum("bhqd,bhkd->bhqk", qi, kh,
                            preferred_element_type=jnp.float32) * scale
        logits = logits + fi[..., None] - cum[:, :, None, :]
        logits = jnp.where(k_pos[None, :] <= pi[:, None], logits, -jnp.inf)
        p = jax.nn.softmax(logits, axis=-1)
        return jnp.einsum("bhqk,bhkd->bhqd", p.astype(vh.dtype), vh)

    o = lax.map(block, (q_blocks, f_blocks, q_pos))
    return o.transpose(1, 0, 3, 2, 4).reshape(b, s_len, h * dh)


def even_mixer(h, w_in, b_f, conv_w, conv_b, conv_norm, q_norm, k_norm, w_out):
    b, s_len, _ = h.shape
    z = h @ w_in
    splits = np.cumsum([D_CONV, D_CONV, D_ATTN, D_ATTN, D_ATTN]).tolist()
    u, g, q, k, v, f_logit = jnp.split(z, splits, axis=-1)
    a = u * jax.nn.sigmoid(g)
    a = causal_depthwise_conv(a, conv_w) + conv_b
    a = jax.nn.silu(rmsnorm(a, conv_norm))
    q = rmsnorm(q.reshape(b, s_len, N_HEADS_B, HEAD_DIM), q_norm)
    k = rmsnorm(k.reshape(b, s_len, N_HEADS_B, HEAD_DIM), k_norm)
    v = v.reshape(b, s_len, N_HEADS_B, HEAD_DIM)
    log_f = jax.nn.log_sigmoid(f_logit.astype(jnp.float32) + b_f.astype(jnp.float32))
    o = forgetting_attention(q, k, v, log_f)
    return jnp.concatenate([a, o.astype(a.dtype)], axis=-1) @ w_out


def odd_mixer(h, w_in, conv_w, w_out):
    z = h @ w_in
    gate_b, gate_c, hh = jnp.split(z, 3, axis=-1)
    y = gate_b * causal_depthwise_conv(gate_c * hh, conv_w)
    return y @ w_out


def _fwd_setup_inputs(seed: int = 0) -> dict:
    key = jax.random.key(seed)
    ks = iter(jax.random.split(key, 32))
    f32 = jnp.float32

    def nrm(shape, fan_in):
        return jax.random.normal(next(ks), shape, f32) * (fan_in ** -0.5)

    def gain(shape):
        return 1.0 + 0.02 * jax.random.normal(next(ks), shape, f32)

    return {
        "x": jax.random.normal(next(ks), (BATCH, SEQ, D_MODEL), f32),
        "ffn1_norm": gain((DEPTH, D_MODEL)),
        "ffn1_w_gate": nrm((DEPTH, D_MODEL, D_FF), D_MODEL),
        "ffn1_w_up": nrm((DEPTH, D_MODEL, D_FF), D_MODEL),
        "ffn1_w_down": nrm((DEPTH, D_FF, D_MODEL), D_FF),
        "mix_norm": gain((DEPTH, D_MODEL)),
        "ffn2_norm": gain((DEPTH, D_MODEL)),
        "ffn2_w_gate": nrm((DEPTH, D_MODEL, D_FF), D_MODEL),
        "ffn2_w_up": nrm((DEPTH, D_MODEL, D_FF), D_MODEL),
        "ffn2_w_down": nrm((DEPTH, D_FF, D_MODEL), D_FF),
        "ev_w_in": nrm((N_EVEN, D_MODEL, D_IN_EVEN), D_MODEL),
        "ev_b_f": jax.random.uniform(next(ks), (N_EVEN, N_HEADS_B), f32, 1.0, 5.0),
        "ev_conv_w": nrm((N_EVEN, CONV_A_WIDTH, D_CONV), CONV_A_WIDTH),
        "ev_conv_b": 0.01 * jax.random.normal(next(ks), (N_EVEN, D_CONV), f32),
        "ev_conv_norm": gain((N_EVEN, D_CONV)),
        "ev_q_norm": gain((N_EVEN, HEAD_DIM)),
        "ev_k_norm": gain((N_EVEN, HEAD_DIM)),
        "ev_w_out": nrm((N_EVEN, D_CONV + D_ATTN, D_MODEL), D_CONV + D_ATTN),
        "od_w_in": nrm((N_ODD, D_MODEL, D_IN_ODD), D_MODEL),
        "od_conv_w": nrm((N_ODD, CONV_C_WIDTH, D_SHORT), CONV_C_WIDTH),
        "od_w_out": nrm((N_ODD, D_SHORT, D_MODEL), D_SHORT),
    }


def _fwd_reference(x, ffn1_norm, ffn1_w_gate, ffn1_w_up, ffn1_w_down, mix_norm,
              ffn2_norm, ffn2_w_gate, ffn2_w_up, ffn2_w_down,
              ev_w_in, ev_b_f, ev_conv_w, ev_conv_b, ev_conv_norm, ev_q_norm,
              ev_k_norm, ev_w_out, od_w_in, od_conv_w, od_w_out):
    for layer in range(DEPTH):
        x = x + FFN_RES * swiglu(rmsnorm(x, ffn1_norm[layer]), ffn1_w_gate[layer],
                                 ffn1_w_up[layer], ffn1_w_down[layer])
        h = rmsnorm(x, mix_norm[layer])
        if layer % 2 == 0:
            i = layer // 2
            x = x + even_mixer(h, ev_w_in[i], ev_b_f[i], ev_conv_w[i], ev_conv_b[i],
                               ev_conv_norm[i], ev_q_norm[i], ev_k_norm[i], ev_w_out[i])
        else:
            i = layer // 2
            x = x + odd_mixer(h, od_w_in[i], od_conv_w[i], od_w_out[i])
        x = x + FFN_RES * swiglu(rmsnorm(x, ffn2_norm[layer]), ffn2_w_gate[layer],
                                 ffn2_w_up[layer], ffn2_w_down[layer])
    return x


import jax as _jax
import jax.numpy as _jnp

TWIN_FORMAT = 'train_step'
FWD_PARAMS = ['x', 'ffn1_norm', 'ffn1_w_gate', 'ffn1_w_up', 'ffn1_w_down', 'mix_norm', 'ffn2_norm', 'ffn2_w_gate', 'ffn2_w_up', 'ffn2_w_down', 'ev_w_in', 'ev_b_f', 'ev_conv_w', 'ev_conv_b', 'ev_conv_norm', 'ev_q_norm', 'ev_k_norm', 'ev_w_out', 'od_w_in', 'od_conv_w', 'od_w_out']
TWIN_WEIGHTS = ['ffn1_norm', 'ffn1_w_gate', 'ffn1_w_up', 'ffn1_w_down', 'mix_norm', 'ffn2_norm', 'ffn2_w_gate', 'ffn2_w_up', 'ffn2_w_down', 'ev_w_in', 'ev_b_f', 'ev_conv_w', 'ev_conv_b', 'ev_conv_norm', 'ev_q_norm', 'ev_k_norm', 'ev_w_out', 'od_w_in', 'od_conv_w', 'od_w_out']
TWIN_DIFF_INPUT = 'x'
TWIN_INPUTS = ['x', 'ffn1_norm', 'ffn1_w_gate', 'ffn1_w_up', 'ffn1_w_down', 'mix_norm', 'ffn2_norm', 'ffn2_w_gate', 'ffn2_w_up', 'ffn2_w_down', 'ev_w_in', 'ev_b_f', 'ev_conv_w', 'ev_conv_b', 'ev_conv_norm', 'ev_q_norm', 'ev_k_norm', 'ev_w_out', 'od_w_in', 'od_conv_w', 'od_w_out', 'loss_target', 'm_ffn1_norm', 'm_ffn1_w_gate', 'm_ffn1_w_up', 'm_ffn1_w_down', 'm_mix_norm', 'm_ffn2_norm', 'm_ffn2_w_gate', 'm_ffn2_w_up', 'm_ffn2_w_down', 'm_ev_w_in', 'm_ev_b_f', 'm_ev_conv_w', 'm_ev_conv_b', 'm_ev_conv_norm', 'm_ev_q_norm', 'm_ev_k_norm', 'm_ev_w_out', 'm_od_w_in', 'm_od_conv_w', 'm_od_w_out', 'v_ffn1_norm', 'v_ffn1_w_gate', 'v_ffn1_w_up', 'v_ffn1_w_down', 'v_mix_norm', 'v_ffn2_norm', 'v_ffn2_w_gate', 'v_ffn2_w_up', 'v_ffn2_w_down', 'v_ev_w_in', 'v_ev_b_f', 'v_ev_conv_w', 'v_ev_conv_b', 'v_ev_conv_norm', 'v_ev_q_norm', 'v_ev_k_norm', 'v_ev_w_out', 'v_od_w_in', 'v_od_conv_w', 'v_od_w_out']
TWIN_OUTPUTS = ['loss', 'grad_x', 'grad_ffn1_norm', 'grad_ffn1_w_gate', 'grad_ffn1_w_up', 'grad_ffn1_w_down', 'grad_mix_norm', 'grad_ffn2_norm', 'grad_ffn2_w_gate', 'grad_ffn2_w_up', 'grad_ffn2_w_down', 'grad_ev_w_in', 'grad_ev_b_f', 'grad_ev_conv_w', 'grad_ev_conv_b', 'grad_ev_conv_norm', 'grad_ev_q_norm', 'grad_ev_k_norm', 'grad_ev_w_out', 'grad_od_w_in', 'grad_od_conv_w', 'grad_od_w_out', 'delta_ffn1_norm', 'delta_ffn1_w_gate', 'delta_ffn1_w_up', 'delta_ffn1_w_down', 'delta_mix_norm', 'delta_ffn2_norm', 'delta_ffn2_w_gate', 'delta_ffn2_w_up', 'delta_ffn2_w_down', 'delta_ev_w_in', 'delta_ev_b_f', 'delta_ev_conv_w', 'delta_ev_conv_b', 'delta_ev_conv_norm', 'delta_ev_q_norm', 'delta_ev_k_norm', 'delta_ev_w_out', 'delta_od_w_in', 'delta_od_conv_w', 'delta_od_w_out', 'new_m_ffn1_norm', 'new_m_ffn1_w_gate', 'new_m_ffn1_w_up', 'new_m_ffn1_w_down', 'new_m_mix_norm', 'new_m_ffn2_norm', 'new_m_ffn2_w_gate', 'new_m_ffn2_w_up', 'new_m_ffn2_w_down', 'new_m_ev_w_in', 'new_m_ev_b_f', 'new_m_ev_conv_w', 'new_m_ev_conv_b', 'new_m_ev_conv_norm', 'new_m_ev_q_norm', 'new_m_ev_k_norm', 'new_m_ev_w_out', 'new_m_od_w_in', 'new_m_od_conv_w', 'new_m_od_w_out', 'new_v_ffn1_norm', 'new_v_ffn1_w_gate', 'new_v_ffn1_w_up', 'new_v_ffn1_w_down', 'new_v_mix_norm', 'new_v_ffn2_norm', 'new_v_ffn2_w_gate', 'new_v_ffn2_w_up', 'new_v_ffn2_w_down', 'new_v_ev_w_in', 'new_v_ev_b_f', 'new_v_ev_conv_w', 'new_v_ev_conv_b', 'new_v_ev_conv_norm', 'new_v_ev_q_norm', 'new_v_ev_k_norm', 'new_v_ev_w_out', 'new_v_od_w_in', 'new_v_od_conv_w', 'new_v_od_w_out']
TWIN_LEAF_KINDS = {'loss': 'loss', 'grad_x': 'grad_x', 'grad_ffn1_norm': 'grad_w', 'grad_ffn1_w_gate': 'grad_w', 'grad_ffn1_w_up': 'grad_w', 'grad_ffn1_w_down': 'grad_w', 'grad_mix_norm': 'grad_w', 'grad_ffn2_norm': 'grad_w', 'grad_ffn2_w_gate': 'grad_w', 'grad_ffn2_w_up': 'grad_w', 'grad_ffn2_w_down': 'grad_w', 'grad_ev_w_in': 'grad_w', 'grad_ev_b_f': 'grad_w', 'grad_ev_conv_w': 'grad_w', 'grad_ev_conv_b': 'grad_w', 'grad_ev_conv_norm': 'grad_w', 'grad_ev_q_norm': 'grad_w', 'grad_ev_k_norm': 'grad_w', 'grad_ev_w_out': 'grad_w', 'grad_od_w_in': 'grad_w', 'grad_od_conv_w': 'grad_w', 'grad_od_w_out': 'grad_w', 'delta_ffn1_norm': 'delta_w', 'delta_ffn1_w_gate': 'delta_w', 'delta_ffn1_w_up': 'delta_w', 'delta_ffn1_w_down': 'delta_w', 'delta_mix_norm': 'delta_w', 'delta_ffn2_norm': 'delta_w', 'delta_ffn2_w_gate': 'delta_w', 'delta_ffn2_w_up': 'delta_w', 'delta_ffn2_w_down': 'delta_w', 'delta_ev_w_in': 'delta_w', 'delta_ev_b_f': 'delta_w', 'delta_ev_conv_w': 'delta_w', 'delta_ev_conv_b': 'delta_w', 'delta_ev_conv_norm': 'delta_w', 'delta_ev_q_norm': 'delta_w', 'delta_ev_k_norm': 'delta_w', 'delta_ev_w_out': 'delta_w', 'delta_od_w_in': 'delta_w', 'delta_od_conv_w': 'delta_w', 'delta_od_w_out': 'delta_w', 'new_m_ffn1_norm': 'new_m', 'new_m_ffn1_w_gate': 'new_m', 'new_m_ffn1_w_up': 'new_m', 'new_m_ffn1_w_down': 'new_m', 'new_m_mix_norm': 'new_m', 'new_m_ffn2_norm': 'new_m', 'new_m_ffn2_w_gate': 'new_m', 'new_m_ffn2_w_up': 'new_m', 'new_m_ffn2_w_down': 'new_m', 'new_m_ev_w_in': 'new_m', 'new_m_ev_b_f': 'new_m', 'new_m_ev_conv_w': 'new_m', 'new_m_ev_conv_b': 'new_m', 'new_m_ev_conv_norm': 'new_m', 'new_m_ev_q_norm': 'new_m', 'new_m_ev_k_norm': 'new_m', 'new_m_ev_w_out': 'new_m', 'new_m_od_w_in': 'new_m', 'new_m_od_conv_w': 'new_m', 'new_m_od_w_out': 'new_m', 'new_v_ffn1_norm': 'new_v', 'new_v_ffn1_w_gate': 'new_v', 'new_v_ffn1_w_up': 'new_v', 'new_v_ffn1_w_down': 'new_v', 'new_v_mix_norm': 'new_v', 'new_v_ffn2_norm': 'new_v', 'new_v_ffn2_w_gate': 'new_v', 'new_v_ffn2_w_up': 'new_v', 'new_v_ffn2_w_down': 'new_v', 'new_v_ev_w_in': 'new_v', 'new_v_ev_b_f': 'new_v', 'new_v_ev_conv_w': 'new_v', 'new_v_ev_conv_b': 'new_v', 'new_v_ev_conv_norm': 'new_v', 'new_v_ev_q_norm': 'new_v', 'new_v_ev_k_norm': 'new_v', 'new_v_ev_w_out': 'new_v', 'new_v_od_w_in': 'new_v', 'new_v_od_conv_w': 'new_v', 'new_v_od_w_out': 'new_v'}


def _forward(args):
    return _fwd_reference(*[args[k] for k in FWD_PARAMS])


def _output_shape():
    out = _jax.eval_shape(lambda: _forward(_fwd_setup_inputs(0)))
    return out.shape, out.dtype

N_MICROBATCH = 1
ADAM_LR = 0.001
ADAM_B1 = 0.9
ADAM_B2 = 0.999
ADAM_EPS = 1e-08
ADAM_WD = 0.01
ADAM_STEP = 10
PER_EXAMPLE_BATCH_AXIS = {'x': 0, 'loss_target': 0}
SHARED_INPUTS = []
_WEIGHT_DTYPES = {'ffn1_norm': _jnp.float32, 'ffn1_w_gate': _jnp.float32, 'ffn1_w_up': _jnp.float32, 'ffn1_w_down': _jnp.float32, 'mix_norm': _jnp.float32, 'ffn2_norm': _jnp.float32, 'ffn2_w_gate': _jnp.float32, 'ffn2_w_up': _jnp.float32, 'ffn2_w_down': _jnp.float32, 'ev_w_in': _jnp.float32, 'ev_b_f': _jnp.float32, 'ev_conv_w': _jnp.float32, 'ev_conv_b': _jnp.float32, 'ev_conv_norm': _jnp.float32, 'ev_q_norm': _jnp.float32, 'ev_k_norm': _jnp.float32, 'ev_w_out': _jnp.float32, 'od_w_in': _jnp.float32, 'od_conv_w': _jnp.float32, 'od_w_out': _jnp.float32}
MOMENT_SCALE = {'ffn1_norm': 5.689231e+00, 'ffn1_w_gate': 2.314547e-01, 'ffn1_w_up': 2.322440e-01, 'ffn1_w_down': 3.848313e-01, 'mix_norm': 6.535462e+01, 'ffn2_norm': 6.101196e+00, 'ffn2_w_gate': 1.661750e-01, 'ffn2_w_up': 1.714688e-01, 'ffn2_w_down': 2.836910e-01, 'ev_w_in': 4.407264e-01, 'ev_b_f': 6.618868e+01, 'ev_conv_w': 8.293399e-01, 'ev_conv_b': 1.721921e+01, 'ev_conv_norm': 1.407597e+01, 'ev_q_norm': 9.183090e+00, 'ev_k_norm': 9.153906e+00, 'ev_w_out': 1.084817e+00, 'od_w_in': 1.090583e+00, 'od_conv_w': 1.812042e+01, 'od_w_out': 9.091388e-01}


def _to_microbatches(a, axis):
    t = _jnp.moveaxis(a, axis, 0)
    t = t.reshape((N_MICROBATCH, t.shape[0] // N_MICROBATCH) + t.shape[1:])
    return _jnp.moveaxis(t, 1, axis + 1)


def setup_inputs(seed: int = 0) -> dict:
    inp = _fwd_setup_inputs(seed)
    key = _jax.random.fold_in(_jax.random.key(seed), 7919)
    shape, _ = _output_shape()
    out = dict(inp)
    out["loss_target"] = _jax.random.normal(_jax.random.fold_in(key, 0), shape, _jnp.float32)
    for i, name in enumerate(TWIN_WEIGHTS):
        w = inp[name].astype(_jnp.float32)
        if MOMENT_SCALE is None:
            s = _jnp.sqrt(_jnp.mean(_jnp.square(w)) + 1e-30)
        else:
            s = MOMENT_SCALE[name]
        km, kv = _jax.random.split(_jax.random.fold_in(key, i + 1))
        out[name] = w
        out["m_" + name] = s * _jax.random.normal(km, w.shape, _jnp.float32)
        out["v_" + name] = (s * s) * _jax.random.uniform(kv, w.shape, _jnp.float32, 0.5, 1.5)
    if N_MICROBATCH > 1:
        for name, axis in PER_EXAMPLE_BATCH_AXIS.items():
            out[name] = _to_microbatches(out[name], axis)
    return {'x': out['x'], 'ffn1_norm': out['ffn1_norm'], 'ffn1_w_gate': out['ffn1_w_gate'], 'ffn1_w_up': out['ffn1_w_up'], 'ffn1_w_down': out['ffn1_w_down'], 'mix_norm': out['mix_norm'], 'ffn2_norm': out['ffn2_norm'], 'ffn2_w_gate': out['ffn2_w_gate'], 'ffn2_w_up': out['ffn2_w_up'], 'ffn2_w_down': out['ffn2_w_down'], 'ev_w_in': out['ev_w_in'], 'ev_b_f': out['ev_b_f'], 'ev_conv_w': out['ev_conv_w'], 'ev_conv_b': out['ev_conv_b'], 'ev_conv_norm': out['ev_conv_norm'], 'ev_q_norm': out['ev_q_norm'], 'ev_k_norm': out['ev_k_norm'], 'ev_w_out': out['ev_w_out'], 'od_w_in': out['od_w_in'], 'od_conv_w': out['od_conv_w'], 'od_w_out': out['od_w_out'], 'loss_target': out['loss_target'], 'm_ffn1_norm': out['m_ffn1_norm'], 'm_ffn1_w_gate': out['m_ffn1_w_gate'], 'm_ffn1_w_up': out['m_ffn1_w_up'], 'm_ffn1_w_down': out['m_ffn1_w_down'], 'm_mix_norm': out['m_mix_norm'], 'm_ffn2_norm': out['m_ffn2_norm'], 'm_ffn2_w_gate': out['m_ffn2_w_gate'], 'm_ffn2_w_up': out['m_ffn2_w_up'], 'm_ffn2_w_down': out['m_ffn2_w_down'], 'm_ev_w_in': out['m_ev_w_in'], 'm_ev_b_f': out['m_ev_b_f'], 'm_ev_conv_w': out['m_ev_conv_w'], 'm_ev_conv_b': out['m_ev_conv_b'], 'm_ev_conv_norm': out['m_ev_conv_norm'], 'm_ev_q_norm': out['m_ev_q_norm'], 'm_ev_k_norm': out['m_ev_k_norm'], 'm_ev_w_out': out['m_ev_w_out'], 'm_od_w_in': out['m_od_w_in'], 'm_od_conv_w': out['m_od_conv_w'], 'm_od_w_out': out['m_od_w_out'], 'v_ffn1_norm': out['v_ffn1_norm'], 'v_ffn1_w_gate': out['v_ffn1_w_gate'], 'v_ffn1_w_up': out['v_ffn1_w_up'], 'v_ffn1_w_down': out['v_ffn1_w_down'], 'v_mix_norm': out['v_mix_norm'], 'v_ffn2_norm': out['v_ffn2_norm'], 'v_ffn2_w_gate': out['v_ffn2_w_gate'], 'v_ffn2_w_up': out['v_ffn2_w_up'], 'v_ffn2_w_down': out['v_ffn2_w_down'], 'v_ev_w_in': out['v_ev_w_in'], 'v_ev_b_f': out['v_ev_b_f'], 'v_ev_conv_w': out['v_ev_conv_w'], 'v_ev_conv_b': out['v_ev_conv_b'], 'v_ev_conv_norm': out['v_ev_conv_norm'], 'v_ev_q_norm': out['v_ev_q_norm'], 'v_ev_k_norm': out['v_ev_k_norm'], 'v_ev_w_out': out['v_ev_w_out'], 'v_od_w_in': out['v_od_w_in'], 'v_od_conv_w': out['v_od_conv_w'], 'v_od_w_out': out['v_od_w_out']}


def _loss(weights, diff, rest, loss_target):
    with _jax.named_scope("forward"):
        args = {**rest, TWIN_DIFF_INPUT: diff, **{k: w.astype(_WEIGHT_DTYPES[k]) for k, w in weights.items()}}
        y = _forward(args)
    with _jax.named_scope("loss_head"):
        err = _jnp.square(y.astype(_jnp.float32) - loss_target)
        return 0.5 * _jnp.sum(_jnp.mean(err, axis=-1)) if err.ndim else 0.5 * err


def _adamw(w, g, m, v):
    m = ADAM_B1 * m + (1.0 - ADAM_B1) * g
    v = ADAM_B2 * v + (1.0 - ADAM_B2) * _jnp.square(g)
    m_hat = m / (1.0 - ADAM_B1 ** ADAM_STEP)
    v_hat = v / (1.0 - ADAM_B2 ** ADAM_STEP)
    delta = -ADAM_LR * (m_hat / (_jnp.sqrt(v_hat) + ADAM_EPS) + ADAM_WD * w)
    return delta, m, v


def reference(x, ffn1_norm, ffn1_w_gate, ffn1_w_up, ffn1_w_down, mix_norm, ffn2_norm, ffn2_w_gate, ffn2_w_up, ffn2_w_down, ev_w_in, ev_b_f, ev_conv_w, ev_conv_b, ev_conv_norm, ev_q_norm, ev_k_norm, ev_w_out, od_w_in, od_conv_w, od_w_out, loss_target, m_ffn1_norm, m_ffn1_w_gate, m_ffn1_w_up, m_ffn1_w_down, m_mix_norm, m_ffn2_norm, m_ffn2_w_gate, m_ffn2_w_up, m_ffn2_w_down, m_ev_w_in, m_ev_b_f, m_ev_conv_w, m_ev_conv_b, m_ev_conv_norm, m_ev_q_norm, m_ev_k_norm, m_ev_w_out, m_od_w_in, m_od_conv_w, m_od_w_out, v_ffn1_norm, v_ffn1_w_gate, v_ffn1_w_up, v_ffn1_w_down, v_mix_norm, v_ffn2_norm, v_ffn2_w_gate, v_ffn2_w_up, v_ffn2_w_down, v_ev_w_in, v_ev_b_f, v_ev_conv_w, v_ev_conv_b, v_ev_conv_norm, v_ev_q_norm, v_ev_k_norm, v_ev_w_out, v_od_w_in, v_od_conv_w, v_od_w_out):
    given = dict(x=x, ffn1_norm=ffn1_norm, ffn1_w_gate=ffn1_w_gate, ffn1_w_up=ffn1_w_up, ffn1_w_down=ffn1_w_down, mix_norm=mix_norm, ffn2_norm=ffn2_norm, ffn2_w_gate=ffn2_w_gate, ffn2_w_up=ffn2_w_up, ffn2_w_down=ffn2_w_down, ev_w_in=ev_w_in, ev_b_f=ev_b_f, ev_conv_w=ev_conv_w, ev_conv_b=ev_conv_b, ev_conv_norm=ev_conv_norm, ev_q_norm=ev_q_norm, ev_k_norm=ev_k_norm, ev_w_out=ev_w_out, od_w_in=od_w_in, od_conv_w=od_conv_w, od_w_out=od_w_out, loss_target=loss_target, m_ffn1_norm=m_ffn1_norm, m_ffn1_w_gate=m_ffn1_w_gate, m_ffn1_w_up=m_ffn1_w_up, m_ffn1_w_down=m_ffn1_w_down, m_mix_norm=m_mix_norm, m_ffn2_norm=m_ffn2_norm, m_ffn2_w_gate=m_ffn2_w_gate, m_ffn2_w_up=m_ffn2_w_up, m_ffn2_w_down=m_ffn2_w_down, m_ev_w_in=m_ev_w_in, m_ev_b_f=m_ev_b_f, m_ev_conv_w=m_ev_conv_w, m_ev_conv_b=m_ev_conv_b, m_ev_conv_norm=m_ev_conv_norm, m_ev_q_norm=m_ev_q_norm, m_ev_k_norm=m_ev_k_norm, m_ev_w_out=m_ev_w_out, m_od_w_in=m_od_w_in, m_od_conv_w=m_od_conv_w, m_od_w_out=m_od_w_out, v_ffn1_norm=v_ffn1_norm, v_ffn1_w_gate=v_ffn1_w_gate, v_ffn1_w_up=v_ffn1_w_up, v_ffn1_w_down=v_ffn1_w_down, v_mix_norm=v_mix_norm, v_ffn2_norm=v_ffn2_norm, v_ffn2_w_gate=v_ffn2_w_gate, v_ffn2_w_up=v_ffn2_w_up, v_ffn2_w_down=v_ffn2_w_down, v_ev_w_in=v_ev_w_in, v_ev_b_f=v_ev_b_f, v_ev_conv_w=v_ev_conv_w, v_ev_conv_b=v_ev_conv_b, v_ev_conv_norm=v_ev_conv_norm, v_ev_q_norm=v_ev_q_norm, v_ev_k_norm=v_ev_k_norm, v_ev_w_out=v_ev_w_out, v_od_w_in=v_od_w_in, v_od_conv_w=v_od_conv_w, v_od_w_out=v_od_w_out)
    weights = {n: given[n] for n in TWIN_WEIGHTS}
    shared = {n: given[n] for n in SHARED_INPUTS}
    per_example = {n: given[n] for n in ['x']}
    grad_fn = _jax.value_and_grad(_loss, argnums=(0, 1))

    def one_microbatch(ex, loss_target):
        ex = dict(ex)
        diff = ex.pop(TWIN_DIFF_INPUT)
        return grad_fn(weights, diff, {**shared, **ex}, loss_target)

    if N_MICROBATCH == 1:
        loss, (grad_w, grad_x) = one_microbatch(per_example, given["loss_target"])
    else:
        def body(carry, xs):
            loss_sum, grad_sum = carry
            l_k, (gw_k, gx_k) = one_microbatch(xs[0], xs[1])
            with _jax.named_scope("update"):
                return (loss_sum + l_k, _jax.tree.map(_jnp.add, grad_sum, gw_k)), gx_k

        init = (_jnp.zeros((), _jnp.float32), _jax.tree.map(_jnp.zeros_like, weights))
        (loss, grad_w), grad_x = _jax.lax.scan(body, init, (per_example, given["loss_target"]))
    with _jax.named_scope("update"):
        delta_w, new_m, new_v = {}, {}, {}
        for n in TWIN_WEIGHTS:
            delta_w[n], new_m[n], new_v[n] = _adamw(weights[n], grad_w[n], given["m_" + n], given["v_" + n])
    return (loss, grad_x, *[grad_w[n] for n in TWIN_WEIGHTS], *[delta_w[n] for n in TWIN_WEIGHTS],
            *[new_m[n] for n in TWIN_WEIGHTS], *[new_v[n] for n in TWIN_WEIGHTS])
```

```python
import functools
import math

import jax
import jax.numpy as jnp
from jax import lax
from jax.experimental import pallas as pl
from jax.experimental.pallas import tpu as pltpu

F32, BF16 = jnp.float32, jnp.bfloat16
EPS = 1e-6
FFN_RES = 0.5
N_HEADS, HEAD_DIM = 8, 64
D_CONV = 512
D_ATTN = N_HEADS * HEAD_DIM
CONV_A_WIDTH, CONV_C_WIDTH = 31, 3
ADAM_LR, ADAM_B1, ADAM_B2, ADAM_EPS, ADAM_WD, ADAM_STEP = 0.001, 0.9, 0.999, 1e-08, 0.01, 10
MESH = pl.DeviceIdType.MESH
ANY = pl.BlockSpec(memory_space=pl.ANY)

TOK_TILE = 512
ATT_TILE = 512
HALO_A, HALO_C = 32, 16
SCAN_BLK = 256
MIB = 2 ** 20


def _pallas(body, **kw):
    return pl.pallas_call(body, **kw)


def _cp(sem=None, vmem_mib=48):
    return pltpu.CompilerParams(dimension_semantics=sem, vmem_limit_bytes=vmem_mib * MIB)


def _dot(a, b):
    return jnp.dot(a, b, preferred_element_type=F32)


def _dot_nt(a, b):
    return lax.dot_general(a, b, (((1,), (1,)), ((), ())), preferred_element_type=F32)


def _dot_tn(a, b):
    return lax.dot_general(a, b, (((0,), (0,)), ((), ())), preferred_element_type=F32)


def _sds(shape, dtype):
    return jax.ShapeDtypeStruct(shape, dtype)


def _rms(x):
    return lax.rsqrt(jnp.mean(x * x, axis=-1, keepdims=True) + EPS)


def _rms_bwd(dy, x, g):
    r = _rms(x)
    xh = x * r
    dxh = dy * g
    dx = r * (dxh - xh * jnp.mean(dxh * xh, axis=-1, keepdims=True))
    return dx, xh


def _silu_grad(z):
    s = jax.nn.sigmoid(z)
    return s * (1.0 + z * (1.0 - s))


def _ffn_fwd(x, g, wg, wu, wd):
    S, D = x.shape
    nc, _, Fs = wg.shape
    tm = TOK_TILE

    def body(x_ref, g_ref, wg_ref, wu_ref, wd_ref, out_ref, xn_ref, G_ref, U_ref, acc_ref):
        j = pl.program_id(1)

        @pl.when(j == 0)
        def _():
            xv = x_ref[...]
            xn_ref[...] = (xv * _rms(xv) * g_ref[...]).astype(BF16)
            acc_ref[...] = jnp.zeros_like(acc_ref)

        xn = xn_ref[...]
        G = _dot(xn, wg_ref[0])
        U = _dot(xn, wu_ref[0])
        G_ref[0] = G.astype(BF16)
        U_ref[0] = U.astype(BF16)
        H = (G * jax.nn.sigmoid(G) * U).astype(BF16)
        acc_ref[...] += _dot(H, wd_ref[0])

        @pl.when(j == nc - 1)
        def _():
            out_ref[...] = x_ref[...] + FFN_RES * acc_ref[...]

    row = pl.BlockSpec((tm, D), lambda i, j: (i, 0))
    return _pallas(
        body, name="ffn_fwd", grid=(S // tm, nc),
        in_specs=[row, pl.BlockSpec((1, D), lambda i, j: (0, 0)),
                  pl.BlockSpec((1, D, Fs), lambda i, j: (j, 0, 0)), pl.BlockSpec((1, D, Fs), lambda i, j: (j, 0, 0)),
                  pl.BlockSpec((1, Fs, D), lambda i, j: (j, 0, 0))],
        out_specs=[row, row, pl.BlockSpec((1, tm, Fs), lambda i, j: (j, i, 0)),
                   pl.BlockSpec((1, tm, Fs), lambda i, j: (j, i, 0))],
        out_shape=[_sds((S, D), F32), _sds((S, D), BF16), _sds((nc, S, Fs), BF16), _sds((nc, S, Fs), BF16)],
        scratch_shapes=[pltpu.VMEM((tm, D), F32)],
        compiler_params=_cp(("parallel", "arbitrary")),
    )(x, g, wg, wu, wd)


def _ffn_bwd_w(dout, xn, G, U, wd):
    S, D = dout.shape
    nc, _, Fs = G.shape
    tm = TOK_TILE
    nt = S // tm

    def body(do_ref, xn_ref, G_ref, U_ref, wd_ref, dwg_ref, dwu_ref, dwd_ref, dG_ref, dU_ref, ag, au, ad):
        i = pl.program_id(1)

        @pl.when(i == 0)
        def _():
            ag[...] = jnp.zeros_like(ag)
            au[...] = jnp.zeros_like(au)
            ad[...] = jnp.zeros_like(ad)

        do = (FFN_RES * do_ref[...]).astype(BF16)
        Gv = G_ref[0].astype(F32)
        Uv = U_ref[0].astype(F32)
        dH = _dot_nt(do, wd_ref[0])
        sg = jax.nn.sigmoid(Gv)
        act = Gv * sg
        H = (act * Uv).astype(BF16)
        dU = (dH * act).astype(BF16)
        dG = (dH * Uv * (sg * (1.0 + Gv * (1.0 - sg)))).astype(BF16)
        dG_ref[0] = dG
        dU_ref[0] = dU
        xnv = xn_ref[...]
        ag[...] += _dot_tn(xnv, dG)
        au[...] += _dot_tn(xnv, dU)
        ad[...] += _dot_tn(H, do)

        @pl.when(i == nt - 1)
        def _():
            dwg_ref[0] = ag[...].astype(BF16)
            dwu_ref[0] = au[...].astype(BF16)
            dwd_ref[0] = ad[...].astype(BF16)

    row = pl.BlockSpec((tm, D), lambda j, i: (i, 0))
    hid = pl.BlockSpec((1, tm, Fs), lambda j, i: (j, i, 0))
    wcol = pl.BlockSpec((1, D, Fs), lambda j, i: (j, 0, 0))
    wrow = pl.BlockSpec((1, Fs, D), lambda j, i: (j, 0, 0))
    return _pallas(
        body, name="ffn_bwd_w", grid=(nc, nt),
        in_specs=[row, row, hid, hid, wrow],
        out_specs=[wcol, wcol, wrow, hid, hid],
        out_shape=[_sds((nc, D, Fs), BF16), _sds((nc, D, Fs), BF16), _sds((nc, Fs, D), BF16),
                   _sds((nc, S, Fs), BF16), _sds((nc, S, Fs), BF16)],
        scratch_shapes=[pltpu.VMEM((D, Fs), F32), pltpu.VMEM((D, Fs), F32), pltpu.VMEM((Fs, D), F32)],
        compiler_params=_cp(("parallel", "arbitrary"), 56),
    )(dout, xn, G, U, wd)


def _norm_in_bwd(dzs, ws, x, g, dres):
    S, D = x.shape
    nc = dzs[0].shape[0]
    n = len(dzs)
    tm = TOK_TILE

    def body(*refs):
        dz_refs, w_refs = refs[:n], refs[n:2 * n]
        x_ref, g_ref, dres_ref, dx_ref, dg_ref, acc_ref = refs[2 * n:]
        i, j = pl.program_id(0), pl.program_id(1)

        @pl.when(j == 0)
        def _():
            acc_ref[...] = jnp.zeros_like(acc_ref)

        @pl.when((i == 0) & (j == 0))
        def _():
            dg_ref[...] = jnp.zeros_like(dg_ref)

        for dz_ref, w_ref in zip(dz_refs, w_refs):
            acc_ref[...] += _dot_nt(dz_ref[0], w_ref[0])

        @pl.when(j == nc - 1)
        def _():
            dxn = acc_ref[...]
            dx, xh = _rms_bwd(dxn, x_ref[...], g_ref[...])
            dx_ref[...] = dx + dres_ref[...]
            dg_ref[...] += jnp.sum(dxn * xh, axis=0, keepdims=True)

    row = pl.BlockSpec((tm, D), lambda i, j: (i, 0))
    one = pl.BlockSpec((1, D), lambda i, j: (0, 0))
    in_specs = [pl.BlockSpec((1, tm, dz.shape[2]), lambda i, j: (j, i, 0)) for dz in dzs]
    in_specs += [pl.BlockSpec((1, D, w.shape[2]), lambda i, j: (j, 0, 0)) for w in ws]
    return _pallas(
        body, name="norm_in_bwd", grid=(S // tm, nc),
        in_specs=in_specs + [row, one, row], out_specs=[row, one],
        out_shape=[_sds((S, D), F32), _sds((1, D), F32)],
        scratch_shapes=[pltpu.VMEM((tm, D), F32)],
        compiler_params=_cp(("arbitrary", "arbitrary")),
    )(*dzs, *ws, x, g, dres)


def _norm_proj(x, g, w, w2=None):
    S, D = x.shape
    N = w.shape[1]
    tm = TOK_TILE

    def body(*refs):
        if w2 is None:
            x_ref, g_ref, w_ref, h_ref, z_ref = refs
        else:
            x_ref, g_ref, w_ref, w2_ref, h_ref, z_ref, z2_ref = refs
        xv = x_ref[...]
        h = (xv * _rms(xv) * g_ref[...]).astype(BF16)
        h_ref[...] = h
        z_ref[...] = _dot(h, w_ref[...]).astype(BF16)
        if w2 is not None:
            z2_ref[...] = _dot(h, w2_ref[...])

    row = pl.BlockSpec((tm, D), lambda i: (i, 0))
    in_specs = [row, pl.BlockSpec((1, D), lambda i: (0, 0)), pl.BlockSpec((D, N), lambda i: (0, 0))]
    out_specs = [row, pl.BlockSpec((tm, N), lambda i: (i, 0))]
    out_shape = [_sds((S, D), BF16), _sds((S, N), BF16)]
    args = [x, g, w]
    if w2 is not None:
        N2 = w2.shape[1]
        in_specs.append(pl.BlockSpec((D, N2), lambda i: (0, 0)))
        out_specs.append(pl.BlockSpec((tm, N2), lambda i: (i, 0)))
        out_shape.append(_sds((S, N2), F32))
        args.append(w2)
    return _pallas(body, name="norm_proj", grid=(S // tm,), in_specs=in_specs, out_specs=out_specs,
                   out_shape=out_shape, compiler_params=_cp(("parallel",)))(*args)


def _proj_res(acts, ws, res):
    S, D = res.shape
    n = len(acts)
    tm = TOK_TILE

    def body(*refs):
        a_refs, w_refs = refs[:n], refs[n:2 * n]
        res_ref, out_ref = refs[2 * n:]
        acc = res_ref[...]
        for a_ref, w_ref in zip(a_refs, w_refs):
            acc = acc + _dot(a_ref[...], w_ref[...])
        out_ref[...] = acc

    row = pl.BlockSpec((tm, D), lambda i: (i, 0))
    in_specs = [pl.BlockSpec((tm, a.shape[1]), lambda i: (i, 0)) for a in acts]
    in_specs += [pl.BlockSpec(w.shape, lambda i: (0, 0)) for w in ws]
    return _pallas(body, name="proj_res", grid=(S // tm,), in_specs=in_specs + [row], out_specs=row,
                   out_shape=_sds((S, D), F32), compiler_params=_cp(("parallel",)))(*acts, *ws, res)


def _matmul_nt(a, w):
    S, K = a.shape
    M = w.shape[0]
    tm = TOK_TILE

    def body(a_ref, w_ref, o_ref):
        o_ref[...] = _dot_nt(a_ref[...].astype(BF16), w_ref[...])

    return _pallas(body, name="matmul_nt", grid=(S // tm,),
                   in_specs=[pl.BlockSpec((tm, K), lambda i: (i, 0)), pl.BlockSpec((M, K), lambda i: (0, 0))],
                   out_specs=pl.BlockSpec((tm, M), lambda i: (i, 0)), out_shape=_sds((S, M), F32),
                   compiler_params=_cp(("parallel",)))(a, w)


def _matmul_tn(a, b, tn):
    S, M = a.shape
    N = b.shape[1]
    tm = TOK_TILE
    nt = S // tm

    def body(a_ref, b_ref, o_ref, acc_ref):
        i = pl.program_id(1)

        @pl.when(i == 0)
        def _():
            acc_ref[...] = jnp.zeros_like(acc_ref)

        acc_ref[...] += _dot_tn(a_ref[...].astype(BF16), b_ref[...].astype(BF16))

        @pl.when(i == nt - 1)
        def _():
            o_ref[0] = acc_ref[...].astype(BF16)

    return _pallas(body, name="matmul_tn", grid=(N // tn, nt),
                   in_specs=[pl.BlockSpec((tm, M), lambda j, i: (i, 0)), pl.BlockSpec((tm, tn), lambda j, i: (i, j))],
                   out_specs=pl.BlockSpec((1, M, tn), lambda j, i: (j, 0, 0)), out_shape=_sds((N // tn, M, tn), BF16),
                   scratch_shapes=[pltpu.VMEM((M, tn), F32)],
                   compiler_params=_cp(("parallel", "arbitrary")))(a, b)


def _conv_a_fwd(z, cw, cb, cn):
    S = z.shape[0]
    C = D_CONV
    tm = TOK_TILE
    hb = tm // HALO_A

    def body(u_ref, gt_ref, up_ref, gp_ref, cw_ref, cb_ref, cn_ref, a_ref, a1_ref, win):
        i = pl.program_id(0)
        prev = up_ref[...].astype(F32) * jax.nn.sigmoid(gp_ref[...].astype(F32))
        win[pl.ds(0, HALO_A), :] = jnp.where(i == 0, 0.0, prev)
        win[pl.ds(HALO_A, tm), :] = u_ref[...].astype(F32) * jax.nn.sigmoid(gt_ref[...].astype(F32))
        acc = jnp.zeros((tm, C), F32)
        for k in range(CONV_A_WIDTH):
            acc = acc + cw_ref[k:k + 1, :] * win[pl.ds(HALO_A - (CONV_A_WIDTH - 1) + k, tm), :]
        a1 = acc + cb_ref[...]
        a1_ref[...] = a1
        a2 = a1 * _rms(a1) * cn_ref[...]
        a_ref[...] = (a2 * jax.nn.sigmoid(a2)).astype(BF16)

    cur = lambda c: pl.BlockSpec((tm, C), lambda i, c=c: (i, c))
    prv = lambda c: pl.BlockSpec((HALO_A, C), lambda i, c=c: (jnp.maximum(i * hb - 1, 0), c))
    vec = pl.BlockSpec((1, C), lambda i: (0, 0))
    return _pallas(body, name="conv_a_fwd", grid=(S // tm,),
                   in_specs=[cur(0), cur(1), prv(0), prv(1), pl.BlockSpec((32, C), lambda i: (0, 0)), vec, vec],
                   out_specs=[pl.BlockSpec((tm, C), lambda i: (i, 0)), pl.BlockSpec((tm, C), lambda i: (i, 0))],
                   out_shape=[_sds((S, C), BF16), _sds((S, C), F32)],
                   scratch_shapes=[pltpu.VMEM((tm + HALO_A, C), F32)],
                   compiler_params=_cp(("parallel",)))(z, z, z, z, cw, cb, cn)


def _conv_a_bwd(da, a1, z, cw, cn):
    S = z.shape[0]
    C = D_CONV
    tm = TOK_TILE
    hb = tm // HALO_A
    nt = S // tm
    W = CONV_A_WIDTH

    def body(da_ref, a1_ref, dan_ref, a1n_ref, u_ref, gt_ref, up_ref, gp_ref, cw_ref, cn_ref,
             duz_ref, dcw_ref, dcb_ref, dcn_ref, win, dwin):
        i = pl.program_id(0)

        @pl.when(i == 0)
        def _():
            dcw_ref[...] = jnp.zeros_like(dcw_ref)
            dcb_ref[...] = jnp.zeros_like(dcb_ref)
            dcn_ref[...] = jnp.zeros_like(dcn_ref)

        cnv = cn_ref[...]

        def da1_of(dav, a1v):
            a2 = a1v * _rms(a1v) * cnv
            da2 = dav * _silu_grad(a2)
            dx, xh = _rms_bwd(da2, a1v, cnv)
            return dx, da2 * xh

        da1, dcn_t = da1_of(da_ref[...], a1_ref[...])
        da1n, _ = da1_of(dan_ref[...], a1n_ref[...])
        dwin[pl.ds(0, tm), :] = da1
        dwin[pl.ds(tm, HALO_A), :] = jnp.where(i == nt - 1, 0.0, da1n)
        dcb_ref[...] += jnp.sum(da1, axis=0, keepdims=True)
        dcn_ref[...] += jnp.sum(dcn_t, axis=0, keepdims=True)

        u = u_ref[...].astype(F32)
        sg = jax.nn.sigmoid(gt_ref[...].astype(F32))
        prev = up_ref[...].astype(F32) * jax.nn.sigmoid(gp_ref[...].astype(F32))
        win[pl.ds(0, HALO_A), :] = jnp.where(i == 0, 0.0, prev)
        win[pl.ds(HALO_A, tm), :] = u * sg

        da0 = jnp.zeros((tm, C), F32)
        for k in range(W):
            da0 = da0 + cw_ref[k:k + 1, :] * dwin[pl.ds(W - 1 - k, tm), :]
            dcw_ref[k:k + 1, :] += jnp.sum(da1 * win[pl.ds(HALO_A - (W - 1) + k, tm), :], axis=0, keepdims=True)
        duz_ref[:, 0:C] = (da0 * sg).astype(BF16)
        duz_ref[:, C:2 * C] = (da0 * u * sg * (1.0 - sg)).astype(BF16)

    cur = lambda c: pl.BlockSpec((tm, C), lambda i, c=c: (i, c))
    prv = lambda c: pl.BlockSpec((HALO_A, C), lambda i, c=c: (jnp.maximum(i * hb - 1, 0), c))
    nxt = pl.BlockSpec((HALO_A, C), lambda i: (jnp.minimum((i + 1) * hb, S // HALO_A - 1), 0))
    vec = pl.BlockSpec((1, C), lambda i: (0, 0))
    return _pallas(body, name="conv_a_bwd", grid=(nt,),
                   in_specs=[cur(0), cur(0), nxt, nxt, cur(0), cur(1), prv(0), prv(1),
                             pl.BlockSpec((32, C), lambda i: (0, 0)), vec],
                   out_specs=[pl.BlockSpec((tm, 2 * C), lambda i: (i, 0)), pl.BlockSpec((32, C), lambda i: (0, 0)), vec, vec],
                   out_shape=[_sds((S, 2 * C), BF16), _sds((32, C), F32), _sds((1, C), F32), _sds((1, C), F32)],
                   scratch_shapes=[pltpu.VMEM((tm + HALO_A, C), F32), pltpu.VMEM((tm + HALO_A, C), F32)],
                   compiler_params=_cp(("arbitrary",)))(da, a1, da, a1, z, z, z, z, cw, cn)


def _forget_scan(fl, bf):
    S, L = fl.shape
    B = SCAN_BLK

    def body(fl_ref, bf_ref, flb_ref, F_ref):
        tri = (lax.broadcasted_iota(jnp.int32, (B, B), 0) >= lax.broadcasted_iota(jnp.int32, (B, B), 1)).astype(F32)

        def step(c, carry):
            rows = pl.ds(pl.multiple_of(c * B, B), B)
            v = fl_ref[rows, :] + bf_ref[...]
            flb_ref[rows, :] = v
            lf = jnp.minimum(v, 0.0) - jnp.log1p(jnp.exp(-jnp.abs(v)))
            cs = jnp.dot(tri, lf, precision=lax.Precision.HIGHEST, preferred_element_type=F32) + carry
            F_ref[rows, :] = cs
            return cs[B - 1:B, :]

        lax.fori_loop(0, S // B, step, jnp.zeros((1, L), F32))

    return _pallas(body, name="forget_scan", out_shape=[_sds((S, L), F32), _sds((S, L), F32)],
                   compiler_params=_cp())(fl, bf)


def _forget_scan_bwd(dF, flb):
    S, L = dF.shape
    B = SCAN_BLK
    nb = S // B

    def body(dF_ref, flb_ref, dfl_ref, db_ref):
        tri = (lax.broadcasted_iota(jnp.int32, (B, B), 0) <= lax.broadcasted_iota(jnp.int32, (B, B), 1)).astype(F32)

        def step(t, carry):
            carry_cs, db = carry
            rows = pl.ds(pl.multiple_of((nb - 1 - t) * B, B), B)
            cs = jnp.dot(tri, dF_ref[rows, :], precision=lax.Precision.HIGHEST, preferred_element_type=F32) + carry_cs
            dfl = cs * jax.nn.sigmoid(-flb_ref[rows, :])
            dfl_ref[rows, :] = dfl
            return cs[0:1, :], db + jnp.sum(dfl, axis=0, keepdims=True)

        _, db = lax.fori_loop(0, nb, step, (jnp.zeros((1, L), F32), jnp.zeros((1, L), F32)))
        db_ref[...] = db

    return _pallas(body, name="forget_scan_bwd", out_shape=[_sds((S, L), F32), _sds((1, L), F32)],
                   compiler_params=_cp())(dF, flb)


def _qk_norm(q, k, qw, kw):
    H, S, Dh = q.shape
    tq = ATT_TILE
    scale = 1.0 / math.sqrt(Dh)

    def body(q_ref, k_ref, qw_ref, kw_ref, qn_ref, kn_ref):
        qv = q_ref[0].astype(F32)
        kv = k_ref[0].astype(F32)
        qn_ref[0] = (qv * _rms(qv) * qw_ref[...] * scale).astype(BF16)
        kn_ref[0] = (kv * _rms(kv) * kw_ref[...]).astype(BF16)

    blk = pl.BlockSpec((1, tq, Dh), lambda h, i: (h, i, 0))
    vec = pl.BlockSpec((1, Dh), lambda h, i: (0, 0))
    return _pallas(body, name="qk_norm", grid=(H, S // tq), in_specs=[blk, blk, vec, vec], out_specs=[blk, blk],
                   out_shape=[_sds((H, S, Dh), BF16), _sds((H, S, Dh), BF16)],
                   compiler_params=_cp(("parallel", "parallel")))(q, k, qw, kw)


def _qk_norm_bwd(dqs, dkn, q, k, qw, kw):
    H, S, Dh = q.shape
    tq = ATT_TILE
    scale = 1.0 / math.sqrt(Dh)

    def body(dqs_ref, dkn_ref, q_ref, k_ref, qw_ref, kw_ref, dq_ref, dk_ref, dqw_ref, dkw_ref):
        @pl.when((pl.program_id(0) == 0) & (pl.program_id(1) == 0))
        def _():
            dqw_ref[...] = jnp.zeros_like(dqw_ref)
            dkw_ref[...] = jnp.zeros_like(dkw_ref)

        dqn = dqs_ref[0] * scale
        dq, qh = _rms_bwd(dqn, q_ref[0].astype(F32), qw_ref[...])
        dq_ref[0] = dq.astype(BF16)
        dqw_ref[...] += jnp.sum(dqn * qh, axis=0, keepdims=True)
        dkv = dkn_ref[0]
        dk, kh = _rms_bwd(dkv, k_ref[0].astype(F32), kw_ref[...])
        dk_ref[0] = dk.astype(BF16)
        dkw_ref[...] += jnp.sum(dkv * kh, axis=0, keepdims=True)

    blk = pl.BlockSpec((1, tq, Dh), lambda h, i: (h, i, 0))
    vec = pl.BlockSpec((1, Dh), lambda h, i: (0, 0))
    return _pallas(body, name="qk_norm_bwd", grid=(H, S // tq), in_specs=[blk, blk, blk, blk, vec, vec],
                   out_specs=[blk, blk, vec, vec],
                   out_shape=[_sds((H, S, Dh), BF16), _sds((H, S, Dh), BF16), _sds((1, Dh), F32), _sds((1, Dh), F32)],
                   compiler_params=_cp(("arbitrary", "arbitrary")))(dqs, dkn, q, k, qw, kw)


NEG = -1e30


def _causal_mask(t):
    return lax.broadcasted_iota(jnp.int32, (t, t), 0) >= lax.broadcasted_iota(jnp.int32, (t, t), 1)


def _fox_fwd(qs, kn, v, fcol, frow):
    H, S, Dh = qs.shape
    t = ATT_TILE
    nq = S // t

    def body(q_ref, k_ref, v_ref, fc_ref, fr_ref, o_ref, lse_ref):
        i = pl.program_id(1)
        q = q_ref[0]
        fq = fc_ref[0]

        def tile(j, carry, diag):
            m, l, acc = carry
            rows = pl.ds(pl.multiple_of(j * t, t), t)
            s = _dot_nt(q, k_ref[0, rows, :]) + fq - fr_ref[0, j]
            if diag:
                s = jnp.where(_causal_mask(t), s, NEG)
            m_new = jnp.maximum(m, jnp.max(s, axis=-1, keepdims=True))
            p = jnp.exp(s - m_new)
            alpha = jnp.exp(m - m_new)
            l = alpha * l + jnp.sum(p, axis=-1, keepdims=True)
            acc = alpha * acc + _dot(p.astype(BF16), v_ref[0, rows, :])
            return m_new, l, acc

        init = (jnp.full((t, 1), NEG, F32), jnp.zeros((t, 1), F32), jnp.zeros((t, Dh), F32))
        carry = lax.fori_loop(0, i, lambda j, c: tile(j, c, False), init)
        m, l, acc = tile(i, carry, True)
        o_ref[0] = (acc / l).astype(BF16)
        lse_ref[0] = m + jnp.log(l)

    qblk = pl.BlockSpec((1, t, Dh), lambda h, i: (h, i, 0))
    full = pl.BlockSpec((1, S, Dh), lambda h, i: (h, 0, 0))
    col = pl.BlockSpec((1, t, 1), lambda h, i: (h, i, 0))
    return _pallas(body, name="fox_fwd", grid=(H, nq),
                   in_specs=[qblk, full, full, col, pl.BlockSpec((1, nq, 1, t), lambda h, i: (h, 0, 0, 0))],
                   out_specs=[qblk, col], out_shape=[_sds((H, S, Dh), BF16), _sds((H, S, 1), F32)],
                   compiler_params=_cp(("parallel", "parallel")))(qs, kn, v, fcol, frow)


def _fox_bwd(qs, kn, v, o, do, lse, fcol, frow):
    H, S, Dh = qs.shape
    t = ATT_TILE
    nq = S // t

    def body(q_ref, k_ref, v_ref, o_ref, do_ref, lse_ref, fc_ref, fr_ref, dq_ref, dk_ref, dv_ref, dfq_ref, dfk_ref):
        j = pl.program_id(1)

        @pl.when(j == 0)
        def _():
            dq_ref[...] = jnp.zeros_like(dq_ref)
            dfq_ref[...] = jnp.zeros_like(dfq_ref)

        k = k_ref[0]
        vv = v_ref[0]
        fk = fr_ref[0, 0]

        def tile(i, carry, diag):
            dk, dv, dfk = carry
            rows = pl.ds(pl.multiple_of(i * t, t), t)
            q = q_ref[0, rows, :]
            dov = do_ref[0, rows, :]
            delta = jnp.sum(dov.astype(F32) * o_ref[0, rows, :].astype(F32), axis=-1, keepdims=True)
            s = _dot_nt(q, k) + fc_ref[0, rows, :] - fk
            if diag:
                s = jnp.where(_causal_mask(t), s, NEG)
            p = jnp.exp(s - lse_ref[0, rows, :])
            dv = dv + _dot_tn(p.astype(BF16), dov)
            ds = p * (_dot_nt(dov, vv) - delta)
            dsb = ds.astype(BF16)
            dq_ref[0, rows, :] += _dot(dsb, k)
            dk = dk + _dot_tn(dsb, q)
            dfq_ref[0, rows, :] += jnp.sum(ds, axis=-1, keepdims=True)
            dfk = dfk + jnp.sum(ds, axis=0, keepdims=True)
            return dk, dv, dfk

        init = (jnp.zeros((t, Dh), F32), jnp.zeros((t, Dh), F32), jnp.zeros((1, t), F32))
        carry = tile(j, init, True)
        dk, dv, dfk = lax.fori_loop(j + 1, nq, lambda i, c: tile(i, c, False), carry)
        dk_ref[0] = dk
        dv_ref[0] = dv
        dfk_ref[0, 0] = dfk

    full = pl.BlockSpec((1, S, Dh), lambda h, j: (h, 0, 0))
    kblk = pl.BlockSpec((1, t, Dh), lambda h, j: (h, j, 0))
    colf = pl.BlockSpec((1, S, 1), lambda h, j: (h, 0, 0))
    rowb = pl.BlockSpec((1, 1, 1, t), lambda h, j: (h, j, 0, 0))
    return _pallas(body, name="fox_bwd", grid=(H, nq),
                   in_specs=[full, kblk, kblk, full, full, colf, colf, rowb],
                   out_specs=[full, kblk, kblk, colf, rowb],
                   out_shape=[_sds((H, S, Dh), F32), _sds((H, S, Dh), F32), _sds((H, S, Dh), F32),
                              _sds((H, S, 1), F32), _sds((H, nq, 1, t), F32)],
                   compiler_params=_cp(("parallel", "arbitrary"), 56))(qs, kn, v, o, do, lse, fcol, frow)


def _odd_mid_fwd(z, cw):
    S = z.shape[0]
    D = z.shape[1] // 3
    tm = TOK_TILE
    hb = tm // HALO_C
    W = CONV_C_WIDTH

    def body(gb_ref, gc_ref, hh_ref, gcp_ref, hhp_ref, cw_ref, y_ref, win):
        i = pl.program_id(0)
        prev = gcp_ref[...].astype(F32) * hhp_ref[...].astype(F32)
        win[pl.ds(0, HALO_C), :] = jnp.where(i == 0, 0.0, prev)
        win[pl.ds(HALO_C, tm), :] = gc_ref[...].astype(F32) * hh_ref[...].astype(F32)
        c1 = jnp.zeros((tm, D), F32)
        for k in range(W):
            c1 = c1 + cw_ref[k:k + 1, :] * win[pl.ds(HALO_C - (W - 1) + k, tm), :]
        y_ref[...] = (gb_ref[...].astype(F32) * c1).astype(BF16)

    cur = lambda c: pl.BlockSpec((tm, D), lambda i, c=c: (i, c))
    prv = lambda c: pl.BlockSpec((HALO_C, D), lambda i, c=c: (jnp.maximum(i * hb - 1, 0), c))
    return _pallas(body, name="odd_mid_fwd", grid=(S // tm,),
                   in_specs=[cur(0), cur(1), cur(2), prv(1), prv(2), pl.BlockSpec((8, D), lambda i: (0, 0))],
                   out_specs=pl.BlockSpec((tm, D), lambda i: (i, 0)), out_shape=_sds((S, D), BF16),
                   scratch_shapes=[pltpu.VMEM((tm + HALO_C, D), F32)],
                   compiler_params=_cp(("parallel",)))(z, z, z, z, z, cw)


def _odd_mid_bwd(dy, z, cw):
    S = z.shape[0]
    D = z.shape[1] // 3
    tm = TOK_TILE
    hb = tm // HALO_C
    nt = S // tm
    W = CONV_C_WIDTH

    def body(dy_ref, dyn_ref, gb_ref, gbn_ref, gc_ref, hh_ref, gcp_ref, hhp_ref, cw_ref, dz_ref, dcw_ref, win, dwin):
        i = pl.program_id(0)

        @pl.when(i == 0)
        def _():
            dcw_ref[...] = jnp.zeros_like(dcw_ref)

        gc = gc_ref[...].astype(F32)
        hh = hh_ref[...].astype(F32)
        prev = gcp_ref[...].astype(F32) * hhp_ref[...].astype(F32)
        win[pl.ds(0, HALO_C), :] = jnp.where(i == 0, 0.0, prev)
        win[pl.ds(HALO_C, tm), :] = gc * hh
        dyv = dy_ref[...]
        dc1 = dyv * gb_ref[...].astype(F32)
        dwin[pl.ds(0, tm), :] = dc1
        dwin[pl.ds(tm, HALO_C), :] = jnp.where(i == nt - 1, 0.0, dyn_ref[...] * gbn_ref[...].astype(F32))
        c1 = jnp.zeros((tm, D), F32)
        dc0 = jnp.zeros((tm, D), F32)
        for k in range(W):
            tap = win[pl.ds(HALO_C - (W - 1) + k, tm), :]
            c1 = c1 + cw_ref[k:k + 1, :] * tap
            dc0 = dc0 + cw_ref[k:k + 1, :] * dwin[pl.ds(W - 1 - k, tm), :]
            dcw_ref[k:k + 1, :] += jnp.sum(dc1 * tap, axis=0, keepdims=True)
        dz_ref[:, 0:D] = (dyv * c1).astype(BF16)
        dz_ref[:, D:2 * D] = (dc0 * hh).astype(BF16)
        dz_ref[:, 2 * D:3 * D] = (dc0 * gc).astype(BF16)

    cur = lambda c: pl.BlockSpec((tm, D), lambda i, c=c: (i, c))
    prv = lambda c: pl.BlockSpec((HALO_C, D), lambda i, c=c: (jnp.maximum(i * hb - 1, 0), c))
    nxt = pl.BlockSpec((HALO_C, D), lambda i: (jnp.minimum((i + 1) * hb, S // HALO_C - 1), 0))
    return _pallas(body, name="odd_mid_bwd", grid=(nt,),
                   in_specs=[cur(0), nxt, cur(0), nxt, cur(1), cur(2), prv(1), prv(2), pl.BlockSpec((8, D), lambda i: (0, 0))],
                   out_specs=[pl.BlockSpec((tm, 3 * D), lambda i: (i, 0)), pl.BlockSpec((8, D), lambda i: (0, 0))],
                   out_shape=[_sds((S, 3 * D), BF16), _sds((8, D), F32)],
                   scratch_shapes=[pltpu.VMEM((tm + HALO_C, D), F32), pltpu.VMEM((tm + HALO_C, D), F32)],
                   compiler_params=_cp(("arbitrary",)))(dy, dy, z, z, z, z, z, z, cw)


def _loss_head(y, tgt):
    S, D = y.shape
    tm = TOK_TILE

    def body(y_ref, t_ref, dy_ref, l_ref):
        @pl.when(pl.program_id(0) == 0)
        def _():
            l_ref[...] = jnp.zeros_like(l_ref)

        e = y_ref[...] - t_ref[...]
        dy_ref[...] = e * (1.0 / D)
        l_ref[...] += jnp.sum(jnp.sum(e * e, axis=-1, keepdims=True), axis=0, keepdims=True) * (0.5 / D)

    row = pl.BlockSpec((tm, D), lambda i: (i, 0))
    return _pallas(body, name="loss_head", grid=(S // tm,), in_specs=[row, row],
                   out_specs=[row, pl.BlockSpec((1, 1), lambda i: (0, 0))],
                   out_shape=[_sds((S, D), F32), _sds((1, 1), F32)],
                   compiler_params=_cp(("arbitrary",)))(y, tgt)


def _heads(a):
    S = a.shape[0]
    return a.reshape(S, N_HEADS, HEAD_DIM).transpose(1, 0, 2)


def _unheads(a):
    return a.transpose(1, 0, 2).reshape(a.shape[1], D_ATTN)


def _pad_rows(a, rows):
    return jnp.pad(a, ((0, rows - a.shape[0]), (0, 0)))


def _local_step(x, tgt, W):
    S, D = x.shape
    nq = S // ATT_TILE
    grads = {}
    saved = {}

    def ffn_f(tag, l, xin):
        out, xn, G, U = _ffn_fwd(xin, W[tag + "_norm"][l:l + 1], W[tag + "_w_gate"][l], W[tag + "_w_up"][l],
                                 W[tag + "_w_down"][l])
        saved[(tag, l)] = (xin, xn, G, U)
        return out

    def ffn_b(tag, l, dout):
        xin, xn, G, U = saved[(tag, l)]
        dwg, dwu, dwd, dG, dU = _ffn_bwd_w(dout, xn, G, U, W[tag + "_w_down"][l])
        dx, dg = _norm_in_bwd([dG, dU], [W[tag + "_w_gate"][l], W[tag + "_w_up"][l]], xin,
                              W[tag + "_norm"][l:l + 1], dout)
        grads[(tag + "_w_gate", l)] = dwg
        grads[(tag + "_w_up", l)] = dwu
        grads[(tag + "_w_down", l)] = dwd
        grads[(tag + "_norm", l)] = dg
        return dx

    x0a = ffn_f("ffn1", 0, x)
    w_in = W["ev_w_in"]
    w_main, w_f = w_in[:, :2560], jnp.pad(w_in[:, 2560:], ((0, 0), (0, 120)))
    h0, z0, fl = _norm_proj(x0a, W["mix_norm"][0:1], w_main, w_f)
    cw_a = _pad_rows(W["ev_conv_w"], 32)
    a_act, a1 = _conv_a_fwd(z0, cw_a, W["ev_conv_b"], W["ev_conv_norm"])
    flb, Fc = _forget_scan(fl, jnp.pad(W["ev_b_f"], ((0, 0), (0, 120))))
    Ft = Fc[:, :N_HEADS].T
    fcol = Ft.reshape(N_HEADS, S, 1)
    frow = Ft.reshape(N_HEADS, nq, 1, ATT_TILE)
    q_raw, k_raw, v_h = _heads(z0[:, 1024:1536]), _heads(z0[:, 1536:2048]), _heads(z0[:, 2048:2560])
    qs, kn = _qk_norm(q_raw, k_raw, W["ev_q_norm"], W["ev_k_norm"])
    o_h, lse = _fox_fwd(qs, kn, v_h, fcol, frow)
    o_flat = _unheads(o_h)
    w_out_e = W["ev_w_out"]
    x0b = _proj_res([a_act, o_flat], [w_out_e[:D_CONV], w_out_e[D_CONV:]], x0a)
    x0c = ffn_f("ffn2", 0, x0b)
    x1a = ffn_f("ffn1", 1, x0c)
    h1, z1 = _norm_proj(x1a, W["mix_norm"][1:2], W["od_w_in"])
    cw_c = _pad_rows(W["od_conv_w"], 8)
    y1 = _odd_mid_fwd(z1, cw_c)
    x1b = _proj_res([y1], [W["od_w_out"]], x1a)
    x1c = ffn_f("ffn2", 1, x1b)
    dy, loss = _loss_head(x1c, tgt)

    d = ffn_b("ffn2", 1, dy)
    dy1 = _matmul_nt(d, W["od_w_out"])
    grads[("od_w_out", 0)] = _matmul_tn(y1, d, D)[0]
    dz1, dcw_c = _odd_mid_bwd(dy1, z1, cw_c)
    grads[("od_conv_w", 0)] = dcw_c[:CONV_C_WIDTH]
    grads[("od_w_in", 0)] = _matmul_tn(h1, dz1, 3 * D // 4)
    d, dg = _norm_in_bwd([dz1[None]], [W["od_w_in"][None]], x1a, W["mix_norm"][1:2], d)
    grads[("mix_norm", 1)] = dg
    d = ffn_b("ffn1", 1, d)
    d = ffn_b("ffn2", 0, d)
    dcat = _matmul_nt(d, w_out_e)
    grads[("ev_w_out", 0)] = jnp.concatenate([_matmul_tn(a_act, d, D)[0], _matmul_tn(o_flat, d, D)[0]], axis=0)
    duz, dcw_a, dcb, dcn = _conv_a_bwd(dcat, a1, z0, cw_a, W["ev_conv_norm"])
    grads[("ev_conv_w", 0)] = dcw_a[:CONV_A_WIDTH]
    grads[("ev_conv_b", 0)] = dcb
    grads[("ev_conv_norm", 0)] = dcn
    do_h = _heads(dcat[:, D_CONV:].astype(BF16))
    dqs, dkn, dv, dfq, dfk = _fox_bwd(qs, kn, v_h, o_h, do_h, lse, fcol, frow)
    dq_raw, dk_raw, dqw, dkw = _qk_norm_bwd(dqs, dkn, q_raw, k_raw, W["ev_q_norm"], W["ev_k_norm"])
    grads[("ev_q_norm", 0)] = dqw
    grads[("ev_k_norm", 0)] = dkw
    dF = (dfq.reshape(N_HEADS, S) - dfk.reshape(N_HEADS, S)).T
    dfl, dbf = _forget_scan_bwd(jnp.pad(dF, ((0, 0), (0, 120))), flb)
    grads[("ev_b_f", 0)] = dbf[:, :N_HEADS]
    dz0 = jnp.concatenate([duz, _unheads(dq_raw), _unheads(dk_raw), _unheads(dv.astype(BF16))], axis=1)
    dflb = dfl.astype(BF16)
    gmain = _matmul_tn(h0, dz0, 640)
    gmain = gmain.transpose(1, 0, 2).reshape(D, 2560)
    gf = _matmul_tn(h0, dflb, 128)[0][:, :N_HEADS]
    grads[("ev_w_in", 0)] = jnp.concatenate([gmain, gf], axis=1)
    d, dg = _norm_in_bwd([dz0[None], dflb[None]], [w_main[None], w_f[None]], x0a, W["mix_norm"][0:1], d)
    grads[("mix_norm", 0)] = dg
    d = ffn_b("ffn1", 0, d)
    return loss, d, grads


def _place():
    x, y, c = lax.axis_index("x"), lax.axis_index("y"), lax.axis_index("c")
    chips = [(1 - x, y), (x, 1 - y), (1 - x, 1 - y)]
    return x, y, c, chips


def _remote(src, dst, send_sem, recv_sem, to):
    return pltpu.make_async_remote_copy(src_ref=src, dst_ref=dst, send_sem=send_sem, recv_sem=recv_sem,
                                        device_id=to, device_id_type=MESH)


def _all_gather(shards, small):
    n = len(shards)

    def body(*refs):
        ins, sm_in = refs[:n], refs[n]
        outs, sm_out = refs[n + 1:2 * n + 1], refs[2 * n + 1]
        send_sems, recv_sems, loc_sems, sm_send, sm_recv = refs[2 * n + 2:]
        x, y, c, chips = _place()
        me = 2 * x + y
        sib = (x, y, 1 - c)
        local = [pltpu.make_async_copy(ins[a], outs[a].at[me], loc_sems.at[a]) for a in range(n)]
        local.append(pltpu.make_async_copy(sm_in, sm_out.at[me], loc_sems.at[n]))
        for cp in local:
            cp.start()
        sends = []
        for jj, (px, py) in enumerate(chips):
            sends.append(_remote(sm_in, sm_out.at[me], sm_send.at[jj], sm_recv.at[jj], (px, py, c)))
        for a in range(n):
            h = ins[a].shape[0] // 2
            mine = pl.ds(c * h, h)
            for jj, (px, py) in enumerate(chips):
                k = 6 * a + jj
                sends.append(_remote(ins[a].at[mine], outs[a].at[me, mine], send_sems.at[k], recv_sems.at[k], (px, py, c)))
        for cp in sends:
            cp.start()
        passed = []
        for a in range(n):
            h = ins[a].shape[0] // 2
            mine = pl.ds(c * h, h)
            for jj, (px, py) in enumerate(chips):
                blk = outs[a].at[2 * px + py, mine]
                _remote(blk, blk, send_sems.at[6 * a + jj], recv_sems.at[6 * a + jj], (px, py, c)).wait_recv()
                k = 6 * a + 3 + jj
                fwd = _remote(blk, blk, send_sems.at[k], recv_sems.at[k], sib)
                fwd.start()
                passed.append(fwd)
        for a in range(n):
            h = ins[a].shape[0] // 2
            theirs = pl.ds((1 - c) * h, h)
            for jj, (px, py) in enumerate(chips):
                blk = outs[a].at[2 * px + py, theirs]
                k = 6 * a + 3 + jj
                _remote(blk, blk, send_sems.at[k], recv_sems.at[k], sib).wait_recv()
        for jj, (px, py) in enumerate(chips):
            blk = sm_out.at[2 * px + py]
            _remote(blk, blk, sm_send.at[jj], sm_recv.at[jj], (px, py, c)).wait_recv()
        for cp in sends + passed:
            cp.wait_send()
        for cp in local:
            cp.wait()

    out_shape = [_sds((4,) + s.shape, s.dtype) for s in shards] + [_sds((4,) + small.shape, small.dtype)]
    return _pallas(
        body, name="all_gather_weights", out_shape=out_shape,
        in_specs=[ANY] * (n + 1), out_specs=[ANY] * (n + 1),
        scratch_shapes=[pltpu.SemaphoreType.DMA((6 * n,)), pltpu.SemaphoreType.DMA((6 * n,)),
                        pltpu.SemaphoreType.DMA((n + 1,)), pltpu.SemaphoreType.DMA((3,)), pltpu.SemaphoreType.DMA((3,))],
    )(*shards, small)


def _pair_exchange(gs):
    n = len(gs)

    def body(*refs):
        ins, outs = refs[:n], refs[n:2 * n]
        send_sems, recv_sems = refs[2 * n:]
        x, y, c, _ = _place()
        cps = []
        for a in range(n):
            h = ins[a].shape[1] // 2
            cps.append(_remote(ins[a].at[:, pl.ds((1 - c) * h, h)], outs[a], send_sems.at[a], recv_sems.at[a], (x, y, 1 - c)))
        for cp in cps:
            cp.start()
        for cp in cps:
            cp.wait()

    return _pallas(body, name="grad_pair_exchange", in_specs=[ANY] * n, out_specs=[ANY] * n,
                   out_shape=[_sds((4, g.shape[1] // 2, g.shape[2]), g.dtype) for g in gs],
                   scratch_shapes=[pltpu.SemaphoreType.DMA((n,)), pltpu.SemaphoreType.DMA((n,))])(*gs)


def _pair_add(g, other, c_arr):
    _, R, C = g.shape
    h = R // 2

    def body(c_ref, g_ref, o_ref, out_ref):
        out_ref[...] = (g_ref[...].astype(F32) + o_ref[...].astype(F32)).astype(BF16)

    grid_spec = pltpu.PrefetchScalarGridSpec(
        num_scalar_prefetch=1, grid=(4,),
        in_specs=[pl.BlockSpec((1, h, C), lambda k, c_ref: (k, c_ref[0], 0)), pl.BlockSpec((1, h, C), lambda k, c_ref: (k, 0, 0))],
        out_specs=pl.BlockSpec((1, h, C), lambda k, c_ref: (k, 0, 0)))
    return _pallas(body, name="grad_pair_add", grid_spec=grid_spec, out_shape=_sds((4, h, C), BF16),
                   compiler_params=_cp(("parallel",)))(c_arr, g, other)


def _chip_exchange(ss):
    n = len(ss)

    def body(*refs):
        ins, outs = refs[:n], refs[n:2 * n]
        send_sems, recv_sems = refs[2 * n:]
        x, y, c, chips = _place()
        cps = []
        for a in range(n):
            for jj, (px, py) in enumerate(chips):
                k = 3 * a + jj
                cps.append(_remote(ins[a].at[2 * px + py], outs[a].at[jj], send_sems.at[k], recv_sems.at[k], (px, py, c)))
        for cp in cps:
            cp.start()
        for cp in cps:
            cp.wait()

    return _pallas(body, name="grad_chip_exchange", in_specs=[ANY] * n, out_specs=[ANY] * n,
                   out_shape=[_sds((3,) + s.shape[1:], s.dtype) for s in ss],
                   scratch_shapes=[pltpu.SemaphoreType.DMA((3 * n,)), pltpu.SemaphoreType.DMA((3 * n,))])(*ss)


def _chip_sum(s, r, chip_arr):
    _, h, C = s.shape
    tr = h // 2

    def body(k_ref, s_ref, r_ref, out_ref):
        acc = s_ref[0].astype(F32)
        for jj in range(3):
            acc = acc + r_ref[jj].astype(F32)
        out_ref[...] = acc

    grid_spec = pltpu.PrefetchScalarGridSpec(
        num_scalar_prefetch=1, grid=(2,),
        in_specs=[pl.BlockSpec((1, tr, C), lambda i, k_ref: (k_ref[0], i, 0)), pl.BlockSpec((3, tr, C), lambda i, k_ref: (0, i, 0))],
        out_specs=pl.BlockSpec((tr, C), lambda i, k_ref: (i, 0)))
    return _pallas(body, name="grad_chip_sum", grid_spec=grid_spec, out_shape=_sds((h, C), F32),
                   compiler_params=_cp(("parallel",)))(chip_arr, s, r)


def _pair_share(reds, layout):
    n = len(reds)
    n_out = 1 + max(o for o, _ in layout)
    shapes = [None] * n_out
    for (o, l), r in zip(layout, reds):
        L = 1 + max(l2 for o2, l2 in layout if o2 == o)
        shapes[o] = _sds((L, 2 * r.shape[0], r.shape[1]), r.dtype)

    def body(*refs):
        ins, outs = refs[:n], refs[n:n + n_out]
        send_sems, recv_sems, loc_sems = refs[n + n_out:]
        x, y, c, _ = _place()
        cps, local = [], []
        for a, (o, l) in enumerate(layout):
            h = ins[a].shape[0]
            dst = outs[o].at[l, pl.ds(c * h, h)]
            local.append(pltpu.make_async_copy(ins[a], dst, loc_sems.at[a]))
            cps.append(_remote(ins[a], dst, send_sems.at[a], recv_sems.at[a], (x, y, 1 - c)))
        for cp in local + cps:
            cp.start()
        for cp in cps + local:
            cp.wait()

    return _pallas(body, name="grad_pair_share", in_specs=[ANY] * n, out_specs=[ANY] * n_out, out_shape=shapes,
                   scratch_shapes=[pltpu.SemaphoreType.DMA((n,)), pltpu.SemaphoreType.DMA((n,)),
                                   pltpu.SemaphoreType.DMA((n,))])(*reds)


def _small_all_reduce(packed):
    P, L = packed.shape

    def body(in_ref, out_ref, slots, send_sems, recv_sems):
        x, y, c, _ = _place()
        me = 4 * x + 2 * y + c
        slots[me] = in_ref[...]
        cps = []
        for r in range(1, 8):
            px = 1 - x if r & 4 else x
            py = 1 - y if r & 2 else y
            pc = 1 - c if r & 1 else c
            cps.append(_remote(in_ref, slots.at[me], send_sems.at[r - 1], recv_sems.at[r - 1], (px, py, pc)))
        for cp in cps:
            cp.start()
        for r in range(1, 8):
            px = 1 - x if r & 4 else x
            py = 1 - y if r & 2 else y
            pc = 1 - c if r & 1 else c
            blk = slots.at[4 * px + 2 * py + pc]
            _remote(blk, blk, send_sems.at[r - 1], recv_sems.at[r - 1], (px, py, pc)).wait_recv()
        for cp in cps:
            cp.wait_send()
        acc = slots[0]
        for k in range(1, 8):
            acc = acc + slots[k]
        out_ref[...] = acc

    vm = pl.BlockSpec(memory_space=pltpu.VMEM)
    return _pallas(body, name="small_all_reduce", in_specs=[vm], out_specs=vm, out_shape=_sds((P, L), F32),
                   scratch_shapes=[pltpu.VMEM((8, P, L), F32), pltpu.SemaphoreType.DMA((7,)), pltpu.SemaphoreType.DMA((7,))])(packed)


def _adamw_math(w, g, m, v):
    m = ADAM_B1 * m + (1.0 - ADAM_B1) * g
    v = ADAM_B2 * v + (1.0 - ADAM_B2) * (g * g)
    m_hat = m / (1.0 - ADAM_B1 ** ADAM_STEP)
    v_hat = v / (1.0 - ADAM_B2 ** ADAM_STEP)
    delta = -ADAM_LR * (m_hat / (jnp.sqrt(v_hat) + ADAM_EPS) + ADAM_WD * w)
    return delta, m, v


def _adamw(w, g, m, v):
    shape = w.shape
    C = shape[-1]
    rows = math.prod(shape[:-1])
    tr = next(t for t in (512, 352, 256, 128, 64, 32, 16, 8, rows) if rows % t == 0)
    w2, g2, m2, v2 = (a.reshape(rows, C) for a in (w, g, m, v))

    def body(w_ref, g_ref, m_ref, v_ref, d_ref, nm_ref, nv_ref):
        d, nm, nv = _adamw_math(w_ref[...], g_ref[...], m_ref[...], v_ref[...])
        d_ref[...] = d
        nm_ref[...] = nm
        nv_ref[...] = nv

    blk = pl.BlockSpec((tr, C), lambda i: (i, 0))
    outs = _pallas(body, name="adamw", grid=(rows // tr,), in_specs=[blk] * 4, out_specs=[blk] * 3,
                   out_shape=[_sds((rows, C), F32)] * 3, compiler_params=_cp(("parallel",)))(w2, g2, m2, v2)
    return tuple(o.reshape(shape) for o in outs)


WEIGHTS = ["ffn1_norm", "ffn1_w_gate", "ffn1_w_up", "ffn1_w_down", "mix_norm", "ffn2_norm", "ffn2_w_gate", "ffn2_w_up",
           "ffn2_w_down", "ev_w_in", "ev_b_f", "ev_conv_w", "ev_conv_b", "ev_conv_norm", "ev_q_norm", "ev_k_norm",
           "ev_w_out", "od_w_in", "od_conv_w", "od_w_out"]
BIG = ([("ffn1_w_gate", 0), ("ffn1_w_up", 0), ("ffn1_w_down", 0), ("ev_w_in", 0), ("ev_w_out", 0),
        ("ffn2_w_gate", 0), ("ffn2_w_up", 0), ("ffn2_w_down", 0)]
       + [("ffn1_w_gate", 1), ("ffn1_w_up", 1), ("ffn1_w_down", 1), ("od_w_in", 0), ("od_w_out", 0),
          ("ffn2_w_gate", 1), ("ffn2_w_up", 1), ("ffn2_w_down", 1)])
BIG_NAMES = ["ffn1_w_gate", "ffn1_w_up", "ffn1_w_down", "ffn2_w_gate", "ffn2_w_up", "ffn2_w_down",
             "ev_w_in", "ev_w_out", "od_w_in", "od_w_out"]
SMALL = [("ffn1_norm", 16), ("mix_norm", 16), ("ffn2_norm", 16), ("ev_b_f", 8), ("ev_conv_w", 128), ("ev_conv_b", 8),
         ("ev_conv_norm", 8), ("ev_q_norm", 8), ("ev_k_norm", 8), ("od_conv_w", 24)]


def _to_lanes(a, rows):
    flat = a.reshape(-1)
    return jnp.pad(flat, (0, rows * 128 - flat.shape[0])).reshape(rows, 128)


def kernel(x, ffn1_norm, ffn1_w_gate, ffn1_w_up, ffn1_w_down, mix_norm, ffn2_norm, ffn2_w_gate, ffn2_w_up, ffn2_w_down, ev_w_in, ev_b_f, ev_conv_w, ev_conv_b, ev_conv_norm, ev_q_norm, ev_k_norm, ev_w_out, od_w_in, od_conv_w, od_w_out, loss_target, m_ffn1_norm, m_ffn1_w_gate, m_ffn1_w_up, m_ffn1_w_down, m_mix_norm, m_ffn2_norm, m_ffn2_w_gate, m_ffn2_w_up, m_ffn2_w_down, m_ev_w_in, m_ev_b_f, m_ev_conv_w, m_ev_conv_b, m_ev_conv_norm, m_ev_q_norm, m_ev_k_norm, m_ev_w_out, m_od_w_in, m_od_conv_w, m_od_w_out, v_ffn1_norm, v_ffn1_w_gate, v_ffn1_w_up, v_ffn1_w_down, v_mix_norm, v_ffn2_norm, v_ffn2_w_gate, v_ffn2_w_up, v_ffn2_w_down, v_ev_w_in, v_ev_b_f, v_ev_conv_w, v_ev_conv_b, v_ev_conv_norm, v_ev_q_norm, v_ev_k_norm, v_ev_w_out, v_od_w_in, v_od_conv_w, v_od_w_out):
    P = dict(ffn1_norm=ffn1_norm, ffn1_w_gate=ffn1_w_gate, ffn1_w_up=ffn1_w_up, ffn1_w_down=ffn1_w_down, mix_norm=mix_norm,
             ffn2_norm=ffn2_norm, ffn2_w_gate=ffn2_w_gate, ffn2_w_up=ffn2_w_up, ffn2_w_down=ffn2_w_down, ev_w_in=ev_w_in,
             ev_b_f=ev_b_f, ev_conv_w=ev_conv_w, ev_conv_b=ev_conv_b, ev_conv_norm=ev_conv_norm, ev_q_norm=ev_q_norm,
             ev_k_norm=ev_k_norm, ev_w_out=ev_w_out, od_w_in=od_w_in, od_conv_w=od_conv_w, od_w_out=od_w_out)
    M = dict(zip(WEIGHTS, [m_ffn1_norm, m_ffn1_w_gate, m_ffn1_w_up, m_ffn1_w_down, m_mix_norm, m_ffn2_norm, m_ffn2_w_gate,
                           m_ffn2_w_up, m_ffn2_w_down, m_ev_w_in, m_ev_b_f, m_ev_conv_w, m_ev_conv_b, m_ev_conv_norm,
                           m_ev_q_norm, m_ev_k_norm, m_ev_w_out, m_od_w_in, m_od_conv_w, m_od_w_out]))
    V = dict(zip(WEIGHTS, [v_ffn1_norm, v_ffn1_w_gate, v_ffn1_w_up, v_ffn1_w_down, v_mix_norm, v_ffn2_norm, v_ffn2_w_gate,
                           v_ffn2_w_up, v_ffn2_w_down, v_ev_w_in, v_ev_b_f, v_ev_conv_w, v_ev_conv_b, v_ev_conv_norm,
                           v_ev_q_norm, v_ev_k_norm, v_ev_w_out, v_od_w_in, v_od_conv_w, v_od_w_out]))
    S, D = x.shape[1], x.shape[2]
    chip = 2 * lax.axis_index("x") + lax.axis_index("y")
    core = lax.axis_index("c")

    shards = [P[name][l].astype(BF16) for name, l in BIG]
    taps = jnp.concatenate([_to_lanes(_pad_rows(ev_conv_w[0], 32), 32), _to_lanes(_pad_rows(od_conv_w[0], 8), 16)], axis=0)
    *full, taps_all = _all_gather(shards, taps)
    G = dict(zip(BIG, full))
    cols = lambda a: a.transpose(1, 0, 2).reshape(a.shape[1], 4 * a.shape[2])
    W = {k: P[k] for k in ("ffn1_norm", "mix_norm", "ffn2_norm", "ev_b_f", "ev_q_norm", "ev_k_norm")}
    W["ev_conv_b"], W["ev_conv_norm"] = ev_conv_b, ev_conv_norm
    for tag in ("ffn1", "ffn2"):
        for kind in ("_w_gate", "_w_up", "_w_down"):
            W[tag + kind] = [G[(tag + kind, 0)], G[(tag + kind, 1)]]
    W["ev_w_in"] = cols(G[("ev_w_in", 0)])
    W["od_w_in"] = cols(G[("od_w_in", 0)])
    W["ev_w_out"] = G[("ev_w_out", 0)].reshape(4 * ev_w_out.shape[1], D)
    W["od_w_out"] = G[("od_w_out", 0)].reshape(4 * od_w_out.shape[1], D)
    W["ev_conv_w"] = cols(taps_all[:, :32].reshape(4, 32, 128))[:CONV_A_WIDTH]
    W["od_conv_w"] = cols(taps_all[:, 32:48].reshape(4, 8, 256))[:CONV_C_WIDTH]

    loss, grad_x, grads = _local_step(x[0], loss_target[0], W)

    rows = lambda a: a.reshape(4, a.shape[0] // 4, a.shape[1])
    colsh = lambda a: a.reshape(a.shape[0], 4, a.shape[1] // 4).transpose(1, 0, 2)
    grads[("ev_w_in", 0)] = colsh(grads[("ev_w_in", 0)])
    grads[("ev_w_out", 0)] = rows(grads[("ev_w_out", 0)])
    grads[("od_w_out", 0)] = rows(grads[("od_w_out", 0)])
    order = list(reversed(BIG))
    gs = [grads[k] for k in order]
    c_arr = core.reshape(1).astype(jnp.int32)
    chip_arr = chip.reshape(1).astype(jnp.int32)
    others = _pair_exchange(gs)
    sums = [_pair_add(g, o, c_arr) for g, o in zip(gs, others)]
    recvd = _chip_exchange(sums)
    reds = [_chip_sum(s, r, chip_arr) for s, r in zip(sums, recvd)]
    layout = [(BIG_NAMES.index(name), l) for name, l in order]
    big_grads = dict(zip(BIG_NAMES, _pair_share(reds, layout)))

    def small_grad(name):
        if name.endswith("_norm") and name[:3] in ("ffn", "mix"):
            return jnp.concatenate([grads[(name, 0)], grads[(name, 1)]], axis=0)
        return grads[(name, 0)]

    packed = jnp.concatenate([_to_lanes(small_grad(name), r) for name, r in SMALL], axis=0)
    total = _small_all_reduce(packed)
    small_grads, at = {}, 0
    for name, r in SMALL:
        part = total[at:at + r].reshape(-1)
        at += r
        if name == "ev_conv_w":
            full_g = part[:CONV_A_WIDTH * D_CONV].reshape(CONV_A_WIDTH, D_CONV)
            small_grads[name] = lax.dynamic_slice_in_dim(full_g, chip * (D_CONV // 4), D_CONV // 4, axis=1)[None]
        elif name == "od_conv_w":
            full_g = part[:CONV_C_WIDTH * D].reshape(CONV_C_WIDTH, D)
            small_grads[name] = lax.dynamic_slice_in_dim(full_g, chip * (D // 4), D // 4, axis=1)[None]
        else:
            small_grads[name] = part[:math.prod(P[name].shape)].reshape(P[name].shape)

    grad_w, delta_w, new_m, new_v = [], [], [], []
    for name in WEIGHTS:
        g = big_grads[name] if name in big_grads else small_grads[name]
        d, nm, nv = _adamw(P[name], g, M[name], V[name])
        grad_w.append(g)
        delta_w.append(d)
        new_m.append(nm)
        new_v.append(nv)
    loss_all = lax.psum(loss[0, 0], ("x", "y", "c"))
    return (loss_all, grad_x[None], *grad_w, *delta_w, *new_m, *new_v)
```

```python
import functools
import math

import jax
import jax.numpy as jnp
from jax import lax
from jax.experimental import pallas as pl
from jax.experimental.pallas import tpu as pltpu

F32, BF16 = jnp.float32, jnp.bfloat16
EPS = 1e-6
FFN_RES = 0.5
N_HEADS, HEAD_DIM = 8, 64
D_CONV = 512
D_ATTN = N_HEADS * HEAD_DIM
CONV_A_WIDTH, CONV_C_WIDTH = 31, 3
ADAM_LR, ADAM_B1, ADAM_B2, ADAM_EPS, ADAM_WD, ADAM_STEP = 0.001, 0.9, 0.999, 1e-08, 0.01, 10
MESH = pl.DeviceIdType.MESH
ANY = pl.BlockSpec(memory_space=pl.ANY)

TOK_TILE = 512
ATT_TILE = 512
HALO_A, HALO_C = 32, 16
SCAN_BLK = 256
MIB = 2 ** 20


def _pallas(body, **kw):
    return pl.pallas_call(body, **kw)


def _cp(sem=None, vmem_mib=48):
    return pltpu.CompilerParams(dimension_semantics=sem, vmem_limit_bytes=vmem_mib * MIB)


def _dot(a, b):
    return jnp.dot(a, b, preferred_element_type=F32)


def _dot_nt(a, b):
    return lax.dot_general(a, b, (((1,), (1,)), ((), ())), preferred_element_type=F32)


def _dot_tn(a, b):
    return lax.dot_general(a, b, (((0,), (0,)), ((), ())), preferred_element_type=F32)


def _sds(shape, dtype):
    return jax.ShapeDtypeStruct(shape, dtype)


def _rms(x):
    return lax.rsqrt(jnp.mean(x * x, axis=-1, keepdims=True) + EPS)


def _rms_bwd(dy, x, g):
    r = _rms(x)
    xh = x * r
    dxh = dy * g
    dx = r * (dxh - xh * jnp.mean(dxh * xh, axis=-1, keepdims=True))
    return dx, xh


def _silu_grad(z):
    s = jax.nn.sigmoid(z)
    return s * (1.0 + z * (1.0 - s))


def _ffn_fwd(x, g, wg, wu, wd):
    S, D = x.shape
    nc, _, Fs = wg.shape
    tm = TOK_TILE

    def body(x_ref, g_ref, wg_ref, wu_ref, wd_ref, out_ref, xn_ref, G_ref, U_ref, acc_ref):
        j = pl.program_id(1)

        @pl.when(j == 0)
        def _():
            xv = x_ref[...]
            xn_ref[...] = (xv * _rms(xv) * g_ref[...]).astype(BF16)
            acc_ref[...] = jnp.zeros_like(acc_ref)

        xn = xn_ref[...]
        G = _dot(xn, wg_ref[0])
        U = _dot(xn, wu_ref[0])
        G_ref[0] = G.astype(BF16)
        U_ref[0] = U.astype(BF16)
        H = (G * jax.nn.sigmoid(G) * U).astype(BF16)
        acc_ref[...] += _dot(H, wd_ref[0])

        @pl.when(j == nc - 1)
        def _():
            out_ref[...] = x_ref[...] + FFN_RES * acc_ref[...]

    row = pl.BlockSpec((tm, D), lambda i, j: (i, 0))
    return _pallas(
        body, name="ffn_fwd", grid=(S // tm, nc),
        in_specs=[row, pl.BlockSpec((1, D), lambda i, j: (0, 0)),
                  pl.BlockSpec((1, D, Fs), lambda i, j: (j, 0, 0)), pl.BlockSpec((1, D, Fs), lambda i, j: (j, 0, 0)),
                  pl.BlockSpec((1, Fs, D), lambda i, j: (j, 0, 0))],
        out_specs=[row, row, pl.BlockSpec((1, tm, Fs), lambda i, j: (j, i, 0)),
                   pl.BlockSpec((1, tm, Fs), lambda i, j: (j, i, 0))],
        out_shape=[_sds((S, D), F32), _sds((S, D), BF16), _sds((nc, S, Fs), BF16), _sds((nc, S, Fs), BF16)],
        scratch_shapes=[pltpu.VMEM((tm, D), F32)],
        compiler_params=_cp(("parallel", "arbitrary")),
    )(x, g, wg, wu, wd)


def _ffn_bwd_w(dout, xn, G, U, wd):
    S, D = dout.shape
    nc, _, Fs = G.shape
    tm = TOK_TILE
    nt = S // tm

    def body(do_ref, xn_ref, G_ref, U_ref, wd_ref, dwg_ref, dwu_ref, dwd_ref, dG_ref, dU_ref, ag, au, ad):
        i = pl.program_id(1)

        @pl.when(i == 0)
        def _():
            ag[...] = jnp.zeros_like(ag)
            au[...] = jnp.zeros_like(au)
            ad[...] = jnp.zeros_like(ad)

        do = (FFN_RES * do_ref[...]).astype(BF16)
        Gv = G_ref[0].astype(F32)
        Uv = U_ref[0].astype(F32)
        dH = _dot_nt(do, wd_ref[0])
        sg = jax.nn.sigmoid(Gv)
        act = Gv * sg
        H = (act * Uv).astype(BF16)
        dU = (dH * act).astype(BF16)
        dG = (dH * Uv * (sg * (1.0 + Gv * (1.0 - sg)))).astype(BF16)
        dG_ref[0] = dG
        dU_ref[0] = dU
        xnv = xn_ref[...]
        ag[...] += _dot_tn(xnv, dG)
        au[...] += _dot_tn(xnv, dU)
        ad[...] += _dot_tn(H, do)

        @pl.when(i == nt - 1)
        def _():
            dwg_ref[0] = ag[...].astype(BF16)
            dwu_ref[0] = au[...].astype(BF16)
            dwd_ref[0] = ad[...].astype(BF16)

    row = pl.BlockSpec((tm, D), lambda j, i: (i, 0))
    hid = pl.BlockSpec((1, tm, Fs), lambda j, i: (j, i, 0))
    wcol = pl.BlockSpec((1, D, Fs), lambda j, i: (j, 0, 0))
    wrow = pl.BlockSpec((1, Fs, D), lambda j, i: (j, 0, 0))
    return _pallas(
        body, name="ffn_bwd_w", grid=(nc, nt),
        in_specs=[row, row, hid, hid, wrow],
        out_specs=[wcol, wcol, wrow, hid, hid],
        out_shape=[_sds((nc, D, Fs), BF16), _sds((nc, D, Fs), BF16), _sds((nc, Fs, D), BF16),
                   _sds((nc, S, Fs), BF16), _sds((nc, S, Fs), BF16)],
        scratch_shapes=[pltpu.VMEM((D, Fs), F32), pltpu.VMEM((D, Fs), F32), pltpu.VMEM((Fs, D), F32)],
        compiler_params=_cp(("parallel", "arbitrary"), 56),
    )(dout, xn, G, U, wd)


def _norm_in_bwd(dzs, ws, x, g, dres):
    S, D = x.shape
    nc = dzs[0].shape[0]
    n = len(dzs)
    tm = TOK_TILE

    def body(*refs):
        dz_refs, w_refs = refs[:n], refs[n:2 * n]
        x_ref, g_ref, dres_ref, dx_ref, dg_ref, acc_ref = refs[2 * n:]
        i, j = pl.program_id(0), pl.program_id(1)

        @pl.when(j == 0)
        def _():
            acc_ref[...] = jnp.zeros_like(acc_ref)

        @pl.when((i == 0) & (j == 0))
        def _():
            dg_ref[...] = jnp.zeros_like(dg_ref)

        for dz_ref, w_ref in zip(dz_refs, w_refs):
            acc_ref[...] += _dot_nt(dz_ref[0], w_ref[0])

        @pl.when(j == nc - 1)
        def _():
            dxn = acc_ref[...]
            dx, xh = _rms_bwd(dxn, x_ref[...], g_ref[...])
            dx_ref[...] = dx + dres_ref[...]
            dg_ref[...] += jnp.sum(dxn * xh, axis=0, keepdims=True)

    row = pl.BlockSpec((tm, D), lambda i, j: (i, 0))
    one = pl.BlockSpec((1, D), lambda i, j: (0, 0))
    in_specs = [pl.BlockSpec((1, tm, dz.shape[2]), lambda i, j: (j, i, 0)) for dz in dzs]
    in_specs += [pl.BlockSpec((1, D, w.shape[2]), lambda i, j: (j, 0, 0)) for w in ws]
    return _pallas(
        body, name="norm_in_bwd", grid=(S // tm, nc),
        in_specs=in_specs + [row, one, row], out_specs=[row, one],
        out_shape=[_sds((S, D), F32), _sds((1, D), F32)],
        scratch_shapes=[pltpu.VMEM((tm, D), F32)],
        compiler_params=_cp(("arbitrary", "arbitrary")),
    )(*dzs, *ws, x, g, dres)


def _norm_proj(x, g, w, w2=None):
    S, D = x.shape
    N = w.shape[1]
    tm = TOK_TILE

    def body(*refs):
        if w2 is None:
            x_ref, g_ref, w_ref, h_ref, z_ref = refs
        else:
            x_ref, g_ref, w_ref, w2_ref, h_ref, z_ref, z2_ref = refs
        xv = x_ref[...]
        h = (xv * _rms(xv) * g_ref[...]).astype(BF16)
        h_ref[...] = h
        z_ref[...] = _dot(h, w_ref[...]).astype(BF16)
        if w2 is not None:
            z2_ref[...] = _dot(h, w2_ref[...])

    row = pl.BlockSpec((tm, D), lambda i: (i, 0))
    in_specs = [row, pl.BlockSpec((1, D), lambda i: (0, 0)), pl.BlockSpec((D, N), lambda i: (0, 0))]
    out_specs = [row, pl.BlockSpec((tm, N), lambda i: (i, 0))]
    out_shape = [_sds((S, D), BF16), _sds((S, N), BF16)]
    args = [x, g, w]
    if w2 is not None:
        N2 = w2.shape[1]
        in_specs.append(pl.BlockSpec((D, N2), lambda i: (0, 0)))
        out_specs.append(pl.BlockSpec((tm, N2), lambda i: (i, 0)))
        out_shape.append(_sds((S, N2), F32))
        args.append(w2)
    return _pallas(body, name="norm_proj", grid=(S // tm,), in_specs=in_specs, out_specs=out_specs,
                   out_shape=out_shape, compiler_params=_cp(("parallel",)))(*args)


def _proj_res(acts, ws, res):
    S, D = res.shape
    n = len(acts)
    tm = TOK_TILE

    def body(*refs):
        a_refs, w_refs = refs[:n], refs[n:2 * n]
        res_ref, out_ref = refs[2 * n:]
        acc = res_ref[...]
        for a_ref, w_ref in zip(a_refs, w_refs):
            acc = acc + _dot(a_ref[...], w_ref[...])
        out_ref[...] = acc

    row = pl.BlockSpec((tm, D), lambda i: (i, 0))
    in_specs = [pl.BlockSpec((tm, a.shape[1]), lambda i: (i, 0)) for a in acts]
    in_specs += [pl.BlockSpec(w.shape, lambda i: (0, 0)) for w in ws]
    return _pallas(body, name="proj_res", grid=(S // tm,), in_specs=in_specs + [row], out_specs=row,
                   out_shape=_sds((S, D), F32), compiler_params=_cp(("parallel",)))(*acts, *ws, res)


def _matmul_nt(a, w):
    S, K = a.shape
    M = w.shape[0]
    tm = TOK_TILE

    def body(a_ref, w_ref, o_ref):
        o_ref[...] = _dot_nt(a_ref[...].astype(BF16), w_ref[...])

    return _pallas(body, name="matmul_nt", grid=(S // tm,),
                   in_specs=[pl.BlockSpec((tm, K), lambda i: (i, 0)), pl.BlockSpec((M, K), lambda i: (0, 0))],
                   out_specs=pl.BlockSpec((tm, M), lambda i: (i, 0)), out_shape=_sds((S, M), F32),
                   compiler_params=_cp(("parallel",)))(a, w)


def _matmul_tn(a, b, tn):
    S, M = a.shape
    N = b.shape[1]
    tm = TOK_TILE
    nt = S // tm

    def body(a_ref, b_ref, o_ref, acc_ref):
        i = pl.program_id(1)

        @pl.when(i == 0)
        def _():
            acc_ref[...] = jnp.zeros_like(acc_ref)

        acc_ref[...] += _dot_tn(a_ref[...].astype(BF16), b_ref[...].astype(BF16))

        @pl.when(i == nt - 1)
        def _():
            o_ref[0] = acc_ref[...].astype(BF16)

    return _pallas(body, name="matmul_tn", grid=(N // tn, nt),
                   in_specs=[pl.BlockSpec((tm, M), lambda j, i: (i, 0)), pl.BlockSpec((tm, tn), lambda j, i: (i, j))],
                   out_specs=pl.BlockSpec((1, M, tn), lambda j, i: (j, 0, 0)), out_shape=_sds((N // tn, M, tn), BF16),
                   scratch_shapes=[pltpu.VMEM((M, tn), F32)],
                   compiler_params=_cp(("parallel", "arbitrary")))(a, b)


def _conv_a_fwd(z, cw, cb, cn):
    S = z.shape[0]
    C = D_CONV
    tm = TOK_TILE
    hb = tm // HALO_A

    def body(u_ref, gt_ref, up_ref, gp_ref, cw_ref, cb_ref, cn_ref, a_ref, a1_ref, win):
        i = pl.program_id(0)
        prev = up_ref[...].astype(F32) * jax.nn.sigmoid(gp_ref[...].astype(F32))
        win[pl.ds(0, HALO_A), :] = jnp.where(i == 0, 0.0, prev)
        win[pl.ds(HALO_A, tm), :] = u_ref[...].astype(F32) * jax.nn.sigmoid(gt_ref[...].astype(F32))
        acc = jnp.zeros((tm, C), F32)
        for k in range(CONV_A_WIDTH):
            acc = acc + cw_ref[k:k + 1, :] * win[pl.ds(HALO_A - (CONV_A_WIDTH - 1) + k, tm), :]
        a1 = acc + cb_ref[...]
        a1_ref[...] = a1
        a2 = a1 * _rms(a1) * cn_ref[...]
        a_ref[...] = (a2 * jax.nn.sigmoid(a2)).astype(BF16)

    cur = lambda c: pl.BlockSpec((tm, C), lambda i, c=c: (i, c))
    prv = lambda c: pl.BlockSpec((HALO_A, C), lambda i, c=c: (jnp.maximum(i * hb - 1, 0), c))
    vec = pl.BlockSpec((1, C), lambda i: (0, 0))
    return _pallas(body, name="conv_a_fwd", grid=(S // tm,),
                   in_specs=[cur(0), cur(1), prv(0), prv(1), pl.BlockSpec((32, C), lambda i: (0, 0)), vec, vec],
                   out_specs=[pl.BlockSpec((tm, C), lambda i: (i, 0)), pl.BlockSpec((tm, C), lambda i: (i, 0))],
                   out_shape=[_sds((S, C), BF16), _sds((S, C), F32)],
                   scratch_shapes=[pltpu.VMEM((tm + HALO_A, C), F32)],
                   compiler_params=_cp(("parallel",)))(z, z, z, z, cw, cb, cn)


def _conv_a_bwd(da, a1, z, cw, cn):
    S = z.shape[0]
    C = D_CONV
    tm = TOK_TILE
    hb = tm // HALO_A
    nt = S // tm
    W = CONV_A_WIDTH

    def body(da_ref, a1_ref, dan_ref, a1n_ref, u_ref, gt_ref, up_ref, gp_ref, cw_ref, cn_ref,
             duz_ref, dcw_ref, dcb_ref, dcn_ref, win, dwin):
        i = pl.program_id(0)

        @pl.when(i == 0)
        def _():
            dcw_ref[...] = jnp.zeros_like(dcw_ref)
            dcb_ref[...] = jnp.zeros_like(dcb_ref)
            dcn_ref[...] = jnp.zeros_like(dcn_ref)

        cnv = cn_ref[...]

        def da1_of(dav, a1v):
            a2 = a1v * _rms(a1v) * cnv
            da2 = dav * _silu_grad(a2)
            dx, xh = _rms_bwd(da2, a1v, cnv)
            return dx, da2 * xh

        da1, dcn_t = da1_of(da_ref[...], a1_ref[...])
        da1n, _ = da1_of(dan_ref[...], a1n_ref[...])
        dwin[pl.ds(0, tm), :] = da1
        dwin[pl.ds(tm, HALO_A), :] = jnp.where(i == nt - 1, 0.0, da1n)
        dcb_ref[...] += jnp.sum(da1, axis=0, keepdims=True)
        dcn_ref[...] += jnp.sum(dcn_t, axis=0, keepdims=True)

        u = u_ref[...].astype(F32)
        sg = jax.nn.sigmoid(gt_ref[...].astype(F32))
        prev = up_ref[...].astype(F32) * jax.nn.sigmoid(gp_ref[...].astype(F32))
        win[pl.ds(0, HALO_A), :] = jnp.where(i == 0, 0.0, prev)
        win[pl.ds(HALO_A, tm), :] = u * sg

        da0 = jnp.zeros((tm, C), F32)
        for k in range(W):
            da0 = da0 + cw_ref[k:k + 1, :] * dwin[pl.ds(W - 1 - k, tm), :]
            dcw_ref[k:k + 1, :] += jnp.sum(da1 * win[pl.ds(HALO_A - (W - 1) + k, tm), :], axis=0, keepdims=True)
        duz_ref[:, 0:C] = (da0 * sg).astype(BF16)
        duz_ref[:, C:2 * C] = (da0 * u * sg * (1.0 - sg)).astype(BF16)

    cur = lambda c: pl.BlockSpec((tm, C), lambda i, c=c: (i, c))
    prv = lambda c: pl.BlockSpec((HALO_A, C), lambda i, c=c: (jnp.maximum(i * hb - 1, 0), c))
    nxt = pl.BlockSpec((HALO_A, C), lambda i: (jnp.minimum((i + 1) * hb, S // HALO_A - 1), 0))
    vec = pl.BlockSpec((1, C), lambda i: (0, 0))
    return _pallas(body, name="conv_a_bwd", grid=(nt,),
                   in_specs=[cur(0), cur(0), nxt, nxt, cur(0), cur(1), prv(0), prv(1),
                             pl.BlockSpec((32, C), lambda i: (0, 0)), vec],
                   out_specs=[pl.BlockSpec((tm, 2 * C), lambda i: (i, 0)), pl.BlockSpec((32, C), lambda i: (0, 0)), vec, vec],
                   out_shape=[_sds((S, 2 * C), BF16), _sds((32, C), F32), _sds((1, C), F32), _sds((1, C), F32)],
                   scratch_shapes=[pltpu.VMEM((tm + HALO_A, C), F32), pltpu.VMEM((tm + HALO_A, C), F32)],
                   compiler_params=_cp(("arbitrary",)))(da, a1, da, a1, z, z, z, z, cw, cn)


def _forget_scan(fl, bf):
    S, L = fl.shape
    B = SCAN_BLK

    def body(fl_ref, bf_ref, flb_ref, F_ref):
        tri = (lax.broadcasted_iota(jnp.int32, (B, B), 0) >= lax.broadcasted_iota(jnp.int32, (B, B), 1)).astype(F32)

        def step(c, carry):
            rows = pl.ds(pl.multiple_of(c * B, B), B)
            v = fl_ref[rows, :] + bf_ref[...]
            flb_ref[rows, :] = v
            lf = jnp.minimum(v, 0.0) - jnp.log1p(jnp.exp(-jnp.abs(v)))
            cs = jnp.dot(tri, lf, precision=lax.Precision.HIGHEST, preferred_element_type=F32) + carry
            F_ref[rows, :] = cs
            return cs[B - 1:B, :]

        lax.fori_loop(0, S // B, step, jnp.zeros((1, L), F32))

    return _pallas(body, name="forget_scan", out_shape=[_sds((S, L), F32), _sds((S, L), F32)],
                   compiler_params=_cp())(fl, bf)


def _forget_scan_bwd(dF, flb):
    S, L = dF.shape
    B = SCAN_BLK
    nb = S // B

    def body(dF_ref, flb_ref, dfl_ref, db_ref):
        tri = (lax.broadcasted_iota(jnp.int32, (B, B), 0) <= lax.broadcasted_iota(jnp.int32, (B, B), 1)).astype(F32)

        def step(t, carry):
            carry_cs, db = carry
            rows = pl.ds(pl.multiple_of((nb - 1 - t) * B, B), B)
            cs = jnp.dot(tri, dF_ref[rows, :], precision=lax.Precision.HIGHEST, preferred_element_type=F32) + carry_cs
            dfl = cs * jax.nn.sigmoid(-flb_ref[rows, :])
            dfl_ref[rows, :] = dfl
            return cs[0:1, :], db + jnp.sum(dfl, axis=0, keepdims=True)

        _, db = lax.fori_loop(0, nb, step, (jnp.zeros((1, L), F32), jnp.zeros((1, L), F32)))
        db_ref[...] = db

    return _pallas(body, name="forget_scan_bwd", out_shape=[_sds((S, L), F32), _sds((1, L), F32)],
                   compiler_params=_cp())(dF, flb)


def _qk_norm(q, k, qw, kw):
    H, S, Dh = q.shape
    tq = ATT_TILE
    scale = 1.0 / math.sqrt(Dh)

    def body(q_ref, k_ref, qw_ref, kw_ref, qn_ref, kn_ref):
        qv = q_ref[0].astype(F32)
        kv = k_ref[0].astype(F32)
        qn_ref[0] = (qv * _rms(qv) * qw_ref[...] * scale).astype(BF16)
        kn_ref[0] = (kv * _rms(kv) * kw_ref[...]).astype(BF16)

    blk = pl.BlockSpec((1, tq, Dh), lambda h, i: (h, i, 0))
    vec = pl.BlockSpec((1, Dh), lambda h, i: (0, 0))
    return _pallas(body, name="qk_norm", grid=(H, S // tq), in_specs=[blk, blk, vec, vec], out_specs=[blk, blk],
                   out_shape=[_sds((H, S, Dh), BF16), _sds((H, S, Dh), BF16)],
                   compiler_params=_cp(("parallel", "parallel")))(q, k, qw, kw)


def _qk_norm_bwd(dqs, dkn, q, k, qw, kw):
    H, S, Dh = q.shape
    tq = ATT_TILE
    scale = 1.0 / math.sqrt(Dh)

    def body(dqs_ref, dkn_ref, q_ref, k_ref, qw_ref, kw_ref, dq_ref, dk_ref, dqw_ref, dkw_ref):
        @pl.when((pl.program_id(0) == 0) & (pl.program_id(1) == 0))
        def _():
            dqw_ref[...] = jnp.zeros_like(dqw_ref)
            dkw_ref[...] = jnp.zeros_like(dkw_ref)

        dqn = dqs_ref[0] * scale
        dq, qh = _rms_bwd(dqn, q_ref[0].astype(F32), qw_ref[...])
        dq_ref[0] = dq.astype(BF16)
        dqw_ref[...] += jnp.sum(dqn * qh, axis=0, keepdims=True)
        dkv = dkn_ref[0]
        dk, kh = _rms_bwd(dkv, k_ref[0].astype(F32), kw_ref[...])
        dk_ref[0] = dk.astype(BF16)
        dkw_ref[...] += jnp.sum(dkv * kh, axis=0, keepdims=True)

    blk = pl.BlockSpec((1, tq, Dh), lambda h, i: (h, i, 0))
    vec = pl.BlockSpec((1, Dh), lambda h, i: (0, 0))
    return _pallas(body, name="qk_norm_bwd", grid=(H, S // tq), in_specs=[blk, blk, blk, blk, vec, vec],
                   out_specs=[blk, blk, vec, vec],
                   out_shape=[_sds((H, S, Dh), BF16), _sds((H, S, Dh), BF16), _sds((1, Dh), F32), _sds((1, Dh), F32)],
                   compiler_params=_cp(("arbitrary", "arbitrary")))(dqs, dkn, q, k, qw, kw)


NEG = -1e30


def _causal_mask(t):
    return lax.broadcasted_iota(jnp.int32, (t, t), 0) >= lax.broadcasted_iota(jnp.int32, (t, t), 1)


def _fox_fwd(qs, kn, v, fcol, frow):
    H, S, Dh = qs.shape
    t = ATT_TILE
    nq = S // t

    def body(q_ref, k_ref, v_ref, fc_ref, fr_ref, o_ref, lse_ref):
        i = pl.program_id(1)
        q = q_ref[0]
        fq = fc_ref[0]

        def tile(j, carry, diag):
            m, l, acc = carry
            rows = pl.ds(pl.multiple_of(j * t, t), t)
            s = _dot_nt(q, k_ref[0, rows, :]) + fq - fr_ref[0, j]
            if diag:
                s = jnp.where(_causal_mask(t), s, NEG)
            m_new = jnp.maximum(m, jnp.max(s, axis=-1, keepdims=True))
            p = jnp.exp(s - m_new)
            alpha = jnp.exp(m - m_new)
            l = alpha * l + jnp.sum(p, axis=-1, keepdims=True)
            acc = alpha * acc + _dot(p.astype(BF16), v_ref[0, rows, :])
            return m_new, l, acc

        init = (jnp.full((t, 1), NEG, F32), jnp.zeros((t, 1), F32), jnp.zeros((t, Dh), F32))
        carry = lax.fori_loop(0, i, lambda j, c: tile(j, c, False), init)
        m, l, acc = tile(i, carry, True)
        o_ref[0] = (acc / l).astype(BF16)
        lse_ref[0] = m + jnp.log(l)

    qblk = pl.BlockSpec((1, t, Dh), lambda h, i: (h, i, 0))
    full = pl.BlockSpec((1, S, Dh), lambda h, i: (h, 0, 0))
    col = pl.BlockSpec((1, t, 1), lambda h, i: (h, i, 0))
    return _pallas(body, name="fox_fwd", grid=(H, nq),
                   in_specs=[qblk, full, full, col, pl.BlockSpec((1, nq, 1, t), lambda h, i: (h, 0, 0, 0))],
                   out_specs=[qblk, col], out_shape=[_sds((H, S, Dh), BF16), _sds((H, S, 1), F32)],
                   compiler_params=_cp(("parallel", "parallel")))(qs, kn, v, fcol, frow)


def _fox_bwd(qs, kn, v, o, do, lse, fcol, frow):
    H, S, Dh = qs.shape
    t = ATT_TILE
    nq = S // t

    def body(q_ref, k_ref, v_ref, o_ref, do_ref, lse_ref, fc_ref, fr_ref, dq_ref, dk_ref, dv_ref, dfq_ref, dfk_ref):
        j = pl.program_id(1)

        @pl.when(j == 0)
        def _():
            dq_ref[...] = jnp.zeros_like(dq_ref)
            dfq_ref[...] = jnp.zeros_like(dfq_ref)

        k = k_ref[0]
        vv = v_ref[0]
        fk = fr_ref[0, 0]

        def tile(i, carry, diag):
            dk, dv, dfk = carry
            rows = pl.ds(pl.multiple_of(i * t, t), t)
            q = q_ref[0, rows, :]
            dov = do_ref[0, rows, :]
            delta = jnp.sum(dov.astype(F32) * o_ref[0, rows, :].astype(F32), axis=-1, keepdims=True)
            s = _dot_nt(q, k) + fc_ref[0, rows, :] - fk
            if diag:
                s = jnp.where(_causal_mask(t), s, NEG)
            p = jnp.exp(s - lse_ref[0, rows, :])
            dv = dv + _dot_tn(p.astype(BF16), dov)
            ds = p * (_dot_nt(dov, vv) - delta)
            dsb = ds.astype(BF16)
            dq_ref[0, rows, :] += _dot(dsb, k)
            dk = dk + _dot_tn(dsb, q)
            dfq_ref[0, rows, :] += jnp.sum(ds, axis=-1, keepdims=True)
            dfk = dfk + jnp.sum(ds, axis=0, keepdims=True)
            return dk, dv, dfk

        init = (jnp.zeros((t, Dh), F32), jnp.zeros((t, Dh), F32), jnp.zeros((1, t), F32))
        carry = tile(j, init, True)
        dk, dv, dfk = lax.fori_loop(j + 1, nq, lambda i, c: tile(i, c, False), carry)
        dk_ref[0] = dk
        dv_ref[0] = dv
        dfk_ref[0, 0] = dfk

    full = pl.BlockSpec((1, S, Dh), lambda h, j: (h, 0, 0))
    kblk = pl.BlockSpec((1, t, Dh), lambda h, j: (h, j, 0))
    colf = pl.BlockSpec((1, S, 1), lambda h, j: (h, 0, 0))
    rowb = pl.BlockSpec((1, 1, 1, t), lambda h, j: (h, j, 0, 0))
    return _pallas(body, name="fox_bwd", grid=(H, nq),
                   in_specs=[full, kblk, kblk, full, full, colf, colf, rowb],
                   out_specs=[full, kblk, kblk, colf, rowb],
                   out_shape=[_sds((H, S, Dh), F32), _sds((H, S, Dh), F32), _sds((H, S, Dh), F32),
                              _sds((H, S, 1), F32), _sds((H, nq, 1, t), F32)],
                   compiler_params=_cp(("parallel", "arbitrary"), 56))(qs, kn, v, o, do, lse, fcol, frow)


def _odd_mid_fwd(z, cw):
    S = z.shape[0]
    D = z.shape[1] // 3
    tm = TOK_TILE
    hb = tm // HALO_C
    W = CONV_C_WIDTH

    def body(gb_ref, gc_ref, hh_ref, gcp_ref, hhp_ref, cw_ref, y_ref, win):
        i = pl.program_id(0)
        prev = gcp_ref[...].astype(F32) * hhp_ref[...].astype(F32)
        win[pl.ds(0, HALO_C), :] = jnp.where(i == 0, 0.0, prev)
        win[pl.ds(HALO_C, tm), :] = gc_ref[...].astype(F32) * hh_ref[...].astype(F32)
        c1 = jnp.zeros((tm, D), F32)
        for k in range(W):
            c1 = c1 + cw_ref[k:k + 1, :] * win[pl.ds(HALO_C - (W - 1) + k, tm), :]
        y_ref[...] = (gb_ref[...].astype(F32) * c1).astype(BF16)

    cur = lambda c: pl.BlockSpec((tm, D), lambda i, c=c: (i, c))
    prv = lambda c: pl.BlockSpec((HALO_C, D), lambda i, c=c: (jnp.maximum(i * hb - 1, 0), c))
    return _pallas(body, name="odd_mid_fwd", grid=(S // tm,),
                   in_specs=[cur(0), cur(1), cur(2), prv(1), prv(2), pl.BlockSpec((8, D), lambda i: (0, 0))],
                   out_specs=pl.BlockSpec((tm, D), lambda i: (i, 0)), out_shape=_sds((S, D), BF16),
                   scratch_shapes=[pltpu.VMEM((tm + HALO_C, D), F32)],
                   compiler_params=_cp(("parallel",)))(z, z, z, z, z, cw)


def _odd_mid_bwd(dy, z, cw):
    S = z.shape[0]
    D = z.shape[1] // 3
    tm = TOK_TILE
    hb = tm // HALO_C
    nt = S // tm
    W = CONV_C_WIDTH

    def body(dy_ref, dyn_ref, gb_ref, gbn_ref, gc_ref, hh_ref, gcp_ref, hhp_ref, cw_ref, dz_ref, dcw_ref, win, dwin):
        i = pl.program_id(0)

        @pl.when(i == 0)
        def _():
            dcw_ref[...] = jnp.zeros_like(dcw_ref)

        gc = gc_ref[...].astype(F32)
        hh = hh_ref[...].astype(F32)
        prev = gcp_ref[...].astype(F32) * hhp_ref[...].astype(F32)
        win[pl.ds(0, HALO_C), :] = jnp.where(i == 0, 0.0, prev)
        win[pl.ds(HALO_C, tm), :] = gc * hh
        dyv = dy_ref[...]
        dc1 = dyv * gb_ref[...].astype(F32)
        dwin[pl.ds(0, tm), :] = dc1
        dwin[pl.ds(tm, HALO_C), :] = jnp.where(i == nt - 1, 0.0, dyn_ref[...] * gbn_ref[...].astype(F32))
        c1 = jnp.zeros((tm, D), F32)
        dc0 = jnp.zeros((tm, D), F32)
        for k in range(W):
            tap = win[pl.ds(HALO_C - (W - 1) + k, tm), :]
            c1 = c1 + cw_ref[k:k + 1, :] * tap
            dc0 = dc0 + cw_ref[k:k + 1, :] * dwin[pl.ds(W - 1 - k, tm), :]
            dcw_ref[k:k + 1, :] += jnp.sum(dc1 * tap, axis=0, keepdims=True)
        dz_ref[:, 0:D] = (dyv * c1).astype(BF16)
        dz_ref[:, D:2 * D] = (dc0 * hh).astype(BF16)
        dz_ref[:, 2 * D:3 * D] = (dc0 * gc).astype(BF16)

    cur = lambda c: pl.BlockSpec((tm, D), lambda i, c=c: (i, c))
    prv = lambda c: pl.BlockSpec((HALO_C, D), lambda i, c=c: (jnp.maximum(i * hb - 1, 0), c))
    nxt = pl.BlockSpec((HALO_C, D), lambda i: (jnp.minimum((i + 1) * hb, S // HALO_C - 1), 0))
    return _pallas(body, name="odd_mid_bwd", grid=(nt,),
                   in_specs=[cur(0), nxt, cur(0), nxt, cur(1), cur(2), prv(1), prv(2), pl.BlockSpec((8, D), lambda i: (0, 0))],
                   out_specs=[pl.BlockSpec((tm, 3 * D), lambda i: (i, 0)), pl.BlockSpec((8, D), lambda i: (0, 0))],
                   out_shape=[_sds((S, 3 * D), BF16), _sds((8, D), F32)],
                   scratch_shapes=[pltpu.VMEM((tm + HALO_C, D), F32), pltpu.VMEM((tm + HALO_C, D), F32)],
                   compiler_params=_cp(("arbitrary",)))(dy, dy, z, z, z, z, z, z, cw)


def _loss_head(y, tgt):
    S, D = y.shape
    tm = TOK_TILE

    def body(y_ref, t_ref, dy_ref, l_ref):
        @pl.when(pl.program_id(0) == 0)
        def _():
            l_ref[...] = jnp.zeros_like(l_ref)

        e = y_ref[...] - t_ref[...]
        dy_ref[...] = e * (1.0 / D)
        l_ref[...] += jnp.sum(jnp.sum(e * e, axis=-1, keepdims=True), axis=0, keepdims=True) * (0.5 / D)

    row = pl.BlockSpec((tm, D), lambda i: (i, 0))
    return _pallas(body, name="loss_head", grid=(S // tm,), in_specs=[row, row],
                   out_specs=[row, pl.BlockSpec((1, 1), lambda i: (0, 0))],
                   out_shape=[_sds((S, D), F32), _sds((1, 1), F32)],
                   compiler_params=_cp(("arbitrary",)))(y, tgt)


def _heads(a):
    S = a.shape[0]
    return a.reshape(S, N_HEADS, HEAD_DIM).transpose(1, 0, 2)


def _unheads(a):
    return a.transpose(1, 0, 2).reshape(a.shape[1], D_ATTN)


def _pad_rows(a, rows):
    return jnp.pad(a, ((0, rows - a.shape[0]), (0, 0)))


def _local_step(x, tgt, W):
    S, D = x.shape
    nq = S // ATT_TILE
    grads = {}
    saved = {}

    def ffn_f(tag, l, xin):
        out, xn, G, U = _ffn_fwd(xin, W[tag + "_norm"][l:l + 1], W[tag + "_w_gate"][l], W[tag + "_w_up"][l],
                                 W[tag + "_w_down"][l])
        saved[(tag, l)] = (xin, xn, G, U)
        return out

    def ffn_b(tag, l, dout):
        xin, xn, G, U = saved[(tag, l)]
        dwg, dwu, dwd, dG, dU = _ffn_bwd_w(dout, xn, G, U, W[tag + "_w_down"][l])
        dx, dg = _norm_in_bwd([dG, dU], [W[tag + "_w_gate"][l], W[tag + "_w_up"][l]], xin,
                              W[tag + "_norm"][l:l + 1], dout)
        grads[(tag + "_w_gate", l)] = dwg
        grads[(tag + "_w_up", l)] = dwu
        grads[(tag + "_w_down", l)] = dwd
        grads[(tag + "_norm", l)] = dg
        return dx

    x0a = ffn_f("ffn1", 0, x)
    w_in = W["ev_w_in"]
    w_main, w_f = w_in[:, :2560], jnp.pad(w_in[:, 2560:], ((0, 0), (0, 120)))
    h0, z0, fl = _norm_proj(x0a, W["mix_norm"][0:1], w_main, w_f)
    cw_a = _pad_rows(W["ev_conv_w"], 32)
    a_act, a1 = _conv_a_fwd(z0, cw_a, W["ev_conv_b"], W["ev_conv_norm"])
    flb, Fc = _forget_scan(fl, jnp.pad(W["ev_b_f"], ((0, 0), (0, 120))))
    Ft = Fc[:, :N_HEADS].T
    fcol = Ft.reshape(N_HEADS, S, 1)
    frow = Ft.reshape(N_HEADS, nq, 1, ATT_TILE)
    q_raw, k_raw, v_h = _heads(z0[:, 1024:1536]), _heads(z0[:, 1536:2048]), _heads(z0[:, 2048:2560])
    qs, kn = _qk_norm(q_raw, k_raw, W["ev_q_norm"], W["ev_k_norm"])
    o_h, lse = _fox_fwd(qs, kn, v_h, fcol, frow)
    o_flat = _unheads(o_h)
    w_out_e = W["ev_w_out"]
    x0b = _proj_res([a_act, o_flat], [w_out_e[:D_CONV], w_out_e[D_CONV:]], x0a)
    x0c = ffn_f("ffn2", 0, x0b)
    x1a = ffn_f("ffn1", 1, x0c)
    h1, z1 = _norm_proj(x1a, W["mix_norm"][1:2], W["od_w_in"])
    cw_c = _pad_rows(W["od_conv_w"], 8)
    y1 = _odd_mid_fwd(z1, cw_c)
    x1b = _proj_res([y1], [W["od_w_out"]], x1a)
    x1c = ffn_f("ffn2", 1, x1b)
    dy, loss = _loss_head(x1c, tgt)

    d = ffn_b("ffn2", 1, dy)
    dy1 = _matmul_nt(d, W["od_w_out"])
    grads[("od_w_out", 0)] = _matmul_tn(y1, d, D)[0]
    dz1, dcw_c = _odd_mid_bwd(dy1, z1, cw_c)
    grads[("od_conv_w", 0)] = dcw_c[:CONV_C_WIDTH]
    grads[("od_w_in", 0)] = _matmul_tn(h1, dz1, 3 * D // 4)
    d, dg = _norm_in_bwd([dz1[None]], [W["od_w_in"][None]], x1a, W["mix_norm"][1:2], d)
    grads[("mix_norm", 1)] = dg
    d = ffn_b("ffn1", 1, d)
    d = ffn_b("ffn2", 0, d)
    dcat = _matmul_nt(d, w_out_e)
    grads[("ev_w_out", 0)] = jnp.concatenate([_matmul_tn(a_act, d, D)[0], _matmul_tn(o_flat, d, D)[0]], axis=0)
    duz, dcw_a, dcb, dcn = _conv_a_bwd(dcat, a1, z0, cw_a, W["ev_conv_norm"])
    grads[("ev_conv_w", 0)] = dcw_a[:CONV_A_WIDTH]
    grads[("ev_conv_b", 0)] = dcb
    grads[("ev_conv_norm", 0)] = dcn
    do_h = _heads(dcat[:, D_CONV:].astype(BF16))
    dqs, dkn, dv, dfq, dfk = _fox_bwd(qs, kn, v_h, o_h, do_h, lse, fcol, frow)
    dq_raw, dk_raw, dqw, dkw = _qk_norm_bwd(dqs, dkn, q_raw, k_raw, W["ev_q_norm"], W["ev_k_norm"])
    grads[("ev_q_norm", 0)] = dqw
    grads[("ev_k_norm", 0)] = dkw
    dF = (dfq.reshape(N_HEADS, S) - dfk.reshape(N_HEADS, S)).T
    dfl, dbf = _forget_scan_bwd(jnp.pad(dF, ((0, 0), (0, 120))), flb)
    grads[("ev_b_f", 0)] = dbf[:, :N_HEADS]
    dz0 = jnp.concatenate([duz, _unheads(dq_raw), _unheads(dk_raw), _unheads(dv.astype(BF16))], axis=1)
    dflb = dfl.astype(BF16)
    gmain = _matmul_tn(h0, dz0, 640)
    gmain = gmain.transpose(1, 0, 2).reshape(D, 2560)
    gf = _matmul_tn(h0, dflb, 128)[0][:, :N_HEADS]
    grads[("ev_w_in", 0)] = jnp.concatenate([gmain, gf], axis=1)
    d, dg = _norm_in_bwd([dz0[None], dflb[None]], [w_main[None], w_f[None]], x0a, W["mix_norm"][0:1], d)
    grads[("mix_norm", 0)] = dg
    d = ffn_b("ffn1", 0, d)
    return loss, d, grads


def _place():
    x, y, c = lax.axis_index("x"), lax.axis_index("y"), lax.axis_index("c")
    chips = [(1 - x, y), (x, 1 - y), (1 - x, 1 - y)]
    return x, y, c, chips


def _remote(src, dst, send_sem, recv_sem, to):
    return pltpu.make_async_remote_copy(src_ref=src, dst_ref=dst, send_sem=send_sem, recv_sem=recv_sem,
                                        device_id=to, device_id_type=MESH)


def _all_gather(bufs):
    n = len(bufs) - 1

    def body(*refs):
        outs, sm_out = refs[n + 1:2 * n + 1], refs[2 * n + 1]
        send_sems, recv_sems, sm_send, sm_recv = refs[2 * n + 2:]
        x, y, c, chips = _place()
        me = 2 * x + y
        sib = (x, y, 1 - c)
        sends = []
        for jj, (px, py) in enumerate(chips):
            sends.append(_remote(sm_out.at[me], sm_out.at[me], sm_send.at[jj], sm_recv.at[jj], (px, py, c)))
        for a in range(n):
            h = outs[a].shape[1] // 2
            blk = outs[a].at[me, pl.ds(c * h, h)]
            for jj, (px, py) in enumerate(chips):
                k = 6 * a + jj
                sends.append(_remote(blk, blk, send_sems.at[k], recv_sems.at[k], (px, py, c)))
        for cp in sends:
            cp.start()
        passed = []
        for a in range(n):
            h = outs[a].shape[1] // 2
            mine = pl.ds(c * h, h)
            for jj, (px, py) in enumerate(chips):
                blk = outs[a].at[2 * px + py, mine]
                _remote(blk, blk, send_sems.at[6 * a + jj], recv_sems.at[6 * a + jj], (px, py, c)).wait_recv()
                k = 6 * a + 3 + jj
                fwd = _remote(blk, blk, send_sems.at[k], recv_sems.at[k], sib)
                fwd.start()
                passed.append(fwd)
        for a in range(n):
            h = outs[a].shape[1] // 2
            theirs = pl.ds((1 - c) * h, h)
            for jj, (px, py) in enumerate(chips):
                blk = outs[a].at[2 * px + py, theirs]
                k = 6 * a + 3 + jj
                _remote(blk, blk, send_sems.at[k], recv_sems.at[k], sib).wait_recv()
        for jj, (px, py) in enumerate(chips):
            blk = sm_out.at[2 * px + py]
            _remote(blk, blk, sm_send.at[jj], sm_recv.at[jj], (px, py, c)).wait_recv()
        for cp in sends + passed:
            cp.wait_send()

    return _pallas(
        body, name="all_gather_weights", out_shape=[_sds(b.shape, b.dtype) for b in bufs],
        in_specs=[ANY] * (n + 1), out_specs=[ANY] * (n + 1), input_output_aliases={a: a for a in range(n + 1)},
        scratch_shapes=[pltpu.SemaphoreType.DMA((6 * n,)), pltpu.SemaphoreType.DMA((6 * n,)),
                        pltpu.SemaphoreType.DMA((3,)), pltpu.SemaphoreType.DMA((3,))],
    )(*bufs)


def _pair_exchange(gs):
    n = len(gs)

    def body(*refs):
        ins, outs = refs[:n], refs[n:2 * n]
        send_sems, recv_sems = refs[2 * n:]
        x, y, c, _ = _place()
        cps = []
        for a in range(n):
            h = ins[a].shape[1] // 2
            cps.append(_remote(ins[a].at[:, pl.ds((1 - c) * h, h)], outs[a], send_sems.at[a], recv_sems.at[a], (x, y, 1 - c)))
        for cp in cps:
            cp.start()
        for cp in cps:
            cp.wait()

    return _pallas(body, name="grad_pair_exchange", in_specs=[ANY] * n, out_specs=[ANY] * n,
                   out_shape=[_sds((4, g.shape[1] // 2, g.shape[2]), g.dtype) for g in gs],
                   scratch_shapes=[pltpu.SemaphoreType.DMA((n,)), pltpu.SemaphoreType.DMA((n,))])(*gs)


def _pair_add(g, other, c_arr):
    _, R, C = g.shape
    h = R // 2

    def body(c_ref, g_ref, o_ref, out_ref):
        out_ref[...] = (g_ref[...].astype(F32) + o_ref[...].astype(F32)).astype(BF16)

    grid_spec = pltpu.PrefetchScalarGridSpec(
        num_scalar_prefetch=1, grid=(4,),
        in_specs=[pl.BlockSpec((1, h, C), lambda k, c_ref: (k, c_ref[0], 0)), pl.BlockSpec((1, h, C), lambda k, c_ref: (k, 0, 0))],
        out_specs=pl.BlockSpec((1, h, C), lambda k, c_ref: (k, 0, 0)))
    return _pallas(body, name="grad_pair_add", grid_spec=grid_spec, out_shape=_sds((4, h, C), BF16),
                   compiler_params=_cp(("parallel",)))(c_arr, g, other)


def _chip_exchange(ss):
    n = len(ss)

    def body(*refs):
        ins, outs = refs[:n], refs[n:2 * n]
        send_sems, recv_sems = refs[2 * n:]
        x, y, c, chips = _place()
        cps = []
        for a in range(n):
            for jj, (px, py) in enumerate(chips):
                k = 3 * a + jj
                cps.append(_remote(ins[a].at[2 * px + py], outs[a].at[jj], send_sems.at[k], recv_sems.at[k], (px, py, c)))
        for cp in cps:
            cp.start()
        for cp in cps:
            cp.wait()

    return _pallas(body, name="grad_chip_exchange", in_specs=[ANY] * n, out_specs=[ANY] * n,
                   out_shape=[_sds((3,) + s.shape[1:], s.dtype) for s in ss],
                   scratch_shapes=[pltpu.SemaphoreType.DMA((3 * n,)), pltpu.SemaphoreType.DMA((3 * n,))])(*ss)


def _chip_sum(s, r, where, dest, l, L):
    _, h, C = s.shape
    tr = h // 2

    def body(k_ref, s_ref, r_ref, *rest):
        out_ref = rest[-1]
        acc = s_ref[0].astype(F32)
        for jj in range(3):
            acc = acc + r_ref[jj].astype(F32)
        out_ref[...] = acc

    in_specs = [pl.BlockSpec((1, tr, C), lambda i, k_ref: (k_ref[0], i, 0)), pl.BlockSpec((3, tr, C), lambda i, k_ref: (0, i, 0))]
    args = [where, s, r]
    alias = {}
    if dest is not None:
        in_specs.append(ANY)
        args.append(dest)
        alias = {3: 0}
    grid_spec = pltpu.PrefetchScalarGridSpec(
        num_scalar_prefetch=1, grid=(2,), in_specs=in_specs,
        out_specs=pl.BlockSpec((None, tr, C), lambda i, k_ref: (l, 2 * k_ref[1] + i, 0)))
    return _pallas(body, name="grad_chip_sum", grid_spec=grid_spec, out_shape=_sds((L, 2 * h, C), F32),
                   input_output_aliases=alias, compiler_params=_cp(("arbitrary",)))(*args)


def _pair_share(bufs, layout):
    n = len(layout)
    n_out = len(bufs)

    def body(*refs):
        outs = refs[n_out:2 * n_out]
        send_sems, recv_sems = refs[2 * n_out:]
        x, y, c, _ = _place()
        cps = []
        for a, (o, l) in enumerate(layout):
            h = outs[o].shape[1] // 2
            blk = outs[o].at[l, pl.ds(c * h, h)]
            cps.append(_remote(blk, blk, send_sems.at[a], recv_sems.at[a], (x, y, 1 - c)))
        for cp in cps:
            cp.start()
        for a, (o, l) in enumerate(layout):
            h = outs[o].shape[1] // 2
            blk = outs[o].at[l, pl.ds((1 - c) * h, h)]
            _remote(blk, blk, send_sems.at[a], recv_sems.at[a], (x, y, 1 - c)).wait_recv()
        for cp in cps:
            cp.wait_send()

    return _pallas(body, name="grad_pair_share", in_specs=[ANY] * n_out, out_specs=[ANY] * n_out,
                   out_shape=[_sds(b.shape, b.dtype) for b in bufs], input_output_aliases={o: o for o in range(n_out)},
                   scratch_shapes=[pltpu.SemaphoreType.DMA((n,)), pltpu.SemaphoreType.DMA((n,))])(*bufs)


def _small_all_reduce(packed):
    P, L = packed.shape

    def body(in_ref, out_ref, slots, send_sems, recv_sems):
        x, y, c, _ = _place()
        me = 4 * x + 2 * y + c
        slots[me] = in_ref[...]
        cps = []
        for r in range(1, 8):
            px = 1 - x if r & 4 else x
            py = 1 - y if r & 2 else y
            pc = 1 - c if r & 1 else c
            cps.append(_remote(in_ref, slots.at[me], send_sems.at[r - 1], recv_sems.at[r - 1], (px, py, pc)))
        for cp in cps:
            cp.start()
        for r in range(1, 8):
            px = 1 - x if r & 4 else x
            py = 1 - y if r & 2 else y
            pc = 1 - c if r & 1 else c
            blk = slots.at[4 * px + 2 * py + pc]
            _remote(blk, blk, send_sems.at[r - 1], recv_sems.at[r - 1], (px, py, pc)).wait_recv()
        for cp in cps:
            cp.wait_send()
        acc = slots[0]
        for k in range(1, 8):
            acc = acc + slots[k]
        out_ref[...] = acc

    vm = pl.BlockSpec(memory_space=pltpu.VMEM)
    return _pallas(body, name="small_all_reduce", in_specs=[vm], out_specs=vm, out_shape=_sds((P, L), F32),
                   scratch_shapes=[pltpu.VMEM((8, P, L), F32), pltpu.SemaphoreType.DMA((7,)), pltpu.SemaphoreType.DMA((7,))])(packed)


def _adamw_math(w, g, m, v):
    m = ADAM_B1 * m + (1.0 - ADAM_B1) * g
    v = ADAM_B2 * v + (1.0 - ADAM_B2) * (g * g)
    m_hat = m / (1.0 - ADAM_B1 ** ADAM_STEP)
    v_hat = v / (1.0 - ADAM_B2 ** ADAM_STEP)
    delta = -ADAM_LR * (m_hat / (jnp.sqrt(v_hat) + ADAM_EPS) + ADAM_WD * w)
    return delta, m, v


def _adamw(w, g, m, v):
    shape = w.shape
    C = shape[-1]
    rows = math.prod(shape[:-1])
    tr = next(t for t in (512, 352, 256, 128, 64, 32, 16, 8, rows) if rows % t == 0)
    w2, g2, m2, v2 = (a.reshape(rows, C) for a in (w, g, m, v))

    def body(w_ref, g_ref, m_ref, v_ref, d_ref, nm_ref, nv_ref):
        d, nm, nv = _adamw_math(w_ref[...], g_ref[...], m_ref[...], v_ref[...])
        d_ref[...] = d
        nm_ref[...] = nm
        nv_ref[...] = nv

    blk = pl.BlockSpec((tr, C), lambda i: (i, 0))
    outs = _pallas(body, name="adamw", grid=(rows // tr,), in_specs=[blk] * 4, out_specs=[blk] * 3,
                   out_shape=[_sds((rows, C), F32)] * 3, compiler_params=_cp(("parallel",)))(w2, g2, m2, v2)
    return tuple(o.reshape(shape) for o in outs)


WEIGHTS = ["ffn1_norm", "ffn1_w_gate", "ffn1_w_up", "ffn1_w_down", "mix_norm", "ffn2_norm", "ffn2_w_gate", "ffn2_w_up",
           "ffn2_w_down", "ev_w_in", "ev_b_f", "ev_conv_w", "ev_conv_b", "ev_conv_norm", "ev_q_norm", "ev_k_norm",
           "ev_w_out", "od_w_in", "od_conv_w", "od_w_out"]
BIG = ([("ffn1_w_gate", 0), ("ffn1_w_up", 0), ("ffn1_w_down", 0), ("ev_w_in", 0), ("ev_w_out", 0),
        ("ffn2_w_gate", 0), ("ffn2_w_up", 0), ("ffn2_w_down", 0)]
       + [("ffn1_w_gate", 1), ("ffn1_w_up", 1), ("ffn1_w_down", 1), ("od_w_in", 0), ("od_w_out", 0),
          ("ffn2_w_gate", 1), ("ffn2_w_up", 1), ("ffn2_w_down", 1)])
BIG_NAMES = ["ffn1_w_gate", "ffn1_w_up", "ffn1_w_down", "ffn2_w_gate", "ffn2_w_up", "ffn2_w_down",
             "ev_w_in", "ev_w_out", "od_w_in", "od_w_out"]
SMALL = [("ffn1_norm", 16), ("mix_norm", 16), ("ffn2_norm", 16), ("ev_b_f", 8), ("ev_conv_w", 128), ("ev_conv_b", 8),
         ("ev_conv_norm", 8), ("ev_q_norm", 8), ("ev_k_norm", 8), ("od_conv_w", 24)]


def _to_lanes(a, rows):
    flat = a.reshape(-1)
    return jnp.pad(flat, (0, rows * 128 - flat.shape[0])).reshape(rows, 128)


def kernel(x, ffn1_norm, ffn1_w_gate, ffn1_w_up, ffn1_w_down, mix_norm, ffn2_norm, ffn2_w_gate, ffn2_w_up, ffn2_w_down, ev_w_in, ev_b_f, ev_conv_w, ev_conv_b, ev_conv_norm, ev_q_norm, ev_k_norm, ev_w_out, od_w_in, od_conv_w, od_w_out, loss_target, m_ffn1_norm, m_ffn1_w_gate, m_ffn1_w_up, m_ffn1_w_down, m_mix_norm, m_ffn2_norm, m_ffn2_w_gate, m_ffn2_w_up, m_ffn2_w_down, m_ev_w_in, m_ev_b_f, m_ev_conv_w, m_ev_conv_b, m_ev_conv_norm, m_ev_q_norm, m_ev_k_norm, m_ev_w_out, m_od_w_in, m_od_conv_w, m_od_w_out, v_ffn1_norm, v_ffn1_w_gate, v_ffn1_w_up, v_ffn1_w_down, v_mix_norm, v_ffn2_norm, v_ffn2_w_gate, v_ffn2_w_up, v_ffn2_w_down, v_ev_w_in, v_ev_b_f, v_ev_conv_w, v_ev_conv_b, v_ev_conv_norm, v_ev_q_norm, v_ev_k_norm, v_ev_w_out, v_od_w_in, v_od_conv_w, v_od_w_out):
    P = dict(ffn1_norm=ffn1_norm, ffn1_w_gate=ffn1_w_gate, ffn1_w_up=ffn1_w_up, ffn1_w_down=ffn1_w_down, mix_norm=mix_norm,
             ffn2_norm=ffn2_norm, ffn2_w_gate=ffn2_w_gate, ffn2_w_up=ffn2_w_up, ffn2_w_down=ffn2_w_down, ev_w_in=ev_w_in,
             ev_b_f=ev_b_f, ev_conv_w=ev_conv_w, ev_conv_b=ev_conv_b, ev_conv_norm=ev_conv_norm, ev_q_norm=ev_q_norm,
             ev_k_norm=ev_k_norm, ev_w_out=ev_w_out, od_w_in=od_w_in, od_conv_w=od_conv_w, od_w_out=od_w_out)
    M = dict(zip(WEIGHTS, [m_ffn1_norm, m_ffn1_w_gate, m_ffn1_w_up, m_ffn1_w_down, m_mix_norm, m_ffn2_norm, m_ffn2_w_gate,
                           m_ffn2_w_up, m_ffn2_w_down, m_ev_w_in, m_ev_b_f, m_ev_conv_w, m_ev_conv_b, m_ev_conv_norm,
                           m_ev_q_norm, m_ev_k_norm, m_ev_w_out, m_od_w_in, m_od_conv_w, m_od_w_out]))
    V = dict(zip(WEIGHTS, [v_ffn1_norm, v_ffn1_w_gate, v_ffn1_w_up, v_ffn1_w_down, v_mix_norm, v_ffn2_norm, v_ffn2_w_gate,
                           v_ffn2_w_up, v_ffn2_w_down, v_ev_w_in, v_ev_b_f, v_ev_conv_w, v_ev_conv_b, v_ev_conv_norm,
                           v_ev_q_norm, v_ev_k_norm, v_ev_w_out, v_od_w_in, v_od_conv_w, v_od_w_out]))
    S, D = x.shape[1], x.shape[2]
    chip = 2 * lax.axis_index("x") + lax.axis_index("y")
    core = lax.axis_index("c")

    def own_slot(shard):
        return lax.dynamic_update_slice(jnp.zeros((4,) + shard.shape, shard.dtype), shard[None], (chip, 0, 0))

    taps = jnp.concatenate([_to_lanes(_pad_rows(ev_conv_w[0], 32), 32), _to_lanes(_pad_rows(od_conv_w[0], 8), 16)], axis=0)
    *full, taps_all = _all_gather([own_slot(P[name][l].astype(BF16)) for name, l in BIG] + [own_slot(taps)])
    G = dict(zip(BIG, full))
    cols = lambda a: a.transpose(1, 0, 2).reshape(a.shape[1], 4 * a.shape[2])
    W = {k: P[k] for k in ("ffn1_norm", "mix_norm", "ffn2_norm", "ev_b_f", "ev_q_norm", "ev_k_norm")}
    W["ev_conv_b"], W["ev_conv_norm"] = ev_conv_b, ev_conv_norm
    for tag in ("ffn1", "ffn2"):
        for kind in ("_w_gate", "_w_up", "_w_down"):
            W[tag + kind] = [G[(tag + kind, 0)], G[(tag + kind, 1)]]
    W["ev_w_in"] = cols(G[("ev_w_in", 0)])
    W["od_w_in"] = cols(G[("od_w_in", 0)])
    W["ev_w_out"] = G[("ev_w_out", 0)].reshape(4 * ev_w_out.shape[1], D)
    W["od_w_out"] = G[("od_w_out", 0)].reshape(4 * od_w_out.shape[1], D)
    W["ev_conv_w"] = cols(taps_all[:, :32].reshape(4, 32, 128))[:CONV_A_WIDTH]
    W["od_conv_w"] = cols(taps_all[:, 32:48].reshape(4, 8, 256))[:CONV_C_WIDTH]

    loss, grad_x, grads = _local_step(x[0], loss_target[0], W)

    rows = lambda a: a.reshape(4, a.shape[0] // 4, a.shape[1])
    colsh = lambda a: a.reshape(a.shape[0], 4, a.shape[1] // 4).transpose(1, 0, 2)
    grads[("ev_w_in", 0)] = colsh(grads[("ev_w_in", 0)])
    grads[("ev_w_out", 0)] = rows(grads[("ev_w_out", 0)])
    grads[("od_w_out", 0)] = rows(grads[("od_w_out", 0)])
    order = list(reversed(BIG))
    gs = [grads[k] for k in order]
    c_arr = core.reshape(1).astype(jnp.int32)
    where = jnp.stack([chip, core]).astype(jnp.int32)
    others = _pair_exchange(gs)
    sums = [_pair_add(g, o, c_arr) for g, o in zip(gs, others)]
    recvd = _chip_exchange(sums)
    stacked = {}
    for (name, l), s, r in zip(order, sums, recvd):
        stacked[name] = _chip_sum(s, r, where, stacked.get(name), l, P[name].shape[0])
    layout = [(BIG_NAMES.index(name), l) for name, l in order]
    big_grads = dict(zip(BIG_NAMES, _pair_share([stacked[name] for name in BIG_NAMES], layout)))

    def small_grad(name):
        if name.endswith("_norm") and name[:3] in ("ffn", "mix"):
            return jnp.concatenate([grads[(name, 0)], grads[(name, 1)]], axis=0)
        return grads[(name, 0)]

    packed = jnp.concatenate([_to_lanes(small_grad(name), r) for name, r in SMALL], axis=0)
    total = _small_all_reduce(packed)
    small_grads, at = {}, 0
    for name, r in SMALL:
        part = total[at:at + r].reshape(-1)
        at += r
        if name == "ev_conv_w":
            full_g = part[:CONV_A_WIDTH * D_CONV].reshape(CONV_A_WIDTH, D_CONV)
            small_grads[name] = lax.dynamic_slice_in_dim(full_g, chip * (D_CONV // 4), D_CONV // 4, axis=1)[None]
        elif name == "od_conv_w":
            full_g = part[:CONV_C_WIDTH * D].reshape(CONV_C_WIDTH, D)
            small_grads[name] = lax.dynamic_slice_in_dim(full_g, chip * (D // 4), D // 4, axis=1)[None]
        else:
            small_grads[name] = part[:math.prod(P[name].shape)].reshape(P[name].shape)

    grad_w, delta_w, new_m, new_v = [], [], [], []
    for name in WEIGHTS:
        g = big_grads[name] if name in big_grads else small_grads[name]
        d, nm, nv = _adamw(P[name], g, M[name], V[name])
        grad_w.append(g)
        delta_w.append(d)
        new_m.append(nm)
        new_v.append(nv)
    loss_all = lax.psum(loss[0, 0], ("x", "y", "c"))
    return (loss_all, grad_x[None], *grad_w, *delta_w, *new_m, *new_v)
```

```python
import functools
import math

import jax
import jax.numpy as jnp
from jax import lax
from jax.experimental import pallas as pl
from jax.experimental.pallas import tpu as pltpu

F32, BF16 = jnp.float32, jnp.bfloat16
EPS = 1e-6
FFN_RES = 0.5
N_HEADS, HEAD_DIM = 8, 64
D_CONV = 512
D_ATTN = N_HEADS * HEAD_DIM
CONV_A_WIDTH, CONV_C_WIDTH = 31, 3
ADAM_LR, ADAM_B1, ADAM_B2, ADAM_EPS, ADAM_WD, ADAM_STEP = 0.001, 0.9, 0.999, 1e-08, 0.01, 10
MESH = pl.DeviceIdType.MESH
ANY = pl.BlockSpec(memory_space=pl.ANY)

TOK_TILE = 512
ATT_TILE = 512
HALO_A, HALO_C = 32, 16
SCAN_BLK = 256
MIB = 2 ** 20


def _pallas(body, **kw):
    return pl.pallas_call(body, **kw)


def _cp(sem=None, vmem_mib=48):
    return pltpu.CompilerParams(dimension_semantics=sem, vmem_limit_bytes=vmem_mib * MIB)


def _dot(a, b):
    return jnp.dot(a, b, preferred_element_type=F32)


def _dot_nt(a, b):
    return lax.dot_general(a, b, (((1,), (1,)), ((), ())), preferred_element_type=F32)


def _dot_tn(a, b):
    return lax.dot_general(a, b, (((0,), (0,)), ((), ())), preferred_element_type=F32)


def _sds(shape, dtype):
    return jax.ShapeDtypeStruct(shape, dtype)


def _rms(x):
    return lax.rsqrt(jnp.mean(x * x, axis=-1, keepdims=True) + EPS)


def _rms_bwd(dy, x, g):
    r = _rms(x)
    xh = x * r
    dxh = dy * g
    dx = r * (dxh - xh * jnp.mean(dxh * xh, axis=-1, keepdims=True))
    return dx, xh


def _silu_grad(z):
    s = jax.nn.sigmoid(z)
    return s * (1.0 + z * (1.0 - s))


def _ffn_fwd(x, g, wg, wu, wd):
    S, D = x.shape
    nc, _, Fs = wg.shape
    tm = TOK_TILE

    def body(x_ref, g_ref, wg_ref, wu_ref, wd_ref, out_ref, xn_ref, G_ref, U_ref, acc_ref):
        j = pl.program_id(1)

        @pl.when(j == 0)
        def _():
            xv = x_ref[...]
            xn_ref[...] = (xv * _rms(xv) * g_ref[...]).astype(BF16)
            acc_ref[...] = jnp.zeros_like(acc_ref)

        xn = xn_ref[...]
        G = _dot(xn, wg_ref[0])
        U = _dot(xn, wu_ref[0])
        G_ref[0] = G.astype(BF16)
        U_ref[0] = U.astype(BF16)
        H = (G * jax.nn.sigmoid(G) * U).astype(BF16)
        acc_ref[...] += _dot(H, wd_ref[0])

        @pl.when(j == nc - 1)
        def _():
            out_ref[...] = x_ref[...] + FFN_RES * acc_ref[...]

    row = pl.BlockSpec((tm, D), lambda i, j: (i, 0))
    return _pallas(
        body, name="ffn_fwd", grid=(S // tm, nc),
        in_specs=[row, pl.BlockSpec((1, D), lambda i, j: (0, 0)),
                  pl.BlockSpec((1, D, Fs), lambda i, j: (j, 0, 0)), pl.BlockSpec((1, D, Fs), lambda i, j: (j, 0, 0)),
                  pl.BlockSpec((1, Fs, D), lambda i, j: (j, 0, 0))],
        out_specs=[row, row, pl.BlockSpec((1, tm, Fs), lambda i, j: (j, i, 0)),
                   pl.BlockSpec((1, tm, Fs), lambda i, j: (j, i, 0))],
        out_shape=[_sds((S, D), F32), _sds((S, D), BF16), _sds((nc, S, Fs), BF16), _sds((nc, S, Fs), BF16)],
        scratch_shapes=[pltpu.VMEM((tm, D), F32)],
        compiler_params=_cp(("parallel", "arbitrary")),
    )(x, g, wg, wu, wd)


def _ffn_bwd_w(dout, xn, G, U, wd):
    S, D = dout.shape
    nc, _, Fs = G.shape
    tm = TOK_TILE
    nt = S // tm

    def body(do_ref, xn_ref, G_ref, U_ref, wd_ref, dwg_ref, dwu_ref, dwd_ref, dG_ref, dU_ref, ag, au, ad):
        i = pl.program_id(1)

        @pl.when(i == 0)
        def _():
            ag[...] = jnp.zeros_like(ag)
            au[...] = jnp.zeros_like(au)
            ad[...] = jnp.zeros_like(ad)

        do = (FFN_RES * do_ref[...]).astype(BF16)
        Gv = G_ref[0].astype(F32)
        Uv = U_ref[0].astype(F32)
        dH = _dot_nt(do, wd_ref[0])
        sg = jax.nn.sigmoid(Gv)
        act = Gv * sg
        H = (act * Uv).astype(BF16)
        dU = (dH * act).astype(BF16)
        dG = (dH * Uv * (sg * (1.0 + Gv * (1.0 - sg)))).astype(BF16)
        dG_ref[0] = dG
        dU_ref[0] = dU
        xnv = xn_ref[...]
        ag[...] += _dot_tn(xnv, dG)
        au[...] += _dot_tn(xnv, dU)
        ad[...] += _dot_tn(H, do)

        @pl.when(i == nt - 1)
        def _():
            dwg_ref[0] = ag[...].astype(BF16)
            dwu_ref[0] = au[...].astype(BF16)
            dwd_ref[0] = ad[...].astype(BF16)

    row = pl.BlockSpec((tm, D), lambda j, i: (i, 0))
    hid = pl.BlockSpec((1, tm, Fs), lambda j, i: (j, i, 0))
    wcol = pl.BlockSpec((1, D, Fs), lambda j, i: (j, 0, 0))
    wrow = pl.BlockSpec((1, Fs, D), lambda j, i: (j, 0, 0))
    return _pallas(
        body, name="ffn_bwd_w", grid=(nc, nt),
        in_specs=[row, row, hid, hid, wrow],
        out_specs=[wcol, wcol, wrow, hid, hid],
        out_shape=[_sds((nc, D, Fs), BF16), _sds((nc, D, Fs), BF16), _sds((nc, Fs, D), BF16),
                   _sds((nc, S, Fs), BF16), _sds((nc, S, Fs), BF16)],
        scratch_shapes=[pltpu.VMEM((D, Fs), F32), pltpu.VMEM((D, Fs), F32), pltpu.VMEM((Fs, D), F32)],
        compiler_params=_cp(("parallel", "arbitrary"), 56),
    )(dout, xn, G, U, wd)


def _norm_in_bwd(dzs, ws, x, g, dres):
    S, D = x.shape
    nc = dzs[0].shape[0]
    n = len(dzs)
    tm = TOK_TILE

    def body(*refs):
        dz_refs, w_refs = refs[:n], refs[n:2 * n]
        x_ref, g_ref, dres_ref, dx_ref, dg_ref, acc_ref = refs[2 * n:]
        i, j = pl.program_id(0), pl.program_id(1)

        @pl.when(j == 0)
        def _():
            acc_ref[...] = jnp.zeros_like(acc_ref)

        @pl.when((i == 0) & (j == 0))
        def _():
            dg_ref[...] = jnp.zeros_like(dg_ref)

        for dz_ref, w_ref in zip(dz_refs, w_refs):
            acc_ref[...] += _dot_nt(dz_ref[0], w_ref[0])

        @pl.when(j == nc - 1)
        def _():
            dxn = acc_ref[...]
            dx, xh = _rms_bwd(dxn, x_ref[...], g_ref[...])
            dx_ref[...] = dx + dres_ref[...]
            dg_ref[...] += jnp.sum(dxn * xh, axis=0, keepdims=True)

    row = pl.BlockSpec((tm, D), lambda i, j: (i, 0))
    one = pl.BlockSpec((1, D), lambda i, j: (0, 0))
    in_specs = [pl.BlockSpec((1, tm, dz.shape[2]), lambda i, j: (j, i, 0)) for dz in dzs]
    in_specs += [pl.BlockSpec((1, D, w.shape[2]), lambda i, j: (j, 0, 0)) for w in ws]
    return _pallas(
        body, name="norm_in_bwd", grid=(S // tm, nc),
        in_specs=in_specs + [row, one, row], out_specs=[row, one],
        out_shape=[_sds((S, D), F32), _sds((1, D), F32)],
        scratch_shapes=[pltpu.VMEM((tm, D), F32)],
        compiler_params=_cp(("arbitrary", "arbitrary")),
    )(*dzs, *ws, x, g, dres)


def _norm_proj(x, g, w, w2=None):
    S, D = x.shape
    N = w.shape[1]
    tm = TOK_TILE

    def body(*refs):
        if w2 is None:
            x_ref, g_ref, w_ref, h_ref, z_ref = refs
        else:
            x_ref, g_ref, w_ref, w2_ref, h_ref, z_ref, z2_ref = refs
        xv = x_ref[...]
        h = (xv * _rms(xv) * g_ref[...]).astype(BF16)
        h_ref[...] = h
        z_ref[...] = _dot(h, w_ref[...]).astype(BF16)
        if w2 is not None:
            z2_ref[...] = _dot(h, w2_ref[...])

    row = pl.BlockSpec((tm, D), lambda i: (i, 0))
    in_specs = [row, pl.BlockSpec((1, D), lambda i: (0, 0)), pl.BlockSpec((D, N), lambda i: (0, 0))]
    out_specs = [row, pl.BlockSpec((tm, N), lambda i: (i, 0))]
    out_shape = [_sds((S, D), BF16), _sds((S, N), BF16)]
    args = [x, g, w]
    if w2 is not None:
        N2 = w2.shape[1]
        in_specs.append(pl.BlockSpec((D, N2), lambda i: (0, 0)))
        out_specs.append(pl.BlockSpec((tm, N2), lambda i: (i, 0)))
        out_shape.append(_sds((S, N2), F32))
        args.append(w2)
    return _pallas(body, name="norm_proj", grid=(S // tm,), in_specs=in_specs, out_specs=out_specs,
                   out_shape=out_shape, compiler_params=_cp(("parallel",)))(*args)


def _proj_res(acts, ws, res):
    S, D = res.shape
    n = len(acts)
    tm = TOK_TILE

    def body(*refs):
        a_refs, w_refs = refs[:n], refs[n:2 * n]
        res_ref, out_ref = refs[2 * n:]
        acc = res_ref[...]
        for a_ref, w_ref in zip(a_refs, w_refs):
            acc = acc + _dot(a_ref[...], w_ref[...])
        out_ref[...] = acc

    row = pl.BlockSpec((tm, D), lambda i: (i, 0))
    in_specs = [pl.BlockSpec((tm, a.shape[1]), lambda i: (i, 0)) for a in acts]
    in_specs += [pl.BlockSpec(w.shape, lambda i: (0, 0)) for w in ws]
    return _pallas(body, name="proj_res", grid=(S // tm,), in_specs=in_specs + [row], out_specs=row,
                   out_shape=_sds((S, D), F32), compiler_params=_cp(("parallel",)))(*acts, *ws, res)


def _matmul_nt(a, w):
    S, K = a.shape
    M = w.shape[0]
    tm = TOK_TILE

    def body(a_ref, w_ref, o_ref):
        o_ref[...] = _dot_nt(a_ref[...].astype(BF16), w_ref[...])

    return _pallas(body, name="matmul_nt", grid=(S // tm,),
                   in_specs=[pl.BlockSpec((tm, K), lambda i: (i, 0)), pl.BlockSpec((M, K), lambda i: (0, 0))],
                   out_specs=pl.BlockSpec((tm, M), lambda i: (i, 0)), out_shape=_sds((S, M), F32),
                   compiler_params=_cp(("parallel",)))(a, w)


def _matmul_tn(a, b, tn):
    S, M = a.shape
    N = b.shape[1]
    tm = TOK_TILE
    nt = S // tm

    def body(a_ref, b_ref, o_ref, acc_ref):
        i = pl.program_id(1)

        @pl.when(i == 0)
        def _():
            acc_ref[...] = jnp.zeros_like(acc_ref)

        acc_ref[...] += _dot_tn(a_ref[...].astype(BF16), b_ref[...].astype(BF16))

        @pl.when(i == nt - 1)
        def _():
            o_ref[0] = acc_ref[...].astype(BF16)

    return _pallas(body, name="matmul_tn", grid=(N // tn, nt),
                   in_specs=[pl.BlockSpec((tm, M), lambda j, i: (i, 0)), pl.BlockSpec((tm, tn), lambda j, i: (i, j))],
                   out_specs=pl.BlockSpec((1, M, tn), lambda j, i: (j, 0, 0)), out_shape=_sds((N // tn, M, tn), BF16),
                   scratch_shapes=[pltpu.VMEM((M, tn), F32)],
                   compiler_params=_cp(("parallel", "arbitrary")))(a, b)


def _conv_a_fwd(z, cw, cb, cn):
    S = z.shape[0]
    C = D_CONV
    tm = TOK_TILE
    hb = tm // HALO_A

    def body(u_ref, gt_ref, up_ref, gp_ref, cw_ref, cb_ref, cn_ref, a_ref, a1_ref, win):
        i = pl.program_id(0)
        prev = up_ref[...].astype(F32) * jax.nn.sigmoid(gp_ref[...].astype(F32))
        win[pl.ds(0, HALO_A), :] = jnp.where(i == 0, 0.0, prev)
        win[pl.ds(HALO_A, tm), :] = u_ref[...].astype(F32) * jax.nn.sigmoid(gt_ref[...].astype(F32))
        acc = jnp.zeros((tm, C), F32)
        for k in range(CONV_A_WIDTH):
            acc = acc + cw_ref[k:k + 1, :] * win[pl.ds(HALO_A - (CONV_A_WIDTH - 1) + k, tm), :]
        a1 = acc + cb_ref[...]
        a1_ref[...] = a1
        a2 = a1 * _rms(a1) * cn_ref[...]
        a_ref[...] = (a2 * jax.nn.sigmoid(a2)).astype(BF16)

    cur = lambda c: pl.BlockSpec((tm, C), lambda i, c=c: (i, c))
    prv = lambda c: pl.BlockSpec((HALO_A, C), lambda i, c=c: (jnp.maximum(i * hb - 1, 0), c))
    vec = pl.BlockSpec((1, C), lambda i: (0, 0))
    return _pallas(body, name="conv_a_fwd", grid=(S // tm,),
                   in_specs=[cur(0), cur(1), prv(0), prv(1), pl.BlockSpec((32, C), lambda i: (0, 0)), vec, vec],
                   out_specs=[pl.BlockSpec((tm, C), lambda i: (i, 0)), pl.BlockSpec((tm, C), lambda i: (i, 0))],
                   out_shape=[_sds((S, C), BF16), _sds((S, C), F32)],
                   scratch_shapes=[pltpu.VMEM((tm + HALO_A, C), F32)],
                   compiler_params=_cp(("parallel",)))(z, z, z, z, cw, cb, cn)


def _conv_a_bwd(da, a1, z, cw, cn):
    S = z.shape[0]
    C = D_CONV
    tm = TOK_TILE
    hb = tm // HALO_A
    nt = S // tm
    W = CONV_A_WIDTH

    def body(da_ref, a1_ref, dan_ref, a1n_ref, u_ref, gt_ref, up_ref, gp_ref, cw_ref, cn_ref,
             duz_ref, dcw_ref, dcb_ref, dcn_ref, win, dwin):
        i = pl.program_id(0)

        @pl.when(i == 0)
        def _():
            dcw_ref[...] = jnp.zeros_like(dcw_ref)
            dcb_ref[...] = jnp.zeros_like(dcb_ref)
            dcn_ref[...] = jnp.zeros_like(dcn_ref)

        cnv = cn_ref[...]

        def da1_of(dav, a1v):
            a2 = a1v * _rms(a1v) * cnv
            da2 = dav * _silu_grad(a2)
            dx, xh = _rms_bwd(da2, a1v, cnv)
            return dx, da2 * xh

        da1, dcn_t = da1_of(da_ref[...], a1_ref[...])
        da1n, _ = da1_of(dan_ref[...], a1n_ref[...])
        dwin[pl.ds(0, tm), :] = da1
        dwin[pl.ds(tm, HALO_A), :] = jnp.where(i == nt - 1, 0.0, da1n)
        dcb_ref[...] += jnp.sum(da1, axis=0, keepdims=True)
        dcn_ref[...] += jnp.sum(dcn_t, axis=0, keepdims=True)

        u = u_ref[...].astype(F32)
        sg = jax.nn.sigmoid(gt_ref[...].astype(F32))
        prev = up_ref[...].astype(F32) * jax.nn.sigmoid(gp_ref[...].astype(F32))
        win[pl.ds(0, HALO_A), :] = jnp.where(i == 0, 0.0, prev)
        win[pl.ds(HALO_A, tm), :] = u * sg

        da0 = jnp.zeros((tm, C), F32)
        for k in range(W):
            da0 = da0 + cw_ref[k:k + 1, :] * dwin[pl.ds(W - 1 - k, tm), :]
            dcw_ref[k:k + 1, :] += jnp.sum(da1 * win[pl.ds(HALO_A - (W - 1) + k, tm), :], axis=0, keepdims=True)
        duz_ref[:, 0:C] = (da0 * sg).astype(BF16)
        duz_ref[:, C:2 * C] = (da0 * u * sg * (1.0 - sg)).astype(BF16)

    cur = lambda c: pl.BlockSpec((tm, C), lambda i, c=c: (i, c))
    prv = lambda c: pl.BlockSpec((HALO_A, C), lambda i, c=c: (jnp.maximum(i * hb - 1, 0), c))
    nxt = pl.BlockSpec((HALO_A, C), lambda i: (jnp.minimum((i + 1) * hb, S // HALO_A - 1), 0))
    vec = pl.BlockSpec((1, C), lambda i: (0, 0))
    return _pallas(body, name="conv_a_bwd", grid=(nt,),
                   in_specs=[cur(0), cur(0), nxt, nxt, cur(0), cur(1), prv(0), prv(1),
                             pl.BlockSpec((32, C), lambda i: (0, 0)), vec],
                   out_specs=[pl.BlockSpec((tm, 2 * C), lambda i: (i, 0)), pl.BlockSpec((32, C), lambda i: (0, 0)), vec, vec],
                   out_shape=[_sds((S, 2 * C), BF16), _sds((32, C), F32), _sds((1, C), F32), _sds((1, C), F32)],
                   scratch_shapes=[pltpu.VMEM((tm + HALO_A, C), F32), pltpu.VMEM((tm + HALO_A, C), F32)],
                   compiler_params=_cp(("arbitrary",)))(da, a1, da, a1, z, z, z, z, cw, cn)


def _forget_scan(fl, bf):
    S, L = fl.shape
    B = SCAN_BLK

    def body(fl_ref, bf_ref, flb_ref, F_ref):
        tri = (lax.broadcasted_iota(jnp.int32, (B, B), 0) >= lax.broadcasted_iota(jnp.int32, (B, B), 1)).astype(F32)

        def step(c, carry):
            rows = pl.ds(pl.multiple_of(c * B, B), B)
            v = fl_ref[rows, :] + bf_ref[...]
            flb_ref[rows, :] = v
            lf = jnp.minimum(v, 0.0) - jnp.log1p(jnp.exp(-jnp.abs(v)))
            cs = jnp.dot(tri, lf, precision=lax.Precision.HIGHEST, preferred_element_type=F32) + carry
            F_ref[rows, :] = cs
            return cs[B - 1:B, :]

        lax.fori_loop(0, S // B, step, jnp.zeros((1, L), F32))

    return _pallas(body, name="forget_scan", out_shape=[_sds((S, L), F32), _sds((S, L), F32)],
                   compiler_params=_cp())(fl, bf)


def _forget_scan_bwd(dF, flb):
    S, L = dF.shape
    B = SCAN_BLK
    nb = S // B

    def body(dF_ref, flb_ref, dfl_ref, db_ref):
        tri = (lax.broadcasted_iota(jnp.int32, (B, B), 0) <= lax.broadcasted_iota(jnp.int32, (B, B), 1)).astype(F32)

        def step(t, carry):
            carry_cs, db = carry
            rows = pl.ds(pl.multiple_of((nb - 1 - t) * B, B), B)
            cs = jnp.dot(tri, dF_ref[rows, :], precision=lax.Precision.HIGHEST, preferred_element_type=F32) + carry_cs
            dfl = cs * jax.nn.sigmoid(-flb_ref[rows, :])
            dfl_ref[rows, :] = dfl
            return cs[0:1, :], db + jnp.sum(dfl, axis=0, keepdims=True)

        _, db = lax.fori_loop(0, nb, step, (jnp.zeros((1, L), F32), jnp.zeros((1, L), F32)))
        db_ref[...] = db

    return _pallas(body, name="forget_scan_bwd", out_shape=[_sds((S, L), F32), _sds((1, L), F32)],
                   compiler_params=_cp())(dF, flb)


def _qk_norm(q, k, qw, kw):
    H, S, Dh = q.shape
    tq = ATT_TILE
    scale = 1.0 / math.sqrt(Dh)

    def body(q_ref, k_ref, qw_ref, kw_ref, qn_ref, kn_ref):
        qv = q_ref[0].astype(F32)
        kv = k_ref[0].astype(F32)
        qn_ref[0] = (qv * _rms(qv) * qw_ref[...] * scale).astype(BF16)
        kn_ref[0] = (kv * _rms(kv) * kw_ref[...]).astype(BF16)

    blk = pl.BlockSpec((1, tq, Dh), lambda h, i: (h, i, 0))
    vec = pl.BlockSpec((1, Dh), lambda h, i: (0, 0))
    return _pallas(body, name="qk_norm", grid=(H, S // tq), in_specs=[blk, blk, vec, vec], out_specs=[blk, blk],
                   out_shape=[_sds((H, S, Dh), BF16), _sds((H, S, Dh), BF16)],
                   compiler_params=_cp(("parallel", "parallel")))(q, k, qw, kw)


def _qk_norm_bwd(dqs, dkn, q, k, qw, kw):
    H, S, Dh = q.shape
    tq = ATT_TILE
    scale = 1.0 / math.sqrt(Dh)

    def body(dqs_ref, dkn_ref, q_ref, k_ref, qw_ref, kw_ref, dq_ref, dk_ref, dqw_ref, dkw_ref):
        @pl.when((pl.program_id(0) == 0) & (pl.program_id(1) == 0))
        def _():
            dqw_ref[...] = jnp.zeros_like(dqw_ref)
            dkw_ref[...] = jnp.zeros_like(dkw_ref)

        dqn = dqs_ref[0] * scale
        dq, qh = _rms_bwd(dqn, q_ref[0].astype(F32), qw_ref[...])
        dq_ref[0] = dq.astype(BF16)
        dqw_ref[...] += jnp.sum(dqn * qh, axis=0, keepdims=True)
        dkv = dkn_ref[0]
        dk, kh = _rms_bwd(dkv, k_ref[0].astype(F32), kw_ref[...])
        dk_ref[0] = dk.astype(BF16)
        dkw_ref[...] += jnp.sum(dkv * kh, axis=0, keepdims=True)

    blk = pl.BlockSpec((1, tq, Dh), lambda h, i: (h, i, 0))
    vec = pl.BlockSpec((1, Dh), lambda h, i: (0, 0))
    return _pallas(body, name="qk_norm_bwd", grid=(H, S // tq), in_specs=[blk, blk, blk, blk, vec, vec],
                   out_specs=[blk, blk, vec, vec],
                   out_shape=[_sds((H, S, Dh), BF16), _sds((H, S, Dh), BF16), _sds((1, Dh), F32), _sds((1, Dh), F32)],
                   compiler_params=_cp(("arbitrary", "arbitrary")))(dqs, dkn, q, k, qw, kw)


NEG = -1e30


def _causal_mask(t):
    return lax.broadcasted_iota(jnp.int32, (t, t), 0) >= lax.broadcasted_iota(jnp.int32, (t, t), 1)


def _fox_fwd(qs, kn, v, fcol, frow):
    H, S, Dh = qs.shape
    t = ATT_TILE
    nq = S // t

    def body(q_ref, k_ref, v_ref, fc_ref, fr_ref, o_ref, lse_ref):
        i = pl.program_id(1)
        q = q_ref[0]
        fq = fc_ref[0]

        def tile(j, carry, diag):
            m, l, acc = carry
            rows = pl.ds(pl.multiple_of(j * t, t), t)
            s = _dot_nt(q, k_ref[0, rows, :]) + fq - fr_ref[0, j]
            if diag:
                s = jnp.where(_causal_mask(t), s, NEG)
            m_new = jnp.maximum(m, jnp.max(s, axis=-1, keepdims=True))
            p = jnp.exp(s - m_new)
            alpha = jnp.exp(m - m_new)
            l = alpha * l + jnp.sum(p, axis=-1, keepdims=True)
            acc = alpha * acc + _dot(p.astype(BF16), v_ref[0, rows, :])
            return m_new, l, acc

        init = (jnp.full((t, 1), NEG, F32), jnp.zeros((t, 1), F32), jnp.zeros((t, Dh), F32))
        carry = lax.fori_loop(0, i, lambda j, c: tile(j, c, False), init)
        m, l, acc = tile(i, carry, True)
        o_ref[0] = (acc / l).astype(BF16)
        lse_ref[0] = m + jnp.log(l)

    qblk = pl.BlockSpec((1, t, Dh), lambda h, i: (h, i, 0))
    full = pl.BlockSpec((1, S, Dh), lambda h, i: (h, 0, 0))
    col = pl.BlockSpec((1, t, 1), lambda h, i: (h, i, 0))
    return _pallas(body, name="fox_fwd", grid=(H, nq),
                   in_specs=[qblk, full, full, col, pl.BlockSpec((1, nq, 1, t), lambda h, i: (h, 0, 0, 0))],
                   out_specs=[qblk, col], out_shape=[_sds((H, S, Dh), BF16), _sds((H, S, 1), F32)],
                   compiler_params=_cp(("parallel", "parallel")))(qs, kn, v, fcol, frow)


def _fox_bwd(qs, kn, v, o, do, lse, fcol, frow):
    H, S, Dh = qs.shape
    t = ATT_TILE
    nq = S // t

    def body(q_ref, k_ref, v_ref, o_ref, do_ref, lse_ref, fc_ref, fr_ref, dq_ref, dk_ref, dv_ref, dfq_ref, dfk_ref):
        j = pl.program_id(1)

        @pl.when(j == 0)
        def _():
            dq_ref[...] = jnp.zeros_like(dq_ref)
            dfq_ref[...] = jnp.zeros_like(dfq_ref)

        k = k_ref[0]
        vv = v_ref[0]
        fk = fr_ref[0, 0]

        def tile(i, carry, diag):
            dk, dv, dfk = carry
            rows = pl.ds(pl.multiple_of(i * t, t), t)
            q = q_ref[0, rows, :]
            dov = do_ref[0, rows, :]
            delta = jnp.sum(dov.astype(F32) * o_ref[0, rows, :].astype(F32), axis=-1, keepdims=True)
            s = _dot_nt(q, k) + fc_ref[0, rows, :] - fk
            if diag:
                s = jnp.where(_causal_mask(t), s, NEG)
            p = jnp.exp(s - lse_ref[0, rows, :])
            dv = dv + _dot_tn(p.astype(BF16), dov)
            ds = p * (_dot_nt(dov, vv) - delta)
            dsb = ds.astype(BF16)
            dq_ref[0, rows, :] += _dot(dsb, k)
            dk = dk + _dot_tn(dsb, q)
            dfq_ref[0, rows, :] += jnp.sum(ds, axis=-1, keepdims=True)
            dfk = dfk + jnp.sum(ds, axis=0, keepdims=True)
            return dk, dv, dfk

        init = (jnp.zeros((t, Dh), F32), jnp.zeros((t, Dh), F32), jnp.zeros((1, t), F32))
        carry = tile(j, init, True)
        dk, dv, dfk = lax.fori_loop(j + 1, nq, lambda i, c: tile(i, c, False), carry)
        dk_ref[0] = dk
        dv_ref[0] = dv
        dfk_ref[0, 0] = dfk

    full = pl.BlockSpec((1, S, Dh), lambda h, j: (h, 0, 0))
    kblk = pl.BlockSpec((1, t, Dh), lambda h, j: (h, j, 0))
    colf = pl.BlockSpec((1, S, 1), lambda h, j: (h, 0, 0))
    rowb = pl.BlockSpec((1, 1, 1, t), lambda h, j: (h, j, 0, 0))
    return _pallas(body, name="fox_bwd", grid=(H, nq),
                   in_specs=[full, kblk, kblk, full, full, colf, colf, rowb],
                   out_specs=[full, kblk, kblk, colf, rowb],
                   out_shape=[_sds((H, S, Dh), F32), _sds((H, S, Dh), F32), _sds((H, S, Dh), F32),
                              _sds((H, S, 1), F32), _sds((H, nq, 1, t), F32)],
                   compiler_params=_cp(("parallel", "arbitrary"), 56))(qs, kn, v, o, do, lse, fcol, frow)


def _odd_mid_fwd(z, cw):
    S = z.shape[0]
    D = z.shape[1] // 3
    tm = TOK_TILE
    hb = tm // HALO_C
    W = CONV_C_WIDTH

    def body(gb_ref, gc_ref, hh_ref, gcp_ref, hhp_ref, cw_ref, y_ref, win):
        i = pl.program_id(0)
        prev = gcp_ref[...].astype(F32) * hhp_ref[...].astype(F32)
        win[pl.ds(0, HALO_C), :] = jnp.where(i == 0, 0.0, prev)
        win[pl.ds(HALO_C, tm), :] = gc_ref[...].astype(F32) * hh_ref[...].astype(F32)
        c1 = jnp.zeros((tm, D), F32)
        for k in range(W):
            c1 = c1 + cw_ref[k:k + 1, :] * win[pl.ds(HALO_C - (W - 1) + k, tm), :]
        y_ref[...] = (gb_ref[...].astype(F32) * c1).astype(BF16)

    cur = lambda c: pl.BlockSpec((tm, D), lambda i, c=c: (i, c))
    prv = lambda c: pl.BlockSpec((HALO_C, D), lambda i, c=c: (jnp.maximum(i * hb - 1, 0), c))
    return _pallas(body, name="odd_mid_fwd", grid=(S // tm,),
                   in_specs=[cur(0), cur(1), cur(2), prv(1), prv(2), pl.BlockSpec((8, D), lambda i: (0, 0))],
                   out_specs=pl.BlockSpec((tm, D), lambda i: (i, 0)), out_shape=_sds((S, D), BF16),
                   scratch_shapes=[pltpu.VMEM((tm + HALO_C, D), F32)],
                   compiler_params=_cp(("parallel",)))(z, z, z, z, z, cw)


def _odd_mid_bwd(dy, z, cw):
    S = z.shape[0]
    D = z.shape[1] // 3
    tm = TOK_TILE
    hb = tm // HALO_C
    nt = S // tm
    W = CONV_C_WIDTH

    def body(dy_ref, dyn_ref, gb_ref, gbn_ref, gc_ref, hh_ref, gcp_ref, hhp_ref, cw_ref, dz_ref, dcw_ref, win, dwin):
        i = pl.program_id(0)

        @pl.when(i == 0)
        def _():
            dcw_ref[...] = jnp.zeros_like(dcw_ref)

        gc = gc_ref[...].astype(F32)
        hh = hh_ref[...].astype(F32)
        prev = gcp_ref[...].astype(F32) * hhp_ref[...].astype(F32)
        win[pl.ds(0, HALO_C), :] = jnp.where(i == 0, 0.0, prev)
        win[pl.ds(HALO_C, tm), :] = gc * hh
        dyv = dy_ref[...]
        dc1 = dyv * gb_ref[...].astype(F32)
        dwin[pl.ds(0, tm), :] = dc1
        dwin[pl.ds(tm, HALO_C), :] = jnp.where(i == nt - 1, 0.0, dyn_ref[...] * gbn_ref[...].astype(F32))
        c1 = jnp.zeros((tm, D), F32)
        dc0 = jnp.zeros((tm, D), F32)
        for k in range(W):
            tap = win[pl.ds(HALO_C - (W - 1) + k, tm), :]
            c1 = c1 + cw_ref[k:k + 1, :] * tap
            dc0 = dc0 + cw_ref[k:k + 1, :] * dwin[pl.ds(W - 1 - k, tm), :]
            dcw_ref[k:k + 1, :] += jnp.sum(dc1 * tap, axis=0, keepdims=True)
        dz_ref[:, 0:D] = (dyv * c1).astype(BF16)
        dz_ref[:, D:2 * D] = (dc0 * hh).astype(BF16)
        dz_ref[:, 2 * D:3 * D] = (dc0 * gc).astype(BF16)

    cur = lambda c: pl.BlockSpec((tm, D), lambda i, c=c: (i, c))
    prv = lambda c: pl.BlockSpec((HALO_C, D), lambda i, c=c: (jnp.maximum(i * hb - 1, 0), c))
    nxt = pl.BlockSpec((HALO_C, D), lambda i: (jnp.minimum((i + 1) * hb, S // HALO_C - 1), 0))
    return _pallas(body, name="odd_mid_bwd", grid=(nt,),
                   in_specs=[cur(0), nxt, cur(0), nxt, cur(1), cur(2), prv(1), prv(2), pl.BlockSpec((8, D), lambda i: (0, 0))],
                   out_specs=[pl.BlockSpec((tm, 3 * D), lambda i: (i, 0)), pl.BlockSpec((8, D), lambda i: (0, 0))],
                   out_shape=[_sds((S, 3 * D), BF16), _sds((8, D), F32)],
                   scratch_shapes=[pltpu.VMEM((tm + HALO_C, D), F32), pltpu.VMEM((tm + HALO_C, D), F32)],
                   compiler_params=_cp(("arbitrary",)))(dy, dy, z, z, z, z, z, z, cw)


def _loss_head(y, tgt):
    S, D = y.shape
    tm = TOK_TILE

    def body(y_ref, t_ref, dy_ref, l_ref):
        @pl.when(pl.program_id(0) == 0)
        def _():
            l_ref[...] = jnp.zeros_like(l_ref)

        e = y_ref[...] - t_ref[...]
        dy_ref[...] = e * (1.0 / D)
        l_ref[...] += jnp.sum(jnp.sum(e * e, axis=-1, keepdims=True), axis=0, keepdims=True) * (0.5 / D)

    row = pl.BlockSpec((tm, D), lambda i: (i, 0))
    return _pallas(body, name="loss_head", grid=(S // tm,), in_specs=[row, row],
                   out_specs=[row, pl.BlockSpec((1, 1), lambda i: (0, 0))],
                   out_shape=[_sds((S, D), F32), _sds((1, 1), F32)],
                   compiler_params=_cp(("arbitrary",)))(y, tgt)


def _heads(a):
    S = a.shape[0]
    return a.reshape(S, N_HEADS, HEAD_DIM).transpose(1, 0, 2)


def _unheads(a):
    return a.transpose(1, 0, 2).reshape(a.shape[1], D_ATTN)


def _pad_rows(a, rows):
    return jnp.pad(a, ((0, rows - a.shape[0]), (0, 0)))


def _local_step(x, tgt, W, need=lambda block, after: None, done=lambda block, block_grads: None):
    S, D = x.shape
    nq = S // ATT_TILE
    grads = {}
    saved = {}

    def gain_after(gain, token):
        return gain if token is None else gain + token

    def ffn_f(tag, l, xin):
        need((tag, l), xin)
        out, xn, G, U = _ffn_fwd(xin, W[tag + "_norm"][l:l + 1], W[tag + "_w_gate"][l], W[tag + "_w_up"][l],
                                 W[tag + "_w_down"][l])
        saved[(tag, l)] = (xin, xn, G, U)
        return out

    def ffn_b(tag, l, dout):
        xin, xn, G, U = saved[(tag, l)]
        dwg, dwu, dwd, dG, dU = _ffn_bwd_w(dout, xn, G, U, W[tag + "_w_down"][l])
        big = {(tag + "_w_gate", l): dwg, (tag + "_w_up", l): dwu, (tag + "_w_down", l): dwd}
        grads.update(big)
        token = done((tag, l), big)
        dx, dg = _norm_in_bwd([dG, dU], [W[tag + "_w_gate"][l], W[tag + "_w_up"][l]], xin,
                              gain_after(W[tag + "_norm"][l:l + 1], token), dout)
        grads[(tag + "_norm", l)] = dg
        return dx

    x0a = ffn_f("ffn1", 0, x)
    need(("ev", 0), x0a)
    w_in = W["ev_w_in"]
    w_main, w_f = w_in[:, :2560], jnp.pad(w_in[:, 2560:], ((0, 0), (0, 120)))
    h0, z0, fl = _norm_proj(x0a, W["mix_norm"][0:1], w_main, w_f)
    cw_a = _pad_rows(W["ev_conv_w"], 32)
    a_act, a1 = _conv_a_fwd(z0, cw_a, W["ev_conv_b"], W["ev_conv_norm"])
    flb, Fc = _forget_scan(fl, jnp.pad(W["ev_b_f"], ((0, 0), (0, 120))))
    Ft = Fc[:, :N_HEADS].T
    fcol = Ft.reshape(N_HEADS, S, 1)
    frow = Ft.reshape(N_HEADS, nq, 1, ATT_TILE)
    q_raw, k_raw, v_h = _heads(z0[:, 1024:1536]), _heads(z0[:, 1536:2048]), _heads(z0[:, 2048:2560])
    qs, kn = _qk_norm(q_raw, k_raw, W["ev_q_norm"], W["ev_k_norm"])
    o_h, lse = _fox_fwd(qs, kn, v_h, fcol, frow)
    o_flat = _unheads(o_h)
    w_out_e = W["ev_w_out"]
    x0b = _proj_res([a_act, o_flat], [w_out_e[:D_CONV], w_out_e[D_CONV:]], x0a)
    x0c = ffn_f("ffn2", 0, x0b)
    x1a = ffn_f("ffn1", 1, x0c)
    need(("od", 0), x1a)
    h1, z1 = _norm_proj(x1a, W["mix_norm"][1:2], W["od_w_in"])
    cw_c = _pad_rows(W["od_conv_w"], 8)
    y1 = _odd_mid_fwd(z1, cw_c)
    x1b = _proj_res([y1], [W["od_w_out"]], x1a)
    x1c = ffn_f("ffn2", 1, x1b)
    dy, loss = _loss_head(x1c, tgt)

    d = ffn_b("ffn2", 1, dy)
    dy1 = _matmul_nt(d, W["od_w_out"])
    grads[("od_w_out", 0)] = _matmul_tn(y1, d, D)[0]
    dz1, dcw_c = _odd_mid_bwd(dy1, z1, cw_c)
    grads[("od_conv_w", 0)] = dcw_c[:CONV_C_WIDTH]
    grads[("od_w_in", 0)] = _matmul_tn(h1, dz1, 3 * D // 4)
    token = done(("od", 0), {k: grads[k] for k in (("od_w_out", 0), ("od_w_in", 0))})
    d, dg = _norm_in_bwd([dz1[None]], [W["od_w_in"][None]], x1a, gain_after(W["mix_norm"][1:2], token), d)
    grads[("mix_norm", 1)] = dg
    d = ffn_b("ffn1", 1, d)
    d = ffn_b("ffn2", 0, d)
    dcat = _matmul_nt(d, w_out_e)
    grads[("ev_w_out", 0)] = jnp.concatenate([_matmul_tn(a_act, d, D)[0], _matmul_tn(o_flat, d, D)[0]], axis=0)
    duz, dcw_a, dcb, dcn = _conv_a_bwd(dcat, a1, z0, cw_a, W["ev_conv_norm"])
    grads[("ev_conv_w", 0)] = dcw_a[:CONV_A_WIDTH]
    grads[("ev_conv_b", 0)] = dcb
    grads[("ev_conv_norm", 0)] = dcn
    do_h = _heads(dcat[:, D_CONV:].astype(BF16))
    dqs, dkn, dv, dfq, dfk = _fox_bwd(qs, kn, v_h, o_h, do_h, lse, fcol, frow)
    dq_raw, dk_raw, dqw, dkw = _qk_norm_bwd(dqs, dkn, q_raw, k_raw, W["ev_q_norm"], W["ev_k_norm"])
    grads[("ev_q_norm", 0)] = dqw
    grads[("ev_k_norm", 0)] = dkw
    dF = (dfq.reshape(N_HEADS, S) - dfk.reshape(N_HEADS, S)).T
    dfl, dbf = _forget_scan_bwd(jnp.pad(dF, ((0, 0), (0, 120))), flb)
    grads[("ev_b_f", 0)] = dbf[:, :N_HEADS]
    dz0 = jnp.concatenate([duz, _unheads(dq_raw), _unheads(dk_raw), _unheads(dv.astype(BF16))], axis=1)
    dflb = dfl.astype(BF16)
    gmain = _matmul_tn(h0, dz0, 640)
    gmain = gmain.transpose(1, 0, 2).reshape(D, 2560)
    gf = _matmul_tn(h0, dflb, 128)[0][:, :N_HEADS]
    grads[("ev_w_in", 0)] = jnp.concatenate([gmain, gf], axis=1)
    token = done(("ev", 0), {k: grads[k] for k in (("ev_w_out", 0), ("ev_w_in", 0))})
    d, dg = _norm_in_bwd([dz0[None], dflb[None]], [w_main[None], w_f[None]], x0a, gain_after(W["mix_norm"][0:1], token), d)
    grads[("mix_norm", 0)] = dg
    d = ffn_b("ffn1", 0, d)
    return loss, d, grads


def _place():
    x, y, c = lax.axis_index("x"), lax.axis_index("y"), lax.axis_index("c")
    chips = [(1 - x, y), (x, 1 - y), (1 - x, 1 - y)]
    return x, y, c, chips


def _remote(src, dst, send_sem, recv_sem, to):
    return pltpu.make_async_remote_copy(src_ref=src, dst_ref=dst, send_sem=send_sem, recv_sem=recv_sem,
                                        device_id=to, device_id_type=MESH)


HBM = pl.BlockSpec(memory_space=pltpu.HBM)
SEM = pl.BlockSpec(memory_space=pltpu.SEMAPHORE)
EFFECT = pltpu.SideEffectType.DATAFLOW_SIDE_EFFECTING


def _in_hbm(a):
    return pltpu.with_memory_space_constraint(a, pltpu.HBM)


def _ag_start(bufs):
    n = len(bufs)

    def body(*refs):
        send_sems, recv_sems = refs[n], refs[n + 1]
        outs = refs[n + 2:]
        x, y, c, chips = _place()
        me = 2 * x + y
        for a in range(n):
            if a == n - 1:
                blk = outs[a].at[me]
            else:
                h = outs[a].shape[1] // 2
                blk = outs[a].at[me, pl.ds(c * h, h)]
            for jj, (px, py) in enumerate(chips):
                _remote(blk, blk, send_sems.at[3 * a + jj], recv_sems.at[3 * a + jj], (px, py, c)).start()

    return _pallas(
        body, name="gather_start",
        out_shape=[pltpu.SemaphoreType.DMA((3 * n,)), pltpu.SemaphoreType.DMA((3 * n,))] + [pltpu.HBM(b.shape, b.dtype) for b in bufs],
        in_specs=[HBM] * n, out_specs=[SEM, SEM] + [HBM] * n, input_output_aliases={a: 2 + a for a in range(n)},
        compiler_params=pltpu.CompilerParams(has_side_effects=EFFECT),
    )(*[_in_hbm(b) for b in bufs])


def _ag_mid(g, ici_send, ici_recv, bufs, idx, taps, n_big, after):
    n = len(bufs)
    arrs = list(bufs) + ([taps] if taps is not None else [])
    m = len(arrs)

    def body(*refs):
        ici_s, ici_r = refs[0], refs[1]
        d_send, d_recv = refs[m + 3], refs[m + 4]
        outs = refs[m + 5:]
        x, y, c, chips = _place()
        me = 2 * x + y
        for i in range(m):
            a = idx[i] if i < n else n_big
            for jj, (px, py) in enumerate(chips):
                k = 3 * a + jj
                if i < n:
                    h = outs[i].shape[1] // 2
                    mine, blk = outs[i].at[me, pl.ds(c * h, h)], outs[i].at[2 * px + py, pl.ds(c * h, h)]
                else:
                    mine, blk = outs[i].at[me], outs[i].at[2 * px + py]
                _remote(mine, mine, ici_s.at[k], ici_r.at[k], (px, py, c)).wait_send()
                _remote(blk, blk, ici_s.at[k], ici_r.at[k], (px, py, c)).wait_recv()
                if i < n:
                    _remote(blk, blk, d_send.at[3 * i + jj], d_recv.at[3 * i + jj], (x, y, 1 - c)).start()

    return _pallas(
        body, name=f"gather_pass_on_{g}",
        out_shape=[pltpu.SemaphoreType.DMA((3 * n,)), pltpu.SemaphoreType.DMA((3 * n,))] + [pltpu.HBM(b.shape, b.dtype) for b in arrs],
        in_specs=[SEM, SEM] + [HBM] * m + [ANY], out_specs=[SEM, SEM] + [HBM] * m,
        input_output_aliases={2 + i: 2 + i for i in range(m)},
        compiler_params=pltpu.CompilerParams(has_side_effects=EFFECT),
    )(ici_send, ici_recv, *arrs, after)


def _ag_wait(g, d_send, d_recv, arrs, n, after):
    m = len(arrs)

    def body(*refs):
        d_s, d_r = refs[0], refs[1]
        outs = refs[m + 3:]
        x, y, c, chips = _place()
        for i in range(n):
            h = outs[i].shape[1] // 2
            for jj, (px, py) in enumerate(chips):
                sent = outs[i].at[2 * px + py, pl.ds(c * h, h)]
                got = outs[i].at[2 * px + py, pl.ds((1 - c) * h, h)]
                _remote(sent, sent, d_s.at[3 * i + jj], d_r.at[3 * i + jj], (x, y, 1 - c)).wait_send()
                _remote(got, got, d_s.at[3 * i + jj], d_r.at[3 * i + jj], (x, y, 1 - c)).wait_recv()

    return _pallas(
        body, name=f"gather_wait_{g}", out_shape=[pltpu.HBM(b.shape, b.dtype) for b in arrs],
        in_specs=[SEM, SEM] + [HBM] * m + [ANY], out_specs=[HBM] * m,
        input_output_aliases={2 + i: i for i in range(m)},
        compiler_params=pltpu.CompilerParams(has_side_effects=EFFECT),
    )(d_send, d_recv, *arrs, after)


def _pair_exchange(gs):
    n = len(gs)

    def body(*refs):
        ins, outs = refs[:n], refs[n:2 * n]
        send_sems, recv_sems = refs[2 * n:]
        x, y, c, _ = _place()
        cps = []
        for a in range(n):
            h = ins[a].shape[1] // 2
            cps.append(_remote(ins[a].at[:, pl.ds((1 - c) * h, h)], outs[a], send_sems.at[a], recv_sems.at[a], (x, y, 1 - c)))
        for cp in cps:
            cp.start()
        for cp in cps:
            cp.wait()

    return _pallas(body, name="grad_pair_exchange", in_specs=[ANY] * n, out_specs=[ANY] * n,
                   out_shape=[_sds((4, g.shape[1] // 2, g.shape[2]), g.dtype) for g in gs],
                   scratch_shapes=[pltpu.SemaphoreType.DMA((n,)), pltpu.SemaphoreType.DMA((n,))])(*gs)


def _pair_add(g, other, c_arr):
    _, R, C = g.shape
    h = R // 2

    def body(c_ref, g_ref, o_ref, out_ref):
        out_ref[...] = (g_ref[...].astype(F32) + o_ref[...].astype(F32)).astype(BF16)

    grid_spec = pltpu.PrefetchScalarGridSpec(
        num_scalar_prefetch=1, grid=(4,),
        in_specs=[pl.BlockSpec((1, h, C), lambda k, c_ref: (k, c_ref[0], 0)), pl.BlockSpec((1, h, C), lambda k, c_ref: (k, 0, 0))],
        out_specs=pl.BlockSpec((1, h, C), lambda k, c_ref: (k, 0, 0)))
    return _pallas(body, name="grad_pair_add", grid_spec=grid_spec, out_shape=_sds((4, h, C), BF16),
                   compiler_params=_cp(("parallel",)))(c_arr, g, other)


def _chip_start(g, ss):
    n = len(ss)
    zones = [lax.empty((3,) + s.shape[1:], s.dtype) for s in ss]

    def body(*refs):
        send_sems, recv_sems = refs[2 * n], refs[2 * n + 1]
        src, dst = refs[2 * n + 2:3 * n + 2], refs[3 * n + 2:4 * n + 2]
        token = refs[4 * n + 2]
        x, y, c, chips = _place()
        for a in range(n):
            for jj, (px, py) in enumerate(chips):
                k = 3 * a + jj
                _remote(src[a].at[2 * px + py], dst[a].at[jj], send_sems.at[k], recv_sems.at[k], (px, py, c)).start()
        token[...] = jnp.zeros_like(token)

    return _pallas(
        body, name=f"grad_chip_start_{g}",
        out_shape=[pltpu.SemaphoreType.DMA((3 * n,)), pltpu.SemaphoreType.DMA((3 * n,))]
        + [pltpu.HBM(a.shape, a.dtype) for a in ss + zones] + [_sds((8, 128), F32)],
        in_specs=[HBM] * (2 * n), out_specs=[SEM, SEM] + [HBM] * (2 * n) + [pl.BlockSpec(memory_space=pltpu.VMEM)],
        input_output_aliases={i: 2 + i for i in range(2 * n)},
        compiler_params=pltpu.CompilerParams(has_side_effects=EFFECT),
    )(*[_in_hbm(a) for a in ss + zones])


def _chip_wait(sends, recvs, counts, ss, zones, after):
    nb, n = len(sends), len(ss)

    def body(*refs):
        s_refs, r_refs = refs[:nb], refs[nb:2 * nb]
        outs = refs[2 * nb + 2 * n + 1:]
        src, dst = outs[:n], outs[n:]
        x, y, c, chips = _place()
        a = 0
        for b in range(nb):
            for i in range(counts[b]):
                for jj, (px, py) in enumerate(chips):
                    k = 3 * i + jj
                    _remote(src[a].at[2 * px + py], dst[a].at[jj], s_refs[b].at[k], r_refs[b].at[k], (px, py, c)).wait()
                a += 1

    return _pallas(
        body, name="grad_chip_wait", out_shape=[pltpu.HBM(a.shape, a.dtype) for a in ss + zones],
        in_specs=[SEM] * (2 * nb) + [HBM] * (2 * n) + [ANY], out_specs=[HBM] * (2 * n),
        input_output_aliases={2 * nb + i: i for i in range(2 * n)},
        compiler_params=pltpu.CompilerParams(has_side_effects=EFFECT),
    )(*sends, *recvs, *ss, *zones, after)


def _chip_sum(s, r, where, dest, l, L):
    _, h, C = s.shape
    tr = h // 2

    def body(k_ref, s_ref, r_ref, *rest):
        out_ref = rest[-1]
        acc = s_ref[0].astype(F32)
        for jj in range(3):
            acc = acc + r_ref[jj].astype(F32)
        out_ref[...] = acc

    in_specs = [pl.BlockSpec((1, tr, C), lambda i, k_ref: (k_ref[0], i, 0)), pl.BlockSpec((3, tr, C), lambda i, k_ref: (0, i, 0))]
    args = [where, s, r]
    alias = {}
    if dest is not None:
        in_specs.append(ANY)
        args.append(dest)
        alias = {3: 0}
    grid_spec = pltpu.PrefetchScalarGridSpec(
        num_scalar_prefetch=1, grid=(2,), in_specs=in_specs,
        out_specs=pl.BlockSpec((None, tr, C), lambda i, k_ref: (l, 2 * k_ref[1] + i, 0)))
    return _pallas(body, name="grad_chip_sum", grid_spec=grid_spec, out_shape=_sds((L, 2 * h, C), F32),
                   input_output_aliases=alias, compiler_params=_cp(("arbitrary",)))(*args)


def _pair_share(bufs, layout):
    n = len(layout)
    n_out = len(bufs)

    def body(*refs):
        outs = refs[n_out:2 * n_out]
        send_sems, recv_sems = refs[2 * n_out:]
        x, y, c, _ = _place()
        cps = []
        for a, (o, l) in enumerate(layout):
            h = outs[o].shape[1] // 2
            blk = outs[o].at[l, pl.ds(c * h, h)]
            cps.append(_remote(blk, blk, send_sems.at[a], recv_sems.at[a], (x, y, 1 - c)))
        for cp in cps:
            cp.start()
        for a, (o, l) in enumerate(layout):
            h = outs[o].shape[1] // 2
            blk = outs[o].at[l, pl.ds((1 - c) * h, h)]
            _remote(blk, blk, send_sems.at[a], recv_sems.at[a], (x, y, 1 - c)).wait_recv()
        for cp in cps:
            cp.wait_send()

    return _pallas(body, name="grad_pair_share", in_specs=[ANY] * n_out, out_specs=[ANY] * n_out,
                   out_shape=[_sds(b.shape, b.dtype) for b in bufs], input_output_aliases={o: o for o in range(n_out)},
                   scratch_shapes=[pltpu.SemaphoreType.DMA((n,)), pltpu.SemaphoreType.DMA((n,))])(*bufs)


def _small_all_reduce(packed):
    P, L = packed.shape

    def body(in_ref, out_ref, slots, send_sems, recv_sems):
        x, y, c, _ = _place()
        me = 4 * x + 2 * y + c
        slots[me] = in_ref[...]
        cps = []
        for r in range(1, 8):
            px = 1 - x if r & 4 else x
            py = 1 - y if r & 2 else y
            pc = 1 - c if r & 1 else c
            cps.append(_remote(in_ref, slots.at[me], send_sems.at[r - 1], recv_sems.at[r - 1], (px, py, pc)))
        for cp in cps:
            cp.start()
        for r in range(1, 8):
            px = 1 - x if r & 4 else x
            py = 1 - y if r & 2 else y
            pc = 1 - c if r & 1 else c
            blk = slots.at[4 * px + 2 * py + pc]
            _remote(blk, blk, send_sems.at[r - 1], recv_sems.at[r - 1], (px, py, pc)).wait_recv()
        for cp in cps:
            cp.wait_send()
        acc = slots[0]
        for k in range(1, 8):
            acc = acc + slots[k]
        out_ref[...] = acc

    vm = pl.BlockSpec(memory_space=pltpu.VMEM)
    return _pallas(body, name="small_all_reduce", in_specs=[vm], out_specs=vm, out_shape=_sds((P, L), F32),
                   scratch_shapes=[pltpu.VMEM((8, P, L), F32), pltpu.SemaphoreType.DMA((7,)), pltpu.SemaphoreType.DMA((7,))])(packed)


def _adamw_math(w, g, m, v):
    m = ADAM_B1 * m + (1.0 - ADAM_B1) * g
    v = ADAM_B2 * v + (1.0 - ADAM_B2) * (g * g)
    m_hat = m / (1.0 - ADAM_B1 ** ADAM_STEP)
    v_hat = v / (1.0 - ADAM_B2 ** ADAM_STEP)
    delta = -ADAM_LR * (m_hat / (jnp.sqrt(v_hat) + ADAM_EPS) + ADAM_WD * w)
    return delta, m, v


def _adamw(w, g, m, v):
    shape = w.shape
    C = shape[-1]
    rows = math.prod(shape[:-1])
    tr = next(t for t in (512, 352, 256, 128, 64, 32, 16, 8, rows) if rows % t == 0)
    w2, g2, m2, v2 = (a.reshape(rows, C) for a in (w, g, m, v))

    def body(w_ref, g_ref, m_ref, v_ref, d_ref, nm_ref, nv_ref):
        d, nm, nv = _adamw_math(w_ref[...], g_ref[...], m_ref[...], v_ref[...])
        d_ref[...] = d
        nm_ref[...] = nm
        nv_ref[...] = nv

    blk = pl.BlockSpec((tr, C), lambda i: (i, 0))
    outs = _pallas(body, name="adamw", grid=(rows // tr,), in_specs=[blk] * 4, out_specs=[blk] * 3,
                   out_shape=[_sds((rows, C), F32)] * 3, compiler_params=_cp(("parallel",)))(w2, g2, m2, v2)
    return tuple(o.reshape(shape) for o in outs)


WEIGHTS = ["ffn1_norm", "ffn1_w_gate", "ffn1_w_up", "ffn1_w_down", "mix_norm", "ffn2_norm", "ffn2_w_gate", "ffn2_w_up",
           "ffn2_w_down", "ev_w_in", "ev_b_f", "ev_conv_w", "ev_conv_b", "ev_conv_norm", "ev_q_norm", "ev_k_norm",
           "ev_w_out", "od_w_in", "od_conv_w", "od_w_out"]
BIG = ([("ffn1_w_gate", 0), ("ffn1_w_up", 0), ("ffn1_w_down", 0), ("ev_w_in", 0), ("ev_w_out", 0),
        ("ffn2_w_gate", 0), ("ffn2_w_up", 0), ("ffn2_w_down", 0)]
       + [("ffn1_w_gate", 1), ("ffn1_w_up", 1), ("ffn1_w_down", 1), ("od_w_in", 0), ("od_w_out", 0),
          ("ffn2_w_gate", 1), ("ffn2_w_up", 1), ("ffn2_w_down", 1)])
BLOCKS = [("ffn1", 0), ("ev", 0), ("ffn2", 0), ("ffn1", 1), ("od", 0), ("ffn2", 1)]
BLOCK_OF = {(name, l): (name.split("_w_")[0], l) for name, l in BIG}
BIG_NAMES = ["ffn1_w_gate", "ffn1_w_up", "ffn1_w_down", "ffn2_w_gate", "ffn2_w_up", "ffn2_w_down",
             "ev_w_in", "ev_w_out", "od_w_in", "od_w_out"]
SMALL = [("ffn1_norm", 16), ("mix_norm", 16), ("ffn2_norm", 16), ("ev_b_f", 8), ("ev_conv_w", 128), ("ev_conv_b", 8),
         ("ev_conv_norm", 8), ("ev_q_norm", 8), ("ev_k_norm", 8), ("od_conv_w", 24)]


def _to_lanes(a, rows):
    flat = a.reshape(-1)
    return jnp.pad(flat, (0, rows * 128 - flat.shape[0])).reshape(rows, 128)


def kernel(x, ffn1_norm, ffn1_w_gate, ffn1_w_up, ffn1_w_down, mix_norm, ffn2_norm, ffn2_w_gate, ffn2_w_up, ffn2_w_down, ev_w_in, ev_b_f, ev_conv_w, ev_conv_b, ev_conv_norm, ev_q_norm, ev_k_norm, ev_w_out, od_w_in, od_conv_w, od_w_out, loss_target, m_ffn1_norm, m_ffn1_w_gate, m_ffn1_w_up, m_ffn1_w_down, m_mix_norm, m_ffn2_norm, m_ffn2_w_gate, m_ffn2_w_up, m_ffn2_w_down, m_ev_w_in, m_ev_b_f, m_ev_conv_w, m_ev_conv_b, m_ev_conv_norm, m_ev_q_norm, m_ev_k_norm, m_ev_w_out, m_od_w_in, m_od_conv_w, m_od_w_out, v_ffn1_norm, v_ffn1_w_gate, v_ffn1_w_up, v_ffn1_w_down, v_mix_norm, v_ffn2_norm, v_ffn2_w_gate, v_ffn2_w_up, v_ffn2_w_down, v_ev_w_in, v_ev_b_f, v_ev_conv_w, v_ev_conv_b, v_ev_conv_norm, v_ev_q_norm, v_ev_k_norm, v_ev_w_out, v_od_w_in, v_od_conv_w, v_od_w_out):
    P = dict(ffn1_norm=ffn1_norm, ffn1_w_gate=ffn1_w_gate, ffn1_w_up=ffn1_w_up, ffn1_w_down=ffn1_w_down, mix_norm=mix_norm,
             ffn2_norm=ffn2_norm, ffn2_w_gate=ffn2_w_gate, ffn2_w_up=ffn2_w_up, ffn2_w_down=ffn2_w_down, ev_w_in=ev_w_in,
             ev_b_f=ev_b_f, ev_conv_w=ev_conv_w, ev_conv_b=ev_conv_b, ev_conv_norm=ev_conv_norm, ev_q_norm=ev_q_norm,
             ev_k_norm=ev_k_norm, ev_w_out=ev_w_out, od_w_in=od_w_in, od_conv_w=od_conv_w, od_w_out=od_w_out)
    M = dict(zip(WEIGHTS, [m_ffn1_norm, m_ffn1_w_gate, m_ffn1_w_up, m_ffn1_w_down, m_mix_norm, m_ffn2_norm, m_ffn2_w_gate,
                           m_ffn2_w_up, m_ffn2_w_down, m_ev_w_in, m_ev_b_f, m_ev_conv_w, m_ev_conv_b, m_ev_conv_norm,
                           m_ev_q_norm, m_ev_k_norm, m_ev_w_out, m_od_w_in, m_od_conv_w, m_od_w_out]))
    V = dict(zip(WEIGHTS, [v_ffn1_norm, v_ffn1_w_gate, v_ffn1_w_up, v_ffn1_w_down, v_mix_norm, v_ffn2_norm, v_ffn2_w_gate,
                           v_ffn2_w_up, v_ffn2_w_down, v_ev_w_in, v_ev_b_f, v_ev_conv_w, v_ev_conv_b, v_ev_conv_norm,
                           v_ev_q_norm, v_ev_k_norm, v_ev_w_out, v_od_w_in, v_od_conv_w, v_od_w_out]))
    S, D = x.shape[1], x.shape[2]
    chip = 2 * lax.axis_index("x") + lax.axis_index("y")
    core = lax.axis_index("c")

    def own_slot(shard):
        return lax.dynamic_update_slice(jnp.zeros((4,) + shard.shape, shard.dtype), shard[None], (chip, 0, 0))

    taps = jnp.concatenate([_to_lanes(_pad_rows(ev_conv_w[0], 32), 32), _to_lanes(_pad_rows(od_conv_w[0], 8), 16)], axis=0)
    ici_send, ici_recv, *bufs = _ag_start([own_slot(P[name][l].astype(BF16)) for name, l in BIG] + [own_slot(taps)])
    cols = lambda a: a.transpose(1, 0, 2).reshape(a.shape[1], 4 * a.shape[2])
    W = {k: P[k] for k in ("ffn1_norm", "mix_norm", "ffn2_norm", "ev_b_f", "ev_q_norm", "ev_k_norm")}
    W["ev_conv_b"], W["ev_conv_norm"] = ev_conv_b, ev_conv_norm
    for tag in ("ffn1", "ffn2"):
        for kind in ("_w_gate", "_w_up", "_w_down"):
            W[tag + kind] = [None, None]
    passing = {}

    def pass_on(g, after):
        idx = [i for i, k in enumerate(BIG) if BLOCK_OF[k] == BLOCKS[g]]
        keys = [BIG[i] for i in idx] + (["taps"] if BLOCKS[g] == ("ev", 0) else [])
        passing[g] = (keys, _ag_mid(g, ici_send, ici_recv, [bufs[i] for i in idx], idx,
                                    bufs[-1] if BLOCKS[g] == ("ev", 0) else None, len(BIG), after))

    def need(block, after):
        g = BLOCKS.index(block)
        if g not in passing:
            pass_on(g, after)
        keys, (d_send, d_recv, *thru) = passing.pop(g)
        got = dict(zip(keys, _ag_wait(g, d_send, d_recv, thru, len(keys) - ("taps" in keys), after)))
        if 1 <= g < len(BLOCKS) - 1:
            pass_on(g + 1, after)
        for key, a in got.items():
            if key == "taps":
                continue
            name, l = key
            if name.startswith("ffn"):
                W[name][l] = a
            elif name.endswith("_w_in"):
                W[name] = cols(a)
            elif name.endswith("_w_out"):
                W[name] = a.reshape(4 * a.shape[1], D)
        if block == ("ev", 0):
            taps_all = got["taps"]
            W["ev_conv_w"] = cols(taps_all[:, :32].reshape(4, 32, 128))[:CONV_A_WIDTH]
            W["od_conv_w"] = cols(taps_all[:, 32:48].reshape(4, 8, 256))[:CONV_C_WIDTH]

    rows = lambda a: a.reshape(4, a.shape[0] // 4, a.shape[1])
    colsh = lambda a: a.reshape(a.shape[0], 4, a.shape[1] // 4).transpose(1, 0, 2)
    c_arr = core.reshape(1).astype(jnp.int32)
    where = jnp.stack([chip, core]).astype(jnp.int32)
    in_flight = []

    def done(block, block_grads):
        g = BLOCKS.index(block)
        keys = list(block_grads)
        gs = []
        for name, l in keys:
            a = block_grads[(name, l)]
            gs.append(colsh(a) if name == "ev_w_in" else rows(a) if name.endswith("_w_out") else a)
        others = _pair_exchange(gs)
        sums = [_pair_add(a, o, c_arr) for a, o in zip(gs, others)]
        send, recv, *rest = _chip_start(g, sums)
        in_flight.append((keys, send, recv, rest[:len(keys)], rest[len(keys):2 * len(keys)]))
        return rest[-1][0:1, 0:1]

    loss, grad_x, grads = _local_step(x[0], loss_target[0], W, need, done)

    order = [k for keys, *_ in in_flight for k in keys]
    landed = _chip_wait([f[1] for f in in_flight], [f[2] for f in in_flight], [len(f[0]) for f in in_flight],
                        [a for f in in_flight for a in f[3]], [a for f in in_flight for a in f[4]], grad_x)
    sums, recvd = landed[:len(order)], landed[len(order):]
    stacked = {}
    for (name, l), s, r in zip(order, sums, recvd):
        stacked[name] = _chip_sum(s, r, where, stacked.get(name), l, P[name].shape[0])
    layout = [(BIG_NAMES.index(name), l) for name, l in order]
    big_grads = dict(zip(BIG_NAMES, _pair_share([stacked[name] for name in BIG_NAMES], layout)))

    def small_grad(name):
        if name.endswith("_norm") and name[:3] in ("ffn", "mix"):
            return jnp.concatenate([grads[(name, 0)], grads[(name, 1)]], axis=0)
        return grads[(name, 0)]

    packed = jnp.concatenate([_to_lanes(small_grad(name), r) for name, r in SMALL], axis=0)
    total = _small_all_reduce(packed)
    small_grads, at = {}, 0
    for name, r in SMALL:
        part = total[at:at + r].reshape(-1)
        at += r
        if name == "ev_conv_w":
            full_g = part[:CONV_A_WIDTH * D_CONV].reshape(CONV_A_WIDTH, D_CONV)
            small_grads[name] = lax.dynamic_slice_in_dim(full_g, chip * (D_CONV // 4), D_CONV // 4, axis=1)[None]
        elif name == "od_conv_w":
            full_g = part[:CONV_C_WIDTH * D].reshape(CONV_C_WIDTH, D)
            small_grads[name] = lax.dynamic_slice_in_dim(full_g, chip * (D // 4), D // 4, axis=1)[None]
        else:
            small_grads[name] = part[:math.prod(P[name].shape)].reshape(P[name].shape)

    grad_w, delta_w, new_m, new_v = [], [], [], []
    for name in WEIGHTS:
        g = big_grads[name] if name in big_grads else small_grads[name]
        d, nm, nv = _adamw(P[name], g, M[name], V[name])
        grad_w.append(g)
        delta_w.append(d)
        new_m.append(nm)
        new_v.append(nv)
    loss_all = lax.psum(loss[0, 0], ("x", "y", "c"))
    return (loss_all, grad_x[None], *grad_w, *delta_w, *new_m, *new_v)
```

```python
import functools
import math

import jax
import jax.numpy as jnp
from jax import lax
from jax.experimental import pallas as pl
from jax.experimental.pallas import tpu as pltpu

F32, BF16 = jnp.float32, jnp.bfloat16
EPS = 1e-6
FFN_RES = 0.5
N_HEADS, HEAD_DIM = 8, 64
D_CONV = 512
D_ATTN = N_HEADS * HEAD_DIM
CONV_A_WIDTH, CONV_C_WIDTH = 31, 3
ADAM_LR, ADAM_B1, ADAM_B2, ADAM_EPS, ADAM_WD, ADAM_STEP = 0.001, 0.9, 0.999, 1e-08, 0.01, 10
MESH = pl.DeviceIdType.MESH
ANY = pl.BlockSpec(memory_space=pl.ANY)

TOK_TILE = 512
ATT_TILE = 512
QKN_TILE = 2048
HALO_A, HALO_C = 32, 16
SCAN_BLK = 256
MIB = 2 ** 20


def _pallas(body, **kw):
    return pl.pallas_call(body, **kw)


def _cp(sem=None, vmem_mib=48):
    return pltpu.CompilerParams(dimension_semantics=sem, vmem_limit_bytes=vmem_mib * MIB)


def _dot(a, b):
    return jnp.dot(a, b, preferred_element_type=F32)


def _dot_nt(a, b):
    return lax.dot_general(a, b, (((1,), (1,)), ((), ())), preferred_element_type=F32)


def _dot_tn(a, b):
    return lax.dot_general(a, b, (((0,), (0,)), ((), ())), preferred_element_type=F32)


def _sds(shape, dtype):
    return jax.ShapeDtypeStruct(shape, dtype)


def _rms(x):
    return lax.rsqrt(jnp.mean(x * x, axis=-1, keepdims=True) + EPS)


def _rms_bwd(dy, x, g):
    r = _rms(x)
    xh = x * r
    dxh = dy * g
    dx = r * (dxh - xh * jnp.mean(dxh * xh, axis=-1, keepdims=True))
    return dx, xh


def _silu_grad(z):
    s = jax.nn.sigmoid(z)
    return s * (1.0 + z * (1.0 - s))


def _ffn_fwd(x, g, wg, wu, wd):
    S, D = x.shape
    nc, Fs, _ = wd.shape
    tm = TOK_TILE

    def body(x_ref, g_ref, wg_ref, wu_ref, wd_ref, out_ref, xn_ref, G_ref, U_ref, acc_ref):
        j = pl.program_id(1)

        @pl.when(j == 0)
        def _():
            xv = x_ref[...]
            xn_ref[...] = (xv * _rms(xv) * g_ref[...]).astype(BF16)
            acc_ref[...] = jnp.zeros_like(acc_ref)

        xn = xn_ref[...]
        G = _dot_nt(xn, wg_ref[0])
        U = _dot_nt(xn, wu_ref[0])
        G_ref[0] = G.astype(BF16)
        U_ref[0] = U.astype(BF16)
        H = (G * jax.nn.sigmoid(G) * U).astype(BF16)
        acc_ref[...] += _dot(H, wd_ref[0])

        @pl.when(j == nc - 1)
        def _():
            out_ref[...] = x_ref[...] + FFN_RES * acc_ref[...]

    row = pl.BlockSpec((tm, D), lambda i, j: (i, 0))
    return _pallas(
        body, name="ffn_fwd", grid=(S // tm, nc),
        in_specs=[row, pl.BlockSpec((1, D), lambda i, j: (0, 0)),
                  pl.BlockSpec((1, Fs, D), lambda i, j: (j, 0, 0)), pl.BlockSpec((1, Fs, D), lambda i, j: (j, 0, 0)),
                  pl.BlockSpec((1, Fs, D), lambda i, j: (j, 0, 0))],
        out_specs=[row, row, pl.BlockSpec((1, tm, Fs), lambda i, j: (j, i, 0)),
                   pl.BlockSpec((1, tm, Fs), lambda i, j: (j, i, 0))],
        out_shape=[_sds((S, D), F32), _sds((S, D), BF16), _sds((nc, S, Fs), BF16), _sds((nc, S, Fs), BF16)],
        scratch_shapes=[pltpu.VMEM((tm, D), F32)],
        compiler_params=_cp(("parallel", "arbitrary")),
    )(x, g, wg, wu, wd)


def _ffn_bwd_w(dout, xn, G, U, wd):
    S, D = dout.shape
    nc, _, Fs = G.shape
    tm = TOK_TILE
    nt = S // tm

    def body(do_ref, xn_ref, G_ref, U_ref, wd_ref, dwg_ref, dwu_ref, dwd_ref, dG_ref, dU_ref, ag, au, ad):
        i = pl.program_id(1)

        @pl.when(i == 0)
        def _():
            ag[...] = jnp.zeros_like(ag)
            au[...] = jnp.zeros_like(au)
            ad[...] = jnp.zeros_like(ad)

        do = (FFN_RES * do_ref[...]).astype(BF16)
        Gv = G_ref[0].astype(F32)
        Uv = U_ref[0].astype(F32)
        dH = _dot_nt(do, wd_ref[0])
        sg = jax.nn.sigmoid(Gv)
        act = Gv * sg
        H = (act * Uv).astype(BF16)
        dU = (dH * act).astype(BF16)
        dG = (dH * Uv * (sg * (1.0 + Gv * (1.0 - sg)))).astype(BF16)
        dG_ref[0] = dG
        dU_ref[0] = dU
        xnv = xn_ref[...]
        ag[...] += _dot_tn(dG, xnv)
        au[...] += _dot_tn(dU, xnv)
        ad[...] += _dot_tn(H, do)

        @pl.when(i == nt - 1)
        def _():
            dwg_ref[0] = ag[...].astype(BF16)
            dwu_ref[0] = au[...].astype(BF16)
            dwd_ref[0] = ad[...].astype(BF16)

    row = pl.BlockSpec((tm, D), lambda j, i: (i, 0))
    hid = pl.BlockSpec((1, tm, Fs), lambda j, i: (j, i, 0))
    wrow = pl.BlockSpec((1, Fs, D), lambda j, i: (j, 0, 0))
    return _pallas(
        body, name="ffn_bwd_w", grid=(nc, nt),
        in_specs=[row, row, hid, hid, wrow],
        out_specs=[wrow, wrow, wrow, hid, hid],
        out_shape=[_sds((nc, Fs, D), BF16)] * 3 + [_sds((nc, S, Fs), BF16)] * 2,
        scratch_shapes=[pltpu.VMEM((Fs, D), F32)] * 3,
        compiler_params=_cp(("parallel", "arbitrary"), 56),
    )(dout, xn, G, U, wd)


def _norm_in_bwd(dzs, ws, x, g, dres, w_rows=False):
    S, D = x.shape
    nc = dzs[0].shape[0]
    n = len(dzs)
    tm = TOK_TILE

    def body(*refs):
        dz_refs, w_refs = refs[:n], refs[n:2 * n]
        x_ref, g_ref, dres_ref, dx_ref, dg_ref, acc_ref = refs[2 * n:]
        i, j = pl.program_id(0), pl.program_id(1)

        @pl.when(j == 0)
        def _():
            acc_ref[...] = jnp.zeros_like(acc_ref)

        @pl.when((i == 0) & (j == 0))
        def _():
            dg_ref[...] = jnp.zeros_like(dg_ref)

        for dz_ref, w_ref in zip(dz_refs, w_refs):
            acc_ref[...] += _dot(dz_ref[0], w_ref[0]) if w_rows else _dot_nt(dz_ref[0], w_ref[0])

        @pl.when(j == nc - 1)
        def _():
            dxn = acc_ref[...]
            dx, xh = _rms_bwd(dxn, x_ref[...], g_ref[...])
            dx_ref[...] = dx + dres_ref[...]
            dg_ref[...] += jnp.sum(dxn * xh, axis=0, keepdims=True)

    row = pl.BlockSpec((tm, D), lambda i, j: (i, 0))
    one = pl.BlockSpec((1, D), lambda i, j: (0, 0))
    in_specs = [pl.BlockSpec((1, tm, dz.shape[2]), lambda i, j: (j, i, 0)) for dz in dzs]
    in_specs += [pl.BlockSpec((1,) + w.shape[1:], lambda i, j: (j, 0, 0)) for w in ws]
    return _pallas(
        body, name="norm_in_bwd", grid=(S // tm, nc),
        in_specs=in_specs + [row, one, row], out_specs=[row, one],
        out_shape=[_sds((S, D), F32), _sds((1, D), F32)],
        scratch_shapes=[pltpu.VMEM((tm, D), F32)],
        compiler_params=_cp(("arbitrary", "arbitrary")),
    )(*dzs, *ws, x, g, dres)


def _norm_proj(x, g, w, w2=None):
    S, D = x.shape
    N = w.shape[1]
    tm = TOK_TILE

    def body(*refs):
        if w2 is None:
            x_ref, g_ref, w_ref, h_ref, z_ref = refs
        else:
            x_ref, g_ref, w_ref, w2_ref, h_ref, z_ref, z2_ref = refs
        xv = x_ref[...]
        h = (xv * _rms(xv) * g_ref[...]).astype(BF16)
        h_ref[...] = h
        z_ref[...] = _dot(h, w_ref[...]).astype(BF16)
        if w2 is not None:
            z2_ref[...] = _dot(h, w2_ref[...])

    row = pl.BlockSpec((tm, D), lambda i: (i, 0))
    in_specs = [row, pl.BlockSpec((1, D), lambda i: (0, 0)), pl.BlockSpec((D, N), lambda i: (0, 0))]
    out_specs = [row, pl.BlockSpec((tm, N), lambda i: (i, 0))]
    out_shape = [_sds((S, D), BF16), _sds((S, N), BF16)]
    args = [x, g, w]
    if w2 is not None:
        N2 = w2.shape[1]
        in_specs.append(pl.BlockSpec((D, N2), lambda i: (0, 0)))
        out_specs.append(pl.BlockSpec((tm, N2), lambda i: (i, 0)))
        out_shape.append(_sds((S, N2), F32))
        args.append(w2)
    return _pallas(body, name="norm_proj", grid=(S // tm,), in_specs=in_specs, out_specs=out_specs,
                   out_shape=out_shape, compiler_params=_cp(("parallel",)))(*args)


def _proj_res(acts, ws, res):
    S, D = res.shape
    n = len(acts)
    tm = TOK_TILE

    def body(*refs):
        a_refs, w_refs = refs[:n], refs[n:2 * n]
        res_ref, out_ref = refs[2 * n:]
        acc = res_ref[...]
        for a_ref, w_ref in zip(a_refs, w_refs):
            acc = acc + _dot(a_ref[...], w_ref[...])
        out_ref[...] = acc

    row = pl.BlockSpec((tm, D), lambda i: (i, 0))
    in_specs = [pl.BlockSpec((tm, a.shape[1]), lambda i: (i, 0)) for a in acts]
    in_specs += [pl.BlockSpec(w.shape, lambda i: (0, 0)) for w in ws]
    return _pallas(body, name="proj_res", grid=(S // tm,), in_specs=in_specs + [row], out_specs=row,
                   out_shape=_sds((S, D), F32), compiler_params=_cp(("parallel",)))(*acts, *ws, res)


def _matmul_nt(a, w):
    S, K = a.shape
    M = w.shape[0]
    tm = TOK_TILE

    def body(a_ref, w_ref, o_ref):
        o_ref[...] = _dot_nt(a_ref[...].astype(BF16), w_ref[...])

    return _pallas(body, name="matmul_nt", grid=(S // tm,),
                   in_specs=[pl.BlockSpec((tm, K), lambda i: (i, 0)), pl.BlockSpec((M, K), lambda i: (0, 0))],
                   out_specs=pl.BlockSpec((tm, M), lambda i: (i, 0)), out_shape=_sds((S, M), F32),
                   compiler_params=_cp(("parallel",)))(a, w)


def _matmul_tn(a, b, tn):
    S, M = a.shape
    N = b.shape[1]
    tm = TOK_TILE
    nt = S // tm

    def body(a_ref, b_ref, o_ref, acc_ref):
        i = pl.program_id(1)

        @pl.when(i == 0)
        def _():
            acc_ref[...] = jnp.zeros_like(acc_ref)

        acc_ref[...] += _dot_tn(a_ref[...].astype(BF16), b_ref[...].astype(BF16))

        @pl.when(i == nt - 1)
        def _():
            o_ref[0] = acc_ref[...].astype(BF16)

    return _pallas(body, name="matmul_tn", grid=(N // tn, nt),
                   in_specs=[pl.BlockSpec((tm, M), lambda j, i: (i, 0)), pl.BlockSpec((tm, tn), lambda j, i: (i, j))],
                   out_specs=pl.BlockSpec((1, M, tn), lambda j, i: (j, 0, 0)), out_shape=_sds((N // tn, M, tn), BF16),
                   scratch_shapes=[pltpu.VMEM((M, tn), F32)],
                   compiler_params=_cp(("parallel", "arbitrary")))(a, b)


def _conv_a_fwd(z, cw, cb, cn):
    S = z.shape[0]
    C = D_CONV
    tm = TOK_TILE
    hb = tm // HALO_A

    def body(u_ref, gt_ref, up_ref, gp_ref, cw_ref, cb_ref, cn_ref, a_ref, a1_ref, win):
        i = pl.program_id(0)
        prev = up_ref[...].astype(F32) * jax.nn.sigmoid(gp_ref[...].astype(F32))
        win[pl.ds(0, HALO_A), :] = jnp.where(i == 0, 0.0, prev)
        win[pl.ds(HALO_A, tm), :] = u_ref[...].astype(F32) * jax.nn.sigmoid(gt_ref[...].astype(F32))
        acc = jnp.zeros((tm, C), F32)
        for k in range(CONV_A_WIDTH):
            acc = acc + cw_ref[k:k + 1, :] * win[pl.ds(HALO_A - (CONV_A_WIDTH - 1) + k, tm), :]
        a1 = acc + cb_ref[...]
        a1_ref[...] = a1
        a2 = a1 * _rms(a1) * cn_ref[...]
        a_ref[...] = (a2 * jax.nn.sigmoid(a2)).astype(BF16)

    cur = lambda c: pl.BlockSpec((tm, C), lambda i, c=c: (i, c))
    prv = lambda c: pl.BlockSpec((HALO_A, C), lambda i, c=c: (jnp.maximum(i * hb - 1, 0), c))
    vec = pl.BlockSpec((1, C), lambda i: (0, 0))
    return _pallas(body, name="conv_a_fwd", grid=(S // tm,),
                   in_specs=[cur(0), cur(1), prv(0), prv(1), pl.BlockSpec((32, C), lambda i: (0, 0)), vec, vec],
                   out_specs=[pl.BlockSpec((tm, C), lambda i: (i, 0)), pl.BlockSpec((tm, C), lambda i: (i, 0))],
                   out_shape=[_sds((S, C), BF16), _sds((S, C), F32)],
                   scratch_shapes=[pltpu.VMEM((tm + HALO_A, C), F32)],
                   compiler_params=_cp(("parallel",)))(z, z, z, z, cw, cb, cn)


def _conv_a_bwd(da, a1, z, cw, cn):
    S = z.shape[0]
    C = D_CONV
    tm = TOK_TILE
    hb = tm // HALO_A
    nt = S // tm
    W = CONV_A_WIDTH

    def body(da_ref, a1_ref, dan_ref, a1n_ref, u_ref, gt_ref, up_ref, gp_ref, cw_ref, cn_ref,
             duz_ref, dcw_ref, dcb_ref, dcn_ref, win, dwin):
        i = pl.program_id(0)

        @pl.when(i == 0)
        def _():
            dcw_ref[...] = jnp.zeros_like(dcw_ref)
            dcb_ref[...] = jnp.zeros_like(dcb_ref)
            dcn_ref[...] = jnp.zeros_like(dcn_ref)

        cnv = cn_ref[...]

        def da1_of(dav, a1v):
            a2 = a1v * _rms(a1v) * cnv
            da2 = dav * _silu_grad(a2)
            dx, xh = _rms_bwd(da2, a1v, cnv)
            return dx, da2 * xh

        da1, dcn_t = da1_of(da_ref[...], a1_ref[...])
        da1n, _ = da1_of(dan_ref[...], a1n_ref[...])
        dwin[pl.ds(0, tm), :] = da1
        dwin[pl.ds(tm, HALO_A), :] = jnp.where(i == nt - 1, 0.0, da1n)
        dcb_ref[...] += jnp.sum(da1, axis=0, keepdims=True)
        dcn_ref[...] += jnp.sum(dcn_t, axis=0, keepdims=True)

        u = u_ref[...].astype(F32)
        sg = jax.nn.sigmoid(gt_ref[...].astype(F32))
        prev = up_ref[...].astype(F32) * jax.nn.sigmoid(gp_ref[...].astype(F32))
        win[pl.ds(0, HALO_A), :] = jnp.where(i == 0, 0.0, prev)
        win[pl.ds(HALO_A, tm), :] = u * sg

        da0 = jnp.zeros((tm, C), F32)
        for k in range(W):
            da0 = da0 + cw_ref[k:k + 1, :] * dwin[pl.ds(W - 1 - k, tm), :]
            dcw_ref[k:k + 1, :] += jnp.sum(da1 * win[pl.ds(HALO_A - (W - 1) + k, tm), :], axis=0, keepdims=True)
        duz_ref[:, 0:C] = (da0 * sg).astype(BF16)
        duz_ref[:, C:2 * C] = (da0 * u * sg * (1.0 - sg)).astype(BF16)

    cur = lambda c: pl.BlockSpec((tm, C), lambda i, c=c: (i, c))
    prv = lambda c: pl.BlockSpec((HALO_A, C), lambda i, c=c: (jnp.maximum(i * hb - 1, 0), c))
    nxt = pl.BlockSpec((HALO_A, C), lambda i: (jnp.minimum((i + 1) * hb, S // HALO_A - 1), 0))
    vec = pl.BlockSpec((1, C), lambda i: (0, 0))
    return _pallas(body, name="conv_a_bwd", grid=(nt,),
                   in_specs=[cur(0), cur(0), nxt, nxt, cur(0), cur(1), prv(0), prv(1),
                             pl.BlockSpec((32, C), lambda i: (0, 0)), vec],
                   out_specs=[pl.BlockSpec((tm, 2 * C), lambda i: (i, 0)), pl.BlockSpec((32, C), lambda i: (0, 0)), vec, vec],
                   out_shape=[_sds((S, 2 * C), BF16), _sds((32, C), F32), _sds((1, C), F32), _sds((1, C), F32)],
                   scratch_shapes=[pltpu.VMEM((tm + HALO_A, C), F32), pltpu.VMEM((tm + HALO_A, C), F32)],
                   compiler_params=_cp(("arbitrary",)))(da, a1, da, a1, z, z, z, z, cw, cn)


def _forget_scan(fl, bf):
    S, L = fl.shape
    B = SCAN_BLK

    def body(fl_ref, bf_ref, flb_ref, F_ref):
        tri = (lax.broadcasted_iota(jnp.int32, (B, B), 0) >= lax.broadcasted_iota(jnp.int32, (B, B), 1)).astype(F32)

        def step(c, carry):
            rows = pl.ds(pl.multiple_of(c * B, B), B)
            v = fl_ref[rows, :] + bf_ref[...]
            flb_ref[rows, :] = v
            lf = jnp.minimum(v, 0.0) - jnp.log1p(jnp.exp(-jnp.abs(v)))
            cs = jnp.dot(tri, lf, precision=lax.Precision.HIGHEST, preferred_element_type=F32) + carry
            F_ref[rows, :] = cs
            return cs[B - 1:B, :]

        lax.fori_loop(0, S // B, step, jnp.zeros((1, L), F32))

    return _pallas(body, name="forget_scan", out_shape=[_sds((S, L), F32), _sds((S, L), F32)],
                   compiler_params=_cp())(fl, bf)


def _forget_scan_bwd(dF, flb):
    S, L = dF.shape
    B = SCAN_BLK
    nb = S // B

    def body(dF_ref, flb_ref, dfl_ref, db_ref):
        tri = (lax.broadcasted_iota(jnp.int32, (B, B), 0) <= lax.broadcasted_iota(jnp.int32, (B, B), 1)).astype(F32)

        def step(t, carry):
            carry_cs, db = carry
            rows = pl.ds(pl.multiple_of((nb - 1 - t) * B, B), B)
            cs = jnp.dot(tri, dF_ref[rows, :], precision=lax.Precision.HIGHEST, preferred_element_type=F32) + carry_cs
            dfl = cs * jax.nn.sigmoid(-flb_ref[rows, :])
            dfl_ref[rows, :] = dfl
            return cs[0:1, :], db + jnp.sum(dfl, axis=0, keepdims=True)

        _, db = lax.fori_loop(0, nb, step, (jnp.zeros((1, L), F32), jnp.zeros((1, L), F32)))
        db_ref[...] = db

    return _pallas(body, name="forget_scan_bwd", out_shape=[_sds((S, L), F32), _sds((1, L), F32)],
                   compiler_params=_cp())(dF, flb)


def _qk_norm(q, k, qw, kw):
    H, S, Dh = q.shape
    tq = min(QKN_TILE, S)
    scale = 1.0 / math.sqrt(Dh)

    def body(q_ref, k_ref, qw_ref, kw_ref, qn_ref, kn_ref):
        qv = q_ref[0].astype(F32)
        kv = k_ref[0].astype(F32)
        qn_ref[0] = (qv * _rms(qv) * qw_ref[...] * scale).astype(BF16)
        kn_ref[0] = (kv * _rms(kv) * kw_ref[...]).astype(BF16)

    blk = pl.BlockSpec((1, tq, Dh), lambda h, i: (h, i, 0))
    vec = pl.BlockSpec((1, Dh), lambda h, i: (0, 0))
    return _pallas(body, name="qk_norm", grid=(H, S // tq), in_specs=[blk, blk, vec, vec], out_specs=[blk, blk],
                   out_shape=[_sds((H, S, Dh), BF16), _sds((H, S, Dh), BF16)],
                   compiler_params=_cp(("parallel", "parallel")))(q, k, qw, kw)


def _qk_norm_bwd(dqs, dkn, q, k, qw, kw):
    H, S, Dh = q.shape
    tq = min(QKN_TILE, S)
    scale = 1.0 / math.sqrt(Dh)

    def body(dqs_ref, dkn_ref, q_ref, k_ref, qw_ref, kw_ref, dq_ref, dk_ref, dqw_ref, dkw_ref):
        @pl.when((pl.program_id(0) == 0) & (pl.program_id(1) == 0))
        def _():
            dqw_ref[...] = jnp.zeros_like(dqw_ref)
            dkw_ref[...] = jnp.zeros_like(dkw_ref)

        dqn = dqs_ref[0] * scale
        dq, qh = _rms_bwd(dqn, q_ref[0].astype(F32), qw_ref[...])
        dq_ref[0] = dq.astype(BF16)
        dqw_ref[...] += jnp.sum(dqn * qh, axis=0, keepdims=True)
        dkv = dkn_ref[0]
        dk, kh = _rms_bwd(dkv, k_ref[0].astype(F32), kw_ref[...])
        dk_ref[0] = dk.astype(BF16)
        dkw_ref[...] += jnp.sum(dkv * kh, axis=0, keepdims=True)

    blk = pl.BlockSpec((1, tq, Dh), lambda h, i: (h, i, 0))
    vec = pl.BlockSpec((1, Dh), lambda h, i: (0, 0))
    return _pallas(body, name="qk_norm_bwd", grid=(H, S // tq), in_specs=[blk, blk, blk, blk, vec, vec],
                   out_specs=[blk, blk, vec, vec],
                   out_shape=[_sds((H, S, Dh), BF16), _sds((H, S, Dh), BF16), _sds((1, Dh), F32), _sds((1, Dh), F32)],
                   compiler_params=_cp(("arbitrary", "arbitrary")))(dqs, dkn, q, k, qw, kw)


NEG = -1e30


def _causal_mask(t):
    return lax.broadcasted_iota(jnp.int32, (t, t), 0) >= lax.broadcasted_iota(jnp.int32, (t, t), 1)


def _fox_fwd(qs, kn, v, fcol, frow):
    H, S, Dh = qs.shape
    t = ATT_TILE
    nq = S // t

    def body(q_ref, k_ref, v_ref, fc_ref, fr_ref, o_ref, lse_ref):
        i = pl.program_id(1)
        q = q_ref[0]
        fq = fc_ref[0]

        def tile(j, carry, diag):
            m, l, acc = carry
            rows = pl.ds(pl.multiple_of(j * t, t), t)
            s = _dot_nt(q, k_ref[0, rows, :]) + fq - fr_ref[0, j]
            if diag:
                s = jnp.where(_causal_mask(t), s, NEG)
            m_new = jnp.maximum(m, jnp.max(s, axis=-1, keepdims=True))
            p = jnp.exp(s - m_new)
            alpha = jnp.exp(m - m_new)
            l = alpha * l + jnp.sum(p, axis=-1, keepdims=True)
            acc = alpha * acc + _dot(p.astype(BF16), v_ref[0, rows, :])
            return m_new, l, acc

        init = (jnp.full((t, 1), NEG, F32), jnp.zeros((t, 1), F32), jnp.zeros((t, Dh), F32))
        carry = lax.fori_loop(0, i, lambda j, c: tile(j, c, False), init)
        m, l, acc = tile(i, carry, True)
        o_ref[0] = (acc / l).astype(BF16)
        lse_ref[0] = m + jnp.log(l)

    qblk = pl.BlockSpec((1, t, Dh), lambda h, i: (h, i, 0))
    full = pl.BlockSpec((1, S, Dh), lambda h, i: (h, 0, 0))
    col = pl.BlockSpec((1, t, 1), lambda h, i: (h, i, 0))
    return _pallas(body, name="fox_fwd", grid=(H, nq),
                   in_specs=[qblk, full, full, col, pl.BlockSpec((1, nq, 1, t), lambda h, i: (h, 0, 0, 0))],
                   out_specs=[qblk, col], out_shape=[_sds((H, S, Dh), BF16), _sds((H, S, 1), F32)],
                   compiler_params=_cp(("parallel", "parallel")))(qs, kn, v, fcol, frow)


def _fox_bwd(qs, kn, v, o, do, lse, fcol, frow):
    H, S, Dh = qs.shape
    t = ATT_TILE
    nq = S // t

    def body(q_ref, k_ref, v_ref, o_ref, do_ref, lse_ref, fc_ref, fr_ref, dq_ref, dk_ref, dv_ref, dfq_ref, dfk_ref):
        j = pl.program_id(1)

        @pl.when(j == 0)
        def _():
            dq_ref[...] = jnp.zeros_like(dq_ref)
            dfq_ref[...] = jnp.zeros_like(dfq_ref)

        k = k_ref[0]
        vv = v_ref[0]
        fk = fr_ref[0, 0]

        def tile(i, carry, diag):
            dk, dv, dfk = carry
            rows = pl.ds(pl.multiple_of(i * t, t), t)
            q = q_ref[0, rows, :]
            dov = do_ref[0, rows, :]
            delta = jnp.sum(dov.astype(F32) * o_ref[0, rows, :].astype(F32), axis=-1, keepdims=True)
            s = _dot_nt(q, k) + fc_ref[0, rows, :] - fk
            if diag:
                s = jnp.where(_causal_mask(t), s, NEG)
            p = jnp.exp(s - lse_ref[0, rows, :])
            dv = dv + _dot_tn(p.astype(BF16), dov)
            ds = p * (_dot_nt(dov, vv) - delta)
            dsb = ds.astype(BF16)
            dq_ref[0, rows, :] += _dot(dsb, k)
            dk = dk + _dot_tn(dsb, q)
            dfq_ref[0, rows, :] += jnp.sum(ds, axis=-1, keepdims=True)
            dfk = dfk + jnp.sum(ds, axis=0, keepdims=True)
            return dk, dv, dfk

        init = (jnp.zeros((t, Dh), F32), jnp.zeros((t, Dh), F32), jnp.zeros((1, t), F32))
        carry = tile(j, init, True)
        dk, dv, dfk = lax.fori_loop(j + 1, nq, lambda i, c: tile(i, c, False), carry)
        dk_ref[0] = dk
        dv_ref[0] = dv
        dfk_ref[0, 0] = dfk

    full = pl.BlockSpec((1, S, Dh), lambda h, j: (h, 0, 0))
    kblk = pl.BlockSpec((1, t, Dh), lambda h, j: (h, j, 0))
    colf = pl.BlockSpec((1, S, 1), lambda h, j: (h, 0, 0))
    rowb = pl.BlockSpec((1, 1, 1, t), lambda h, j: (h, j, 0, 0))
    return _pallas(body, name="fox_bwd", grid=(H, nq),
                   in_specs=[full, kblk, kblk, full, full, colf, colf, rowb],
                   out_specs=[full, kblk, kblk, colf, rowb],
                   out_shape=[_sds((H, S, Dh), F32), _sds((H, S, Dh), F32), _sds((H, S, Dh), F32),
                              _sds((H, S, 1), F32), _sds((H, nq, 1, t), F32)],
                   compiler_params=_cp(("parallel", "arbitrary"), 56))(qs, kn, v, o, do, lse, fcol, frow)


def _odd_mid_fwd(z, cw):
    S = z.shape[0]
    D = z.shape[1] // 3
    tm = TOK_TILE
    hb = tm // HALO_C
    W = CONV_C_WIDTH

    def body(gb_ref, gc_ref, hh_ref, gcp_ref, hhp_ref, cw_ref, y_ref, win):
        i = pl.program_id(0)
        prev = gcp_ref[...].astype(F32) * hhp_ref[...].astype(F32)
        win[pl.ds(0, HALO_C), :] = jnp.where(i == 0, 0.0, prev)
        win[pl.ds(HALO_C, tm), :] = gc_ref[...].astype(F32) * hh_ref[...].astype(F32)
        c1 = jnp.zeros((tm, D), F32)
        for k in range(W):
            c1 = c1 + cw_ref[k:k + 1, :] * win[pl.ds(HALO_C - (W - 1) + k, tm), :]
        y_ref[...] = (gb_ref[...].astype(F32) * c1).astype(BF16)

    cur = lambda c: pl.BlockSpec((tm, D), lambda i, c=c: (i, c))
    prv = lambda c: pl.BlockSpec((HALO_C, D), lambda i, c=c: (jnp.maximum(i * hb - 1, 0), c))
    return _pallas(body, name="odd_mid_fwd", grid=(S // tm,),
                   in_specs=[cur(0), cur(1), cur(2), prv(1), prv(2), pl.BlockSpec((8, D), lambda i: (0, 0))],
                   out_specs=pl.BlockSpec((tm, D), lambda i: (i, 0)), out_shape=_sds((S, D), BF16),
                   scratch_shapes=[pltpu.VMEM((tm + HALO_C, D), F32)],
                   compiler_params=_cp(("parallel",)))(z, z, z, z, z, cw)


def _odd_mid_bwd(dy, z, cw):
    S = z.shape[0]
    D = z.shape[1] // 3
    tm = TOK_TILE
    hb = tm // HALO_C
    nt = S // tm
    W = CONV_C_WIDTH

    def body(dy_ref, dyn_ref, gb_ref, gbn_ref, gc_ref, hh_ref, gcp_ref, hhp_ref, cw_ref, dz_ref, dcw_ref, win, dwin):
        i = pl.program_id(0)

        @pl.when(i == 0)
        def _():
            dcw_ref[...] = jnp.zeros_like(dcw_ref)

        gc = gc_ref[...].astype(F32)
        hh = hh_ref[...].astype(F32)
        prev = gcp_ref[...].astype(F32) * hhp_ref[...].astype(F32)
        win[pl.ds(0, HALO_C), :] = jnp.where(i == 0, 0.0, prev)
        win[pl.ds(HALO_C, tm), :] = gc * hh
        dyv = dy_ref[...]
        dc1 = dyv * gb_ref[...].astype(F32)
        dwin[pl.ds(0, tm), :] = dc1
        dwin[pl.ds(tm, HALO_C), :] = jnp.where(i == nt - 1, 0.0, dyn_ref[...] * gbn_ref[...].astype(F32))
        c1 = jnp.zeros((tm, D), F32)
        dc0 = jnp.zeros((tm, D), F32)
        for k in range(W):
            tap = win[pl.ds(HALO_C - (W - 1) + k, tm), :]
            c1 = c1 + cw_ref[k:k + 1, :] * tap
            dc0 = dc0 + cw_ref[k:k + 1, :] * dwin[pl.ds(W - 1 - k, tm), :]
            dcw_ref[k:k + 1, :] += jnp.sum(dc1 * tap, axis=0, keepdims=True)
        dz_ref[:, 0:D] = (dyv * c1).astype(BF16)
        dz_ref[:, D:2 * D] = (dc0 * hh).astype(BF16)
        dz_ref[:, 2 * D:3 * D] = (dc0 * gc).astype(BF16)

    cur = lambda c: pl.BlockSpec((tm, D), lambda i, c=c: (i, c))
    prv = lambda c: pl.BlockSpec((HALO_C, D), lambda i, c=c: (jnp.maximum(i * hb - 1, 0), c))
    nxt = pl.BlockSpec((HALO_C, D), lambda i: (jnp.minimum((i + 1) * hb, S // HALO_C - 1), 0))
    return _pallas(body, name="odd_mid_bwd", grid=(nt,),
                   in_specs=[cur(0), nxt, cur(0), nxt, cur(1), cur(2), prv(1), prv(2), pl.BlockSpec((8, D), lambda i: (0, 0))],
                   out_specs=[pl.BlockSpec((tm, 3 * D), lambda i: (i, 0)), pl.BlockSpec((8, D), lambda i: (0, 0))],
                   out_shape=[_sds((S, 3 * D), BF16), _sds((8, D), F32)],
                   scratch_shapes=[pltpu.VMEM((tm + HALO_C, D), F32), pltpu.VMEM((tm + HALO_C, D), F32)],
                   compiler_params=_cp(("arbitrary",)))(dy, dy, z, z, z, z, z, z, cw)


def _loss_head(y, tgt):
    S, D = y.shape
    tm = TOK_TILE

    def body(y_ref, t_ref, dy_ref, l_ref):
        @pl.when(pl.program_id(0) == 0)
        def _():
            l_ref[...] = jnp.zeros_like(l_ref)

        e = y_ref[...] - t_ref[...]
        dy_ref[...] = e * (1.0 / D)
        l_ref[...] += jnp.sum(jnp.sum(e * e, axis=-1, keepdims=True), axis=0, keepdims=True) * (0.5 / D)

    row = pl.BlockSpec((tm, D), lambda i: (i, 0))
    return _pallas(body, name="loss_head", grid=(S // tm,), in_specs=[row, row],
                   out_specs=[row, pl.BlockSpec((1, 1), lambda i: (0, 0))],
                   out_shape=[_sds((S, D), F32), _sds((1, 1), F32)],
                   compiler_params=_cp(("arbitrary",)))(y, tgt)


def _heads(a):
    S = a.shape[0]
    return a.reshape(S, N_HEADS, HEAD_DIM).transpose(1, 0, 2)


def _unheads(a):
    return a.transpose(1, 0, 2).reshape(a.shape[1], D_ATTN)


def _pad_rows(a, rows):
    return jnp.pad(a, ((0, rows - a.shape[0]), (0, 0)))


def _local_step(x, tgt, W, need=lambda block, after: None, done=lambda block, block_grads: None):
    S, D = x.shape
    nq = S // ATT_TILE
    grads = {}
    saved = {}

    def gain_after(gain, token):
        return gain if token is None else gain + token

    def ffn_f(tag, l, xin):
        need((tag, l), xin)
        out, xn, G, U = _ffn_fwd(xin, W[tag + "_norm"][l:l + 1], W[tag + "_w_gate"][l], W[tag + "_w_up"][l],
                                 W[tag + "_w_down"][l])
        saved[(tag, l)] = (xin, xn, G, U)
        return out

    def ffn_b(tag, l, dout):
        xin, xn, G, U = saved[(tag, l)]
        dwg, dwu, dwd, dG, dU = _ffn_bwd_w(dout, xn, G, U, W[tag + "_w_down"][l])
        big = {(tag + "_w_gate", l): dwg, (tag + "_w_up", l): dwu, (tag + "_w_down", l): dwd}
        grads.update(big)
        token = done((tag, l), big)
        dx, dg = _norm_in_bwd([dG, dU], [W[tag + "_w_gate"][l], W[tag + "_w_up"][l]], xin,
                              gain_after(W[tag + "_norm"][l:l + 1], token), dout, w_rows=True)
        grads[(tag + "_norm", l)] = dg
        return dx

    x0a = ffn_f("ffn1", 0, x)
    need(("ev", 0), x0a)
    w_in = W["ev_w_in"]
    w_main, w_f = w_in[:, :2560], jnp.pad(w_in[:, 2560:], ((0, 0), (0, 120)))
    h0, z0, fl = _norm_proj(x0a, W["mix_norm"][0:1], w_main, w_f)
    cw_a = _pad_rows(W["ev_conv_w"], 32)
    a_act, a1 = _conv_a_fwd(z0, cw_a, W["ev_conv_b"], W["ev_conv_norm"])
    flb, Fc = _forget_scan(fl, jnp.pad(W["ev_b_f"], ((0, 0), (0, 120))))
    Ft = Fc[:, :N_HEADS].T
    fcol = Ft.reshape(N_HEADS, S, 1)
    frow = Ft.reshape(N_HEADS, nq, 1, ATT_TILE)
    q_raw, k_raw, v_h = _heads(z0[:, 1024:1536]), _heads(z0[:, 1536:2048]), _heads(z0[:, 2048:2560])
    qs, kn = _qk_norm(q_raw, k_raw, W["ev_q_norm"], W["ev_k_norm"])
    o_h, lse = _fox_fwd(qs, kn, v_h, fcol, frow)
    o_flat = _unheads(o_h)
    w_out_e = W["ev_w_out"]
    x0b = _proj_res([a_act, o_flat], [w_out_e[:D_CONV], w_out_e[D_CONV:]], x0a)
    x0c = ffn_f("ffn2", 0, x0b)
    x1a = ffn_f("ffn1", 1, x0c)
    need(("od", 0), x1a)
    h1, z1 = _norm_proj(x1a, W["mix_norm"][1:2], W["od_w_in"])
    cw_c = _pad_rows(W["od_conv_w"], 8)
    y1 = _odd_mid_fwd(z1, cw_c)
    x1b = _proj_res([y1], [W["od_w_out"]], x1a)
    x1c = ffn_f("ffn2", 1, x1b)
    dy, loss = _loss_head(x1c, tgt)

    d = ffn_b("ffn2", 1, dy)
    dy1 = _matmul_nt(d, W["od_w_out"])
    grads[("od_w_out", 0)] = _matmul_tn(y1, d, D)[0]
    dz1, dcw_c = _odd_mid_bwd(dy1, z1, cw_c)
    grads[("od_conv_w", 0)] = dcw_c[:CONV_C_WIDTH]
    grads[("od_w_in", 0)] = _matmul_tn(h1, dz1, 3 * D // 4)
    token = done(("od", 0), {k: grads[k] for k in (("od_w_out", 0), ("od_w_in", 0))})
    d, dg = _norm_in_bwd([dz1[None]], [W["od_w_in"][None]], x1a, gain_after(W["mix_norm"][1:2], token), d)
    grads[("mix_norm", 1)] = dg
    d = ffn_b("ffn1", 1, d)
    d = ffn_b("ffn2", 0, d)
    dcat = _matmul_nt(d, w_out_e)
    grads[("ev_w_out", 0)] = jnp.concatenate([_matmul_tn(a_act, d, D)[0], _matmul_tn(o_flat, d, D)[0]], axis=0)
    duz, dcw_a, dcb, dcn = _conv_a_bwd(dcat, a1, z0, cw_a, W["ev_conv_norm"])
    grads[("ev_conv_w", 0)] = dcw_a[:CONV_A_WIDTH]
    grads[("ev_conv_b", 0)] = dcb
    grads[("ev_conv_norm", 0)] = dcn
    do_h = _heads(dcat[:, D_CONV:].astype(BF16))
    dqs, dkn, dv, dfq, dfk = _fox_bwd(qs, kn, v_h, o_h, do_h, lse, fcol, frow)
    dq_raw, dk_raw, dqw, dkw = _qk_norm_bwd(dqs, dkn, q_raw, k_raw, W["ev_q_norm"], W["ev_k_norm"])
    grads[("ev_q_norm", 0)] = dqw
    grads[("ev_k_norm", 0)] = dkw
    dF = (dfq.reshape(N_HEADS, S) - dfk.reshape(N_HEADS, S)).T
    dfl, dbf = _forget_scan_bwd(jnp.pad(dF, ((0, 0), (0, 120))), flb)
    grads[("ev_b_f", 0)] = dbf[:, :N_HEADS]
    dz0 = jnp.concatenate([duz, _unheads(dq_raw), _unheads(dk_raw), _unheads(dv.astype(BF16))], axis=1)
    dflb = dfl.astype(BF16)
    gmain = _matmul_tn(h0, dz0, 640)
    gmain = gmain.transpose(1, 0, 2).reshape(D, 2560)
    gf = _matmul_tn(h0, dflb, 128)[0][:, :N_HEADS]
    grads[("ev_w_in", 0)] = jnp.concatenate([gmain, gf], axis=1)
    token = done(("ev", 0), {k: grads[k] for k in (("ev_w_out", 0), ("ev_w_in", 0))})
    d, dg = _norm_in_bwd([dz0[None], dflb[None]], [w_main[None], w_f[None]], x0a, gain_after(W["mix_norm"][0:1], token), d)
    grads[("mix_norm", 0)] = dg
    d = ffn_b("ffn1", 0, d)
    return loss, d, grads


def _place():
    x, y, c = lax.axis_index("x"), lax.axis_index("y"), lax.axis_index("c")
    chips = [(1 - x, y), (x, 1 - y), (1 - x, 1 - y)]
    return x, y, c, chips


def _remote(src, dst, send_sem, recv_sem, to):
    return pltpu.make_async_remote_copy(src_ref=src, dst_ref=dst, send_sem=send_sem, recv_sem=recv_sem,
                                        device_id=to, device_id_type=MESH)


HBM = pl.BlockSpec(memory_space=pltpu.HBM)
SEM = pl.BlockSpec(memory_space=pltpu.SEMAPHORE)
EFFECT = pltpu.SideEffectType.DATAFLOW_SIDE_EFFECTING


def _in_hbm(a):
    return pltpu.with_memory_space_constraint(a, pltpu.HBM)


def _ag_start(bufs):
    n = len(bufs)

    def body(*refs):
        send_sems, recv_sems = refs[n], refs[n + 1]
        outs = refs[n + 2:]
        x, y, c, chips = _place()
        me = 2 * x + y
        for a in [n - 1] + list(range(n - 1)):
            if a == n - 1:
                blk = outs[a].at[me]
            else:
                h = outs[a].shape[1] // 2
                blk = outs[a].at[me, pl.ds(c * h, h)]
            for jj, (px, py) in enumerate(chips):
                _remote(blk, blk, send_sems.at[3 * a + jj], recv_sems.at[3 * a + jj], (px, py, c)).start()

    return _pallas(
        body, name="gather_start",
        out_shape=[pltpu.SemaphoreType.DMA((3 * n,)), pltpu.SemaphoreType.DMA((3 * n,))] + [pltpu.HBM(b.shape, b.dtype) for b in bufs],
        in_specs=[HBM] * n, out_specs=[SEM, SEM] + [HBM] * n, input_output_aliases={a: 2 + a for a in range(n)},
        compiler_params=pltpu.CompilerParams(has_side_effects=EFFECT),
    )(*[_in_hbm(b) for b in bufs])


def _ag_mid(g, ici_send, ici_recv, bufs, idx, taps, n_big, after):
    n = len(bufs)
    arrs = list(bufs) + ([taps] if taps is not None else [])
    m = len(arrs)

    def body(*refs):
        ici_s, ici_r = refs[0], refs[1]
        d_send, d_recv = refs[m + 3], refs[m + 4]
        outs = refs[m + 5:]
        x, y, c, chips = _place()
        me = 2 * x + y
        for i in range(m):
            a = idx[i] if i < n else n_big
            for jj, (px, py) in enumerate(chips):
                k = 3 * a + jj
                if i < n:
                    h = outs[i].shape[1] // 2
                    mine, blk = outs[i].at[me, pl.ds(c * h, h)], outs[i].at[2 * px + py, pl.ds(c * h, h)]
                else:
                    mine, blk = outs[i].at[me], outs[i].at[2 * px + py]
                _remote(mine, mine, ici_s.at[k], ici_r.at[k], (px, py, c)).wait_send()
                _remote(blk, blk, ici_s.at[k], ici_r.at[k], (px, py, c)).wait_recv()
                if i < n:
                    _remote(blk, blk, d_send.at[3 * i + jj], d_recv.at[3 * i + jj], (x, y, 1 - c)).start()

    return _pallas(
        body, name=f"gather_pass_on_{g}",
        out_shape=[pltpu.SemaphoreType.DMA((3 * n,)), pltpu.SemaphoreType.DMA((3 * n,))] + [pltpu.HBM(b.shape, b.dtype) for b in arrs],
        in_specs=[SEM, SEM] + [HBM] * m + [ANY], out_specs=[SEM, SEM] + [HBM] * m,
        input_output_aliases={2 + i: 2 + i for i in range(m)},
        compiler_params=pltpu.CompilerParams(has_side_effects=EFFECT),
    )(ici_send, ici_recv, *arrs, after)


def _ag_wait(g, d_send, d_recv, arrs, n, after):
    m = len(arrs)

    def body(*refs):
        d_s, d_r = refs[0], refs[1]
        outs = refs[m + 3:]
        x, y, c, chips = _place()
        for i in range(n):
            h = outs[i].shape[1] // 2
            for jj, (px, py) in enumerate(chips):
                sent = outs[i].at[2 * px + py, pl.ds(c * h, h)]
                got = outs[i].at[2 * px + py, pl.ds((1 - c) * h, h)]
                _remote(sent, sent, d_s.at[3 * i + jj], d_r.at[3 * i + jj], (x, y, 1 - c)).wait_send()
                _remote(got, got, d_s.at[3 * i + jj], d_r.at[3 * i + jj], (x, y, 1 - c)).wait_recv()

    return _pallas(
        body, name=f"gather_wait_{g}", out_shape=[pltpu.HBM(b.shape, b.dtype) for b in arrs],
        in_specs=[SEM, SEM] + [HBM] * m + [ANY], out_specs=[HBM] * m,
        input_output_aliases={2 + i: i for i in range(m)},
        compiler_params=pltpu.CompilerParams(has_side_effects=EFFECT),
    )(d_send, d_recv, *arrs, after)


def _pair_exchange(gs):
    n = len(gs)

    def body(*refs):
        ins, outs = refs[:n], refs[n:2 * n]
        send_sems, recv_sems = refs[2 * n:]
        x, y, c, _ = _place()
        cps = []
        for a in range(n):
            h = ins[a].shape[1] // 2
            cps.append(_remote(ins[a].at[:, pl.ds((1 - c) * h, h)], outs[a], send_sems.at[a], recv_sems.at[a], (x, y, 1 - c)))
        for cp in cps:
            cp.start()
        for cp in cps:
            cp.wait()

    return _pallas(body, name="grad_pair_exchange", in_specs=[ANY] * n, out_specs=[ANY] * n,
                   out_shape=[_sds((4, g.shape[1] // 2, g.shape[2]), g.dtype) for g in gs],
                   scratch_shapes=[pltpu.SemaphoreType.DMA((n,)), pltpu.SemaphoreType.DMA((n,))])(*gs)


def _pair_add(g, other, c_arr):
    _, R, C = g.shape
    h = R // 2

    def body(c_ref, g_ref, o_ref, out_ref):
        out_ref[...] = (g_ref[...].astype(F32) + o_ref[...].astype(F32)).astype(BF16)

    grid_spec = pltpu.PrefetchScalarGridSpec(
        num_scalar_prefetch=1, grid=(4,),
        in_specs=[pl.BlockSpec((1, h, C), lambda k, c_ref: (k, c_ref[0], 0)), pl.BlockSpec((1, h, C), lambda k, c_ref: (k, 0, 0))],
        out_specs=pl.BlockSpec((1, h, C), lambda k, c_ref: (k, 0, 0)))
    return _pallas(body, name="grad_pair_add", grid_spec=grid_spec, out_shape=_sds((4, h, C), BF16),
                   compiler_params=_cp(("parallel",)))(c_arr, g, other)


def _chip_start(g, ss):
    n = len(ss)
    zones = [lax.empty((3,) + s.shape[1:], s.dtype) for s in ss]

    def body(*refs):
        send_sems, recv_sems = refs[2 * n], refs[2 * n + 1]
        src, dst = refs[2 * n + 2:3 * n + 2], refs[3 * n + 2:4 * n + 2]
        token = refs[4 * n + 2]
        x, y, c, chips = _place()
        for a in range(n):
            for jj, (px, py) in enumerate(chips):
                k = 3 * a + jj
                _remote(src[a].at[2 * px + py], dst[a].at[jj], send_sems.at[k], recv_sems.at[k], (px, py, c)).start()
        token[...] = jnp.zeros_like(token)

    return _pallas(
        body, name=f"grad_chip_start_{g}",
        out_shape=[pltpu.SemaphoreType.DMA((3 * n,)), pltpu.SemaphoreType.DMA((3 * n,))]
        + [pltpu.HBM(a.shape, a.dtype) for a in ss + zones] + [_sds((8, 128), F32)],
        in_specs=[HBM] * (2 * n), out_specs=[SEM, SEM] + [HBM] * (2 * n) + [pl.BlockSpec(memory_space=pltpu.VMEM)],
        input_output_aliases={i: 2 + i for i in range(2 * n)},
        compiler_params=pltpu.CompilerParams(has_side_effects=EFFECT),
    )(*[_in_hbm(a) for a in ss + zones])


def _chip_wait(sends, recvs, counts, ss, zones, after):
    nb, n = len(sends), len(ss)

    def body(*refs):
        s_refs, r_refs = refs[:nb], refs[nb:2 * nb]
        outs = refs[2 * nb + 2 * n + 1:]
        src, dst = outs[:n], outs[n:]
        x, y, c, chips = _place()
        a = 0
        for b in range(nb):
            for i in range(counts[b]):
                for jj, (px, py) in enumerate(chips):
                    k = 3 * i + jj
                    _remote(src[a].at[2 * px + py], dst[a].at[jj], s_refs[b].at[k], r_refs[b].at[k], (px, py, c)).wait()
                a += 1

    return _pallas(
        body, name="grad_chip_wait", out_shape=[pltpu.HBM(a.shape, a.dtype) for a in ss + zones],
        in_specs=[SEM] * (2 * nb) + [HBM] * (2 * n) + [ANY], out_specs=[HBM] * (2 * n),
        input_output_aliases={2 * nb + i: i for i in range(2 * n)},
        compiler_params=pltpu.CompilerParams(has_side_effects=EFFECT),
    )(*sends, *recvs, *ss, *zones, after)


def _chip_sum(s, r, where, dest, l, L):
    _, h, C = s.shape
    tr = h // 2

    def body(k_ref, s_ref, r_ref, *rest):
        out_ref = rest[-1]
        acc = s_ref[0].astype(F32)
        for jj in range(3):
            acc = acc + r_ref[jj].astype(F32)
        out_ref[...] = acc

    in_specs = [pl.BlockSpec((1, tr, C), lambda i, k_ref: (k_ref[0], i, 0)), pl.BlockSpec((3, tr, C), lambda i, k_ref: (0, i, 0))]
    args = [where, s, r]
    alias = {}
    if dest is not None:
        in_specs.append(ANY)
        args.append(dest)
        alias = {3: 0}
    grid_spec = pltpu.PrefetchScalarGridSpec(
        num_scalar_prefetch=1, grid=(2,), in_specs=in_specs,
        out_specs=pl.BlockSpec((None, tr, C), lambda i, k_ref: (l, 2 * k_ref[1] + i, 0)))
    return _pallas(body, name="grad_chip_sum", grid_spec=grid_spec, out_shape=_sds((L, 2 * h, C), F32),
                   input_output_aliases=alias, compiler_params=_cp(("arbitrary",)))(*args)


def _pair_share(bufs, layout):
    n = len(layout)
    n_out = len(bufs)

    def body(*refs):
        outs = refs[n_out:2 * n_out]
        send_sems, recv_sems = refs[2 * n_out:]
        x, y, c, _ = _place()
        cps = []
        for a, (o, l) in enumerate(layout):
            h = outs[o].shape[1] // 2
            blk = outs[o].at[l, pl.ds(c * h, h)]
            cps.append(_remote(blk, blk, send_sems.at[a], recv_sems.at[a], (x, y, 1 - c)))
        for cp in cps:
            cp.start()
        for a, (o, l) in enumerate(layout):
            h = outs[o].shape[1] // 2
            blk = outs[o].at[l, pl.ds((1 - c) * h, h)]
            _remote(blk, blk, send_sems.at[a], recv_sems.at[a], (x, y, 1 - c)).wait_recv()
        for cp in cps:
            cp.wait_send()

    return _pallas(body, name="grad_pair_share", in_specs=[ANY] * n_out, out_specs=[ANY] * n_out,
                   out_shape=[_sds(b.shape, b.dtype) for b in bufs], input_output_aliases={o: o for o in range(n_out)},
                   scratch_shapes=[pltpu.SemaphoreType.DMA((n,)), pltpu.SemaphoreType.DMA((n,))])(*bufs)


def _small_all_reduce(packed):
    P, L = packed.shape

    def body(in_ref, out_ref, slots, send_sems, recv_sems):
        x, y, c, _ = _place()
        me = 4 * x + 2 * y + c
        slots[me] = in_ref[...]
        cps = []
        for r in range(1, 8):
            px = 1 - x if r & 4 else x
            py = 1 - y if r & 2 else y
            pc = 1 - c if r & 1 else c
            cps.append(_remote(in_ref, slots.at[me], send_sems.at[r - 1], recv_sems.at[r - 1], (px, py, pc)))
        for cp in cps:
            cp.start()
        for r in range(1, 8):
            px = 1 - x if r & 4 else x
            py = 1 - y if r & 2 else y
            pc = 1 - c if r & 1 else c
            blk = slots.at[4 * px + 2 * py + pc]
            _remote(blk, blk, send_sems.at[r - 1], recv_sems.at[r - 1], (px, py, pc)).wait_recv()
        for cp in cps:
            cp.wait_send()
        acc = slots[0]
        for k in range(1, 8):
            acc = acc + slots[k]
        out_ref[...] = acc

    vm = pl.BlockSpec(memory_space=pltpu.VMEM)
    return _pallas(body, name="small_all_reduce", in_specs=[vm], out_specs=vm, out_shape=_sds((P, L), F32),
                   scratch_shapes=[pltpu.VMEM((8, P, L), F32), pltpu.SemaphoreType.DMA((7,)), pltpu.SemaphoreType.DMA((7,))])(packed)


def _adamw_math(w, g, m, v):
    m = ADAM_B1 * m + (1.0 - ADAM_B1) * g
    v = ADAM_B2 * v + (1.0 - ADAM_B2) * (g * g)
    m_hat = m / (1.0 - ADAM_B1 ** ADAM_STEP)
    v_hat = v / (1.0 - ADAM_B2 ** ADAM_STEP)
    delta = -ADAM_LR * (m_hat / (jnp.sqrt(v_hat) + ADAM_EPS) + ADAM_WD * w)
    return delta, m, v


def _adamw(w, g, m, v):
    shape = w.shape
    C = shape[-1]
    rows = math.prod(shape[:-1])
    tr = next(t for t in (512, 352, 256, 128, 64, 32, 16, 8, rows) if rows % t == 0)
    w2, g2, m2, v2 = (a.reshape(rows, C) for a in (w, g, m, v))

    def body(w_ref, g_ref, m_ref, v_ref, d_ref, nm_ref, nv_ref):
        d, nm, nv = _adamw_math(w_ref[...], g_ref[...], m_ref[...], v_ref[...])
        d_ref[...] = d
        nm_ref[...] = nm
        nv_ref[...] = nv

    blk = pl.BlockSpec((tr, C), lambda i: (i, 0))
    outs = _pallas(body, name="adamw", grid=(rows // tr,), in_specs=[blk] * 4, out_specs=[blk] * 3,
                   out_shape=[_sds((rows, C), F32)] * 3, compiler_params=_cp(("parallel",)))(w2, g2, m2, v2)
    return tuple(o.reshape(shape) for o in outs)


WEIGHTS = ["ffn1_norm", "ffn1_w_gate", "ffn1_w_up", "ffn1_w_down", "mix_norm", "ffn2_norm", "ffn2_w_gate", "ffn2_w_up",
           "ffn2_w_down", "ev_w_in", "ev_b_f", "ev_conv_w", "ev_conv_b", "ev_conv_norm", "ev_q_norm", "ev_k_norm",
           "ev_w_out", "od_w_in", "od_conv_w", "od_w_out"]
BIG = ([("ffn1_w_gate", 0), ("ffn1_w_up", 0), ("ffn1_w_down", 0), ("ev_w_in", 0), ("ev_w_out", 0),
        ("ffn2_w_gate", 0), ("ffn2_w_up", 0), ("ffn2_w_down", 0)]
       + [("ffn1_w_gate", 1), ("ffn1_w_up", 1), ("ffn1_w_down", 1), ("od_w_in", 0), ("od_w_out", 0),
          ("ffn2_w_gate", 1), ("ffn2_w_up", 1), ("ffn2_w_down", 1)])
TRANSPOSED = ("ffn1_w_gate", "ffn1_w_up", "ffn2_w_gate", "ffn2_w_up")
BLOCKS = [("ffn1", 0), ("ev", 0), ("ffn2", 0), ("ffn1", 1), ("od", 0), ("ffn2", 1)]
BLOCK_OF = {(name, l): (name.split("_w_")[0], l) for name, l in BIG}
BIG_NAMES = ["ffn1_w_gate", "ffn1_w_up", "ffn1_w_down", "ffn2_w_gate", "ffn2_w_up", "ffn2_w_down",
             "ev_w_in", "ev_w_out", "od_w_in", "od_w_out"]
SMALL = [("ffn1_norm", 16), ("mix_norm", 16), ("ffn2_norm", 16), ("ev_b_f", 8), ("ev_conv_w", 128), ("ev_conv_b", 8),
         ("ev_conv_norm", 8), ("ev_q_norm", 8), ("ev_k_norm", 8), ("od_conv_w", 24)]


def _to_lanes(a, rows):
    flat = a.reshape(-1)
    return jnp.pad(flat, (0, rows * 128 - flat.shape[0])).reshape(rows, 128)


def kernel(x, ffn1_norm, ffn1_w_gate, ffn1_w_up, ffn1_w_down, mix_norm, ffn2_norm, ffn2_w_gate, ffn2_w_up, ffn2_w_down, ev_w_in, ev_b_f, ev_conv_w, ev_conv_b, ev_conv_norm, ev_q_norm, ev_k_norm, ev_w_out, od_w_in, od_conv_w, od_w_out, loss_target, m_ffn1_norm, m_ffn1_w_gate, m_ffn1_w_up, m_ffn1_w_down, m_mix_norm, m_ffn2_norm, m_ffn2_w_gate, m_ffn2_w_up, m_ffn2_w_down, m_ev_w_in, m_ev_b_f, m_ev_conv_w, m_ev_conv_b, m_ev_conv_norm, m_ev_q_norm, m_ev_k_norm, m_ev_w_out, m_od_w_in, m_od_conv_w, m_od_w_out, v_ffn1_norm, v_ffn1_w_gate, v_ffn1_w_up, v_ffn1_w_down, v_mix_norm, v_ffn2_norm, v_ffn2_w_gate, v_ffn2_w_up, v_ffn2_w_down, v_ev_w_in, v_ev_b_f, v_ev_conv_w, v_ev_conv_b, v_ev_conv_norm, v_ev_q_norm, v_ev_k_norm, v_ev_w_out, v_od_w_in, v_od_conv_w, v_od_w_out):
    P = dict(ffn1_norm=ffn1_norm, ffn1_w_gate=ffn1_w_gate, ffn1_w_up=ffn1_w_up, ffn1_w_down=ffn1_w_down, mix_norm=mix_norm,
             ffn2_norm=ffn2_norm, ffn2_w_gate=ffn2_w_gate, ffn2_w_up=ffn2_w_up, ffn2_w_down=ffn2_w_down, ev_w_in=ev_w_in,
             ev_b_f=ev_b_f, ev_conv_w=ev_conv_w, ev_conv_b=ev_conv_b, ev_conv_norm=ev_conv_norm, ev_q_norm=ev_q_norm,
             ev_k_norm=ev_k_norm, ev_w_out=ev_w_out, od_w_in=od_w_in, od_conv_w=od_conv_w, od_w_out=od_w_out)
    M = dict(zip(WEIGHTS, [m_ffn1_norm, m_ffn1_w_gate, m_ffn1_w_up, m_ffn1_w_down, m_mix_norm, m_ffn2_norm, m_ffn2_w_gate,
                           m_ffn2_w_up, m_ffn2_w_down, m_ev_w_in, m_ev_b_f, m_ev_conv_w, m_ev_conv_b, m_ev_conv_norm,
                           m_ev_q_norm, m_ev_k_norm, m_ev_w_out, m_od_w_in, m_od_conv_w, m_od_w_out]))
    V = dict(zip(WEIGHTS, [v_ffn1_norm, v_ffn1_w_gate, v_ffn1_w_up, v_ffn1_w_down, v_mix_norm, v_ffn2_norm, v_ffn2_w_gate,
                           v_ffn2_w_up, v_ffn2_w_down, v_ev_w_in, v_ev_b_f, v_ev_conv_w, v_ev_conv_b, v_ev_conv_norm,
                           v_ev_q_norm, v_ev_k_norm, v_ev_w_out, v_od_w_in, v_od_conv_w, v_od_w_out]))
    for name in TRANSPOSED:
        P[name], M[name], V[name] = (jnp.swapaxes(a, 1, 2) for a in (P[name], M[name], V[name]))
    S, D = x.shape[1], x.shape[2]
    chip = 2 * lax.axis_index("x") + lax.axis_index("y")
    core = lax.axis_index("c")

    def own_slot(shard):
        return lax.dynamic_update_slice(jnp.zeros((4,) + shard.shape, shard.dtype), shard[None], (chip, 0, 0))

    taps = jnp.concatenate([_to_lanes(_pad_rows(ev_conv_w[0], 32), 32), _to_lanes(_pad_rows(od_conv_w[0], 8), 16)], axis=0)
    ici_send, ici_recv, *bufs = _ag_start([own_slot(P[name][l].astype(BF16)) for name, l in BIG] + [own_slot(taps)])
    cols = lambda a: a.transpose(1, 0, 2).reshape(a.shape[1], 4 * a.shape[2])
    W = {k: P[k] for k in ("ffn1_norm", "mix_norm", "ffn2_norm", "ev_b_f", "ev_q_norm", "ev_k_norm")}
    W["ev_conv_b"], W["ev_conv_norm"] = ev_conv_b, ev_conv_norm
    for tag in ("ffn1", "ffn2"):
        for kind in ("_w_gate", "_w_up", "_w_down"):
            W[tag + kind] = [None, None]
    passing = {}

    def pass_on(g, after):
        idx = [i for i, k in enumerate(BIG) if BLOCK_OF[k] == BLOCKS[g]]
        keys = [BIG[i] for i in idx] + (["taps"] if BLOCKS[g] == ("ev", 0) else [])
        passing[g] = (keys, _ag_mid(g, ici_send, ici_recv, [bufs[i] for i in idx], idx,
                                    bufs[-1] if BLOCKS[g] == ("ev", 0) else None, len(BIG), after))

    def need(block, after):
        g = BLOCKS.index(block)
        if g not in passing:
            pass_on(g, after)
        keys, (d_send, d_recv, *thru) = passing.pop(g)
        got = dict(zip(keys, _ag_wait(g, d_send, d_recv, thru, len(keys) - ("taps" in keys), after)))
        if 1 <= g < len(BLOCKS) - 1:
            pass_on(g + 1, after)
        for key, a in got.items():
            if key == "taps":
                continue
            name, l = key
            if name.startswith("ffn"):
                W[name][l] = a
            elif name.endswith("_w_in"):
                W[name] = cols(a)
            elif name.endswith("_w_out"):
                W[name] = a.reshape(4 * a.shape[1], D)
        if block == ("ev", 0):
            taps_all = got["taps"]
            W["ev_conv_w"] = cols(taps_all[:, :32].reshape(4, 32, 128))[:CONV_A_WIDTH]
            W["od_conv_w"] = cols(taps_all[:, 32:48].reshape(4, 8, 256))[:CONV_C_WIDTH]

    rows = lambda a: a.reshape(4, a.shape[0] // 4, a.shape[1])
    colsh = lambda a: a.reshape(a.shape[0], 4, a.shape[1] // 4).transpose(1, 0, 2)
    c_arr = core.reshape(1).astype(jnp.int32)
    where = jnp.stack([chip, core]).astype(jnp.int32)
    in_flight = []

    def done(block, block_grads):
        g = BLOCKS.index(block)
        keys = list(block_grads)
        gs = []
        for name, l in keys:
            a = block_grads[(name, l)]
            gs.append(colsh(a) if name == "ev_w_in" else rows(a) if name.endswith("_w_out") else a)
        others = _pair_exchange(gs)
        sums = [_pair_add(a, o, c_arr) for a, o in zip(gs, others)]
        send, recv, *rest = _chip_start(g, sums)
        in_flight.append((keys, send, recv, rest[:len(keys)], rest[len(keys):2 * len(keys)]))
        return rest[-1][0:1, 0:1]

    loss, grad_x, grads = _local_step(x[0], loss_target[0], W, need, done)

    order = [k for keys, *_ in in_flight for k in keys]
    landed = _chip_wait([f[1] for f in in_flight], [f[2] for f in in_flight], [len(f[0]) for f in in_flight],
                        [a for f in in_flight for a in f[3]], [a for f in in_flight for a in f[4]], grad_x)
    sums, recvd = landed[:len(order)], landed[len(order):]
    stacked = {}
    for (name, l), s, r in zip(order, sums, recvd):
        stacked[name] = _chip_sum(s, r, where, stacked.get(name), l, P[name].shape[0])
    layout = [(BIG_NAMES.index(name), l) for name, l in order]
    big_grads = dict(zip(BIG_NAMES, _pair_share([stacked[name] for name in BIG_NAMES], layout)))

    def small_grad(name):
        if name.endswith("_norm") and name[:3] in ("ffn", "mix"):
            return jnp.concatenate([grads[(name, 0)], grads[(name, 1)]], axis=0)
        return grads[(name, 0)]

    packed = jnp.concatenate([_to_lanes(small_grad(name), r) for name, r in SMALL], axis=0)
    total = _small_all_reduce(packed)
    small_grads, at = {}, 0
    for name, r in SMALL:
        part = total[at:at + r].reshape(-1)
        at += r
        if name == "ev_conv_w":
            full_g = part[:CONV_A_WIDTH * D_CONV].reshape(CONV_A_WIDTH, D_CONV)
            small_grads[name] = lax.dynamic_slice_in_dim(full_g, chip * (D_CONV // 4), D_CONV // 4, axis=1)[None]
        elif name == "od_conv_w":
            full_g = part[:CONV_C_WIDTH * D].reshape(CONV_C_WIDTH, D)
            small_grads[name] = lax.dynamic_slice_in_dim(full_g, chip * (D // 4), D // 4, axis=1)[None]
        else:
            small_grads[name] = part[:math.prod(P[name].shape)].reshape(P[name].shape)

    grad_w, delta_w, new_m, new_v = [], [], [], []
    for name in WEIGHTS:
        g = big_grads[name] if name in big_grads else small_grads[name]
        outs = (g,) + _adamw(P[name], g, M[name], V[name])
        if name in TRANSPOSED:
            outs = tuple(jnp.swapaxes(a, 1, 2) for a in outs)
        for acc, a in zip((grad_w, delta_w, new_m, new_v), outs):
            acc.append(a)
    loss_all = lax.psum(loss[0, 0], ("x", "y", "c"))
    return (loss_all, grad_x[None], *grad_w, *delta_w, *new_m, *new_v)
```

```python
import functools
import math

import jax
import jax.numpy as jnp
from jax import lax
from jax.experimental import pallas as pl
from jax.experimental.pallas import tpu as pltpu

F32, BF16 = jnp.float32, jnp.bfloat16
EPS = 1e-6
FFN_RES = 0.5
N_HEADS, HEAD_DIM = 8, 64
D_CONV = 512
D_ATTN = N_HEADS * HEAD_DIM
CONV_A_WIDTH, CONV_C_WIDTH = 31, 3
ADAM_LR, ADAM_B1, ADAM_B2, ADAM_EPS, ADAM_WD, ADAM_STEP = 0.001, 0.9, 0.999, 1e-08, 0.01, 10
MESH = pl.DeviceIdType.MESH
ANY = pl.BlockSpec(memory_space=pl.ANY)

TOK_TILE = 512
DW_TILE = 1024
ATT_TILE = 512
QKN_TILE = 2048
HALO_A, HALO_C = 32, 16
SCAN_BLK = 256
MIB = 2 ** 20


def _pallas(body, **kw):
    return pl.pallas_call(body, **kw)


def _cp(sem=None, vmem_mib=48):
    return pltpu.CompilerParams(dimension_semantics=sem, vmem_limit_bytes=vmem_mib * MIB)


def _dot(a, b):
    return jnp.dot(a, b, preferred_element_type=F32)


def _dot_nt(a, b):
    return lax.dot_general(a, b, (((1,), (1,)), ((), ())), preferred_element_type=F32)


def _dot_tn(a, b):
    return lax.dot_general(a, b, (((0,), (0,)), ((), ())), preferred_element_type=F32)


def _sds(shape, dtype):
    return jax.ShapeDtypeStruct(shape, dtype)


def _rms(x):
    return lax.rsqrt(jnp.mean(x * x, axis=-1, keepdims=True) + EPS)


def _rms_bwd(dy, x, g):
    r = _rms(x)
    xh = x * r
    dxh = dy * g
    dx = r * (dxh - xh * jnp.mean(dxh * xh, axis=-1, keepdims=True))
    return dx, xh


def _silu_grad(z):
    s = jax.nn.sigmoid(z)
    return s * (1.0 + z * (1.0 - s))


def _ffn_fwd(x, g, wg, wu, wd):
    S, D = x.shape
    nc, Fs, _ = wd.shape
    tm = TOK_TILE

    def body(x_ref, g_ref, wg_ref, wu_ref, wd_ref, out_ref, xn_ref, G_ref, U_ref, acc_ref):
        j = pl.program_id(1)

        @pl.when(j == 0)
        def _():
            xv = x_ref[...]
            xn_ref[...] = (xv * _rms(xv) * g_ref[...]).astype(BF16)
            acc_ref[...] = jnp.zeros_like(acc_ref)

        xn = xn_ref[...]
        G = _dot_nt(xn, wg_ref[0])
        U = _dot_nt(xn, wu_ref[0])
        G_ref[0] = G.astype(BF16)
        U_ref[0] = U.astype(BF16)
        H = (G * jax.nn.sigmoid(G) * U).astype(BF16)
        acc_ref[...] += _dot(H, wd_ref[0])

        @pl.when(j == nc - 1)
        def _():
            out_ref[...] = x_ref[...] + FFN_RES * acc_ref[...]

    row = pl.BlockSpec((tm, D), lambda i, j: (i, 0))
    return _pallas(
        body, name="ffn_fwd", grid=(S // tm, nc),
        in_specs=[row, pl.BlockSpec((1, D), lambda i, j: (0, 0)),
                  pl.BlockSpec((1, Fs, D), lambda i, j: (j, 0, 0)), pl.BlockSpec((1, Fs, D), lambda i, j: (j, 0, 0)),
                  pl.BlockSpec((1, Fs, D), lambda i, j: (j, 0, 0))],
        out_specs=[row, row, pl.BlockSpec((1, tm, Fs), lambda i, j: (j, i, 0)),
                   pl.BlockSpec((1, tm, Fs), lambda i, j: (j, i, 0))],
        out_shape=[_sds((S, D), F32), _sds((S, D), BF16), _sds((nc, S, Fs), BF16), _sds((nc, S, Fs), BF16)],
        scratch_shapes=[pltpu.VMEM((tm, D), F32)],
        compiler_params=_cp(("parallel", "arbitrary")),
    )(x, g, wg, wu, wd)


def _ffn_bwd_w(dout, xn, G, U, wd):
    S, D = dout.shape
    nc, _, Fs = G.shape
    tm = TOK_TILE
    nt = S // tm

    def body(do_ref, xn_ref, G_ref, U_ref, wd_ref, dwg_ref, dwu_ref, dwd_ref, dG_ref, dU_ref, ag, au, ad):
        i = pl.program_id(1)

        @pl.when(i == 0)
        def _():
            ag[...] = jnp.zeros_like(ag)
            au[...] = jnp.zeros_like(au)
            ad[...] = jnp.zeros_like(ad)

        do = (FFN_RES * do_ref[...]).astype(BF16)
        Gv = G_ref[0].astype(F32)
        Uv = U_ref[0].astype(F32)
        dH = _dot_nt(do, wd_ref[0])
        sg = jax.nn.sigmoid(Gv)
        act = Gv * sg
        H = (act * Uv).astype(BF16)
        dU = (dH * act).astype(BF16)
        dG = (dH * Uv * (sg * (1.0 + Gv * (1.0 - sg)))).astype(BF16)
        dG_ref[0] = dG
        dU_ref[0] = dU
        xnv = xn_ref[...]
        ag[...] += _dot_tn(dG, xnv)
        au[...] += _dot_tn(dU, xnv)
        ad[...] += _dot_tn(H, do)

        @pl.when(i == nt - 1)
        def _():
            dwg_ref[0] = ag[...].astype(BF16)
            dwu_ref[0] = au[...].astype(BF16)
            dwd_ref[0] = ad[...].astype(BF16)

    row = pl.BlockSpec((tm, D), lambda j, i: (i, 0))
    hid = pl.BlockSpec((1, tm, Fs), lambda j, i: (j, i, 0))
    wrow = pl.BlockSpec((1, Fs, D), lambda j, i: (j, 0, 0))
    return _pallas(
        body, name="ffn_bwd_w", grid=(nc, nt),
        in_specs=[row, row, hid, hid, wrow],
        out_specs=[wrow, wrow, wrow, hid, hid],
        out_shape=[_sds((nc, Fs, D), BF16)] * 3 + [_sds((nc, S, Fs), BF16)] * 2,
        scratch_shapes=[pltpu.VMEM((Fs, D), F32)] * 3,
        compiler_params=_cp(("parallel", "arbitrary"), 56),
    )(dout, xn, G, U, wd)


def _ffn_bwd_x(dout, x, g, G, U, wg, wu, wd, after=None):
    S, D = dout.shape
    nc, _, Fs = G.shape
    tm = TOK_TILE
    n_in = 8 if after is None else 9

    def body(*refs):
        do_ref, x_ref, g_ref, G_ref, U_ref, wg_ref, wu_ref, wd_ref = refs[:8]
        dx_ref, dg_ref, dob_ref, dG_ref, dU_ref, H_ref, acc_ref = refs[n_in:]
        i, j = pl.program_id(0), pl.program_id(1)

        @pl.when(j == 0)
        def _():
            dob_ref[...] = (FFN_RES * do_ref[...]).astype(BF16)
            acc_ref[...] = jnp.zeros_like(acc_ref)

        @pl.when((i == 0) & (j == 0))
        def _():
            dg_ref[...] = jnp.zeros_like(dg_ref)

        Gv = G_ref[0].astype(F32)
        Uv = U_ref[0].astype(F32)
        dH = _dot_nt(dob_ref[...], wd_ref[0])
        sg = jax.nn.sigmoid(Gv)
        act = Gv * sg
        H_ref[0] = (act * Uv).astype(BF16)
        dU = (dH * act).astype(BF16)
        dG = (dH * Uv * (sg * (1.0 + Gv * (1.0 - sg)))).astype(BF16)
        dG_ref[0] = dG
        dU_ref[0] = dU
        acc_ref[...] += _dot(dG, wg_ref[0]) + _dot(dU, wu_ref[0])

        @pl.when(j == nc - 1)
        def _():
            dxn = acc_ref[...]
            dx, xh = _rms_bwd(dxn, x_ref[...], g_ref[...])
            dx_ref[...] = dx + do_ref[...]
            dg_ref[...] += jnp.sum(dxn * xh, axis=0, keepdims=True)

    row = pl.BlockSpec((tm, D), lambda i, j: (i, 0))
    one = pl.BlockSpec((1, D), lambda i, j: (0, 0))
    hid = pl.BlockSpec((1, tm, Fs), lambda i, j: (j, i, 0))
    wrow = pl.BlockSpec((1, Fs, D), lambda i, j: (j, 0, 0))
    args = [dout, x, g, G, U, wg, wu, wd] + ([] if after is None else [after])
    return _pallas(
        body, name="ffn_bwd_x", grid=(S // tm, nc),
        in_specs=[row, row, one, hid, hid, wrow, wrow, wrow] + ([] if after is None else [ANY]),
        out_specs=[row, one, row, hid, hid, hid],
        out_shape=[_sds((S, D), F32), _sds((1, D), F32), _sds((S, D), BF16)] + [_sds((nc, S, Fs), BF16)] * 3,
        scratch_shapes=[pltpu.VMEM((tm, D), F32)],
        compiler_params=_cp(("arbitrary", "arbitrary"), 56),
    )(*args)


def _ffn_bwd_dw(dG, dU, H, xn, dob):
    nc, S, Fs = dG.shape
    D = xn.shape[1]
    tk = min(DW_TILE, S)
    nt = S // tk

    def body(dG_ref, dU_ref, H_ref, xn_ref, do_ref, dwg_ref, dwu_ref, dwd_ref, ag, au, ad):
        i = pl.program_id(1)

        @pl.when(i == 0)
        def _():
            ag[...] = jnp.zeros_like(ag)
            au[...] = jnp.zeros_like(au)
            ad[...] = jnp.zeros_like(ad)

        xnv = xn_ref[...]
        ag[...] += _dot_tn(dG_ref[0], xnv)
        au[...] += _dot_tn(dU_ref[0], xnv)
        ad[...] += _dot_tn(H_ref[0], do_ref[...])

        @pl.when(i == nt - 1)
        def _():
            dwg_ref[0] = ag[...].astype(BF16)
            dwu_ref[0] = au[...].astype(BF16)
            dwd_ref[0] = ad[...].astype(BF16)

    row = pl.BlockSpec((tk, D), lambda j, i: (i, 0))
    hid = pl.BlockSpec((1, tk, Fs), lambda j, i: (j, i, 0))
    wrow = pl.BlockSpec((1, Fs, D), lambda j, i: (j, 0, 0))
    return _pallas(
        body, name="ffn_bwd_dw", grid=(nc, nt), in_specs=[hid, hid, hid, row, row], out_specs=[wrow] * 3,
        out_shape=[_sds((nc, Fs, D), BF16)] * 3, scratch_shapes=[pltpu.VMEM((Fs, D), F32)] * 3,
        compiler_params=_cp(("parallel", "arbitrary"), 56),
    )(dG, dU, H, xn, dob)


def _norm_in_bwd(dzs, ws, x, g, dres, w_rows=False):
    S, D = x.shape
    nc = dzs[0].shape[0]
    n = len(dzs)
    tm = TOK_TILE

    def body(*refs):
        dz_refs, w_refs = refs[:n], refs[n:2 * n]
        x_ref, g_ref, dres_ref, dx_ref, dg_ref, acc_ref = refs[2 * n:]
        i, j = pl.program_id(0), pl.program_id(1)

        @pl.when(j == 0)
        def _():
            acc_ref[...] = jnp.zeros_like(acc_ref)

        @pl.when((i == 0) & (j == 0))
        def _():
            dg_ref[...] = jnp.zeros_like(dg_ref)

        for dz_ref, w_ref in zip(dz_refs, w_refs):
            acc_ref[...] += _dot(dz_ref[0], w_ref[0]) if w_rows else _dot_nt(dz_ref[0], w_ref[0])

        @pl.when(j == nc - 1)
        def _():
            dxn = acc_ref[...]
            dx, xh = _rms_bwd(dxn, x_ref[...], g_ref[...])
            dx_ref[...] = dx + dres_ref[...]
            dg_ref[...] += jnp.sum(dxn * xh, axis=0, keepdims=True)

    row = pl.BlockSpec((tm, D), lambda i, j: (i, 0))
    one = pl.BlockSpec((1, D), lambda i, j: (0, 0))
    in_specs = [pl.BlockSpec((1, tm, dz.shape[2]), lambda i, j: (j, i, 0)) for dz in dzs]
    in_specs += [pl.BlockSpec((1,) + w.shape[1:], lambda i, j: (j, 0, 0)) for w in ws]
    return _pallas(
        body, name="norm_in_bwd", grid=(S // tm, nc),
        in_specs=in_specs + [row, one, row], out_specs=[row, one],
        out_shape=[_sds((S, D), F32), _sds((1, D), F32)],
        scratch_shapes=[pltpu.VMEM((tm, D), F32)],
        compiler_params=_cp(("arbitrary", "arbitrary")),
    )(*dzs, *ws, x, g, dres)


def _norm_proj(x, g, w, w2=None):
    S, D = x.shape
    N = w.shape[1]
    tm = TOK_TILE

    def body(*refs):
        if w2 is None:
            x_ref, g_ref, w_ref, h_ref, z_ref = refs
        else:
            x_ref, g_ref, w_ref, w2_ref, h_ref, z_ref, z2_ref = refs
        xv = x_ref[...]
        h = (xv * _rms(xv) * g_ref[...]).astype(BF16)
        h_ref[...] = h
        z_ref[...] = _dot(h, w_ref[...]).astype(BF16)
        if w2 is not None:
            z2_ref[...] = _dot(h, w2_ref[...])

    row = pl.BlockSpec((tm, D), lambda i: (i, 0))
    in_specs = [row, pl.BlockSpec((1, D), lambda i: (0, 0)), pl.BlockSpec((D, N), lambda i: (0, 0))]
    out_specs = [row, pl.BlockSpec((tm, N), lambda i: (i, 0))]
    out_shape = [_sds((S, D), BF16), _sds((S, N), BF16)]
    args = [x, g, w]
    if w2 is not None:
        N2 = w2.shape[1]
        in_specs.append(pl.BlockSpec((D, N2), lambda i: (0, 0)))
        out_specs.append(pl.BlockSpec((tm, N2), lambda i: (i, 0)))
        out_shape.append(_sds((S, N2), F32))
        args.append(w2)
    return _pallas(body, name="norm_proj", grid=(S // tm,), in_specs=in_specs, out_specs=out_specs,
                   out_shape=out_shape, compiler_params=_cp(("parallel",)))(*args)


def _proj_res(acts, ws, res):
    S, D = res.shape
    n = len(acts)
    tm = TOK_TILE

    def body(*refs):
        a_refs, w_refs = refs[:n], refs[n:2 * n]
        res_ref, out_ref = refs[2 * n:]
        acc = res_ref[...]
        for a_ref, w_ref in zip(a_refs, w_refs):
            acc = acc + _dot(a_ref[...], w_ref[...])
        out_ref[...] = acc

    row = pl.BlockSpec((tm, D), lambda i: (i, 0))
    in_specs = [pl.BlockSpec((tm, a.shape[1]), lambda i: (i, 0)) for a in acts]
    in_specs += [pl.BlockSpec(w.shape, lambda i: (0, 0)) for w in ws]
    return _pallas(body, name="proj_res", grid=(S // tm,), in_specs=in_specs + [row], out_specs=row,
                   out_shape=_sds((S, D), F32), compiler_params=_cp(("parallel",)))(*acts, *ws, res)


def _matmul_nt(a, w, after=None):
    S, K = a.shape
    M = w.shape[0]
    tm = TOK_TILE

    def body(a_ref, w_ref, *rest):
        rest[-1][...] = _dot_nt(a_ref[...].astype(BF16), w_ref[...])

    extra = [] if after is None else [after]
    return _pallas(body, name="matmul_nt", grid=(S // tm,),
                   in_specs=[pl.BlockSpec((tm, K), lambda i: (i, 0)), pl.BlockSpec((M, K), lambda i: (0, 0))] + [ANY] * len(extra),
                   out_specs=pl.BlockSpec((tm, M), lambda i: (i, 0)), out_shape=_sds((S, M), F32),
                   compiler_params=_cp(("parallel",)))(a, w, *extra)


def _matmul_tn(a, b, tn):
    S, M = a.shape
    N = b.shape[1]
    tm = TOK_TILE
    nt = S // tm

    def body(a_ref, b_ref, o_ref, acc_ref):
        i = pl.program_id(1)

        @pl.when(i == 0)
        def _():
            acc_ref[...] = jnp.zeros_like(acc_ref)

        acc_ref[...] += _dot_tn(a_ref[...].astype(BF16), b_ref[...].astype(BF16))

        @pl.when(i == nt - 1)
        def _():
            o_ref[0] = acc_ref[...].astype(BF16)

    return _pallas(body, name="matmul_tn", grid=(N // tn, nt),
                   in_specs=[pl.BlockSpec((tm, M), lambda j, i: (i, 0)), pl.BlockSpec((tm, tn), lambda j, i: (i, j))],
                   out_specs=pl.BlockSpec((1, M, tn), lambda j, i: (j, 0, 0)), out_shape=_sds((N // tn, M, tn), BF16),
                   scratch_shapes=[pltpu.VMEM((M, tn), F32)],
                   compiler_params=_cp(("parallel", "arbitrary")))(a, b)


def _conv_a_fwd(z, cw, cb, cn):
    S = z.shape[0]
    C = D_CONV
    tm = TOK_TILE
    hb = tm // HALO_A

    def body(u_ref, gt_ref, up_ref, gp_ref, cw_ref, cb_ref, cn_ref, a_ref, a1_ref, win):
        i = pl.program_id(0)
        prev = up_ref[...].astype(F32) * jax.nn.sigmoid(gp_ref[...].astype(F32))
        win[pl.ds(0, HALO_A), :] = jnp.where(i == 0, 0.0, prev)
        win[pl.ds(HALO_A, tm), :] = u_ref[...].astype(F32) * jax.nn.sigmoid(gt_ref[...].astype(F32))
        acc = jnp.zeros((tm, C), F32)
        for k in range(CONV_A_WIDTH):
            acc = acc + cw_ref[k:k + 1, :] * win[pl.ds(HALO_A - (CONV_A_WIDTH - 1) + k, tm), :]
        a1 = acc + cb_ref[...]
        a1_ref[...] = a1
        a2 = a1 * _rms(a1) * cn_ref[...]
        a_ref[...] = (a2 * jax.nn.sigmoid(a2)).astype(BF16)

    cur = lambda c: pl.BlockSpec((tm, C), lambda i, c=c: (i, c))
    prv = lambda c: pl.BlockSpec((HALO_A, C), lambda i, c=c: (jnp.maximum(i * hb - 1, 0), c))
    vec = pl.BlockSpec((1, C), lambda i: (0, 0))
    return _pallas(body, name="conv_a_fwd", grid=(S // tm,),
                   in_specs=[cur(0), cur(1), prv(0), prv(1), pl.BlockSpec((32, C), lambda i: (0, 0)), vec, vec],
                   out_specs=[pl.BlockSpec((tm, C), lambda i: (i, 0)), pl.BlockSpec((tm, C), lambda i: (i, 0))],
                   out_shape=[_sds((S, C), BF16), _sds((S, C), F32)],
                   scratch_shapes=[pltpu.VMEM((tm + HALO_A, C), F32)],
                   compiler_params=_cp(("parallel",)))(z, z, z, z, cw, cb, cn)


def _conv_a_bwd(da, a1, z, cw, cn):
    S = z.shape[0]
    C = D_CONV
    tm = TOK_TILE
    hb = tm // HALO_A
    nt = S // tm
    W = CONV_A_WIDTH

    def body(da_ref, a1_ref, dan_ref, a1n_ref, u_ref, gt_ref, up_ref, gp_ref, cw_ref, cn_ref,
             duz_ref, dcw_ref, dcb_ref, dcn_ref, win, dwin):
        i = pl.program_id(0)

        @pl.when(i == 0)
        def _():
            dcw_ref[...] = jnp.zeros_like(dcw_ref)
            dcb_ref[...] = jnp.zeros_like(dcb_ref)
            dcn_ref[...] = jnp.zeros_like(dcn_ref)

        cnv = cn_ref[...]

        def da1_of(dav, a1v):
            a2 = a1v * _rms(a1v) * cnv
            da2 = dav * _silu_grad(a2)
            dx, xh = _rms_bwd(da2, a1v, cnv)
            return dx, da2 * xh

        da1, dcn_t = da1_of(da_ref[...], a1_ref[...])
        da1n, _ = da1_of(dan_ref[...], a1n_ref[...])
        dwin[pl.ds(0, tm), :] = da1
        dwin[pl.ds(tm, HALO_A), :] = jnp.where(i == nt - 1, 0.0, da1n)
        dcb_ref[...] += jnp.sum(da1, axis=0, keepdims=True)
        dcn_ref[...] += jnp.sum(dcn_t, axis=0, keepdims=True)

        u = u_ref[...].astype(F32)
        sg = jax.nn.sigmoid(gt_ref[...].astype(F32))
        prev = up_ref[...].astype(F32) * jax.nn.sigmoid(gp_ref[...].astype(F32))
        win[pl.ds(0, HALO_A), :] = jnp.where(i == 0, 0.0, prev)
        win[pl.ds(HALO_A, tm), :] = u * sg

        da0 = jnp.zeros((tm, C), F32)
        for k in range(W):
            da0 = da0 + cw_ref[k:k + 1, :] * dwin[pl.ds(W - 1 - k, tm), :]
            dcw_ref[k:k + 1, :] += jnp.sum(da1 * win[pl.ds(HALO_A - (W - 1) + k, tm), :], axis=0, keepdims=True)
        duz_ref[:, 0:C] = (da0 * sg).astype(BF16)
        duz_ref[:, C:2 * C] = (da0 * u * sg * (1.0 - sg)).astype(BF16)

    cur = lambda c: pl.BlockSpec((tm, C), lambda i, c=c: (i, c))
    prv = lambda c: pl.BlockSpec((HALO_A, C), lambda i, c=c: (jnp.maximum(i * hb - 1, 0), c))
    nxt = pl.BlockSpec((HALO_A, C), lambda i: (jnp.minimum((i + 1) * hb, S // HALO_A - 1), 0))
    vec = pl.BlockSpec((1, C), lambda i: (0, 0))
    return _pallas(body, name="conv_a_bwd", grid=(nt,),
                   in_specs=[cur(0), cur(0), nxt, nxt, cur(0), cur(1), prv(0), prv(1),
                             pl.BlockSpec((32, C), lambda i: (0, 0)), vec],
                   out_specs=[pl.BlockSpec((tm, 2 * C), lambda i: (i, 0)), pl.BlockSpec((32, C), lambda i: (0, 0)), vec, vec],
                   out_shape=[_sds((S, 2 * C), BF16), _sds((32, C), F32), _sds((1, C), F32), _sds((1, C), F32)],
                   scratch_shapes=[pltpu.VMEM((tm + HALO_A, C), F32), pltpu.VMEM((tm + HALO_A, C), F32)],
                   compiler_params=_cp(("arbitrary",)))(da, a1, da, a1, z, z, z, z, cw, cn)


def _forget_scan(fl, bf):
    S, L = fl.shape
    B = SCAN_BLK

    def body(fl_ref, bf_ref, flb_ref, F_ref):
        tri = (lax.broadcasted_iota(jnp.int32, (B, B), 0) >= lax.broadcasted_iota(jnp.int32, (B, B), 1)).astype(F32)

        def step(c, carry):
            rows = pl.ds(pl.multiple_of(c * B, B), B)
            v = fl_ref[rows, :] + bf_ref[...]
            flb_ref[rows, :] = v
            lf = jnp.minimum(v, 0.0) - jnp.log1p(jnp.exp(-jnp.abs(v)))
            cs = jnp.dot(tri, lf, precision=lax.Precision.HIGHEST, preferred_element_type=F32) + carry
            F_ref[rows, :] = cs
            return cs[B - 1:B, :]

        lax.fori_loop(0, S // B, step, jnp.zeros((1, L), F32))

    return _pallas(body, name="forget_scan", out_shape=[_sds((S, L), F32), _sds((S, L), F32)],
                   compiler_params=_cp())(fl, bf)


def _forget_scan_bwd(dF, flb):
    S, L = dF.shape
    B = SCAN_BLK
    nb = S // B

    def body(dF_ref, flb_ref, dfl_ref, db_ref):
        tri = (lax.broadcasted_iota(jnp.int32, (B, B), 0) <= lax.broadcasted_iota(jnp.int32, (B, B), 1)).astype(F32)

        def step(t, carry):
            carry_cs, db = carry
            rows = pl.ds(pl.multiple_of((nb - 1 - t) * B, B), B)
            cs = jnp.dot(tri, dF_ref[rows, :], precision=lax.Precision.HIGHEST, preferred_element_type=F32) + carry_cs
            dfl = cs * jax.nn.sigmoid(-flb_ref[rows, :])
            dfl_ref[rows, :] = dfl
            return cs[0:1, :], db + jnp.sum(dfl, axis=0, keepdims=True)

        _, db = lax.fori_loop(0, nb, step, (jnp.zeros((1, L), F32), jnp.zeros((1, L), F32)))
        db_ref[...] = db

    return _pallas(body, name="forget_scan_bwd", out_shape=[_sds((S, L), F32), _sds((1, L), F32)],
                   compiler_params=_cp())(dF, flb)


def _qk_norm(q, k, qw, kw):
    H, S, Dh = q.shape
    tq = min(QKN_TILE, S)
    scale = 1.0 / math.sqrt(Dh)

    def body(q_ref, k_ref, qw_ref, kw_ref, qn_ref, kn_ref):
        qv = q_ref[0].astype(F32)
        kv = k_ref[0].astype(F32)
        qn_ref[0] = (qv * _rms(qv) * qw_ref[...] * scale).astype(BF16)
        kn_ref[0] = (kv * _rms(kv) * kw_ref[...]).astype(BF16)

    blk = pl.BlockSpec((1, tq, Dh), lambda h, i: (h, i, 0))
    vec = pl.BlockSpec((1, Dh), lambda h, i: (0, 0))
    return _pallas(body, name="qk_norm", grid=(H, S // tq), in_specs=[blk, blk, vec, vec], out_specs=[blk, blk],
                   out_shape=[_sds((H, S, Dh), BF16), _sds((H, S, Dh), BF16)],
                   compiler_params=_cp(("parallel", "parallel")))(q, k, qw, kw)


def _qk_norm_bwd(dqs, dkn, q, k, qw, kw):
    H, S, Dh = q.shape
    tq = min(QKN_TILE, S)
    scale = 1.0 / math.sqrt(Dh)

    def body(dqs_ref, dkn_ref, q_ref, k_ref, qw_ref, kw_ref, dq_ref, dk_ref, dqw_ref, dkw_ref):
        @pl.when((pl.program_id(0) == 0) & (pl.program_id(1) == 0))
        def _():
            dqw_ref[...] = jnp.zeros_like(dqw_ref)
            dkw_ref[...] = jnp.zeros_like(dkw_ref)

        dqn = dqs_ref[0] * scale
        dq, qh = _rms_bwd(dqn, q_ref[0].astype(F32), qw_ref[...])
        dq_ref[0] = dq.astype(BF16)
        dqw_ref[...] += jnp.sum(dqn * qh, axis=0, keepdims=True)
        dkv = dkn_ref[0]
        dk, kh = _rms_bwd(dkv, k_ref[0].astype(F32), kw_ref[...])
        dk_ref[0] = dk.astype(BF16)
        dkw_ref[...] += jnp.sum(dkv * kh, axis=0, keepdims=True)

    blk = pl.BlockSpec((1, tq, Dh), lambda h, i: (h, i, 0))
    vec = pl.BlockSpec((1, Dh), lambda h, i: (0, 0))
    return _pallas(body, name="qk_norm_bwd", grid=(H, S // tq), in_specs=[blk, blk, blk, blk, vec, vec],
                   out_specs=[blk, blk, vec, vec],
                   out_shape=[_sds((H, S, Dh), BF16), _sds((H, S, Dh), BF16), _sds((1, Dh), F32), _sds((1, Dh), F32)],
                   compiler_params=_cp(("arbitrary", "arbitrary")))(dqs, dkn, q, k, qw, kw)


NEG = -1e30


def _causal_mask(t):
    return lax.broadcasted_iota(jnp.int32, (t, t), 0) >= lax.broadcasted_iota(jnp.int32, (t, t), 1)


def _fox_fwd(qs, kn, v, fcol, frow):
    H, S, Dh = qs.shape
    t = ATT_TILE
    nq = S // t

    def body(q_ref, k_ref, v_ref, fc_ref, fr_ref, o_ref, lse_ref):
        i = pl.program_id(1)
        q = q_ref[0]
        fq = fc_ref[0]

        def tile(j, carry, diag):
            m, l, acc = carry
            rows = pl.ds(pl.multiple_of(j * t, t), t)
            s = _dot_nt(q, k_ref[0, rows, :]) + fq - fr_ref[0, j]
            if diag:
                s = jnp.where(_causal_mask(t), s, NEG)
            m_new = jnp.maximum(m, jnp.max(s, axis=-1, keepdims=True))
            p = jnp.exp(s - m_new)
            alpha = jnp.exp(m - m_new)
            l = alpha * l + jnp.sum(p, axis=-1, keepdims=True)
            acc = alpha * acc + _dot(p.astype(BF16), v_ref[0, rows, :])
            return m_new, l, acc

        init = (jnp.full((t, 1), NEG, F32), jnp.zeros((t, 1), F32), jnp.zeros((t, Dh), F32))
        carry = lax.fori_loop(0, i, lambda j, c: tile(j, c, False), init)
        m, l, acc = tile(i, carry, True)
        o_ref[0] = (acc / l).astype(BF16)
        lse_ref[0] = m + jnp.log(l)

    qblk = pl.BlockSpec((1, t, Dh), lambda h, i: (h, i, 0))
    full = pl.BlockSpec((1, S, Dh), lambda h, i: (h, 0, 0))
    col = pl.BlockSpec((1, t, 1), lambda h, i: (h, i, 0))
    return _pallas(body, name="fox_fwd", grid=(H, nq),
                   in_specs=[qblk, full, full, col, pl.BlockSpec((1, nq, 1, t), lambda h, i: (h, 0, 0, 0))],
                   out_specs=[qblk, col], out_shape=[_sds((H, S, Dh), BF16), _sds((H, S, 1), F32)],
                   compiler_params=_cp(("parallel", "parallel")))(qs, kn, v, fcol, frow)


def _fox_bwd(qs, kn, v, o, do, lse, fcol, frow):
    H, S, Dh = qs.shape
    t = ATT_TILE
    nq = S // t

    def body(q_ref, k_ref, v_ref, o_ref, do_ref, lse_ref, fc_ref, fr_ref, dq_ref, dk_ref, dv_ref, dfq_ref, dfk_ref):
        j = pl.program_id(1)

        @pl.when(j == 0)
        def _():
            dq_ref[...] = jnp.zeros_like(dq_ref)
            dfq_ref[...] = jnp.zeros_like(dfq_ref)

        k = k_ref[0]
        vv = v_ref[0]
        fk = fr_ref[0, 0]

        def tile(i, carry, diag):
            dk, dv, dfk = carry
            rows = pl.ds(pl.multiple_of(i * t, t), t)
            q = q_ref[0, rows, :]
            dov = do_ref[0, rows, :]
            delta = jnp.sum(dov.astype(F32) * o_ref[0, rows, :].astype(F32), axis=-1, keepdims=True)
            s = _dot_nt(q, k) + fc_ref[0, rows, :] - fk
            if diag:
                s = jnp.where(_causal_mask(t), s, NEG)
            p = jnp.exp(s - lse_ref[0, rows, :])
            dv = dv + _dot_tn(p.astype(BF16), dov)
            ds = p * (_dot_nt(dov, vv) - delta)
            dsb = ds.astype(BF16)
            dq_ref[0, rows, :] += _dot(dsb, k)
            dk = dk + _dot_tn(dsb, q)
            dfq_ref[0, rows, :] += jnp.sum(ds, axis=-1, keepdims=True)
            dfk = dfk + jnp.sum(ds, axis=0, keepdims=True)
            return dk, dv, dfk

        init = (jnp.zeros((t, Dh), F32), jnp.zeros((t, Dh), F32), jnp.zeros((1, t), F32))
        carry = tile(j, init, True)
        dk, dv, dfk = lax.fori_loop(j + 1, nq, lambda i, c: tile(i, c, False), carry)
        dk_ref[0] = dk
        dv_ref[0] = dv
        dfk_ref[0, 0] = dfk

    full = pl.BlockSpec((1, S, Dh), lambda h, j: (h, 0, 0))
    kblk = pl.BlockSpec((1, t, Dh), lambda h, j: (h, j, 0))
    colf = pl.BlockSpec((1, S, 1), lambda h, j: (h, 0, 0))
    rowb = pl.BlockSpec((1, 1, 1, t), lambda h, j: (h, j, 0, 0))
    return _pallas(body, name="fox_bwd", grid=(H, nq),
                   in_specs=[full, kblk, kblk, full, full, colf, colf, rowb],
                   out_specs=[full, kblk, kblk, colf, rowb],
                   out_shape=[_sds((H, S, Dh), F32), _sds((H, S, Dh), F32), _sds((H, S, Dh), F32),
                              _sds((H, S, 1), F32), _sds((H, nq, 1, t), F32)],
                   compiler_params=_cp(("parallel", "arbitrary"), 56))(qs, kn, v, o, do, lse, fcol, frow)


def _odd_mid_fwd(z, cw):
    S = z.shape[0]
    D = z.shape[1] // 3
    tm = TOK_TILE
    hb = tm // HALO_C
    W = CONV_C_WIDTH

    def body(gb_ref, gc_ref, hh_ref, gcp_ref, hhp_ref, cw_ref, y_ref, win):
        i = pl.program_id(0)
        prev = gcp_ref[...].astype(F32) * hhp_ref[...].astype(F32)
        win[pl.ds(0, HALO_C), :] = jnp.where(i == 0, 0.0, prev)
        win[pl.ds(HALO_C, tm), :] = gc_ref[...].astype(F32) * hh_ref[...].astype(F32)
        c1 = jnp.zeros((tm, D), F32)
        for k in range(W):
            c1 = c1 + cw_ref[k:k + 1, :] * win[pl.ds(HALO_C - (W - 1) + k, tm), :]
        y_ref[...] = (gb_ref[...].astype(F32) * c1).astype(BF16)

    cur = lambda c: pl.BlockSpec((tm, D), lambda i, c=c: (i, c))
    prv = lambda c: pl.BlockSpec((HALO_C, D), lambda i, c=c: (jnp.maximum(i * hb - 1, 0), c))
    return _pallas(body, name="odd_mid_fwd", grid=(S // tm,),
                   in_specs=[cur(0), cur(1), cur(2), prv(1), prv(2), pl.BlockSpec((8, D), lambda i: (0, 0))],
                   out_specs=pl.BlockSpec((tm, D), lambda i: (i, 0)), out_shape=_sds((S, D), BF16),
                   scratch_shapes=[pltpu.VMEM((tm + HALO_C, D), F32)],
                   compiler_params=_cp(("parallel",)))(z, z, z, z, z, cw)


def _odd_mid_bwd(dy, z, cw):
    S = z.shape[0]
    D = z.shape[1] // 3
    tm = TOK_TILE
    hb = tm // HALO_C
    nt = S // tm
    W = CONV_C_WIDTH

    def body(dy_ref, dyn_ref, gb_ref, gbn_ref, gc_ref, hh_ref, gcp_ref, hhp_ref, cw_ref, dz_ref, dcw_ref, win, dwin):
        i = pl.program_id(0)

        @pl.when(i == 0)
        def _():
            dcw_ref[...] = jnp.zeros_like(dcw_ref)

        gc = gc_ref[...].astype(F32)
        hh = hh_ref[...].astype(F32)
        prev = gcp_ref[...].astype(F32) * hhp_ref[...].astype(F32)
        win[pl.ds(0, HALO_C), :] = jnp.where(i == 0, 0.0, prev)
        win[pl.ds(HALO_C, tm), :] = gc * hh
        dyv = dy_ref[...]
        dc1 = dyv * gb_ref[...].astype(F32)
        dwin[pl.ds(0, tm), :] = dc1
        dwin[pl.ds(tm, HALO_C), :] = jnp.where(i == nt - 1, 0.0, dyn_ref[...] * gbn_ref[...].astype(F32))
        c1 = jnp.zeros((tm, D), F32)
        dc0 = jnp.zeros((tm, D), F32)
        for k in range(W):
            tap = win[pl.ds(HALO_C - (W - 1) + k, tm), :]
            c1 = c1 + cw_ref[k:k + 1, :] * tap
            dc0 = dc0 + cw_ref[k:k + 1, :] * dwin[pl.ds(W - 1 - k, tm), :]
            dcw_ref[k:k + 1, :] += jnp.sum(dc1 * tap, axis=0, keepdims=True)
        dz_ref[:, 0:D] = (dyv * c1).astype(BF16)
        dz_ref[:, D:2 * D] = (dc0 * hh).astype(BF16)
        dz_ref[:, 2 * D:3 * D] = (dc0 * gc).astype(BF16)

    cur = lambda c: pl.BlockSpec((tm, D), lambda i, c=c: (i, c))
    prv = lambda c: pl.BlockSpec((HALO_C, D), lambda i, c=c: (jnp.maximum(i * hb - 1, 0), c))
    nxt = pl.BlockSpec((HALO_C, D), lambda i: (jnp.minimum((i + 1) * hb, S // HALO_C - 1), 0))
    return _pallas(body, name="odd_mid_bwd", grid=(nt,),
                   in_specs=[cur(0), nxt, cur(0), nxt, cur(1), cur(2), prv(1), prv(2), pl.BlockSpec((8, D), lambda i: (0, 0))],
                   out_specs=[pl.BlockSpec((tm, 3 * D), lambda i: (i, 0)), pl.BlockSpec((8, D), lambda i: (0, 0))],
                   out_shape=[_sds((S, 3 * D), BF16), _sds((8, D), F32)],
                   scratch_shapes=[pltpu.VMEM((tm + HALO_C, D), F32), pltpu.VMEM((tm + HALO_C, D), F32)],
                   compiler_params=_cp(("arbitrary",)))(dy, dy, z, z, z, z, z, z, cw)


def _loss_head(y, tgt):
    S, D = y.shape
    tm = TOK_TILE

    def body(y_ref, t_ref, dy_ref, l_ref):
        @pl.when(pl.program_id(0) == 0)
        def _():
            l_ref[...] = jnp.zeros_like(l_ref)

        e = y_ref[...] - t_ref[...]
        dy_ref[...] = e * (1.0 / D)
        l_ref[...] += jnp.sum(jnp.sum(e * e, axis=-1, keepdims=True), axis=0, keepdims=True) * (0.5 / D)

    row = pl.BlockSpec((tm, D), lambda i: (i, 0))
    return _pallas(body, name="loss_head", grid=(S // tm,), in_specs=[row, row],
                   out_specs=[row, pl.BlockSpec((1, 1), lambda i: (0, 0))],
                   out_shape=[_sds((S, D), F32), _sds((1, 1), F32)],
                   compiler_params=_cp(("arbitrary",)))(y, tgt)


def _heads(a):
    S = a.shape[0]
    return a.reshape(S, N_HEADS, HEAD_DIM).transpose(1, 0, 2)


def _unheads(a):
    return a.transpose(1, 0, 2).reshape(a.shape[1], D_ATTN)


def _pad_rows(a, rows):
    return jnp.pad(a, ((0, rows - a.shape[0]), (0, 0)))


def _local_step(x, tgt, W, need=lambda block, after: None, done=lambda block, block_grads: None):
    S, D = x.shape
    nq = S // ATT_TILE
    grads = {}
    saved = {}

    def gain_after(gain, token):
        return gain if token is None else gain + token

    def ffn_f(tag, l, xin):
        need((tag, l), xin)
        out, xn, G, U = _ffn_fwd(xin, W[tag + "_norm"][l:l + 1], W[tag + "_w_gate"][l], W[tag + "_w_up"][l],
                                 W[tag + "_w_down"][l])
        saved[(tag, l)] = (xin, xn, G, U)
        return out

    order = {"after": None}

    def ffn_b(tag, l, dout):
        xin, xn, G, U = saved[(tag, l)]
        keys = [(tag + "_w_gate", l), (tag + "_w_up", l), (tag + "_w_down", l)]
        gain = W[tag + "_norm"][l:l + 1]
        wg, wu, wd = W[tag + "_w_gate"][l], W[tag + "_w_up"][l], W[tag + "_w_down"][l]
        if (tag, l) == ("ffn1", 0):
            *dws, dG, dU = _ffn_bwd_w(dout, xn, G, U, wd)
            big = dict(zip(keys, dws))
            token = done((tag, l), big)
            dx, dg = _norm_in_bwd([dG, dU], [wg, wu], xin, gain_after(gain, token), dout, w_rows=True)
        else:
            dx, dg, dob, dG, dU, H = _ffn_bwd_x(dout, xin, gain, G, U, wg, wu, wd, order["after"])
            big = dict(zip(keys, _ffn_bwd_dw(dG, dU, H, xn, dob)))
            order["after"] = done((tag, l), big)
        grads.update(big)
        grads[(tag + "_norm", l)] = dg
        return dx

    x0a = ffn_f("ffn1", 0, x)
    need(("ev", 0), x0a)
    w_in = W["ev_w_in"]
    w_main, w_f = w_in[:, :2560], jnp.pad(w_in[:, 2560:], ((0, 0), (0, 120)))
    h0, z0, fl = _norm_proj(x0a, W["mix_norm"][0:1], w_main, w_f)
    cw_a = _pad_rows(W["ev_conv_w"], 32)
    a_act, a1 = _conv_a_fwd(z0, cw_a, W["ev_conv_b"], W["ev_conv_norm"])
    flb, Fc = _forget_scan(fl, jnp.pad(W["ev_b_f"], ((0, 0), (0, 120))))
    Ft = Fc[:, :N_HEADS].T
    fcol = Ft.reshape(N_HEADS, S, 1)
    frow = Ft.reshape(N_HEADS, nq, 1, ATT_TILE)
    q_raw, k_raw, v_h = _heads(z0[:, 1024:1536]), _heads(z0[:, 1536:2048]), _heads(z0[:, 2048:2560])
    qs, kn = _qk_norm(q_raw, k_raw, W["ev_q_norm"], W["ev_k_norm"])
    o_h, lse = _fox_fwd(qs, kn, v_h, fcol, frow)
    o_flat = _unheads(o_h)
    w_out_e = W["ev_w_out"]
    x0b = _proj_res([a_act, o_flat], [w_out_e[:D_CONV], w_out_e[D_CONV:]], x0a)
    x0c = ffn_f("ffn2", 0, x0b)
    x1a = ffn_f("ffn1", 1, x0c)
    need(("od", 0), x1a)
    h1, z1 = _norm_proj(x1a, W["mix_norm"][1:2], W["od_w_in"])
    cw_c = _pad_rows(W["od_conv_w"], 8)
    y1 = _odd_mid_fwd(z1, cw_c)
    x1b = _proj_res([y1], [W["od_w_out"]], x1a)
    x1c = ffn_f("ffn2", 1, x1b)
    dy, loss = _loss_head(x1c, tgt)

    d = ffn_b("ffn2", 1, dy)
    dy1 = _matmul_nt(d, W["od_w_out"], order["after"])
    grads[("od_w_out", 0)] = _matmul_tn(y1, d, D)[0]
    dz1, dcw_c = _odd_mid_bwd(dy1, z1, cw_c)
    grads[("od_conv_w", 0)] = dcw_c[:CONV_C_WIDTH]
    grads[("od_w_in", 0)] = _matmul_tn(h1, dz1, 3 * D // 4)
    token = done(("od", 0), {k: grads[k] for k in (("od_w_out", 0), ("od_w_in", 0))})
    d, dg = _norm_in_bwd([dz1[None]], [W["od_w_in"][None]], x1a, gain_after(W["mix_norm"][1:2], token), d)
    grads[("mix_norm", 1)] = dg
    d = ffn_b("ffn1", 1, d)
    d = ffn_b("ffn2", 0, d)
    dcat = _matmul_nt(d, w_out_e, order["after"])
    grads[("ev_w_out", 0)] = jnp.concatenate([_matmul_tn(a_act, d, D)[0], _matmul_tn(o_flat, d, D)[0]], axis=0)
    duz, dcw_a, dcb, dcn = _conv_a_bwd(dcat, a1, z0, cw_a, W["ev_conv_norm"])
    grads[("ev_conv_w", 0)] = dcw_a[:CONV_A_WIDTH]
    grads[("ev_conv_b", 0)] = dcb
    grads[("ev_conv_norm", 0)] = dcn
    do_h = _heads(dcat[:, D_CONV:].astype(BF16))
    dqs, dkn, dv, dfq, dfk = _fox_bwd(qs, kn, v_h, o_h, do_h, lse, fcol, frow)
    dq_raw, dk_raw, dqw, dkw = _qk_norm_bwd(dqs, dkn, q_raw, k_raw, W["ev_q_norm"], W["ev_k_norm"])
    grads[("ev_q_norm", 0)] = dqw
    grads[("ev_k_norm", 0)] = dkw
    dF = (dfq.reshape(N_HEADS, S) - dfk.reshape(N_HEADS, S)).T
    dfl, dbf = _forget_scan_bwd(jnp.pad(dF, ((0, 0), (0, 120))), flb)
    grads[("ev_b_f", 0)] = dbf[:, :N_HEADS]
    dz0 = jnp.concatenate([duz, _unheads(dq_raw), _unheads(dk_raw), _unheads(dv.astype(BF16))], axis=1)
    dflb = dfl.astype(BF16)
    gmain = _matmul_tn(h0, dz0, 640)
    gmain = gmain.transpose(1, 0, 2).reshape(D, 2560)
    gf = _matmul_tn(h0, dflb, 128)[0][:, :N_HEADS]
    grads[("ev_w_in", 0)] = jnp.concatenate([gmain, gf], axis=1)
    token = done(("ev", 0), {k: grads[k] for k in (("ev_w_out", 0), ("ev_w_in", 0))})
    d, dg = _norm_in_bwd([dz0[None], dflb[None]], [w_main[None], w_f[None]], x0a, gain_after(W["mix_norm"][0:1], token), d)
    grads[("mix_norm", 0)] = dg
    d = ffn_b("ffn1", 0, d)
    return loss, d, grads


def _place():
    x, y, c = lax.axis_index("x"), lax.axis_index("y"), lax.axis_index("c")
    chips = [(1 - x, y), (x, 1 - y), (1 - x, 1 - y)]
    return x, y, c, chips


def _remote(src, dst, send_sem, recv_sem, to):
    return pltpu.make_async_remote_copy(src_ref=src, dst_ref=dst, send_sem=send_sem, recv_sem=recv_sem,
                                        device_id=to, device_id_type=MESH)


HBM = pl.BlockSpec(memory_space=pltpu.HBM)
SEM = pl.BlockSpec(memory_space=pltpu.SEMAPHORE)
EFFECT = pltpu.SideEffectType.DATAFLOW_SIDE_EFFECTING


def _in_hbm(a):
    return pltpu.with_memory_space_constraint(a, pltpu.HBM)


def _ag_start(bufs):
    n = len(bufs)

    def body(*refs):
        send_sems, recv_sems = refs[n], refs[n + 1]
        outs = refs[n + 2:]
        x, y, c, chips = _place()
        me = 2 * x + y
        for a in [n - 1] + list(range(n - 1)):
            if a == n - 1:
                blk = outs[a].at[me]
            else:
                h = outs[a].shape[1] // 2
                blk = outs[a].at[me, pl.ds(c * h, h)]
            for jj, (px, py) in enumerate(chips):
                _remote(blk, blk, send_sems.at[3 * a + jj], recv_sems.at[3 * a + jj], (px, py, c)).start()

    return _pallas(
        body, name="gather_start",
        out_shape=[pltpu.SemaphoreType.DMA((3 * n,)), pltpu.SemaphoreType.DMA((3 * n,))] + [pltpu.HBM(b.shape, b.dtype) for b in bufs],
        in_specs=[HBM] * n, out_specs=[SEM, SEM] + [HBM] * n, input_output_aliases={a: 2 + a for a in range(n)},
        compiler_params=pltpu.CompilerParams(has_side_effects=EFFECT),
    )(*[_in_hbm(b) for b in bufs])


def _ag_mid(g, ici_send, ici_recv, bufs, idx, taps, n_big, after):
    n = len(bufs)
    arrs = list(bufs) + ([taps] if taps is not None else [])
    m = len(arrs)

    def body(*refs):
        ici_s, ici_r = refs[0], refs[1]
        d_send, d_recv = refs[m + 3], refs[m + 4]
        outs = refs[m + 5:]
        x, y, c, chips = _place()
        me = 2 * x + y
        for i in range(m):
            a = idx[i] if i < n else n_big
            for jj, (px, py) in enumerate(chips):
                k = 3 * a + jj
                if i < n:
                    h = outs[i].shape[1] // 2
                    mine, blk = outs[i].at[me, pl.ds(c * h, h)], outs[i].at[2 * px + py, pl.ds(c * h, h)]
                else:
                    mine, blk = outs[i].at[me], outs[i].at[2 * px + py]
                _remote(mine, mine, ici_s.at[k], ici_r.at[k], (px, py, c)).wait_send()
                _remote(blk, blk, ici_s.at[k], ici_r.at[k], (px, py, c)).wait_recv()
                if i < n:
                    _remote(blk, blk, d_send.at[3 * i + jj], d_recv.at[3 * i + jj], (x, y, 1 - c)).start()

    return _pallas(
        body, name=f"gather_pass_on_{g}",
        out_shape=[pltpu.SemaphoreType.DMA((3 * n,)), pltpu.SemaphoreType.DMA((3 * n,))] + [pltpu.HBM(b.shape, b.dtype) for b in arrs],
        in_specs=[SEM, SEM] + [HBM] * m + [ANY], out_specs=[SEM, SEM] + [HBM] * m,
        input_output_aliases={2 + i: 2 + i for i in range(m)},
        compiler_params=pltpu.CompilerParams(has_side_effects=EFFECT),
    )(ici_send, ici_recv, *arrs, after)


def _ag_wait(g, d_send, d_recv, arrs, n, after):
    m = len(arrs)

    def body(*refs):
        d_s, d_r = refs[0], refs[1]
        outs = refs[m + 3:]
        x, y, c, chips = _place()
        for i in range(n):
            h = outs[i].shape[1] // 2
            for jj, (px, py) in enumerate(chips):
                sent = outs[i].at[2 * px + py, pl.ds(c * h, h)]
                got = outs[i].at[2 * px + py, pl.ds((1 - c) * h, h)]
                _remote(sent, sent, d_s.at[3 * i + jj], d_r.at[3 * i + jj], (x, y, 1 - c)).wait_send()
                _remote(got, got, d_s.at[3 * i + jj], d_r.at[3 * i + jj], (x, y, 1 - c)).wait_recv()

    return _pallas(
        body, name=f"gather_wait_{g}", out_shape=[pltpu.HBM(b.shape, b.dtype) for b in arrs],
        in_specs=[SEM, SEM] + [HBM] * m + [ANY], out_specs=[HBM] * m,
        input_output_aliases={2 + i: i for i in range(m)},
        compiler_params=pltpu.CompilerParams(has_side_effects=EFFECT),
    )(d_send, d_recv, *arrs, after)


def _pair_exchange(gs):
    n = len(gs)

    def body(*refs):
        ins, outs = refs[:n], refs[n:2 * n]
        send_sems, recv_sems = refs[2 * n:]
        x, y, c, _ = _place()
        cps = []
        for a in range(n):
            h = ins[a].shape[1] // 2
            cps.append(_remote(ins[a].at[:, pl.ds((1 - c) * h, h)], outs[a], send_sems.at[a], recv_sems.at[a], (x, y, 1 - c)))
        for cp in cps:
            cp.start()
        for cp in cps:
            cp.wait()

    return _pallas(body, name="grad_pair_exchange", in_specs=[ANY] * n, out_specs=[ANY] * n,
                   out_shape=[_sds((4, g.shape[1] // 2, g.shape[2]), g.dtype) for g in gs],
                   scratch_shapes=[pltpu.SemaphoreType.DMA((n,)), pltpu.SemaphoreType.DMA((n,))])(*gs)


def _pair_add(g, other, c_arr):
    _, R, C = g.shape
    h = R // 2

    def body(c_ref, g_ref, o_ref, out_ref):
        out_ref[...] = (g_ref[...].astype(F32) + o_ref[...].astype(F32)).astype(BF16)

    grid_spec = pltpu.PrefetchScalarGridSpec(
        num_scalar_prefetch=1, grid=(4,),
        in_specs=[pl.BlockSpec((1, h, C), lambda k, c_ref: (k, c_ref[0], 0)), pl.BlockSpec((1, h, C), lambda k, c_ref: (k, 0, 0))],
        out_specs=pl.BlockSpec((1, h, C), lambda k, c_ref: (k, 0, 0)))
    return _pallas(body, name="grad_pair_add", grid_spec=grid_spec, out_shape=_sds((4, h, C), BF16),
                   compiler_params=_cp(("parallel",)))(c_arr, g, other)


def _chip_start(g, ss):
    n = len(ss)
    zones = [lax.empty((3,) + s.shape[1:], s.dtype) for s in ss]

    def body(*refs):
        send_sems, recv_sems = refs[2 * n], refs[2 * n + 1]
        src, dst = refs[2 * n + 2:3 * n + 2], refs[3 * n + 2:4 * n + 2]
        token = refs[4 * n + 2]
        x, y, c, chips = _place()
        for a in range(n):
            for jj, (px, py) in enumerate(chips):
                k = 3 * a + jj
                _remote(src[a].at[2 * px + py], dst[a].at[jj], send_sems.at[k], recv_sems.at[k], (px, py, c)).start()
        token[...] = jnp.zeros_like(token)

    return _pallas(
        body, name=f"grad_chip_start_{g}",
        out_shape=[pltpu.SemaphoreType.DMA((3 * n,)), pltpu.SemaphoreType.DMA((3 * n,))]
        + [pltpu.HBM(a.shape, a.dtype) for a in ss + zones] + [_sds((8, 128), F32)],
        in_specs=[HBM] * (2 * n), out_specs=[SEM, SEM] + [HBM] * (2 * n) + [pl.BlockSpec(memory_space=pltpu.VMEM)],
        input_output_aliases={i: 2 + i for i in range(2 * n)},
        compiler_params=pltpu.CompilerParams(has_side_effects=EFFECT),
    )(*[_in_hbm(a) for a in ss + zones])


def _chip_wait(sends, recvs, counts, ss, zones, after):
    nb, n = len(sends), len(ss)

    def body(*refs):
        s_refs, r_refs = refs[:nb], refs[nb:2 * nb]
        outs = refs[2 * nb + 2 * n + 1:]
        src, dst = outs[:n], outs[n:]
        x, y, c, chips = _place()
        a = 0
        for b in range(nb):
            for i in range(counts[b]):
                for jj, (px, py) in enumerate(chips):
                    k = 3 * i + jj
                    _remote(src[a].at[2 * px + py], dst[a].at[jj], s_refs[b].at[k], r_refs[b].at[k], (px, py, c)).wait()
                a += 1

    return _pallas(
        body, name="grad_chip_wait", out_shape=[pltpu.HBM(a.shape, a.dtype) for a in ss + zones],
        in_specs=[SEM] * (2 * nb) + [HBM] * (2 * n) + [ANY], out_specs=[HBM] * (2 * n),
        input_output_aliases={2 * nb + i: i for i in range(2 * n)},
        compiler_params=pltpu.CompilerParams(has_side_effects=EFFECT),
    )(*sends, *recvs, *ss, *zones, after)


def _chip_sum(s, r, where, dest, l, L):
    _, h, C = s.shape
    tr = h // 2

    def body(k_ref, s_ref, r_ref, *rest):
        out_ref = rest[-1]
        acc = s_ref[0].astype(F32)
        for jj in range(3):
            acc = acc + r_ref[jj].astype(F32)
        out_ref[...] = acc

    in_specs = [pl.BlockSpec((1, tr, C), lambda i, k_ref: (k_ref[0], i, 0)), pl.BlockSpec((3, tr, C), lambda i, k_ref: (0, i, 0))]
    args = [where, s, r]
    alias = {}
    if dest is not None:
        in_specs.append(ANY)
        args.append(dest)
        alias = {3: 0}
    grid_spec = pltpu.PrefetchScalarGridSpec(
        num_scalar_prefetch=1, grid=(2,), in_specs=in_specs,
        out_specs=pl.BlockSpec((None, tr, C), lambda i, k_ref: (l, 2 * k_ref[1] + i, 0)))
    return _pallas(body, name="grad_chip_sum", grid_spec=grid_spec, out_shape=_sds((L, 2 * h, C), F32),
                   input_output_aliases=alias, compiler_params=_cp(("arbitrary",)))(*args)


def _pair_share(bufs, layout):
    n = len(layout)
    n_out = len(bufs)

    def body(*refs):
        outs = refs[n_out:2 * n_out]
        send_sems, recv_sems = refs[2 * n_out:]
        x, y, c, _ = _place()
        cps = []
        for a, (o, l) in enumerate(layout):
            h = outs[o].shape[1] // 2
            blk = outs[o].at[l, pl.ds(c * h, h)]
            cps.append(_remote(blk, blk, send_sems.at[a], recv_sems.at[a], (x, y, 1 - c)))
        for cp in cps:
            cp.start()
        for a, (o, l) in enumerate(layout):
            h = outs[o].shape[1] // 2
            blk = outs[o].at[l, pl.ds((1 - c) * h, h)]
            _remote(blk, blk, send_sems.at[a], recv_sems.at[a], (x, y, 1 - c)).wait_recv()
        for cp in cps:
            cp.wait_send()

    return _pallas(body, name="grad_pair_share", in_specs=[ANY] * n_out, out_specs=[ANY] * n_out,
                   out_shape=[_sds(b.shape, b.dtype) for b in bufs], input_output_aliases={o: o for o in range(n_out)},
                   scratch_shapes=[pltpu.SemaphoreType.DMA((n,)), pltpu.SemaphoreType.DMA((n,))])(*bufs)


def _small_all_reduce(packed):
    P, L = packed.shape

    def body(in_ref, out_ref, slots, send_sems, recv_sems):
        x, y, c, _ = _place()
        me = 4 * x + 2 * y + c
        slots[me] = in_ref[...]
        cps = []
        for r in range(1, 8):
            px = 1 - x if r & 4 else x
            py = 1 - y if r & 2 else y
            pc = 1 - c if r & 1 else c
            cps.append(_remote(in_ref, slots.at[me], send_sems.at[r - 1], recv_sems.at[r - 1], (px, py, pc)))
        for cp in cps:
            cp.start()
        for r in range(1, 8):
            px = 1 - x if r & 4 else x
            py = 1 - y if r & 2 else y
            pc = 1 - c if r & 1 else c
            blk = slots.at[4 * px + 2 * py + pc]
            _remote(blk, blk, send_sems.at[r - 1], recv_sems.at[r - 1], (px, py, pc)).wait_recv()
        for cp in cps:
            cp.wait_send()
        acc = slots[0]
        for k in range(1, 8):
            acc = acc + slots[k]
        out_ref[...] = acc

    vm = pl.BlockSpec(memory_space=pltpu.VMEM)
    return _pallas(body, name="small_all_reduce", in_specs=[vm], out_specs=vm, out_shape=_sds((P, L), F32),
                   scratch_shapes=[pltpu.VMEM((8, P, L), F32), pltpu.SemaphoreType.DMA((7,)), pltpu.SemaphoreType.DMA((7,))])(packed)


def _adamw_math(w, g, m, v):
    m = ADAM_B1 * m + (1.0 - ADAM_B1) * g
    v = ADAM_B2 * v + (1.0 - ADAM_B2) * (g * g)
    m_hat = m / (1.0 - ADAM_B1 ** ADAM_STEP)
    v_hat = v / (1.0 - ADAM_B2 ** ADAM_STEP)
    delta = -ADAM_LR * (m_hat / (jnp.sqrt(v_hat) + ADAM_EPS) + ADAM_WD * w)
    return delta, m, v


def _adamw(w, g, m, v):
    shape = w.shape
    C = shape[-1]
    rows = math.prod(shape[:-1])
    tr = next(t for t in (512, 352, 256, 128, 64, 32, 16, 8, rows) if rows % t == 0)
    w2, g2, m2, v2 = (a.reshape(rows, C) for a in (w, g, m, v))

    def body(w_ref, g_ref, m_ref, v_ref, d_ref, nm_ref, nv_ref):
        d, nm, nv = _adamw_math(w_ref[...], g_ref[...], m_ref[...], v_ref[...])
        d_ref[...] = d
        nm_ref[...] = nm
        nv_ref[...] = nv

    blk = pl.BlockSpec((tr, C), lambda i: (i, 0))
    outs = _pallas(body, name="adamw", grid=(rows // tr,), in_specs=[blk] * 4, out_specs=[blk] * 3,
                   out_shape=[_sds((rows, C), F32)] * 3, compiler_params=_cp(("parallel",)))(w2, g2, m2, v2)
    return tuple(o.reshape(shape) for o in outs)


WEIGHTS = ["ffn1_norm", "ffn1_w_gate", "ffn1_w_up", "ffn1_w_down", "mix_norm", "ffn2_norm", "ffn2_w_gate", "ffn2_w_up",
           "ffn2_w_down", "ev_w_in", "ev_b_f", "ev_conv_w", "ev_conv_b", "ev_conv_norm", "ev_q_norm", "ev_k_norm",
           "ev_w_out", "od_w_in", "od_conv_w", "od_w_out"]
BIG = ([("ffn1_w_gate", 0), ("ffn1_w_up", 0), ("ffn1_w_down", 0), ("ev_w_in", 0), ("ev_w_out", 0),
        ("ffn2_w_gate", 0), ("ffn2_w_up", 0), ("ffn2_w_down", 0)]
       + [("ffn1_w_gate", 1), ("ffn1_w_up", 1), ("ffn1_w_down", 1), ("od_w_in", 0), ("od_w_out", 0),
          ("ffn2_w_gate", 1), ("ffn2_w_up", 1), ("ffn2_w_down", 1)])
TRANSPOSED = ("ffn1_w_gate", "ffn1_w_up", "ffn2_w_gate", "ffn2_w_up")
BLOCKS = [("ffn1", 0), ("ev", 0), ("ffn2", 0), ("ffn1", 1), ("od", 0), ("ffn2", 1)]
BLOCK_OF = {(name, l): (name.split("_w_")[0], l) for name, l in BIG}
BIG_NAMES = ["ffn1_w_gate", "ffn1_w_up", "ffn1_w_down", "ffn2_w_gate", "ffn2_w_up", "ffn2_w_down",
             "ev_w_in", "ev_w_out", "od_w_in", "od_w_out"]
SMALL = [("ffn1_norm", 16), ("mix_norm", 16), ("ffn2_norm", 16), ("ev_b_f", 8), ("ev_conv_w", 128), ("ev_conv_b", 8),
         ("ev_conv_norm", 8), ("ev_q_norm", 8), ("ev_k_norm", 8), ("od_conv_w", 24)]


def _to_lanes(a, rows):
    flat = a.reshape(-1)
    return jnp.pad(flat, (0, rows * 128 - flat.shape[0])).reshape(rows, 128)


def kernel(x, ffn1_norm, ffn1_w_gate, ffn1_w_up, ffn1_w_down, mix_norm, ffn2_norm, ffn2_w_gate, ffn2_w_up, ffn2_w_down, ev_w_in, ev_b_f, ev_conv_w, ev_conv_b, ev_conv_norm, ev_q_norm, ev_k_norm, ev_w_out, od_w_in, od_conv_w, od_w_out, loss_target, m_ffn1_norm, m_ffn1_w_gate, m_ffn1_w_up, m_ffn1_w_down, m_mix_norm, m_ffn2_norm, m_ffn2_w_gate, m_ffn2_w_up, m_ffn2_w_down, m_ev_w_in, m_ev_b_f, m_ev_conv_w, m_ev_conv_b, m_ev_conv_norm, m_ev_q_norm, m_ev_k_norm, m_ev_w_out, m_od_w_in, m_od_conv_w, m_od_w_out, v_ffn1_norm, v_ffn1_w_gate, v_ffn1_w_up, v_ffn1_w_down, v_mix_norm, v_ffn2_norm, v_ffn2_w_gate, v_ffn2_w_up, v_ffn2_w_down, v_ev_w_in, v_ev_b_f, v_ev_conv_w, v_ev_conv_b, v_ev_conv_norm, v_ev_q_norm, v_ev_k_norm, v_ev_w_out, v_od_w_in, v_od_conv_w, v_od_w_out):
    P = dict(ffn1_norm=ffn1_norm, ffn1_w_gate=ffn1_w_gate, ffn1_w_up=ffn1_w_up, ffn1_w_down=ffn1_w_down, mix_norm=mix_norm,
             ffn2_norm=ffn2_norm, ffn2_w_gate=ffn2_w_gate, ffn2_w_up=ffn2_w_up, ffn2_w_down=ffn2_w_down, ev_w_in=ev_w_in,
             ev_b_f=ev_b_f, ev_conv_w=ev_conv_w, ev_conv_b=ev_conv_b, ev_conv_norm=ev_conv_norm, ev_q_norm=ev_q_norm,
             ev_k_norm=ev_k_norm, ev_w_out=ev_w_out, od_w_in=od_w_in, od_conv_w=od_conv_w, od_w_out=od_w_out)
    M = dict(zip(WEIGHTS, [m_ffn1_norm, m_ffn1_w_gate, m_ffn1_w_up, m_ffn1_w_down, m_mix_norm, m_ffn2_norm, m_ffn2_w_gate,
                           m_ffn2_w_up, m_ffn2_w_down, m_ev_w_in, m_ev_b_f, m_ev_conv_w, m_ev_conv_b, m_ev_conv_norm,
                           m_ev_q_norm, m_ev_k_norm, m_ev_w_out, m_od_w_in, m_od_conv_w, m_od_w_out]))
    V = dict(zip(WEIGHTS, [v_ffn1_norm, v_ffn1_w_gate, v_ffn1_w_up, v_ffn1_w_down, v_mix_norm, v_ffn2_norm, v_ffn2_w_gate,
                           v_ffn2_w_up, v_ffn2_w_down, v_ev_w_in, v_ev_b_f, v_ev_conv_w, v_ev_conv_b, v_ev_conv_norm,
                           v_ev_q_norm, v_ev_k_norm, v_ev_w_out, v_od_w_in, v_od_conv_w, v_od_w_out]))
    for name in TRANSPOSED:
        P[name], M[name], V[name] = (jnp.swapaxes(a, 1, 2) for a in (P[name], M[name], V[name]))
    S, D = x.shape[1], x.shape[2]
    chip = 2 * lax.axis_index("x") + lax.axis_index("y")
    core = lax.axis_index("c")

    def own_slot(shard):
        return lax.dynamic_update_slice(lax.empty((4,) + shard.shape, shard.dtype), shard[None], (chip, 0, 0))

    taps = jnp.concatenate([_to_lanes(_pad_rows(ev_conv_w[0], 32), 32), _to_lanes(_pad_rows(od_conv_w[0], 8), 16)], axis=0)
    ici_send, ici_recv, *bufs = _ag_start([own_slot(P[name][l].astype(BF16)) for name, l in BIG] + [own_slot(taps)])
    cols = lambda a: a.transpose(1, 0, 2).reshape(a.shape[1], 4 * a.shape[2])
    W = {k: P[k] for k in ("ffn1_norm", "mix_norm", "ffn2_norm", "ev_b_f", "ev_q_norm", "ev_k_norm")}
    W["ev_conv_b"], W["ev_conv_norm"] = ev_conv_b, ev_conv_norm
    for tag in ("ffn1", "ffn2"):
        for kind in ("_w_gate", "_w_up", "_w_down"):
            W[tag + kind] = [None, None]
    passing = {}

    def pass_on(g, after):
        idx = [i for i, k in enumerate(BIG) if BLOCK_OF[k] == BLOCKS[g]]
        keys = [BIG[i] for i in idx] + (["taps"] if BLOCKS[g] == ("ev", 0) else [])
        passing[g] = (keys, _ag_mid(g, ici_send, ici_recv, [bufs[i] for i in idx], idx,
                                    bufs[-1] if BLOCKS[g] == ("ev", 0) else None, len(BIG), after))

    def need(block, after):
        g = BLOCKS.index(block)
        if g not in passing:
            pass_on(g, after)
        keys, (d_send, d_recv, *thru) = passing.pop(g)
        got = dict(zip(keys, _ag_wait(g, d_send, d_recv, thru, len(keys) - ("taps" in keys), after)))
        if 1 <= g < len(BLOCKS) - 1:
            pass_on(g + 1, after)
        for key, a in got.items():
            if key == "taps":
                continue
            name, l = key
            if name.startswith("ffn"):
                W[name][l] = a
            elif name.endswith("_w_in"):
                W[name] = cols(a)
            elif name.endswith("_w_out"):
                W[name] = a.reshape(4 * a.shape[1], D)
        if block == ("ev", 0):
            taps_all = got["taps"]
            W["ev_conv_w"] = cols(taps_all[:, :32].reshape(4, 32, 128))[:CONV_A_WIDTH]
            W["od_conv_w"] = cols(taps_all[:, 32:48].reshape(4, 8, 256))[:CONV_C_WIDTH]

    rows = lambda a: a.reshape(4, a.shape[0] // 4, a.shape[1])
    colsh = lambda a: a.reshape(a.shape[0], 4, a.shape[1] // 4).transpose(1, 0, 2)
    c_arr = core.reshape(1).astype(jnp.int32)
    where = jnp.stack([chip, core]).astype(jnp.int32)
    in_flight = []

    def done(block, block_grads):
        g = BLOCKS.index(block)
        keys = list(block_grads)
        gs = []
        for name, l in keys:
            a = block_grads[(name, l)]
            gs.append(colsh(a) if name == "ev_w_in" else rows(a) if name.endswith("_w_out") else a)
        others = _pair_exchange(gs)
        sums = [_pair_add(a, o, c_arr) for a, o in zip(gs, others)]
        send, recv, *rest = _chip_start(g, sums)
        in_flight.append((keys, send, recv, rest[:len(keys)], rest[len(keys):2 * len(keys)]))
        return rest[-1][0:1, 0:1]

    loss, grad_x, grads = _local_step(x[0], loss_target[0], W, need, done)

    order = [k for keys, *_ in in_flight for k in keys]
    landed = _chip_wait([f[1] for f in in_flight], [f[2] for f in in_flight], [len(f[0]) for f in in_flight],
                        [a for f in in_flight for a in f[3]], [a for f in in_flight for a in f[4]], grad_x)
    sums, recvd = landed[:len(order)], landed[len(order):]
    stacked = {}
    for (name, l), s, r in zip(order, sums, recvd):
        stacked[name] = _chip_sum(s, r, where, stacked.get(name), l, P[name].shape[0])
    layout = [(BIG_NAMES.index(name), l) for name, l in order]
    big_grads = dict(zip(BIG_NAMES, _pair_share([stacked[name] for name in BIG_NAMES], layout)))

    def small_grad(name):
        if name.endswith("_norm") and name[:3] in ("ffn", "mix"):
            return jnp.concatenate([grads[(name, 0)], grads[(name, 1)]], axis=0)
        return grads[(name, 0)]

    packed = jnp.concatenate([_to_lanes(small_grad(name), r) for name, r in SMALL], axis=0)
    total = _small_all_reduce(packed)
    small_grads, at = {}, 0
    for name, r in SMALL:
        part = total[at:at + r].reshape(-1)
        at += r
        if name == "ev_conv_w":
            full_g = part[:CONV_A_WIDTH * D_CONV].reshape(CONV_A_WIDTH, D_CONV)
            small_grads[name] = lax.dynamic_slice_in_dim(full_g, chip * (D_CONV // 4), D_CONV // 4, axis=1)[None]
        elif name == "od_conv_w":
            full_g = part[:CONV_C_WIDTH * D].reshape(CONV_C_WIDTH, D)
            small_grads[name] = lax.dynamic_slice_in_dim(full_g, chip * (D // 4), D // 4, axis=1)[None]
        else:
            small_grads[name] = part[:math.prod(P[name].shape)].reshape(P[name].shape)

    grad_w, delta_w, new_m, new_v = [], [], [], []
    for name in WEIGHTS:
        g = big_grads[name] if name in big_grads else small_grads[name]
        outs = (g,) + _adamw(P[name], g, M[name], V[name])
        if name in TRANSPOSED:
            outs = tuple(jnp.swapaxes(a, 1, 2) for a in outs)
        for acc, a in zip((grad_w, delta_w, new_m, new_v), outs):
            acc.append(a)
    loss_all = lax.psum(loss[0, 0], ("x", "y", "c"))
    return (loss_all, grad_x[None], *grad_w, *delta_w, *new_m, *new_v)
```

```python
import functools
import math

import jax
import jax.numpy as jnp
from jax import lax
from jax.experimental import pallas as pl
from jax.experimental.pallas import tpu as pltpu

F32, BF16 = jnp.float32, jnp.bfloat16
EPS = 1e-6
FFN_RES = 0.5
N_HEADS, HEAD_DIM = 8, 64
D_CONV = 512
D_ATTN = N_HEADS * HEAD_DIM
CONV_A_WIDTH, CONV_C_WIDTH = 31, 3
ADAM_LR, ADAM_B1, ADAM_B2, ADAM_EPS, ADAM_WD, ADAM_STEP = 0.001, 0.9, 0.999, 1e-08, 0.01, 10
MESH = pl.DeviceIdType.MESH
ANY = pl.BlockSpec(memory_space=pl.ANY)

TOK_TILE = 512
DW_TILE = 1024
ATT_TILE = 512
QKN_TILE = 2048
HALO_A, HALO_C = 32, 16
SCAN_BLK = 256
MIB = 2 ** 20


def _pallas(body, **kw):
    return pl.pallas_call(body, **kw)


def _cp(sem=None, vmem_mib=48):
    return pltpu.CompilerParams(dimension_semantics=sem, vmem_limit_bytes=vmem_mib * MIB)


def _dot(a, b):
    return jnp.dot(a, b, preferred_element_type=F32)


def _dot_nt(a, b):
    return lax.dot_general(a, b, (((1,), (1,)), ((), ())), preferred_element_type=F32)


def _dot_tn(a, b):
    return lax.dot_general(a, b, (((0,), (0,)), ((), ())), preferred_element_type=F32)


def _sds(shape, dtype):
    return jax.ShapeDtypeStruct(shape, dtype)


def _rms(x):
    return lax.rsqrt(jnp.mean(x * x, axis=-1, keepdims=True) + EPS)


def _rms_bwd(dy, x, g):
    r = _rms(x)
    xh = x * r
    dxh = dy * g
    dx = r * (dxh - xh * jnp.mean(dxh * xh, axis=-1, keepdims=True))
    return dx, xh


def _silu_grad(z):
    s = jax.nn.sigmoid(z)
    return s * (1.0 + z * (1.0 - s))


def _ffn_fwd(x, g, wg, wu, wd):
    S, D = x.shape
    nc, Fs, _ = wd.shape
    tm = TOK_TILE

    def body(x_ref, g_ref, wg_ref, wu_ref, wd_ref, out_ref, xn_ref, G_ref, U_ref, acc_ref):
        j = pl.program_id(1)

        @pl.when(j == 0)
        def _():
            xv = x_ref[...]
            xn_ref[...] = (xv * _rms(xv) * g_ref[...]).astype(BF16)
            acc_ref[...] = jnp.zeros_like(acc_ref)

        xn = xn_ref[...]
        G = _dot_nt(xn, wg_ref[0])
        U = _dot_nt(xn, wu_ref[0])
        G_ref[0] = G.astype(BF16)
        U_ref[0] = U.astype(BF16)
        H = (G * jax.nn.sigmoid(G) * U).astype(BF16)
        acc_ref[...] += _dot(H, wd_ref[0])

        @pl.when(j == nc - 1)
        def _():
            out_ref[...] = x_ref[...] + FFN_RES * acc_ref[...]

    row = pl.BlockSpec((tm, D), lambda i, j: (i, 0))
    return _pallas(
        body, name="ffn_fwd", grid=(S // tm, nc),
        in_specs=[row, pl.BlockSpec((1, D), lambda i, j: (0, 0)),
                  pl.BlockSpec((1, Fs, D), lambda i, j: (j, 0, 0)), pl.BlockSpec((1, Fs, D), lambda i, j: (j, 0, 0)),
                  pl.BlockSpec((1, Fs, D), lambda i, j: (j, 0, 0))],
        out_specs=[row, row, pl.BlockSpec((1, tm, Fs), lambda i, j: (j, i, 0)),
                   pl.BlockSpec((1, tm, Fs), lambda i, j: (j, i, 0))],
        out_shape=[_sds((S, D), F32), _sds((S, D), BF16), _sds((nc, S, Fs), BF16), _sds((nc, S, Fs), BF16)],
        scratch_shapes=[pltpu.VMEM((tm, D), F32)],
        compiler_params=_cp(("parallel", "arbitrary")),
    )(x, g, wg, wu, wd)


def _ffn_bwd_w(dout, xn, G, U, wd):
    S, D = dout.shape
    nc, _, Fs = G.shape
    tm = TOK_TILE
    nt = S // tm

    def body(do_ref, xn_ref, G_ref, U_ref, wd_ref, dwg_ref, dwu_ref, dwd_ref, dG_ref, dU_ref, ag, au, ad):
        i = pl.program_id(1)

        @pl.when(i == 0)
        def _():
            ag[...] = jnp.zeros_like(ag)
            au[...] = jnp.zeros_like(au)
            ad[...] = jnp.zeros_like(ad)

        do = (FFN_RES * do_ref[...]).astype(BF16)
        Gv = G_ref[0].astype(F32)
        Uv = U_ref[0].astype(F32)
        dH = _dot_nt(do, wd_ref[0])
        sg = jax.nn.sigmoid(Gv)
        act = Gv * sg
        H = (act * Uv).astype(BF16)
        dU = (dH * act).astype(BF16)
        dG = (dH * Uv * (sg * (1.0 + Gv * (1.0 - sg)))).astype(BF16)
        dG_ref[0] = dG
        dU_ref[0] = dU
        xnv = xn_ref[...]
        ag[...] += _dot_tn(dG, xnv)
        au[...] += _dot_tn(dU, xnv)
        ad[...] += _dot_tn(H, do)

        @pl.when(i == nt - 1)
        def _():
            dwg_ref[0] = ag[...].astype(BF16)
            dwu_ref[0] = au[...].astype(BF16)
            dwd_ref[0] = ad[...].astype(BF16)

    row = pl.BlockSpec((tm, D), lambda j, i: (i, 0))
    hid = pl.BlockSpec((1, tm, Fs), lambda j, i: (j, i, 0))
    wrow = pl.BlockSpec((1, Fs, D), lambda j, i: (j, 0, 0))
    return _pallas(
        body, name="ffn_bwd_w", grid=(nc, nt),
        in_specs=[row, row, hid, hid, wrow],
        out_specs=[wrow, wrow, wrow, hid, hid],
        out_shape=[_sds((nc, Fs, D), BF16)] * 3 + [_sds((nc, S, Fs), BF16)] * 2,
        scratch_shapes=[pltpu.VMEM((Fs, D), F32)] * 3,
        compiler_params=_cp(("parallel", "arbitrary"), 56),
    )(dout, xn, G, U, wd)


def _norm_in_bwd(dzs, ws, x, g, dres, w_rows=False):
    S, D = x.shape
    nc = dzs[0].shape[0]
    n = len(dzs)
    tm = TOK_TILE

    def body(*refs):
        dz_refs, w_refs = refs[:n], refs[n:2 * n]
        x_ref, g_ref, dres_ref, dx_ref, dg_ref, acc_ref = refs[2 * n:]
        i, j = pl.program_id(0), pl.program_id(1)

        @pl.when(j == 0)
        def _():
            acc_ref[...] = jnp.zeros_like(acc_ref)

        @pl.when((i == 0) & (j == 0))
        def _():
            dg_ref[...] = jnp.zeros_like(dg_ref)

        for dz_ref, w_ref in zip(dz_refs, w_refs):
            acc_ref[...] += _dot(dz_ref[0], w_ref[0]) if w_rows else _dot_nt(dz_ref[0], w_ref[0])

        @pl.when(j == nc - 1)
        def _():
            dxn = acc_ref[...]
            dx, xh = _rms_bwd(dxn, x_ref[...], g_ref[...])
            dx_ref[...] = dx + dres_ref[...]
            dg_ref[...] += jnp.sum(dxn * xh, axis=0, keepdims=True)

    row = pl.BlockSpec((tm, D), lambda i, j: (i, 0))
    one = pl.BlockSpec((1, D), lambda i, j: (0, 0))
    in_specs = [pl.BlockSpec((1, tm, dz.shape[2]), lambda i, j: (j, i, 0)) for dz in dzs]
    in_specs += [pl.BlockSpec((1,) + w.shape[1:], lambda i, j: (j, 0, 0)) for w in ws]
    return _pallas(
        body, name="norm_in_bwd", grid=(S // tm, nc),
        in_specs=in_specs + [row, one, row], out_specs=[row, one],
        out_shape=[_sds((S, D), F32), _sds((1, D), F32)],
        scratch_shapes=[pltpu.VMEM((tm, D), F32)],
        compiler_params=_cp(("arbitrary", "arbitrary")),
    )(*dzs, *ws, x, g, dres)


def _norm_proj(x, g, w, w2=None):
    S, D = x.shape
    N = w.shape[1]
    tm = TOK_TILE

    def body(*refs):
        if w2 is None:
            x_ref, g_ref, w_ref, h_ref, z_ref = refs
        else:
            x_ref, g_ref, w_ref, w2_ref, h_ref, z_ref, z2_ref = refs
        xv = x_ref[...]
        h = (xv * _rms(xv) * g_ref[...]).astype(BF16)
        h_ref[...] = h
        z_ref[...] = _dot(h, w_ref[...]).astype(BF16)
        if w2 is not None:
            z2_ref[...] = _dot(h, w2_ref[...])

    row = pl.BlockSpec((tm, D), lambda i: (i, 0))
    in_specs = [row, pl.BlockSpec((1, D), lambda i: (0, 0)), pl.BlockSpec((D, N), lambda i: (0, 0))]
    out_specs = [row, pl.BlockSpec((tm, N), lambda i: (i, 0))]
    out_shape = [_sds((S, D), BF16), _sds((S, N), BF16)]
    args = [x, g, w]
    if w2 is not None:
        N2 = w2.shape[1]
        in_specs.append(pl.BlockSpec((D, N2), lambda i: (0, 0)))
        out_specs.append(pl.BlockSpec((tm, N2), lambda i: (i, 0)))
        out_shape.append(_sds((S, N2), F32))
        args.append(w2)
    return _pallas(body, name="norm_proj", grid=(S // tm,), in_specs=in_specs, out_specs=out_specs,
                   out_shape=out_shape, compiler_params=_cp(("parallel",)))(*args)


def _proj_res(acts, ws, res):
    S, D = res.shape
    n = len(acts)
    tm = TOK_TILE

    def body(*refs):
        a_refs, w_refs = refs[:n], refs[n:2 * n]
        res_ref, out_ref = refs[2 * n:]
        acc = res_ref[...]
        for a_ref, w_ref in zip(a_refs, w_refs):
            acc = acc + _dot(a_ref[...], w_ref[...])
        out_ref[...] = acc

    row = pl.BlockSpec((tm, D), lambda i: (i, 0))
    in_specs = [pl.BlockSpec((tm, a.shape[1]), lambda i: (i, 0)) for a in acts]
    in_specs += [pl.BlockSpec(w.shape, lambda i: (0, 0)) for w in ws]
    return _pallas(body, name="proj_res", grid=(S // tm,), in_specs=in_specs + [row], out_specs=row,
                   out_shape=_sds((S, D), F32), compiler_params=_cp(("parallel",)))(*acts, *ws, res)


def _matmul_nt(a, w, after=None):
    S, K = a.shape
    M = w.shape[0]
    tm = TOK_TILE

    def body(a_ref, w_ref, *rest):
        rest[-1][...] = _dot_nt(a_ref[...].astype(BF16), w_ref[...])

    extra = [] if after is None else [after]
    return _pallas(body, name="matmul_nt", grid=(S // tm,),
                   in_specs=[pl.BlockSpec((tm, K), lambda i: (i, 0)), pl.BlockSpec((M, K), lambda i: (0, 0))] + [ANY] * len(extra),
                   out_specs=pl.BlockSpec((tm, M), lambda i: (i, 0)), out_shape=_sds((S, M), F32),
                   compiler_params=_cp(("parallel",)))(a, w, *extra)


def _matmul_tn(a, b, tn):
    S, M = a.shape
    N = b.shape[1]
    tm = min(DW_TILE, S)
    nt = S // tm

    def body(a_ref, b_ref, o_ref, acc_ref):
        i = pl.program_id(1)

        @pl.when(i == 0)
        def _():
            acc_ref[...] = jnp.zeros_like(acc_ref)

        acc_ref[...] += _dot_tn(a_ref[...].astype(BF16), b_ref[...].astype(BF16))

        @pl.when(i == nt - 1)
        def _():
            o_ref[0] = acc_ref[...].astype(BF16)

    return _pallas(body, name="matmul_tn", grid=(N // tn, nt),
                   in_specs=[pl.BlockSpec((tm, M), lambda j, i: (i, 0)), pl.BlockSpec((tm, tn), lambda j, i: (i, j))],
                   out_specs=pl.BlockSpec((1, M, tn), lambda j, i: (j, 0, 0)), out_shape=_sds((N // tn, M, tn), BF16),
                   scratch_shapes=[pltpu.VMEM((M, tn), F32)],
                   compiler_params=_cp(("parallel", "arbitrary")))(a, b)


def _conv_a_fwd(z, cw, cb, cn):
    S = z.shape[0]
    C = D_CONV
    tm = TOK_TILE
    hb = tm // HALO_A

    def body(u_ref, gt_ref, up_ref, gp_ref, cw_ref, cb_ref, cn_ref, a_ref, a1_ref, win):
        i = pl.program_id(0)
        prev = up_ref[...].astype(F32) * jax.nn.sigmoid(gp_ref[...].astype(F32))
        win[pl.ds(0, HALO_A), :] = jnp.where(i == 0, 0.0, prev)
        win[pl.ds(HALO_A, tm), :] = u_ref[...].astype(F32) * jax.nn.sigmoid(gt_ref[...].astype(F32))
        acc = jnp.zeros((tm, C), F32)
        for k in range(CONV_A_WIDTH):
            acc = acc + cw_ref[k:k + 1, :] * win[pl.ds(HALO_A - (CONV_A_WIDTH - 1) + k, tm), :]
        a1 = acc + cb_ref[...]
        a1_ref[...] = a1
        a2 = a1 * _rms(a1) * cn_ref[...]
        a_ref[...] = (a2 * jax.nn.sigmoid(a2)).astype(BF16)

    cur = lambda c: pl.BlockSpec((tm, C), lambda i, c=c: (i, c))
    prv = lambda c: pl.BlockSpec((HALO_A, C), lambda i, c=c: (jnp.maximum(i * hb - 1, 0), c))
    vec = pl.BlockSpec((1, C), lambda i: (0, 0))
    return _pallas(body, name="conv_a_fwd", grid=(S // tm,),
                   in_specs=[cur(0), cur(1), prv(0), prv(1), pl.BlockSpec((32, C), lambda i: (0, 0)), vec, vec],
                   out_specs=[pl.BlockSpec((tm, C), lambda i: (i, 0)), pl.BlockSpec((tm, C), lambda i: (i, 0))],
                   out_shape=[_sds((S, C), BF16), _sds((S, C), F32)],
                   scratch_shapes=[pltpu.VMEM((tm + HALO_A, C), F32)],
                   compiler_params=_cp(("parallel",)))(z, z, z, z, cw, cb, cn)


def _conv_a_bwd(da, a1, z, cw, cn):
    S = z.shape[0]
    C = D_CONV
    tm = TOK_TILE
    hb = tm // HALO_A
    nt = S // tm
    W = CONV_A_WIDTH

    def body(da_ref, a1_ref, dan_ref, a1n_ref, u_ref, gt_ref, up_ref, gp_ref, cw_ref, cn_ref,
             duz_ref, dcw_ref, dcb_ref, dcn_ref, win, dwin):
        i = pl.program_id(0)

        @pl.when(i == 0)
        def _():
            dcw_ref[...] = jnp.zeros_like(dcw_ref)
            dcb_ref[...] = jnp.zeros_like(dcb_ref)
            dcn_ref[...] = jnp.zeros_like(dcn_ref)

        cnv = cn_ref[...]

        def da1_of(dav, a1v):
            a2 = a1v * _rms(a1v) * cnv
            da2 = dav * _silu_grad(a2)
            dx, xh = _rms_bwd(da2, a1v, cnv)
            return dx, da2 * xh

        da1, dcn_t = da1_of(da_ref[...], a1_ref[...])
        da1n, _ = da1_of(dan_ref[...], a1n_ref[...])
        dwin[pl.ds(0, tm), :] = da1
        dwin[pl.ds(tm, HALO_A), :] = jnp.where(i == nt - 1, 0.0, da1n)
        dcb_ref[...] += jnp.sum(da1, axis=0, keepdims=True)
        dcn_ref[...] += jnp.sum(dcn_t, axis=0, keepdims=True)

        u = u_ref[...].astype(F32)
        sg = jax.nn.sigmoid(gt_ref[...].astype(F32))
        prev = up_ref[...].astype(F32) * jax.nn.sigmoid(gp_ref[...].astype(F32))
        win[pl.ds(0, HALO_A), :] = jnp.where(i == 0, 0.0, prev)
        win[pl.ds(HALO_A, tm), :] = u * sg

        da0 = jnp.zeros((tm, C), F32)
        for k in range(W):
            da0 = da0 + cw_ref[k:k + 1, :] * dwin[pl.ds(W - 1 - k, tm), :]
            dcw_ref[k:k + 1, :] += jnp.sum(da1 * win[pl.ds(HALO_A - (W - 1) + k, tm), :], axis=0, keepdims=True)
        duz_ref[:, 0:C] = (da0 * sg).astype(BF16)
        duz_ref[:, C:2 * C] = (da0 * u * sg * (1.0 - sg)).astype(BF16)

    cur = lambda c: pl.BlockSpec((tm, C), lambda i, c=c: (i, c))
    prv = lambda c: pl.BlockSpec((HALO_A, C), lambda i, c=c: (jnp.maximum(i * hb - 1, 0), c))
    nxt = pl.BlockSpec((HALO_A, C), lambda i: (jnp.minimum((i + 1) * hb, S // HALO_A - 1), 0))
    vec = pl.BlockSpec((1, C), lambda i: (0, 0))
    return _pallas(body, name="conv_a_bwd", grid=(nt,),
                   in_specs=[cur(0), cur(0), nxt, nxt, cur(0), cur(1), prv(0), prv(1),
                             pl.BlockSpec((32, C), lambda i: (0, 0)), vec],
                   out_specs=[pl.BlockSpec((tm, 2 * C), lambda i: (i, 0)), pl.BlockSpec((32, C), lambda i: (0, 0)), vec, vec],
                   out_shape=[_sds((S, 2 * C), BF16), _sds((32, C), F32), _sds((1, C), F32), _sds((1, C), F32)],
                   scratch_shapes=[pltpu.VMEM((tm + HALO_A, C), F32), pltpu.VMEM((tm + HALO_A, C), F32)],
                   compiler_params=_cp(("arbitrary",)))(da, a1, da, a1, z, z, z, z, cw, cn)


def _forget_scan(fl, bf):
    S, L = fl.shape
    B = SCAN_BLK

    def body(fl_ref, bf_ref, flb_ref, F_ref):
        tri = (lax.broadcasted_iota(jnp.int32, (B, B), 0) >= lax.broadcasted_iota(jnp.int32, (B, B), 1)).astype(F32)

        def step(c, carry):
            rows = pl.ds(pl.multiple_of(c * B, B), B)
            v = fl_ref[rows, :] + bf_ref[...]
            flb_ref[rows, :] = v
            lf = jnp.minimum(v, 0.0) - jnp.log1p(jnp.exp(-jnp.abs(v)))
            cs = jnp.dot(tri, lf, precision=lax.Precision.HIGHEST, preferred_element_type=F32) + carry
            F_ref[rows, :] = cs
            return cs[B - 1:B, :]

        lax.fori_loop(0, S // B, step, jnp.zeros((1, L), F32))

    return _pallas(body, name="forget_scan", out_shape=[_sds((S, L), F32), _sds((S, L), F32)],
                   compiler_params=_cp())(fl, bf)


def _forget_scan_bwd(dF, flb):
    S, L = dF.shape
    B = SCAN_BLK
    nb = S // B

    def body(dF_ref, flb_ref, dfl_ref, db_ref):
        tri = (lax.broadcasted_iota(jnp.int32, (B, B), 0) <= lax.broadcasted_iota(jnp.int32, (B, B), 1)).astype(F32)

        def step(t, carry):
            carry_cs, db = carry
            rows = pl.ds(pl.multiple_of((nb - 1 - t) * B, B), B)
            cs = jnp.dot(tri, dF_ref[rows, :], precision=lax.Precision.HIGHEST, preferred_element_type=F32) + carry_cs
            dfl = cs * jax.nn.sigmoid(-flb_ref[rows, :])
            dfl_ref[rows, :] = dfl
            return cs[0:1, :], db + jnp.sum(dfl, axis=0, keepdims=True)

        _, db = lax.fori_loop(0, nb, step, (jnp.zeros((1, L), F32), jnp.zeros((1, L), F32)))
        db_ref[...] = db

    return _pallas(body, name="forget_scan_bwd", out_shape=[_sds((S, L), F32), _sds((1, L), F32)],
                   compiler_params=_cp())(dF, flb)


NEG = -1e30


def _causal_mask(t):
    return lax.broadcasted_iota(jnp.int32, (t, t), 0) >= lax.broadcasted_iota(jnp.int32, (t, t), 1)


AUG = 128
C_F, C_ONE, C_LSE = 64, 67, 70


def _split3(f):
    a = f.astype(BF16).astype(F32)
    r = f - a
    b = r.astype(BF16).astype(F32)
    return a, b, r - b


def _put3(lane, base, parts, other):
    out = other
    for k, p in enumerate(parts):
        out = jnp.where(lane == base + k, p, out)
    return out


def _ones3(lane, base):
    return (lane >= base) & (lane < base + 3)


def _lane_ids(rows):
    return lax.broadcasted_iota(jnp.int32, (rows, AUG), 1)


def _pair_rms(x, lo):
    sq = x * x
    ms_a = jnp.sum(jnp.where(lo, sq, 0.0), axis=-1, keepdims=True) * (1.0 / HEAD_DIM)
    ms_b = jnp.sum(jnp.where(lo, 0.0, sq), axis=-1, keepdims=True) * (1.0 / HEAD_DIM)
    return jnp.where(lo, lax.rsqrt(ms_a + EPS), lax.rsqrt(ms_b + EPS))


def _qkv_prep(z, Fc, qw, kw):
    S = z.shape[0]
    tp = min(QKN_TILE, S)
    scale = 1.0 / math.sqrt(HEAD_DIM)

    def body(zq_ref, zk_ref, zv_ref, F_ref, qw_ref, kw_ref, q_ref, k_ref, v_ref):
        j = pl.program_id(0)
        lane = _lane_ids(tp)
        lo = lane < HEAD_DIM
        Fv = F_ref[...]
        xq = zq_ref[...].astype(F32)
        xk = zk_ref[...].astype(F32)
        qn = xq * _pair_rms(xq, lo) * qw_ref[...] * scale
        kn = xk * _pair_rms(xk, lo) * kw_ref[...]
        vv = zv_ref[...].astype(F32)
        for half in range(2):
            take = (lambda a: a) if half == 0 else (lambda a: pltpu.roll(a, HEAD_DIM, 1))
            fp = _split3(jnp.sum(jnp.where(lane == 2 * j + half, Fv, 0.0), axis=-1, keepdims=True))
            qx = _put3(lane, C_F, fp, jnp.where(_ones3(lane, C_ONE), 1.0, 0.0))
            kx = _put3(lane, C_ONE, [-p for p in fp], jnp.where(_ones3(lane, C_F) | _ones3(lane, C_LSE), 1.0, 0.0))
            vx = jnp.where(_ones3(lane, C_F), 1.0, 0.0)
            q_ref[half] = jnp.where(lo, take(qn), qx).astype(BF16)
            k_ref[half] = jnp.where(lo, take(kn), kx).astype(BF16)
            v_ref[half] = jnp.where(lo, take(vv), vx).astype(BF16)

    col = lambda c0: pl.BlockSpec((tp, AUG), lambda j, i, c0=c0: (i, c0 + j))
    vec = pl.BlockSpec((1, AUG), lambda j, i: (0, 0))
    out = pl.BlockSpec((2, tp, AUG), lambda j, i: (j, i, 0))
    return _pallas(body, name="qkv_prep", grid=(N_HEADS // 2, S // tp),
                   in_specs=[col(8), col(12), col(16), pl.BlockSpec((tp, AUG), lambda j, i: (i, 0)), vec, vec],
                   out_specs=[out, out, out], out_shape=[_sds((N_HEADS, S, AUG), BF16)] * 3,
                   compiler_params=_cp(("parallel", "parallel")))(z, z, z, Fc, qw, kw)


def _fox_fwd(q_aug, k_aug, v_aug):
    H, S, A = q_aug.shape
    t = ATT_TILE
    nq = S // t

    def body(q_ref, k_ref, v_ref, o_ref, q2_ref):
        i = pl.program_id(1)
        q = q_ref[0]

        def tile(j, carry, diag):
            m, acc = carry
            rows = pl.ds(pl.multiple_of(j * t, t), t)
            s = _dot_nt(q, k_ref[0, rows, :])
            if diag:
                s = jnp.where(_causal_mask(t), s, NEG)
            m_new = jnp.maximum(m, jnp.max(s, axis=-1, keepdims=True))
            p = jnp.exp(s - m_new)
            acc = jnp.exp(m - m_new) * acc + _dot(p.astype(BF16), v_ref[0, rows, :])
            return m_new, acc

        init = (jnp.full((t, 1), NEG, F32), jnp.zeros((t, A), F32))
        carry = lax.fori_loop(0, i, lambda j, c: tile(j, c, False), init)
        m, acc = tile(i, carry, True)
        lane = _lane_ids(t)
        l = jnp.sum(jnp.where(lane == C_F, acc, 0.0), axis=-1, keepdims=True)
        o_ref[0] = (acc / l).astype(BF16)
        lse = m + jnp.log(l)
        q2_ref[0] = (q.astype(F32) + _put3(lane, C_LSE, [-p for p in _split3(lse)], 0.0)).astype(BF16)

    qblk = pl.BlockSpec((1, t, A), lambda h, i: (h, i, 0))
    full = pl.BlockSpec((1, S, A), lambda h, i: (h, 0, 0))
    return _pallas(body, name="fox_fwd", grid=(H, nq), in_specs=[qblk, full, full], out_specs=[qblk, qblk],
                   out_shape=[_sds((H, S, A), BF16)] * 2, compiler_params=_cp(("parallel", "parallel")))(q_aug, k_aug, v_aug)


def _do_prep(dcat, o_aug):
    S = dcat.shape[0]
    tp = min(QKN_TILE, S)

    def body(d_ref, o_ref, out_ref):
        lane = _lane_ids(tp)
        lo = lane < HEAD_DIM
        x = d_ref[...]
        for half in range(2):
            d = jnp.where(lo, x if half == 0 else pltpu.roll(x, HEAD_DIM, 1), 0.0)
            delta = jnp.sum(d * o_ref[half].astype(F32), axis=-1, keepdims=True)
            out_ref[half] = jnp.where(lo, d, _put3(lane, C_F, [-p for p in _split3(delta)], 0.0)).astype(BF16)

    pair = pl.BlockSpec((2, tp, AUG), lambda j, i: (j, i, 0))
    return _pallas(body, name="do_prep", grid=(N_HEADS // 2, S // tp),
                   in_specs=[pl.BlockSpec((tp, AUG), lambda j, i: (i, D_CONV // AUG + j)), pair], out_specs=pair,
                   out_shape=_sds((N_HEADS, S, AUG), BF16), compiler_params=_cp(("parallel", "parallel")))(dcat, o_aug)


def _fox_bwd(q2, k_aug, v_aug, do_aug):
    H, S, A = q2.shape
    t = ATT_TILE
    nq = S // t

    def body(q_ref, k_ref, v_ref, do_ref, dq_ref, dk_ref, dv_ref):
        j = pl.program_id(1)

        @pl.when(j == 0)
        def _():
            dq_ref[...] = jnp.zeros_like(dq_ref)

        k = k_ref[0]
        vv = v_ref[0]

        def tile(i, carry, diag):
            dk, dv = carry
            rows = pl.ds(pl.multiple_of(i * t, t), t)
            q = q_ref[0, rows, :]
            dov = do_ref[0, rows, :]
            s = _dot_nt(q, k)
            if diag:
                s = jnp.where(_causal_mask(t), s, NEG)
            p = jnp.exp(s)
            dv = dv + _dot_tn(p.astype(BF16), dov)
            dsb = (p * _dot_nt(dov, vv)).astype(BF16)
            dq_ref[0, rows, :] += _dot(dsb, k)
            dk = dk + _dot_tn(dsb, q)
            return dk, dv

        init = (jnp.zeros((t, A), F32), jnp.zeros((t, A), F32))
        carry = tile(j, init, True)
        dk, dv = lax.fori_loop(j + 1, nq, lambda i, c: tile(i, c, False), carry)
        dk_ref[0] = dk
        dv_ref[0] = dv

    full = pl.BlockSpec((1, S, A), lambda h, j: (h, 0, 0))
    kblk = pl.BlockSpec((1, t, A), lambda h, j: (h, j, 0))
    return _pallas(body, name="fox_bwd", grid=(H, nq), in_specs=[full, kblk, kblk, full], out_specs=[full, kblk, kblk],
                   out_shape=[_sds((H, S, A), F32)] * 3,
                   compiler_params=_cp(("parallel", "arbitrary")))(q2, k_aug, v_aug, do_aug)


def _qkv_bwd(dq, dk, dv, z, qw, kw):
    S = z.shape[0]
    tp = min(QKN_TILE, S)
    scale = 1.0 / math.sqrt(HEAD_DIM)

    def body(dq_ref, dk_ref, dv_ref, zq_ref, zk_ref, qw_ref, kw_ref, dqf_ref, dkf_ref, dvf_ref, dF_ref, dqw_ref, dkw_ref):
        i, j = pl.program_id(0), pl.program_id(1)
        lane = _lane_ids(tp)
        lo = lane < HEAD_DIM

        @pl.when((i == 0) & (j == 0))
        def _():
            dqw_ref[...] = jnp.zeros_like(dqw_ref)
            dkw_ref[...] = jnp.zeros_like(dkw_ref)

        def pair(ref):
            return jnp.where(lo, ref[0], pltpu.roll(ref[1], HEAD_DIM, 1))

        def norm_bwd(g, x, w):
            r = _pair_rms(x, lo)
            xh = x * r
            dxh = g * w
            tt = dxh * xh
            mean_a = jnp.sum(jnp.where(lo, tt, 0.0), axis=-1, keepdims=True) * (1.0 / HEAD_DIM)
            mean_b = jnp.sum(jnp.where(lo, 0.0, tt), axis=-1, keepdims=True) * (1.0 / HEAD_DIM)
            return r * (dxh - xh * jnp.where(lo, mean_a, mean_b)), g * xh

        dxq, gq = norm_bwd(pair(dq_ref) * scale, zq_ref[...].astype(F32), qw_ref[...])
        dqf_ref[...] = dxq.astype(BF16)
        dqw_ref[...] += jnp.sum(gq, axis=0, keepdims=True)
        dxk, gk = norm_bwd(pair(dk_ref), zk_ref[...].astype(F32), kw_ref[...])
        dkf_ref[...] = dxk.astype(BF16)
        dkw_ref[...] += jnp.sum(gk, axis=0, keepdims=True)
        dvf_ref[...] = pair(dv_ref).astype(BF16)

        contrib = jnp.zeros((tp, AUG), F32)
        for half in range(2):
            df = (jnp.sum(jnp.where(lane == C_F, dq_ref[half], 0.0), axis=-1, keepdims=True)
                  - jnp.sum(jnp.where(lane == C_ONE, dk_ref[half], 0.0), axis=-1, keepdims=True))
            contrib = jnp.where(lane == 2 * j + half, df, contrib)

        @pl.when(j == 0)
        def _():
            dF_ref[...] = contrib

        @pl.when(j > 0)
        def _():
            dF_ref[...] += contrib

    pairb = pl.BlockSpec((2, tp, AUG), lambda i, j: (j, i, 0))
    col = lambda c0: pl.BlockSpec((tp, AUG), lambda i, j, c0=c0: (i, c0 + j))
    vec = pl.BlockSpec((1, AUG), lambda i, j: (0, 0))
    flat = pl.BlockSpec((tp, AUG), lambda i, j: (i, j))
    return _pallas(body, name="qkv_bwd", grid=(S // tp, N_HEADS // 2),
                   in_specs=[pairb, pairb, pairb, col(8), col(12), vec, vec],
                   out_specs=[flat, flat, flat, pl.BlockSpec((tp, AUG), lambda i, j: (i, 0)), vec, vec],
                   out_shape=[_sds((S, D_ATTN), BF16)] * 3 + [_sds((S, AUG), F32), _sds((1, AUG), F32), _sds((1, AUG), F32)],
                   compiler_params=_cp(("arbitrary", "arbitrary")))(dq, dk, dv, z, z, qw, kw)


def _proj_res_heads(a, wa, o_aug, wo, res):
    S, D = res.shape
    H = o_aug.shape[0]
    tm = TOK_TILE

    def body(a_ref, wa_ref, o_ref, wo_ref, res_ref, out_ref):
        acc = res_ref[...] + _dot(a_ref[...], wa_ref[...])
        for h in range(H):
            acc = acc + _dot(o_ref[h], wo_ref[h])
        out_ref[...] = acc

    row = pl.BlockSpec((tm, D), lambda i: (i, 0))
    return _pallas(body, name="proj_res_heads", grid=(S // tm,),
                   in_specs=[pl.BlockSpec((tm, a.shape[1]), lambda i: (i, 0)), pl.BlockSpec(wa.shape, lambda i: (0, 0)),
                             pl.BlockSpec((H, tm, AUG), lambda i: (0, i, 0)), pl.BlockSpec(wo.shape, lambda i: (0, 0, 0)), row],
                   out_specs=row, out_shape=_sds((S, D), F32), compiler_params=_cp(("parallel",)))(a, wa, o_aug, wo, res)


def _heads_tn(o_aug, d):
    H, S, A = o_aug.shape
    D = d.shape[1]
    tm = min(DW_TILE, S)
    nt = S // tm

    def body(o_ref, d_ref, out_ref, acc_ref):
        i = pl.program_id(0)

        @pl.when(i == 0)
        def _():
            acc_ref[...] = jnp.zeros_like(acc_ref)

        dv = d_ref[...].astype(BF16)
        for h in range(H):
            acc_ref[h] += _dot_tn(o_ref[h], dv)

        @pl.when(i == nt - 1)
        def _():
            out_ref[...] = acc_ref[...].astype(BF16)

    return _pallas(body, name="heads_tn", grid=(nt,),
                   in_specs=[pl.BlockSpec((H, tm, A), lambda i: (0, i, 0)), pl.BlockSpec((tm, D), lambda i: (i, 0))],
                   out_specs=pl.BlockSpec((H, A, D), lambda i: (0, 0, 0)), out_shape=_sds((H, A, D), BF16),
                   scratch_shapes=[pltpu.VMEM((H, A, D), F32)], compiler_params=_cp(("arbitrary",)))(o_aug, d)


def _odd_mid_fwd(z, cw):
    S = z.shape[0]
    D = z.shape[1] // 3
    tm = TOK_TILE
    hb = tm // HALO_C
    W = CONV_C_WIDTH

    def body(gb_ref, gc_ref, hh_ref, gcp_ref, hhp_ref, cw_ref, y_ref, win):
        i = pl.program_id(0)
        prev = gcp_ref[...].astype(F32) * hhp_ref[...].astype(F32)
        win[pl.ds(0, HALO_C), :] = jnp.where(i == 0, 0.0, prev)
        win[pl.ds(HALO_C, tm), :] = gc_ref[...].astype(F32) * hh_ref[...].astype(F32)
        c1 = jnp.zeros((tm, D), F32)
        for k in range(W):
            c1 = c1 + cw_ref[k:k + 1, :] * win[pl.ds(HALO_C - (W - 1) + k, tm), :]
        y_ref[...] = (gb_ref[...].astype(F32) * c1).astype(BF16)

    cur = lambda c: pl.BlockSpec((tm, D), lambda i, c=c: (i, c))
    prv = lambda c: pl.BlockSpec((HALO_C, D), lambda i, c=c: (jnp.maximum(i * hb - 1, 0), c))
    return _pallas(body, name="odd_mid_fwd", grid=(S // tm,),
                   in_specs=[cur(0), cur(1), cur(2), prv(1), prv(2), pl.BlockSpec((8, D), lambda i: (0, 0))],
                   out_specs=pl.BlockSpec((tm, D), lambda i: (i, 0)), out_shape=_sds((S, D), BF16),
                   scratch_shapes=[pltpu.VMEM((tm + HALO_C, D), F32)],
                   compiler_params=_cp(("parallel",)))(z, z, z, z, z, cw)


def _odd_mid_bwd(dy, z, cw):
    S = z.shape[0]
    D = z.shape[1] // 3
    tm = TOK_TILE
    hb = tm // HALO_C
    nt = S // tm
    W = CONV_C_WIDTH

    def body(dy_ref, dyn_ref, gb_ref, gbn_ref, gc_ref, hh_ref, gcp_ref, hhp_ref, cw_ref, dz_ref, dcw_ref, win, dwin):
        i = pl.program_id(0)

        @pl.when(i == 0)
        def _():
            dcw_ref[...] = jnp.zeros_like(dcw_ref)

        gc = gc_ref[...].astype(F32)
        hh = hh_ref[...].astype(F32)
        prev = gcp_ref[...].astype(F32) * hhp_ref[...].astype(F32)
        win[pl.ds(0, HALO_C), :] = jnp.where(i == 0, 0.0, prev)
        win[pl.ds(HALO_C, tm), :] = gc * hh
        dyv = dy_ref[...]
        dc1 = dyv * gb_ref[...].astype(F32)
        dwin[pl.ds(0, tm), :] = dc1
        dwin[pl.ds(tm, HALO_C), :] = jnp.where(i == nt - 1, 0.0, dyn_ref[...] * gbn_ref[...].astype(F32))
        c1 = jnp.zeros((tm, D), F32)
        dc0 = jnp.zeros((tm, D), F32)
        for k in range(W):
            tap = win[pl.ds(HALO_C - (W - 1) + k, tm), :]
            c1 = c1 + cw_ref[k:k + 1, :] * tap
            dc0 = dc0 + cw_ref[k:k + 1, :] * dwin[pl.ds(W - 1 - k, tm), :]
            dcw_ref[k:k + 1, :] += jnp.sum(dc1 * tap, axis=0, keepdims=True)
        dz_ref[:, 0:D] = (dyv * c1).astype(BF16)
        dz_ref[:, D:2 * D] = (dc0 * hh).astype(BF16)
        dz_ref[:, 2 * D:3 * D] = (dc0 * gc).astype(BF16)

    cur = lambda c: pl.BlockSpec((tm, D), lambda i, c=c: (i, c))
    prv = lambda c: pl.BlockSpec((HALO_C, D), lambda i, c=c: (jnp.maximum(i * hb - 1, 0), c))
    nxt = pl.BlockSpec((HALO_C, D), lambda i: (jnp.minimum((i + 1) * hb, S // HALO_C - 1), 0))
    return _pallas(body, name="odd_mid_bwd", grid=(nt,),
                   in_specs=[cur(0), nxt, cur(0), nxt, cur(1), cur(2), prv(1), prv(2), pl.BlockSpec((8, D), lambda i: (0, 0))],
                   out_specs=[pl.BlockSpec((tm, 3 * D), lambda i: (i, 0)), pl.BlockSpec((8, D), lambda i: (0, 0))],
                   out_shape=[_sds((S, 3 * D), BF16), _sds((8, D), F32)],
                   scratch_shapes=[pltpu.VMEM((tm + HALO_C, D), F32), pltpu.VMEM((tm + HALO_C, D), F32)],
                   compiler_params=_cp(("arbitrary",)))(dy, dy, z, z, z, z, z, z, cw)


def _loss_head(y, tgt):
    S, D = y.shape
    tm = TOK_TILE

    def body(y_ref, t_ref, dy_ref, l_ref):
        @pl.when(pl.program_id(0) == 0)
        def _():
            l_ref[...] = jnp.zeros_like(l_ref)

        e = y_ref[...] - t_ref[...]
        dy_ref[...] = e * (1.0 / D)
        l_ref[...] += jnp.sum(jnp.sum(e * e, axis=-1, keepdims=True), axis=0, keepdims=True) * (0.5 / D)

    row = pl.BlockSpec((tm, D), lambda i: (i, 0))
    return _pallas(body, name="loss_head", grid=(S // tm,), in_specs=[row, row],
                   out_specs=[row, pl.BlockSpec((1, 1), lambda i: (0, 0))],
                   out_shape=[_sds((S, D), F32), _sds((1, 1), F32)],
                   compiler_params=_cp(("arbitrary",)))(y, tgt)


def _pad_rows(a, rows):
    return jnp.pad(a, ((0, rows - a.shape[0]), (0, 0)))


def _local_step(x, tgt, W, need=lambda block, after: None, done=lambda block, block_grads: None):
    S, D = x.shape
    grads = {}
    saved = {}

    def gain_after(gain, token):
        return gain if token is None else gain + token

    def ffn_f(tag, l, xin):
        need((tag, l), xin)
        out, xn, G, U = _ffn_fwd(xin, W[tag + "_norm"][l:l + 1], W[tag + "_w_gate"][l], W[tag + "_w_up"][l],
                                 W[tag + "_w_down"][l])
        saved[(tag, l)] = (xin, xn, G, U)
        return out

    def ffn_b(tag, l, dout):
        xin, xn, G, U = saved[(tag, l)]
        keys = [(tag + "_w_gate", l), (tag + "_w_up", l), (tag + "_w_down", l)]
        *dws, dG, dU = _ffn_bwd_w(dout, xn, G, U, W[tag + "_w_down"][l])
        big = dict(zip(keys, dws))
        grads.update(big)
        token = done((tag, l), big)
        dx, dg = _norm_in_bwd([dG, dU], [W[tag + "_w_gate"][l], W[tag + "_w_up"][l]], xin,
                              gain_after(W[tag + "_norm"][l:l + 1], token), dout, w_rows=True)
        grads[(tag + "_norm", l)] = dg
        return dx

    x0a = ffn_f("ffn1", 0, x)
    need(("ev", 0), x0a)
    w_in = W["ev_w_in"]
    w_main, w_f = w_in[:, :2560], jnp.pad(w_in[:, 2560:], ((0, 0), (0, 120)))
    h0, z0, fl = _norm_proj(x0a, W["mix_norm"][0:1], w_main, w_f)
    cw_a = _pad_rows(W["ev_conv_w"], 32)
    a_act, a1 = _conv_a_fwd(z0, cw_a, W["ev_conv_b"], W["ev_conv_norm"])
    flb, Fc = _forget_scan(fl, jnp.pad(W["ev_b_f"], ((0, 0), (0, 120))))
    qw2, kw2 = jnp.tile(W["ev_q_norm"], (1, 2)), jnp.tile(W["ev_k_norm"], (1, 2))
    q_aug, k_aug, v_aug = _qkv_prep(z0, Fc, qw2, kw2)
    o_aug, q_lse = _fox_fwd(q_aug, k_aug, v_aug)
    w_out_e = W["ev_w_out"]
    w_out_o = jnp.pad(w_out_e[D_CONV:].reshape(N_HEADS, HEAD_DIM, D), ((0, 0), (0, AUG - HEAD_DIM), (0, 0)))
    x0b = _proj_res_heads(a_act, w_out_e[:D_CONV], o_aug, w_out_o, x0a)
    x0c = ffn_f("ffn2", 0, x0b)
    x1a = ffn_f("ffn1", 1, x0c)
    need(("od", 0), x1a)
    h1, z1 = _norm_proj(x1a, W["mix_norm"][1:2], W["od_w_in"])
    cw_c = _pad_rows(W["od_conv_w"], 8)
    y1 = _odd_mid_fwd(z1, cw_c)
    x1b = _proj_res([y1], [W["od_w_out"]], x1a)
    x1c = ffn_f("ffn2", 1, x1b)
    dy, loss = _loss_head(x1c, tgt)

    d = ffn_b("ffn2", 1, dy)
    dy1 = _matmul_nt(d, W["od_w_out"])
    grads[("od_w_out", 0)] = _matmul_tn(y1, d, D)[0]
    dz1, dcw_c = _odd_mid_bwd(dy1, z1, cw_c)
    grads[("od_conv_w", 0)] = dcw_c[:CONV_C_WIDTH]
    grads[("od_w_in", 0)] = _matmul_tn(h1, dz1, 3 * D // 4)
    token = done(("od", 0), {k: grads[k] for k in (("od_w_out", 0), ("od_w_in", 0))})
    d, dg = _norm_in_bwd([dz1[None]], [W["od_w_in"][None]], x1a, gain_after(W["mix_norm"][1:2], token), d)
    grads[("mix_norm", 1)] = dg
    d = ffn_b("ffn1", 1, d)
    d = ffn_b("ffn2", 0, d)
    dcat = _matmul_nt(d, w_out_e)
    grads[("ev_w_out", 0)] = jnp.concatenate([_matmul_tn(a_act, d, D)[0],
                                              _heads_tn(o_aug, d)[:, :HEAD_DIM].reshape(D_ATTN, D)], axis=0)
    duz, dcw_a, dcb, dcn = _conv_a_bwd(dcat, a1, z0, cw_a, W["ev_conv_norm"])
    grads[("ev_conv_w", 0)] = dcw_a[:CONV_A_WIDTH]
    grads[("ev_conv_b", 0)] = dcb
    grads[("ev_conv_norm", 0)] = dcn
    dq_a, dk_a, dv_a = _fox_bwd(q_lse, k_aug, v_aug, _do_prep(dcat, o_aug))
    dqf, dkf, dvf, dF, dqw, dkw = _qkv_bwd(dq_a, dk_a, dv_a, z0, qw2, kw2)
    grads[("ev_q_norm", 0)] = dqw[:, :HEAD_DIM] + dqw[:, HEAD_DIM:]
    grads[("ev_k_norm", 0)] = dkw[:, :HEAD_DIM] + dkw[:, HEAD_DIM:]
    dfl, dbf = _forget_scan_bwd(dF, flb)
    grads[("ev_b_f", 0)] = dbf[:, :N_HEADS]
    dz0 = jnp.concatenate([duz, dqf, dkf, dvf], axis=1)
    dflb = dfl.astype(BF16)
    gmain = _matmul_tn(h0, dz0, 640)
    gmain = gmain.transpose(1, 0, 2).reshape(D, 2560)
    gf = _matmul_tn(h0, dflb, 128)[0][:, :N_HEADS]
    grads[("ev_w_in", 0)] = jnp.concatenate([gmain, gf], axis=1)
    token = done(("ev", 0), {k: grads[k] for k in (("ev_w_out", 0), ("ev_w_in", 0))})
    d, dg = _norm_in_bwd([dz0[None], dflb[None]], [w_main[None], w_f[None]], x0a, gain_after(W["mix_norm"][0:1], token), d)
    grads[("mix_norm", 0)] = dg
    d = ffn_b("ffn1", 0, d)
    return loss, d, grads


def _place():
    x, y, c = lax.axis_index("x"), lax.axis_index("y"), lax.axis_index("c")
    chips = [(1 - x, y), (x, 1 - y), (1 - x, 1 - y)]
    return x, y, c, chips


def _remote(src, dst, send_sem, recv_sem, to):
    return pltpu.make_async_remote_copy(src_ref=src, dst_ref=dst, send_sem=send_sem, recv_sem=recv_sem,
                                        device_id=to, device_id_type=MESH)


HBM = pl.BlockSpec(memory_space=pltpu.HBM)
SEM = pl.BlockSpec(memory_space=pltpu.SEMAPHORE)
EFFECT = pltpu.SideEffectType.DATAFLOW_SIDE_EFFECTING


def _in_hbm(a):
    return pltpu.with_memory_space_constraint(a, pltpu.HBM)


def _ag_start(bufs):
    n = len(bufs)

    def body(*refs):
        send_sems, recv_sems = refs[n], refs[n + 1]
        outs = refs[n + 2:]
        x, y, c, chips = _place()
        me = 2 * x + y
        for a in [n - 1] + list(range(n - 1)):
            if a == n - 1:
                blk = outs[a].at[me]
            else:
                h = outs[a].shape[1] // 2
                blk = outs[a].at[me, pl.ds(c * h, h)]
            for jj, (px, py) in enumerate(chips):
                _remote(blk, blk, send_sems.at[3 * a + jj], recv_sems.at[3 * a + jj], (px, py, c)).start()

    return _pallas(
        body, name="gather_start",
        out_shape=[pltpu.SemaphoreType.DMA((3 * n,)), pltpu.SemaphoreType.DMA((3 * n,))] + [pltpu.HBM(b.shape, b.dtype) for b in bufs],
        in_specs=[HBM] * n, out_specs=[SEM, SEM] + [HBM] * n, input_output_aliases={a: 2 + a for a in range(n)},
        compiler_params=pltpu.CompilerParams(has_side_effects=EFFECT),
    )(*[_in_hbm(b) for b in bufs])


def _ag_mid(g, ici_send, ici_recv, bufs, idx, taps, n_big, after):
    n = len(bufs)
    arrs = list(bufs) + ([taps] if taps is not None else [])
    m = len(arrs)

    def body(*refs):
        ici_s, ici_r = refs[0], refs[1]
        d_send, d_recv = refs[m + 3], refs[m + 4]
        outs = refs[m + 5:]
        x, y, c, chips = _place()
        me = 2 * x + y
        for i in range(m):
            a = idx[i] if i < n else n_big
            for jj, (px, py) in enumerate(chips):
                k = 3 * a + jj
                if i < n:
                    h = outs[i].shape[1] // 2
                    mine, blk = outs[i].at[me, pl.ds(c * h, h)], outs[i].at[2 * px + py, pl.ds(c * h, h)]
                else:
                    mine, blk = outs[i].at[me], outs[i].at[2 * px + py]
                _remote(mine, mine, ici_s.at[k], ici_r.at[k], (px, py, c)).wait_send()
                _remote(blk, blk, ici_s.at[k], ici_r.at[k], (px, py, c)).wait_recv()
                if i < n:
                    _remote(blk, blk, d_send.at[3 * i + jj], d_recv.at[3 * i + jj], (x, y, 1 - c)).start()

    return _pallas(
        body, name=f"gather_pass_on_{g}",
        out_shape=[pltpu.SemaphoreType.DMA((3 * n,)), pltpu.SemaphoreType.DMA((3 * n,))] + [pltpu.HBM(b.shape, b.dtype) for b in arrs],
        in_specs=[SEM, SEM] + [HBM] * m + [ANY], out_specs=[SEM, SEM] + [HBM] * m,
        input_output_aliases={2 + i: 2 + i for i in range(m)},
        compiler_params=pltpu.CompilerParams(has_side_effects=EFFECT),
    )(ici_send, ici_recv, *arrs, after)


def _ag_wait(g, d_send, d_recv, arrs, n, after):
    m = len(arrs)

    def body(*refs):
        d_s, d_r = refs[0], refs[1]
        outs = refs[m + 3:]
        x, y, c, chips = _place()
        for i in range(n):
            h = outs[i].shape[1] // 2
            for jj, (px, py) in enumerate(chips):
                sent = outs[i].at[2 * px + py, pl.ds(c * h, h)]
                got = outs[i].at[2 * px + py, pl.ds((1 - c) * h, h)]
                _remote(sent, sent, d_s.at[3 * i + jj], d_r.at[3 * i + jj], (x, y, 1 - c)).wait_send()
                _remote(got, got, d_s.at[3 * i + jj], d_r.at[3 * i + jj], (x, y, 1 - c)).wait_recv()

    return _pallas(
        body, name=f"gather_wait_{g}", out_shape=[pltpu.HBM(b.shape, b.dtype) for b in arrs],
        in_specs=[SEM, SEM] + [HBM] * m + [ANY], out_specs=[HBM] * m,
        input_output_aliases={2 + i: i for i in range(m)},
        compiler_params=pltpu.CompilerParams(has_side_effects=EFFECT),
    )(d_send, d_recv, *arrs, after)


def _pair_exchange(gs):
    n = len(gs)

    def body(*refs):
        ins, outs = refs[:n], refs[n:2 * n]
        send_sems, recv_sems = refs[2 * n:]
        x, y, c, _ = _place()
        cps = []
        for a in range(n):
            h = ins[a].shape[1] // 2
            cps.append(_remote(ins[a].at[:, pl.ds((1 - c) * h, h)], outs[a], send_sems.at[a], recv_sems.at[a], (x, y, 1 - c)))
        for cp in cps:
            cp.start()
        for cp in cps:
            cp.wait()

    return _pallas(body, name="grad_pair_exchange", in_specs=[ANY] * n, out_specs=[ANY] * n,
                   out_shape=[_sds((4, g.shape[1] // 2, g.shape[2]), g.dtype) for g in gs],
                   scratch_shapes=[pltpu.SemaphoreType.DMA((n,)), pltpu.SemaphoreType.DMA((n,))])(*gs)


def _pair_add(g, other, c_arr):
    _, R, C = g.shape
    h = R // 2

    def body(c_ref, g_ref, o_ref, out_ref):
        out_ref[...] = (g_ref[...].astype(F32) + o_ref[...].astype(F32)).astype(BF16)

    grid_spec = pltpu.PrefetchScalarGridSpec(
        num_scalar_prefetch=1, grid=(4,),
        in_specs=[pl.BlockSpec((1, h, C), lambda k, c_ref: (k, c_ref[0], 0)), pl.BlockSpec((1, h, C), lambda k, c_ref: (k, 0, 0))],
        out_specs=pl.BlockSpec((1, h, C), lambda k, c_ref: (k, 0, 0)))
    return _pallas(body, name="grad_pair_add", grid_spec=grid_spec, out_shape=_sds((4, h, C), BF16),
                   compiler_params=_cp(("parallel",)))(c_arr, g, other)


def _chip_start(g, ss):
    n = len(ss)
    zones = [lax.empty((3,) + s.shape[1:], s.dtype) for s in ss]

    def body(*refs):
        send_sems, recv_sems = refs[2 * n], refs[2 * n + 1]
        src, dst = refs[2 * n + 2:3 * n + 2], refs[3 * n + 2:4 * n + 2]
        token = refs[4 * n + 2]
        x, y, c, chips = _place()
        for a in range(n):
            for jj, (px, py) in enumerate(chips):
                k = 3 * a + jj
                _remote(src[a].at[2 * px + py], dst[a].at[jj], send_sems.at[k], recv_sems.at[k], (px, py, c)).start()
        token[...] = jnp.zeros_like(token)

    return _pallas(
        body, name=f"grad_chip_start_{g}",
        out_shape=[pltpu.SemaphoreType.DMA((3 * n,)), pltpu.SemaphoreType.DMA((3 * n,))]
        + [pltpu.HBM(a.shape, a.dtype) for a in ss + zones] + [_sds((8, 128), F32)],
        in_specs=[HBM] * (2 * n), out_specs=[SEM, SEM] + [HBM] * (2 * n) + [pl.BlockSpec(memory_space=pltpu.VMEM)],
        input_output_aliases={i: 2 + i for i in range(2 * n)},
        compiler_params=pltpu.CompilerParams(has_side_effects=EFFECT),
    )(*[_in_hbm(a) for a in ss + zones])


def _chip_wait(sends, recvs, counts, ss, zones, after):
    nb, n = len(sends), len(ss)

    def body(*refs):
        s_refs, r_refs = refs[:nb], refs[nb:2 * nb]
        outs = refs[2 * nb + 2 * n + 1:]
        src, dst = outs[:n], outs[n:]
        x, y, c, chips = _place()
        a = 0
        for b in range(nb):
            for i in range(counts[b]):
                for jj, (px, py) in enumerate(chips):
                    k = 3 * i + jj
                    _remote(src[a].at[2 * px + py], dst[a].at[jj], s_refs[b].at[k], r_refs[b].at[k], (px, py, c)).wait()
                a += 1

    return _pallas(
        body, name="grad_chip_wait", out_shape=[pltpu.HBM(a.shape, a.dtype) for a in ss + zones],
        in_specs=[SEM] * (2 * nb) + [HBM] * (2 * n) + [ANY], out_specs=[HBM] * (2 * n),
        input_output_aliases={2 * nb + i: i for i in range(2 * n)},
        compiler_params=pltpu.CompilerParams(has_side_effects=EFFECT),
    )(*sends, *recvs, *ss, *zones, after)


def _chip_sum(s, r, where, dest, l, L):
    _, h, C = s.shape
    tr = h // 2

    def body(k_ref, s_ref, r_ref, *rest):
        out_ref = rest[-1]
        acc = s_ref[0].astype(F32)
        for jj in range(3):
            acc = acc + r_ref[jj].astype(F32)
        out_ref[...] = acc

    in_specs = [pl.BlockSpec((1, tr, C), lambda i, k_ref: (k_ref[0], i, 0)), pl.BlockSpec((3, tr, C), lambda i, k_ref: (0, i, 0))]
    args = [where, s, r]
    alias = {}
    if dest is not None:
        in_specs.append(ANY)
        args.append(dest)
        alias = {3: 0}
    grid_spec = pltpu.PrefetchScalarGridSpec(
        num_scalar_prefetch=1, grid=(2,), in_specs=in_specs,
        out_specs=pl.BlockSpec((None, tr, C), lambda i, k_ref: (l, 2 * k_ref[1] + i, 0)))
    return _pallas(body, name="grad_chip_sum", grid_spec=grid_spec, out_shape=_sds((L, 2 * h, C), F32),
                   input_output_aliases=alias, compiler_params=_cp(("arbitrary",)))(*args)


def _pair_share(bufs, layout):
    n = len(layout)
    n_out = len(bufs)

    def body(*refs):
        outs = refs[n_out:2 * n_out]
        send_sems, recv_sems = refs[2 * n_out:]
        x, y, c, _ = _place()
        cps = []
        for a, (o, l) in enumerate(layout):
            h = outs[o].shape[1] // 2
            blk = outs[o].at[l, pl.ds(c * h, h)]
            cps.append(_remote(blk, blk, send_sems.at[a], recv_sems.at[a], (x, y, 1 - c)))
        for cp in cps:
            cp.start()
        for a, (o, l) in enumerate(layout):
            h = outs[o].shape[1] // 2
            blk = outs[o].at[l, pl.ds((1 - c) * h, h)]
            _remote(blk, blk, send_sems.at[a], recv_sems.at[a], (x, y, 1 - c)).wait_recv()
        for cp in cps:
            cp.wait_send()

    return _pallas(body, name="grad_pair_share", in_specs=[ANY] * n_out, out_specs=[ANY] * n_out,
                   out_shape=[_sds(b.shape, b.dtype) for b in bufs], input_output_aliases={o: o for o in range(n_out)},
                   scratch_shapes=[pltpu.SemaphoreType.DMA((n,)), pltpu.SemaphoreType.DMA((n,))])(*bufs)


def _small_all_reduce(packed):
    P, L = packed.shape

    def body(in_ref, out_ref, slots, send_sems, recv_sems):
        x, y, c, _ = _place()
        me = 4 * x + 2 * y + c
        slots[me] = in_ref[...]
        cps = []
        for r in range(1, 8):
            px = 1 - x if r & 4 else x
            py = 1 - y if r & 2 else y
            pc = 1 - c if r & 1 else c
            cps.append(_remote(in_ref, slots.at[me], send_sems.at[r - 1], recv_sems.at[r - 1], (px, py, pc)))
        for cp in cps:
            cp.start()
        for r in range(1, 8):
            px = 1 - x if r & 4 else x
            py = 1 - y if r & 2 else y
            pc = 1 - c if r & 1 else c
            blk = slots.at[4 * px + 2 * py + pc]
            _remote(blk, blk, send_sems.at[r - 1], recv_sems.at[r - 1], (px, py, pc)).wait_recv()
        for cp in cps:
            cp.wait_send()
        acc = slots[0]
        for k in range(1, 8):
            acc = acc + slots[k]
        out_ref[...] = acc

    vm = pl.BlockSpec(memory_space=pltpu.VMEM)
    return _pallas(body, name="small_all_reduce", in_specs=[vm], out_specs=vm, out_shape=_sds((P, L), F32),
                   scratch_shapes=[pltpu.VMEM((8, P, L), F32), pltpu.SemaphoreType.DMA((7,)), pltpu.SemaphoreType.DMA((7,))])(packed)


def _adamw_math(w, g, m, v):
    m = ADAM_B1 * m + (1.0 - ADAM_B1) * g
    v = ADAM_B2 * v + (1.0 - ADAM_B2) * (g * g)
    m_hat = m / (1.0 - ADAM_B1 ** ADAM_STEP)
    v_hat = v / (1.0 - ADAM_B2 ** ADAM_STEP)
    delta = -ADAM_LR * (m_hat / (jnp.sqrt(v_hat) + ADAM_EPS) + ADAM_WD * w)
    return delta, m, v


def _adamw(w, g, m, v):
    shape = w.shape
    C = shape[-1]
    rows = math.prod(shape[:-1])
    tr = next(t for t in (512, 352, 256, 128, 64, 32, 16, 8, rows) if rows % t == 0)
    w2, g2, m2, v2 = (a.reshape(rows, C) for a in (w, g, m, v))

    def body(w_ref, g_ref, m_ref, v_ref, d_ref, nm_ref, nv_ref):
        d, nm, nv = _adamw_math(w_ref[...], g_ref[...], m_ref[...], v_ref[...])
        d_ref[...] = d
        nm_ref[...] = nm
        nv_ref[...] = nv

    blk = pl.BlockSpec((tr, C), lambda i: (i, 0))
    outs = _pallas(body, name="adamw", grid=(rows // tr,), in_specs=[blk] * 4, out_specs=[blk] * 3,
                   out_shape=[_sds((rows, C), F32)] * 3, compiler_params=_cp(("parallel",)))(w2, g2, m2, v2)
    return tuple(o.reshape(shape) for o in outs)


WEIGHTS = ["ffn1_norm", "ffn1_w_gate", "ffn1_w_up", "ffn1_w_down", "mix_norm", "ffn2_norm", "ffn2_w_gate", "ffn2_w_up",
           "ffn2_w_down", "ev_w_in", "ev_b_f", "ev_conv_w", "ev_conv_b", "ev_conv_norm", "ev_q_norm", "ev_k_norm",
           "ev_w_out", "od_w_in", "od_conv_w", "od_w_out"]
BIG = ([("ffn1_w_gate", 0), ("ffn1_w_up", 0), ("ffn1_w_down", 0), ("ev_w_in", 0), ("ev_w_out", 0),
        ("ffn2_w_gate", 0), ("ffn2_w_up", 0), ("ffn2_w_down", 0)]
       + [("ffn1_w_gate", 1), ("ffn1_w_up", 1), ("ffn1_w_down", 1), ("od_w_in", 0), ("od_w_out", 0),
          ("ffn2_w_gate", 1), ("ffn2_w_up", 1), ("ffn2_w_down", 1)])
TRANSPOSED = ("ffn1_w_gate", "ffn1_w_up", "ffn2_w_gate", "ffn2_w_up")
BLOCKS = [("ffn1", 0), ("ev", 0), ("ffn2", 0), ("ffn1", 1), ("od", 0), ("ffn2", 1)]
BLOCK_OF = {(name, l): (name.split("_w_")[0], l) for name, l in BIG}
BIG_NAMES = ["ffn1_w_gate", "ffn1_w_up", "ffn1_w_down", "ffn2_w_gate", "ffn2_w_up", "ffn2_w_down",
             "ev_w_in", "ev_w_out", "od_w_in", "od_w_out"]
SMALL = [("ffn1_norm", 16), ("mix_norm", 16), ("ffn2_norm", 16), ("ev_b_f", 8), ("ev_conv_w", 128), ("ev_conv_b", 8),
         ("ev_conv_norm", 8), ("ev_q_norm", 8), ("ev_k_norm", 8), ("od_conv_w", 24)]


def _to_lanes(a, rows):
    flat = a.reshape(-1)
    return jnp.pad(flat, (0, rows * 128 - flat.shape[0])).reshape(rows, 128)


def kernel(x, ffn1_norm, ffn1_w_gate, ffn1_w_up, ffn1_w_down, mix_norm, ffn2_norm, ffn2_w_gate, ffn2_w_up, ffn2_w_down, ev_w_in, ev_b_f, ev_conv_w, ev_conv_b, ev_conv_norm, ev_q_norm, ev_k_norm, ev_w_out, od_w_in, od_conv_w, od_w_out, loss_target, m_ffn1_norm, m_ffn1_w_gate, m_ffn1_w_up, m_ffn1_w_down, m_mix_norm, m_ffn2_norm, m_ffn2_w_gate, m_ffn2_w_up, m_ffn2_w_down, m_ev_w_in, m_ev_b_f, m_ev_conv_w, m_ev_conv_b, m_ev_conv_norm, m_ev_q_norm, m_ev_k_norm, m_ev_w_out, m_od_w_in, m_od_conv_w, m_od_w_out, v_ffn1_norm, v_ffn1_w_gate, v_ffn1_w_up, v_ffn1_w_down, v_mix_norm, v_ffn2_norm, v_ffn2_w_gate, v_ffn2_w_up, v_ffn2_w_down, v_ev_w_in, v_ev_b_f, v_ev_conv_w, v_ev_conv_b, v_ev_conv_norm, v_ev_q_norm, v_ev_k_norm, v_ev_w_out, v_od_w_in, v_od_conv_w, v_od_w_out):
    P = dict(ffn1_norm=ffn1_norm, ffn1_w_gate=ffn1_w_gate, ffn1_w_up=ffn1_w_up, ffn1_w_down=ffn1_w_down, mix_norm=mix_norm,
             ffn2_norm=ffn2_norm, ffn2_w_gate=ffn2_w_gate, ffn2_w_up=ffn2_w_up, ffn2_w_down=ffn2_w_down, ev_w_in=ev_w_in,
             ev_b_f=ev_b_f, ev_conv_w=ev_conv_w, ev_conv_b=ev_conv_b, ev_conv_norm=ev_conv_norm, ev_q_norm=ev_q_norm,
             ev_k_norm=ev_k_norm, ev_w_out=ev_w_out, od_w_in=od_w_in, od_conv_w=od_conv_w, od_w_out=od_w_out)
    M = dict(zip(WEIGHTS, [m_ffn1_norm, m_ffn1_w_gate, m_ffn1_w_up, m_ffn1_w_down, m_mix_norm, m_ffn2_norm, m_ffn2_w_gate,
                           m_ffn2_w_up, m_ffn2_w_down, m_ev_w_in, m_ev_b_f, m_ev_conv_w, m_ev_conv_b, m_ev_conv_norm,
                           m_ev_q_norm, m_ev_k_norm, m_ev_w_out, m_od_w_in, m_od_conv_w, m_od_w_out]))
    V = dict(zip(WEIGHTS, [v_ffn1_norm, v_ffn1_w_gate, v_ffn1_w_up, v_ffn1_w_down, v_mix_norm, v_ffn2_norm, v_ffn2_w_gate,
                           v_ffn2_w_up, v_ffn2_w_down, v_ev_w_in, v_ev_b_f, v_ev_conv_w, v_ev_conv_b, v_ev_conv_norm,
                           v_ev_q_norm, v_ev_k_norm, v_ev_w_out, v_od_w_in, v_od_conv_w, v_od_w_out]))
    for name in TRANSPOSED:
        P[name], M[name], V[name] = (jnp.swapaxes(a, 1, 2) for a in (P[name], M[name], V[name]))
    S, D = x.shape[1], x.shape[2]
    chip = 2 * lax.axis_index("x") + lax.axis_index("y")
    core = lax.axis_index("c")

    def own_slot(shard):
        return lax.dynamic_update_slice(lax.empty((4,) + shard.shape, shard.dtype), shard[None], (chip, 0, 0))

    taps = jnp.concatenate([_to_lanes(_pad_rows(ev_conv_w[0], 32), 32), _to_lanes(_pad_rows(od_conv_w[0], 8), 16)], axis=0)
    ici_send, ici_recv, *bufs = _ag_start([own_slot(P[name][l].astype(BF16)) for name, l in BIG] + [own_slot(taps)])
    cols = lambda a: a.transpose(1, 0, 2).reshape(a.shape[1], 4 * a.shape[2])
    W = {k: P[k] for k in ("ffn1_norm", "mix_norm", "ffn2_norm", "ev_b_f", "ev_q_norm", "ev_k_norm")}
    W["ev_conv_b"], W["ev_conv_norm"] = ev_conv_b, ev_conv_norm
    for tag in ("ffn1", "ffn2"):
        for kind in ("_w_gate", "_w_up", "_w_down"):
            W[tag + kind] = [None, None]
    passing = {}

    def pass_on(g, after):
        idx = [i for i, k in enumerate(BIG) if BLOCK_OF[k] == BLOCKS[g]]
        keys = [BIG[i] for i in idx] + (["taps"] if BLOCKS[g] == ("ev", 0) else [])
        passing[g] = (keys, _ag_mid(g, ici_send, ici_recv, [bufs[i] for i in idx], idx,
                                    bufs[-1] if BLOCKS[g] == ("ev", 0) else None, len(BIG), after))

    def need(block, after):
        g = BLOCKS.index(block)
        if g not in passing:
            pass_on(g, after)
        keys, (d_send, d_recv, *thru) = passing.pop(g)
        got = dict(zip(keys, _ag_wait(g, d_send, d_recv, thru, len(keys) - ("taps" in keys), after)))
        if 1 <= g < len(BLOCKS) - 1:
            pass_on(g + 1, after)
        for key, a in got.items():
            if key == "taps":
                continue
            name, l = key
            if name.startswith("ffn"):
                W[name][l] = a
            elif name.endswith("_w_in"):
                W[name] = cols(a)
            elif name.endswith("_w_out"):
                W[name] = a.reshape(4 * a.shape[1], D)
        if block == ("ev", 0):
            taps_all = got["taps"]
            W["ev_conv_w"] = cols(taps_all[:, :32].reshape(4, 32, 128))[:CONV_A_WIDTH]
            W["od_conv_w"] = cols(taps_all[:, 32:48].reshape(4, 8, 256))[:CONV_C_WIDTH]

    rows = lambda a: a.reshape(4, a.shape[0] // 4, a.shape[1])
    colsh = lambda a: a.reshape(a.shape[0], 4, a.shape[1] // 4).transpose(1, 0, 2)
    c_arr = core.reshape(1).astype(jnp.int32)
    where = jnp.stack([chip, core]).astype(jnp.int32)
    in_flight = []

    def done(block, block_grads):
        g = BLOCKS.index(block)
        keys = list(block_grads)
        gs = []
        for name, l in keys:
            a = block_grads[(name, l)]
            gs.append(colsh(a) if name == "ev_w_in" else rows(a) if name.endswith("_w_out") else a)
        others = _pair_exchange(gs)
        sums = [_pair_add(a, o, c_arr) for a, o in zip(gs, others)]
        send, recv, *rest = _chip_start(g, sums)
        in_flight.append((keys, send, recv, rest[:len(keys)], rest[len(keys):2 * len(keys)]))
        return rest[-1][0:1, 0:1]

    loss, grad_x, grads = _local_step(x[0], loss_target[0], W, need, done)

    order = [k for keys, *_ in in_flight for k in keys]
    landed = _chip_wait([f[1] for f in in_flight], [f[2] for f in in_flight], [len(f[0]) for f in in_flight],
                        [a for f in in_flight for a in f[3]], [a for f in in_flight for a in f[4]], grad_x)
    sums, recvd = landed[:len(order)], landed[len(order):]
    stacked = {}
    for (name, l), s, r in zip(order, sums, recvd):
        stacked[name] = _chip_sum(s, r, where, stacked.get(name), l, P[name].shape[0])
    layout = [(BIG_NAMES.index(name), l) for name, l in order]
    big_grads = dict(zip(BIG_NAMES, _pair_share([stacked[name] for name in BIG_NAMES], layout)))

    def small_grad(name):
        if name.endswith("_norm") and name[:3] in ("ffn", "mix"):
            return jnp.concatenate([grads[(name, 0)], grads[(name, 1)]], axis=0)
        return grads[(name, 0)]

    packed = jnp.concatenate([_to_lanes(small_grad(name), r) for name, r in SMALL], axis=0)
    total = _small_all_reduce(packed)
    small_grads, at = {}, 0
    for name, r in SMALL:
        part = total[at:at + r].reshape(-1)
        at += r
        if name == "ev_conv_w":
            full_g = part[:CONV_A_WIDTH * D_CONV].reshape(CONV_A_WIDTH, D_CONV)
            small_grads[name] = lax.dynamic_slice_in_dim(full_g, chip * (D_CONV // 4), D_CONV // 4, axis=1)[None]
        elif name == "od_conv_w":
            full_g = part[:CONV_C_WIDTH * D].reshape(CONV_C_WIDTH, D)
            small_grads[name] = lax.dynamic_slice_in_dim(full_g, chip * (D // 4), D // 4, axis=1)[None]
        else:
            small_grads[name] = part[:math.prod(P[name].shape)].reshape(P[name].shape)

    grad_w, delta_w, new_m, new_v = [], [], [], []
    for name in WEIGHTS:
        g = big_grads[name] if name in big_grads else small_grads[name]
        outs = (g,) + _adamw(P[name], g, M[name], V[name])
        if name in TRANSPOSED:
            outs = tuple(jnp.swapaxes(a, 1, 2) for a in outs)
        for acc, a in zip((grad_w, delta_w, new_m, new_v), outs):
            acc.append(a)
    loss_all = lax.psum(loss[0, 0], ("x", "y", "c"))
    return (loss_all, grad_x[None], *grad_w, *delta_w, *new_m, *new_v)
```

```python
import functools
import math

import jax
import jax.numpy as jnp
from jax import lax
from jax.experimental import pallas as pl
from jax.experimental.pallas import tpu as pltpu

F32, BF16 = jnp.float32, jnp.bfloat16
EPS = 1e-6
FFN_RES = 0.5
N_HEADS, HEAD_DIM = 8, 64
D_CONV = 512
D_ATTN = N_HEADS * HEAD_DIM
CONV_A_WIDTH, CONV_C_WIDTH = 31, 3
ADAM_LR, ADAM_B1, ADAM_B2, ADAM_EPS, ADAM_WD, ADAM_STEP = 0.001, 0.9, 0.999, 1e-08, 0.01, 10
MESH = pl.DeviceIdType.MESH
ANY = pl.BlockSpec(memory_space=pl.ANY)

TOK_TILE = 512
DW_TILE = 1024
ATT_TILE = 512
QKN_TILE = 2048
HALO_A, HALO_C = 32, 16
SCAN_BLK = 256
MIB = 2 ** 20


def _pallas(body, **kw):
    return pl.pallas_call(body, **kw)


def _cp(sem=None, vmem_mib=48):
    return pltpu.CompilerParams(dimension_semantics=sem, vmem_limit_bytes=vmem_mib * MIB)


def _dot(a, b):
    return jnp.dot(a, b, preferred_element_type=F32)


def _dot_nt(a, b):
    return lax.dot_general(a, b, (((1,), (1,)), ((), ())), preferred_element_type=F32)


def _dot_tn(a, b):
    return lax.dot_general(a, b, (((0,), (0,)), ((), ())), preferred_element_type=F32)


def _sds(shape, dtype):
    return jax.ShapeDtypeStruct(shape, dtype)


def _rms(x):
    return lax.rsqrt(jnp.mean(x * x, axis=-1, keepdims=True) + EPS)


def _rms_bwd(dy, x, g):
    r = _rms(x)
    xh = x * r
    dxh = dy * g
    dx = r * (dxh - xh * jnp.mean(dxh * xh, axis=-1, keepdims=True))
    return dx, xh


def _silu_grad(z):
    s = jax.nn.sigmoid(z)
    return s * (1.0 + z * (1.0 - s))


def _ffn_fwd(x, g, wg, wu, wd):
    S, D = x.shape
    nc, Fs, _ = wd.shape
    tm = TOK_TILE

    def body(x_ref, g_ref, wg_ref, wu_ref, wd_ref, out_ref, xn_ref, G_ref, U_ref, acc_ref):
        j = pl.program_id(1)

        @pl.when(j == 0)
        def _():
            xv = x_ref[...]
            xn_ref[...] = (xv * _rms(xv) * g_ref[...]).astype(BF16)
            acc_ref[...] = jnp.zeros_like(acc_ref)

        xn = xn_ref[...]
        G = _dot_nt(xn, wg_ref[0])
        U = _dot_nt(xn, wu_ref[0])
        G_ref[0] = G.astype(BF16)
        U_ref[0] = U.astype(BF16)
        H = (G * jax.nn.sigmoid(G) * U).astype(BF16)
        acc_ref[...] += _dot(H, wd_ref[0])

        @pl.when(j == nc - 1)
        def _():
            out_ref[...] = x_ref[...] + FFN_RES * acc_ref[...]

    row = pl.BlockSpec((tm, D), lambda i, j: (i, 0))
    return _pallas(
        body, name="ffn_fwd", grid=(S // tm, nc),
        in_specs=[row, pl.BlockSpec((1, D), lambda i, j: (0, 0)),
                  pl.BlockSpec((1, Fs, D), lambda i, j: (j, 0, 0)), pl.BlockSpec((1, Fs, D), lambda i, j: (j, 0, 0)),
                  pl.BlockSpec((1, Fs, D), lambda i, j: (j, 0, 0))],
        out_specs=[row, row, pl.BlockSpec((1, tm, Fs), lambda i, j: (j, i, 0)),
                   pl.BlockSpec((1, tm, Fs), lambda i, j: (j, i, 0))],
        out_shape=[_sds((S, D), F32), _sds((S, D), BF16), _sds((nc, S, Fs), BF16), _sds((nc, S, Fs), BF16)],
        scratch_shapes=[pltpu.VMEM((tm, D), F32)],
        compiler_params=_cp(("parallel", "arbitrary")),
    )(x, g, wg, wu, wd)


def _ffn_bwd_w(dout, xn, G, U, wd):
    S, D = dout.shape
    nc, _, Fs = G.shape
    tm = min(DW_TILE, S)
    nt = S // tm
    sub = min(TOK_TILE, tm)

    def body(do_ref, xn_ref, G_ref, U_ref, wd_ref, dwg_ref, dwu_ref, dwd_ref, dG_ref, dU_ref, ag, au, ad, do_s, H_s):
        i = pl.program_id(1)

        @pl.when(i == 0)
        def _():
            ag[...] = jnp.zeros_like(ag)
            au[...] = jnp.zeros_like(au)
            ad[...] = jnp.zeros_like(ad)

        for r in range(0, tm, sub):
            rows = pl.ds(r, sub)
            do = (FFN_RES * do_ref[rows, :]).astype(BF16)
            do_s[rows, :] = do
            Gv = G_ref[0, rows, :].astype(F32)
            Uv = U_ref[0, rows, :].astype(F32)
            dH = _dot_nt(do, wd_ref[0])
            sg = jax.nn.sigmoid(Gv)
            act = Gv * sg
            H_s[rows, :] = (act * Uv).astype(BF16)
            dU_ref[0, rows, :] = (dH * act).astype(BF16)
            dG_ref[0, rows, :] = (dH * Uv * (sg * (1.0 + Gv * (1.0 - sg)))).astype(BF16)
        xnv = xn_ref[...]
        ag[...] += _dot_tn(dG_ref[0], xnv)
        au[...] += _dot_tn(dU_ref[0], xnv)
        ad[...] += _dot_tn(H_s[...], do_s[...])

        @pl.when(i == nt - 1)
        def _():
            dwg_ref[0] = ag[...].astype(BF16)
            dwu_ref[0] = au[...].astype(BF16)
            dwd_ref[0] = ad[...].astype(BF16)

    row = pl.BlockSpec((tm, D), lambda j, i: (i, 0))
    hid = pl.BlockSpec((1, tm, Fs), lambda j, i: (j, i, 0))
    wrow = pl.BlockSpec((1, Fs, D), lambda j, i: (j, 0, 0))
    return _pallas(
        body, name="ffn_bwd_w", grid=(nc, nt),
        in_specs=[row, row, hid, hid, wrow],
        out_specs=[wrow, wrow, wrow, hid, hid],
        out_shape=[_sds((nc, Fs, D), BF16)] * 3 + [_sds((nc, S, Fs), BF16)] * 2,
        scratch_shapes=[pltpu.VMEM((Fs, D), F32)] * 3 + [pltpu.VMEM((tm, D), BF16), pltpu.VMEM((tm, Fs), BF16)],
        compiler_params=_cp(("parallel", "arbitrary"), 56),
    )(dout, xn, G, U, wd)


def _norm_in_bwd(dzs, ws, x, g, dres, w_rows=False):
    S, D = x.shape
    nc = dzs[0].shape[0]
    n = len(dzs)
    tm = TOK_TILE

    def body(*refs):
        dz_refs, w_refs = refs[:n], refs[n:2 * n]
        x_ref, g_ref, dres_ref, dx_ref, dg_ref, acc_ref = refs[2 * n:]
        i, j = pl.program_id(0), pl.program_id(1)

        @pl.when(j == 0)
        def _():
            acc_ref[...] = jnp.zeros_like(acc_ref)

        @pl.when((i == 0) & (j == 0))
        def _():
            dg_ref[...] = jnp.zeros_like(dg_ref)

        for dz_ref, w_ref in zip(dz_refs, w_refs):
            acc_ref[...] += _dot(dz_ref[0], w_ref[0]) if w_rows else _dot_nt(dz_ref[0], w_ref[0])

        @pl.when(j == nc - 1)
        def _():
            dxn = acc_ref[...]
            dx, xh = _rms_bwd(dxn, x_ref[...], g_ref[...])
            dx_ref[...] = dx + dres_ref[...]
            dg_ref[...] += jnp.sum(dxn * xh, axis=0, keepdims=True)

    row = pl.BlockSpec((tm, D), lambda i, j: (i, 0))
    one = pl.BlockSpec((1, D), lambda i, j: (0, 0))
    in_specs = [pl.BlockSpec((1, tm, dz.shape[2]), lambda i, j: (j, i, 0)) for dz in dzs]
    in_specs += [pl.BlockSpec((1,) + w.shape[1:], lambda i, j: (j, 0, 0)) for w in ws]
    return _pallas(
        body, name="norm_in_bwd", grid=(S // tm, nc),
        in_specs=in_specs + [row, one, row], out_specs=[row, one],
        out_shape=[_sds((S, D), F32), _sds((1, D), F32)],
        scratch_shapes=[pltpu.VMEM((tm, D), F32)],
        compiler_params=_cp(("arbitrary", "arbitrary")),
    )(*dzs, *ws, x, g, dres)


def _norm_proj(x, g, w, w2=None):
    S, D = x.shape
    N = w.shape[1]
    tm = TOK_TILE

    def body(*refs):
        if w2 is None:
            x_ref, g_ref, w_ref, h_ref, z_ref = refs
        else:
            x_ref, g_ref, w_ref, w2_ref, h_ref, z_ref, z2_ref = refs
        xv = x_ref[...]
        h = (xv * _rms(xv) * g_ref[...]).astype(BF16)
        h_ref[...] = h
        z_ref[...] = _dot(h, w_ref[...]).astype(BF16)
        if w2 is not None:
            z2_ref[...] = _dot(h, w2_ref[...])

    row = pl.BlockSpec((tm, D), lambda i: (i, 0))
    in_specs = [row, pl.BlockSpec((1, D), lambda i: (0, 0)), pl.BlockSpec((D, N), lambda i: (0, 0))]
    out_specs = [row, pl.BlockSpec((tm, N), lambda i: (i, 0))]
    out_shape = [_sds((S, D), BF16), _sds((S, N), BF16)]
    args = [x, g, w]
    if w2 is not None:
        N2 = w2.shape[1]
        in_specs.append(pl.BlockSpec((D, N2), lambda i: (0, 0)))
        out_specs.append(pl.BlockSpec((tm, N2), lambda i: (i, 0)))
        out_shape.append(_sds((S, N2), F32))
        args.append(w2)
    return _pallas(body, name="norm_proj", grid=(S // tm,), in_specs=in_specs, out_specs=out_specs,
                   out_shape=out_shape, compiler_params=_cp(("parallel",)))(*args)


def _proj_res(acts, ws, res):
    S, D = res.shape
    n = len(acts)
    tm = TOK_TILE

    def body(*refs):
        a_refs, w_refs = refs[:n], refs[n:2 * n]
        res_ref, out_ref = refs[2 * n:]
        acc = res_ref[...]
        for a_ref, w_ref in zip(a_refs, w_refs):
            acc = acc + _dot(a_ref[...], w_ref[...])
        out_ref[...] = acc

    row = pl.BlockSpec((tm, D), lambda i: (i, 0))
    in_specs = [pl.BlockSpec((tm, a.shape[1]), lambda i: (i, 0)) for a in acts]
    in_specs += [pl.BlockSpec(w.shape, lambda i: (0, 0)) for w in ws]
    return _pallas(body, name="proj_res", grid=(S // tm,), in_specs=in_specs + [row], out_specs=row,
                   out_shape=_sds((S, D), F32), compiler_params=_cp(("parallel",)))(*acts, *ws, res)


def _matmul_nt(a, w, after=None):
    S, K = a.shape
    M = w.shape[0]
    tm = TOK_TILE

    def body(a_ref, w_ref, *rest):
        rest[-1][...] = _dot_nt(a_ref[...].astype(BF16), w_ref[...])

    extra = [] if after is None else [after]
    return _pallas(body, name="matmul_nt", grid=(S // tm,),
                   in_specs=[pl.BlockSpec((tm, K), lambda i: (i, 0)), pl.BlockSpec((M, K), lambda i: (0, 0))] + [ANY] * len(extra),
                   out_specs=pl.BlockSpec((tm, M), lambda i: (i, 0)), out_shape=_sds((S, M), F32),
                   compiler_params=_cp(("parallel",)))(a, w, *extra)


def _matmul_tn(a, b, tn):
    S, M = a.shape
    N = b.shape[1]
    tm = min(DW_TILE, S)
    nt = S // tm

    def body(a_ref, b_ref, o_ref, acc_ref):
        i = pl.program_id(1)

        @pl.when(i == 0)
        def _():
            acc_ref[...] = jnp.zeros_like(acc_ref)

        acc_ref[...] += _dot_tn(a_ref[...].astype(BF16), b_ref[...].astype(BF16))

        @pl.when(i == nt - 1)
        def _():
            o_ref[0] = acc_ref[...].astype(BF16)

    return _pallas(body, name="matmul_tn", grid=(N // tn, nt),
                   in_specs=[pl.BlockSpec((tm, M), lambda j, i: (i, 0)), pl.BlockSpec((tm, tn), lambda j, i: (i, j))],
                   out_specs=pl.BlockSpec((1, M, tn), lambda j, i: (j, 0, 0)), out_shape=_sds((N // tn, M, tn), BF16),
                   scratch_shapes=[pltpu.VMEM((M, tn), F32)],
                   compiler_params=_cp(("parallel", "arbitrary")))(a, b)


def _conv_a_fwd(z, cw, cb, cn):
    S = z.shape[0]
    C = D_CONV
    tm = TOK_TILE
    hb = tm // HALO_A

    def body(u_ref, gt_ref, up_ref, gp_ref, cw_ref, cb_ref, cn_ref, a_ref, a1_ref, win):
        i = pl.program_id(0)
        prev = up_ref[...].astype(F32) * jax.nn.sigmoid(gp_ref[...].astype(F32))
        win[pl.ds(0, HALO_A), :] = jnp.where(i == 0, 0.0, prev)
        win[pl.ds(HALO_A, tm), :] = u_ref[...].astype(F32) * jax.nn.sigmoid(gt_ref[...].astype(F32))
        acc = jnp.zeros((tm, C), F32)
        for k in range(CONV_A_WIDTH):
            acc = acc + cw_ref[k:k + 1, :] * win[pl.ds(HALO_A - (CONV_A_WIDTH - 1) + k, tm), :]
        a1 = acc + cb_ref[...]
        a1_ref[...] = a1
        a2 = a1 * _rms(a1) * cn_ref[...]
        a_ref[...] = (a2 * jax.nn.sigmoid(a2)).astype(BF16)

    cur = lambda c: pl.BlockSpec((tm, C), lambda i, c=c: (i, c))
    prv = lambda c: pl.BlockSpec((HALO_A, C), lambda i, c=c: (jnp.maximum(i * hb - 1, 0), c))
    vec = pl.BlockSpec((1, C), lambda i: (0, 0))
    return _pallas(body, name="conv_a_fwd", grid=(S // tm,),
                   in_specs=[cur(0), cur(1), prv(0), prv(1), pl.BlockSpec((32, C), lambda i: (0, 0)), vec, vec],
                   out_specs=[pl.BlockSpec((tm, C), lambda i: (i, 0)), pl.BlockSpec((tm, C), lambda i: (i, 0))],
                   out_shape=[_sds((S, C), BF16), _sds((S, C), F32)],
                   scratch_shapes=[pltpu.VMEM((tm + HALO_A, C), F32)],
                   compiler_params=_cp(("parallel",)))(z, z, z, z, cw, cb, cn)


def _conv_a_bwd(da, a1, z, cw, cn):
    S = z.shape[0]
    C = D_CONV
    tm = TOK_TILE
    hb = tm // HALO_A
    nt = S // tm
    W = CONV_A_WIDTH

    def body(da_ref, a1_ref, dan_ref, a1n_ref, u_ref, gt_ref, up_ref, gp_ref, cw_ref, cn_ref,
             duz_ref, dcw_ref, dcb_ref, dcn_ref, win, dwin):
        i = pl.program_id(0)

        @pl.when(i == 0)
        def _():
            dcw_ref[...] = jnp.zeros_like(dcw_ref)
            dcb_ref[...] = jnp.zeros_like(dcb_ref)
            dcn_ref[...] = jnp.zeros_like(dcn_ref)

        cnv = cn_ref[...]

        def da1_of(dav, a1v):
            a2 = a1v * _rms(a1v) * cnv
            da2 = dav * _silu_grad(a2)
            dx, xh = _rms_bwd(da2, a1v, cnv)
            return dx, da2 * xh

        da1, dcn_t = da1_of(da_ref[...], a1_ref[...])
        da1n, _ = da1_of(dan_ref[...], a1n_ref[...])
        dwin[pl.ds(0, tm), :] = da1
        dwin[pl.ds(tm, HALO_A), :] = jnp.where(i == nt - 1, 0.0, da1n)
        dcb_ref[...] += jnp.sum(da1, axis=0, keepdims=True)
        dcn_ref[...] += jnp.sum(dcn_t, axis=0, keepdims=True)

        u = u_ref[...].astype(F32)
        sg = jax.nn.sigmoid(gt_ref[...].astype(F32))
        prev = up_ref[...].astype(F32) * jax.nn.sigmoid(gp_ref[...].astype(F32))
        win[pl.ds(0, HALO_A), :] = jnp.where(i == 0, 0.0, prev)
        win[pl.ds(HALO_A, tm), :] = u * sg

        da0 = jnp.zeros((tm, C), F32)
        for k in range(W):
            da0 = da0 + cw_ref[k:k + 1, :] * dwin[pl.ds(W - 1 - k, tm), :]
            dcw_ref[k:k + 1, :] += jnp.sum(da1 * win[pl.ds(HALO_A - (W - 1) + k, tm), :], axis=0, keepdims=True)
        duz_ref[:, 0:C] = (da0 * sg).astype(BF16)
        duz_ref[:, C:2 * C] = (da0 * u * sg * (1.0 - sg)).astype(BF16)

    cur = lambda c: pl.BlockSpec((tm, C), lambda i, c=c: (i, c))
    prv = lambda c: pl.BlockSpec((HALO_A, C), lambda i, c=c: (jnp.maximum(i * hb - 1, 0), c))
    nxt = pl.BlockSpec((HALO_A, C), lambda i: (jnp.minimum((i + 1) * hb, S // HALO_A - 1), 0))
    vec = pl.BlockSpec((1, C), lambda i: (0, 0))
    return _pallas(body, name="conv_a_bwd", grid=(nt,),
                   in_specs=[cur(0), cur(0), nxt, nxt, cur(0), cur(1), prv(0), prv(1),
                             pl.BlockSpec((32, C), lambda i: (0, 0)), vec],
                   out_specs=[pl.BlockSpec((tm, 2 * C), lambda i: (i, 0)), pl.BlockSpec((32, C), lambda i: (0, 0)), vec, vec],
                   out_shape=[_sds((S, 2 * C), BF16), _sds((32, C), F32), _sds((1, C), F32), _sds((1, C), F32)],
                   scratch_shapes=[pltpu.VMEM((tm + HALO_A, C), F32), pltpu.VMEM((tm + HALO_A, C), F32)],
                   compiler_params=_cp(("arbitrary",)))(da, a1, da, a1, z, z, z, z, cw, cn)


def _forget_scan(fl, bf):
    S, L = fl.shape
    B = SCAN_BLK

    def body(fl_ref, bf_ref, flb_ref, F_ref):
        tri = (lax.broadcasted_iota(jnp.int32, (B, B), 0) >= lax.broadcasted_iota(jnp.int32, (B, B), 1)).astype(F32)

        def step(c, carry):
            rows = pl.ds(pl.multiple_of(c * B, B), B)
            v = fl_ref[rows, :] + bf_ref[...]
            flb_ref[rows, :] = v
            lf = jnp.minimum(v, 0.0) - jnp.log1p(jnp.exp(-jnp.abs(v)))
            cs = jnp.dot(tri, lf, precision=lax.Precision.HIGHEST, preferred_element_type=F32) + carry
            F_ref[rows, :] = cs
            return cs[B - 1:B, :]

        lax.fori_loop(0, S // B, step, jnp.zeros((1, L), F32))

    return _pallas(body, name="forget_scan", out_shape=[_sds((S, L), F32), _sds((S, L), F32)],
                   compiler_params=_cp())(fl, bf)


def _forget_scan_bwd(dF, flb):
    S, L = dF.shape
    B = SCAN_BLK
    nb = S // B

    def body(dF_ref, flb_ref, dfl_ref, db_ref):
        tri = (lax.broadcasted_iota(jnp.int32, (B, B), 0) <= lax.broadcasted_iota(jnp.int32, (B, B), 1)).astype(F32)

        def step(t, carry):
            carry_cs, db = carry
            rows = pl.ds(pl.multiple_of((nb - 1 - t) * B, B), B)
            cs = jnp.dot(tri, dF_ref[rows, :], precision=lax.Precision.HIGHEST, preferred_element_type=F32) + carry_cs
            dfl = cs * jax.nn.sigmoid(-flb_ref[rows, :])
            dfl_ref[rows, :] = dfl
            return cs[0:1, :], db + jnp.sum(dfl, axis=0, keepdims=True)

        _, db = lax.fori_loop(0, nb, step, (jnp.zeros((1, L), F32), jnp.zeros((1, L), F32)))
        db_ref[...] = db

    return _pallas(body, name="forget_scan_bwd", out_shape=[_sds((S, L), F32), _sds((1, L), F32)],
                   compiler_params=_cp())(dF, flb)


NEG = -1e30


def _causal_mask(t):
    return lax.broadcasted_iota(jnp.int32, (t, t), 0) >= lax.broadcasted_iota(jnp.int32, (t, t), 1)


AUG = 128
C_F, C_ONE, C_LSE = 64, 67, 70


def _split3(f):
    a = f.astype(BF16).astype(F32)
    r = f - a
    b = r.astype(BF16).astype(F32)
    return a, b, r - b


def _put3(lane, base, parts, other):
    out = other
    for k, p in enumerate(parts):
        out = jnp.where(lane == base + k, p, out)
    return out


def _ones3(lane, base):
    return (lane >= base) & (lane < base + 3)


def _lane_ids(rows):
    return lax.broadcasted_iota(jnp.int32, (rows, AUG), 1)


def _pair_rms(x, lo):
    sq = x * x
    ms_a = jnp.sum(jnp.where(lo, sq, 0.0), axis=-1, keepdims=True) * (1.0 / HEAD_DIM)
    ms_b = jnp.sum(jnp.where(lo, 0.0, sq), axis=-1, keepdims=True) * (1.0 / HEAD_DIM)
    return jnp.where(lo, lax.rsqrt(ms_a + EPS), lax.rsqrt(ms_b + EPS))


def _qkv_prep(z, Fc, qw, kw):
    S = z.shape[0]
    tp = min(QKN_TILE, S)
    scale = 1.0 / math.sqrt(HEAD_DIM)

    def body(zq_ref, zk_ref, zv_ref, F_ref, qw_ref, kw_ref, q_ref, k_ref, v_ref):
        j = pl.program_id(0)
        lane = _lane_ids(tp)
        lo = lane < HEAD_DIM
        Fv = F_ref[...]
        xq = zq_ref[...].astype(F32)
        xk = zk_ref[...].astype(F32)
        qn = xq * _pair_rms(xq, lo) * qw_ref[...] * scale
        kn = xk * _pair_rms(xk, lo) * kw_ref[...]
        vv = zv_ref[...].astype(F32)
        for half in range(2):
            take = (lambda a: a) if half == 0 else (lambda a: pltpu.roll(a, HEAD_DIM, 1))
            fp = _split3(jnp.sum(jnp.where(lane == 2 * j + half, Fv, 0.0), axis=-1, keepdims=True))
            qx = _put3(lane, C_F, fp, jnp.where(_ones3(lane, C_ONE), 1.0, 0.0))
            kx = _put3(lane, C_ONE, [-p for p in fp], jnp.where(_ones3(lane, C_F) | _ones3(lane, C_LSE), 1.0, 0.0))
            vx = jnp.where(_ones3(lane, C_F), 1.0, 0.0)
            q_ref[half] = jnp.where(lo, take(qn), qx).astype(BF16)
            k_ref[half] = jnp.where(lo, take(kn), kx).astype(BF16)
            v_ref[half] = jnp.where(lo, take(vv), vx).astype(BF16)

    col = lambda c0: pl.BlockSpec((tp, AUG), lambda j, i, c0=c0: (i, c0 + j))
    vec = pl.BlockSpec((1, AUG), lambda j, i: (0, 0))
    out = pl.BlockSpec((2, tp, AUG), lambda j, i: (j, i, 0))
    return _pallas(body, name="qkv_prep", grid=(N_HEADS // 2, S // tp),
                   in_specs=[col(8), col(12), col(16), pl.BlockSpec((tp, AUG), lambda j, i: (i, 0)), vec, vec],
                   out_specs=[out, out, out], out_shape=[_sds((N_HEADS, S, AUG), BF16)] * 3,
                   compiler_params=_cp(("parallel", "parallel")))(z, z, z, Fc, qw, kw)


def _fox_fwd(q_aug, k_aug, v_aug):
    H, S, A = q_aug.shape
    t = ATT_TILE
    nq = S // t

    def body(q_ref, k_ref, v_ref, o_ref, q2_ref):
        i = pl.program_id(1)
        q = q_ref[0]

        def tile(j, carry, diag):
            m, acc = carry
            rows = pl.ds(pl.multiple_of(j * t, t), t)
            s = _dot_nt(q, k_ref[0, rows, :])
            if diag:
                s = jnp.where(_causal_mask(t), s, NEG)
            m_new = jnp.maximum(m, jnp.max(s, axis=-1, keepdims=True))
            p = jnp.exp(s - m_new)
            acc = jnp.exp(m - m_new) * acc + _dot(p.astype(BF16), v_ref[0, rows, :])
            return m_new, acc

        init = (jnp.full((t, 1), NEG, F32), jnp.zeros((t, A), F32))
        carry = lax.fori_loop(0, i, lambda j, c: tile(j, c, False), init)
        m, acc = tile(i, carry, True)
        lane = _lane_ids(t)
        l = jnp.sum(jnp.where(lane == C_F, acc, 0.0), axis=-1, keepdims=True)
        o_ref[0] = (acc / l).astype(BF16)
        lse = m + jnp.log(l)
        q2_ref[0] = (q.astype(F32) + _put3(lane, C_LSE, [-p for p in _split3(lse)], 0.0)).astype(BF16)

    qblk = pl.BlockSpec((1, t, A), lambda h, i: (h, i, 0))
    full = pl.BlockSpec((1, S, A), lambda h, i: (h, 0, 0))
    return _pallas(body, name="fox_fwd", grid=(H, nq), in_specs=[qblk, full, full], out_specs=[qblk, qblk],
                   out_shape=[_sds((H, S, A), BF16)] * 2, compiler_params=_cp(("parallel", "parallel")))(q_aug, k_aug, v_aug)


def _do_prep(dcat, o_aug):
    S = dcat.shape[0]
    tp = min(QKN_TILE, S)

    def body(d_ref, o_ref, out_ref):
        lane = _lane_ids(tp)
        lo = lane < HEAD_DIM
        x = d_ref[...]
        for half in range(2):
            d = jnp.where(lo, x if half == 0 else pltpu.roll(x, HEAD_DIM, 1), 0.0)
            delta = jnp.sum(d * o_ref[half].astype(F32), axis=-1, keepdims=True)
            out_ref[half] = jnp.where(lo, d, _put3(lane, C_F, [-p for p in _split3(delta)], 0.0)).astype(BF16)

    pair = pl.BlockSpec((2, tp, AUG), lambda j, i: (j, i, 0))
    return _pallas(body, name="do_prep", grid=(N_HEADS // 2, S // tp),
                   in_specs=[pl.BlockSpec((tp, AUG), lambda j, i: (i, D_CONV // AUG + j)), pair], out_specs=pair,
                   out_shape=_sds((N_HEADS, S, AUG), BF16), compiler_params=_cp(("parallel", "parallel")))(dcat, o_aug)


def _fox_bwd(q2, k_aug, v_aug, do_aug):
    H, S, A = q2.shape
    t = ATT_TILE
    nq = S // t

    def body(q_ref, k_ref, v_ref, do_ref, dq_ref, dk_ref, dv_ref):
        j = pl.program_id(1)

        @pl.when(j == 0)
        def _():
            dq_ref[...] = jnp.zeros_like(dq_ref)

        k = k_ref[0]
        vv = v_ref[0]

        def tile(i, carry, diag):
            dk, dv = carry
            rows = pl.ds(pl.multiple_of(i * t, t), t)
            q = q_ref[0, rows, :]
            dov = do_ref[0, rows, :]
            s = _dot_nt(q, k)
            if diag:
                s = jnp.where(_causal_mask(t), s, NEG)
            p = jnp.exp(s)
            dv = dv + _dot_tn(p.astype(BF16), dov)
            dsb = (p * _dot_nt(dov, vv)).astype(BF16)
            dq_ref[0, rows, :] += _dot(dsb, k)
            dk = dk + _dot_tn(dsb, q)
            return dk, dv

        init = (jnp.zeros((t, A), F32), jnp.zeros((t, A), F32))
        carry = tile(j, init, True)
        dk, dv = lax.fori_loop(j + 1, nq, lambda i, c: tile(i, c, False), carry)
        dk_ref[0] = dk
        dv_ref[0] = dv

    full = pl.BlockSpec((1, S, A), lambda h, j: (h, 0, 0))
    kblk = pl.BlockSpec((1, t, A), lambda h, j: (h, j, 0))
    return _pallas(body, name="fox_bwd", grid=(H, nq), in_specs=[full, kblk, kblk, full], out_specs=[full, kblk, kblk],
                   out_shape=[_sds((H, S, A), F32)] * 3,
                   compiler_params=_cp(("parallel", "arbitrary")))(q2, k_aug, v_aug, do_aug)


def _qkv_bwd(dq, dk, dv, z, qw, kw):
    S = z.shape[0]
    tp = min(QKN_TILE, S)
    scale = 1.0 / math.sqrt(HEAD_DIM)

    def body(dq_ref, dk_ref, dv_ref, zq_ref, zk_ref, qw_ref, kw_ref, dqf_ref, dkf_ref, dvf_ref, dF_ref, dqw_ref, dkw_ref):
        i, j = pl.program_id(0), pl.program_id(1)
        lane = _lane_ids(tp)
        lo = lane < HEAD_DIM

        @pl.when((i == 0) & (j == 0))
        def _():
            dqw_ref[...] = jnp.zeros_like(dqw_ref)
            dkw_ref[...] = jnp.zeros_like(dkw_ref)

        def pair(ref):
            return jnp.where(lo, ref[0], pltpu.roll(ref[1], HEAD_DIM, 1))

        def norm_bwd(g, x, w):
            r = _pair_rms(x, lo)
            xh = x * r
            dxh = g * w
            tt = dxh * xh
            mean_a = jnp.sum(jnp.where(lo, tt, 0.0), axis=-1, keepdims=True) * (1.0 / HEAD_DIM)
            mean_b = jnp.sum(jnp.where(lo, 0.0, tt), axis=-1, keepdims=True) * (1.0 / HEAD_DIM)
            return r * (dxh - xh * jnp.where(lo, mean_a, mean_b)), g * xh

        dxq, gq = norm_bwd(pair(dq_ref) * scale, zq_ref[...].astype(F32), qw_ref[...])
        dqf_ref[...] = dxq.astype(BF16)
        dqw_ref[...] += jnp.sum(gq, axis=0, keepdims=True)
        dxk, gk = norm_bwd(pair(dk_ref), zk_ref[...].astype(F32), kw_ref[...])
        dkf_ref[...] = dxk.astype(BF16)
        dkw_ref[...] += jnp.sum(gk, axis=0, keepdims=True)
        dvf_ref[...] = pair(dv_ref).astype(BF16)

        contrib = jnp.zeros((tp, AUG), F32)
        for half in range(2):
            df = (jnp.sum(jnp.where(lane == C_F, dq_ref[half], 0.0), axis=-1, keepdims=True)
                  - jnp.sum(jnp.where(lane == C_ONE, dk_ref[half], 0.0), axis=-1, keepdims=True))
            contrib = jnp.where(lane == 2 * j + half, df, contrib)

        @pl.when(j == 0)
        def _():
            dF_ref[...] = contrib

        @pl.when(j > 0)
        def _():
            dF_ref[...] += contrib

    pairb = pl.BlockSpec((2, tp, AUG), lambda i, j: (j, i, 0))
    col = lambda c0: pl.BlockSpec((tp, AUG), lambda i, j, c0=c0: (i, c0 + j))
    vec = pl.BlockSpec((1, AUG), lambda i, j: (0, 0))
    flat = pl.BlockSpec((tp, AUG), lambda i, j: (i, j))
    return _pallas(body, name="qkv_bwd", grid=(S // tp, N_HEADS // 2),
                   in_specs=[pairb, pairb, pairb, col(8), col(12), vec, vec],
                   out_specs=[flat, flat, flat, pl.BlockSpec((tp, AUG), lambda i, j: (i, 0)), vec, vec],
                   out_shape=[_sds((S, D_ATTN), BF16)] * 3 + [_sds((S, AUG), F32), _sds((1, AUG), F32), _sds((1, AUG), F32)],
                   compiler_params=_cp(("arbitrary", "arbitrary")))(dq, dk, dv, z, z, qw, kw)


def _proj_res_heads(a, wa, o_aug, wo, res):
    S, D = res.shape
    H = o_aug.shape[0]
    tm = TOK_TILE

    def body(a_ref, wa_ref, o_ref, wo_ref, res_ref, out_ref):
        acc = res_ref[...] + _dot(a_ref[...], wa_ref[...])
        for h in range(H):
            acc = acc + _dot(o_ref[h], wo_ref[h])
        out_ref[...] = acc

    row = pl.BlockSpec((tm, D), lambda i: (i, 0))
    return _pallas(body, name="proj_res_heads", grid=(S // tm,),
                   in_specs=[pl.BlockSpec((tm, a.shape[1]), lambda i: (i, 0)), pl.BlockSpec(wa.shape, lambda i: (0, 0)),
                             pl.BlockSpec((H, tm, AUG), lambda i: (0, i, 0)), pl.BlockSpec(wo.shape, lambda i: (0, 0, 0)), row],
                   out_specs=row, out_shape=_sds((S, D), F32), compiler_params=_cp(("parallel",)))(a, wa, o_aug, wo, res)


def _heads_tn(o_aug, d):
    H, S, A = o_aug.shape
    D = d.shape[1]
    tm = min(DW_TILE, S)
    nt = S // tm

    def body(o_ref, d_ref, out_ref, acc_ref):
        i = pl.program_id(0)

        @pl.when(i == 0)
        def _():
            acc_ref[...] = jnp.zeros_like(acc_ref)

        dv = d_ref[...].astype(BF16)
        for h in range(H):
            acc_ref[h] += _dot_tn(o_ref[h], dv)

        @pl.when(i == nt - 1)
        def _():
            out_ref[...] = acc_ref[...].astype(BF16)

    return _pallas(body, name="heads_tn", grid=(nt,),
                   in_specs=[pl.BlockSpec((H, tm, A), lambda i: (0, i, 0)), pl.BlockSpec((tm, D), lambda i: (i, 0))],
                   out_specs=pl.BlockSpec((H, A, D), lambda i: (0, 0, 0)), out_shape=_sds((H, A, D), BF16),
                   scratch_shapes=[pltpu.VMEM((H, A, D), F32)], compiler_params=_cp(("arbitrary",)))(o_aug, d)


def _odd_mid_fwd(z, cw):
    S = z.shape[0]
    D = z.shape[1] // 3
    tm = TOK_TILE
    hb = tm // HALO_C
    W = CONV_C_WIDTH

    def body(gb_ref, gc_ref, hh_ref, gcp_ref, hhp_ref, cw_ref, y_ref, win):
        i = pl.program_id(0)
        prev = gcp_ref[...].astype(F32) * hhp_ref[...].astype(F32)
        win[pl.ds(0, HALO_C), :] = jnp.where(i == 0, 0.0, prev)
        win[pl.ds(HALO_C, tm), :] = gc_ref[...].astype(F32) * hh_ref[...].astype(F32)
        c1 = jnp.zeros((tm, D), F32)
        for k in range(W):
            c1 = c1 + cw_ref[k:k + 1, :] * win[pl.ds(HALO_C - (W - 1) + k, tm), :]
        y_ref[...] = (gb_ref[...].astype(F32) * c1).astype(BF16)

    cur = lambda c: pl.BlockSpec((tm, D), lambda i, c=c: (i, c))
    prv = lambda c: pl.BlockSpec((HALO_C, D), lambda i, c=c: (jnp.maximum(i * hb - 1, 0), c))
    return _pallas(body, name="odd_mid_fwd", grid=(S // tm,),
                   in_specs=[cur(0), cur(1), cur(2), prv(1), prv(2), pl.BlockSpec((8, D), lambda i: (0, 0))],
                   out_specs=pl.BlockSpec((tm, D), lambda i: (i, 0)), out_shape=_sds((S, D), BF16),
                   scratch_shapes=[pltpu.VMEM((tm + HALO_C, D), F32)],
                   compiler_params=_cp(("parallel",)))(z, z, z, z, z, cw)


def _odd_mid_bwd(dy, z, cw):
    S = z.shape[0]
    D = z.shape[1] // 3
    tm = TOK_TILE
    hb = tm // HALO_C
    nt = S // tm
    W = CONV_C_WIDTH

    def body(dy_ref, dyn_ref, gb_ref, gbn_ref, gc_ref, hh_ref, gcp_ref, hhp_ref, cw_ref, dz_ref, dcw_ref, win, dwin):
        i = pl.program_id(0)

        @pl.when(i == 0)
        def _():
            dcw_ref[...] = jnp.zeros_like(dcw_ref)

        gc = gc_ref[...].astype(F32)
        hh = hh_ref[...].astype(F32)
        prev = gcp_ref[...].astype(F32) * hhp_ref[...].astype(F32)
        win[pl.ds(0, HALO_C), :] = jnp.where(i == 0, 0.0, prev)
        win[pl.ds(HALO_C, tm), :] = gc * hh
        dyv = dy_ref[...]
        dc1 = dyv * gb_ref[...].astype(F32)
        dwin[pl.ds(0, tm), :] = dc1
        dwin[pl.ds(tm, HALO_C), :] = jnp.where(i == nt - 1, 0.0, dyn_ref[...] * gbn_ref[...].astype(F32))
        c1 = jnp.zeros((tm, D), F32)
        dc0 = jnp.zeros((tm, D), F32)
        for k in range(W):
            tap = win[pl.ds(HALO_C - (W - 1) + k, tm), :]
            c1 = c1 + cw_ref[k:k + 1, :] * tap
            dc0 = dc0 + cw_ref[k:k + 1, :] * dwin[pl.ds(W - 1 - k, tm), :]
            dcw_ref[k:k + 1, :] += jnp.sum(dc1 * tap, axis=0, keepdims=True)
        dz_ref[:, 0:D] = (dyv * c1).astype(BF16)
        dz_ref[:, D:2 * D] = (dc0 * hh).astype(BF16)
        dz_ref[:, 2 * D:3 * D] = (dc0 * gc).astype(BF16)

    cur = lambda c: pl.BlockSpec((tm, D), lambda i, c=c: (i, c))
    prv = lambda c: pl.BlockSpec((HALO_C, D), lambda i, c=c: (jnp.maximum(i * hb - 1, 0), c))
    nxt = pl.BlockSpec((HALO_C, D), lambda i: (jnp.minimum((i + 1) * hb, S // HALO_C - 1), 0))
    return _pallas(body, name="odd_mid_bwd", grid=(nt,),
                   in_specs=[cur(0), nxt, cur(0), nxt, cur(1), cur(2), prv(1), prv(2), pl.BlockSpec((8, D), lambda i: (0, 0))],
                   out_specs=[pl.BlockSpec((tm, 3 * D), lambda i: (i, 0)), pl.BlockSpec((8, D), lambda i: (0, 0))],
                   out_shape=[_sds((S, 3 * D), BF16), _sds((8, D), F32)],
                   scratch_shapes=[pltpu.VMEM((tm + HALO_C, D), F32), pltpu.VMEM((tm + HALO_C, D), F32)],
                   compiler_params=_cp(("arbitrary",)))(dy, dy, z, z, z, z, z, z, cw)


def _loss_head(y, tgt):
    S, D = y.shape
    tm = TOK_TILE

    def body(y_ref, t_ref, dy_ref, l_ref):
        @pl.when(pl.program_id(0) == 0)
        def _():
            l_ref[...] = jnp.zeros_like(l_ref)

        e = y_ref[...] - t_ref[...]
        dy_ref[...] = e * (1.0 / D)
        l_ref[...] += jnp.sum(jnp.sum(e * e, axis=-1, keepdims=True), axis=0, keepdims=True) * (0.5 / D)

    row = pl.BlockSpec((tm, D), lambda i: (i, 0))
    return _pallas(body, name="loss_head", grid=(S // tm,), in_specs=[row, row],
                   out_specs=[row, pl.BlockSpec((1, 1), lambda i: (0, 0))],
                   out_shape=[_sds((S, D), F32), _sds((1, 1), F32)],
                   compiler_params=_cp(("arbitrary",)))(y, tgt)


def _pad_rows(a, rows):
    return jnp.pad(a, ((0, rows - a.shape[0]), (0, 0)))


def _local_step(x, tgt, W, need=lambda block, after: None, done=lambda block, block_grads: None):
    S, D = x.shape
    grads = {}
    saved = {}

    def gain_after(gain, token):
        return gain if token is None else gain + token

    def ffn_f(tag, l, xin):
        need((tag, l), xin)
        out, xn, G, U = _ffn_fwd(xin, W[tag + "_norm"][l:l + 1], W[tag + "_w_gate"][l], W[tag + "_w_up"][l],
                                 W[tag + "_w_down"][l])
        saved[(tag, l)] = (xin, xn, G, U)
        return out

    def ffn_b(tag, l, dout):
        xin, xn, G, U = saved[(tag, l)]
        keys = [(tag + "_w_gate", l), (tag + "_w_up", l), (tag + "_w_down", l)]
        *dws, dG, dU = _ffn_bwd_w(dout, xn, G, U, W[tag + "_w_down"][l])
        big = dict(zip(keys, dws))
        grads.update(big)
        token = done((tag, l), big)
        dx, dg = _norm_in_bwd([dG, dU], [W[tag + "_w_gate"][l], W[tag + "_w_up"][l]], xin,
                              gain_after(W[tag + "_norm"][l:l + 1], token), dout, w_rows=True)
        grads[(tag + "_norm", l)] = dg
        return dx

    x0a = ffn_f("ffn1", 0, x)
    need(("ev", 0), x0a)
    w_in = W["ev_w_in"]
    w_main, w_f = w_in[:, :2560], jnp.pad(w_in[:, 2560:], ((0, 0), (0, 120)))
    h0, z0, fl = _norm_proj(x0a, W["mix_norm"][0:1], w_main, w_f)
    cw_a = _pad_rows(W["ev_conv_w"], 32)
    a_act, a1 = _conv_a_fwd(z0, cw_a, W["ev_conv_b"], W["ev_conv_norm"])
    flb, Fc = _forget_scan(fl, jnp.pad(W["ev_b_f"], ((0, 0), (0, 120))))
    qw2, kw2 = jnp.tile(W["ev_q_norm"], (1, 2)), jnp.tile(W["ev_k_norm"], (1, 2))
    q_aug, k_aug, v_aug = _qkv_prep(z0, Fc, qw2, kw2)
    o_aug, q_lse = _fox_fwd(q_aug, k_aug, v_aug)
    w_out_e = W["ev_w_out"]
    w_out_o = jnp.pad(w_out_e[D_CONV:].reshape(N_HEADS, HEAD_DIM, D), ((0, 0), (0, AUG - HEAD_DIM), (0, 0)))
    x0b = _proj_res_heads(a_act, w_out_e[:D_CONV], o_aug, w_out_o, x0a)
    x0c = ffn_f("ffn2", 0, x0b)
    x1a = ffn_f("ffn1", 1, x0c)
    need(("od", 0), x1a)
    h1, z1 = _norm_proj(x1a, W["mix_norm"][1:2], W["od_w_in"])
    cw_c = _pad_rows(W["od_conv_w"], 8)
    y1 = _odd_mid_fwd(z1, cw_c)
    x1b = _proj_res([y1], [W["od_w_out"]], x1a)
    x1c = ffn_f("ffn2", 1, x1b)
    dy, loss = _loss_head(x1c, tgt)

    d = ffn_b("ffn2", 1, dy)
    dy1 = _matmul_nt(d, W["od_w_out"])
    grads[("od_w_out", 0)] = _matmul_tn(y1, d, D)[0]
    dz1, dcw_c = _odd_mid_bwd(dy1, z1, cw_c)
    grads[("od_conv_w", 0)] = dcw_c[:CONV_C_WIDTH]
    grads[("od_w_in", 0)] = _matmul_tn(h1, dz1, 3 * D // 4)
    token = done(("od", 0), {k: grads[k] for k in (("od_w_out", 0), ("od_w_in", 0))})
    d, dg = _norm_in_bwd([dz1[None]], [W["od_w_in"][None]], x1a, gain_after(W["mix_norm"][1:2], token), d)
    grads[("mix_norm", 1)] = dg
    d = ffn_b("ffn1", 1, d)
    d = ffn_b("ffn2", 0, d)
    dcat = _matmul_nt(d, w_out_e)
    grads[("ev_w_out", 0)] = jnp.concatenate([_matmul_tn(a_act, d, D)[0],
                                              _heads_tn(o_aug, d)[:, :HEAD_DIM].reshape(D_ATTN, D)], axis=0)
    duz, dcw_a, dcb, dcn = _conv_a_bwd(dcat, a1, z0, cw_a, W["ev_conv_norm"])
    grads[("ev_conv_w", 0)] = dcw_a[:CONV_A_WIDTH]
    grads[("ev_conv_b", 0)] = dcb
    grads[("ev_conv_norm", 0)] = dcn
    dq_a, dk_a, dv_a = _fox_bwd(q_lse, k_aug, v_aug, _do_prep(dcat, o_aug))
    dqf, dkf, dvf, dF, dqw, dkw = _qkv_bwd(dq_a, dk_a, dv_a, z0, qw2, kw2)
    grads[("ev_q_norm", 0)] = dqw[:, :HEAD_DIM] + dqw[:, HEAD_DIM:]
    grads[("ev_k_norm", 0)] = dkw[:, :HEAD_DIM] + dkw[:, HEAD_DIM:]
    dfl, dbf = _forget_scan_bwd(dF, flb)
    grads[("ev_b_f", 0)] = dbf[:, :N_HEADS]
    dz0 = jnp.concatenate([duz, dqf, dkf, dvf], axis=1)
    dflb = dfl.astype(BF16)
    gmain = _matmul_tn(h0, dz0, 640)
    gmain = gmain.transpose(1, 0, 2).reshape(D, 2560)
    gf = _matmul_tn(h0, dflb, 128)[0][:, :N_HEADS]
    grads[("ev_w_in", 0)] = jnp.concatenate([gmain, gf], axis=1)
    token = done(("ev", 0), {k: grads[k] for k in (("ev_w_out", 0), ("ev_w_in", 0))})
    d, dg = _norm_in_bwd([dz0[None], dflb[None]], [w_main[None], w_f[None]], x0a, gain_after(W["mix_norm"][0:1], token), d)
    grads[("mix_norm", 0)] = dg
    d = ffn_b("ffn1", 0, d)
    return loss, d, grads


def _place():
    x, y, c = lax.axis_index("x"), lax.axis_index("y"), lax.axis_index("c")
    chips = [(1 - x, y), (x, 1 - y), (1 - x, 1 - y)]
    return x, y, c, chips


def _remote(src, dst, send_sem, recv_sem, to):
    return pltpu.make_async_remote_copy(src_ref=src, dst_ref=dst, send_sem=send_sem, recv_sem=recv_sem,
                                        device_id=to, device_id_type=MESH)


HBM = pl.BlockSpec(memory_space=pltpu.HBM)
SEM = pl.BlockSpec(memory_space=pltpu.SEMAPHORE)
EFFECT = pltpu.SideEffectType.DATAFLOW_SIDE_EFFECTING


def _in_hbm(a):
    return pltpu.with_memory_space_constraint(a, pltpu.HBM)


def _ag_start(bufs):
    n = len(bufs)

    def body(*refs):
        send_sems, recv_sems = refs[n], refs[n + 1]
        outs = refs[n + 2:]
        x, y, c, chips = _place()
        me = 2 * x + y
        for a in [n - 1] + list(range(n - 1)):
            if a == n - 1:
                blk = outs[a].at[me]
            else:
                h = outs[a].shape[1] // 2
                blk = outs[a].at[me, pl.ds(c * h, h)]
            for jj, (px, py) in enumerate(chips):
                _remote(blk, blk, send_sems.at[3 * a + jj], recv_sems.at[3 * a + jj], (px, py, c)).start()

    return _pallas(
        body, name="gather_start",
        out_shape=[pltpu.SemaphoreType.DMA((3 * n,)), pltpu.SemaphoreType.DMA((3 * n,))] + [pltpu.HBM(b.shape, b.dtype) for b in bufs],
        in_specs=[HBM] * n, out_specs=[SEM, SEM] + [HBM] * n, input_output_aliases={a: 2 + a for a in range(n)},
        compiler_params=pltpu.CompilerParams(has_side_effects=EFFECT),
    )(*[_in_hbm(b) for b in bufs])


def _ag_mid(g, ici_send, ici_recv, bufs, idx, taps, n_big, after):
    n = len(bufs)
    arrs = list(bufs) + ([taps] if taps is not None else [])
    m = len(arrs)

    def body(*refs):
        ici_s, ici_r = refs[0], refs[1]
        d_send, d_recv = refs[m + 3], refs[m + 4]
        outs = refs[m + 5:]
        x, y, c, chips = _place()
        me = 2 * x + y
        for i in range(m):
            a = idx[i] if i < n else n_big
            for jj, (px, py) in enumerate(chips):
                k = 3 * a + jj
                if i < n:
                    h = outs[i].shape[1] // 2
                    mine, blk = outs[i].at[me, pl.ds(c * h, h)], outs[i].at[2 * px + py, pl.ds(c * h, h)]
                else:
                    mine, blk = outs[i].at[me], outs[i].at[2 * px + py]
                _remote(mine, mine, ici_s.at[k], ici_r.at[k], (px, py, c)).wait_send()
                _remote(blk, blk, ici_s.at[k], ici_r.at[k], (px, py, c)).wait_recv()
                if i < n:
                    _remote(blk, blk, d_send.at[3 * i + jj], d_recv.at[3 * i + jj], (x, y, 1 - c)).start()

    return _pallas(
        body, name=f"gather_pass_on_{g}",
        out_shape=[pltpu.SemaphoreType.DMA((3 * n,)), pltpu.SemaphoreType.DMA((3 * n,))] + [pltpu.HBM(b.shape, b.dtype) for b in arrs],
        in_specs=[SEM, SEM] + [HBM] * m + [ANY], out_specs=[SEM, SEM] + [HBM] * m,
        input_output_aliases={2 + i: 2 + i for i in range(m)},
        compiler_params=pltpu.CompilerParams(has_side_effects=EFFECT),
    )(ici_send, ici_recv, *arrs, after)


def _ag_wait(g, d_send, d_recv, arrs, n, after):
    m = len(arrs)

    def body(*refs):
        d_s, d_r = refs[0], refs[1]
        outs = refs[m + 3:]
        x, y, c, chips = _place()
        for i in range(n):
            h = outs[i].shape[1] // 2
            for jj, (px, py) in enumerate(chips):
                sent = outs[i].at[2 * px + py, pl.ds(c * h, h)]
                got = outs[i].at[2 * px + py, pl.ds((1 - c) * h, h)]
                _remote(sent, sent, d_s.at[3 * i + jj], d_r.at[3 * i + jj], (x, y, 1 - c)).wait_send()
                _remote(got, got, d_s.at[3 * i + jj], d_r.at[3 * i + jj], (x, y, 1 - c)).wait_recv()

    return _pallas(
        body, name=f"gather_wait_{g}", out_shape=[pltpu.HBM(b.shape, b.dtype) for b in arrs],
        in_specs=[SEM, SEM] + [HBM] * m + [ANY], out_specs=[HBM] * m,
        input_output_aliases={2 + i: i for i in range(m)},
        compiler_params=pltpu.CompilerParams(has_side_effects=EFFECT),
    )(d_send, d_recv, *arrs, after)


def _pair_start(g, gs, after):
    n = len(gs)
    zones = [lax.empty((4, a.shape[1] // 2, a.shape[2]), a.dtype) for a in gs]
    extra = [] if after is None else [after]

    def body(*refs):
        k0 = 2 * n + len(extra)
        send_sems, recv_sems = refs[k0], refs[k0 + 1]
        src, dst = refs[k0 + 2:k0 + 2 + n], refs[k0 + 2 + n:k0 + 2 + 2 * n]
        token = refs[k0 + 2 + 2 * n]
        x, y, c, _ = _place()
        for a in range(n):
            h = src[a].shape[1] // 2
            _remote(src[a].at[:, pl.ds((1 - c) * h, h)], dst[a], send_sems.at[a], recv_sems.at[a], (x, y, 1 - c)).start()
        token[...] = jnp.zeros_like(token)

    return _pallas(
        body, name=f"grad_pair_start_{g}",
        out_shape=[pltpu.SemaphoreType.DMA((n,)), pltpu.SemaphoreType.DMA((n,))]
        + [pltpu.HBM(a.shape, a.dtype) for a in gs + zones] + [_sds((8, 128), F32)],
        in_specs=[HBM] * (2 * n) + [ANY] * len(extra),
        out_specs=[SEM, SEM] + [HBM] * (2 * n) + [pl.BlockSpec(memory_space=pltpu.VMEM)],
        input_output_aliases={i: 2 + i for i in range(2 * n)},
        compiler_params=pltpu.CompilerParams(has_side_effects=EFFECT),
    )(*[_in_hbm(a) for a in gs + zones], *extra)


def _pair_wait(g, send, recv, gs, zones):
    n = len(gs)

    def body(*refs):
        s_ref, r_ref = refs[0], refs[1]
        outs = refs[2 + 2 * n:]
        src, dst = outs[:n], outs[n:]
        x, y, c, _ = _place()
        for a in range(n):
            h = src[a].shape[1] // 2
            _remote(src[a].at[:, pl.ds((1 - c) * h, h)], dst[a], s_ref.at[a], r_ref.at[a], (x, y, 1 - c)).wait()

    return _pallas(
        body, name=f"grad_pair_wait_{g}", out_shape=[pltpu.HBM(a.shape, a.dtype) for a in gs + zones],
        in_specs=[SEM, SEM] + [HBM] * (2 * n), out_specs=[HBM] * (2 * n),
        input_output_aliases={2 + i: i for i in range(2 * n)},
        compiler_params=pltpu.CompilerParams(has_side_effects=EFFECT),
    )(send, recv, *gs, *zones)


def _pair_add(gs, others, c_arr):
    n = len(gs)

    def body(c_ref, *refs):
        for g_ref, o_ref, out_ref in zip(refs[:n], refs[n:2 * n], refs[2 * n:]):
            out_ref[...] = (g_ref[...].astype(F32) + o_ref[...].astype(F32)).astype(BF16)

    half = lambda a: pl.BlockSpec((1, a.shape[1] // 2, a.shape[2]), lambda k, c_ref: (k, c_ref[0], 0))
    whole = lambda a: pl.BlockSpec((1,) + a.shape[1:], lambda k, c_ref: (k, 0, 0))
    grid_spec = pltpu.PrefetchScalarGridSpec(
        num_scalar_prefetch=1, grid=(4,), in_specs=[half(a) for a in gs] + [whole(o) for o in others],
        out_specs=[whole(o) for o in others])
    return _pallas(body, name="grad_pair_add", grid_spec=grid_spec, out_shape=[_sds(o.shape, BF16) for o in others],
                   compiler_params=_cp(("parallel",)))(c_arr, *gs, *others)


def _chip_start(g, ss):
    n = len(ss)
    zones = [lax.empty((3,) + s.shape[1:], s.dtype) for s in ss]

    def body(*refs):
        send_sems, recv_sems = refs[2 * n], refs[2 * n + 1]
        src, dst = refs[2 * n + 2:3 * n + 2], refs[3 * n + 2:4 * n + 2]
        token = refs[4 * n + 2]
        x, y, c, chips = _place()
        for a in range(n):
            for jj, (px, py) in enumerate(chips):
                k = 3 * a + jj
                _remote(src[a].at[2 * px + py], dst[a].at[jj], send_sems.at[k], recv_sems.at[k], (px, py, c)).start()
        token[...] = jnp.zeros_like(token)

    return _pallas(
        body, name=f"grad_chip_start_{g}",
        out_shape=[pltpu.SemaphoreType.DMA((3 * n,)), pltpu.SemaphoreType.DMA((3 * n,))]
        + [pltpu.HBM(a.shape, a.dtype) for a in ss + zones] + [_sds((8, 128), F32)],
        in_specs=[HBM] * (2 * n), out_specs=[SEM, SEM] + [HBM] * (2 * n) + [pl.BlockSpec(memory_space=pltpu.VMEM)],
        input_output_aliases={i: 2 + i for i in range(2 * n)},
        compiler_params=pltpu.CompilerParams(has_side_effects=EFFECT),
    )(*[_in_hbm(a) for a in ss + zones])


def _chip_wait(sends, recvs, counts, ss, zones, after):
    nb, n = len(sends), len(ss)

    def body(*refs):
        s_refs, r_refs = refs[:nb], refs[nb:2 * nb]
        outs = refs[2 * nb + 2 * n + 1:]
        src, dst = outs[:n], outs[n:]
        x, y, c, chips = _place()
        a = 0
        for b in range(nb):
            for i in range(counts[b]):
                for jj, (px, py) in enumerate(chips):
                    k = 3 * i + jj
                    _remote(src[a].at[2 * px + py], dst[a].at[jj], s_refs[b].at[k], r_refs[b].at[k], (px, py, c)).wait()
                a += 1

    return _pallas(
        body, name="grad_chip_wait", out_shape=[pltpu.HBM(a.shape, a.dtype) for a in ss + zones],
        in_specs=[SEM] * (2 * nb) + [HBM] * (2 * n) + [ANY], out_specs=[HBM] * (2 * n),
        input_output_aliases={2 * nb + i: i for i in range(2 * n)},
        compiler_params=pltpu.CompilerParams(has_side_effects=EFFECT),
    )(*sends, *recvs, *ss, *zones, after)


def _chip_sum(s, r, where, dest, l, L):
    _, h, C = s.shape
    tr = h // 2

    def body(k_ref, s_ref, r_ref, *rest):
        out_ref = rest[-1]
        acc = s_ref[0].astype(F32)
        for jj in range(3):
            acc = acc + r_ref[jj].astype(F32)
        out_ref[...] = acc

    in_specs = [pl.BlockSpec((1, tr, C), lambda i, k_ref: (k_ref[0], i, 0)), pl.BlockSpec((3, tr, C), lambda i, k_ref: (0, i, 0))]
    args = [where, s, r]
    alias = {}
    if dest is not None:
        in_specs.append(ANY)
        args.append(dest)
        alias = {3: 0}
    grid_spec = pltpu.PrefetchScalarGridSpec(
        num_scalar_prefetch=1, grid=(2,), in_specs=in_specs,
        out_specs=pl.BlockSpec((None, tr, C), lambda i, k_ref: (l, 2 * k_ref[1] + i, 0)))
    return _pallas(body, name="grad_chip_sum", grid_spec=grid_spec, out_shape=_sds((L, 2 * h, C), F32),
                   input_output_aliases=alias, compiler_params=_cp(("arbitrary",)))(*args)


def _pair_share(bufs, layout):
    n = len(layout)
    n_out = len(bufs)

    def body(*refs):
        outs = refs[n_out:2 * n_out]
        send_sems, recv_sems = refs[2 * n_out:]
        x, y, c, _ = _place()
        cps = []
        for a, (o, l) in enumerate(layout):
            h = outs[o].shape[1] // 2
            blk = outs[o].at[l, pl.ds(c * h, h)]
            cps.append(_remote(blk, blk, send_sems.at[a], recv_sems.at[a], (x, y, 1 - c)))
        for cp in cps:
            cp.start()
        for a, (o, l) in enumerate(layout):
            h = outs[o].shape[1] // 2
            blk = outs[o].at[l, pl.ds((1 - c) * h, h)]
            _remote(blk, blk, send_sems.at[a], recv_sems.at[a], (x, y, 1 - c)).wait_recv()
        for cp in cps:
            cp.wait_send()

    return _pallas(body, name="grad_pair_share", in_specs=[ANY] * n_out, out_specs=[ANY] * n_out,
                   out_shape=[_sds(b.shape, b.dtype) for b in bufs], input_output_aliases={o: o for o in range(n_out)},
                   scratch_shapes=[pltpu.SemaphoreType.DMA((n,)), pltpu.SemaphoreType.DMA((n,))])(*bufs)


def _small_all_reduce(packed):
    P, L = packed.shape

    def body(in_ref, out_ref, slots, send_sems, recv_sems):
        x, y, c, _ = _place()
        me = 4 * x + 2 * y + c
        slots[me] = in_ref[...]
        cps = []
        for r in range(1, 8):
            px = 1 - x if r & 4 else x
            py = 1 - y if r & 2 else y
            pc = 1 - c if r & 1 else c
            cps.append(_remote(in_ref, slots.at[me], send_sems.at[r - 1], recv_sems.at[r - 1], (px, py, pc)))
        for cp in cps:
            cp.start()
        for r in range(1, 8):
            px = 1 - x if r & 4 else x
            py = 1 - y if r & 2 else y
            pc = 1 - c if r & 1 else c
            blk = slots.at[4 * px + 2 * py + pc]
            _remote(blk, blk, send_sems.at[r - 1], recv_sems.at[r - 1], (px, py, pc)).wait_recv()
        for cp in cps:
            cp.wait_send()
        acc = slots[0]
        for k in range(1, 8):
            acc = acc + slots[k]
        out_ref[...] = acc

    vm = pl.BlockSpec(memory_space=pltpu.VMEM)
    return _pallas(body, name="small_all_reduce", in_specs=[vm], out_specs=vm, out_shape=_sds((P, L), F32),
                   scratch_shapes=[pltpu.VMEM((8, P, L), F32), pltpu.SemaphoreType.DMA((7,)), pltpu.SemaphoreType.DMA((7,))])(packed)


def _adamw_math(w, g, m, v):
    m = ADAM_B1 * m + (1.0 - ADAM_B1) * g
    v = ADAM_B2 * v + (1.0 - ADAM_B2) * (g * g)
    m_hat = m / (1.0 - ADAM_B1 ** ADAM_STEP)
    v_hat = v / (1.0 - ADAM_B2 ** ADAM_STEP)
    delta = -ADAM_LR * (m_hat / (jnp.sqrt(v_hat) + ADAM_EPS) + ADAM_WD * w)
    return delta, m, v


def _adamw(w, g, m, v):
    shape = w.shape
    C = shape[-1]
    rows = math.prod(shape[:-1])
    tr = next(t for t in (512, 352, 256, 128, 64, 32, 16, 8, rows) if rows % t == 0)
    w2, g2, m2, v2 = (a.reshape(rows, C) for a in (w, g, m, v))

    def body(w_ref, g_ref, m_ref, v_ref, d_ref, nm_ref, nv_ref):
        d, nm, nv = _adamw_math(w_ref[...], g_ref[...], m_ref[...], v_ref[...])
        d_ref[...] = d
        nm_ref[...] = nm
        nv_ref[...] = nv

    blk = pl.BlockSpec((tr, C), lambda i: (i, 0))
    outs = _pallas(body, name="adamw", grid=(rows // tr,), in_specs=[blk] * 4, out_specs=[blk] * 3,
                   out_shape=[_sds((rows, C), F32)] * 3, compiler_params=_cp(("parallel",)))(w2, g2, m2, v2)
    return tuple(o.reshape(shape) for o in outs)


WEIGHTS = ["ffn1_norm", "ffn1_w_gate", "ffn1_w_up", "ffn1_w_down", "mix_norm", "ffn2_norm", "ffn2_w_gate", "ffn2_w_up",
           "ffn2_w_down", "ev_w_in", "ev_b_f", "ev_conv_w", "ev_conv_b", "ev_conv_norm", "ev_q_norm", "ev_k_norm",
           "ev_w_out", "od_w_in", "od_conv_w", "od_w_out"]
BIG = ([("ffn1_w_gate", 0), ("ffn1_w_up", 0), ("ffn1_w_down", 0), ("ev_w_in", 0), ("ev_w_out", 0),
        ("ffn2_w_gate", 0), ("ffn2_w_up", 0), ("ffn2_w_down", 0)]
       + [("ffn1_w_gate", 1), ("ffn1_w_up", 1), ("ffn1_w_down", 1), ("od_w_in", 0), ("od_w_out", 0),
          ("ffn2_w_gate", 1), ("ffn2_w_up", 1), ("ffn2_w_down", 1)])
TRANSPOSED = ("ffn1_w_gate", "ffn1_w_up", "ffn2_w_gate", "ffn2_w_up")
BLOCKS = [("ffn1", 0), ("ev", 0), ("ffn2", 0), ("ffn1", 1), ("od", 0), ("ffn2", 1)]
BLOCK_OF = {(name, l): (name.split("_w_")[0], l) for name, l in BIG}
BIG_NAMES = ["ffn1_w_gate", "ffn1_w_up", "ffn1_w_down", "ffn2_w_gate", "ffn2_w_up", "ffn2_w_down",
             "ev_w_in", "ev_w_out", "od_w_in", "od_w_out"]
SMALL = [("ffn1_norm", 16), ("mix_norm", 16), ("ffn2_norm", 16), ("ev_b_f", 8), ("ev_conv_w", 128), ("ev_conv_b", 8),
         ("ev_conv_norm", 8), ("ev_q_norm", 8), ("ev_k_norm", 8), ("od_conv_w", 24)]


def _to_lanes(a, rows):
    flat = a.reshape(-1)
    return jnp.pad(flat, (0, rows * 128 - flat.shape[0])).reshape(rows, 128)


def kernel(x, ffn1_norm, ffn1_w_gate, ffn1_w_up, ffn1_w_down, mix_norm, ffn2_norm, ffn2_w_gate, ffn2_w_up, ffn2_w_down, ev_w_in, ev_b_f, ev_conv_w, ev_conv_b, ev_conv_norm, ev_q_norm, ev_k_norm, ev_w_out, od_w_in, od_conv_w, od_w_out, loss_target, m_ffn1_norm, m_ffn1_w_gate, m_ffn1_w_up, m_ffn1_w_down, m_mix_norm, m_ffn2_norm, m_ffn2_w_gate, m_ffn2_w_up, m_ffn2_w_down, m_ev_w_in, m_ev_b_f, m_ev_conv_w, m_ev_conv_b, m_ev_conv_norm, m_ev_q_norm, m_ev_k_norm, m_ev_w_out, m_od_w_in, m_od_conv_w, m_od_w_out, v_ffn1_norm, v_ffn1_w_gate, v_ffn1_w_up, v_ffn1_w_down, v_mix_norm, v_ffn2_norm, v_ffn2_w_gate, v_ffn2_w_up, v_ffn2_w_down, v_ev_w_in, v_ev_b_f, v_ev_conv_w, v_ev_conv_b, v_ev_conv_norm, v_ev_q_norm, v_ev_k_norm, v_ev_w_out, v_od_w_in, v_od_conv_w, v_od_w_out):
    P = dict(ffn1_norm=ffn1_norm, ffn1_w_gate=ffn1_w_gate, ffn1_w_up=ffn1_w_up, ffn1_w_down=ffn1_w_down, mix_norm=mix_norm,
             ffn2_norm=ffn2_norm, ffn2_w_gate=ffn2_w_gate, ffn2_w_up=ffn2_w_up, ffn2_w_down=ffn2_w_down, ev_w_in=ev_w_in,
             ev_b_f=ev_b_f, ev_conv_w=ev_conv_w, ev_conv_b=ev_conv_b, ev_conv_norm=ev_conv_norm, ev_q_norm=ev_q_norm,
             ev_k_norm=ev_k_norm, ev_w_out=ev_w_out, od_w_in=od_w_in, od_conv_w=od_conv_w, od_w_out=od_w_out)
    M = dict(zip(WEIGHTS, [m_ffn1_norm, m_ffn1_w_gate, m_ffn1_w_up, m_ffn1_w_down, m_mix_norm, m_ffn2_norm, m_ffn2_w_gate,
                           m_ffn2_w_up, m_ffn2_w_down, m_ev_w_in, m_ev_b_f, m_ev_conv_w, m_ev_conv_b, m_ev_conv_norm,
                           m_ev_q_norm, m_ev_k_norm, m_ev_w_out, m_od_w_in, m_od_conv_w, m_od_w_out]))
    V = dict(zip(WEIGHTS, [v_ffn1_norm, v_ffn1_w_gate, v_ffn1_w_up, v_ffn1_w_down, v_mix_norm, v_ffn2_norm, v_ffn2_w_gate,
                           v_ffn2_w_up, v_ffn2_w_down, v_ev_w_in, v_ev_b_f, v_ev_conv_w, v_ev_conv_b, v_ev_conv_norm,
                           v_ev_q_norm, v_ev_k_norm, v_ev_w_out, v_od_w_in, v_od_conv_w, v_od_w_out]))
    for name in TRANSPOSED:
        P[name], M[name], V[name] = (jnp.swapaxes(a, 1, 2) for a in (P[name], M[name], V[name]))
    S, D = x.shape[1], x.shape[2]
    chip = 2 * lax.axis_index("x") + lax.axis_index("y")
    core = lax.axis_index("c")

    def own_slot(shard):
        return lax.dynamic_update_slice(lax.empty((4,) + shard.shape, shard.dtype), shard[None], (chip, 0, 0))

    taps = jnp.concatenate([_to_lanes(_pad_rows(ev_conv_w[0], 32), 32), _to_lanes(_pad_rows(od_conv_w[0], 8), 16)], axis=0)
    ici_send, ici_recv, *bufs = _ag_start([own_slot(P[name][l].astype(BF16)) for name, l in BIG] + [own_slot(taps)])
    cols = lambda a: a.transpose(1, 0, 2).reshape(a.shape[1], 4 * a.shape[2])
    W = {k: P[k] for k in ("ffn1_norm", "mix_norm", "ffn2_norm", "ev_b_f", "ev_q_norm", "ev_k_norm")}
    W["ev_conv_b"], W["ev_conv_norm"] = ev_conv_b, ev_conv_norm
    for tag in ("ffn1", "ffn2"):
        for kind in ("_w_gate", "_w_up", "_w_down"):
            W[tag + kind] = [None, None]
    passing = {}

    def pass_on(g, after):
        idx = [i for i, k in enumerate(BIG) if BLOCK_OF[k] == BLOCKS[g]]
        keys = [BIG[i] for i in idx] + (["taps"] if BLOCKS[g] == ("ev", 0) else [])
        passing[g] = (keys, _ag_mid(g, ici_send, ici_recv, [bufs[i] for i in idx], idx,
                                    bufs[-1] if BLOCKS[g] == ("ev", 0) else None, len(BIG), after))

    def need(block, after):
        g = BLOCKS.index(block)
        if g not in passing:
            pass_on(g, after)
        keys, (d_send, d_recv, *thru) = passing.pop(g)
        got = dict(zip(keys, _ag_wait(g, d_send, d_recv, thru, len(keys) - ("taps" in keys), after)))
        if 1 <= g < len(BLOCKS) - 1:
            pass_on(g + 1, after)
        for key, a in got.items():
            if key == "taps":
                continue
            name, l = key
            if name.startswith("ffn"):
                W[name][l] = a
            elif name.endswith("_w_in"):
                W[name] = cols(a)
            elif name.endswith("_w_out"):
                W[name] = a.reshape(4 * a.shape[1], D)
        if block == ("ev", 0):
            taps_all = got["taps"]
            W["ev_conv_w"] = cols(taps_all[:, :32].reshape(4, 32, 128))[:CONV_A_WIDTH]
            W["od_conv_w"] = cols(taps_all[:, 32:48].reshape(4, 8, 256))[:CONV_C_WIDTH]

    rows = lambda a: a.reshape(4, a.shape[0] // 4, a.shape[1])
    colsh = lambda a: a.reshape(a.shape[0], 4, a.shape[1] // 4).transpose(1, 0, 2)
    c_arr = core.reshape(1).astype(jnp.int32)
    where = jnp.stack([chip, core]).astype(jnp.int32)
    in_flight = []

    def done(block, block_grads):
        g = BLOCKS.index(block)
        keys = list(block_grads)
        gs = []
        for name, l in keys:
            a = block_grads[(name, l)]
            gs.append(colsh(a) if name == "ev_w_in" else rows(a) if name.endswith("_w_out") else a)
        for item in list(pairs):
            to_chips(item)
        send, recv, *rest = _pair_start(g, gs, chained.get("token"))
        n = len(keys)
        pairs.append((g, keys, send, recv, rest[:n], rest[n:2 * n]))
        if g == 0:
            to_chips(pairs[0])
        chained["token"] = rest[-1] if g else chained["token"]
        return chained["token"][0:1, 0:1]

    pairs, chained = [], {}

    def to_chips(item):
        pairs.remove(item)
        g, keys, send, recv, gs, zones = item
        n = len(keys)
        done_ = _pair_wait(g, send, recv, gs, zones)
        sums = list(_pair_add(list(done_[:n]), list(done_[n:]), c_arr))
        send2, recv2, *rest = _chip_start(g, sums)
        in_flight.append((keys, send2, recv2, rest[:n], rest[n:2 * n]))
        chained["token"] = rest[-1]

    loss, grad_x, grads = _local_step(x[0], loss_target[0], W, need, done)

    order = [k for keys, *_ in in_flight for k in keys]
    landed = _chip_wait([f[1] for f in in_flight], [f[2] for f in in_flight], [len(f[0]) for f in in_flight],
                        [a for f in in_flight for a in f[3]], [a for f in in_flight for a in f[4]], grad_x)
    sums, recvd = landed[:len(order)], landed[len(order):]
    stacked = {}
    for (name, l), s, r in zip(order, sums, recvd):
        stacked[name] = _chip_sum(s, r, where, stacked.get(name), l, P[name].shape[0])
    layout = [(BIG_NAMES.index(name), l) for name, l in order]
    big_grads = dict(zip(BIG_NAMES, _pair_share([stacked[name] for name in BIG_NAMES], layout)))

    def small_grad(name):
        if name.endswith("_norm") and name[:3] in ("ffn", "mix"):
            return jnp.concatenate([grads[(name, 0)], grads[(name, 1)]], axis=0)
        return grads[(name, 0)]

    packed = jnp.concatenate([_to_lanes(small_grad(name), r) for name, r in SMALL], axis=0)
    total = _small_all_reduce(packed)
    small_grads, at = {}, 0
    for name, r in SMALL:
        part = total[at:at + r].reshape(-1)
        at += r
        if name == "ev_conv_w":
            full_g = part[:CONV_A_WIDTH * D_CONV].reshape(CONV_A_WIDTH, D_CONV)
            small_grads[name] = lax.dynamic_slice_in_dim(full_g, chip * (D_CONV // 4), D_CONV // 4, axis=1)[None]
        elif name == "od_conv_w":
            full_g = part[:CONV_C_WIDTH * D].reshape(CONV_C_WIDTH, D)
            small_grads[name] = lax.dynamic_slice_in_dim(full_g, chip * (D // 4), D // 4, axis=1)[None]
        else:
            small_grads[name] = part[:math.prod(P[name].shape)].reshape(P[name].shape)

    grad_w, delta_w, new_m, new_v = [], [], [], []
    for name in WEIGHTS:
        g = big_grads[name] if name in big_grads else small_grads[name]
        outs = (g,) + _adamw(P[name], g, M[name], V[name])
        if name in TRANSPOSED:
            outs = tuple(jnp.swapaxes(a, 1, 2) for a in outs)
        for acc, a in zip((grad_w, delta_w, new_m, new_v), outs):
            acc.append(a)
    loss_all = lax.psum(loss[0, 0], ("x", "y", "c"))
    return (loss_all, grad_x[None], *grad_w, *delta_w, *new_m, *new_v)
```

```python
import functools
import math

import jax
import jax.numpy as jnp
from jax import lax
from jax.experimental import pallas as pl
from jax.experimental.pallas import tpu as pltpu

F32, BF16 = jnp.float32, jnp.bfloat16
EPS = 1e-6
FFN_RES = 0.5
N_HEADS, HEAD_DIM = 8, 64
D_CONV = 512
D_ATTN = N_HEADS * HEAD_DIM
CONV_A_WIDTH, CONV_C_WIDTH = 31, 3
ADAM_LR, ADAM_B1, ADAM_B2, ADAM_EPS, ADAM_WD, ADAM_STEP = 0.001, 0.9, 0.999, 1e-08, 0.01, 10
MESH = pl.DeviceIdType.MESH
ANY = pl.BlockSpec(memory_space=pl.ANY)

TOK_TILE = 512
DW_TILE = 1024
ATT_TILE = 512
QKN_TILE = 2048
HALO_A, HALO_C = 32, 16
SUBLANES = 8
CONV_ROWS = 64
SCAN_BLK = 256
MIB = 2 ** 20


def _pallas(body, **kw):
    return pl.pallas_call(body, **kw)


def _cp(sem=None, vmem_mib=48):
    return pltpu.CompilerParams(dimension_semantics=sem, vmem_limit_bytes=vmem_mib * MIB)


def _dot(a, b):
    return jnp.dot(a, b, preferred_element_type=F32)


def _dot_nt(a, b):
    return lax.dot_general(a, b, (((1,), (1,)), ((), ())), preferred_element_type=F32)


def _dot_tn(a, b):
    return lax.dot_general(a, b, (((0,), (0,)), ((), ())), preferred_element_type=F32)


def _sds(shape, dtype):
    return jax.ShapeDtypeStruct(shape, dtype)


def _rms(x):
    return lax.rsqrt(jnp.mean(x * x, axis=-1, keepdims=True) + EPS)


def _rms_bwd(dy, x, g):
    r = _rms(x)
    xh = x * r
    dxh = dy * g
    dx = r * (dxh - xh * jnp.mean(dxh * xh, axis=-1, keepdims=True))
    return dx, xh


def _silu_grad(z):
    s = jax.nn.sigmoid(z)
    return s * (1.0 + z * (1.0 - s))


def _ffn_fwd(x, g, wg, wu, wd):
    S, D = x.shape
    nc, Fs, _ = wd.shape
    tm = TOK_TILE

    def body(x_ref, g_ref, wg_ref, wu_ref, wd_ref, out_ref, xn_ref, G_ref, U_ref, acc_ref):
        j = pl.program_id(1)

        @pl.when(j == 0)
        def _():
            xv = x_ref[...]
            xn_ref[...] = (xv * _rms(xv) * g_ref[...]).astype(BF16)
            acc_ref[...] = jnp.zeros_like(acc_ref)

        xn = xn_ref[...]
        G = _dot_nt(xn, wg_ref[0])
        U = _dot_nt(xn, wu_ref[0])
        G_ref[0] = G.astype(BF16)
        U_ref[0] = U.astype(BF16)
        H = (G * jax.nn.sigmoid(G) * U).astype(BF16)
        acc_ref[...] += _dot(H, wd_ref[0])

        @pl.when(j == nc - 1)
        def _():
            out_ref[...] = x_ref[...] + FFN_RES * acc_ref[...]

    row = pl.BlockSpec((tm, D), lambda i, j: (i, 0))
    return _pallas(
        body, name="ffn_fwd", grid=(S // tm, nc),
        in_specs=[row, pl.BlockSpec((1, D), lambda i, j: (0, 0)),
                  pl.BlockSpec((1, Fs, D), lambda i, j: (j, 0, 0)), pl.BlockSpec((1, Fs, D), lambda i, j: (j, 0, 0)),
                  pl.BlockSpec((1, Fs, D), lambda i, j: (j, 0, 0))],
        out_specs=[row, row, pl.BlockSpec((1, tm, Fs), lambda i, j: (j, i, 0)),
                   pl.BlockSpec((1, tm, Fs), lambda i, j: (j, i, 0))],
        out_shape=[_sds((S, D), F32), _sds((S, D), BF16), _sds((nc, S, Fs), BF16), _sds((nc, S, Fs), BF16)],
        scratch_shapes=[pltpu.VMEM((tm, D), F32)],
        compiler_params=_cp(("parallel", "arbitrary")),
    )(x, g, wg, wu, wd)


def _ffn_bwd_w(dout, xn, G, U, wd):
    S, D = dout.shape
    nc, _, Fs = G.shape
    tm = min(DW_TILE, S)
    nt = S // tm
    sub = min(TOK_TILE, tm)

    def body(do_ref, xn_ref, G_ref, U_ref, wd_ref, dwg_ref, dwu_ref, dwd_ref, dG_ref, dU_ref, ag, au, ad, do_s, H_s):
        i = pl.program_id(1)

        @pl.when(i == 0)
        def _():
            ag[...] = jnp.zeros_like(ag)
            au[...] = jnp.zeros_like(au)
            ad[...] = jnp.zeros_like(ad)

        for r in range(0, tm, sub):
            rows = pl.ds(r, sub)
            do = (FFN_RES * do_ref[rows, :]).astype(BF16)
            do_s[rows, :] = do
            Gv = G_ref[0, rows, :].astype(F32)
            Uv = U_ref[0, rows, :].astype(F32)
            dH = _dot_nt(do, wd_ref[0])
            sg = jax.nn.sigmoid(Gv)
            act = Gv * sg
            H_s[rows, :] = (act * Uv).astype(BF16)
            dU_ref[0, rows, :] = (dH * act).astype(BF16)
            dG_ref[0, rows, :] = (dH * Uv * (sg * (1.0 + Gv * (1.0 - sg)))).astype(BF16)
        xnv = xn_ref[...]
        ag[...] += _dot_tn(dG_ref[0], xnv)
        au[...] += _dot_tn(dU_ref[0], xnv)
        ad[...] += _dot_tn(H_s[...], do_s[...])

        @pl.when(i == nt - 1)
        def _():
            dwg_ref[0] = ag[...].astype(BF16)
            dwu_ref[0] = au[...].astype(BF16)
            dwd_ref[0] = ad[...].astype(BF16)

    row = pl.BlockSpec((tm, D), lambda j, i: (i, 0))
    hid = pl.BlockSpec((1, tm, Fs), lambda j, i: (j, i, 0))
    wrow = pl.BlockSpec((1, Fs, D), lambda j, i: (j, 0, 0))
    return _pallas(
        body, name="ffn_bwd_w", grid=(nc, nt),
        in_specs=[row, row, hid, hid, wrow],
        out_specs=[wrow, wrow, wrow, hid, hid],
        out_shape=[_sds((nc, Fs, D), BF16)] * 3 + [_sds((nc, S, Fs), BF16)] * 2,
        scratch_shapes=[pltpu.VMEM((Fs, D), F32)] * 3 + [pltpu.VMEM((tm, D), BF16), pltpu.VMEM((tm, Fs), BF16)],
        compiler_params=_cp(("parallel", "arbitrary"), 56),
    )(dout, xn, G, U, wd)


def _norm_in_bwd(dzs, ws, x, g, dres, w_rows=False):
    S, D = x.shape
    nc = dzs[0].shape[0]
    n = len(dzs)
    tm = TOK_TILE

    def body(*refs):
        dz_refs, w_refs = refs[:n], refs[n:2 * n]
        x_ref, g_ref, dres_ref, dx_ref, dg_ref, acc_ref = refs[2 * n:]
        i, j = pl.program_id(0), pl.program_id(1)

        @pl.when(j == 0)
        def _():
            acc_ref[...] = jnp.zeros_like(acc_ref)

        @pl.when((i == 0) & (j == 0))
        def _():
            dg_ref[...] = jnp.zeros_like(dg_ref)

        for dz_ref, w_ref in zip(dz_refs, w_refs):
            acc_ref[...] += _dot(dz_ref[0], w_ref[0]) if w_rows else _dot_nt(dz_ref[0], w_ref[0])

        @pl.when(j == nc - 1)
        def _():
            dxn = acc_ref[...]
            dx, xh = _rms_bwd(dxn, x_ref[...], g_ref[...])
            dx_ref[...] = dx + dres_ref[...]
            dg_ref[...] += jnp.sum(dxn * xh, axis=0, keepdims=True)

    row = pl.BlockSpec((tm, D), lambda i, j: (i, 0))
    one = pl.BlockSpec((1, D), lambda i, j: (0, 0))
    in_specs = [pl.BlockSpec((1, tm, dz.shape[2]), lambda i, j: (j, i, 0)) for dz in dzs]
    in_specs += [pl.BlockSpec((1,) + w.shape[1:], lambda i, j: (j, 0, 0)) for w in ws]
    return _pallas(
        body, name="norm_in_bwd", grid=(S // tm, nc),
        in_specs=in_specs + [row, one, row], out_specs=[row, one],
        out_shape=[_sds((S, D), F32), _sds((1, D), F32)],
        scratch_shapes=[pltpu.VMEM((tm, D), F32)],
        compiler_params=_cp(("arbitrary", "arbitrary")),
    )(*dzs, *ws, x, g, dres)


def _norm_proj(x, g, w, w2=None):
    S, D = x.shape
    N = w.shape[1]
    tm = TOK_TILE

    def body(*refs):
        if w2 is None:
            x_ref, g_ref, w_ref, h_ref, z_ref = refs
        else:
            x_ref, g_ref, w_ref, w2_ref, h_ref, z_ref, z2_ref = refs
        xv = x_ref[...]
        h = (xv * _rms(xv) * g_ref[...]).astype(BF16)
        h_ref[...] = h
        z_ref[...] = _dot(h, w_ref[...]).astype(BF16)
        if w2 is not None:
            z2_ref[...] = _dot(h, w2_ref[...])

    row = pl.BlockSpec((tm, D), lambda i: (i, 0))
    in_specs = [row, pl.BlockSpec((1, D), lambda i: (0, 0)), pl.BlockSpec((D, N), lambda i: (0, 0))]
    out_specs = [row, pl.BlockSpec((tm, N), lambda i: (i, 0))]
    out_shape = [_sds((S, D), BF16), _sds((S, N), BF16)]
    args = [x, g, w]
    if w2 is not None:
        N2 = w2.shape[1]
        in_specs.append(pl.BlockSpec((D, N2), lambda i: (0, 0)))
        out_specs.append(pl.BlockSpec((tm, N2), lambda i: (i, 0)))
        out_shape.append(_sds((S, N2), F32))
        args.append(w2)
    return _pallas(body, name="norm_proj", grid=(S // tm,), in_specs=in_specs, out_specs=out_specs,
                   out_shape=out_shape, compiler_params=_cp(("parallel",)))(*args)


def _proj_res(acts, ws, res):
    S, D = res.shape
    n = len(acts)
    tm = TOK_TILE

    def body(*refs):
        a_refs, w_refs = refs[:n], refs[n:2 * n]
        res_ref, out_ref = refs[2 * n:]
        acc = res_ref[...]
        for a_ref, w_ref in zip(a_refs, w_refs):
            acc = acc + _dot(a_ref[...], w_ref[...])
        out_ref[...] = acc

    row = pl.BlockSpec((tm, D), lambda i: (i, 0))
    in_specs = [pl.BlockSpec((tm, a.shape[1]), lambda i: (i, 0)) for a in acts]
    in_specs += [pl.BlockSpec(w.shape, lambda i: (0, 0)) for w in ws]
    return _pallas(body, name="proj_res", grid=(S // tm,), in_specs=in_specs + [row], out_specs=row,
                   out_shape=_sds((S, D), F32), compiler_params=_cp(("parallel",)))(*acts, *ws, res)


def _matmul_nt(a, w, after=None):
    S, K = a.shape
    M = w.shape[0]
    tm = TOK_TILE

    def body(a_ref, w_ref, *rest):
        rest[-1][...] = _dot_nt(a_ref[...].astype(BF16), w_ref[...])

    extra = [] if after is None else [after]
    return _pallas(body, name="matmul_nt", grid=(S // tm,),
                   in_specs=[pl.BlockSpec((tm, K), lambda i: (i, 0)), pl.BlockSpec((M, K), lambda i: (0, 0))] + [ANY] * len(extra),
                   out_specs=pl.BlockSpec((tm, M), lambda i: (i, 0)), out_shape=_sds((S, M), F32),
                   compiler_params=_cp(("parallel",)))(a, w, *extra)


def _matmul_tn(a, b, tn):
    S, M = a.shape
    N = b.shape[1]
    tm = min(DW_TILE, S)
    nt = S // tm

    def body(a_ref, b_ref, o_ref, acc_ref):
        i = pl.program_id(1)

        @pl.when(i == 0)
        def _():
            acc_ref[...] = jnp.zeros_like(acc_ref)

        acc_ref[...] += _dot_tn(a_ref[...].astype(BF16), b_ref[...].astype(BF16))

        @pl.when(i == nt - 1)
        def _():
            o_ref[0] = acc_ref[...].astype(BF16)

    return _pallas(body, name="matmul_tn", grid=(N // tn, nt),
                   in_specs=[pl.BlockSpec((tm, M), lambda j, i: (i, 0)), pl.BlockSpec((tm, tn), lambda j, i: (i, j))],
                   out_specs=pl.BlockSpec((1, M, tn), lambda j, i: (j, 0, 0)), out_shape=_sds((N // tn, M, tn), BF16),
                   scratch_shapes=[pltpu.VMEM((M, tn), F32)],
                   compiler_params=_cp(("parallel", "arbitrary")))(a, b)


def _fill_shifts(win, rows):
    for b in range(1, SUBLANES):
        win[b, pl.ds(0, rows - SUBLANES), :] = win[0, pl.ds(b, rows - SUBLANES), :]


def _tap(win, offset, n, base=0):
    start = base + (offset - offset % SUBLANES)
    if not isinstance(start, int):
        start = pl.multiple_of(start, SUBLANES)
    return win[offset % SUBLANES, pl.ds(start, n), :]


def _conv_a_fwd(z, cw, cb, cn):
    S = z.shape[0]
    C = D_CONV
    tm = TOK_TILE
    hb = tm // HALO_A

    def body(u_ref, gt_ref, up_ref, gp_ref, cw_ref, cb_ref, cn_ref, a_ref, a1_ref, win):
        i = pl.program_id(0)
        prev = up_ref[...].astype(F32) * jax.nn.sigmoid(gp_ref[...].astype(F32))
        win[0, pl.ds(0, HALO_A), :] = jnp.where(i == 0, 0.0, prev)
        win[0, pl.ds(HALO_A, tm), :] = u_ref[...].astype(F32) * jax.nn.sigmoid(gt_ref[...].astype(F32))
        _fill_shifts(win, tm + HALO_A)

        acc = jnp.zeros((tm, C), F32)
        for k in range(CONV_A_WIDTH):
            acc = acc + cw_ref[k:k + 1, :] * _tap(win, HALO_A - (CONV_A_WIDTH - 1) + k, tm)
        a1 = acc + cb_ref[...]
        a1_ref[...] = a1
        a2 = a1 * _rms(a1) * cn_ref[...]
        a_ref[...] = (a2 * jax.nn.sigmoid(a2)).astype(BF16)

    cur = lambda c: pl.BlockSpec((tm, C), lambda i, c=c: (i, c))
    prv = lambda c: pl.BlockSpec((HALO_A, C), lambda i, c=c: (jnp.maximum(i * hb - 1, 0), c))
    vec = pl.BlockSpec((1, C), lambda i: (0, 0))
    return _pallas(body, name="conv_a_fwd", grid=(S // tm,),
                   in_specs=[cur(0), cur(1), prv(0), prv(1), pl.BlockSpec((32, C), lambda i: (0, 0)), vec, vec],
                   out_specs=[pl.BlockSpec((tm, C), lambda i: (i, 0)), pl.BlockSpec((tm, C), lambda i: (i, 0))],
                   out_shape=[_sds((S, C), BF16), _sds((S, C), F32)],
                   scratch_shapes=[pltpu.VMEM((SUBLANES, tm + HALO_A, C), F32)],
                   compiler_params=_cp(("parallel",)))(z, z, z, z, cw, cb, cn)


def _conv_a_bwd(da, a1, z, cw, cn):
    S = z.shape[0]
    C = D_CONV
    tm = TOK_TILE
    hb = tm // HALO_A
    nt = S // tm
    W = CONV_A_WIDTH

    def body(da_ref, a1_ref, dan_ref, a1n_ref, u_ref, gt_ref, up_ref, gp_ref, cw_ref, cn_ref,
             duz_ref, dcw_ref, dcb_ref, dcn_ref, win, dwin):
        i = pl.program_id(0)

        @pl.when(i == 0)
        def _():
            dcw_ref[...] = jnp.zeros_like(dcw_ref)
            dcb_ref[...] = jnp.zeros_like(dcb_ref)
            dcn_ref[...] = jnp.zeros_like(dcn_ref)

        cnv = cn_ref[...]

        def da1_of(dav, a1v):
            a2 = a1v * _rms(a1v) * cnv
            da2 = dav * _silu_grad(a2)
            dx, xh = _rms_bwd(da2, a1v, cnv)
            return dx, da2 * xh

        da1, dcn_t = da1_of(da_ref[...], a1_ref[...])
        da1n, _ = da1_of(dan_ref[...], a1n_ref[...])
        dwin[0, pl.ds(0, tm), :] = da1
        dwin[0, pl.ds(tm, HALO_A), :] = jnp.where(i == nt - 1, 0.0, da1n)
        _fill_shifts(dwin, tm + HALO_A)
        dcb_ref[...] += jnp.sum(da1, axis=0, keepdims=True)
        dcn_ref[...] += jnp.sum(dcn_t, axis=0, keepdims=True)

        prev = up_ref[...].astype(F32) * jax.nn.sigmoid(gp_ref[...].astype(F32))
        win[0, pl.ds(0, HALO_A), :] = jnp.where(i == 0, 0.0, prev)
        win[0, pl.ds(HALO_A, tm), :] = u_ref[...].astype(F32) * jax.nn.sigmoid(gt_ref[...].astype(F32))
        _fill_shifts(win, tm + HALO_A)

        def rows_block(rb, carry):
            r0 = pl.multiple_of(rb * CONV_ROWS, CONV_ROWS)
            rows = pl.ds(r0, CONV_ROWS)
            da1_b = dwin[0, rows, :]
            da0 = jnp.zeros((CONV_ROWS, C), F32)
            for k in range(W):
                da0 = da0 + cw_ref[k:k + 1, :] * _tap(dwin, W - 1 - k, CONV_ROWS, r0)
                dcw_ref[k:k + 1, :] += jnp.sum(da1_b * _tap(win, HALO_A - (W - 1) + k, CONV_ROWS, r0), axis=0, keepdims=True)
            u = u_ref[rows, :].astype(F32)
            sg = jax.nn.sigmoid(gt_ref[rows, :].astype(F32))
            duz_ref[rows, 0:C] = (da0 * sg).astype(BF16)
            duz_ref[rows, C:2 * C] = (da0 * u * sg * (1.0 - sg)).astype(BF16)
            return carry

        lax.fori_loop(0, tm // CONV_ROWS, rows_block, 0)

    cur = lambda c: pl.BlockSpec((tm, C), lambda i, c=c: (i, c))
    prv = lambda c: pl.BlockSpec((HALO_A, C), lambda i, c=c: (jnp.maximum(i * hb - 1, 0), c))
    nxt = pl.BlockSpec((HALO_A, C), lambda i: (jnp.minimum((i + 1) * hb, S // HALO_A - 1), 0))
    vec = pl.BlockSpec((1, C), lambda i: (0, 0))
    return _pallas(body, name="conv_a_bwd", grid=(nt,),
                   in_specs=[cur(0), cur(0), nxt, nxt, cur(0), cur(1), prv(0), prv(1),
                             pl.BlockSpec((32, C), lambda i: (0, 0)), vec],
                   out_specs=[pl.BlockSpec((tm, 2 * C), lambda i: (i, 0)), pl.BlockSpec((32, C), lambda i: (0, 0)), vec, vec],
                   out_shape=[_sds((S, 2 * C), BF16), _sds((32, C), F32), _sds((1, C), F32), _sds((1, C), F32)],
                   scratch_shapes=[pltpu.VMEM((SUBLANES, tm + HALO_A, C), F32)] * 2,
                   compiler_params=_cp(("arbitrary",)))(da, a1, da, a1, z, z, z, z, cw, cn)


def _forget_scan(fl, bf):
    S, L = fl.shape
    B = SCAN_BLK

    def body(fl_ref, bf_ref, flb_ref, F_ref):
        tri = (lax.broadcasted_iota(jnp.int32, (B, B), 0) >= lax.broadcasted_iota(jnp.int32, (B, B), 1)).astype(F32)

        def step(c, carry):
            rows = pl.ds(pl.multiple_of(c * B, B), B)
            v = fl_ref[rows, :] + bf_ref[...]
            flb_ref[rows, :] = v
            lf = jnp.minimum(v, 0.0) - jnp.log1p(jnp.exp(-jnp.abs(v)))
            cs = jnp.dot(tri, lf, precision=lax.Precision.HIGHEST, preferred_element_type=F32) + carry
            F_ref[rows, :] = cs
            return cs[B - 1:B, :]

        lax.fori_loop(0, S // B, step, jnp.zeros((1, L), F32))

    return _pallas(body, name="forget_scan", out_shape=[_sds((S, L), F32), _sds((S, L), F32)],
                   compiler_params=_cp())(fl, bf)


def _forget_scan_bwd(dF, flb):
    S, L = dF.shape
    B = SCAN_BLK
    nb = S // B

    def body(dF_ref, flb_ref, dfl_ref, db_ref):
        tri = (lax.broadcasted_iota(jnp.int32, (B, B), 0) <= lax.broadcasted_iota(jnp.int32, (B, B), 1)).astype(F32)

        def step(t, carry):
            carry_cs, db = carry
            rows = pl.ds(pl.multiple_of((nb - 1 - t) * B, B), B)
            cs = jnp.dot(tri, dF_ref[rows, :], precision=lax.Precision.HIGHEST, preferred_element_type=F32) + carry_cs
            dfl = cs * jax.nn.sigmoid(-flb_ref[rows, :])
            dfl_ref[rows, :] = dfl
            return cs[0:1, :], db + jnp.sum(dfl, axis=0, keepdims=True)

        _, db = lax.fori_loop(0, nb, step, (jnp.zeros((1, L), F32), jnp.zeros((1, L), F32)))
        db_ref[...] = db

    return _pallas(body, name="forget_scan_bwd", out_shape=[_sds((S, L), F32), _sds((1, L), F32)],
                   compiler_params=_cp())(dF, flb)


NEG = -1e30


def _causal_mask(t):
    return lax.broadcasted_iota(jnp.int32, (t, t), 0) >= lax.broadcasted_iota(jnp.int32, (t, t), 1)


AUG = 128
C_F, C_ONE, C_LSE = 64, 67, 70


def _split3(f):
    a = f.astype(BF16).astype(F32)
    r = f - a
    b = r.astype(BF16).astype(F32)
    return a, b, r - b


def _put3(lane, base, parts, other):
    out = other
    for k, p in enumerate(parts):
        out = jnp.where(lane == base + k, p, out)
    return out


def _ones3(lane, base):
    return (lane >= base) & (lane < base + 3)


def _lane_ids(rows):
    return lax.broadcasted_iota(jnp.int32, (rows, AUG), 1)


def _pair_rms(x, lo):
    sq = x * x
    ms_a = jnp.sum(jnp.where(lo, sq, 0.0), axis=-1, keepdims=True) * (1.0 / HEAD_DIM)
    ms_b = jnp.sum(jnp.where(lo, 0.0, sq), axis=-1, keepdims=True) * (1.0 / HEAD_DIM)
    return jnp.where(lo, lax.rsqrt(ms_a + EPS), lax.rsqrt(ms_b + EPS))


def _qkv_prep(z, Fc, qw, kw):
    S = z.shape[0]
    tp = min(QKN_TILE, S)
    scale = 1.0 / math.sqrt(HEAD_DIM)

    def body(zq_ref, zk_ref, zv_ref, F_ref, qw_ref, kw_ref, q_ref, k_ref, v_ref):
        j = pl.program_id(0)
        lane = _lane_ids(tp)
        lo = lane < HEAD_DIM
        Fv = F_ref[...]
        xq = zq_ref[...].astype(F32)
        xk = zk_ref[...].astype(F32)
        qn = xq * _pair_rms(xq, lo) * qw_ref[...] * scale
        kn = xk * _pair_rms(xk, lo) * kw_ref[...]
        vv = zv_ref[...].astype(F32)
        for half in range(2):
            take = (lambda a: a) if half == 0 else (lambda a: pltpu.roll(a, HEAD_DIM, 1))
            fp = _split3(jnp.sum(jnp.where(lane == 2 * j + half, Fv, 0.0), axis=-1, keepdims=True))
            qx = _put3(lane, C_F, fp, jnp.where(_ones3(lane, C_ONE), 1.0, 0.0))
            kx = _put3(lane, C_ONE, [-p for p in fp], jnp.where(_ones3(lane, C_F) | _ones3(lane, C_LSE), 1.0, 0.0))
            vx = jnp.where(_ones3(lane, C_F), 1.0, 0.0)
            q_ref[half] = jnp.where(lo, take(qn), qx).astype(BF16)
            k_ref[half] = jnp.where(lo, take(kn), kx).astype(BF16)
            v_ref[half] = jnp.where(lo, take(vv), vx).astype(BF16)

    col = lambda c0: pl.BlockSpec((tp, AUG), lambda j, i, c0=c0: (i, c0 + j))
    vec = pl.BlockSpec((1, AUG), lambda j, i: (0, 0))
    out = pl.BlockSpec((2, tp, AUG), lambda j, i: (j, i, 0))
    return _pallas(body, name="qkv_prep", grid=(N_HEADS // 2, S // tp),
                   in_specs=[col(8), col(12), col(16), pl.BlockSpec((tp, AUG), lambda j, i: (i, 0)), vec, vec],
                   out_specs=[out, out, out], out_shape=[_sds((N_HEADS, S, AUG), BF16)] * 3,
                   compiler_params=_cp(("parallel", "parallel")))(z, z, z, Fc, qw, kw)


def _fox_fwd(q_aug, k_aug, v_aug):
    H, S, A = q_aug.shape
    t = ATT_TILE
    nq = S // t

    def body(q_ref, k_ref, v_ref, o_ref, q2_ref):
        i = pl.program_id(1)
        q = q_ref[0]

        def tile(j, carry, diag):
            m, acc = carry
            rows = pl.ds(pl.multiple_of(j * t, t), t)
            s = _dot_nt(q, k_ref[0, rows, :])
            if diag:
                s = jnp.where(_causal_mask(t), s, NEG)
            m_new = jnp.maximum(m, jnp.max(s, axis=-1, keepdims=True))
            p = jnp.exp(s - m_new)
            acc = jnp.exp(m - m_new) * acc + _dot(p.astype(BF16), v_ref[0, rows, :])
            return m_new, acc

        init = (jnp.full((t, 1), NEG, F32), jnp.zeros((t, A), F32))
        carry = lax.fori_loop(0, i, lambda j, c: tile(j, c, False), init)
        m, acc = tile(i, carry, True)
        lane = _lane_ids(t)
        l = jnp.sum(jnp.where(lane == C_F, acc, 0.0), axis=-1, keepdims=True)
        o_ref[0] = (acc / l).astype(BF16)
        lse = m + jnp.log(l)
        q2_ref[0] = (q.astype(F32) + _put3(lane, C_LSE, [-p for p in _split3(lse)], 0.0)).astype(BF16)

    qblk = pl.BlockSpec((1, t, A), lambda h, i: (h, i, 0))
    full = pl.BlockSpec((1, S, A), lambda h, i: (h, 0, 0))
    return _pallas(body, name="fox_fwd", grid=(H, nq), in_specs=[qblk, full, full], out_specs=[qblk, qblk],
                   out_shape=[_sds((H, S, A), BF16)] * 2, compiler_params=_cp(("parallel", "parallel")))(q_aug, k_aug, v_aug)


def _do_prep(dcat, o_aug):
    S = dcat.shape[0]
    tp = min(QKN_TILE, S)

    def body(d_ref, o_ref, out_ref):
        lane = _lane_ids(tp)
        lo = lane < HEAD_DIM
        x = d_ref[...]
        for half in range(2):
            d = jnp.where(lo, x if half == 0 else pltpu.roll(x, HEAD_DIM, 1), 0.0)
            delta = jnp.sum(d * o_ref[half].astype(F32), axis=-1, keepdims=True)
            out_ref[half] = jnp.where(lo, d, _put3(lane, C_F, [-p for p in _split3(delta)], 0.0)).astype(BF16)

    pair = pl.BlockSpec((2, tp, AUG), lambda j, i: (j, i, 0))
    return _pallas(body, name="do_prep", grid=(N_HEADS // 2, S // tp),
                   in_specs=[pl.BlockSpec((tp, AUG), lambda j, i: (i, D_CONV // AUG + j)), pair], out_specs=pair,
                   out_shape=_sds((N_HEADS, S, AUG), BF16), compiler_params=_cp(("parallel", "parallel")))(dcat, o_aug)


def _fox_bwd(q2, k_aug, v_aug, do_aug):
    H, S, A = q2.shape
    t = ATT_TILE
    nq = S // t

    def body(q_ref, k_ref, v_ref, do_ref, dq_ref, dk_ref, dv_ref):
        j = pl.program_id(1)

        @pl.when(j == 0)
        def _():
            dq_ref[...] = jnp.zeros_like(dq_ref)

        k = k_ref[0]
        vv = v_ref[0]

        def tile(i, carry, diag):
            dk, dv = carry
            rows = pl.ds(pl.multiple_of(i * t, t), t)
            q = q_ref[0, rows, :]
            dov = do_ref[0, rows, :]
            s = _dot_nt(q, k)
            if diag:
                s = jnp.where(_causal_mask(t), s, NEG)
            p = jnp.exp(s)
            dv = dv + _dot_tn(p.astype(BF16), dov)
            dsb = (p * _dot_nt(dov, vv)).astype(BF16)
            dq_ref[0, rows, :] += _dot(dsb, k)
            dk = dk + _dot_tn(dsb, q)
            return dk, dv

        init = (jnp.zeros((t, A), F32), jnp.zeros((t, A), F32))
        carry = tile(j, init, True)
        dk, dv = lax.fori_loop(j + 1, nq, lambda i, c: tile(i, c, False), carry)
        dk_ref[0] = dk
        dv_ref[0] = dv

    full = pl.BlockSpec((1, S, A), lambda h, j: (h, 0, 0))
    kblk = pl.BlockSpec((1, t, A), lambda h, j: (h, j, 0))
    return _pallas(body, name="fox_bwd", grid=(H, nq), in_specs=[full, kblk, kblk, full], out_specs=[full, kblk, kblk],
                   out_shape=[_sds((H, S, A), F32)] * 3,
                   compiler_params=_cp(("parallel", "arbitrary")))(q2, k_aug, v_aug, do_aug)


def _qkv_bwd(dq, dk, dv, z, qw, kw):
    S = z.shape[0]
    tp = min(QKN_TILE, S)
    scale = 1.0 / math.sqrt(HEAD_DIM)

    def body(dq_ref, dk_ref, dv_ref, zq_ref, zk_ref, qw_ref, kw_ref, dqf_ref, dkf_ref, dvf_ref, dF_ref, dqw_ref, dkw_ref):
        i, j = pl.program_id(0), pl.program_id(1)
        lane = _lane_ids(tp)
        lo = lane < HEAD_DIM

        @pl.when((i == 0) & (j == 0))
        def _():
            dqw_ref[...] = jnp.zeros_like(dqw_ref)
            dkw_ref[...] = jnp.zeros_like(dkw_ref)

        def pair(ref):
            return jnp.where(lo, ref[0], pltpu.roll(ref[1], HEAD_DIM, 1))

        def norm_bwd(g, x, w):
            r = _pair_rms(x, lo)
            xh = x * r
            dxh = g * w
            tt = dxh * xh
            mean_a = jnp.sum(jnp.where(lo, tt, 0.0), axis=-1, keepdims=True) * (1.0 / HEAD_DIM)
            mean_b = jnp.sum(jnp.where(lo, 0.0, tt), axis=-1, keepdims=True) * (1.0 / HEAD_DIM)
            return r * (dxh - xh * jnp.where(lo, mean_a, mean_b)), g * xh

        dxq, gq = norm_bwd(pair(dq_ref) * scale, zq_ref[...].astype(F32), qw_ref[...])
        dqf_ref[...] = dxq.astype(BF16)
        dqw_ref[...] += jnp.sum(gq, axis=0, keepdims=True)
        dxk, gk = norm_bwd(pair(dk_ref), zk_ref[...].astype(F32), kw_ref[...])
        dkf_ref[...] = dxk.astype(BF16)
        dkw_ref[...] += jnp.sum(gk, axis=0, keepdims=True)
        dvf_ref[...] = pair(dv_ref).astype(BF16)

        contrib = jnp.zeros((tp, AUG), F32)
        for half in range(2):
            df = (jnp.sum(jnp.where(lane == C_F, dq_ref[half], 0.0), axis=-1, keepdims=True)
                  - jnp.sum(jnp.where(lane == C_ONE, dk_ref[half], 0.0), axis=-1, keepdims=True))
            contrib = jnp.where(lane == 2 * j + half, df, contrib)

        @pl.when(j == 0)
        def _():
            dF_ref[...] = contrib

        @pl.when(j > 0)
        def _():
            dF_ref[...] += contrib

    pairb = pl.BlockSpec((2, tp, AUG), lambda i, j: (j, i, 0))
    col = lambda c0: pl.BlockSpec((tp, AUG), lambda i, j, c0=c0: (i, c0 + j))
    vec = pl.BlockSpec((1, AUG), lambda i, j: (0, 0))
    flat = pl.BlockSpec((tp, AUG), lambda i, j: (i, j))
    return _pallas(body, name="qkv_bwd", grid=(S // tp, N_HEADS // 2),
                   in_specs=[pairb, pairb, pairb, col(8), col(12), vec, vec],
                   out_specs=[flat, flat, flat, pl.BlockSpec((tp, AUG), lambda i, j: (i, 0)), vec, vec],
                   out_shape=[_sds((S, D_ATTN), BF16)] * 3 + [_sds((S, AUG), F32), _sds((1, AUG), F32), _sds((1, AUG), F32)],
                   compiler_params=_cp(("arbitrary", "arbitrary")))(dq, dk, dv, z, z, qw, kw)


def _proj_res_heads(a, wa, o_aug, wo, res):
    S, D = res.shape
    H = o_aug.shape[0]
    tm = TOK_TILE

    def body(a_ref, wa_ref, o_ref, wo_ref, res_ref, out_ref):
        acc = res_ref[...] + _dot(a_ref[...], wa_ref[...])
        for h in range(H):
            acc = acc + _dot(o_ref[h], wo_ref[h])
        out_ref[...] = acc

    row = pl.BlockSpec((tm, D), lambda i: (i, 0))
    return _pallas(body, name="proj_res_heads", grid=(S // tm,),
                   in_specs=[pl.BlockSpec((tm, a.shape[1]), lambda i: (i, 0)), pl.BlockSpec(wa.shape, lambda i: (0, 0)),
                             pl.BlockSpec((H, tm, AUG), lambda i: (0, i, 0)), pl.BlockSpec(wo.shape, lambda i: (0, 0, 0)), row],
                   out_specs=row, out_shape=_sds((S, D), F32), compiler_params=_cp(("parallel",)))(a, wa, o_aug, wo, res)


def _heads_tn(o_aug, d):
    H, S, A = o_aug.shape
    D = d.shape[1]
    tm = min(DW_TILE, S)
    nt = S // tm

    def body(o_ref, d_ref, out_ref, acc_ref):
        i = pl.program_id(0)

        @pl.when(i == 0)
        def _():
            acc_ref[...] = jnp.zeros_like(acc_ref)

        dv = d_ref[...].astype(BF16)
        for h in range(H):
            acc_ref[h] += _dot_tn(o_ref[h], dv)

        @pl.when(i == nt - 1)
        def _():
            out_ref[...] = acc_ref[...].astype(BF16)

    return _pallas(body, name="heads_tn", grid=(nt,),
                   in_specs=[pl.BlockSpec((H, tm, A), lambda i: (0, i, 0)), pl.BlockSpec((tm, D), lambda i: (i, 0))],
                   out_specs=pl.BlockSpec((H, A, D), lambda i: (0, 0, 0)), out_shape=_sds((H, A, D), BF16),
                   scratch_shapes=[pltpu.VMEM((H, A, D), F32)], compiler_params=_cp(("arbitrary",)))(o_aug, d)


def _odd_mid_fwd(z, cw):
    S = z.shape[0]
    D = z.shape[1] // 3
    tm = TOK_TILE
    hb = tm // HALO_C
    W = CONV_C_WIDTH

    def body(gb_ref, gc_ref, hh_ref, gcp_ref, hhp_ref, cw_ref, y_ref, win):
        i = pl.program_id(0)
        prev = gcp_ref[...].astype(F32) * hhp_ref[...].astype(F32)
        win[pl.ds(0, HALO_C), :] = jnp.where(i == 0, 0.0, prev)
        win[pl.ds(HALO_C, tm), :] = gc_ref[...].astype(F32) * hh_ref[...].astype(F32)
        c1 = jnp.zeros((tm, D), F32)
        for k in range(W):
            c1 = c1 + cw_ref[k:k + 1, :] * win[pl.ds(HALO_C - (W - 1) + k, tm), :]
        y_ref[...] = (gb_ref[...].astype(F32) * c1).astype(BF16)

    cur = lambda c: pl.BlockSpec((tm, D), lambda i, c=c: (i, c))
    prv = lambda c: pl.BlockSpec((HALO_C, D), lambda i, c=c: (jnp.maximum(i * hb - 1, 0), c))
    return _pallas(body, name="odd_mid_fwd", grid=(S // tm,),
                   in_specs=[cur(0), cur(1), cur(2), prv(1), prv(2), pl.BlockSpec((8, D), lambda i: (0, 0))],
                   out_specs=pl.BlockSpec((tm, D), lambda i: (i, 0)), out_shape=_sds((S, D), BF16),
                   scratch_shapes=[pltpu.VMEM((tm + HALO_C, D), F32)],
                   compiler_params=_cp(("parallel",)))(z, z, z, z, z, cw)


def _odd_mid_bwd(dy, z, cw):
    S = z.shape[0]
    D = z.shape[1] // 3
    tm = TOK_TILE
    hb = tm // HALO_C
    nt = S // tm
    W = CONV_C_WIDTH

    def body(dy_ref, dyn_ref, gb_ref, gbn_ref, gc_ref, hh_ref, gcp_ref, hhp_ref, cw_ref, dz_ref, dcw_ref, win, dwin):
        i = pl.program_id(0)

        @pl.when(i == 0)
        def _():
            dcw_ref[...] = jnp.zeros_like(dcw_ref)

        gc = gc_ref[...].astype(F32)
        hh = hh_ref[...].astype(F32)
        prev = gcp_ref[...].astype(F32) * hhp_ref[...].astype(F32)
        win[pl.ds(0, HALO_C), :] = jnp.where(i == 0, 0.0, prev)
        win[pl.ds(HALO_C, tm), :] = gc * hh
        dyv = dy_ref[...]
        dc1 = dyv * gb_ref[...].astype(F32)
        dwin[pl.ds(0, tm), :] = dc1
        dwin[pl.ds(tm, HALO_C), :] = jnp.where(i == nt - 1, 0.0, dyn_ref[...] * gbn_ref[...].astype(F32))
        c1 = jnp.zeros((tm, D), F32)
        dc0 = jnp.zeros((tm, D), F32)
        for k in range(W):
            tap = win[pl.ds(HALO_C - (W - 1) + k, tm), :]
            c1 = c1 + cw_ref[k:k + 1, :] * tap
            dc0 = dc0 + cw_ref[k:k + 1, :] * dwin[pl.ds(W - 1 - k, tm), :]
            dcw_ref[k:k + 1, :] += jnp.sum(dc1 * tap, axis=0, keepdims=True)
        dz_ref[:, 0:D] = (dyv * c1).astype(BF16)
        dz_ref[:, D:2 * D] = (dc0 * hh).astype(BF16)
        dz_ref[:, 2 * D:3 * D] = (dc0 * gc).astype(BF16)

    cur = lambda c: pl.BlockSpec((tm, D), lambda i, c=c: (i, c))
    prv = lambda c: pl.BlockSpec((HALO_C, D), lambda i, c=c: (jnp.maximum(i * hb - 1, 0), c))
    nxt = pl.BlockSpec((HALO_C, D), lambda i: (jnp.minimum((i + 1) * hb, S // HALO_C - 1), 0))
    return _pallas(body, name="odd_mid_bwd", grid=(nt,),
                   in_specs=[cur(0), nxt, cur(0), nxt, cur(1), cur(2), prv(1), prv(2), pl.BlockSpec((8, D), lambda i: (0, 0))],
                   out_specs=[pl.BlockSpec((tm, 3 * D), lambda i: (i, 0)), pl.BlockSpec((8, D), lambda i: (0, 0))],
                   out_shape=[_sds((S, 3 * D), BF16), _sds((8, D), F32)],
                   scratch_shapes=[pltpu.VMEM((tm + HALO_C, D), F32), pltpu.VMEM((tm + HALO_C, D), F32)],
                   compiler_params=_cp(("arbitrary",)))(dy, dy, z, z, z, z, z, z, cw)


def _loss_head(y, tgt):
    S, D = y.shape
    tm = TOK_TILE

    def body(y_ref, t_ref, dy_ref, l_ref):
        @pl.when(pl.program_id(0) == 0)
        def _():
            l_ref[...] = jnp.zeros_like(l_ref)

        e = y_ref[...] - t_ref[...]
        dy_ref[...] = e * (1.0 / D)
        l_ref[...] += jnp.sum(jnp.sum(e * e, axis=-1, keepdims=True), axis=0, keepdims=True) * (0.5 / D)

    row = pl.BlockSpec((tm, D), lambda i: (i, 0))
    return _pallas(body, name="loss_head", grid=(S // tm,), in_specs=[row, row],
                   out_specs=[row, pl.BlockSpec((1, 1), lambda i: (0, 0))],
                   out_shape=[_sds((S, D), F32), _sds((1, 1), F32)],
                   compiler_params=_cp(("arbitrary",)))(y, tgt)


def _pad_rows(a, rows):
    return jnp.pad(a, ((0, rows - a.shape[0]), (0, 0)))


def _local_step(x, tgt, W, need=lambda block, after: None, done=lambda block, block_grads: None):
    S, D = x.shape
    grads = {}
    saved = {}

    def gain_after(gain, token):
        return gain if token is None else gain + token

    def ffn_f(tag, l, xin):
        need((tag, l), xin)
        out, xn, G, U = _ffn_fwd(xin, W[tag + "_norm"][l:l + 1], W[tag + "_w_gate"][l], W[tag + "_w_up"][l],
                                 W[tag + "_w_down"][l])
        saved[(tag, l)] = (xin, xn, G, U)
        return out

    def ffn_b(tag, l, dout):
        xin, xn, G, U = saved[(tag, l)]
        keys = [(tag + "_w_gate", l), (tag + "_w_up", l), (tag + "_w_down", l)]
        *dws, dG, dU = _ffn_bwd_w(dout, xn, G, U, W[tag + "_w_down"][l])
        big = dict(zip(keys, dws))
        grads.update(big)
        token = done((tag, l), big)
        dx, dg = _norm_in_bwd([dG, dU], [W[tag + "_w_gate"][l], W[tag + "_w_up"][l]], xin,
                              gain_after(W[tag + "_norm"][l:l + 1], token), dout, w_rows=True)
        grads[(tag + "_norm", l)] = dg
        return dx

    x0a = ffn_f("ffn1", 0, x)
    need(("ev", 0), x0a)
    w_in = W["ev_w_in"]
    w_main, w_f = w_in[:, :2560], jnp.pad(w_in[:, 2560:], ((0, 0), (0, 120)))
    h0, z0, fl = _norm_proj(x0a, W["mix_norm"][0:1], w_main, w_f)
    cw_a = _pad_rows(W["ev_conv_w"], 32)
    a_act, a1 = _conv_a_fwd(z0, cw_a, W["ev_conv_b"], W["ev_conv_norm"])
    flb, Fc = _forget_scan(fl, jnp.pad(W["ev_b_f"], ((0, 0), (0, 120))))
    qw2, kw2 = jnp.tile(W["ev_q_norm"], (1, 2)), jnp.tile(W["ev_k_norm"], (1, 2))
    q_aug, k_aug, v_aug = _qkv_prep(z0, Fc, qw2, kw2)
    o_aug, q_lse = _fox_fwd(q_aug, k_aug, v_aug)
    w_out_e = W["ev_w_out"]
    w_out_o = jnp.pad(w_out_e[D_CONV:].reshape(N_HEADS, HEAD_DIM, D), ((0, 0), (0, AUG - HEAD_DIM), (0, 0)))
    x0b = _proj_res_heads(a_act, w_out_e[:D_CONV], o_aug, w_out_o, x0a)
    x0c = ffn_f("ffn2", 0, x0b)
    x1a = ffn_f("ffn1", 1, x0c)
    need(("od", 0), x1a)
    h1, z1 = _norm_proj(x1a, W["mix_norm"][1:2], W["od_w_in"])
    cw_c = _pad_rows(W["od_conv_w"], 8)
    y1 = _odd_mid_fwd(z1, cw_c)
    x1b = _proj_res([y1], [W["od_w_out"]], x1a)
    x1c = ffn_f("ffn2", 1, x1b)
    dy, loss = _loss_head(x1c, tgt)

    d = ffn_b("ffn2", 1, dy)
    dy1 = _matmul_nt(d, W["od_w_out"])
    grads[("od_w_out", 0)] = _matmul_tn(y1, d, D)[0]
    dz1, dcw_c = _odd_mid_bwd(dy1, z1, cw_c)
    grads[("od_conv_w", 0)] = dcw_c[:CONV_C_WIDTH]
    grads[("od_w_in", 0)] = _matmul_tn(h1, dz1, 3 * D // 4)
    token = done(("od", 0), {k: grads[k] for k in (("od_w_out", 0), ("od_w_in", 0))})
    d, dg = _norm_in_bwd([dz1[None]], [W["od_w_in"][None]], x1a, gain_after(W["mix_norm"][1:2], token), d)
    grads[("mix_norm", 1)] = dg
    d = ffn_b("ffn1", 1, d)
    d = ffn_b("ffn2", 0, d)
    dcat = _matmul_nt(d, w_out_e)
    grads[("ev_w_out", 0)] = jnp.concatenate([_matmul_tn(a_act, d, D)[0],
                                              _heads_tn(o_aug, d)[:, :HEAD_DIM].reshape(D_ATTN, D)], axis=0)
    duz, dcw_a, dcb, dcn = _conv_a_bwd(dcat, a1, z0, cw_a, W["ev_conv_norm"])
    grads[("ev_conv_w", 0)] = dcw_a[:CONV_A_WIDTH]
    grads[("ev_conv_b", 0)] = dcb
    grads[("ev_conv_norm", 0)] = dcn
    dq_a, dk_a, dv_a = _fox_bwd(q_lse, k_aug, v_aug, _do_prep(dcat, o_aug))
    dqf, dkf, dvf, dF, dqw, dkw = _qkv_bwd(dq_a, dk_a, dv_a, z0, qw2, kw2)
    grads[("ev_q_norm", 0)] = dqw[:, :HEAD_DIM] + dqw[:, HEAD_DIM:]
    grads[("ev_k_norm", 0)] = dkw[:, :HEAD_DIM] + dkw[:, HEAD_DIM:]
    dfl, dbf = _forget_scan_bwd(dF, flb)
    grads[("ev_b_f", 0)] = dbf[:, :N_HEADS]
    dz0 = jnp.concatenate([duz, dqf, dkf, dvf], axis=1)
    dflb = dfl.astype(BF16)
    gmain = _matmul_tn(h0, dz0, 640)
    gmain = gmain.transpose(1, 0, 2).reshape(D, 2560)
    gf = _matmul_tn(h0, dflb, 128)[0][:, :N_HEADS]
    grads[("ev_w_in", 0)] = jnp.concatenate([gmain, gf], axis=1)
    token = done(("ev", 0), {k: grads[k] for k in (("ev_w_out", 0), ("ev_w_in", 0))})
    d, dg = _norm_in_bwd([dz0[None], dflb[None]], [w_main[None], w_f[None]], x0a, gain_after(W["mix_norm"][0:1], token), d)
    grads[("mix_norm", 0)] = dg
    d = ffn_b("ffn1", 0, d)
    return loss, d, grads


def _place():
    x, y, c = lax.axis_index("x"), lax.axis_index("y"), lax.axis_index("c")
    chips = [(1 - x, y), (x, 1 - y), (1 - x, 1 - y)]
    return x, y, c, chips


def _remote(src, dst, send_sem, recv_sem, to):
    return pltpu.make_async_remote_copy(src_ref=src, dst_ref=dst, send_sem=send_sem, recv_sem=recv_sem,
                                        device_id=to, device_id_type=MESH)


HBM = pl.BlockSpec(memory_space=pltpu.HBM)
SEM = pl.BlockSpec(memory_space=pltpu.SEMAPHORE)
EFFECT = pltpu.SideEffectType.DATAFLOW_SIDE_EFFECTING


def _in_hbm(a):
    return pltpu.with_memory_space_constraint(a, pltpu.HBM)


def _ag_start(bufs):
    n = len(bufs)

    def body(*refs):
        send_sems, recv_sems = refs[n], refs[n + 1]
        outs = refs[n + 2:]
        x, y, c, chips = _place()
        me = 2 * x + y
        for a in [n - 1] + list(range(n - 1)):
            if a == n - 1:
                blk = outs[a].at[me]
            else:
                h = outs[a].shape[1] // 2
                blk = outs[a].at[me, pl.ds(c * h, h)]
            for jj, (px, py) in enumerate(chips):
                _remote(blk, blk, send_sems.at[3 * a + jj], recv_sems.at[3 * a + jj], (px, py, c)).start()

    return _pallas(
        body, name="gather_start",
        out_shape=[pltpu.SemaphoreType.DMA((3 * n,)), pltpu.SemaphoreType.DMA((3 * n,))] + [pltpu.HBM(b.shape, b.dtype) for b in bufs],
        in_specs=[HBM] * n, out_specs=[SEM, SEM] + [HBM] * n, input_output_aliases={a: 2 + a for a in range(n)},
        compiler_params=pltpu.CompilerParams(has_side_effects=EFFECT),
    )(*[_in_hbm(b) for b in bufs])


def _ag_mid(g, ici_send, ici_recv, bufs, idx, taps, n_big, after):
    n = len(bufs)
    arrs = list(bufs) + ([taps] if taps is not None else [])
    m = len(arrs)

    def body(*refs):
        ici_s, ici_r = refs[0], refs[1]
        d_send, d_recv = refs[m + 3], refs[m + 4]
        outs = refs[m + 5:]
        x, y, c, chips = _place()
        me = 2 * x + y
        for i in range(m):
            a = idx[i] if i < n else n_big
            for jj, (px, py) in enumerate(chips):
                k = 3 * a + jj
                if i < n:
                    h = outs[i].shape[1] // 2
                    mine, blk = outs[i].at[me, pl.ds(c * h, h)], outs[i].at[2 * px + py, pl.ds(c * h, h)]
                else:
                    mine, blk = outs[i].at[me], outs[i].at[2 * px + py]
                _remote(mine, mine, ici_s.at[k], ici_r.at[k], (px, py, c)).wait_send()
                _remote(blk, blk, ici_s.at[k], ici_r.at[k], (px, py, c)).wait_recv()
                if i < n:
                    _remote(blk, blk, d_send.at[3 * i + jj], d_recv.at[3 * i + jj], (x, y, 1 - c)).start()

    return _pallas(
        body, name=f"gather_pass_on_{g}",
        out_shape=[pltpu.SemaphoreType.DMA((3 * n,)), pltpu.SemaphoreType.DMA((3 * n,))] + [pltpu.HBM(b.shape, b.dtype) for b in arrs],
        in_specs=[SEM, SEM] + [HBM] * m + [ANY], out_specs=[SEM, SEM] + [HBM] * m,
        input_output_aliases={2 + i: 2 + i for i in range(m)},
        compiler_params=pltpu.CompilerParams(has_side_effects=EFFECT),
    )(ici_send, ici_recv, *arrs, after)


def _ag_wait(g, d_send, d_recv, arrs, n, after):
    m = len(arrs)

    def body(*refs):
        d_s, d_r = refs[0], refs[1]
        outs = refs[m + 3:]
        x, y, c, chips = _place()
        for i in range(n):
            h = outs[i].shape[1] // 2
            for jj, (px, py) in enumerate(chips):
                sent = outs[i].at[2 * px + py, pl.ds(c * h, h)]
                got = outs[i].at[2 * px + py, pl.ds((1 - c) * h, h)]
                _remote(sent, sent, d_s.at[3 * i + jj], d_r.at[3 * i + jj], (x, y, 1 - c)).wait_send()
                _remote(got, got, d_s.at[3 * i + jj], d_r.at[3 * i + jj], (x, y, 1 - c)).wait_recv()

    return _pallas(
        body, name=f"gather_wait_{g}", out_shape=[pltpu.HBM(b.shape, b.dtype) for b in arrs],
        in_specs=[SEM, SEM] + [HBM] * m + [ANY], out_specs=[HBM] * m,
        input_output_aliases={2 + i: i for i in range(m)},
        compiler_params=pltpu.CompilerParams(has_side_effects=EFFECT),
    )(d_send, d_recv, *arrs, after)


def _pair_start(g, gs, after):
    n = len(gs)
    zones = [lax.empty((4, a.shape[1] // 2, a.shape[2]), a.dtype) for a in gs]
    extra = [] if after is None else [after]

    def body(*refs):
        k0 = 2 * n + len(extra)
        send_sems, recv_sems = refs[k0], refs[k0 + 1]
        src, dst = refs[k0 + 2:k0 + 2 + n], refs[k0 + 2 + n:k0 + 2 + 2 * n]
        token = refs[k0 + 2 + 2 * n]
        x, y, c, _ = _place()
        for a in range(n):
            h = src[a].shape[1] // 2
            _remote(src[a].at[:, pl.ds((1 - c) * h, h)], dst[a], send_sems.at[a], recv_sems.at[a], (x, y, 1 - c)).start()
        token[...] = jnp.zeros_like(token)

    return _pallas(
        body, name=f"grad_pair_start_{g}",
        out_shape=[pltpu.SemaphoreType.DMA((n,)), pltpu.SemaphoreType.DMA((n,))]
        + [pltpu.HBM(a.shape, a.dtype) for a in gs + zones] + [_sds((8, 128), F32)],
        in_specs=[HBM] * (2 * n) + [ANY] * len(extra),
        out_specs=[SEM, SEM] + [HBM] * (2 * n) + [pl.BlockSpec(memory_space=pltpu.VMEM)],
        input_output_aliases={i: 2 + i for i in range(2 * n)},
        compiler_params=pltpu.CompilerParams(has_side_effects=EFFECT),
    )(*[_in_hbm(a) for a in gs + zones], *extra)


def _pair_wait(g, send, recv, gs, zones):
    n = len(gs)

    def body(*refs):
        s_ref, r_ref = refs[0], refs[1]
        outs = refs[2 + 2 * n:]
        src, dst = outs[:n], outs[n:]
        x, y, c, _ = _place()
        for a in range(n):
            h = src[a].shape[1] // 2
            _remote(src[a].at[:, pl.ds((1 - c) * h, h)], dst[a], s_ref.at[a], r_ref.at[a], (x, y, 1 - c)).wait()

    return _pallas(
        body, name=f"grad_pair_wait_{g}", out_shape=[pltpu.HBM(a.shape, a.dtype) for a in gs + zones],
        in_specs=[SEM, SEM] + [HBM] * (2 * n), out_specs=[HBM] * (2 * n),
        input_output_aliases={2 + i: i for i in range(2 * n)},
        compiler_params=pltpu.CompilerParams(has_side_effects=EFFECT),
    )(send, recv, *gs, *zones)


def _pair_add(gs, others, c_arr):
    n = len(gs)

    def body(c_ref, *refs):
        for g_ref, o_ref, out_ref in zip(refs[:n], refs[n:2 * n], refs[2 * n:]):
            out_ref[...] = (g_ref[...].astype(F32) + o_ref[...].astype(F32)).astype(BF16)

    half = lambda a: pl.BlockSpec((1, a.shape[1] // 2, a.shape[2]), lambda k, c_ref: (k, c_ref[0], 0))
    whole = lambda a: pl.BlockSpec((1,) + a.shape[1:], lambda k, c_ref: (k, 0, 0))
    grid_spec = pltpu.PrefetchScalarGridSpec(
        num_scalar_prefetch=1, grid=(4,), in_specs=[half(a) for a in gs] + [whole(o) for o in others],
        out_specs=[whole(o) for o in others])
    return _pallas(body, name="grad_pair_add", grid_spec=grid_spec, out_shape=[_sds(o.shape, BF16) for o in others],
                   compiler_params=_cp(("parallel",)))(c_arr, *gs, *others)


def _chip_start(g, ss):
    n = len(ss)
    zones = [lax.empty((3,) + s.shape[1:], s.dtype) for s in ss]

    def body(*refs):
        send_sems, recv_sems = refs[2 * n], refs[2 * n + 1]
        src, dst = refs[2 * n + 2:3 * n + 2], refs[3 * n + 2:4 * n + 2]
        token = refs[4 * n + 2]
        x, y, c, chips = _place()
        for a in range(n):
            for jj, (px, py) in enumerate(chips):
                k = 3 * a + jj
                _remote(src[a].at[2 * px + py], dst[a].at[jj], send_sems.at[k], recv_sems.at[k], (px, py, c)).start()
        token[...] = jnp.zeros_like(token)

    return _pallas(
        body, name=f"grad_chip_start_{g}",
        out_shape=[pltpu.SemaphoreType.DMA((3 * n,)), pltpu.SemaphoreType.DMA((3 * n,))]
        + [pltpu.HBM(a.shape, a.dtype) for a in ss + zones] + [_sds((8, 128), F32)],
        in_specs=[HBM] * (2 * n), out_specs=[SEM, SEM] + [HBM] * (2 * n) + [pl.BlockSpec(memory_space=pltpu.VMEM)],
        input_output_aliases={i: 2 + i for i in range(2 * n)},
        compiler_params=pltpu.CompilerParams(has_side_effects=EFFECT),
    )(*[_in_hbm(a) for a in ss + zones])


def _chip_wait(sends, recvs, counts, ss, zones, after):
    nb, n = len(sends), len(ss)

    def body(*refs):
        s_refs, r_refs = refs[:nb], refs[nb:2 * nb]
        outs = refs[2 * nb + 2 * n + 1:]
        src, dst = outs[:n], outs[n:]
        x, y, c, chips = _place()
        a = 0
        for b in range(nb):
            for i in range(counts[b]):
                for jj, (px, py) in enumerate(chips):
                    k = 3 * i + jj
                    _remote(src[a].at[2 * px + py], dst[a].at[jj], s_refs[b].at[k], r_refs[b].at[k], (px, py, c)).wait()
                a += 1

    return _pallas(
        body, name="grad_chip_wait", out_shape=[pltpu.HBM(a.shape, a.dtype) for a in ss + zones],
        in_specs=[SEM] * (2 * nb) + [HBM] * (2 * n) + [ANY], out_specs=[HBM] * (2 * n),
        input_output_aliases={2 * nb + i: i for i in range(2 * n)},
        compiler_params=pltpu.CompilerParams(has_side_effects=EFFECT),
    )(*sends, *recvs, *ss, *zones, after)


def _chip_sum(s, r, where, dest, l, L):
    _, h, C = s.shape
    tr = h // 2

    def body(k_ref, s_ref, r_ref, *rest):
        out_ref = rest[-1]
        acc = s_ref[0].astype(F32)
        for jj in range(3):
            acc = acc + r_ref[jj].astype(F32)
        out_ref[...] = acc

    in_specs = [pl.BlockSpec((1, tr, C), lambda i, k_ref: (k_ref[0], i, 0)), pl.BlockSpec((3, tr, C), lambda i, k_ref: (0, i, 0))]
    args = [where, s, r]
    alias = {}
    if dest is not None:
        in_specs.append(ANY)
        args.append(dest)
        alias = {3: 0}
    grid_spec = pltpu.PrefetchScalarGridSpec(
        num_scalar_prefetch=1, grid=(2,), in_specs=in_specs,
        out_specs=pl.BlockSpec((None, tr, C), lambda i, k_ref: (l, 2 * k_ref[1] + i, 0)))
    return _pallas(body, name="grad_chip_sum", grid_spec=grid_spec, out_shape=_sds((L, 2 * h, C), F32),
                   input_output_aliases=alias, compiler_params=_cp(("arbitrary",)))(*args)


def _pair_share(bufs, layout):
    n = len(layout)
    n_out = len(bufs)

    def body(*refs):
        outs = refs[n_out:2 * n_out]
        send_sems, recv_sems = refs[2 * n_out:]
        x, y, c, _ = _place()
        cps = []
        for a, (o, l) in enumerate(layout):
            h = outs[o].shape[1] // 2
            blk = outs[o].at[l, pl.ds(c * h, h)]
            cps.append(_remote(blk, blk, send_sems.at[a], recv_sems.at[a], (x, y, 1 - c)))
        for cp in cps:
            cp.start()
        for a, (o, l) in enumerate(layout):
            h = outs[o].shape[1] // 2
            blk = outs[o].at[l, pl.ds((1 - c) * h, h)]
            _remote(blk, blk, send_sems.at[a], recv_sems.at[a], (x, y, 1 - c)).wait_recv()
        for cp in cps:
            cp.wait_send()

    return _pallas(body, name="grad_pair_share", in_specs=[ANY] * n_out, out_specs=[ANY] * n_out,
                   out_shape=[_sds(b.shape, b.dtype) for b in bufs], input_output_aliases={o: o for o in range(n_out)},
                   scratch_shapes=[pltpu.SemaphoreType.DMA((n,)), pltpu.SemaphoreType.DMA((n,))])(*bufs)


def _small_all_reduce(packed):
    P, L = packed.shape

    def body(in_ref, out_ref, slots, send_sems, recv_sems):
        x, y, c, _ = _place()
        me = 4 * x + 2 * y + c
        slots[me] = in_ref[...]
        cps = []
        for r in range(1, 8):
            px = 1 - x if r & 4 else x
            py = 1 - y if r & 2 else y
            pc = 1 - c if r & 1 else c
            cps.append(_remote(in_ref, slots.at[me], send_sems.at[r - 1], recv_sems.at[r - 1], (px, py, pc)))
        for cp in cps:
            cp.start()
        for r in range(1, 8):
            px = 1 - x if r & 4 else x
            py = 1 - y if r & 2 else y
            pc = 1 - c if r & 1 else c
            blk = slots.at[4 * px + 2 * py + pc]
            _remote(blk, blk, send_sems.at[r - 1], recv_sems.at[r - 1], (px, py, pc)).wait_recv()
        for cp in cps:
            cp.wait_send()
        acc = slots[0]
        for k in range(1, 8):
            acc = acc + slots[k]
        out_ref[...] = acc

    vm = pl.BlockSpec(memory_space=pltpu.VMEM)
    return _pallas(body, name="small_all_reduce", in_specs=[vm], out_specs=vm, out_shape=_sds((P, L), F32),
                   scratch_shapes=[pltpu.VMEM((8, P, L), F32), pltpu.SemaphoreType.DMA((7,)), pltpu.SemaphoreType.DMA((7,))])(packed)


def _adamw_math(w, g, m, v):
    m = ADAM_B1 * m + (1.0 - ADAM_B1) * g
    v = ADAM_B2 * v + (1.0 - ADAM_B2) * (g * g)
    m_hat = m / (1.0 - ADAM_B1 ** ADAM_STEP)
    v_hat = v / (1.0 - ADAM_B2 ** ADAM_STEP)
    delta = -ADAM_LR * (m_hat / (jnp.sqrt(v_hat) + ADAM_EPS) + ADAM_WD * w)
    return delta, m, v


def _adamw(w, g, m, v):
    shape = w.shape
    C = shape[-1]
    rows = math.prod(shape[:-1])
    tr = next(t for t in (512, 352, 256, 128, 64, 32, 16, 8, rows) if rows % t == 0)
    w2, g2, m2, v2 = (a.reshape(rows, C) for a in (w, g, m, v))

    def body(w_ref, g_ref, m_ref, v_ref, d_ref, nm_ref, nv_ref):
        d, nm, nv = _adamw_math(w_ref[...], g_ref[...], m_ref[...], v_ref[...])
        d_ref[...] = d
        nm_ref[...] = nm
        nv_ref[...] = nv

    blk = pl.BlockSpec((tr, C), lambda i: (i, 0))
    outs = _pallas(body, name="adamw", grid=(rows // tr,), in_specs=[blk] * 4, out_specs=[blk] * 3,
                   out_shape=[_sds((rows, C), F32)] * 3, compiler_params=_cp(("parallel",)))(w2, g2, m2, v2)
    return tuple(o.reshape(shape) for o in outs)


WEIGHTS = ["ffn1_norm", "ffn1_w_gate", "ffn1_w_up", "ffn1_w_down", "mix_norm", "ffn2_norm", "ffn2_w_gate", "ffn2_w_up",
           "ffn2_w_down", "ev_w_in", "ev_b_f", "ev_conv_w", "ev_conv_b", "ev_conv_norm", "ev_q_norm", "ev_k_norm",
           "ev_w_out", "od_w_in", "od_conv_w", "od_w_out"]
BIG = ([("ffn1_w_gate", 0), ("ffn1_w_up", 0), ("ffn1_w_down", 0), ("ev_w_in", 0), ("ev_w_out", 0),
        ("ffn2_w_gate", 0), ("ffn2_w_up", 0), ("ffn2_w_down", 0)]
       + [("ffn1_w_gate", 1), ("ffn1_w_up", 1), ("ffn1_w_down", 1), ("od_w_in", 0), ("od_w_out", 0),
          ("ffn2_w_gate", 1), ("ffn2_w_up", 1), ("ffn2_w_down", 1)])
TRANSPOSED = ("ffn1_w_gate", "ffn1_w_up", "ffn2_w_gate", "ffn2_w_up")
BLOCKS = [("ffn1", 0), ("ev", 0), ("ffn2", 0), ("ffn1", 1), ("od", 0), ("ffn2", 1)]
BLOCK_OF = {(name, l): (name.split("_w_")[0], l) for name, l in BIG}
BIG_NAMES = ["ffn1_w_gate", "ffn1_w_up", "ffn1_w_down", "ffn2_w_gate", "ffn2_w_up", "ffn2_w_down",
             "ev_w_in", "ev_w_out", "od_w_in", "od_w_out"]
SMALL = [("ffn1_norm", 16), ("mix_norm", 16), ("ffn2_norm", 16), ("ev_b_f", 8), ("ev_conv_w", 128), ("ev_conv_b", 8),
         ("ev_conv_norm", 8), ("ev_q_norm", 8), ("ev_k_norm", 8), ("od_conv_w", 24)]


def _to_lanes(a, rows):
    flat = a.reshape(-1)
    return jnp.pad(flat, (0, rows * 128 - flat.shape[0])).reshape(rows, 128)


def kernel(x, ffn1_norm, ffn1_w_gate, ffn1_w_up, ffn1_w_down, mix_norm, ffn2_norm, ffn2_w_gate, ffn2_w_up, ffn2_w_down, ev_w_in, ev_b_f, ev_conv_w, ev_conv_b, ev_conv_norm, ev_q_norm, ev_k_norm, ev_w_out, od_w_in, od_conv_w, od_w_out, loss_target, m_ffn1_norm, m_ffn1_w_gate, m_ffn1_w_up, m_ffn1_w_down, m_mix_norm, m_ffn2_norm, m_ffn2_w_gate, m_ffn2_w_up, m_ffn2_w_down, m_ev_w_in, m_ev_b_f, m_ev_conv_w, m_ev_conv_b, m_ev_conv_norm, m_ev_q_norm, m_ev_k_norm, m_ev_w_out, m_od_w_in, m_od_conv_w, m_od_w_out, v_ffn1_norm, v_ffn1_w_gate, v_ffn1_w_up, v_ffn1_w_down, v_mix_norm, v_ffn2_norm, v_ffn2_w_gate, v_ffn2_w_up, v_ffn2_w_down, v_ev_w_in, v_ev_b_f, v_ev_conv_w, v_ev_conv_b, v_ev_conv_norm, v_ev_q_norm, v_ev_k_norm, v_ev_w_out, v_od_w_in, v_od_conv_w, v_od_w_out):
    P = dict(ffn1_norm=ffn1_norm, ffn1_w_gate=ffn1_w_gate, ffn1_w_up=ffn1_w_up, ffn1_w_down=ffn1_w_down, mix_norm=mix_norm,
             ffn2_norm=ffn2_norm, ffn2_w_gate=ffn2_w_gate, ffn2_w_up=ffn2_w_up, ffn2_w_down=ffn2_w_down, ev_w_in=ev_w_in,
             ev_b_f=ev_b_f, ev_conv_w=ev_conv_w, ev_conv_b=ev_conv_b, ev_conv_norm=ev_conv_norm, ev_q_norm=ev_q_norm,
             ev_k_norm=ev_k_norm, ev_w_out=ev_w_out, od_w_in=od_w_in, od_conv_w=od_conv_w, od_w_out=od_w_out)
    M = dict(zip(WEIGHTS, [m_ffn1_norm, m_ffn1_w_gate, m_ffn1_w_up, m_ffn1_w_down, m_mix_norm, m_ffn2_norm, m_ffn2_w_gate,
                           m_ffn2_w_up, m_ffn2_w_down, m_ev_w_in, m_ev_b_f, m_ev_conv_w, m_ev_conv_b, m_ev_conv_norm,
                           m_ev_q_norm, m_ev_k_norm, m_ev_w_out, m_od_w_in, m_od_conv_w, m_od_w_out]))
    V = dict(zip(WEIGHTS, [v_ffn1_norm, v_ffn1_w_gate, v_ffn1_w_up, v_ffn1_w_down, v_mix_norm, v_ffn2_norm, v_ffn2_w_gate,
                           v_ffn2_w_up, v_ffn2_w_down, v_ev_w_in, v_ev_b_f, v_ev_conv_w, v_ev_conv_b, v_ev_conv_norm,
                           v_ev_q_norm, v_ev_k_norm, v_ev_w_out, v_od_w_in, v_od_conv_w, v_od_w_out]))
    for name in TRANSPOSED:
        P[name], M[name], V[name] = (jnp.swapaxes(a, 1, 2) for a in (P[name], M[name], V[name]))
    S, D = x.shape[1], x.shape[2]
    chip = 2 * lax.axis_index("x") + lax.axis_index("y")
    core = lax.axis_index("c")

    def own_slot(shard):
        return lax.dynamic_update_slice(lax.empty((4,) + shard.shape, shard.dtype), shard[None], (chip, 0, 0))

    taps = jnp.concatenate([_to_lanes(_pad_rows(ev_conv_w[0], 32), 32), _to_lanes(_pad_rows(od_conv_w[0], 8), 16)], axis=0)
    ici_send, ici_recv, *bufs = _ag_start([own_slot(P[name][l].astype(BF16)) for name, l in BIG] + [own_slot(taps)])
    cols = lambda a: a.transpose(1, 0, 2).reshape(a.shape[1], 4 * a.shape[2])
    W = {k: P[k] for k in ("ffn1_norm", "mix_norm", "ffn2_norm", "ev_b_f", "ev_q_norm", "ev_k_norm")}
    W["ev_conv_b"], W["ev_conv_norm"] = ev_conv_b, ev_conv_norm
    for tag in ("ffn1", "ffn2"):
        for kind in ("_w_gate", "_w_up", "_w_down"):
            W[tag + kind] = [None, None]
    passing = {}

    def pass_on(g, after):
        idx = [i for i, k in enumerate(BIG) if BLOCK_OF[k] == BLOCKS[g]]
        keys = [BIG[i] for i in idx] + (["taps"] if BLOCKS[g] == ("ev", 0) else [])
        passing[g] = (keys, _ag_mid(g, ici_send, ici_recv, [bufs[i] for i in idx], idx,
                                    bufs[-1] if BLOCKS[g] == ("ev", 0) else None, len(BIG), after))

    def need(block, after):
        g = BLOCKS.index(block)
        if g not in passing:
            pass_on(g, after)
        keys, (d_send, d_recv, *thru) = passing.pop(g)
        got = dict(zip(keys, _ag_wait(g, d_send, d_recv, thru, len(keys) - ("taps" in keys), after)))
        if 1 <= g < len(BLOCKS) - 1:
            pass_on(g + 1, after)
        for key, a in got.items():
            if key == "taps":
                continue
            name, l = key
            if name.startswith("ffn"):
                W[name][l] = _in_hbm(a)
            elif name.endswith("_w_in"):
                W[name] = cols(a)
            elif name.endswith("_w_out"):
                W[name] = a.reshape(4 * a.shape[1], D)
        if block == ("ev", 0):
            taps_all = got["taps"]
            W["ev_conv_w"] = cols(taps_all[:, :32].reshape(4, 32, 128))[:CONV_A_WIDTH]
            W["od_conv_w"] = cols(taps_all[:, 32:48].reshape(4, 8, 256))[:CONV_C_WIDTH]

    rows = lambda a: a.reshape(4, a.shape[0] // 4, a.shape[1])
    colsh = lambda a: a.reshape(a.shape[0], 4, a.shape[1] // 4).transpose(1, 0, 2)
    c_arr = core.reshape(1).astype(jnp.int32)
    where = jnp.stack([chip, core]).astype(jnp.int32)
    in_flight = []

    def done(block, block_grads):
        g = BLOCKS.index(block)
        keys = list(block_grads)
        gs = []
        for name, l in keys:
            a = block_grads[(name, l)]
            gs.append(colsh(a) if name == "ev_w_in" else rows(a) if name.endswith("_w_out") else a)
        for item in list(pairs):
            to_chips(item)
        send, recv, *rest = _pair_start(g, gs, chained.get("token"))
        n = len(keys)
        pairs.append((g, keys, send, recv, rest[:n], rest[n:2 * n]))
        if g == 0:
            to_chips(pairs[0])
        chained["token"] = rest[-1] if g else chained["token"]
        return chained["token"][0:1, 0:1]

    pairs, chained = [], {}

    def to_chips(item):
        pairs.remove(item)
        g, keys, send, recv, gs, zones = item
        n = len(keys)
        done_ = _pair_wait(g, send, recv, gs, zones)
        sums = list(_pair_add(list(done_[:n]), list(done_[n:]), c_arr))
        send2, recv2, *rest = _chip_start(g, sums)
        in_flight.append((keys, send2, recv2, rest[:n], rest[n:2 * n]))
        chained["token"] = rest[-1]

    loss, grad_x, grads = _local_step(x[0], loss_target[0], W, need, done)

    order = [k for keys, *_ in in_flight for k in keys]
    landed = _chip_wait([f[1] for f in in_flight], [f[2] for f in in_flight], [len(f[0]) for f in in_flight],
                        [a for f in in_flight for a in f[3]], [a for f in in_flight for a in f[4]], grad_x)
    sums, recvd = landed[:len(order)], landed[len(order):]
    stacked = {}
    for (name, l), s, r in zip(order, sums, recvd):
        stacked[name] = _chip_sum(_in_hbm(s), _in_hbm(r), where, stacked.get(name), l, P[name].shape[0])
    layout = [(BIG_NAMES.index(name), l) for name, l in order]
    big_grads = dict(zip(BIG_NAMES, _pair_share([stacked[name] for name in BIG_NAMES], layout)))

    def small_grad(name):
        if name.endswith("_norm") and name[:3] in ("ffn", "mix"):
            return jnp.concatenate([grads[(name, 0)], grads[(name, 1)]], axis=0)
        return grads[(name, 0)]

    packed = jnp.concatenate([_to_lanes(small_grad(name), r) for name, r in SMALL], axis=0)
    total = _small_all_reduce(packed)
    small_grads, at = {}, 0
    for name, r in SMALL:
        part = total[at:at + r].reshape(-1)
        at += r
        if name == "ev_conv_w":
            full_g = part[:CONV_A_WIDTH * D_CONV].reshape(CONV_A_WIDTH, D_CONV)
            small_grads[name] = lax.dynamic_slice_in_dim(full_g, chip * (D_CONV // 4), D_CONV // 4, axis=1)[None]
        elif name == "od_conv_w":
            full_g = part[:CONV_C_WIDTH * D].reshape(CONV_C_WIDTH, D)
            small_grads[name] = lax.dynamic_slice_in_dim(full_g, chip * (D // 4), D // 4, axis=1)[None]
        else:
            small_grads[name] = part[:math.prod(P[name].shape)].reshape(P[name].shape)

    grad_w, delta_w, new_m, new_v = [], [], [], []
    for name in WEIGHTS:
        g = big_grads[name] if name in big_grads else small_grads[name]
        outs = (g,) + _adamw(P[name], g, M[name], V[name])
        if name in TRANSPOSED:
            outs = tuple(jnp.swapaxes(a, 1, 2) for a in outs)
        for acc, a in zip((grad_w, delta_w, new_m, new_v), outs):
            acc.append(a)
    loss_all = lax.psum(loss[0, 0], ("x", "y", "c"))
    return (loss_all, grad_x[None], *grad_w, *delta_w, *new_m, *new_v)
```

```python
import functools
import math

import jax
import jax.numpy as jnp
from jax import lax
from jax.experimental import pallas as pl
from jax.experimental.pallas import tpu as pltpu

F32, BF16 = jnp.float32, jnp.bfloat16
EPS = 1e-6
FFN_RES = 0.5
N_HEADS, HEAD_DIM = 8, 64
D_CONV = 512
D_ATTN = N_HEADS * HEAD_DIM
CONV_A_WIDTH, CONV_C_WIDTH = 31, 3
ADAM_LR, ADAM_B1, ADAM_B2, ADAM_EPS, ADAM_WD, ADAM_STEP = 0.001, 0.9, 0.999, 1e-08, 0.01, 10
MESH = pl.DeviceIdType.MESH
ANY = pl.BlockSpec(memory_space=pl.ANY)

TOK_TILE = 512
DW_TILE = 1024
ATT_TILE = 1024
QKN_TILE = 2048
HALO_A, HALO_C = 32, 16
SUBLANES = 8
CONV_ROWS = 64
SCAN_BLK = 256
MIB = 2 ** 20


def _pallas(body, **kw):
    return pl.pallas_call(body, **kw)


def _cp(sem=None, vmem_mib=48):
    return pltpu.CompilerParams(dimension_semantics=sem, vmem_limit_bytes=vmem_mib * MIB)


def _dot(a, b):
    return jnp.dot(a, b, preferred_element_type=F32)


def _dot_nt(a, b):
    return lax.dot_general(a, b, (((1,), (1,)), ((), ())), preferred_element_type=F32)


def _dot_tn(a, b):
    return lax.dot_general(a, b, (((0,), (0,)), ((), ())), preferred_element_type=F32)


def _sds(shape, dtype):
    return jax.ShapeDtypeStruct(shape, dtype)


def _rms(x):
    return lax.rsqrt(jnp.mean(x * x, axis=-1, keepdims=True) + EPS)


def _rms_bwd(dy, x, g):
    r = _rms(x)
    xh = x * r
    dxh = dy * g
    dx = r * (dxh - xh * jnp.mean(dxh * xh, axis=-1, keepdims=True))
    return dx, xh


def _silu_grad(z):
    s = jax.nn.sigmoid(z)
    return s * (1.0 + z * (1.0 - s))


def _ffn_fwd(x, g, wg, wu, wd):
    S, D = x.shape
    nc, Fs, _ = wd.shape
    tm = TOK_TILE

    def body(x_ref, g_ref, wg_ref, wu_ref, wd_ref, out_ref, xn_ref, G_ref, U_ref, acc_ref):
        j = pl.program_id(1)

        @pl.when(j == 0)
        def _():
            xv = x_ref[...]
            xn_ref[...] = (xv * _rms(xv) * g_ref[...]).astype(BF16)
            acc_ref[...] = jnp.zeros_like(acc_ref)

        xn = xn_ref[...]
        G = _dot_nt(xn, wg_ref[0])
        U = _dot_nt(xn, wu_ref[0])
        G_ref[0] = G.astype(BF16)
        U_ref[0] = U.astype(BF16)
        H = (G * jax.nn.sigmoid(G) * U).astype(BF16)
        acc_ref[...] += _dot(H, wd_ref[0])

        @pl.when(j == nc - 1)
        def _():
            out_ref[...] = x_ref[...] + FFN_RES * acc_ref[...]

    row = pl.BlockSpec((tm, D), lambda i, j: (i, 0))
    return _pallas(
        body, name="ffn_fwd", grid=(S // tm, nc),
        in_specs=[row, pl.BlockSpec((1, D), lambda i, j: (0, 0)),
                  pl.BlockSpec((1, Fs, D), lambda i, j: (j, 0, 0)), pl.BlockSpec((1, Fs, D), lambda i, j: (j, 0, 0)),
                  pl.BlockSpec((1, Fs, D), lambda i, j: (j, 0, 0))],
        out_specs=[row, row, pl.BlockSpec((1, tm, Fs), lambda i, j: (j, i, 0)),
                   pl.BlockSpec((1, tm, Fs), lambda i, j: (j, i, 0))],
        out_shape=[_sds((S, D), F32), _sds((S, D), BF16), _sds((nc, S, Fs), BF16), _sds((nc, S, Fs), BF16)],
        scratch_shapes=[pltpu.VMEM((tm, D), F32)],
        compiler_params=_cp(("parallel", "arbitrary")),
    )(x, g, wg, wu, wd)


def _ffn_bwd_w(dout, xn, G, U, wd):
    S, D = dout.shape
    nc, _, Fs = G.shape
    tm = min(DW_TILE, S)
    nt = S // tm
    sub = min(TOK_TILE, tm)

    def body(do_ref, xn_ref, G_ref, U_ref, wd_ref, dwg_ref, dwu_ref, dwd_ref, dG_ref, dU_ref, ag, au, ad, do_s, H_s):
        i = pl.program_id(1)

        @pl.when(i == 0)
        def _():
            ag[...] = jnp.zeros_like(ag)
            au[...] = jnp.zeros_like(au)
            ad[...] = jnp.zeros_like(ad)

        for r in range(0, tm, sub):
            rows = pl.ds(r, sub)
            do = (FFN_RES * do_ref[rows, :]).astype(BF16)
            do_s[rows, :] = do
            Gv = G_ref[0, rows, :].astype(F32)
            Uv = U_ref[0, rows, :].astype(F32)
            dH = _dot_nt(do, wd_ref[0])
            sg = jax.nn.sigmoid(Gv)
            act = Gv * sg
            H_s[rows, :] = (act * Uv).astype(BF16)
            dU_ref[0, rows, :] = (dH * act).astype(BF16)
            dG_ref[0, rows, :] = (dH * Uv * (sg * (1.0 + Gv * (1.0 - sg)))).astype(BF16)
        xnv = xn_ref[...]
        ag[...] += _dot_tn(dG_ref[0], xnv)
        au[...] += _dot_tn(dU_ref[0], xnv)
        ad[...] += _dot_tn(H_s[...], do_s[...])

        @pl.when(i == nt - 1)
        def _():
            dwg_ref[0] = ag[...].astype(BF16)
            dwu_ref[0] = au[...].astype(BF16)
            dwd_ref[0] = ad[...].astype(BF16)

    row = pl.BlockSpec((tm, D), lambda j, i: (i, 0))
    hid = pl.BlockSpec((1, tm, Fs), lambda j, i: (j, i, 0))
    wrow = pl.BlockSpec((1, Fs, D), lambda j, i: (j, 0, 0))
    return _pallas(
        body, name="ffn_bwd_w", grid=(nc, nt),
        in_specs=[row, row, hid, hid, wrow],
        out_specs=[wrow, wrow, wrow, hid, hid],
        out_shape=[_sds((nc, Fs, D), BF16)] * 3 + [_sds((nc, S, Fs), BF16)] * 2,
        scratch_shapes=[pltpu.VMEM((Fs, D), F32)] * 3 + [pltpu.VMEM((tm, D), BF16), pltpu.VMEM((tm, Fs), BF16)],
        compiler_params=_cp(("parallel", "arbitrary"), 56),
    )(dout, xn, G, U, wd)


def _norm_in_bwd(dzs, ws, x, g, dres, w_rows=False):
    S, D = x.shape
    nc = dzs[0].shape[0]
    n = len(dzs)
    tm = TOK_TILE

    def body(*refs):
        dz_refs, w_refs = refs[:n], refs[n:2 * n]
        x_ref, g_ref, dres_ref, dx_ref, dg_ref, acc_ref = refs[2 * n:]
        i, j = pl.program_id(0), pl.program_id(1)

        @pl.when(j == 0)
        def _():
            acc_ref[...] = jnp.zeros_like(acc_ref)

        @pl.when((i == 0) & (j == 0))
        def _():
            dg_ref[...] = jnp.zeros_like(dg_ref)

        for dz_ref, w_ref in zip(dz_refs, w_refs):
            acc_ref[...] += _dot(dz_ref[0], w_ref[0]) if w_rows else _dot_nt(dz_ref[0], w_ref[0])

        @pl.when(j == nc - 1)
        def _():
            dxn = acc_ref[...]
            dx, xh = _rms_bwd(dxn, x_ref[...], g_ref[...])
            dx_ref[...] = dx + dres_ref[...]
            dg_ref[...] += jnp.sum(dxn * xh, axis=0, keepdims=True)

    row = pl.BlockSpec((tm, D), lambda i, j: (i, 0))
    one = pl.BlockSpec((1, D), lambda i, j: (0, 0))
    in_specs = [pl.BlockSpec((1, tm, dz.shape[2]), lambda i, j: (j, i, 0)) for dz in dzs]
    in_specs += [pl.BlockSpec((1,) + w.shape[1:], lambda i, j: (j, 0, 0)) for w in ws]
    return _pallas(
        body, name="norm_in_bwd", grid=(S // tm, nc),
        in_specs=in_specs + [row, one, row], out_specs=[row, one],
        out_shape=[_sds((S, D), F32), _sds((1, D), F32)],
        scratch_shapes=[pltpu.VMEM((tm, D), F32)],
        compiler_params=_cp(("arbitrary", "arbitrary")),
    )(*dzs, *ws, x, g, dres)


def _norm_proj(x, g, w, w2=None):
    S, D = x.shape
    N = w.shape[1]
    tm = TOK_TILE

    def body(*refs):
        if w2 is None:
            x_ref, g_ref, w_ref, h_ref, z_ref = refs
        else:
            x_ref, g_ref, w_ref, w2_ref, h_ref, z_ref, z2_ref = refs
        xv = x_ref[...]
        h = (xv * _rms(xv) * g_ref[...]).astype(BF16)
        h_ref[...] = h
        z_ref[...] = _dot(h, w_ref[...]).astype(BF16)
        if w2 is not None:
            z2_ref[...] = _dot(h, w2_ref[...])

    row = pl.BlockSpec((tm, D), lambda i: (i, 0))
    in_specs = [row, pl.BlockSpec((1, D), lambda i: (0, 0)), pl.BlockSpec((D, N), lambda i: (0, 0))]
    out_specs = [row, pl.BlockSpec((tm, N), lambda i: (i, 0))]
    out_shape = [_sds((S, D), BF16), _sds((S, N), BF16)]
    args = [x, g, w]
    if w2 is not None:
        N2 = w2.shape[1]
        in_specs.append(pl.BlockSpec((D, N2), lambda i: (0, 0)))
        out_specs.append(pl.BlockSpec((tm, N2), lambda i: (i, 0)))
        out_shape.append(_sds((S, N2), F32))
        args.append(w2)
    return _pallas(body, name="norm_proj", grid=(S // tm,), in_specs=in_specs, out_specs=out_specs,
                   out_shape=out_shape, compiler_params=_cp(("parallel",)))(*args)


def _proj_res(acts, ws, res):
    S, D = res.shape
    n = len(acts)
    tm = TOK_TILE

    def body(*refs):
        a_refs, w_refs = refs[:n], refs[n:2 * n]
        res_ref, out_ref = refs[2 * n:]
        acc = res_ref[...]
        for a_ref, w_ref in zip(a_refs, w_refs):
            acc = acc + _dot(a_ref[...], w_ref[...])
        out_ref[...] = acc

    row = pl.BlockSpec((tm, D), lambda i: (i, 0))
    in_specs = [pl.BlockSpec((tm, a.shape[1]), lambda i: (i, 0)) for a in acts]
    in_specs += [pl.BlockSpec(w.shape, lambda i: (0, 0)) for w in ws]
    return _pallas(body, name="proj_res", grid=(S // tm,), in_specs=in_specs + [row], out_specs=row,
                   out_shape=_sds((S, D), F32), compiler_params=_cp(("parallel",)))(*acts, *ws, res)


def _matmul_nt(a, w, after=None):
    S, K = a.shape
    M = w.shape[0]
    tm = TOK_TILE

    def body(a_ref, w_ref, *rest):
        rest[-1][...] = _dot_nt(a_ref[...].astype(BF16), w_ref[...])

    extra = [] if after is None else [after]
    return _pallas(body, name="matmul_nt", grid=(S // tm,),
                   in_specs=[pl.BlockSpec((tm, K), lambda i: (i, 0)), pl.BlockSpec((M, K), lambda i: (0, 0))] + [ANY] * len(extra),
                   out_specs=pl.BlockSpec((tm, M), lambda i: (i, 0)), out_shape=_sds((S, M), F32),
                   compiler_params=_cp(("parallel",)))(a, w, *extra)


def _matmul_tn(a, b, tn):
    S, M = a.shape
    N = b.shape[1]
    tm = min(DW_TILE, S)
    nt = S // tm

    def body(a_ref, b_ref, o_ref, acc_ref):
        i = pl.program_id(1)

        @pl.when(i == 0)
        def _():
            acc_ref[...] = jnp.zeros_like(acc_ref)

        acc_ref[...] += _dot_tn(a_ref[...].astype(BF16), b_ref[...].astype(BF16))

        @pl.when(i == nt - 1)
        def _():
            o_ref[0] = acc_ref[...].astype(BF16)

    return _pallas(body, name="matmul_tn", grid=(N // tn, nt),
                   in_specs=[pl.BlockSpec((tm, M), lambda j, i: (i, 0)), pl.BlockSpec((tm, tn), lambda j, i: (i, j))],
                   out_specs=pl.BlockSpec((1, M, tn), lambda j, i: (j, 0, 0)), out_shape=_sds((N // tn, M, tn), BF16),
                   scratch_shapes=[pltpu.VMEM((M, tn), F32)],
                   compiler_params=_cp(("parallel", "arbitrary")))(a, b)


def _fill_shifts(win, rows):
    for b in range(1, SUBLANES):
        win[b, pl.ds(0, rows - SUBLANES), :] = win[0, pl.ds(b, rows - SUBLANES), :]


def _tap(win, offset, n, base=0):
    start = base + (offset - offset % SUBLANES)
    if not isinstance(start, int):
        start = pl.multiple_of(start, SUBLANES)
    return win[offset % SUBLANES, pl.ds(start, n), :]


def _conv_a_fwd(z, cw, cb, cn):
    S = z.shape[0]
    C = D_CONV
    tm = TOK_TILE
    hb = tm // HALO_A

    def body(u_ref, gt_ref, up_ref, gp_ref, cw_ref, cb_ref, cn_ref, a_ref, a1_ref, win):
        i = pl.program_id(0)
        prev = up_ref[...].astype(F32) * jax.nn.sigmoid(gp_ref[...].astype(F32))
        win[0, pl.ds(0, HALO_A), :] = jnp.where(i == 0, 0.0, prev)
        win[0, pl.ds(HALO_A, tm), :] = u_ref[...].astype(F32) * jax.nn.sigmoid(gt_ref[...].astype(F32))
        _fill_shifts(win, tm + HALO_A)

        acc = jnp.zeros((tm, C), F32)
        for k in range(CONV_A_WIDTH):
            acc = acc + cw_ref[k:k + 1, :] * _tap(win, HALO_A - (CONV_A_WIDTH - 1) + k, tm)
        a1 = acc + cb_ref[...]
        a1_ref[...] = a1
        a2 = a1 * _rms(a1) * cn_ref[...]
        a_ref[...] = (a2 * jax.nn.sigmoid(a2)).astype(BF16)

    cur = lambda c: pl.BlockSpec((tm, C), lambda i, c=c: (i, c))
    prv = lambda c: pl.BlockSpec((HALO_A, C), lambda i, c=c: (jnp.maximum(i * hb - 1, 0), c))
    vec = pl.BlockSpec((1, C), lambda i: (0, 0))
    return _pallas(body, name="conv_a_fwd", grid=(S // tm,),
                   in_specs=[cur(0), cur(1), prv(0), prv(1), pl.BlockSpec((32, C), lambda i: (0, 0)), vec, vec],
                   out_specs=[pl.BlockSpec((tm, C), lambda i: (i, 0)), pl.BlockSpec((tm, C), lambda i: (i, 0))],
                   out_shape=[_sds((S, C), BF16), _sds((S, C), F32)],
                   scratch_shapes=[pltpu.VMEM((SUBLANES, tm + HALO_A, C), F32)],
                   compiler_params=_cp(("parallel",)))(z, z, z, z, cw, cb, cn)


def _conv_a_bwd(da, a1, z, cw, cn):
    S = z.shape[0]
    C = D_CONV
    tm = TOK_TILE
    hb = tm // HALO_A
    nt = S // tm
    W = CONV_A_WIDTH

    def body(da_ref, a1_ref, dan_ref, a1n_ref, u_ref, gt_ref, up_ref, gp_ref, cw_ref, cn_ref,
             duz_ref, dcw_ref, dcb_ref, dcn_ref, win, dwin):
        i = pl.program_id(0)

        @pl.when(i == 0)
        def _():
            dcw_ref[...] = jnp.zeros_like(dcw_ref)
            dcb_ref[...] = jnp.zeros_like(dcb_ref)
            dcn_ref[...] = jnp.zeros_like(dcn_ref)

        cnv = cn_ref[...]

        def da1_of(dav, a1v):
            a2 = a1v * _rms(a1v) * cnv
            da2 = dav * _silu_grad(a2)
            dx, xh = _rms_bwd(da2, a1v, cnv)
            return dx, da2 * xh

        da1, dcn_t = da1_of(da_ref[...], a1_ref[...])
        da1n, _ = da1_of(dan_ref[...], a1n_ref[...])
        dwin[0, pl.ds(0, tm), :] = da1
        dwin[0, pl.ds(tm, HALO_A), :] = jnp.where(i == nt - 1, 0.0, da1n)
        _fill_shifts(dwin, tm + HALO_A)
        dcb_ref[...] += jnp.sum(da1, axis=0, keepdims=True)
        dcn_ref[...] += jnp.sum(dcn_t, axis=0, keepdims=True)

        prev = up_ref[...].astype(F32) * jax.nn.sigmoid(gp_ref[...].astype(F32))
        win[0, pl.ds(0, HALO_A), :] = jnp.where(i == 0, 0.0, prev)
        win[0, pl.ds(HALO_A, tm), :] = u_ref[...].astype(F32) * jax.nn.sigmoid(gt_ref[...].astype(F32))
        _fill_shifts(win, tm + HALO_A)

        def rows_block(rb, carry):
            r0 = pl.multiple_of(rb * CONV_ROWS, CONV_ROWS)
            rows = pl.ds(r0, CONV_ROWS)
            da1_b = dwin[0, rows, :]
            da0 = jnp.zeros((CONV_ROWS, C), F32)
            for k in range(W):
                da0 = da0 + cw_ref[k:k + 1, :] * _tap(dwin, W - 1 - k, CONV_ROWS, r0)
                dcw_ref[k:k + 1, :] += jnp.sum(da1_b * _tap(win, HALO_A - (W - 1) + k, CONV_ROWS, r0), axis=0, keepdims=True)
            u = u_ref[rows, :].astype(F32)
            sg = jax.nn.sigmoid(gt_ref[rows, :].astype(F32))
            duz_ref[rows, 0:C] = (da0 * sg).astype(BF16)
            duz_ref[rows, C:2 * C] = (da0 * u * sg * (1.0 - sg)).astype(BF16)
            return carry

        lax.fori_loop(0, tm // CONV_ROWS, rows_block, 0)

    cur = lambda c: pl.BlockSpec((tm, C), lambda i, c=c: (i, c))
    prv = lambda c: pl.BlockSpec((HALO_A, C), lambda i, c=c: (jnp.maximum(i * hb - 1, 0), c))
    nxt = pl.BlockSpec((HALO_A, C), lambda i: (jnp.minimum((i + 1) * hb, S // HALO_A - 1), 0))
    vec = pl.BlockSpec((1, C), lambda i: (0, 0))
    return _pallas(body, name="conv_a_bwd", grid=(nt,),
                   in_specs=[cur(0), cur(0), nxt, nxt, cur(0), cur(1), prv(0), prv(1),
                             pl.BlockSpec((32, C), lambda i: (0, 0)), vec],
                   out_specs=[pl.BlockSpec((tm, 2 * C), lambda i: (i, 0)), pl.BlockSpec((32, C), lambda i: (0, 0)), vec, vec],
                   out_shape=[_sds((S, 2 * C), BF16), _sds((32, C), F32), _sds((1, C), F32), _sds((1, C), F32)],
                   scratch_shapes=[pltpu.VMEM((SUBLANES, tm + HALO_A, C), F32)] * 2,
                   compiler_params=_cp(("arbitrary",)))(da, a1, da, a1, z, z, z, z, cw, cn)


def _forget_scan(fl, bf):
    S, L = fl.shape
    B = SCAN_BLK

    def body(fl_ref, bf_ref, flb_ref, F_ref):
        tri = (lax.broadcasted_iota(jnp.int32, (B, B), 0) >= lax.broadcasted_iota(jnp.int32, (B, B), 1)).astype(F32)

        def step(c, carry):
            rows = pl.ds(pl.multiple_of(c * B, B), B)
            v = fl_ref[rows, :] + bf_ref[...]
            flb_ref[rows, :] = v
            lf = jnp.minimum(v, 0.0) - jnp.log1p(jnp.exp(-jnp.abs(v)))
            cs = jnp.dot(tri, lf, precision=lax.Precision.HIGHEST, preferred_element_type=F32) + carry
            F_ref[rows, :] = cs
            return cs[B - 1:B, :]

        lax.fori_loop(0, S // B, step, jnp.zeros((1, L), F32))

    return _pallas(body, name="forget_scan", out_shape=[_sds((S, L), F32), _sds((S, L), F32)],
                   compiler_params=_cp())(fl, bf)


def _forget_scan_bwd(dF, flb):
    S, L = dF.shape
    B = SCAN_BLK
    nb = S // B

    def body(dF_ref, flb_ref, dfl_ref, db_ref):
        tri = (lax.broadcasted_iota(jnp.int32, (B, B), 0) <= lax.broadcasted_iota(jnp.int32, (B, B), 1)).astype(F32)

        def step(t, carry):
            carry_cs, db = carry
            rows = pl.ds(pl.multiple_of((nb - 1 - t) * B, B), B)
            cs = jnp.dot(tri, dF_ref[rows, :], precision=lax.Precision.HIGHEST, preferred_element_type=F32) + carry_cs
            dfl = cs * jax.nn.sigmoid(-flb_ref[rows, :])
            dfl_ref[rows, :] = dfl
            return cs[0:1, :], db + jnp.sum(dfl, axis=0, keepdims=True)

        _, db = lax.fori_loop(0, nb, step, (jnp.zeros((1, L), F32), jnp.zeros((1, L), F32)))
        db_ref[...] = db

    return _pallas(body, name="forget_scan_bwd", out_shape=[_sds((S, L), F32), _sds((1, L), F32)],
                   compiler_params=_cp())(dF, flb)


NEG = -1e30


def _causal_mask(t):
    return lax.broadcasted_iota(jnp.int32, (t, t), 0) >= lax.broadcasted_iota(jnp.int32, (t, t), 1)


AUG = 128
C_F, C_ONE, C_LSE = 64, 67, 70


def _split3(f):
    a = f.astype(BF16).astype(F32)
    r = f - a
    b = r.astype(BF16).astype(F32)
    return a, b, r - b


def _put3(lane, base, parts, other):
    out = other
    for k, p in enumerate(parts):
        out = jnp.where(lane == base + k, p, out)
    return out


def _ones3(lane, base):
    return (lane >= base) & (lane < base + 3)


def _lane_ids(rows):
    return lax.broadcasted_iota(jnp.int32, (rows, AUG), 1)


def _pair_rms(x, lo):
    sq = x * x
    ms_a = jnp.sum(jnp.where(lo, sq, 0.0), axis=-1, keepdims=True) * (1.0 / HEAD_DIM)
    ms_b = jnp.sum(jnp.where(lo, 0.0, sq), axis=-1, keepdims=True) * (1.0 / HEAD_DIM)
    return jnp.where(lo, lax.rsqrt(ms_a + EPS), lax.rsqrt(ms_b + EPS))


def _qkv_prep(z, Fc, qw, kw):
    S = z.shape[0]
    tp = min(QKN_TILE, S)
    scale = 1.0 / math.sqrt(HEAD_DIM)

    def body(zq_ref, zk_ref, zv_ref, F_ref, qw_ref, kw_ref, q_ref, k_ref, v_ref):
        j = pl.program_id(0)
        lane = _lane_ids(tp)
        lo = lane < HEAD_DIM
        Fv = F_ref[...]
        xq = zq_ref[...].astype(F32)
        xk = zk_ref[...].astype(F32)
        qn = xq * _pair_rms(xq, lo) * qw_ref[...] * scale
        kn = xk * _pair_rms(xk, lo) * kw_ref[...]
        vv = zv_ref[...].astype(F32)
        for half in range(2):
            take = (lambda a: a) if half == 0 else (lambda a: pltpu.roll(a, HEAD_DIM, 1))
            fp = _split3(jnp.sum(jnp.where(lane == 2 * j + half, Fv, 0.0), axis=-1, keepdims=True))
            qx = _put3(lane, C_F, fp, jnp.where(_ones3(lane, C_ONE), 1.0, 0.0))
            kx = _put3(lane, C_ONE, [-p for p in fp], jnp.where(_ones3(lane, C_F) | _ones3(lane, C_LSE), 1.0, 0.0))
            vx = jnp.where(_ones3(lane, C_F), 1.0, 0.0)
            q_ref[half] = jnp.where(lo, take(qn), qx).astype(BF16)
            k_ref[half] = jnp.where(lo, take(kn), kx).astype(BF16)
            v_ref[half] = jnp.where(lo, take(vv), vx).astype(BF16)

    col = lambda c0: pl.BlockSpec((tp, AUG), lambda j, i, c0=c0: (i, c0 + j))
    vec = pl.BlockSpec((1, AUG), lambda j, i: (0, 0))
    out = pl.BlockSpec((2, tp, AUG), lambda j, i: (j, i, 0))
    return _pallas(body, name="qkv_prep", grid=(N_HEADS // 2, S // tp),
                   in_specs=[col(8), col(12), col(16), pl.BlockSpec((tp, AUG), lambda j, i: (i, 0)), vec, vec],
                   out_specs=[out, out, out], out_shape=[_sds((N_HEADS, S, AUG), BF16)] * 3,
                   compiler_params=_cp(("parallel", "parallel")))(z, z, z, Fc, qw, kw)


def _fox_fwd(q_aug, k_aug, v_aug):
    H, S, A = q_aug.shape
    t = ATT_TILE
    nq = S // t

    def body(q_ref, k_ref, v_ref, o_ref, q2_ref):
        i = pl.program_id(1)
        q = q_ref[0]

        def tile(j, carry, diag):
            m, acc = carry
            rows = pl.ds(pl.multiple_of(j * t, t), t)
            s = _dot_nt(q, k_ref[0, rows, :])
            if diag:
                s = jnp.where(_causal_mask(t), s, NEG)
            m_new = jnp.maximum(m, jnp.max(s, axis=-1, keepdims=True))
            p = jnp.exp(s - m_new)
            acc = jnp.exp(m - m_new) * acc + _dot(p.astype(BF16), v_ref[0, rows, :])
            return m_new, acc

        init = (jnp.full((t, 1), NEG, F32), jnp.zeros((t, A), F32))
        carry = lax.fori_loop(0, i, lambda j, c: tile(j, c, False), init)
        m, acc = tile(i, carry, True)
        lane = _lane_ids(t)
        l = jnp.sum(jnp.where(lane == C_F, acc, 0.0), axis=-1, keepdims=True)
        o_ref[0] = (acc / l).astype(BF16)
        lse = m + jnp.log(l)
        q2_ref[0] = (q.astype(F32) + _put3(lane, C_LSE, [-p for p in _split3(lse)], 0.0)).astype(BF16)

    qblk = pl.BlockSpec((1, t, A), lambda h, i: (h, i, 0))
    full = pl.BlockSpec((1, S, A), lambda h, i: (h, 0, 0))
    return _pallas(body, name="fox_fwd", grid=(H, nq), in_specs=[qblk, full, full], out_specs=[qblk, qblk],
                   out_shape=[_sds((H, S, A), BF16)] * 2, compiler_params=_cp(("parallel", "parallel")))(q_aug, k_aug, v_aug)


def _do_prep(dcat, o_aug):
    S = dcat.shape[0]
    tp = min(QKN_TILE, S)

    def body(d_ref, o_ref, out_ref):
        lane = _lane_ids(tp)
        lo = lane < HEAD_DIM
        x = d_ref[...]
        for half in range(2):
            d = jnp.where(lo, x if half == 0 else pltpu.roll(x, HEAD_DIM, 1), 0.0)
            delta = jnp.sum(d * o_ref[half].astype(F32), axis=-1, keepdims=True)
            out_ref[half] = jnp.where(lo, d, _put3(lane, C_F, [-p for p in _split3(delta)], 0.0)).astype(BF16)

    pair = pl.BlockSpec((2, tp, AUG), lambda j, i: (j, i, 0))
    return _pallas(body, name="do_prep", grid=(N_HEADS // 2, S // tp),
                   in_specs=[pl.BlockSpec((tp, AUG), lambda j, i: (i, D_CONV // AUG + j)), pair], out_specs=pair,
                   out_shape=_sds((N_HEADS, S, AUG), BF16), compiler_params=_cp(("parallel", "parallel")))(dcat, o_aug)


def _fox_bwd(q2, k_aug, v_aug, do_aug):
    H, S, A = q2.shape
    t = ATT_TILE
    nq = S // t

    def body(q_ref, k_ref, v_ref, do_ref, dq_ref, dk_ref, dv_ref):
        j = pl.program_id(1)

        @pl.when(j == 0)
        def _():
            dq_ref[...] = jnp.zeros_like(dq_ref)

        k = k_ref[0]
        vv = v_ref[0]

        def tile(i, carry, diag):
            dk, dv = carry
            rows = pl.ds(pl.multiple_of(i * t, t), t)
            q = q_ref[0, rows, :]
            dov = do_ref[0, rows, :]
            s = _dot_nt(q, k)
            if diag:
                s = jnp.where(_causal_mask(t), s, NEG)
            p = jnp.exp(s)
            dv = dv + _dot_tn(p.astype(BF16), dov)
            dsb = (p * _dot_nt(dov, vv)).astype(BF16)
            dq_ref[0, rows, :] += _dot(dsb, k)
            dk = dk + _dot_tn(dsb, q)
            return dk, dv

        init = (jnp.zeros((t, A), F32), jnp.zeros((t, A), F32))
        carry = tile(j, init, True)
        dk, dv = lax.fori_loop(j + 1, nq, lambda i, c: tile(i, c, False), carry)
        dk_ref[0] = dk
        dv_ref[0] = dv

    full = pl.BlockSpec((1, S, A), lambda h, j: (h, 0, 0))
    kblk = pl.BlockSpec((1, t, A), lambda h, j: (h, j, 0))
    return _pallas(body, name="fox_bwd", grid=(H, nq), in_specs=[full, kblk, kblk, full], out_specs=[full, kblk, kblk],
                   out_shape=[_sds((H, S, A), F32)] * 3,
                   compiler_params=_cp(("parallel", "arbitrary")))(q2, k_aug, v_aug, do_aug)


def _qkv_bwd(dq, dk, dv, z, qw, kw):
    S = z.shape[0]
    tp = min(QKN_TILE, S)
    scale = 1.0 / math.sqrt(HEAD_DIM)

    def body(dq_ref, dk_ref, dv_ref, zq_ref, zk_ref, qw_ref, kw_ref, dqf_ref, dkf_ref, dvf_ref, dF_ref, dqw_ref, dkw_ref):
        i, j = pl.program_id(0), pl.program_id(1)
        lane = _lane_ids(tp)
        lo = lane < HEAD_DIM

        @pl.when((i == 0) & (j == 0))
        def _():
            dqw_ref[...] = jnp.zeros_like(dqw_ref)
            dkw_ref[...] = jnp.zeros_like(dkw_ref)

        def pair(ref):
            return jnp.where(lo, ref[0], pltpu.roll(ref[1], HEAD_DIM, 1))

        def norm_bwd(g, x, w):
            r = _pair_rms(x, lo)
            xh = x * r
            dxh = g * w
            tt = dxh * xh
            mean_a = jnp.sum(jnp.where(lo, tt, 0.0), axis=-1, keepdims=True) * (1.0 / HEAD_DIM)
            mean_b = jnp.sum(jnp.where(lo, 0.0, tt), axis=-1, keepdims=True) * (1.0 / HEAD_DIM)
            return r * (dxh - xh * jnp.where(lo, mean_a, mean_b)), g * xh

        dxq, gq = norm_bwd(pair(dq_ref) * scale, zq_ref[...].astype(F32), qw_ref[...])
        dqf_ref[...] = dxq.astype(BF16)
        dqw_ref[...] += jnp.sum(gq, axis=0, keepdims=True)
        dxk, gk = norm_bwd(pair(dk_ref), zk_ref[...].astype(F32), kw_ref[...])
        dkf_ref[...] = dxk.astype(BF16)
        dkw_ref[...] += jnp.sum(gk, axis=0, keepdims=True)
        dvf_ref[...] = pair(dv_ref).astype(BF16)

        contrib = jnp.zeros((tp, AUG), F32)
        for half in range(2):
            df = (jnp.sum(jnp.where(lane == C_F, dq_ref[half], 0.0), axis=-1, keepdims=True)
                  - jnp.sum(jnp.where(lane == C_ONE, dk_ref[half], 0.0), axis=-1, keepdims=True))
            contrib = jnp.where(lane == 2 * j + half, df, contrib)

        @pl.when(j == 0)
        def _():
            dF_ref[...] = contrib

        @pl.when(j > 0)
        def _():
            dF_ref[...] += contrib

    pairb = pl.BlockSpec((2, tp, AUG), lambda i, j: (j, i, 0))
    col = lambda c0: pl.BlockSpec((tp, AUG), lambda i, j, c0=c0: (i, c0 + j))
    vec = pl.BlockSpec((1, AUG), lambda i, j: (0, 0))
    flat = pl.BlockSpec((tp, AUG), lambda i, j: (i, j))
    return _pallas(body, name="qkv_bwd", grid=(S // tp, N_HEADS // 2),
                   in_specs=[pairb, pairb, pairb, col(8), col(12), vec, vec],
                   out_specs=[flat, flat, flat, pl.BlockSpec((tp, AUG), lambda i, j: (i, 0)), vec, vec],
                   out_shape=[_sds((S, D_ATTN), BF16)] * 3 + [_sds((S, AUG), F32), _sds((1, AUG), F32), _sds((1, AUG), F32)],
                   compiler_params=_cp(("arbitrary", "arbitrary")))(dq, dk, dv, z, z, qw, kw)


def _proj_res_heads(a, wa, o_aug, wo, res):
    S, D = res.shape
    H = o_aug.shape[0]
    tm = TOK_TILE

    def body(a_ref, wa_ref, o_ref, wo_ref, res_ref, out_ref):
        acc = res_ref[...] + _dot(a_ref[...], wa_ref[...])
        for h in range(H):
            acc = acc + _dot(o_ref[h], wo_ref[h])
        out_ref[...] = acc

    row = pl.BlockSpec((tm, D), lambda i: (i, 0))
    return _pallas(body, name="proj_res_heads", grid=(S // tm,),
                   in_specs=[pl.BlockSpec((tm, a.shape[1]), lambda i: (i, 0)), pl.BlockSpec(wa.shape, lambda i: (0, 0)),
                             pl.BlockSpec((H, tm, AUG), lambda i: (0, i, 0)), pl.BlockSpec(wo.shape, lambda i: (0, 0, 0)), row],
                   out_specs=row, out_shape=_sds((S, D), F32), compiler_params=_cp(("parallel",)))(a, wa, o_aug, wo, res)


def _heads_tn(o_aug, d):
    H, S, A = o_aug.shape
    D = d.shape[1]
    tm = min(DW_TILE, S)
    nt = S // tm

    def body(o_ref, d_ref, out_ref, acc_ref):
        i = pl.program_id(0)

        @pl.when(i == 0)
        def _():
            acc_ref[...] = jnp.zeros_like(acc_ref)

        dv = d_ref[...].astype(BF16)
        for h in range(H):
            acc_ref[h] += _dot_tn(o_ref[h], dv)

        @pl.when(i == nt - 1)
        def _():
            out_ref[...] = acc_ref[...].astype(BF16)

    return _pallas(body, name="heads_tn", grid=(nt,),
                   in_specs=[pl.BlockSpec((H, tm, A), lambda i: (0, i, 0)), pl.BlockSpec((tm, D), lambda i: (i, 0))],
                   out_specs=pl.BlockSpec((H, A, D), lambda i: (0, 0, 0)), out_shape=_sds((H, A, D), BF16),
                   scratch_shapes=[pltpu.VMEM((H, A, D), F32)], compiler_params=_cp(("arbitrary",)))(o_aug, d)


def _odd_mid_fwd(z, cw):
    S = z.shape[0]
    D = z.shape[1] // 3
    tm = TOK_TILE
    hb = tm // HALO_C
    W = CONV_C_WIDTH

    def body(gb_ref, gc_ref, hh_ref, gcp_ref, hhp_ref, cw_ref, y_ref, win):
        i = pl.program_id(0)
        prev = gcp_ref[...].astype(F32) * hhp_ref[...].astype(F32)
        win[pl.ds(0, HALO_C), :] = jnp.where(i == 0, 0.0, prev)
        win[pl.ds(HALO_C, tm), :] = gc_ref[...].astype(F32) * hh_ref[...].astype(F32)
        c1 = jnp.zeros((tm, D), F32)
        for k in range(W):
            c1 = c1 + cw_ref[k:k + 1, :] * win[pl.ds(HALO_C - (W - 1) + k, tm), :]
        y_ref[...] = (gb_ref[...].astype(F32) * c1).astype(BF16)

    cur = lambda c: pl.BlockSpec((tm, D), lambda i, c=c: (i, c))
    prv = lambda c: pl.BlockSpec((HALO_C, D), lambda i, c=c: (jnp.maximum(i * hb - 1, 0), c))
    return _pallas(body, name="odd_mid_fwd", grid=(S // tm,),
                   in_specs=[cur(0), cur(1), cur(2), prv(1), prv(2), pl.BlockSpec((8, D), lambda i: (0, 0))],
                   out_specs=pl.BlockSpec((tm, D), lambda i: (i, 0)), out_shape=_sds((S, D), BF16),
                   scratch_shapes=[pltpu.VMEM((tm + HALO_C, D), F32)],
                   compiler_params=_cp(("parallel",)))(z, z, z, z, z, cw)


def _odd_mid_bwd(dy, z, cw):
    S = z.shape[0]
    D = z.shape[1] // 3
    tm = TOK_TILE
    hb = tm // HALO_C
    nt = S // tm
    W = CONV_C_WIDTH

    def body(dy_ref, dyn_ref, gb_ref, gbn_ref, gc_ref, hh_ref, gcp_ref, hhp_ref, cw_ref, dz_ref, dcw_ref, win, dwin):
        i = pl.program_id(0)

        @pl.when(i == 0)
        def _():
            dcw_ref[...] = jnp.zeros_like(dcw_ref)

        gc = gc_ref[...].astype(F32)
        hh = hh_ref[...].astype(F32)
        prev = gcp_ref[...].astype(F32) * hhp_ref[...].astype(F32)
        win[pl.ds(0, HALO_C), :] = jnp.where(i == 0, 0.0, prev)
        win[pl.ds(HALO_C, tm), :] = gc * hh
        dyv = dy_ref[...]
        dc1 = dyv * gb_ref[...].astype(F32)
        dwin[pl.ds(0, tm), :] = dc1
        dwin[pl.ds(tm, HALO_C), :] = jnp.where(i == nt - 1, 0.0, dyn_ref[...] * gbn_ref[...].astype(F32))
        c1 = jnp.zeros((tm, D), F32)
        dc0 = jnp.zeros((tm, D), F32)
        for k in range(W):
            tap = win[pl.ds(HALO_C - (W - 1) + k, tm), :]
            c1 = c1 + cw_ref[k:k + 1, :] * tap
            dc0 = dc0 + cw_ref[k:k + 1, :] * dwin[pl.ds(W - 1 - k, tm), :]
            dcw_ref[k:k + 1, :] += jnp.sum(dc1 * tap, axis=0, keepdims=True)
        dz_ref[:, 0:D] = (dyv * c1).astype(BF16)
        dz_ref[:, D:2 * D] = (dc0 * hh).astype(BF16)
        dz_ref[:, 2 * D:3 * D] = (dc0 * gc).astype(BF16)

    cur = lambda c: pl.BlockSpec((tm, D), lambda i, c=c: (i, c))
    prv = lambda c: pl.BlockSpec((HALO_C, D), lambda i, c=c: (jnp.maximum(i * hb - 1, 0), c))
    nxt = pl.BlockSpec((HALO_C, D), lambda i: (jnp.minimum((i + 1) * hb, S // HALO_C - 1), 0))
    return _pallas(body, name="odd_mid_bwd", grid=(nt,),
                   in_specs=[cur(0), nxt, cur(0), nxt, cur(1), cur(2), prv(1), prv(2), pl.BlockSpec((8, D), lambda i: (0, 0))],
                   out_specs=[pl.BlockSpec((tm, 3 * D), lambda i: (i, 0)), pl.BlockSpec((8, D), lambda i: (0, 0))],
                   out_shape=[_sds((S, 3 * D), BF16), _sds((8, D), F32)],
                   scratch_shapes=[pltpu.VMEM((tm + HALO_C, D), F32), pltpu.VMEM((tm + HALO_C, D), F32)],
                   compiler_params=_cp(("arbitrary",)))(dy, dy, z, z, z, z, z, z, cw)


def _loss_head(y, tgt):
    S, D = y.shape
    tm = TOK_TILE

    def body(y_ref, t_ref, dy_ref, l_ref):
        @pl.when(pl.program_id(0) == 0)
        def _():
            l_ref[...] = jnp.zeros_like(l_ref)

        e = y_ref[...] - t_ref[...]
        dy_ref[...] = e * (1.0 / D)
        l_ref[...] += jnp.sum(jnp.sum(e * e, axis=-1, keepdims=True), axis=0, keepdims=True) * (0.5 / D)

    row = pl.BlockSpec((tm, D), lambda i: (i, 0))
    return _pallas(body, name="loss_head", grid=(S // tm,), in_specs=[row, row],
                   out_specs=[row, pl.BlockSpec((1, 1), lambda i: (0, 0))],
                   out_shape=[_sds((S, D), F32), _sds((1, 1), F32)],
                   compiler_params=_cp(("arbitrary",)))(y, tgt)


def _pad_rows(a, rows):
    return jnp.pad(a, ((0, rows - a.shape[0]), (0, 0)))


def _local_step(x, tgt, W, need=lambda block, after: None, done=lambda block, block_grads: None):
    S, D = x.shape
    grads = {}
    saved = {}

    def gain_after(gain, token):
        return gain if token is None else gain + token

    def ffn_f(tag, l, xin):
        need((tag, l), xin)
        out, xn, G, U = _ffn_fwd(xin, W[tag + "_norm"][l:l + 1], W[tag + "_w_gate"][l], W[tag + "_w_up"][l],
                                 W[tag + "_w_down"][l])
        saved[(tag, l)] = (xin, xn, G, U)
        return out

    def ffn_b(tag, l, dout):
        xin, xn, G, U = saved[(tag, l)]
        keys = [(tag + "_w_gate", l), (tag + "_w_up", l), (tag + "_w_down", l)]
        *dws, dG, dU = _ffn_bwd_w(dout, xn, G, U, W[tag + "_w_down"][l])
        big = dict(zip(keys, dws))
        grads.update(big)
        token = done((tag, l), big)
        dx, dg = _norm_in_bwd([dG, dU], [W[tag + "_w_gate"][l], W[tag + "_w_up"][l]], xin,
                              gain_after(W[tag + "_norm"][l:l + 1], token), dout, w_rows=True)
        grads[(tag + "_norm", l)] = dg
        return dx

    x0a = ffn_f("ffn1", 0, x)
    need(("ev", 0), x0a)
    w_in = W["ev_w_in"]
    w_main, w_f = w_in[:, :2560], jnp.pad(w_in[:, 2560:], ((0, 0), (0, 120)))
    h0, z0, fl = _norm_proj(x0a, W["mix_norm"][0:1], w_main, w_f)
    cw_a = _pad_rows(W["ev_conv_w"], 32)
    a_act, a1 = _conv_a_fwd(z0, cw_a, W["ev_conv_b"], W["ev_conv_norm"])
    flb, Fc = _forget_scan(fl, jnp.pad(W["ev_b_f"], ((0, 0), (0, 120))))
    qw2, kw2 = jnp.tile(W["ev_q_norm"], (1, 2)), jnp.tile(W["ev_k_norm"], (1, 2))
    q_aug, k_aug, v_aug = _qkv_prep(z0, Fc, qw2, kw2)
    o_aug, q_lse = _fox_fwd(q_aug, k_aug, v_aug)
    w_out_e = W["ev_w_out"]
    w_out_o = jnp.pad(w_out_e[D_CONV:].reshape(N_HEADS, HEAD_DIM, D), ((0, 0), (0, AUG - HEAD_DIM), (0, 0)))
    x0b = _proj_res_heads(a_act, w_out_e[:D_CONV], o_aug, w_out_o, x0a)
    x0c = ffn_f("ffn2", 0, x0b)
    x1a = ffn_f("ffn1", 1, x0c)
    need(("od", 0), x1a)
    h1, z1 = _norm_proj(x1a, W["mix_norm"][1:2], W["od_w_in"])
    cw_c = _pad_rows(W["od_conv_w"], 8)
    y1 = _odd_mid_fwd(z1, cw_c)
    x1b = _proj_res([y1], [W["od_w_out"]], x1a)
    x1c = ffn_f("ffn2", 1, x1b)
    dy, loss = _loss_head(x1c, tgt)

    d = ffn_b("ffn2", 1, dy)
    dy1 = _matmul_nt(d, W["od_w_out"])
    grads[("od_w_out", 0)] = _matmul_tn(y1, d, D)[0]
    dz1, dcw_c = _odd_mid_bwd(dy1, z1, cw_c)
    grads[("od_conv_w", 0)] = dcw_c[:CONV_C_WIDTH]
    grads[("od_w_in", 0)] = _matmul_tn(h1, dz1, 3 * D // 4)
    token = done(("od", 0), {k: grads[k] for k in (("od_w_out", 0), ("od_w_in", 0))})
    d, dg = _norm_in_bwd([dz1[None]], [W["od_w_in"][None]], x1a, gain_after(W["mix_norm"][1:2], token), d)
    grads[("mix_norm", 1)] = dg
    d = ffn_b("ffn1", 1, d)
    d = ffn_b("ffn2", 0, d)
    dcat = _matmul_nt(d, w_out_e)
    grads[("ev_w_out", 0)] = jnp.concatenate([_matmul_tn(a_act, d, D)[0],
                                              _heads_tn(o_aug, d)[:, :HEAD_DIM].reshape(D_ATTN, D)], axis=0)
    duz, dcw_a, dcb, dcn = _conv_a_bwd(dcat, a1, z0, cw_a, W["ev_conv_norm"])
    grads[("ev_conv_w", 0)] = dcw_a[:CONV_A_WIDTH]
    grads[("ev_conv_b", 0)] = dcb
    grads[("ev_conv_norm", 0)] = dcn
    dq_a, dk_a, dv_a = _fox_bwd(q_lse, k_aug, v_aug, _do_prep(dcat, o_aug))
    dqf, dkf, dvf, dF, dqw, dkw = _qkv_bwd(dq_a, dk_a, dv_a, z0, qw2, kw2)
    grads[("ev_q_norm", 0)] = dqw[:, :HEAD_DIM] + dqw[:, HEAD_DIM:]
    grads[("ev_k_norm", 0)] = dkw[:, :HEAD_DIM] + dkw[:, HEAD_DIM:]
    dfl, dbf = _forget_scan_bwd(dF, flb)
    grads[("ev_b_f", 0)] = dbf[:, :N_HEADS]
    dz0 = jnp.concatenate([duz, dqf, dkf, dvf], axis=1)
    dflb = dfl.astype(BF16)
    gmain = _matmul_tn(h0, dz0, 640)
    gmain = gmain.transpose(1, 0, 2).reshape(D, 2560)
    gf = _matmul_tn(h0, dflb, 128)[0][:, :N_HEADS]
    grads[("ev_w_in", 0)] = jnp.concatenate([gmain, gf], axis=1)
    token = done(("ev", 0), {k: grads[k] for k in (("ev_w_out", 0), ("ev_w_in", 0))})
    d, dg = _norm_in_bwd([dz0[None], dflb[None]], [w_main[None], w_f[None]], x0a, gain_after(W["mix_norm"][0:1], token), d)
    grads[("mix_norm", 0)] = dg
    d = ffn_b("ffn1", 0, d)
    return loss, d, grads


def _place():
    x, y, c = lax.axis_index("x"), lax.axis_index("y"), lax.axis_index("c")
    chips = [(1 - x, y), (x, 1 - y), (1 - x, 1 - y)]
    return x, y, c, chips


def _remote(src, dst, send_sem, recv_sem, to):
    return pltpu.make_async_remote_copy(src_ref=src, dst_ref=dst, send_sem=send_sem, recv_sem=recv_sem,
                                        device_id=to, device_id_type=MESH)


HBM = pl.BlockSpec(memory_space=pltpu.HBM)
SEM = pl.BlockSpec(memory_space=pltpu.SEMAPHORE)
EFFECT = pltpu.SideEffectType.DATAFLOW_SIDE_EFFECTING


def _in_hbm(a):
    return pltpu.with_memory_space_constraint(a, pltpu.HBM)


def _ag_start(bufs):
    n = len(bufs)

    def body(*refs):
        send_sems, recv_sems = refs[n], refs[n + 1]
        outs = refs[n + 2:]
        x, y, c, chips = _place()
        me = 2 * x + y
        for a in [n - 1] + list(range(n - 1)):
            if a == n - 1:
                blk = outs[a].at[me]
            else:
                h = outs[a].shape[1] // 2
                blk = outs[a].at[me, pl.ds(c * h, h)]
            for jj, (px, py) in enumerate(chips):
                _remote(blk, blk, send_sems.at[3 * a + jj], recv_sems.at[3 * a + jj], (px, py, c)).start()

    return _pallas(
        body, name="gather_start",
        out_shape=[pltpu.SemaphoreType.DMA((3 * n,)), pltpu.SemaphoreType.DMA((3 * n,))] + [pltpu.HBM(b.shape, b.dtype) for b in bufs],
        in_specs=[HBM] * n, out_specs=[SEM, SEM] + [HBM] * n, input_output_aliases={a: 2 + a for a in range(n)},
        compiler_params=pltpu.CompilerParams(has_side_effects=EFFECT),
    )(*[_in_hbm(b) for b in bufs])


def _ag_mid(g, ici_send, ici_recv, bufs, idx, taps, n_big, after):
    n = len(bufs)
    arrs = list(bufs) + ([taps] if taps is not None else [])
    m = len(arrs)

    def body(*refs):
        ici_s, ici_r = refs[0], refs[1]
        d_send, d_recv = refs[m + 3], refs[m + 4]
        outs = refs[m + 5:]
        x, y, c, chips = _place()
        me = 2 * x + y
        for i in range(m):
            a = idx[i] if i < n else n_big
            for jj, (px, py) in enumerate(chips):
                k = 3 * a + jj
                if i < n:
                    h = outs[i].shape[1] // 2
                    mine, blk = outs[i].at[me, pl.ds(c * h, h)], outs[i].at[2 * px + py, pl.ds(c * h, h)]
                else:
                    mine, blk = outs[i].at[me], outs[i].at[2 * px + py]
                _remote(mine, mine, ici_s.at[k], ici_r.at[k], (px, py, c)).wait_send()
                _remote(blk, blk, ici_s.at[k], ici_r.at[k], (px, py, c)).wait_recv()
                if i < n:
                    _remote(blk, blk, d_send.at[3 * i + jj], d_recv.at[3 * i + jj], (x, y, 1 - c)).start()

    return _pallas(
        body, name=f"gather_pass_on_{g}",
        out_shape=[pltpu.SemaphoreType.DMA((3 * n,)), pltpu.SemaphoreType.DMA((3 * n,))] + [pltpu.HBM(b.shape, b.dtype) for b in arrs],
        in_specs=[SEM, SEM] + [HBM] * m + [ANY], out_specs=[SEM, SEM] + [HBM] * m,
        input_output_aliases={2 + i: 2 + i for i in range(m)},
        compiler_params=pltpu.CompilerParams(has_side_effects=EFFECT),
    )(ici_send, ici_recv, *arrs, after)


def _ag_wait(g, d_send, d_recv, arrs, n, after):
    m = len(arrs)

    def body(*refs):
        d_s, d_r = refs[0], refs[1]
        outs = refs[m + 3:]
        x, y, c, chips = _place()
        for i in range(n):
            h = outs[i].shape[1] // 2
            for jj, (px, py) in enumerate(chips):
                sent = outs[i].at[2 * px + py, pl.ds(c * h, h)]
                got = outs[i].at[2 * px + py, pl.ds((1 - c) * h, h)]
                _remote(sent, sent, d_s.at[3 * i + jj], d_r.at[3 * i + jj], (x, y, 1 - c)).wait_send()
                _remote(got, got, d_s.at[3 * i + jj], d_r.at[3 * i + jj], (x, y, 1 - c)).wait_recv()

    return _pallas(
        body, name=f"gather_wait_{g}", out_shape=[pltpu.HBM(b.shape, b.dtype) for b in arrs],
        in_specs=[SEM, SEM] + [HBM] * m + [ANY], out_specs=[HBM] * m,
        input_output_aliases={2 + i: i for i in range(m)},
        compiler_params=pltpu.CompilerParams(has_side_effects=EFFECT),
    )(d_send, d_recv, *arrs, after)


def _pair_start(g, gs, after):
    n = len(gs)
    zones = [lax.empty((4, a.shape[1] // 2, a.shape[2]), a.dtype) for a in gs]
    extra = [] if after is None else [after]

    def body(*refs):
        k0 = 2 * n + len(extra)
        send_sems, recv_sems = refs[k0], refs[k0 + 1]
        src, dst = refs[k0 + 2:k0 + 2 + n], refs[k0 + 2 + n:k0 + 2 + 2 * n]
        token = refs[k0 + 2 + 2 * n]
        x, y, c, _ = _place()
        for a in range(n):
            h = src[a].shape[1] // 2
            _remote(src[a].at[:, pl.ds((1 - c) * h, h)], dst[a], send_sems.at[a], recv_sems.at[a], (x, y, 1 - c)).start()
        token[...] = jnp.zeros_like(token)

    return _pallas(
        body, name=f"grad_pair_start_{g}",
        out_shape=[pltpu.SemaphoreType.DMA((n,)), pltpu.SemaphoreType.DMA((n,))]
        + [pltpu.HBM(a.shape, a.dtype) for a in gs + zones] + [_sds((8, 128), F32)],
        in_specs=[HBM] * (2 * n) + [ANY] * len(extra),
        out_specs=[SEM, SEM] + [HBM] * (2 * n) + [pl.BlockSpec(memory_space=pltpu.VMEM)],
        input_output_aliases={i: 2 + i for i in range(2 * n)},
        compiler_params=pltpu.CompilerParams(has_side_effects=EFFECT),
    )(*[_in_hbm(a) for a in gs + zones], *extra)


def _pair_wait(g, send, recv, gs, zones):
    n = len(gs)

    def body(*refs):
        s_ref, r_ref = refs[0], refs[1]
        outs = refs[2 + 2 * n:]
        src, dst = outs[:n], outs[n:]
        x, y, c, _ = _place()
        for a in range(n):
            h = src[a].shape[1] // 2
            _remote(src[a].at[:, pl.ds((1 - c) * h, h)], dst[a], s_ref.at[a], r_ref.at[a], (x, y, 1 - c)).wait()

    return _pallas(
        body, name=f"grad_pair_wait_{g}", out_shape=[pltpu.HBM(a.shape, a.dtype) for a in gs + zones],
        in_specs=[SEM, SEM] + [HBM] * (2 * n), out_specs=[HBM] * (2 * n),
        input_output_aliases={2 + i: i for i in range(2 * n)},
        compiler_params=pltpu.CompilerParams(has_side_effects=EFFECT),
    )(send, recv, *gs, *zones)


def _pair_add(gs, others, c_arr):
    n = len(gs)

    def body(c_ref, *refs):
        for g_ref, o_ref, out_ref in zip(refs[:n], refs[n:2 * n], refs[2 * n:]):
            out_ref[...] = (g_ref[...].astype(F32) + o_ref[...].astype(F32)).astype(BF16)

    half = lambda a: pl.BlockSpec((1, a.shape[1] // 2, a.shape[2]), lambda k, c_ref: (k, c_ref[0], 0))
    whole = lambda a: pl.BlockSpec((1,) + a.shape[1:], lambda k, c_ref: (k, 0, 0))
    grid_spec = pltpu.PrefetchScalarGridSpec(
        num_scalar_prefetch=1, grid=(4,), in_specs=[half(a) for a in gs] + [whole(o) for o in others],
        out_specs=[whole(o) for o in others])
    return _pallas(body, name="grad_pair_add", grid_spec=grid_spec, out_shape=[_sds(o.shape, BF16) for o in others],
                   compiler_params=_cp(("parallel",)))(c_arr, *gs, *others)


def _chip_start(g, ss):
    n = len(ss)
    zones = [lax.empty((3,) + s.shape[1:], s.dtype) for s in ss]

    def body(*refs):
        send_sems, recv_sems = refs[2 * n], refs[2 * n + 1]
        src, dst = refs[2 * n + 2:3 * n + 2], refs[3 * n + 2:4 * n + 2]
        token = refs[4 * n + 2]
        x, y, c, chips = _place()
        for a in range(n):
            for jj, (px, py) in enumerate(chips):
                k = 3 * a + jj
                _remote(src[a].at[2 * px + py], dst[a].at[jj], send_sems.at[k], recv_sems.at[k], (px, py, c)).start()
        token[...] = jnp.zeros_like(token)

    return _pallas(
        body, name=f"grad_chip_start_{g}",
        out_shape=[pltpu.SemaphoreType.DMA((3 * n,)), pltpu.SemaphoreType.DMA((3 * n,))]
        + [pltpu.HBM(a.shape, a.dtype) for a in ss + zones] + [_sds((8, 128), F32)],
        in_specs=[HBM] * (2 * n), out_specs=[SEM, SEM] + [HBM] * (2 * n) + [pl.BlockSpec(memory_space=pltpu.VMEM)],
        input_output_aliases={i: 2 + i for i in range(2 * n)},
        compiler_params=pltpu.CompilerParams(has_side_effects=EFFECT),
    )(*[_in_hbm(a) for a in ss + zones])


def _chip_wait(sends, recvs, counts, ss, zones, after):
    nb, n = len(sends), len(ss)

    def body(*refs):
        s_refs, r_refs = refs[:nb], refs[nb:2 * nb]
        outs = refs[2 * nb + 2 * n + 1:]
        src, dst = outs[:n], outs[n:]
        x, y, c, chips = _place()
        a = 0
        for b in range(nb):
            for i in range(counts[b]):
                for jj, (px, py) in enumerate(chips):
                    k = 3 * i + jj
                    _remote(src[a].at[2 * px + py], dst[a].at[jj], s_refs[b].at[k], r_refs[b].at[k], (px, py, c)).wait()
                a += 1

    return _pallas(
        body, name="grad_chip_wait", out_shape=[pltpu.HBM(a.shape, a.dtype) for a in ss + zones],
        in_specs=[SEM] * (2 * nb) + [HBM] * (2 * n) + [ANY], out_specs=[HBM] * (2 * n),
        input_output_aliases={2 * nb + i: i for i in range(2 * n)},
        compiler_params=pltpu.CompilerParams(has_side_effects=EFFECT),
    )(*sends, *recvs, *ss, *zones, after)


def _chip_sum(s, r, where, dest, l, L):
    _, h, C = s.shape
    tr = h // 2

    def body(k_ref, s_ref, r_ref, *rest):
        out_ref = rest[-1]
        acc = s_ref[0].astype(F32)
        for jj in range(3):
            acc = acc + r_ref[jj].astype(F32)
        out_ref[...] = acc

    in_specs = [pl.BlockSpec((1, tr, C), lambda i, k_ref: (k_ref[0], i, 0)), pl.BlockSpec((3, tr, C), lambda i, k_ref: (0, i, 0))]
    args = [where, s, r]
    alias = {}
    if dest is not None:
        in_specs.append(ANY)
        args.append(dest)
        alias = {3: 0}
    grid_spec = pltpu.PrefetchScalarGridSpec(
        num_scalar_prefetch=1, grid=(2,), in_specs=in_specs,
        out_specs=pl.BlockSpec((None, tr, C), lambda i, k_ref: (l, 2 * k_ref[1] + i, 0)))
    return _pallas(body, name="grad_chip_sum", grid_spec=grid_spec, out_shape=_sds((L, 2 * h, C), F32),
                   input_output_aliases=alias, compiler_params=_cp(("arbitrary",)))(*args)


def _pair_share(bufs, layout):
    n = len(layout)
    n_out = len(bufs)

    def body(*refs):
        outs = refs[n_out:2 * n_out]
        send_sems, recv_sems = refs[2 * n_out:]
        x, y, c, _ = _place()
        cps = []
        for a, (o, l) in enumerate(layout):
            h = outs[o].shape[1] // 2
            blk = outs[o].at[l, pl.ds(c * h, h)]
            cps.append(_remote(blk, blk, send_sems.at[a], recv_sems.at[a], (x, y, 1 - c)))
        for cp in cps:
            cp.start()
        for a, (o, l) in enumerate(layout):
            h = outs[o].shape[1] // 2
            blk = outs[o].at[l, pl.ds((1 - c) * h, h)]
            _remote(blk, blk, send_sems.at[a], recv_sems.at[a], (x, y, 1 - c)).wait_recv()
        for cp in cps:
            cp.wait_send()

    return _pallas(body, name="grad_pair_share", in_specs=[ANY] * n_out, out_specs=[ANY] * n_out,
                   out_shape=[_sds(b.shape, b.dtype) for b in bufs], input_output_aliases={o: o for o in range(n_out)},
                   scratch_shapes=[pltpu.SemaphoreType.DMA((n,)), pltpu.SemaphoreType.DMA((n,))])(*bufs)


def _small_all_reduce(packed):
    P, L = packed.shape

    def body(in_ref, out_ref, slots, send_sems, recv_sems):
        x, y, c, _ = _place()
        me = 4 * x + 2 * y + c
        slots[me] = in_ref[...]
        cps = []
        for r in range(1, 8):
            px = 1 - x if r & 4 else x
            py = 1 - y if r & 2 else y
            pc = 1 - c if r & 1 else c
            cps.append(_remote(in_ref, slots.at[me], send_sems.at[r - 1], recv_sems.at[r - 1], (px, py, pc)))
        for cp in cps:
            cp.start()
        for r in range(1, 8):
            px = 1 - x if r & 4 else x
            py = 1 - y if r & 2 else y
            pc = 1 - c if r & 1 else c
            blk = slots.at[4 * px + 2 * py + pc]
            _remote(blk, blk, send_sems.at[r - 1], recv_sems.at[r - 1], (px, py, pc)).wait_recv()
        for cp in cps:
            cp.wait_send()
        acc = slots[0]
        for k in range(1, 8):
            acc = acc + slots[k]
        out_ref[...] = acc

    vm = pl.BlockSpec(memory_space=pltpu.VMEM)
    return _pallas(body, name="small_all_reduce", in_specs=[vm], out_specs=vm, out_shape=_sds((P, L), F32),
                   scratch_shapes=[pltpu.VMEM((8, P, L), F32), pltpu.SemaphoreType.DMA((7,)), pltpu.SemaphoreType.DMA((7,))])(packed)


def _adamw_math(w, g, m, v):
    m = ADAM_B1 * m + (1.0 - ADAM_B1) * g
    v = ADAM_B2 * v + (1.0 - ADAM_B2) * (g * g)
    m_hat = m / (1.0 - ADAM_B1 ** ADAM_STEP)
    v_hat = v / (1.0 - ADAM_B2 ** ADAM_STEP)
    delta = -ADAM_LR * (m_hat / (jnp.sqrt(v_hat) + ADAM_EPS) + ADAM_WD * w)
    return delta, m, v


def _adamw(w, g, m, v):
    shape = w.shape
    C = shape[-1]
    rows = math.prod(shape[:-1])
    tr = next(t for t in (512, 352, 256, 128, 64, 32, 16, 8, rows) if rows % t == 0)
    w2, g2, m2, v2 = (a.reshape(rows, C) for a in (w, g, m, v))

    def body(w_ref, g_ref, m_ref, v_ref, d_ref, nm_ref, nv_ref):
        d, nm, nv = _adamw_math(w_ref[...], g_ref[...], m_ref[...], v_ref[...])
        d_ref[...] = d
        nm_ref[...] = nm
        nv_ref[...] = nv

    blk = pl.BlockSpec((tr, C), lambda i: (i, 0))
    outs = _pallas(body, name="adamw", grid=(rows // tr,), in_specs=[blk] * 4, out_specs=[blk] * 3,
                   out_shape=[_sds((rows, C), F32)] * 3, compiler_params=_cp(("parallel",)))(w2, g2, m2, v2)
    return tuple(o.reshape(shape) for o in outs)


WEIGHTS = ["ffn1_norm", "ffn1_w_gate", "ffn1_w_up", "ffn1_w_down", "mix_norm", "ffn2_norm", "ffn2_w_gate", "ffn2_w_up",
           "ffn2_w_down", "ev_w_in", "ev_b_f", "ev_conv_w", "ev_conv_b", "ev_conv_norm", "ev_q_norm", "ev_k_norm",
           "ev_w_out", "od_w_in", "od_conv_w", "od_w_out"]
BIG = ([("ffn1_w_gate", 0), ("ffn1_w_up", 0), ("ffn1_w_down", 0), ("ev_w_in", 0), ("ev_w_out", 0),
        ("ffn2_w_gate", 0), ("ffn2_w_up", 0), ("ffn2_w_down", 0)]
       + [("ffn1_w_gate", 1), ("ffn1_w_up", 1), ("ffn1_w_down", 1), ("od_w_in", 0), ("od_w_out", 0),
          ("ffn2_w_gate", 1), ("ffn2_w_up", 1), ("ffn2_w_down", 1)])
TRANSPOSED = ("ffn1_w_gate", "ffn1_w_up", "ffn2_w_gate", "ffn2_w_up")
BLOCKS = [("ffn1", 0), ("ev", 0), ("ffn2", 0), ("ffn1", 1), ("od", 0), ("ffn2", 1)]
BLOCK_OF = {(name, l): (name.split("_w_")[0], l) for name, l in BIG}
BIG_NAMES = ["ffn1_w_gate", "ffn1_w_up", "ffn1_w_down", "ffn2_w_gate", "ffn2_w_up", "ffn2_w_down",
             "ev_w_in", "ev_w_out", "od_w_in", "od_w_out"]
SMALL = [("ffn1_norm", 16), ("mix_norm", 16), ("ffn2_norm", 16), ("ev_b_f", 8), ("ev_conv_w", 128), ("ev_conv_b", 8),
         ("ev_conv_norm", 8), ("ev_q_norm", 8), ("ev_k_norm", 8), ("od_conv_w", 24)]


def _to_lanes(a, rows):
    flat = a.reshape(-1)
    return jnp.pad(flat, (0, rows * 128 - flat.shape[0])).reshape(rows, 128)


def kernel(x, ffn1_norm, ffn1_w_gate, ffn1_w_up, ffn1_w_down, mix_norm, ffn2_norm, ffn2_w_gate, ffn2_w_up, ffn2_w_down, ev_w_in, ev_b_f, ev_conv_w, ev_conv_b, ev_conv_norm, ev_q_norm, ev_k_norm, ev_w_out, od_w_in, od_conv_w, od_w_out, loss_target, m_ffn1_norm, m_ffn1_w_gate, m_ffn1_w_up, m_ffn1_w_down, m_mix_norm, m_ffn2_norm, m_ffn2_w_gate, m_ffn2_w_up, m_ffn2_w_down, m_ev_w_in, m_ev_b_f, m_ev_conv_w, m_ev_conv_b, m_ev_conv_norm, m_ev_q_norm, m_ev_k_norm, m_ev_w_out, m_od_w_in, m_od_conv_w, m_od_w_out, v_ffn1_norm, v_ffn1_w_gate, v_ffn1_w_up, v_ffn1_w_down, v_mix_norm, v_ffn2_norm, v_ffn2_w_gate, v_ffn2_w_up, v_ffn2_w_down, v_ev_w_in, v_ev_b_f, v_ev_conv_w, v_ev_conv_b, v_ev_conv_norm, v_ev_q_norm, v_ev_k_norm, v_ev_w_out, v_od_w_in, v_od_conv_w, v_od_w_out):
    P = dict(ffn1_norm=ffn1_norm, ffn1_w_gate=ffn1_w_gate, ffn1_w_up=ffn1_w_up, ffn1_w_down=ffn1_w_down, mix_norm=mix_norm,
             ffn2_norm=ffn2_norm, ffn2_w_gate=ffn2_w_gate, ffn2_w_up=ffn2_w_up, ffn2_w_down=ffn2_w_down, ev_w_in=ev_w_in,
             ev_b_f=ev_b_f, ev_conv_w=ev_conv_w, ev_conv_b=ev_conv_b, ev_conv_norm=ev_conv_norm, ev_q_norm=ev_q_norm,
             ev_k_norm=ev_k_norm, ev_w_out=ev_w_out, od_w_in=od_w_in, od_conv_w=od_conv_w, od_w_out=od_w_out)
    M = dict(zip(WEIGHTS, [m_ffn1_norm, m_ffn1_w_gate, m_ffn1_w_up, m_ffn1_w_down, m_mix_norm, m_ffn2_norm, m_ffn2_w_gate,
                           m_ffn2_w_up, m_ffn2_w_down, m_ev_w_in, m_ev_b_f, m_ev_conv_w, m_ev_conv_b, m_ev_conv_norm,
                           m_ev_q_norm, m_ev_k_norm, m_ev_w_out, m_od_w_in, m_od_conv_w, m_od_w_out]))
    V = dict(zip(WEIGHTS, [v_ffn1_norm, v_ffn1_w_gate, v_ffn1_w_up, v_ffn1_w_down, v_mix_norm, v_ffn2_norm, v_ffn2_w_gate,
                           v_ffn2_w_up, v_ffn2_w_down, v_ev_w_in, v_ev_b_f, v_ev_conv_w, v_ev_conv_b, v_ev_conv_norm,
                           v_ev_q_norm, v_ev_k_norm, v_ev_w_out, v_od_w_in, v_od_conv_w, v_od_w_out]))
    for name in TRANSPOSED:
        P[name], M[name], V[name] = (jnp.swapaxes(a, 1, 2) for a in (P[name], M[name], V[name]))
    S, D = x.shape[1], x.shape[2]
    chip = 2 * lax.axis_index("x") + lax.axis_index("y")
    core = lax.axis_index("c")

    def own_slot(shard):
        return lax.dynamic_update_slice(lax.empty((4,) + shard.shape, shard.dtype), shard[None], (chip, 0, 0))

    taps = jnp.concatenate([_to_lanes(_pad_rows(ev_conv_w[0], 32), 32), _to_lanes(_pad_rows(od_conv_w[0], 8), 16)], axis=0)
    ici_send, ici_recv, *bufs = _ag_start([own_slot(P[name][l].astype(BF16)) for name, l in BIG] + [own_slot(taps)])
    cols = lambda a: a.transpose(1, 0, 2).reshape(a.shape[1], 4 * a.shape[2])
    W = {k: P[k] for k in ("ffn1_norm", "mix_norm", "ffn2_norm", "ev_b_f", "ev_q_norm", "ev_k_norm")}
    W["ev_conv_b"], W["ev_conv_norm"] = ev_conv_b, ev_conv_norm
    for tag in ("ffn1", "ffn2"):
        for kind in ("_w_gate", "_w_up", "_w_down"):
            W[tag + kind] = [None, None]
    passing = {}

    def pass_on(g, after):
        idx = [i for i, k in enumerate(BIG) if BLOCK_OF[k] == BLOCKS[g]]
        keys = [BIG[i] for i in idx] + (["taps"] if BLOCKS[g] == ("ev", 0) else [])
        passing[g] = (keys, _ag_mid(g, ici_send, ici_recv, [bufs[i] for i in idx], idx,
                                    bufs[-1] if BLOCKS[g] == ("ev", 0) else None, len(BIG), after))

    def need(block, after):
        g = BLOCKS.index(block)
        if g not in passing:
            pass_on(g, after)
        keys, (d_send, d_recv, *thru) = passing.pop(g)
        got = dict(zip(keys, _ag_wait(g, d_send, d_recv, thru, len(keys) - ("taps" in keys), after)))
        if 1 <= g < len(BLOCKS) - 1:
            pass_on(g + 1, after)
        for key, a in got.items():
            if key == "taps":
                continue
            name, l = key
            if name.startswith("ffn"):
                W[name][l] = a
            elif name.endswith("_w_in"):
                W[name] = cols(a)
            elif name.endswith("_w_out"):
                W[name] = a.reshape(4 * a.shape[1], D)
        if block == ("ev", 0):
            taps_all = got["taps"]
            W["ev_conv_w"] = cols(taps_all[:, :32].reshape(4, 32, 128))[:CONV_A_WIDTH]
            W["od_conv_w"] = cols(taps_all[:, 32:48].reshape(4, 8, 256))[:CONV_C_WIDTH]

    rows = lambda a: a.reshape(4, a.shape[0] // 4, a.shape[1])
    colsh = lambda a: a.reshape(a.shape[0], 4, a.shape[1] // 4).transpose(1, 0, 2)
    c_arr = core.reshape(1).astype(jnp.int32)
    where = jnp.stack([chip, core]).astype(jnp.int32)
    in_flight = []

    def done(block, block_grads):
        g = BLOCKS.index(block)
        keys = list(block_grads)
        gs = []
        for name, l in keys:
            a = block_grads[(name, l)]
            gs.append(colsh(a) if name == "ev_w_in" else rows(a) if name.endswith("_w_out") else a)
        for item in list(pairs):
            to_chips(item)
        send, recv, *rest = _pair_start(g, gs, chained.get("token"))
        n = len(keys)
        pairs.append((g, keys, send, recv, rest[:n], rest[n:2 * n]))
        if g == 0:
            to_chips(pairs[0])
        chained["token"] = rest[-1] if g else chained["token"]
        return chained["token"][0:1, 0:1]

    pairs, chained = [], {}

    def to_chips(item):
        pairs.remove(item)
        g, keys, send, recv, gs, zones = item
        n = len(keys)
        done_ = _pair_wait(g, send, recv, gs, zones)
        sums = list(_pair_add(list(done_[:n]), list(done_[n:]), c_arr))
        send2, recv2, *rest = _chip_start(g, sums)
        in_flight.append((keys, send2, recv2, rest[:n], rest[n:2 * n]))
        chained["token"] = rest[-1]

    loss, grad_x, grads = _local_step(x[0], loss_target[0], W, need, done)

    order = [k for keys, *_ in in_flight for k in keys]
    landed = _chip_wait([f[1] for f in in_flight], [f[2] for f in in_flight], [len(f[0]) for f in in_flight],
                        [a for f in in_flight for a in f[3]], [a for f in in_flight for a in f[4]], grad_x)
    sums, recvd = landed[:len(order)], landed[len(order):]
    stacked = {}
    for (name, l), s, r in zip(order, sums, recvd):
        stacked[name] = _chip_sum(s, r, where, stacked.get(name), l, P[name].shape[0])
    layout = [(BIG_NAMES.index(name), l) for name, l in order]
    big_grads = dict(zip(BIG_NAMES, _pair_share([stacked[name] for name in BIG_NAMES], layout)))

    def small_grad(name):
        if name.endswith("_norm") and name[:3] in ("ffn", "mix"):
            return jnp.concatenate([grads[(name, 0)], grads[(name, 1)]], axis=0)
        return grads[(name, 0)]

    packed = jnp.concatenate([_to_lanes(small_grad(name), r) for name, r in SMALL], axis=0)
    total = _small_all_reduce(packed)
    small_grads, at = {}, 0
    for name, r in SMALL:
        part = total[at:at + r].reshape(-1)
        at += r
        if name == "ev_conv_w":
            full_g = part[:CONV_A_WIDTH * D_CONV].reshape(CONV_A_WIDTH, D_CONV)
            small_grads[name] = lax.dynamic_slice_in_dim(full_g, chip * (D_CONV // 4), D_CONV // 4, axis=1)[None]
        elif name == "od_conv_w":
            full_g = part[:CONV_C_WIDTH * D].reshape(CONV_C_WIDTH, D)
            small_grads[name] = lax.dynamic_slice_in_dim(full_g, chip * (D // 4), D // 4, axis=1)[None]
        else:
            small_grads[name] = part[:math.prod(P[name].shape)].reshape(P[name].shape)

    grad_w, delta_w, new_m, new_v = [], [], [], []
    for name in WEIGHTS:
        g = big_grads[name] if name in big_grads else small_grads[name]
        outs = (g,) + _adamw(P[name], g, M[name], V[name])
        if name in TRANSPOSED:
            outs = tuple(jnp.swapaxes(a, 1, 2) for a in outs)
        for acc, a in zip((grad_w, delta_w, new_m, new_v), outs):
            acc.append(a)
    loss_all = lax.psum(loss[0, 0], ("x", "y", "c"))
    return (loss_all, grad_x[None], *grad_w, *delta_w, *new_m, *new_v)
```

```python
import functools
import math

import jax
import jax.numpy as jnp
from jax import lax
from jax.experimental import pallas as pl
from jax.experimental.pallas import tpu as pltpu

F32, BF16 = jnp.float32, jnp.bfloat16
EPS = 1e-6
FFN_RES = 0.5
N_HEADS, HEAD_DIM = 8, 64
D_CONV = 512
D_ATTN = N_HEADS * HEAD_DIM
CONV_A_WIDTH, CONV_C_WIDTH = 31, 3
ADAM_LR, ADAM_B1, ADAM_B2, ADAM_EPS, ADAM_WD, ADAM_STEP = 0.001, 0.9, 0.999, 1e-08, 0.01, 10
MESH = pl.DeviceIdType.MESH
ANY = pl.BlockSpec(memory_space=pl.ANY)

TOK_TILE = 512
FFN_TILE = 1024
DW_TILE = 1024
ATT_TILE = 1024
QKN_TILE = 2048
HALO_A, HALO_C = 32, 16
SUBLANES = 8
CONV_ROWS = 64
SCAN_BLK = 256
MIB = 2 ** 20


def _pallas(body, **kw):
    return pl.pallas_call(body, **kw)


def _cp(sem=None, vmem_mib=48):
    return pltpu.CompilerParams(dimension_semantics=sem, vmem_limit_bytes=vmem_mib * MIB)


def _dot(a, b):
    return jnp.dot(a, b, preferred_element_type=F32)


def _dot_nt(a, b):
    return lax.dot_general(a, b, (((1,), (1,)), ((), ())), preferred_element_type=F32)


def _dot_tn(a, b):
    return lax.dot_general(a, b, (((0,), (0,)), ((), ())), preferred_element_type=F32)


def _sds(shape, dtype):
    return jax.ShapeDtypeStruct(shape, dtype)


def _rms(x):
    return lax.rsqrt(jnp.mean(x * x, axis=-1, keepdims=True) + EPS)


def _rms_bwd(dy, x, g):
    r = _rms(x)
    xh = x * r
    dxh = dy * g
    dx = r * (dxh - xh * jnp.mean(dxh * xh, axis=-1, keepdims=True))
    return dx, xh


def _silu_grad(z):
    s = jax.nn.sigmoid(z)
    return s * (1.0 + z * (1.0 - s))


def _ffn_fwd(x, g, wg, wu, wd):
    S, D = x.shape
    nc, Fs, _ = wd.shape
    tm = min(FFN_TILE, S)

    def body(x_ref, g_ref, wg_ref, wu_ref, wd_ref, out_ref, xn_ref, G_ref, U_ref, acc_ref):
        j = pl.program_id(1)

        @pl.when(j == 0)
        def _():
            xv = x_ref[...]
            xn_ref[...] = (xv * _rms(xv) * g_ref[...]).astype(BF16)
            acc_ref[...] = jnp.zeros_like(acc_ref)

        xn = xn_ref[...]
        G = _dot_nt(xn, wg_ref[0])
        U = _dot_nt(xn, wu_ref[0])
        G_ref[0] = G.astype(BF16)
        U_ref[0] = U.astype(BF16)
        H = (G * jax.nn.sigmoid(G) * U).astype(BF16)
        acc_ref[...] += _dot(H, wd_ref[0])

        @pl.when(j == nc - 1)
        def _():
            out_ref[...] = x_ref[...] + FFN_RES * acc_ref[...]

    row = pl.BlockSpec((tm, D), lambda i, j: (i, 0))
    return _pallas(
        body, name="ffn_fwd", grid=(S // tm, nc),
        in_specs=[row, pl.BlockSpec((1, D), lambda i, j: (0, 0)),
                  pl.BlockSpec((1, Fs, D), lambda i, j: (j, 0, 0)), pl.BlockSpec((1, Fs, D), lambda i, j: (j, 0, 0)),
                  pl.BlockSpec((1, Fs, D), lambda i, j: (j, 0, 0))],
        out_specs=[row, row, pl.BlockSpec((1, tm, Fs), lambda i, j: (j, i, 0)),
                   pl.BlockSpec((1, tm, Fs), lambda i, j: (j, i, 0))],
        out_shape=[_sds((S, D), F32), _sds((S, D), BF16), _sds((nc, S, Fs), BF16), _sds((nc, S, Fs), BF16)],
        scratch_shapes=[pltpu.VMEM((tm, D), F32)],
        compiler_params=_cp(("parallel", "arbitrary"), 56),
    )(x, g, wg, wu, wd)


def _ffn_bwd_w(dout, xn, G, U, wd):
    S, D = dout.shape
    nc, _, Fs = G.shape
    tm = min(DW_TILE, S)
    nt = S // tm
    sub = min(TOK_TILE, tm)

    def body(do_ref, xn_ref, G_ref, U_ref, wd_ref, dwg_ref, dwu_ref, dwd_ref, dG_ref, dU_ref, ag, au, ad, do_s, H_s):
        i = pl.program_id(1)

        @pl.when(i == 0)
        def _():
            ag[...] = jnp.zeros_like(ag)
            au[...] = jnp.zeros_like(au)
            ad[...] = jnp.zeros_like(ad)

        for r in range(0, tm, sub):
            rows = pl.ds(r, sub)
            do = (FFN_RES * do_ref[rows, :]).astype(BF16)
            do_s[rows, :] = do
            Gv = G_ref[0, rows, :].astype(F32)
            Uv = U_ref[0, rows, :].astype(F32)
            dH = _dot_nt(do, wd_ref[0])
            sg = jax.nn.sigmoid(Gv)
            act = Gv * sg
            H_s[rows, :] = (act * Uv).astype(BF16)
            dU_ref[0, rows, :] = (dH * act).astype(BF16)
            dG_ref[0, rows, :] = (dH * Uv * (sg * (1.0 + Gv * (1.0 - sg)))).astype(BF16)
        xnv = xn_ref[...]
        ag[...] += _dot_tn(dG_ref[0], xnv)
        au[...] += _dot_tn(dU_ref[0], xnv)
        ad[...] += _dot_tn(H_s[...], do_s[...])

        @pl.when(i == nt - 1)
        def _():
            dwg_ref[0] = ag[...].astype(BF16)
            dwu_ref[0] = au[...].astype(BF16)
            dwd_ref[0] = ad[...].astype(BF16)

    row = pl.BlockSpec((tm, D), lambda j, i: (i, 0))
    hid = pl.BlockSpec((1, tm, Fs), lambda j, i: (j, i, 0))
    wrow = pl.BlockSpec((1, Fs, D), lambda j, i: (j, 0, 0))
    return _pallas(
        body, name="ffn_bwd_w", grid=(nc, nt),
        in_specs=[row, row, hid, hid, wrow],
        out_specs=[wrow, wrow, wrow, hid, hid],
        out_shape=[_sds((nc, Fs, D), BF16)] * 3 + [_sds((nc, S, Fs), BF16)] * 2,
        scratch_shapes=[pltpu.VMEM((Fs, D), F32)] * 3 + [pltpu.VMEM((tm, D), BF16), pltpu.VMEM((tm, Fs), BF16)],
        compiler_params=_cp(("parallel", "arbitrary"), 56),
    )(dout, xn, G, U, wd)


def _norm_in_bwd(dzs, ws, x, g, dres, w_rows=False):
    S, D = x.shape
    nc = dzs[0].shape[0]
    n = len(dzs)
    tm = TOK_TILE

    def body(*refs):
        dz_refs, w_refs = refs[:n], refs[n:2 * n]
        x_ref, g_ref, dres_ref, dx_ref, dg_ref, acc_ref = refs[2 * n:]
        i, j = pl.program_id(0), pl.program_id(1)

        @pl.when(j == 0)
        def _():
            acc_ref[...] = jnp.zeros_like(acc_ref)

        @pl.when((i == 0) & (j == 0))
        def _():
            dg_ref[...] = jnp.zeros_like(dg_ref)

        for dz_ref, w_ref in zip(dz_refs, w_refs):
            acc_ref[...] += _dot(dz_ref[0], w_ref[0]) if w_rows else _dot_nt(dz_ref[0], w_ref[0])

        @pl.when(j == nc - 1)
        def _():
            dxn = acc_ref[...]
            dx, xh = _rms_bwd(dxn, x_ref[...], g_ref[...])
            dx_ref[...] = dx + dres_ref[...]
            dg_ref[...] += jnp.sum(dxn * xh, axis=0, keepdims=True)

    row = pl.BlockSpec((tm, D), lambda i, j: (i, 0))
    one = pl.BlockSpec((1, D), lambda i, j: (0, 0))
    in_specs = [pl.BlockSpec((1, tm, dz.shape[2]), lambda i, j: (j, i, 0)) for dz in dzs]
    in_specs += [pl.BlockSpec((1,) + w.shape[1:], lambda i, j: (j, 0, 0)) for w in ws]
    return _pallas(
        body, name="norm_in_bwd", grid=(S // tm, nc),
        in_specs=in_specs + [row, one, row], out_specs=[row, one],
        out_shape=[_sds((S, D), F32), _sds((1, D), F32)],
        scratch_shapes=[pltpu.VMEM((tm, D), F32)],
        compiler_params=_cp(("arbitrary", "arbitrary")),
    )(*dzs, *ws, x, g, dres)


def _norm_proj(x, g, w, w2=None):
    S, D = x.shape
    N = w.shape[1]
    tm = TOK_TILE

    def body(*refs):
        if w2 is None:
            x_ref, g_ref, w_ref, h_ref, z_ref = refs
        else:
            x_ref, g_ref, w_ref, w2_ref, h_ref, z_ref, z2_ref = refs
        xv = x_ref[...]
        h = (xv * _rms(xv) * g_ref[...]).astype(BF16)
        h_ref[...] = h
        z_ref[...] = _dot(h, w_ref[...]).astype(BF16)
        if w2 is not None:
            z2_ref[...] = _dot(h, w2_ref[...])

    row = pl.BlockSpec((tm, D), lambda i: (i, 0))
    in_specs = [row, pl.BlockSpec((1, D), lambda i: (0, 0)), pl.BlockSpec((D, N), lambda i: (0, 0))]
    out_specs = [row, pl.BlockSpec((tm, N), lambda i: (i, 0))]
    out_shape = [_sds((S, D), BF16), _sds((S, N), BF16)]
    args = [x, g, w]
    if w2 is not None:
        N2 = w2.shape[1]
        in_specs.append(pl.BlockSpec((D, N2), lambda i: (0, 0)))
        out_specs.append(pl.BlockSpec((tm, N2), lambda i: (i, 0)))
        out_shape.append(_sds((S, N2), F32))
        args.append(w2)
    return _pallas(body, name="norm_proj", grid=(S // tm,), in_specs=in_specs, out_specs=out_specs,
                   out_shape=out_shape, compiler_params=_cp(("parallel",)))(*args)


def _proj_res(acts, ws, res):
    S, D = res.shape
    n = len(acts)
    tm = TOK_TILE

    def body(*refs):
        a_refs, w_refs = refs[:n], refs[n:2 * n]
        res_ref, out_ref = refs[2 * n:]
        acc = res_ref[...]
        for a_ref, w_ref in zip(a_refs, w_refs):
            acc = acc + _dot(a_ref[...], w_ref[...])
        out_ref[...] = acc

    row = pl.BlockSpec((tm, D), lambda i: (i, 0))
    in_specs = [pl.BlockSpec((tm, a.shape[1]), lambda i: (i, 0)) for a in acts]
    in_specs += [pl.BlockSpec(w.shape, lambda i: (0, 0)) for w in ws]
    return _pallas(body, name="proj_res", grid=(S // tm,), in_specs=in_specs + [row], out_specs=row,
                   out_shape=_sds((S, D), F32), compiler_params=_cp(("parallel",)))(*acts, *ws, res)


def _matmul_nt(a, w, after=None):
    S, K = a.shape
    M = w.shape[0]
    tm = TOK_TILE

    def body(a_ref, w_ref, *rest):
        rest[-1][...] = _dot_nt(a_ref[...].astype(BF16), w_ref[...])

    extra = [] if after is None else [after]
    return _pallas(body, name="matmul_nt", grid=(S // tm,),
                   in_specs=[pl.BlockSpec((tm, K), lambda i: (i, 0)), pl.BlockSpec((M, K), lambda i: (0, 0))] + [ANY] * len(extra),
                   out_specs=pl.BlockSpec((tm, M), lambda i: (i, 0)), out_shape=_sds((S, M), F32),
                   compiler_params=_cp(("parallel",)))(a, w, *extra)


def _matmul_tn(a, b, tn):
    S, M = a.shape
    N = b.shape[1]
    tm = min(DW_TILE, S)
    nt = S // tm

    def body(a_ref, b_ref, o_ref, acc_ref):
        i = pl.program_id(1)

        @pl.when(i == 0)
        def _():
            acc_ref[...] = jnp.zeros_like(acc_ref)

        acc_ref[...] += _dot_tn(a_ref[...].astype(BF16), b_ref[...].astype(BF16))

        @pl.when(i == nt - 1)
        def _():
            o_ref[0] = acc_ref[...].astype(BF16)

    return _pallas(body, name="matmul_tn", grid=(N // tn, nt),
                   in_specs=[pl.BlockSpec((tm, M), lambda j, i: (i, 0)), pl.BlockSpec((tm, tn), lambda j, i: (i, j))],
                   out_specs=pl.BlockSpec((1, M, tn), lambda j, i: (j, 0, 0)), out_shape=_sds((N // tn, M, tn), BF16),
                   scratch_shapes=[pltpu.VMEM((M, tn), F32)],
                   compiler_params=_cp(("parallel", "arbitrary")))(a, b)


def _fill_shifts(win, rows):
    for b in range(1, SUBLANES):
        win[b, pl.ds(0, rows - SUBLANES), :] = win[0, pl.ds(b, rows - SUBLANES), :]


def _tap(win, offset, n, base=0):
    start = base + (offset - offset % SUBLANES)
    if not isinstance(start, int):
        start = pl.multiple_of(start, SUBLANES)
    return win[offset % SUBLANES, pl.ds(start, n), :]


def _conv_a_fwd(z, cw, cb, cn):
    S = z.shape[0]
    C = D_CONV
    tm = TOK_TILE
    hb = tm // HALO_A

    def body(u_ref, gt_ref, up_ref, gp_ref, cw_ref, cb_ref, cn_ref, a_ref, a1_ref, win):
        i = pl.program_id(0)
        prev = up_ref[...].astype(F32) * jax.nn.sigmoid(gp_ref[...].astype(F32))
        win[0, pl.ds(0, HALO_A), :] = jnp.where(i == 0, 0.0, prev)
        win[0, pl.ds(HALO_A, tm), :] = u_ref[...].astype(F32) * jax.nn.sigmoid(gt_ref[...].astype(F32))
        _fill_shifts(win, tm + HALO_A)

        acc = jnp.zeros((tm, C), F32)
        for k in range(CONV_A_WIDTH):
            acc = acc + cw_ref[k:k + 1, :] * _tap(win, HALO_A - (CONV_A_WIDTH - 1) + k, tm)
        a1 = acc + cb_ref[...]
        a1_ref[...] = a1
        a2 = a1 * _rms(a1) * cn_ref[...]
        a_ref[...] = (a2 * jax.nn.sigmoid(a2)).astype(BF16)

    cur = lambda c: pl.BlockSpec((tm, C), lambda i, c=c: (i, c))
    prv = lambda c: pl.BlockSpec((HALO_A, C), lambda i, c=c: (jnp.maximum(i * hb - 1, 0), c))
    vec = pl.BlockSpec((1, C), lambda i: (0, 0))
    return _pallas(body, name="conv_a_fwd", grid=(S // tm,),
                   in_specs=[cur(0), cur(1), prv(0), prv(1), pl.BlockSpec((32, C), lambda i: (0, 0)), vec, vec],
                   out_specs=[pl.BlockSpec((tm, C), lambda i: (i, 0)), pl.BlockSpec((tm, C), lambda i: (i, 0))],
                   out_shape=[_sds((S, C), BF16), _sds((S, C), F32)],
                   scratch_shapes=[pltpu.VMEM((SUBLANES, tm + HALO_A, C), F32)],
                   compiler_params=_cp(("parallel",)))(z, z, z, z, cw, cb, cn)


def _conv_a_bwd(da, a1, z, cw, cn):
    S = z.shape[0]
    C = D_CONV
    tm = TOK_TILE
    hb = tm // HALO_A
    nt = S // tm
    W = CONV_A_WIDTH

    def body(da_ref, a1_ref, dan_ref, a1n_ref, u_ref, gt_ref, up_ref, gp_ref, cw_ref, cn_ref,
             duz_ref, dcw_ref, dcb_ref, dcn_ref, win, dwin):
        i = pl.program_id(0)

        @pl.when(i == 0)
        def _():
            dcw_ref[...] = jnp.zeros_like(dcw_ref)
            dcb_ref[...] = jnp.zeros_like(dcb_ref)
            dcn_ref[...] = jnp.zeros_like(dcn_ref)

        cnv = cn_ref[...]

        def da1_of(dav, a1v):
            a2 = a1v * _rms(a1v) * cnv
            da2 = dav * _silu_grad(a2)
            dx, xh = _rms_bwd(da2, a1v, cnv)
            return dx, da2 * xh

        da1, dcn_t = da1_of(da_ref[...], a1_ref[...])
        da1n, _ = da1_of(dan_ref[...], a1n_ref[...])
        dwin[0, pl.ds(0, tm), :] = da1
        dwin[0, pl.ds(tm, HALO_A), :] = jnp.where(i == nt - 1, 0.0, da1n)
        _fill_shifts(dwin, tm + HALO_A)
        dcb_ref[...] += jnp.sum(da1, axis=0, keepdims=True)
        dcn_ref[...] += jnp.sum(dcn_t, axis=0, keepdims=True)

        prev = up_ref[...].astype(F32) * jax.nn.sigmoid(gp_ref[...].astype(F32))
        win[0, pl.ds(0, HALO_A), :] = jnp.where(i == 0, 0.0, prev)
        win[0, pl.ds(HALO_A, tm), :] = u_ref[...].astype(F32) * jax.nn.sigmoid(gt_ref[...].astype(F32))
        _fill_shifts(win, tm + HALO_A)

        def rows_block(rb, carry):
            r0 = pl.multiple_of(rb * CONV_ROWS, CONV_ROWS)
            rows = pl.ds(r0, CONV_ROWS)
            da1_b = dwin[0, rows, :]
            da0 = jnp.zeros((CONV_ROWS, C), F32)
            for k in range(W):
                da0 = da0 + cw_ref[k:k + 1, :] * _tap(dwin, W - 1 - k, CONV_ROWS, r0)
                dcw_ref[k:k + 1, :] += jnp.sum(da1_b * _tap(win, HALO_A - (W - 1) + k, CONV_ROWS, r0), axis=0, keepdims=True)
            u = u_ref[rows, :].astype(F32)
            sg = jax.nn.sigmoid(gt_ref[rows, :].astype(F32))
            duz_ref[rows, 0:C] = (da0 * sg).astype(BF16)
            duz_ref[rows, C:2 * C] = (da0 * u * sg * (1.0 - sg)).astype(BF16)
            return carry

        lax.fori_loop(0, tm // CONV_ROWS, rows_block, 0)

    cur = lambda c: pl.BlockSpec((tm, C), lambda i, c=c: (i, c))
    prv = lambda c: pl.BlockSpec((HALO_A, C), lambda i, c=c: (jnp.maximum(i * hb - 1, 0), c))
    nxt = pl.BlockSpec((HALO_A, C), lambda i: (jnp.minimum((i + 1) * hb, S // HALO_A - 1), 0))
    vec = pl.BlockSpec((1, C), lambda i: (0, 0))
    return _pallas(body, name="conv_a_bwd", grid=(nt,),
                   in_specs=[cur(0), cur(0), nxt, nxt, cur(0), cur(1), prv(0), prv(1),
                             pl.BlockSpec((32, C), lambda i: (0, 0)), vec],
                   out_specs=[pl.BlockSpec((tm, 2 * C), lambda i: (i, 0)), pl.BlockSpec((32, C), lambda i: (0, 0)), vec, vec],
                   out_shape=[_sds((S, 2 * C), BF16), _sds((32, C), F32), _sds((1, C), F32), _sds((1, C), F32)],
                   scratch_shapes=[pltpu.VMEM((SUBLANES, tm + HALO_A, C), F32)] * 2,
                   compiler_params=_cp(("arbitrary",)))(da, a1, da, a1, z, z, z, z, cw, cn)


def _forget_scan(fl, bf):
    S, L = fl.shape
    B = SCAN_BLK

    def body(fl_ref, bf_ref, flb_ref, F_ref):
        tri = (lax.broadcasted_iota(jnp.int32, (B, B), 0) >= lax.broadcasted_iota(jnp.int32, (B, B), 1)).astype(F32)

        def step(c, carry):
            rows = pl.ds(pl.multiple_of(c * B, B), B)
            v = fl_ref[rows, :] + bf_ref[...]
            flb_ref[rows, :] = v
            lf = jnp.minimum(v, 0.0) - jnp.log1p(jnp.exp(-jnp.abs(v)))
            cs = jnp.dot(tri, lf, precision=lax.Precision.HIGHEST, preferred_element_type=F32) + carry
            F_ref[rows, :] = cs
            return cs[B - 1:B, :]

        lax.fori_loop(0, S // B, step, jnp.zeros((1, L), F32))

    return _pallas(body, name="forget_scan", out_shape=[_sds((S, L), F32), _sds((S, L), F32)],
                   compiler_params=_cp())(fl, bf)


def _forget_scan_bwd(dF, flb):
    S, L = dF.shape
    B = SCAN_BLK
    nb = S // B

    def body(dF_ref, flb_ref, dfl_ref, db_ref):
        tri = (lax.broadcasted_iota(jnp.int32, (B, B), 0) <= lax.broadcasted_iota(jnp.int32, (B, B), 1)).astype(F32)

        def step(t, carry):
            carry_cs, db = carry
            rows = pl.ds(pl.multiple_of((nb - 1 - t) * B, B), B)
            cs = jnp.dot(tri, dF_ref[rows, :], precision=lax.Precision.HIGHEST, preferred_element_type=F32) + carry_cs
            dfl = cs * jax.nn.sigmoid(-flb_ref[rows, :])
            dfl_ref[rows, :] = dfl
            return cs[0:1, :], db + jnp.sum(dfl, axis=0, keepdims=True)

        _, db = lax.fori_loop(0, nb, step, (jnp.zeros((1, L), F32), jnp.zeros((1, L), F32)))
        db_ref[...] = db

    return _pallas(body, name="forget_scan_bwd", out_shape=[_sds((S, L), F32), _sds((1, L), F32)],
                   compiler_params=_cp())(dF, flb)


NEG = -1e30


def _causal_mask(t):
    return lax.broadcasted_iota(jnp.int32, (t, t), 0) >= lax.broadcasted_iota(jnp.int32, (t, t), 1)


AUG = 128
C_F, C_ONE, C_LSE = 64, 67, 70


def _split3(f):
    a = f.astype(BF16).astype(F32)
    r = f - a
    b = r.astype(BF16).astype(F32)
    return a, b, r - b


def _put3(lane, base, parts, other):
    out = other
    for k, p in enumerate(parts):
        out = jnp.where(lane == base + k, p, out)
    return out


def _ones3(lane, base):
    return (lane >= base) & (lane < base + 3)


def _lane_ids(rows):
    return lax.broadcasted_iota(jnp.int32, (rows, AUG), 1)


def _pair_rms(x, lo):
    sq = x * x
    ms_a = jnp.sum(jnp.where(lo, sq, 0.0), axis=-1, keepdims=True) * (1.0 / HEAD_DIM)
    ms_b = jnp.sum(jnp.where(lo, 0.0, sq), axis=-1, keepdims=True) * (1.0 / HEAD_DIM)
    return jnp.where(lo, lax.rsqrt(ms_a + EPS), lax.rsqrt(ms_b + EPS))


def _qkv_prep(z, Fc, qw, kw):
    S = z.shape[0]
    tp = min(QKN_TILE, S)
    scale = 1.0 / math.sqrt(HEAD_DIM)

    def body(zq_ref, zk_ref, zv_ref, F_ref, qw_ref, kw_ref, q_ref, k_ref, v_ref):
        j = pl.program_id(0)
        lane = _lane_ids(tp)
        lo = lane < HEAD_DIM
        Fv = F_ref[...]
        xq = zq_ref[...].astype(F32)
        xk = zk_ref[...].astype(F32)
        qn = xq * _pair_rms(xq, lo) * qw_ref[...] * scale
        kn = xk * _pair_rms(xk, lo) * kw_ref[...]
        vv = zv_ref[...].astype(F32)
        for half in range(2):
            take = (lambda a: a) if half == 0 else (lambda a: pltpu.roll(a, HEAD_DIM, 1))
            fp = _split3(jnp.sum(jnp.where(lane == 2 * j + half, Fv, 0.0), axis=-1, keepdims=True))
            qx = _put3(lane, C_F, fp, jnp.where(_ones3(lane, C_ONE), 1.0, 0.0))
            kx = _put3(lane, C_ONE, [-p for p in fp], jnp.where(_ones3(lane, C_F) | _ones3(lane, C_LSE), 1.0, 0.0))
            vx = jnp.where(_ones3(lane, C_F), 1.0, 0.0)
            q_ref[half] = jnp.where(lo, take(qn), qx).astype(BF16)
            k_ref[half] = jnp.where(lo, take(kn), kx).astype(BF16)
            v_ref[half] = jnp.where(lo, take(vv), vx).astype(BF16)

    col = lambda c0: pl.BlockSpec((tp, AUG), lambda j, i, c0=c0: (i, c0 + j))
    vec = pl.BlockSpec((1, AUG), lambda j, i: (0, 0))
    out = pl.BlockSpec((2, tp, AUG), lambda j, i: (j, i, 0))
    return _pallas(body, name="qkv_prep", grid=(N_HEADS // 2, S // tp),
                   in_specs=[col(8), col(12), col(16), pl.BlockSpec((tp, AUG), lambda j, i: (i, 0)), vec, vec],
                   out_specs=[out, out, out], out_shape=[_sds((N_HEADS, S, AUG), BF16)] * 3,
                   compiler_params=_cp(("parallel", "parallel")))(z, z, z, Fc, qw, kw)


def _fox_fwd(q_aug, k_aug, v_aug):
    H, S, A = q_aug.shape
    t = ATT_TILE
    nq = S // t

    def body(q_ref, k_ref, v_ref, o_ref, q2_ref):
        i = pl.program_id(1)
        q = q_ref[0]

        def tile(j, carry, diag):
            m, acc = carry
            rows = pl.ds(pl.multiple_of(j * t, t), t)
            s = _dot_nt(q, k_ref[0, rows, :])
            if diag:
                s = jnp.where(_causal_mask(t), s, NEG)
            m_new = jnp.maximum(m, jnp.max(s, axis=-1, keepdims=True))
            p = jnp.exp(s - m_new)
            acc = jnp.exp(m - m_new) * acc + _dot(p.astype(BF16), v_ref[0, rows, :])
            return m_new, acc

        init = (jnp.full((t, 1), NEG, F32), jnp.zeros((t, A), F32))
        carry = lax.fori_loop(0, i, lambda j, c: tile(j, c, False), init)
        m, acc = tile(i, carry, True)
        lane = _lane_ids(t)
        l = jnp.sum(jnp.where(lane == C_F, acc, 0.0), axis=-1, keepdims=True)
        o_ref[0] = (acc / l).astype(BF16)
        lse = m + jnp.log(l)
        q2_ref[0] = (q.astype(F32) + _put3(lane, C_LSE, [-p for p in _split3(lse)], 0.0)).astype(BF16)

    qblk = pl.BlockSpec((1, t, A), lambda h, i: (h, i, 0))
    full = pl.BlockSpec((1, S, A), lambda h, i: (h, 0, 0))
    return _pallas(body, name="fox_fwd", grid=(H, nq), in_specs=[qblk, full, full], out_specs=[qblk, qblk],
                   out_shape=[_sds((H, S, A), BF16)] * 2, compiler_params=_cp(("parallel", "parallel")))(q_aug, k_aug, v_aug)


def _do_prep(dcat, o_aug):
    S = dcat.shape[0]
    tp = min(QKN_TILE, S)

    def body(d_ref, o_ref, out_ref):
        lane = _lane_ids(tp)
        lo = lane < HEAD_DIM
        x = d_ref[...]
        for half in range(2):
            d = jnp.where(lo, x if half == 0 else pltpu.roll(x, HEAD_DIM, 1), 0.0)
            delta = jnp.sum(d * o_ref[half].astype(F32), axis=-1, keepdims=True)
            out_ref[half] = jnp.where(lo, d, _put3(lane, C_F, [-p for p in _split3(delta)], 0.0)).astype(BF16)

    pair = pl.BlockSpec((2, tp, AUG), lambda j, i: (j, i, 0))
    return _pallas(body, name="do_prep", grid=(N_HEADS // 2, S // tp),
                   in_specs=[pl.BlockSpec((tp, AUG), lambda j, i: (i, D_CONV // AUG + j)), pair], out_specs=pair,
                   out_shape=_sds((N_HEADS, S, AUG), BF16), compiler_params=_cp(("parallel", "parallel")))(dcat, o_aug)


def _fox_bwd(q2, k_aug, v_aug, do_aug):
    H, S, A = q2.shape
    t = ATT_TILE
    nq = S // t

    def body(q_ref, k_ref, v_ref, do_ref, dq_ref, dk_ref, dv_ref):
        j = pl.program_id(1)

        @pl.when(j == 0)
        def _():
            dq_ref[...] = jnp.zeros_like(dq_ref)

        k = k_ref[0]
        vv = v_ref[0]

        def tile(i, carry, diag):
            dk, dv = carry
            rows = pl.ds(pl.multiple_of(i * t, t), t)
            q = q_ref[0, rows, :]
            dov = do_ref[0, rows, :]
            s = _dot_nt(q, k)
            if diag:
                s = jnp.where(_causal_mask(t), s, NEG)
            p = jnp.exp(s)
            dv = dv + _dot_tn(p.astype(BF16), dov)
            dsb = (p * _dot_nt(dov, vv)).astype(BF16)
            dq_ref[0, rows, :] += _dot(dsb, k)
            dk = dk + _dot_tn(dsb, q)
            return dk, dv

        init = (jnp.zeros((t, A), F32), jnp.zeros((t, A), F32))
        carry = tile(j, init, True)
        dk, dv = lax.fori_loop(j + 1, nq, lambda i, c: tile(i, c, False), carry)
        dk_ref[0] = dk
        dv_ref[0] = dv

    full = pl.BlockSpec((1, S, A), lambda h, j: (h, 0, 0))
    kblk = pl.BlockSpec((1, t, A), lambda h, j: (h, j, 0))
    return _pallas(body, name="fox_bwd", grid=(H, nq), in_specs=[full, kblk, kblk, full], out_specs=[full, kblk, kblk],
                   out_shape=[_sds((H, S, A), F32)] * 3,
                   compiler_params=_cp(("parallel", "arbitrary")))(q2, k_aug, v_aug, do_aug)


def _qkv_bwd(dq, dk, dv, z, qw, kw):
    S = z.shape[0]
    tp = min(QKN_TILE, S)
    scale = 1.0 / math.sqrt(HEAD_DIM)

    def body(dq_ref, dk_ref, dv_ref, zq_ref, zk_ref, qw_ref, kw_ref, dqf_ref, dkf_ref, dvf_ref, dF_ref, dqw_ref, dkw_ref):
        i, j = pl.program_id(0), pl.program_id(1)
        lane = _lane_ids(tp)
        lo = lane < HEAD_DIM

        @pl.when((i == 0) & (j == 0))
        def _():
            dqw_ref[...] = jnp.zeros_like(dqw_ref)
            dkw_ref[...] = jnp.zeros_like(dkw_ref)

        def pair(ref):
            return jnp.where(lo, ref[0], pltpu.roll(ref[1], HEAD_DIM, 1))

        def norm_bwd(g, x, w):
            r = _pair_rms(x, lo)
            xh = x * r
            dxh = g * w
            tt = dxh * xh
            mean_a = jnp.sum(jnp.where(lo, tt, 0.0), axis=-1, keepdims=True) * (1.0 / HEAD_DIM)
            mean_b = jnp.sum(jnp.where(lo, 0.0, tt), axis=-1, keepdims=True) * (1.0 / HEAD_DIM)
            return r * (dxh - xh * jnp.where(lo, mean_a, mean_b)), g * xh

        dxq, gq = norm_bwd(pair(dq_ref) * scale, zq_ref[...].astype(F32), qw_ref[...])
        dqf_ref[...] = dxq.astype(BF16)
        dqw_ref[...] += jnp.sum(gq, axis=0, keepdims=True)
        dxk, gk = norm_bwd(pair(dk_ref), zk_ref[...].astype(F32), kw_ref[...])
        dkf_ref[...] = dxk.astype(BF16)
        dkw_ref[...] += jnp.sum(gk, axis=0, keepdims=True)
        dvf_ref[...] = pair(dv_ref).astype(BF16)

        contrib = jnp.zeros((tp, AUG), F32)
        for half in range(2):
            df = (jnp.sum(jnp.where(lane == C_F, dq_ref[half], 0.0), axis=-1, keepdims=True)
                  - jnp.sum(jnp.where(lane == C_ONE, dk_ref[half], 0.0), axis=-1, keepdims=True))
            contrib = jnp.where(lane == 2 * j + half, df, contrib)

        @pl.when(j == 0)
        def _():
            dF_ref[...] = contrib

        @pl.when(j > 0)
        def _():
            dF_ref[...] += contrib

    pairb = pl.BlockSpec((2, tp, AUG), lambda i, j: (j, i, 0))
    col = lambda c0: pl.BlockSpec((tp, AUG), lambda i, j, c0=c0: (i, c0 + j))
    vec = pl.BlockSpec((1, AUG), lambda i, j: (0, 0))
    flat = pl.BlockSpec((tp, AUG), lambda i, j: (i, j))
    return _pallas(body, name="qkv_bwd", grid=(S // tp, N_HEADS // 2),
                   in_specs=[pairb, pairb, pairb, col(8), col(12), vec, vec],
                   out_specs=[flat, flat, flat, pl.BlockSpec((tp, AUG), lambda i, j: (i, 0)), vec, vec],
                   out_shape=[_sds((S, D_ATTN), BF16)] * 3 + [_sds((S, AUG), F32), _sds((1, AUG), F32), _sds((1, AUG), F32)],
                   compiler_params=_cp(("arbitrary", "arbitrary")))(dq, dk, dv, z, z, qw, kw)


def _proj_res_heads(a, wa, o_aug, wo, res):
    S, D = res.shape
    H = o_aug.shape[0]
    tm = TOK_TILE

    def body(a_ref, wa_ref, o_ref, wo_ref, res_ref, out_ref):
        acc = res_ref[...] + _dot(a_ref[...], wa_ref[...])
        for h in range(H):
            acc = acc + _dot(o_ref[h], wo_ref[h])
        out_ref[...] = acc

    row = pl.BlockSpec((tm, D), lambda i: (i, 0))
    return _pallas(body, name="proj_res_heads", grid=(S // tm,),
                   in_specs=[pl.BlockSpec((tm, a.shape[1]), lambda i: (i, 0)), pl.BlockSpec(wa.shape, lambda i: (0, 0)),
                             pl.BlockSpec((H, tm, AUG), lambda i: (0, i, 0)), pl.BlockSpec(wo.shape, lambda i: (0, 0, 0)), row],
                   out_specs=row, out_shape=_sds((S, D), F32), compiler_params=_cp(("parallel",)))(a, wa, o_aug, wo, res)


def _heads_tn(o_aug, d):
    H, S, A = o_aug.shape
    D = d.shape[1]
    tm = min(DW_TILE, S)
    nt = S // tm

    def body(o_ref, d_ref, out_ref, acc_ref):
        i = pl.program_id(0)

        @pl.when(i == 0)
        def _():
            acc_ref[...] = jnp.zeros_like(acc_ref)

        dv = d_ref[...].astype(BF16)
        for h in range(H):
            acc_ref[h] += _dot_tn(o_ref[h], dv)

        @pl.when(i == nt - 1)
        def _():
            out_ref[...] = acc_ref[...].astype(BF16)

    return _pallas(body, name="heads_tn", grid=(nt,),
                   in_specs=[pl.BlockSpec((H, tm, A), lambda i: (0, i, 0)), pl.BlockSpec((tm, D), lambda i: (i, 0))],
                   out_specs=pl.BlockSpec((H, A, D), lambda i: (0, 0, 0)), out_shape=_sds((H, A, D), BF16),
                   scratch_shapes=[pltpu.VMEM((H, A, D), F32)], compiler_params=_cp(("arbitrary",)))(o_aug, d)


def _odd_mid_fwd(z, cw):
    S = z.shape[0]
    D = z.shape[1] // 3
    tm = TOK_TILE
    hb = tm // HALO_C
    W = CONV_C_WIDTH

    def body(gb_ref, gc_ref, hh_ref, gcp_ref, hhp_ref, cw_ref, y_ref, win):
        i = pl.program_id(0)
        prev = gcp_ref[...].astype(F32) * hhp_ref[...].astype(F32)
        win[pl.ds(0, HALO_C), :] = jnp.where(i == 0, 0.0, prev)
        win[pl.ds(HALO_C, tm), :] = gc_ref[...].astype(F32) * hh_ref[...].astype(F32)
        c1 = jnp.zeros((tm, D), F32)
        for k in range(W):
            c1 = c1 + cw_ref[k:k + 1, :] * win[pl.ds(HALO_C - (W - 1) + k, tm), :]
        y_ref[...] = (gb_ref[...].astype(F32) * c1).astype(BF16)

    cur = lambda c: pl.BlockSpec((tm, D), lambda i, c=c: (i, c))
    prv = lambda c: pl.BlockSpec((HALO_C, D), lambda i, c=c: (jnp.maximum(i * hb - 1, 0), c))
    return _pallas(body, name="odd_mid_fwd", grid=(S // tm,),
                   in_specs=[cur(0), cur(1), cur(2), prv(1), prv(2), pl.BlockSpec((8, D), lambda i: (0, 0))],
                   out_specs=pl.BlockSpec((tm, D), lambda i: (i, 0)), out_shape=_sds((S, D), BF16),
                   scratch_shapes=[pltpu.VMEM((tm + HALO_C, D), F32)],
                   compiler_params=_cp(("parallel",)))(z, z, z, z, z, cw)


def _odd_mid_bwd(dy, z, cw):
    S = z.shape[0]
    D = z.shape[1] // 3
    tm = TOK_TILE
    hb = tm // HALO_C
    nt = S // tm
    W = CONV_C_WIDTH

    def body(dy_ref, dyn_ref, gb_ref, gbn_ref, gc_ref, hh_ref, gcp_ref, hhp_ref, cw_ref, dz_ref, dcw_ref, win, dwin):
        i = pl.program_id(0)

        @pl.when(i == 0)
        def _():
            dcw_ref[...] = jnp.zeros_like(dcw_ref)

        gc = gc_ref[...].astype(F32)
        hh = hh_ref[...].astype(F32)
        prev = gcp_ref[...].astype(F32) * hhp_ref[...].astype(F32)
        win[pl.ds(0, HALO_C), :] = jnp.where(i == 0, 0.0, prev)
        win[pl.ds(HALO_C, tm), :] = gc * hh
        dyv = dy_ref[...]
        dc1 = dyv * gb_ref[...].astype(F32)
        dwin[pl.ds(0, tm), :] = dc1
        dwin[pl.ds(tm, HALO_C), :] = jnp.where(i == nt - 1, 0.0, dyn_ref[...] * gbn_ref[...].astype(F32))
        c1 = jnp.zeros((tm, D), F32)
        dc0 = jnp.zeros((tm, D), F32)
        for k in range(W):
            tap = win[pl.ds(HALO_C - (W - 1) + k, tm), :]
            c1 = c1 + cw_ref[k:k + 1, :] * tap
            dc0 = dc0 + cw_ref[k:k + 1, :] * dwin[pl.ds(W - 1 - k, tm), :]
            dcw_ref[k:k + 1, :] += jnp.sum(dc1 * tap, axis=0, keepdims=True)
        dz_ref[:, 0:D] = (dyv * c1).astype(BF16)
        dz_ref[:, D:2 * D] = (dc0 * hh).astype(BF16)
        dz_ref[:, 2 * D:3 * D] = (dc0 * gc).astype(BF16)

    cur = lambda c: pl.BlockSpec((tm, D), lambda i, c=c: (i, c))
    prv = lambda c: pl.BlockSpec((HALO_C, D), lambda i, c=c: (jnp.maximum(i * hb - 1, 0), c))
    nxt = pl.BlockSpec((HALO_C, D), lambda i: (jnp.minimum((i + 1) * hb, S // HALO_C - 1), 0))
    return _pallas(body, name="odd_mid_bwd", grid=(nt,),
                   in_specs=[cur(0), nxt, cur(0), nxt, cur(1), cur(2), prv(1), prv(2), pl.BlockSpec((8, D), lambda i: (0, 0))],
                   out_specs=[pl.BlockSpec((tm, 3 * D), lambda i: (i, 0)), pl.BlockSpec((8, D), lambda i: (0, 0))],
                   out_shape=[_sds((S, 3 * D), BF16), _sds((8, D), F32)],
                   scratch_shapes=[pltpu.VMEM((tm + HALO_C, D), F32), pltpu.VMEM((tm + HALO_C, D), F32)],
                   compiler_params=_cp(("arbitrary",)))(dy, dy, z, z, z, z, z, z, cw)


def _loss_head(y, tgt):
    S, D = y.shape
    tm = TOK_TILE

    def body(y_ref, t_ref, dy_ref, l_ref):
        @pl.when(pl.program_id(0) == 0)
        def _():
            l_ref[...] = jnp.zeros_like(l_ref)

        e = y_ref[...] - t_ref[...]
        dy_ref[...] = e * (1.0 / D)
        l_ref[...] += jnp.sum(jnp.sum(e * e, axis=-1, keepdims=True), axis=0, keepdims=True) * (0.5 / D)

    row = pl.BlockSpec((tm, D), lambda i: (i, 0))
    return _pallas(body, name="loss_head", grid=(S // tm,), in_specs=[row, row],
                   out_specs=[row, pl.BlockSpec((1, 1), lambda i: (0, 0))],
                   out_shape=[_sds((S, D), F32), _sds((1, 1), F32)],
                   compiler_params=_cp(("arbitrary",)))(y, tgt)


def _pad_rows(a, rows):
    return jnp.pad(a, ((0, rows - a.shape[0]), (0, 0)))


def _local_step(x, tgt, W, need=lambda block, after: None, done=lambda block, block_grads: None):
    S, D = x.shape
    grads = {}
    saved = {}

    def gain_after(gain, token):
        return gain if token is None else gain + token

    def ffn_f(tag, l, xin):
        need((tag, l), xin)
        out, xn, G, U = _ffn_fwd(xin, W[tag + "_norm"][l:l + 1], W[tag + "_w_gate"][l], W[tag + "_w_up"][l],
                                 W[tag + "_w_down"][l])
        saved[(tag, l)] = (xin, xn, G, U)
        return out

    def ffn_b(tag, l, dout):
        xin, xn, G, U = saved[(tag, l)]
        keys = [(tag + "_w_gate", l), (tag + "_w_up", l), (tag + "_w_down", l)]
        *dws, dG, dU = _ffn_bwd_w(dout, xn, G, U, W[tag + "_w_down"][l])
        big = dict(zip(keys, dws))
        grads.update(big)
        token = done((tag, l), big)
        dx, dg = _norm_in_bwd([dG, dU], [W[tag + "_w_gate"][l], W[tag + "_w_up"][l]], xin,
                              gain_after(W[tag + "_norm"][l:l + 1], token), dout, w_rows=True)
        grads[(tag + "_norm", l)] = dg
        return dx

    x0a = ffn_f("ffn1", 0, x)
    need(("ev", 0), x0a)
    w_in = W["ev_w_in"]
    w_main, w_f = w_in[:, :2560], jnp.pad(w_in[:, 2560:], ((0, 0), (0, 120)))
    h0, z0, fl = _norm_proj(x0a, W["mix_norm"][0:1], w_main, w_f)
    cw_a = _pad_rows(W["ev_conv_w"], 32)
    a_act, a1 = _conv_a_fwd(z0, cw_a, W["ev_conv_b"], W["ev_conv_norm"])
    flb, Fc = _forget_scan(fl, jnp.pad(W["ev_b_f"], ((0, 0), (0, 120))))
    qw2, kw2 = jnp.tile(W["ev_q_norm"], (1, 2)), jnp.tile(W["ev_k_norm"], (1, 2))
    q_aug, k_aug, v_aug = _qkv_prep(z0, Fc, qw2, kw2)
    o_aug, q_lse = _fox_fwd(q_aug, k_aug, v_aug)
    w_out_e = W["ev_w_out"]
    w_out_o = jnp.pad(w_out_e[D_CONV:].reshape(N_HEADS, HEAD_DIM, D), ((0, 0), (0, AUG - HEAD_DIM), (0, 0)))
    x0b = _proj_res_heads(a_act, w_out_e[:D_CONV], o_aug, w_out_o, x0a)
    x0c = ffn_f("ffn2", 0, x0b)
    x1a = ffn_f("ffn1", 1, x0c)
    need(("od", 0), x1a)
    h1, z1 = _norm_proj(x1a, W["mix_norm"][1:2], W["od_w_in"])
    cw_c = _pad_rows(W["od_conv_w"], 8)
    y1 = _odd_mid_fwd(z1, cw_c)
    x1b = _proj_res([y1], [W["od_w_out"]], x1a)
    x1c = ffn_f("ffn2", 1, x1b)
    dy, loss = _loss_head(x1c, tgt)

    d = ffn_b("ffn2", 1, dy)
    dy1 = _matmul_nt(d, W["od_w_out"])
    grads[("od_w_out", 0)] = _matmul_tn(y1, d, D)[0]
    dz1, dcw_c = _odd_mid_bwd(dy1, z1, cw_c)
    grads[("od_conv_w", 0)] = dcw_c[:CONV_C_WIDTH]
    grads[("od_w_in", 0)] = _matmul_tn(h1, dz1, 3 * D // 4)
    token = done(("od", 0), {k: grads[k] for k in (("od_w_out", 0), ("od_w_in", 0))})
    d, dg = _norm_in_bwd([dz1[None]], [W["od_w_in"][None]], x1a, gain_after(W["mix_norm"][1:2], token), d)
    grads[("mix_norm", 1)] = dg
    d = ffn_b("ffn1", 1, d)
    d = ffn_b("ffn2", 0, d)
    dcat = _matmul_nt(d, w_out_e)
    grads[("ev_w_out", 0)] = jnp.concatenate([_matmul_tn(a_act, d, D)[0],
                                              _heads_tn(o_aug, d)[:, :HEAD_DIM].reshape(D_ATTN, D)], axis=0)
    duz, dcw_a, dcb, dcn = _conv_a_bwd(dcat, a1, z0, cw_a, W["ev_conv_norm"])
    grads[("ev_conv_w", 0)] = dcw_a[:CONV_A_WIDTH]
    grads[("ev_conv_b", 0)] = dcb
    grads[("ev_conv_norm", 0)] = dcn
    dq_a, dk_a, dv_a = _fox_bwd(q_lse, k_aug, v_aug, _do_prep(dcat, o_aug))
    dqf, dkf, dvf, dF, dqw, dkw = _qkv_bwd(dq_a, dk_a, dv_a, z0, qw2, kw2)
    grads[("ev_q_norm", 0)] = dqw[:, :HEAD_DIM] + dqw[:, HEAD_DIM:]
    grads[("ev_k_norm", 0)] = dkw[:, :HEAD_DIM] + dkw[:, HEAD_DIM:]
    dfl, dbf = _forget_scan_bwd(dF, flb)
    grads[("ev_b_f", 0)] = dbf[:, :N_HEADS]
    dz0 = jnp.concatenate([duz, dqf, dkf, dvf], axis=1)
    dflb = dfl.astype(BF16)
    gmain = _matmul_tn(h0, dz0, 640)
    gmain = gmain.transpose(1, 0, 2).reshape(D, 2560)
    gf = _matmul_tn(h0, dflb, 128)[0][:, :N_HEADS]
    grads[("ev_w_in", 0)] = jnp.concatenate([gmain, gf], axis=1)
    token = done(("ev", 0), {k: grads[k] for k in (("ev_w_out", 0), ("ev_w_in", 0))})
    d, dg = _norm_in_bwd([dz0[None], dflb[None]], [w_main[None], w_f[None]], x0a, gain_after(W["mix_norm"][0:1], token), d)
    grads[("mix_norm", 0)] = dg
    d = ffn_b("ffn1", 0, d)
    return loss, d, grads


def _place():
    x, y, c = lax.axis_index("x"), lax.axis_index("y"), lax.axis_index("c")
    chips = [(1 - x, y), (x, 1 - y), (1 - x, 1 - y)]
    return x, y, c, chips


def _remote(src, dst, send_sem, recv_sem, to):
    return pltpu.make_async_remote_copy(src_ref=src, dst_ref=dst, send_sem=send_sem, recv_sem=recv_sem,
                                        device_id=to, device_id_type=MESH)


HBM = pl.BlockSpec(memory_space=pltpu.HBM)
SEM = pl.BlockSpec(memory_space=pltpu.SEMAPHORE)
EFFECT = pltpu.SideEffectType.DATAFLOW_SIDE_EFFECTING


def _in_hbm(a):
    return pltpu.with_memory_space_constraint(a, pltpu.HBM)


def _ag_start(tag, bufs, with_taps):
    n = len(bufs)
    order = ([n - 1] + list(range(n - 1))) if with_taps else list(range(n))

    def body(*refs):
        send_sems, recv_sems = refs[n], refs[n + 1]
        outs, token = refs[n + 2:2 * n + 2], refs[2 * n + 2]
        x, y, c, chips = _place()
        me = 2 * x + y
        for a in order:
            if with_taps and a == n - 1:
                blk = outs[a].at[me]
            else:
                h = outs[a].shape[1] // 2
                blk = outs[a].at[me, pl.ds(c * h, h)]
            for jj, (px, py) in enumerate(chips):
                _remote(blk, blk, send_sems.at[3 * a + jj], recv_sems.at[3 * a + jj], (px, py, c)).start()
        token[...] = jnp.zeros_like(token)

    return _pallas(
        body, name=f"gather_start_{tag}",
        out_shape=[pltpu.SemaphoreType.DMA((3 * n,)), pltpu.SemaphoreType.DMA((3 * n,))]
        + [pltpu.HBM(b.shape, b.dtype) for b in bufs] + [_sds((8, 128), F32)],
        in_specs=[HBM] * n, out_specs=[SEM, SEM] + [HBM] * n + [pl.BlockSpec(memory_space=pltpu.VMEM)],
        input_output_aliases={a: 2 + a for a in range(n)},
        compiler_params=pltpu.CompilerParams(has_side_effects=EFFECT),
    )(*[_in_hbm(b) for b in bufs])


def _ag_mid(g, ici_send, ici_recv, bufs, idx, taps, n_big, after):
    n = len(bufs)
    arrs = list(bufs) + ([taps] if taps is not None else [])
    m = len(arrs)

    def body(*refs):
        ici_s, ici_r = refs[0], refs[1]
        d_send, d_recv = refs[m + 3], refs[m + 4]
        outs = refs[m + 5:]
        x, y, c, chips = _place()
        me = 2 * x + y
        for i in range(m):
            a = idx[i] if i < n else n_big
            for jj, (px, py) in enumerate(chips):
                k = 3 * a + jj
                if i < n:
                    h = outs[i].shape[1] // 2
                    mine, blk = outs[i].at[me, pl.ds(c * h, h)], outs[i].at[2 * px + py, pl.ds(c * h, h)]
                else:
                    mine, blk = outs[i].at[me], outs[i].at[2 * px + py]
                _remote(mine, mine, ici_s.at[k], ici_r.at[k], (px, py, c)).wait_send()
                _remote(blk, blk, ici_s.at[k], ici_r.at[k], (px, py, c)).wait_recv()
                if i < n:
                    _remote(blk, blk, d_send.at[3 * i + jj], d_recv.at[3 * i + jj], (x, y, 1 - c)).start()

    return _pallas(
        body, name=f"gather_pass_on_{g}",
        out_shape=[pltpu.SemaphoreType.DMA((3 * n,)), pltpu.SemaphoreType.DMA((3 * n,))] + [pltpu.HBM(b.shape, b.dtype) for b in arrs],
        in_specs=[SEM, SEM] + [HBM] * m + [ANY], out_specs=[SEM, SEM] + [HBM] * m,
        input_output_aliases={2 + i: 2 + i for i in range(m)},
        compiler_params=pltpu.CompilerParams(has_side_effects=EFFECT),
    )(ici_send, ici_recv, *arrs, after)


def _ag_wait(g, d_send, d_recv, arrs, n, after):
    m = len(arrs)

    def body(*refs):
        d_s, d_r = refs[0], refs[1]
        outs = refs[m + 3:]
        x, y, c, chips = _place()
        for i in range(n):
            h = outs[i].shape[1] // 2
            for jj, (px, py) in enumerate(chips):
                sent = outs[i].at[2 * px + py, pl.ds(c * h, h)]
                got = outs[i].at[2 * px + py, pl.ds((1 - c) * h, h)]
                _remote(sent, sent, d_s.at[3 * i + jj], d_r.at[3 * i + jj], (x, y, 1 - c)).wait_send()
                _remote(got, got, d_s.at[3 * i + jj], d_r.at[3 * i + jj], (x, y, 1 - c)).wait_recv()

    return _pallas(
        body, name=f"gather_wait_{g}", out_shape=[pltpu.HBM(b.shape, b.dtype) for b in arrs],
        in_specs=[SEM, SEM] + [HBM] * m + [ANY], out_specs=[HBM] * m,
        input_output_aliases={2 + i: i for i in range(m)},
        compiler_params=pltpu.CompilerParams(has_side_effects=EFFECT),
    )(d_send, d_recv, *arrs, after)


def _pair_start(g, gs, after):
    n = len(gs)
    zones = [lax.empty((4, a.shape[1] // 2, a.shape[2]), a.dtype) for a in gs]
    extra = [] if after is None else [after]

    def body(*refs):
        k0 = 2 * n + len(extra)
        send_sems, recv_sems = refs[k0], refs[k0 + 1]
        src, dst = refs[k0 + 2:k0 + 2 + n], refs[k0 + 2 + n:k0 + 2 + 2 * n]
        token = refs[k0 + 2 + 2 * n]
        x, y, c, _ = _place()
        for a in range(n):
            h = src[a].shape[1] // 2
            _remote(src[a].at[:, pl.ds((1 - c) * h, h)], dst[a], send_sems.at[a], recv_sems.at[a], (x, y, 1 - c)).start()
        token[...] = jnp.zeros_like(token)

    return _pallas(
        body, name=f"grad_pair_start_{g}",
        out_shape=[pltpu.SemaphoreType.DMA((n,)), pltpu.SemaphoreType.DMA((n,))]
        + [pltpu.HBM(a.shape, a.dtype) for a in gs + zones] + [_sds((8, 128), F32)],
        in_specs=[HBM] * (2 * n) + [ANY] * len(extra),
        out_specs=[SEM, SEM] + [HBM] * (2 * n) + [pl.BlockSpec(memory_space=pltpu.VMEM)],
        input_output_aliases={i: 2 + i for i in range(2 * n)},
        compiler_params=pltpu.CompilerParams(has_side_effects=EFFECT),
    )(*[_in_hbm(a) for a in gs + zones], *extra)


def _pair_wait(g, send, recv, gs, zones):
    n = len(gs)

    def body(*refs):
        s_ref, r_ref = refs[0], refs[1]
        outs = refs[2 + 2 * n:]
        src, dst = outs[:n], outs[n:]
        x, y, c, _ = _place()
        for a in range(n):
            h = src[a].shape[1] // 2
            _remote(src[a].at[:, pl.ds((1 - c) * h, h)], dst[a], s_ref.at[a], r_ref.at[a], (x, y, 1 - c)).wait()

    return _pallas(
        body, name=f"grad_pair_wait_{g}", out_shape=[pltpu.HBM(a.shape, a.dtype) for a in gs + zones],
        in_specs=[SEM, SEM] + [HBM] * (2 * n), out_specs=[HBM] * (2 * n),
        input_output_aliases={2 + i: i for i in range(2 * n)},
        compiler_params=pltpu.CompilerParams(has_side_effects=EFFECT),
    )(send, recv, *gs, *zones)


def _pair_add(gs, others, c_arr):
    n = len(gs)

    def body(c_ref, *refs):
        for g_ref, o_ref, out_ref in zip(refs[:n], refs[n:2 * n], refs[2 * n:]):
            out_ref[...] = (g_ref[...].astype(F32) + o_ref[...].astype(F32)).astype(BF16)

    half = lambda a: pl.BlockSpec((1, a.shape[1] // 2, a.shape[2]), lambda k, c_ref: (k, c_ref[0], 0))
    whole = lambda a: pl.BlockSpec((1,) + a.shape[1:], lambda k, c_ref: (k, 0, 0))
    grid_spec = pltpu.PrefetchScalarGridSpec(
        num_scalar_prefetch=1, grid=(4,), in_specs=[half(a) for a in gs] + [whole(o) for o in others],
        out_specs=[whole(o) for o in others])
    return _pallas(body, name="grad_pair_add", grid_spec=grid_spec, out_shape=[_sds(o.shape, BF16) for o in others],
                   compiler_params=_cp(("parallel",)))(c_arr, *gs, *others)


def _chip_start(g, ss):
    n = len(ss)
    zones = [lax.empty((3,) + s.shape[1:], s.dtype) for s in ss]

    def body(*refs):
        send_sems, recv_sems = refs[2 * n], refs[2 * n + 1]
        src, dst = refs[2 * n + 2:3 * n + 2], refs[3 * n + 2:4 * n + 2]
        token = refs[4 * n + 2]
        x, y, c, chips = _place()
        for a in range(n):
            for jj, (px, py) in enumerate(chips):
                k = 3 * a + jj
                _remote(src[a].at[2 * px + py], dst[a].at[jj], send_sems.at[k], recv_sems.at[k], (px, py, c)).start()
        token[...] = jnp.zeros_like(token)

    return _pallas(
        body, name=f"grad_chip_start_{g}",
        out_shape=[pltpu.SemaphoreType.DMA((3 * n,)), pltpu.SemaphoreType.DMA((3 * n,))]
        + [pltpu.HBM(a.shape, a.dtype) for a in ss + zones] + [_sds((8, 128), F32)],
        in_specs=[HBM] * (2 * n), out_specs=[SEM, SEM] + [HBM] * (2 * n) + [pl.BlockSpec(memory_space=pltpu.VMEM)],
        input_output_aliases={i: 2 + i for i in range(2 * n)},
        compiler_params=pltpu.CompilerParams(has_side_effects=EFFECT),
    )(*[_in_hbm(a) for a in ss + zones])


def _chip_wait(sends, recvs, counts, ss, zones, after):
    nb, n = len(sends), len(ss)

    def body(*refs):
        s_refs, r_refs = refs[:nb], refs[nb:2 * nb]
        outs = refs[2 * nb + 2 * n + 1:]
        src, dst = outs[:n], outs[n:]
        x, y, c, chips = _place()
        a = 0
        for b in range(nb):
            for i in range(counts[b]):
                for jj, (px, py) in enumerate(chips):
                    k = 3 * i + jj
                    _remote(src[a].at[2 * px + py], dst[a].at[jj], s_refs[b].at[k], r_refs[b].at[k], (px, py, c)).wait()
                a += 1

    return _pallas(
        body, name="grad_chip_wait", out_shape=[pltpu.HBM(a.shape, a.dtype) for a in ss + zones],
        in_specs=[SEM] * (2 * nb) + [HBM] * (2 * n) + [ANY], out_specs=[HBM] * (2 * n),
        input_output_aliases={2 * nb + i: i for i in range(2 * n)},
        compiler_params=pltpu.CompilerParams(has_side_effects=EFFECT),
    )(*sends, *recvs, *ss, *zones, after)


def _chip_sum(s, r, where, dest, l, L):
    _, h, C = s.shape
    tr = h // 2

    def body(k_ref, s_ref, r_ref, *rest):
        out_ref = rest[-1]
        acc = s_ref[0].astype(F32)
        for jj in range(3):
            acc = acc + r_ref[jj].astype(F32)
        out_ref[...] = acc

    in_specs = [pl.BlockSpec((1, tr, C), lambda i, k_ref: (k_ref[0], i, 0)), pl.BlockSpec((3, tr, C), lambda i, k_ref: (0, i, 0))]
    args = [where, s, r]
    alias = {}
    if dest is not None:
        in_specs.append(ANY)
        args.append(dest)
        alias = {3: 0}
    grid_spec = pltpu.PrefetchScalarGridSpec(
        num_scalar_prefetch=1, grid=(2,), in_specs=in_specs,
        out_specs=pl.BlockSpec((None, tr, C), lambda i, k_ref: (l, 2 * k_ref[1] + i, 0)))
    return _pallas(body, name="grad_chip_sum", grid_spec=grid_spec, out_shape=_sds((L, 2 * h, C), F32),
                   input_output_aliases=alias, compiler_params=_cp(("arbitrary",)))(*args)


def _share_start(tag, bufs, layout):
    n, n_buf = len(layout), len(bufs)

    def body(*refs):
        send_sems, recv_sems = refs[n_buf], refs[n_buf + 1]
        outs = refs[n_buf + 2:]
        x, y, c, _ = _place()
        for a, (o, l) in enumerate(layout):
            h = outs[o].shape[1] // 2
            blk = outs[o].at[l, pl.ds(c * h, h)]
            _remote(blk, blk, send_sems.at[a], recv_sems.at[a], (x, y, 1 - c)).start()

    return _pallas(
        body, name=f"grad_share_start_{tag}",
        out_shape=[pltpu.SemaphoreType.DMA((n,)), pltpu.SemaphoreType.DMA((n,))] + [pltpu.HBM(b.shape, b.dtype) for b in bufs],
        in_specs=[HBM] * n_buf, out_specs=[SEM, SEM] + [HBM] * n_buf, input_output_aliases={o: 2 + o for o in range(n_buf)},
        compiler_params=pltpu.CompilerParams(has_side_effects=EFFECT),
    )(*[_in_hbm(b) for b in bufs])


def _share_wait(tag, send, recv, bufs, layout, after):
    n_buf = len(bufs)

    def body(*refs):
        s_ref, r_ref = refs[0], refs[1]
        outs = refs[n_buf + 3:]
        x, y, c, _ = _place()
        for a, (o, l) in enumerate(layout):
            h = outs[o].shape[1] // 2
            mine, theirs = outs[o].at[l, pl.ds(c * h, h)], outs[o].at[l, pl.ds((1 - c) * h, h)]
            _remote(mine, mine, s_ref.at[a], r_ref.at[a], (x, y, 1 - c)).wait_send()
            _remote(theirs, theirs, s_ref.at[a], r_ref.at[a], (x, y, 1 - c)).wait_recv()

    return _pallas(
        body, name=f"grad_share_wait_{tag}", out_shape=[pltpu.HBM(b.shape, b.dtype) for b in bufs],
        in_specs=[SEM, SEM] + [HBM] * n_buf + [ANY], out_specs=[HBM] * n_buf,
        input_output_aliases={2 + o: o for o in range(n_buf)},
        compiler_params=pltpu.CompilerParams(has_side_effects=EFFECT),
    )(send, recv, *bufs, after)


def _small_all_reduce(packed):
    P, L = packed.shape

    def body(in_ref, out_ref, slots, send_sems, recv_sems):
        x, y, c, _ = _place()
        me = 4 * x + 2 * y + c
        slots[me] = in_ref[...]
        cps = []
        for r in range(1, 8):
            px = 1 - x if r & 4 else x
            py = 1 - y if r & 2 else y
            pc = 1 - c if r & 1 else c
            cps.append(_remote(in_ref, slots.at[me], send_sems.at[r - 1], recv_sems.at[r - 1], (px, py, pc)))
        for cp in cps:
            cp.start()
        for r in range(1, 8):
            px = 1 - x if r & 4 else x
            py = 1 - y if r & 2 else y
            pc = 1 - c if r & 1 else c
            blk = slots.at[4 * px + 2 * py + pc]
            _remote(blk, blk, send_sems.at[r - 1], recv_sems.at[r - 1], (px, py, pc)).wait_recv()
        for cp in cps:
            cp.wait_send()
        acc = slots[0]
        for k in range(1, 8):
            acc = acc + slots[k]
        out_ref[...] = acc

    vm = pl.BlockSpec(memory_space=pltpu.VMEM)
    return _pallas(body, name="small_all_reduce", in_specs=[vm], out_specs=vm, out_shape=_sds((P, L), F32),
                   scratch_shapes=[pltpu.VMEM((8, P, L), F32), pltpu.SemaphoreType.DMA((7,)), pltpu.SemaphoreType.DMA((7,))])(packed)


def _adamw_math(w, g, m, v):
    m = ADAM_B1 * m + (1.0 - ADAM_B1) * g
    v = ADAM_B2 * v + (1.0 - ADAM_B2) * (g * g)
    m_hat = m / (1.0 - ADAM_B1 ** ADAM_STEP)
    v_hat = v / (1.0 - ADAM_B2 ** ADAM_STEP)
    delta = -ADAM_LR * (m_hat / (jnp.sqrt(v_hat) + ADAM_EPS) + ADAM_WD * w)
    return delta, m, v


def _adamw(w, g, m, v):
    shape = w.shape
    C = shape[-1]
    rows = math.prod(shape[:-1])
    tr = next(t for t in (512, 352, 256, 128, 64, 32, 16, 8, rows) if rows % t == 0)
    w2, g2, m2, v2 = (a.reshape(rows, C) for a in (w, g, m, v))

    def body(w_ref, g_ref, m_ref, v_ref, d_ref, nm_ref, nv_ref):
        d, nm, nv = _adamw_math(w_ref[...], g_ref[...], m_ref[...], v_ref[...])
        d_ref[...] = d
        nm_ref[...] = nm
        nv_ref[...] = nv

    blk = pl.BlockSpec((tr, C), lambda i: (i, 0))
    outs = _pallas(body, name="adamw", grid=(rows // tr,), in_specs=[blk] * 4, out_specs=[blk] * 3,
                   out_shape=[_sds((rows, C), F32)] * 3, compiler_params=_cp(("parallel",)))(w2, g2, m2, v2)
    return tuple(o.reshape(shape) for o in outs)


WEIGHTS = ["ffn1_norm", "ffn1_w_gate", "ffn1_w_up", "ffn1_w_down", "mix_norm", "ffn2_norm", "ffn2_w_gate", "ffn2_w_up",
           "ffn2_w_down", "ev_w_in", "ev_b_f", "ev_conv_w", "ev_conv_b", "ev_conv_norm", "ev_q_norm", "ev_k_norm",
           "ev_w_out", "od_w_in", "od_conv_w", "od_w_out"]
BIG = ([("ffn1_w_gate", 0), ("ffn1_w_up", 0), ("ffn1_w_down", 0), ("ev_w_in", 0), ("ev_w_out", 0),
        ("ffn2_w_gate", 0), ("ffn2_w_up", 0), ("ffn2_w_down", 0)]
       + [("ffn1_w_gate", 1), ("ffn1_w_up", 1), ("ffn1_w_down", 1), ("od_w_in", 0), ("od_w_out", 0),
          ("ffn2_w_gate", 1), ("ffn2_w_up", 1), ("ffn2_w_down", 1)])
TRANSPOSED = ("ffn1_w_gate", "ffn1_w_up", "ffn2_w_gate", "ffn2_w_up")
SHARED_LAST = ("ffn1_w_gate", "ffn1_w_up", "ffn1_w_down", "ev_w_in", "ev_w_out")
BLOCKS = [("ffn1", 0), ("ev", 0), ("ffn2", 0), ("ffn1", 1), ("od", 0), ("ffn2", 1)]
BLOCK_OF = {(name, l): (name.split("_w_")[0], l) for name, l in BIG}
BIG_NAMES = ["ffn1_w_gate", "ffn1_w_up", "ffn1_w_down", "ffn2_w_gate", "ffn2_w_up", "ffn2_w_down",
             "ev_w_in", "ev_w_out", "od_w_in", "od_w_out"]
SMALL = [("ffn1_norm", 16), ("mix_norm", 16), ("ffn2_norm", 16), ("ev_b_f", 8), ("ev_conv_w", 128), ("ev_conv_b", 8),
         ("ev_conv_norm", 8), ("ev_q_norm", 8), ("ev_k_norm", 8), ("od_conv_w", 24)]


def _to_lanes(a, rows):
    flat = a.reshape(-1)
    return jnp.pad(flat, (0, rows * 128 - flat.shape[0])).reshape(rows, 128)


def kernel(x, ffn1_norm, ffn1_w_gate, ffn1_w_up, ffn1_w_down, mix_norm, ffn2_norm, ffn2_w_gate, ffn2_w_up, ffn2_w_down, ev_w_in, ev_b_f, ev_conv_w, ev_conv_b, ev_conv_norm, ev_q_norm, ev_k_norm, ev_w_out, od_w_in, od_conv_w, od_w_out, loss_target, m_ffn1_norm, m_ffn1_w_gate, m_ffn1_w_up, m_ffn1_w_down, m_mix_norm, m_ffn2_norm, m_ffn2_w_gate, m_ffn2_w_up, m_ffn2_w_down, m_ev_w_in, m_ev_b_f, m_ev_conv_w, m_ev_conv_b, m_ev_conv_norm, m_ev_q_norm, m_ev_k_norm, m_ev_w_out, m_od_w_in, m_od_conv_w, m_od_w_out, v_ffn1_norm, v_ffn1_w_gate, v_ffn1_w_up, v_ffn1_w_down, v_mix_norm, v_ffn2_norm, v_ffn2_w_gate, v_ffn2_w_up, v_ffn2_w_down, v_ev_w_in, v_ev_b_f, v_ev_conv_w, v_ev_conv_b, v_ev_conv_norm, v_ev_q_norm, v_ev_k_norm, v_ev_w_out, v_od_w_in, v_od_conv_w, v_od_w_out):
    P = dict(ffn1_norm=ffn1_norm, ffn1_w_gate=ffn1_w_gate, ffn1_w_up=ffn1_w_up, ffn1_w_down=ffn1_w_down, mix_norm=mix_norm,
             ffn2_norm=ffn2_norm, ffn2_w_gate=ffn2_w_gate, ffn2_w_up=ffn2_w_up, ffn2_w_down=ffn2_w_down, ev_w_in=ev_w_in,
             ev_b_f=ev_b_f, ev_conv_w=ev_conv_w, ev_conv_b=ev_conv_b, ev_conv_norm=ev_conv_norm, ev_q_norm=ev_q_norm,
             ev_k_norm=ev_k_norm, ev_w_out=ev_w_out, od_w_in=od_w_in, od_conv_w=od_conv_w, od_w_out=od_w_out)
    M = dict(zip(WEIGHTS, [m_ffn1_norm, m_ffn1_w_gate, m_ffn1_w_up, m_ffn1_w_down, m_mix_norm, m_ffn2_norm, m_ffn2_w_gate,
                           m_ffn2_w_up, m_ffn2_w_down, m_ev_w_in, m_ev_b_f, m_ev_conv_w, m_ev_conv_b, m_ev_conv_norm,
                           m_ev_q_norm, m_ev_k_norm, m_ev_w_out, m_od_w_in, m_od_conv_w, m_od_w_out]))
    V = dict(zip(WEIGHTS, [v_ffn1_norm, v_ffn1_w_gate, v_ffn1_w_up, v_ffn1_w_down, v_mix_norm, v_ffn2_norm, v_ffn2_w_gate,
                           v_ffn2_w_up, v_ffn2_w_down, v_ev_w_in, v_ev_b_f, v_ev_conv_w, v_ev_conv_b, v_ev_conv_norm,
                           v_ev_q_norm, v_ev_k_norm, v_ev_w_out, v_od_w_in, v_od_conv_w, v_od_w_out]))
    for name in TRANSPOSED:
        P[name], M[name], V[name] = (jnp.swapaxes(a, 1, 2) for a in (P[name], M[name], V[name]))
    S, D = x.shape[1], x.shape[2]
    chip = 2 * lax.axis_index("x") + lax.axis_index("y")
    core = lax.axis_index("c")

    def own_slot(shard):
        return lax.dynamic_update_slice(lax.empty((4,) + shard.shape, shard.dtype), shard[None], (chip, 0, 0))

    taps = jnp.concatenate([_to_lanes(_pad_rows(ev_conv_w[0], 32), 32), _to_lanes(_pad_rows(od_conv_w[0], 8), 16)], axis=0)
    first = [i for i, k in enumerate(BIG) if BLOCK_OF[k] == BLOCKS[0]]
    rest = [i for i in range(len(BIG)) if i not in first]
    send0, recv0, *bufs0 = _ag_start("first", [own_slot(P[BIG[i][0]][BIG[i][1]].astype(BF16)) for i in first], False)
    zero = bufs0.pop()[0, 0]
    send1, recv1, *bufs1 = _ag_start("rest", [own_slot((P[BIG[i][0]][BIG[i][1]] + zero).astype(BF16)) for i in rest]
                                     + [own_slot(taps)], True)
    bufs1.pop()
    cols = lambda a: a.transpose(1, 0, 2).reshape(a.shape[1], 4 * a.shape[2])
    W = {k: P[k] for k in ("ffn1_norm", "mix_norm", "ffn2_norm", "ev_b_f", "ev_q_norm", "ev_k_norm")}
    W["ev_conv_b"], W["ev_conv_norm"] = ev_conv_b, ev_conv_norm
    for tag in ("ffn1", "ffn2"):
        for kind in ("_w_gate", "_w_up", "_w_down"):
            W[tag + kind] = [None, None]
    passing = {}

    def pass_on(g, after):
        idx = [i for i, k in enumerate(BIG) if BLOCK_OF[k] == BLOCKS[g]]
        keys = [BIG[i] for i in idx] + (["taps"] if BLOCKS[g] == ("ev", 0) else [])
        send, recv, bufs, members = (send0, recv0, bufs0, first) if g == 0 else (send1, recv1, bufs1, rest)
        local = [members.index(i) for i in idx]
        passing[g] = (keys, _ag_mid(g, send, recv, [bufs[i] for i in local], local,
                                    bufs1[-1] if BLOCKS[g] == ("ev", 0) else None, len(rest), after))

    def need(block, after):
        g = BLOCKS.index(block)
        if g not in passing:
            pass_on(g, after)
        keys, (d_send, d_recv, *thru) = passing.pop(g)
        got = dict(zip(keys, _ag_wait(g, d_send, d_recv, thru, len(keys) - ("taps" in keys), after)))
        if 1 <= g < len(BLOCKS) - 1:
            pass_on(g + 1, after)
        for key, a in got.items():
            if key == "taps":
                continue
            name, l = key
            if name.startswith("ffn"):
                W[name][l] = a
            elif name.endswith("_w_in"):
                W[name] = cols(a)
            elif name.endswith("_w_out"):
                W[name] = a.reshape(4 * a.shape[1], D)
        if block == ("ev", 0):
            taps_all = got["taps"]
            W["ev_conv_w"] = cols(taps_all[:, :32].reshape(4, 32, 128))[:CONV_A_WIDTH]
            W["od_conv_w"] = cols(taps_all[:, 32:48].reshape(4, 8, 256))[:CONV_C_WIDTH]

    rows = lambda a: a.reshape(4, a.shape[0] // 4, a.shape[1])
    colsh = lambda a: a.reshape(a.shape[0], 4, a.shape[1] // 4).transpose(1, 0, 2)
    c_arr = core.reshape(1).astype(jnp.int32)
    where = jnp.stack([chip, core]).astype(jnp.int32)
    in_flight = []

    def done(block, block_grads):
        g = BLOCKS.index(block)
        keys = list(block_grads)
        gs = []
        for name, l in keys:
            a = block_grads[(name, l)]
            gs.append(colsh(a) if name == "ev_w_in" else rows(a) if name.endswith("_w_out") else a)
        for item in list(pairs):
            to_chips(item)
        send, recv, *rest = _pair_start(g, gs, chained.get("token"))
        n = len(keys)
        pairs.append((g, keys, send, recv, rest[:n], rest[n:2 * n]))
        if g == 0:
            to_chips(pairs[0])
        chained["token"] = rest[-1] if g else chained["token"]
        return chained["token"][0:1, 0:1]

    pairs, chained = [], {}

    def to_chips(item):
        pairs.remove(item)
        g, keys, send, recv, gs, zones = item
        n = len(keys)
        done_ = _pair_wait(g, send, recv, gs, zones)
        sums = list(_pair_add(list(done_[:n]), list(done_[n:]), c_arr))
        send2, recv2, *rest = _chip_start(g, sums)
        in_flight.append((keys, send2, recv2, rest[:n], rest[n:2 * n]))
        chained["token"] = rest[-1]

    loss, grad_x, grads = _local_step(x[0], loss_target[0], W, need, done)

    order = [k for keys, *_ in in_flight for k in keys]
    landed = _chip_wait([f[1] for f in in_flight], [f[2] for f in in_flight], [len(f[0]) for f in in_flight],
                        [a for f in in_flight for a in f[3]], [a for f in in_flight for a in f[4]], grad_x)
    sums, recvd = landed[:len(order)], landed[len(order):]
    stacked, shares = {}, []
    for tag, names in (("a", [n for n in BIG_NAMES if n not in SHARED_LAST]), ("b", list(SHARED_LAST))):
        for (name, l), s, r in zip(order, sums, recvd):
            if name in names:
                stacked[name] = _chip_sum(s, r, where, stacked.get(name), l, P[name].shape[0])
        layout = [(names.index(name), l) for name, l in order if name in names]
        send, recv, *thru = _share_start(tag, [stacked[name] for name in names], layout)
        shares.append((tag, names, send, recv, thru, layout))

    def small_grad(name):
        if name.endswith("_norm") and name[:3] in ("ffn", "mix"):
            return jnp.concatenate([grads[(name, 0)], grads[(name, 1)]], axis=0)
        return grads[(name, 0)]

    packed = jnp.concatenate([_to_lanes(small_grad(name), r) for name, r in SMALL], axis=0)
    total = _small_all_reduce(packed)
    small_grads, at = {}, 0
    for name, r in SMALL:
        part = total[at:at + r].reshape(-1)
        at += r
        if name == "ev_conv_w":
            full_g = part[:CONV_A_WIDTH * D_CONV].reshape(CONV_A_WIDTH, D_CONV)
            small_grads[name] = lax.dynamic_slice_in_dim(full_g, chip * (D_CONV // 4), D_CONV // 4, axis=1)[None]
        elif name == "od_conv_w":
            full_g = part[:CONV_C_WIDTH * D].reshape(CONV_C_WIDTH, D)
            small_grads[name] = lax.dynamic_slice_in_dim(full_g, chip * (D // 4), D // 4, axis=1)[None]
        else:
            small_grads[name] = part[:math.prod(P[name].shape)].reshape(P[name].shape)

    results = {}

    def update(name, g):
        outs = (g,) + _adamw(P[name], g, M[name], V[name])
        results[name] = tuple(jnp.swapaxes(a, 1, 2) for a in outs) if name in TRANSPOSED else outs

    for name, _ in SMALL:
        update(name, small_grads[name])
    after = results[SMALL[-1][0]][1]
    for tag, names, send, recv, thru, layout in shares:
        for name, g in zip(names, _share_wait(tag, send, recv, thru, layout, after)):
            update(name, g)
        after = results[names[-1]][1]
    loss_all = lax.psum(loss[0, 0], ("x", "y", "c"))
    return (loss_all, grad_x[None], *[results[name][k] for k in range(4) for name in WEIGHTS])
```

```python
import functools
import math

import jax
import jax.numpy as jnp
from jax import lax
from jax.experimental import pallas as pl
from jax.experimental.pallas import tpu as pltpu

F32, BF16 = jnp.float32, jnp.bfloat16
EPS = 1e-6
FFN_RES = 0.5
N_HEADS, HEAD_DIM = 8, 64
D_CONV = 512
D_ATTN = N_HEADS * HEAD_DIM
CONV_A_WIDTH, CONV_C_WIDTH = 31, 3
ADAM_LR, ADAM_B1, ADAM_B2, ADAM_EPS, ADAM_WD, ADAM_STEP = 0.001, 0.9, 0.999, 1e-08, 0.01, 10
MESH = pl.DeviceIdType.MESH
ANY = pl.BlockSpec(memory_space=pl.ANY)

TOK_TILE = 512
FFN_TILE = 1024
DW_TILE = 1024
ATT_TILE = 1024
QKN_TILE = 2048
HALO_A, HALO_C = 32, 16
SUBLANES = 8
CONV_ROWS = 64
SCAN_BLK = 256
MIB = 2 ** 20


def _pallas(body, **kw):
    return pl.pallas_call(body, **kw)


def _cp(sem=None, vmem_mib=48):
    return pltpu.CompilerParams(dimension_semantics=sem, vmem_limit_bytes=vmem_mib * MIB)


def _dot(a, b):
    return jnp.dot(a, b, preferred_element_type=F32)


def _dot_nt(a, b):
    return lax.dot_general(a, b, (((1,), (1,)), ((), ())), preferred_element_type=F32)


def _dot_tn(a, b):
    return lax.dot_general(a, b, (((0,), (0,)), ((), ())), preferred_element_type=F32)


def _sds(shape, dtype):
    return jax.ShapeDtypeStruct(shape, dtype)


def _rms(x):
    return lax.rsqrt(jnp.mean(x * x, axis=-1, keepdims=True) + EPS)


def _rms_bwd(dy, x, g):
    r = _rms(x)
    xh = x * r
    dxh = dy * g
    dx = r * (dxh - xh * jnp.mean(dxh * xh, axis=-1, keepdims=True))
    return dx, xh


def _silu_grad(z):
    s = jax.nn.sigmoid(z)
    return s * (1.0 + z * (1.0 - s))


def _ffn_fwd(x, g, wg, wu, wd):
    S, D = x.shape
    nc, Fs, _ = wd.shape
    tm = min(FFN_TILE, S)

    def body(x_ref, g_ref, wg_ref, wu_ref, wd_ref, out_ref, xn_ref, G_ref, U_ref, acc_ref):
        j = pl.program_id(1)

        @pl.when(j == 0)
        def _():
            xv = x_ref[...]
            xn_ref[...] = (xv * _rms(xv) * g_ref[...]).astype(BF16)
            acc_ref[...] = jnp.zeros_like(acc_ref)

        xn = xn_ref[...]
        G = _dot_nt(xn, wg_ref[0])
        U = _dot_nt(xn, wu_ref[0])
        G_ref[0] = G.astype(BF16)
        U_ref[0] = U.astype(BF16)
        H = (G * jax.nn.sigmoid(G) * U).astype(BF16)
        acc_ref[...] += _dot(H, wd_ref[0])

        @pl.when(j == nc - 1)
        def _():
            out_ref[...] = x_ref[...] + FFN_RES * acc_ref[...]

    row = pl.BlockSpec((tm, D), lambda i, j: (i, 0))
    return _pallas(
        body, name="ffn_fwd", grid=(S // tm, nc),
        in_specs=[row, pl.BlockSpec((1, D), lambda i, j: (0, 0)),
                  pl.BlockSpec((1, Fs, D), lambda i, j: (j, 0, 0)), pl.BlockSpec((1, Fs, D), lambda i, j: (j, 0, 0)),
                  pl.BlockSpec((1, Fs, D), lambda i, j: (j, 0, 0))],
        out_specs=[row, row, pl.BlockSpec((1, tm, Fs), lambda i, j: (j, i, 0)),
                   pl.BlockSpec((1, tm, Fs), lambda i, j: (j, i, 0))],
        out_shape=[_sds((S, D), F32), _sds((S, D), BF16), _sds((nc, S, Fs), BF16), _sds((nc, S, Fs), BF16)],
        scratch_shapes=[pltpu.VMEM((tm, D), F32)],
        compiler_params=_cp(("parallel", "arbitrary"), 56),
    )(x, g, wg, wu, wd)


def _ffn_bwd_w(dout, xn, G, U, wd):
    S, D = dout.shape
    nc, _, Fs = G.shape
    tm = min(DW_TILE, S)
    nt = S // tm
    sub = min(TOK_TILE, tm)

    def body(do_ref, xn_ref, G_ref, U_ref, wd_ref, dwg_ref, dwu_ref, dwd_ref, dG_ref, dU_ref, ag, au, ad, do_s, H_s):
        i = pl.program_id(1)

        @pl.when(i == 0)
        def _():
            ag[...] = jnp.zeros_like(ag)
            au[...] = jnp.zeros_like(au)
            ad[...] = jnp.zeros_like(ad)

        for r in range(0, tm, sub):
            rows = pl.ds(r, sub)
            do = (FFN_RES * do_ref[rows, :]).astype(BF16)
            do_s[rows, :] = do
            Gv = G_ref[0, rows, :].astype(F32)
            Uv = U_ref[0, rows, :].astype(F32)
            dH = _dot_nt(do, wd_ref[0])
            sg = jax.nn.sigmoid(Gv)
            act = Gv * sg
            H_s[rows, :] = (act * Uv).astype(BF16)
            dU_ref[0, rows, :] = (dH * act).astype(BF16)
            dG_ref[0, rows, :] = (dH * Uv * (sg * (1.0 + Gv * (1.0 - sg)))).astype(BF16)
        xnv = xn_ref[...]
        ag[...] += _dot_tn(dG_ref[0], xnv)
        au[...] += _dot_tn(dU_ref[0], xnv)
        ad[...] += _dot_tn(H_s[...], do_s[...])

        @pl.when(i == nt - 1)
        def _():
            dwg_ref[0] = ag[...].astype(BF16)
            dwu_ref[0] = au[...].astype(BF16)
            dwd_ref[0] = ad[...].astype(BF16)

    row = pl.BlockSpec((tm, D), lambda j, i: (i, 0))
    hid = pl.BlockSpec((1, tm, Fs), lambda j, i: (j, i, 0))
    wrow = pl.BlockSpec((1, Fs, D), lambda j, i: (j, 0, 0))
    return _pallas(
        body, name="ffn_bwd_w", grid=(nc, nt),
        in_specs=[row, row, hid, hid, wrow],
        out_specs=[wrow, wrow, wrow, hid, hid],
        out_shape=[_sds((nc, Fs, D), BF16)] * 3 + [_sds((nc, S, Fs), BF16)] * 2,
        scratch_shapes=[pltpu.VMEM((Fs, D), F32)] * 3 + [pltpu.VMEM((tm, D), BF16), pltpu.VMEM((tm, Fs), BF16)],
        compiler_params=_cp(("parallel", "arbitrary"), 56),
    )(dout, xn, G, U, wd)


def _norm_in_bwd(dzs, ws, x, g, dres, w_rows=False):
    S, D = x.shape
    nc = dzs[0].shape[0]
    n = len(dzs)
    tm = TOK_TILE

    def body(*refs):
        dz_refs, w_refs = refs[:n], refs[n:2 * n]
        x_ref, g_ref, dres_ref, dx_ref, dg_ref, acc_ref = refs[2 * n:]
        i, j = pl.program_id(0), pl.program_id(1)

        @pl.when(j == 0)
        def _():
            acc_ref[...] = jnp.zeros_like(acc_ref)

        @pl.when((i == 0) & (j == 0))
        def _():
            dg_ref[...] = jnp.zeros_like(dg_ref)

        for dz_ref, w_ref in zip(dz_refs, w_refs):
            acc_ref[...] += _dot(dz_ref[0], w_ref[0]) if w_rows else _dot_nt(dz_ref[0], w_ref[0])

        @pl.when(j == nc - 1)
        def _():
            dxn = acc_ref[...]
            dx, xh = _rms_bwd(dxn, x_ref[...], g_ref[...])
            dx_ref[...] = dx + dres_ref[...]
            dg_ref[...] += jnp.sum(dxn * xh, axis=0, keepdims=True)

    row = pl.BlockSpec((tm, D), lambda i, j: (i, 0))
    one = pl.BlockSpec((1, D), lambda i, j: (0, 0))
    in_specs = [pl.BlockSpec((1, tm, dz.shape[2]), lambda i, j: (j, i, 0)) for dz in dzs]
    in_specs += [pl.BlockSpec((1,) + w.shape[1:], lambda i, j: (j, 0, 0)) for w in ws]
    return _pallas(
        body, name="norm_in_bwd", grid=(S // tm, nc),
        in_specs=in_specs + [row, one, row], out_specs=[row, one],
        out_shape=[_sds((S, D), F32), _sds((1, D), F32)],
        scratch_shapes=[pltpu.VMEM((tm, D), F32)],
        compiler_params=_cp(("arbitrary", "arbitrary")),
    )(*dzs, *ws, x, g, dres)


def _norm_proj(x, g, w, w2=None):
    S, D = x.shape
    N = w.shape[1]
    tm = TOK_TILE

    def body(*refs):
        if w2 is None:
            x_ref, g_ref, w_ref, h_ref, z_ref = refs
        else:
            x_ref, g_ref, w_ref, w2_ref, h_ref, z_ref, z2_ref = refs
        xv = x_ref[...]
        h = (xv * _rms(xv) * g_ref[...]).astype(BF16)
        h_ref[...] = h
        z_ref[...] = _dot(h, w_ref[...]).astype(BF16)
        if w2 is not None:
            z2_ref[...] = _dot(h, w2_ref[...])

    row = pl.BlockSpec((tm, D), lambda i: (i, 0))
    in_specs = [row, pl.BlockSpec((1, D), lambda i: (0, 0)), pl.BlockSpec((D, N), lambda i: (0, 0))]
    out_specs = [row, pl.BlockSpec((tm, N), lambda i: (i, 0))]
    out_shape = [_sds((S, D), BF16), _sds((S, N), BF16)]
    args = [x, g, w]
    if w2 is not None:
        N2 = w2.shape[1]
        in_specs.append(pl.BlockSpec((D, N2), lambda i: (0, 0)))
        out_specs.append(pl.BlockSpec((tm, N2), lambda i: (i, 0)))
        out_shape.append(_sds((S, N2), F32))
        args.append(w2)
    return _pallas(body, name="norm_proj", grid=(S // tm,), in_specs=in_specs, out_specs=out_specs,
                   out_shape=out_shape, compiler_params=_cp(("parallel",)))(*args)


def _proj_res(acts, ws, res):
    S, D = res.shape
    n = len(acts)
    tm = TOK_TILE

    def body(*refs):
        a_refs, w_refs = refs[:n], refs[n:2 * n]
        res_ref, out_ref = refs[2 * n:]
        acc = res_ref[...]
        for a_ref, w_ref in zip(a_refs, w_refs):
            acc = acc + _dot(a_ref[...], w_ref[...])
        out_ref[...] = acc

    row = pl.BlockSpec((tm, D), lambda i: (i, 0))
    in_specs = [pl.BlockSpec((tm, a.shape[1]), lambda i: (i, 0)) for a in acts]
    in_specs += [pl.BlockSpec(w.shape, lambda i: (0, 0)) for w in ws]
    return _pallas(body, name="proj_res", grid=(S // tm,), in_specs=in_specs + [row], out_specs=row,
                   out_shape=_sds((S, D), F32), compiler_params=_cp(("parallel",)))(*acts, *ws, res)


def _matmul_nt(a, w, after=None):
    S, K = a.shape
    M = w.shape[0]
    tm = TOK_TILE

    def body(a_ref, w_ref, *rest):
        rest[-1][...] = _dot_nt(a_ref[...].astype(BF16), w_ref[...])

    extra = [] if after is None else [after]
    return _pallas(body, name="matmul_nt", grid=(S // tm,),
                   in_specs=[pl.BlockSpec((tm, K), lambda i: (i, 0)), pl.BlockSpec((M, K), lambda i: (0, 0))] + [ANY] * len(extra),
                   out_specs=pl.BlockSpec((tm, M), lambda i: (i, 0)), out_shape=_sds((S, M), F32),
                   compiler_params=_cp(("parallel",)))(a, w, *extra)


def _matmul_tn(a, b, tn):
    S, M = a.shape
    N = b.shape[1]
    tm = min(DW_TILE, S)
    nt = S // tm

    def body(a_ref, b_ref, o_ref, acc_ref):
        i = pl.program_id(1)

        @pl.when(i == 0)
        def _():
            acc_ref[...] = jnp.zeros_like(acc_ref)

        acc_ref[...] += _dot_tn(a_ref[...].astype(BF16), b_ref[...].astype(BF16))

        @pl.when(i == nt - 1)
        def _():
            o_ref[0] = acc_ref[...].astype(BF16)

    return _pallas(body, name="matmul_tn", grid=(N // tn, nt),
                   in_specs=[pl.BlockSpec((tm, M), lambda j, i: (i, 0)), pl.BlockSpec((tm, tn), lambda j, i: (i, j))],
                   out_specs=pl.BlockSpec((1, M, tn), lambda j, i: (j, 0, 0)), out_shape=_sds((N // tn, M, tn), BF16),
                   scratch_shapes=[pltpu.VMEM((M, tn), F32)],
                   compiler_params=_cp(("parallel", "arbitrary")))(a, b)


def _fill_shifts(win, rows):
    for b in range(1, SUBLANES):
        win[b, pl.ds(0, rows - SUBLANES), :] = win[0, pl.ds(b, rows - SUBLANES), :]


def _tap(win, offset, n, base=0):
    start = base + (offset - offset % SUBLANES)
    if not isinstance(start, int):
        start = pl.multiple_of(start, SUBLANES)
    return win[offset % SUBLANES, pl.ds(start, n), :]


def _conv_a_fwd(z, cw, cb, cn):
    S = z.shape[0]
    C = D_CONV
    tm = TOK_TILE
    hb = tm // HALO_A

    def body(u_ref, gt_ref, up_ref, gp_ref, cw_ref, cb_ref, cn_ref, a_ref, a1_ref, win):
        i = pl.program_id(0)
        prev = up_ref[...].astype(F32) * jax.nn.sigmoid(gp_ref[...].astype(F32))
        win[0, pl.ds(0, HALO_A), :] = jnp.where(i == 0, 0.0, prev)
        win[0, pl.ds(HALO_A, tm), :] = u_ref[...].astype(F32) * jax.nn.sigmoid(gt_ref[...].astype(F32))
        _fill_shifts(win, tm + HALO_A)

        acc = jnp.zeros((tm, C), F32)
        for k in range(CONV_A_WIDTH):
            acc = acc + cw_ref[k:k + 1, :] * _tap(win, HALO_A - (CONV_A_WIDTH - 1) + k, tm)
        a1 = acc + cb_ref[...]
        a1_ref[...] = a1
        a2 = a1 * _rms(a1) * cn_ref[...]
        a_ref[...] = (a2 * jax.nn.sigmoid(a2)).astype(BF16)

    cur = lambda c: pl.BlockSpec((tm, C), lambda i, c=c: (i, c))
    prv = lambda c: pl.BlockSpec((HALO_A, C), lambda i, c=c: (jnp.maximum(i * hb - 1, 0), c))
    vec = pl.BlockSpec((1, C), lambda i: (0, 0))
    return _pallas(body, name="conv_a_fwd", grid=(S // tm,),
                   in_specs=[cur(0), cur(1), prv(0), prv(1), pl.BlockSpec((32, C), lambda i: (0, 0)), vec, vec],
                   out_specs=[pl.BlockSpec((tm, C), lambda i: (i, 0)), pl.BlockSpec((tm, C), lambda i: (i, 0))],
                   out_shape=[_sds((S, C), BF16), _sds((S, C), F32)],
                   scratch_shapes=[pltpu.VMEM((SUBLANES, tm + HALO_A, C), F32)],
                   compiler_params=_cp(("parallel",)))(z, z, z, z, cw, cb, cn)


def _conv_a_bwd(da, a1, z, cw, cn):
    S = z.shape[0]
    C = D_CONV
    tm = TOK_TILE
    hb = tm // HALO_A
    nt = S // tm
    W = CONV_A_WIDTH

    def body(da_ref, a1_ref, dan_ref, a1n_ref, u_ref, gt_ref, up_ref, gp_ref, cw_ref, cn_ref,
             duz_ref, dcw_ref, dcb_ref, dcn_ref, win, dwin):
        i = pl.program_id(0)

        @pl.when(i == 0)
        def _():
            dcw_ref[...] = jnp.zeros_like(dcw_ref)
            dcb_ref[...] = jnp.zeros_like(dcb_ref)
            dcn_ref[...] = jnp.zeros_like(dcn_ref)

        cnv = cn_ref[...]

        def da1_of(dav, a1v):
            a2 = a1v * _rms(a1v) * cnv
            da2 = dav * _silu_grad(a2)
            dx, xh = _rms_bwd(da2, a1v, cnv)
            return dx, da2 * xh

        da1, dcn_t = da1_of(da_ref[...], a1_ref[...])
        da1n, _ = da1_of(dan_ref[...], a1n_ref[...])
        dwin[0, pl.ds(0, tm), :] = da1
        dwin[0, pl.ds(tm, HALO_A), :] = jnp.where(i == nt - 1, 0.0, da1n)
        _fill_shifts(dwin, tm + HALO_A)
        dcb_ref[...] += jnp.sum(da1, axis=0, keepdims=True)
        dcn_ref[...] += jnp.sum(dcn_t, axis=0, keepdims=True)

        prev = up_ref[...].astype(F32) * jax.nn.sigmoid(gp_ref[...].astype(F32))
        win[0, pl.ds(0, HALO_A), :] = jnp.where(i == 0, 0.0, prev)
        win[0, pl.ds(HALO_A, tm), :] = u_ref[...].astype(F32) * jax.nn.sigmoid(gt_ref[...].astype(F32))
        _fill_shifts(win, tm + HALO_A)

        def rows_block(rb, carry):
            r0 = pl.multiple_of(rb * CONV_ROWS, CONV_ROWS)
            rows = pl.ds(r0, CONV_ROWS)
            da1_b = dwin[0, rows, :]
            da0 = jnp.zeros((CONV_ROWS, C), F32)
            for k in range(W):
                da0 = da0 + cw_ref[k:k + 1, :] * _tap(dwin, W - 1 - k, CONV_ROWS, r0)
                dcw_ref[k:k + 1, :] += jnp.sum(da1_b * _tap(win, HALO_A - (W - 1) + k, CONV_ROWS, r0), axis=0, keepdims=True)
            u = u_ref[rows, :].astype(F32)
            sg = jax.nn.sigmoid(gt_ref[rows, :].astype(F32))
            duz_ref[rows, 0:C] = (da0 * sg).astype(BF16)
            duz_ref[rows, C:2 * C] = (da0 * u * sg * (1.0 - sg)).astype(BF16)
            return carry

        lax.fori_loop(0, tm // CONV_ROWS, rows_block, 0)

    cur = lambda c: pl.BlockSpec((tm, C), lambda i, c=c: (i, c))
    prv = lambda c: pl.BlockSpec((HALO_A, C), lambda i, c=c: (jnp.maximum(i * hb - 1, 0), c))
    nxt = pl.BlockSpec((HALO_A, C), lambda i: (jnp.minimum((i + 1) * hb, S // HALO_A - 1), 0))
    vec = pl.BlockSpec((1, C), lambda i: (0, 0))
    return _pallas(body, name="conv_a_bwd", grid=(nt,),
                   in_specs=[cur(0), cur(0), nxt, nxt, cur(0), cur(1), prv(0), prv(1),
                             pl.BlockSpec((32, C), lambda i: (0, 0)), vec],
                   out_specs=[pl.BlockSpec((tm, 2 * C), lambda i: (i, 0)), pl.BlockSpec((32, C), lambda i: (0, 0)), vec, vec],
                   out_shape=[_sds((S, 2 * C), BF16), _sds((32, C), F32), _sds((1, C), F32), _sds((1, C), F32)],
                   scratch_shapes=[pltpu.VMEM((SUBLANES, tm + HALO_A, C), F32)] * 2,
                   compiler_params=_cp(("arbitrary",)))(da, a1, da, a1, z, z, z, z, cw, cn)


def _forget_scan(fl, bf):
    S, L = fl.shape
    B = SCAN_BLK

    def body(fl_ref, bf_ref, flb_ref, F_ref):
        tri = (lax.broadcasted_iota(jnp.int32, (B, B), 0) >= lax.broadcasted_iota(jnp.int32, (B, B), 1)).astype(F32)

        def step(c, carry):
            rows = pl.ds(pl.multiple_of(c * B, B), B)
            v = fl_ref[rows, :] + bf_ref[...]
            flb_ref[rows, :] = v
            lf = jnp.minimum(v, 0.0) - jnp.log1p(jnp.exp(-jnp.abs(v)))
            cs = jnp.dot(tri, lf, precision=lax.Precision.HIGHEST, preferred_element_type=F32) + carry
            F_ref[rows, :] = cs
            return cs[B - 1:B, :]

        lax.fori_loop(0, S // B, step, jnp.zeros((1, L), F32))

    return _pallas(body, name="forget_scan", out_shape=[_sds((S, L), F32), _sds((S, L), F32)],
                   compiler_params=_cp())(fl, bf)


def _forget_scan_bwd(dF, flb):
    S, L = dF.shape
    B = SCAN_BLK
    nb = S // B

    def body(dF_ref, flb_ref, dfl_ref, db_ref):
        tri = (lax.broadcasted_iota(jnp.int32, (B, B), 0) <= lax.broadcasted_iota(jnp.int32, (B, B), 1)).astype(F32)

        def step(t, carry):
            carry_cs, db = carry
            rows = pl.ds(pl.multiple_of((nb - 1 - t) * B, B), B)
            cs = jnp.dot(tri, dF_ref[rows, :], precision=lax.Precision.HIGHEST, preferred_element_type=F32) + carry_cs
            dfl = cs * jax.nn.sigmoid(-flb_ref[rows, :])
            dfl_ref[rows, :] = dfl
            return cs[0:1, :], db + jnp.sum(dfl, axis=0, keepdims=True)

        _, db = lax.fori_loop(0, nb, step, (jnp.zeros((1, L), F32), jnp.zeros((1, L), F32)))
        db_ref[...] = db

    return _pallas(body, name="forget_scan_bwd", out_shape=[_sds((S, L), F32), _sds((1, L), F32)],
                   compiler_params=_cp())(dF, flb)


NEG = -1e30


def _causal_mask(t):
    return lax.broadcasted_iota(jnp.int32, (t, t), 0) >= lax.broadcasted_iota(jnp.int32, (t, t), 1)


AUG = 128
C_F, C_ONE, C_LSE = 64, 67, 70


def _split3(f):
    a = f.astype(BF16).astype(F32)
    r = f - a
    b = r.astype(BF16).astype(F32)
    return a, b, r - b


def _put3(lane, base, parts, other):
    out = other
    for k, p in enumerate(parts):
        out = jnp.where(lane == base + k, p, out)
    return out


def _ones3(lane, base):
    return (lane >= base) & (lane < base + 3)


def _lane_ids(rows):
    return lax.broadcasted_iota(jnp.int32, (rows, AUG), 1)


def _pair_rms(x, lo):
    sq = x * x
    ms_a = jnp.sum(jnp.where(lo, sq, 0.0), axis=-1, keepdims=True) * (1.0 / HEAD_DIM)
    ms_b = jnp.sum(jnp.where(lo, 0.0, sq), axis=-1, keepdims=True) * (1.0 / HEAD_DIM)
    return jnp.where(lo, lax.rsqrt(ms_a + EPS), lax.rsqrt(ms_b + EPS))


def _qkv_prep(z, Fc, qw, kw):
    S = z.shape[0]
    tp = min(QKN_TILE, S)
    scale = 1.0 / math.sqrt(HEAD_DIM)

    def body(zq_ref, zk_ref, zv_ref, F_ref, qw_ref, kw_ref, q_ref, k_ref, v_ref):
        j = pl.program_id(0)
        lane = _lane_ids(tp)
        lo = lane < HEAD_DIM
        Fv = F_ref[...]
        xq = zq_ref[...].astype(F32)
        xk = zk_ref[...].astype(F32)
        qn = xq * _pair_rms(xq, lo) * qw_ref[...] * scale
        kn = xk * _pair_rms(xk, lo) * kw_ref[...]
        vv = zv_ref[...].astype(F32)
        for half in range(2):
            take = (lambda a: a) if half == 0 else (lambda a: pltpu.roll(a, HEAD_DIM, 1))
            fp = _split3(jnp.sum(jnp.where(lane == 2 * j + half, Fv, 0.0), axis=-1, keepdims=True))
            qx = _put3(lane, C_F, fp, jnp.where(_ones3(lane, C_ONE), 1.0, 0.0))
            kx = _put3(lane, C_ONE, [-p for p in fp], jnp.where(_ones3(lane, C_F) | _ones3(lane, C_LSE), 1.0, 0.0))
            vx = jnp.where(_ones3(lane, C_F), 1.0, 0.0)
            q_ref[half] = jnp.where(lo, take(qn), qx).astype(BF16)
            k_ref[half] = jnp.where(lo, take(kn), kx).astype(BF16)
            v_ref[half] = jnp.where(lo, take(vv), vx).astype(BF16)

    col = lambda c0: pl.BlockSpec((tp, AUG), lambda j, i, c0=c0: (i, c0 + j))
    vec = pl.BlockSpec((1, AUG), lambda j, i: (0, 0))
    out = pl.BlockSpec((2, tp, AUG), lambda j, i: (j, i, 0))
    return _pallas(body, name="qkv_prep", grid=(N_HEADS // 2, S // tp),
                   in_specs=[col(8), col(12), col(16), pl.BlockSpec((tp, AUG), lambda j, i: (i, 0)), vec, vec],
                   out_specs=[out, out, out], out_shape=[_sds((N_HEADS, S, AUG), BF16)] * 3,
                   compiler_params=_cp(("parallel", "parallel")))(z, z, z, Fc, qw, kw)


def _fox_fwd(q_aug, k_aug, v_aug):
    H, S, A = q_aug.shape
    t = ATT_TILE
    nq = S // t

    def body(q_ref, k_ref, v_ref, o_ref, q2_ref):
        i = pl.program_id(1)
        q = q_ref[0]

        def tile(j, carry, diag):
            m, acc = carry
            rows = pl.ds(pl.multiple_of(j * t, t), t)
            s = _dot_nt(q, k_ref[0, rows, :])
            if diag:
                s = jnp.where(_causal_mask(t), s, NEG)
            m_new = jnp.maximum(m, jnp.max(s, axis=-1, keepdims=True))
            p = jnp.exp(s - m_new)
            acc = jnp.exp(m - m_new) * acc + _dot(p.astype(BF16), v_ref[0, rows, :])
            return m_new, acc

        init = (jnp.full((t, 1), NEG, F32), jnp.zeros((t, A), F32))
        carry = lax.fori_loop(0, i, lambda j, c: tile(j, c, False), init)
        m, acc = tile(i, carry, True)
        lane = _lane_ids(t)
        l = jnp.sum(jnp.where(lane == C_F, acc, 0.0), axis=-1, keepdims=True)
        o_ref[0] = (acc / l).astype(BF16)
        lse = m + jnp.log(l)
        q2_ref[0] = (q.astype(F32) + _put3(lane, C_LSE, [-p for p in _split3(lse)], 0.0)).astype(BF16)

    qblk = pl.BlockSpec((1, t, A), lambda h, i: (h, i, 0))
    full = pl.BlockSpec((1, S, A), lambda h, i: (h, 0, 0))
    return _pallas(body, name="fox_fwd", grid=(H, nq), in_specs=[qblk, full, full], out_specs=[qblk, qblk],
                   out_shape=[_sds((H, S, A), BF16)] * 2, compiler_params=_cp(("parallel", "parallel")))(q_aug, k_aug, v_aug)


def _do_prep(dcat, o_aug):
    S = dcat.shape[0]
    tp = min(QKN_TILE, S)

    def body(d_ref, o_ref, out_ref):
        lane = _lane_ids(tp)
        lo = lane < HEAD_DIM
        x = d_ref[...]
        for half in range(2):
            d = jnp.where(lo, x if half == 0 else pltpu.roll(x, HEAD_DIM, 1), 0.0)
            delta = jnp.sum(d * o_ref[half].astype(F32), axis=-1, keepdims=True)
            out_ref[half] = jnp.where(lo, d, _put3(lane, C_F, [-p for p in _split3(delta)], 0.0)).astype(BF16)

    pair = pl.BlockSpec((2, tp, AUG), lambda j, i: (j, i, 0))
    return _pallas(body, name="do_prep", grid=(N_HEADS // 2, S // tp),
                   in_specs=[pl.BlockSpec((tp, AUG), lambda j, i: (i, D_CONV // AUG + j)), pair], out_specs=pair,
                   out_shape=_sds((N_HEADS, S, AUG), BF16), compiler_params=_cp(("parallel", "parallel")))(dcat, o_aug)


def _fox_bwd(q2, k_aug, v_aug, do_aug):
    H, S, A = q2.shape
    t = ATT_TILE
    nq = S // t

    def body(q_ref, k_ref, v_ref, do_ref, dq_ref, dk_ref, dv_ref):
        j = pl.program_id(1)

        @pl.when(j == 0)
        def _():
            dq_ref[...] = jnp.zeros_like(dq_ref)

        k = k_ref[0]
        vv = v_ref[0]

        def tile(i, carry, diag):
            dk, dv = carry
            rows = pl.ds(pl.multiple_of(i * t, t), t)
            q = q_ref[0, rows, :]
            dov = do_ref[0, rows, :]
            s = _dot_nt(q, k)
            if diag:
                s = jnp.where(_causal_mask(t), s, NEG)
            p = jnp.exp(s)
            dv = dv + _dot_tn(p.astype(BF16), dov)
            dsb = (p * _dot_nt(dov, vv)).astype(BF16)
            dq_ref[0, rows, :] += _dot(dsb, k)
            dk = dk + _dot_tn(dsb, q)
            return dk, dv

        init = (jnp.zeros((t, A), F32), jnp.zeros((t, A), F32))
        carry = tile(j, init, True)
        dk, dv = lax.fori_loop(j + 1, nq, lambda i, c: tile(i, c, False), carry)
        dk_ref[0] = dk
        dv_ref[0] = dv

    full = pl.BlockSpec((1, S, A), lambda h, j: (h, 0, 0))
    kblk = pl.BlockSpec((1, t, A), lambda h, j: (h, j, 0))
    return _pallas(body, name="fox_bwd", grid=(H, nq), in_specs=[full, kblk, kblk, full], out_specs=[full, kblk, kblk],
                   out_shape=[_sds((H, S, A), F32)] * 3,
                   compiler_params=_cp(("parallel", "arbitrary")))(q2, k_aug, v_aug, do_aug)


def _qkv_bwd(dq, dk, dv, z, qw, kw):
    S = z.shape[0]
    tp = min(QKN_TILE, S)
    scale = 1.0 / math.sqrt(HEAD_DIM)

    def body(dq_ref, dk_ref, dv_ref, zq_ref, zk_ref, qw_ref, kw_ref, dqf_ref, dkf_ref, dvf_ref, dF_ref, dqw_ref, dkw_ref):
        i, j = pl.program_id(0), pl.program_id(1)
        lane = _lane_ids(tp)
        lo = lane < HEAD_DIM

        @pl.when((i == 0) & (j == 0))
        def _():
            dqw_ref[...] = jnp.zeros_like(dqw_ref)
            dkw_ref[...] = jnp.zeros_like(dkw_ref)

        def pair(ref):
            return jnp.where(lo, ref[0], pltpu.roll(ref[1], HEAD_DIM, 1))

        def norm_bwd(g, x, w):
            r = _pair_rms(x, lo)
            xh = x * r
            dxh = g * w
            tt = dxh * xh
            mean_a = jnp.sum(jnp.where(lo, tt, 0.0), axis=-1, keepdims=True) * (1.0 / HEAD_DIM)
            mean_b = jnp.sum(jnp.where(lo, 0.0, tt), axis=-1, keepdims=True) * (1.0 / HEAD_DIM)
            return r * (dxh - xh * jnp.where(lo, mean_a, mean_b)), g * xh

        dxq, gq = norm_bwd(pair(dq_ref) * scale, zq_ref[...].astype(F32), qw_ref[...])
        dqf_ref[...] = dxq.astype(BF16)
        dqw_ref[...] += jnp.sum(gq, axis=0, keepdims=True)
        dxk, gk = norm_bwd(pair(dk_ref), zk_ref[...].astype(F32), kw_ref[...])
        dkf_ref[...] = dxk.astype(BF16)
        dkw_ref[...] += jnp.sum(gk, axis=0, keepdims=True)
        dvf_ref[...] = pair(dv_ref).astype(BF16)

        contrib = jnp.zeros((tp, AUG), F32)
        for half in range(2):
            df = (jnp.sum(jnp.where(lane == C_F, dq_ref[half], 0.0), axis=-1, keepdims=True)
                  - jnp.sum(jnp.where(lane == C_ONE, dk_ref[half], 0.0), axis=-1, keepdims=True))
            contrib = jnp.where(lane == 2 * j + half, df, contrib)

        @pl.when(j == 0)
        def _():
            dF_ref[...] = contrib

        @pl.when(j > 0)
        def _():
            dF_ref[...] += contrib

    pairb = pl.BlockSpec((2, tp, AUG), lambda i, j: (j, i, 0))
    col = lambda c0: pl.BlockSpec((tp, AUG), lambda i, j, c0=c0: (i, c0 + j))
    vec = pl.BlockSpec((1, AUG), lambda i, j: (0, 0))
    flat = pl.BlockSpec((tp, AUG), lambda i, j: (i, j))
    return _pallas(body, name="qkv_bwd", grid=(S // tp, N_HEADS // 2),
                   in_specs=[pairb, pairb, pairb, col(8), col(12), vec, vec],
                   out_specs=[flat, flat, flat, pl.BlockSpec((tp, AUG), lambda i, j: (i, 0)), vec, vec],
                   out_shape=[_sds((S, D_ATTN), BF16)] * 3 + [_sds((S, AUG), F32), _sds((1, AUG), F32), _sds((1, AUG), F32)],
                   compiler_params=_cp(("arbitrary", "arbitrary")))(dq, dk, dv, z, z, qw, kw)


def _proj_res_heads(a, wa, o_aug, wo, res):
    S, D = res.shape
    H = o_aug.shape[0]
    tm = TOK_TILE

    def body(a_ref, wa_ref, o_ref, wo_ref, res_ref, out_ref):
        acc = res_ref[...] + _dot(a_ref[...], wa_ref[...])
        for h in range(H):
            acc = acc + _dot(o_ref[h], wo_ref[h])
        out_ref[...] = acc

    row = pl.BlockSpec((tm, D), lambda i: (i, 0))
    return _pallas(body, name="proj_res_heads", grid=(S // tm,),
                   in_specs=[pl.BlockSpec((tm, a.shape[1]), lambda i: (i, 0)), pl.BlockSpec(wa.shape, lambda i: (0, 0)),
                             pl.BlockSpec((H, tm, AUG), lambda i: (0, i, 0)), pl.BlockSpec(wo.shape, lambda i: (0, 0, 0)), row],
                   out_specs=row, out_shape=_sds((S, D), F32), compiler_params=_cp(("parallel",)))(a, wa, o_aug, wo, res)


def _heads_tn(o_aug, d):
    H, S, A = o_aug.shape
    D = d.shape[1]
    tm = min(DW_TILE, S)
    nt = S // tm

    def body(o_ref, d_ref, out_ref, acc_ref):
        i = pl.program_id(0)

        @pl.when(i == 0)
        def _():
            acc_ref[...] = jnp.zeros_like(acc_ref)

        dv = d_ref[...].astype(BF16)
        for h in range(H):
            acc_ref[h] += _dot_tn(o_ref[h], dv)

        @pl.when(i == nt - 1)
        def _():
            out_ref[...] = acc_ref[...].astype(BF16)

    return _pallas(body, name="heads_tn", grid=(nt,),
                   in_specs=[pl.BlockSpec((H, tm, A), lambda i: (0, i, 0)), pl.BlockSpec((tm, D), lambda i: (i, 0))],
                   out_specs=pl.BlockSpec((H, A, D), lambda i: (0, 0, 0)), out_shape=_sds((H, A, D), BF16),
                   scratch_shapes=[pltpu.VMEM((H, A, D), F32)], compiler_params=_cp(("arbitrary",)))(o_aug, d)


def _odd_mid_fwd(z, cw):
    S = z.shape[0]
    D = z.shape[1] // 3
    tm = TOK_TILE
    hb = tm // HALO_C
    W = CONV_C_WIDTH

    def body(gb_ref, gc_ref, hh_ref, gcp_ref, hhp_ref, cw_ref, y_ref, win):
        i = pl.program_id(0)
        prev = gcp_ref[...].astype(F32) * hhp_ref[...].astype(F32)
        win[pl.ds(0, HALO_C), :] = jnp.where(i == 0, 0.0, prev)
        win[pl.ds(HALO_C, tm), :] = gc_ref[...].astype(F32) * hh_ref[...].astype(F32)
        c1 = jnp.zeros((tm, D), F32)
        for k in range(W):
            c1 = c1 + cw_ref[k:k + 1, :] * win[pl.ds(HALO_C - (W - 1) + k, tm), :]
        y_ref[...] = (gb_ref[...].astype(F32) * c1).astype(BF16)

    cur = lambda c: pl.BlockSpec((tm, D), lambda i, c=c: (i, c))
    prv = lambda c: pl.BlockSpec((HALO_C, D), lambda i, c=c: (jnp.maximum(i * hb - 1, 0), c))
    return _pallas(body, name="odd_mid_fwd", grid=(S // tm,),
                   in_specs=[cur(0), cur(1), cur(2), prv(1), prv(2), pl.BlockSpec((8, D), lambda i: (0, 0))],
                   out_specs=pl.BlockSpec((tm, D), lambda i: (i, 0)), out_shape=_sds((S, D), BF16),
                   scratch_shapes=[pltpu.VMEM((tm + HALO_C, D), F32)],
                   compiler_params=_cp(("parallel",)))(z, z, z, z, z, cw)


def _odd_mid_bwd(dy, z, cw):
    S = z.shape[0]
    D = z.shape[1] // 3
    tm = TOK_TILE
    hb = tm // HALO_C
    nt = S // tm
    W = CONV_C_WIDTH

    def body(dy_ref, dyn_ref, gb_ref, gbn_ref, gc_ref, hh_ref, gcp_ref, hhp_ref, cw_ref, dz_ref, dcw_ref, win, dwin):
        i = pl.program_id(0)

        @pl.when(i == 0)
        def _():
            dcw_ref[...] = jnp.zeros_like(dcw_ref)

        gc = gc_ref[...].astype(F32)
        hh = hh_ref[...].astype(F32)
        prev = gcp_ref[...].astype(F32) * hhp_ref[...].astype(F32)
        win[pl.ds(0, HALO_C), :] = jnp.where(i == 0, 0.0, prev)
        win[pl.ds(HALO_C, tm), :] = gc * hh
        dyv = dy_ref[...]
        dc1 = dyv * gb_ref[...].astype(F32)
        dwin[pl.ds(0, tm), :] = dc1
        dwin[pl.ds(tm, HALO_C), :] = jnp.where(i == nt - 1, 0.0, dyn_ref[...] * gbn_ref[...].astype(F32))
        c1 = jnp.zeros((tm, D), F32)
        dc0 = jnp.zeros((tm, D), F32)
        for k in range(W):
            tap = win[pl.ds(HALO_C - (W - 1) + k, tm), :]
            c1 = c1 + cw_ref[k:k + 1, :] * tap
            dc0 = dc0 + cw_ref[k:k + 1, :] * dwin[pl.ds(W - 1 - k, tm), :]
            dcw_ref[k:k + 1, :] += jnp.sum(dc1 * tap, axis=0, keepdims=True)
        dz_ref[:, 0:D] = (dyv * c1).astype(BF16)
        dz_ref[:, D:2 * D] = (dc0 * hh).astype(BF16)
        dz_ref[:, 2 * D:3 * D] = (dc0 * gc).astype(BF16)

    cur = lambda c: pl.BlockSpec((tm, D), lambda i, c=c: (i, c))
    prv = lambda c: pl.BlockSpec((HALO_C, D), lambda i, c=c: (jnp.maximum(i * hb - 1, 0), c))
    nxt = pl.BlockSpec((HALO_C, D), lambda i: (jnp.minimum((i + 1) * hb, S // HALO_C - 1), 0))
    return _pallas(body, name="odd_mid_bwd", grid=(nt,),
                   in_specs=[cur(0), nxt, cur(0), nxt, cur(1), cur(2), prv(1), prv(2), pl.BlockSpec((8, D), lambda i: (0, 0))],
                   out_specs=[pl.BlockSpec((tm, 3 * D), lambda i: (i, 0)), pl.BlockSpec((8, D), lambda i: (0, 0))],
                   out_shape=[_sds((S, 3 * D), BF16), _sds((8, D), F32)],
                   scratch_shapes=[pltpu.VMEM((tm + HALO_C, D), F32), pltpu.VMEM((tm + HALO_C, D), F32)],
                   compiler_params=_cp(("arbitrary",)))(dy, dy, z, z, z, z, z, z, cw)


def _loss_head(y, tgt):
    S, D = y.shape
    tm = TOK_TILE

    def body(y_ref, t_ref, dy_ref, l_ref):
        @pl.when(pl.program_id(0) == 0)
        def _():
            l_ref[...] = jnp.zeros_like(l_ref)

        e = y_ref[...] - t_ref[...]
        dy_ref[...] = e * (1.0 / D)
        l_ref[...] += jnp.sum(jnp.sum(e * e, axis=-1, keepdims=True), axis=0, keepdims=True) * (0.5 / D)

    row = pl.BlockSpec((tm, D), lambda i: (i, 0))
    return _pallas(body, name="loss_head", grid=(S // tm,), in_specs=[row, row],
                   out_specs=[row, pl.BlockSpec((1, 1), lambda i: (0, 0))],
                   out_shape=[_sds((S, D), F32), _sds((1, 1), F32)],
                   compiler_params=_cp(("arbitrary",)))(y, tgt)


def _pad_rows(a, rows):
    return jnp.pad(a, ((0, rows - a.shape[0]), (0, 0)))


def _local_step(x, tgt, W, need=lambda block, after: None, done=lambda block, block_grads: None):
    S, D = x.shape
    grads = {}
    saved = {}

    def gain_after(gain, token):
        return gain if token is None else gain + token

    def ffn_f(tag, l, xin):
        need((tag, l), xin)
        out, xn, G, U = _ffn_fwd(xin, W[tag + "_norm"][l:l + 1], W[tag + "_w_gate"][l], W[tag + "_w_up"][l],
                                 W[tag + "_w_down"][l])
        saved[(tag, l)] = (xin, xn, G, U)
        return out

    def ffn_b(tag, l, dout):
        xin, xn, G, U = saved[(tag, l)]
        keys = [(tag + "_w_gate", l), (tag + "_w_up", l), (tag + "_w_down", l)]
        *dws, dG, dU = _ffn_bwd_w(dout, xn, G, U, W[tag + "_w_down"][l])
        big = dict(zip(keys, dws))
        grads.update(big)
        token = done((tag, l), big)
        dx, dg = _norm_in_bwd([dG, dU], [W[tag + "_w_gate"][l], W[tag + "_w_up"][l]], xin,
                              gain_after(W[tag + "_norm"][l:l + 1], token), dout, w_rows=True)
        grads[(tag + "_norm", l)] = dg
        return dx

    x0a = ffn_f("ffn1", 0, x)
    need(("ev", 0), x0a)
    w_in = W["ev_w_in"]
    w_main, w_f = w_in[:, :2560], jnp.pad(w_in[:, 2560:], ((0, 0), (0, 120)))
    h0, z0, fl = _norm_proj(x0a, W["mix_norm"][0:1], w_main, w_f)
    cw_a = _pad_rows(W["ev_conv_w"], 32)
    a_act, a1 = _conv_a_fwd(z0, cw_a, W["ev_conv_b"], W["ev_conv_norm"])
    flb, Fc = _forget_scan(fl, jnp.pad(W["ev_b_f"], ((0, 0), (0, 120))))
    qw2, kw2 = jnp.tile(W["ev_q_norm"], (1, 2)), jnp.tile(W["ev_k_norm"], (1, 2))
    q_aug, k_aug, v_aug = _qkv_prep(z0, Fc, qw2, kw2)
    o_aug, q_lse = _fox_fwd(q_aug, k_aug, v_aug)
    w_out_e = W["ev_w_out"]
    w_out_o = jnp.pad(w_out_e[D_CONV:].reshape(N_HEADS, HEAD_DIM, D), ((0, 0), (0, AUG - HEAD_DIM), (0, 0)))
    x0b = _proj_res_heads(a_act, w_out_e[:D_CONV], o_aug, w_out_o, x0a)
    x0c = ffn_f("ffn2", 0, x0b)
    x1a = ffn_f("ffn1", 1, x0c)
    need(("od", 0), x1a)
    h1, z1 = _norm_proj(x1a, W["mix_norm"][1:2], W["od_w_in"])
    cw_c = _pad_rows(W["od_conv_w"], 8)
    y1 = _odd_mid_fwd(z1, cw_c)
    x1b = _proj_res([y1], [W["od_w_out"]], x1a)
    x1c = ffn_f("ffn2", 1, x1b)
    dy, loss = _loss_head(x1c, tgt)

    d = ffn_b("ffn2", 1, dy)
    dy1 = _matmul_nt(d, W["od_w_out"])
    grads[("od_w_out", 0)] = _matmul_tn(y1, d, D)[0]
    dz1, dcw_c = _odd_mid_bwd(dy1, z1, cw_c)
    grads[("od_conv_w", 0)] = dcw_c[:CONV_C_WIDTH]
    grads[("od_w_in", 0)] = _matmul_tn(h1, dz1, 3 * D // 4)
    token = done(("od", 0), {k: grads[k] for k in (("od_w_out", 0), ("od_w_in", 0))})
    d, dg = _norm_in_bwd([dz1[None]], [W["od_w_in"][None]], x1a, gain_after(W["mix_norm"][1:2], token), d)
    grads[("mix_norm", 1)] = dg
    d = ffn_b("ffn1", 1, d)
    d = ffn_b("ffn2", 0, d)
    dcat = _matmul_nt(d, w_out_e)
    grads[("ev_w_out", 0)] = jnp.concatenate([_matmul_tn(a_act, d, D)[0],
                                              _heads_tn(o_aug, d)[:, :HEAD_DIM].reshape(D_ATTN, D)], axis=0)
    duz, dcw_a, dcb, dcn = _conv_a_bwd(dcat, a1, z0, cw_a, W["ev_conv_norm"])
    grads[("ev_conv_w", 0)] = dcw_a[:CONV_A_WIDTH]
    grads[("ev_conv_b", 0)] = dcb
    grads[("ev_conv_norm", 0)] = dcn
    dq_a, dk_a, dv_a = _fox_bwd(q_lse, k_aug, v_aug, _do_prep(dcat, o_aug))
    dqf, dkf, dvf, dF, dqw, dkw = _qkv_bwd(dq_a, dk_a, dv_a, z0, qw2, kw2)
    grads[("ev_q_norm", 0)] = dqw[:, :HEAD_DIM] + dqw[:, HEAD_DIM:]
    grads[("ev_k_norm", 0)] = dkw[:, :HEAD_DIM] + dkw[:, HEAD_DIM:]
    dfl, dbf = _forget_scan_bwd(dF, flb)
    grads[("ev_b_f", 0)] = dbf[:, :N_HEADS]
    dz0 = jnp.concatenate([duz, dqf, dkf, dvf], axis=1)
    dflb = dfl.astype(BF16)
    gmain = _matmul_tn(h0, dz0, 640)
    gmain = gmain.transpose(1, 0, 2).reshape(D, 2560)
    gf = _matmul_tn(h0, dflb, 128)[0][:, :N_HEADS]
    grads[("ev_w_in", 0)] = jnp.concatenate([gmain, gf], axis=1)
    token = done(("ev", 0), {k: grads[k] for k in (("ev_w_out", 0), ("ev_w_in", 0))})
    d, dg = _norm_in_bwd([dz0[None], dflb[None]], [w_main[None], w_f[None]], x0a, gain_after(W["mix_norm"][0:1], token), d)
    grads[("mix_norm", 0)] = dg
    d = ffn_b("ffn1", 0, d)
    return loss, d, grads


def _place():
    x, y, c = lax.axis_index("x"), lax.axis_index("y"), lax.axis_index("c")
    chips = [(1 - x, y), (x, 1 - y), (1 - x, 1 - y)]
    return x, y, c, chips


def _remote(src, dst, send_sem, recv_sem, to):
    return pltpu.make_async_remote_copy(src_ref=src, dst_ref=dst, send_sem=send_sem, recv_sem=recv_sem,
                                        device_id=to, device_id_type=MESH)


HBM = pl.BlockSpec(memory_space=pltpu.HBM)
SEM = pl.BlockSpec(memory_space=pltpu.SEMAPHORE)
EFFECT = pltpu.SideEffectType.DATAFLOW_SIDE_EFFECTING


def _in_hbm(a):
    return pltpu.with_memory_space_constraint(a, pltpu.HBM)


def _ag_start(tag, bufs, with_taps):
    n = len(bufs)
    order = ([n - 1] + list(range(n - 1))) if with_taps else list(range(n))

    def body(*refs):
        send_sems, recv_sems = refs[n], refs[n + 1]
        outs, token = refs[n + 2:2 * n + 2], refs[2 * n + 2]
        x, y, c, chips = _place()
        me = 2 * x + y
        for a in order:
            if with_taps and a == n - 1:
                blk = outs[a].at[me]
            else:
                h = outs[a].shape[1] // 2
                blk = outs[a].at[me, pl.ds(c * h, h)]
            for jj, (px, py) in enumerate(chips):
                _remote(blk, blk, send_sems.at[3 * a + jj], recv_sems.at[3 * a + jj], (px, py, c)).start()
        token[...] = jnp.zeros_like(token)

    return _pallas(
        body, name=f"gather_start_{tag}",
        out_shape=[pltpu.SemaphoreType.DMA((3 * n,)), pltpu.SemaphoreType.DMA((3 * n,))]
        + [pltpu.HBM(b.shape, b.dtype) for b in bufs] + [_sds((8, 128), F32)],
        in_specs=[HBM] * n, out_specs=[SEM, SEM] + [HBM] * n + [pl.BlockSpec(memory_space=pltpu.VMEM)],
        input_output_aliases={a: 2 + a for a in range(n)},
        compiler_params=pltpu.CompilerParams(has_side_effects=EFFECT),
    )(*[_in_hbm(b) for b in bufs])


def _ag_mid(g, ici_send, ici_recv, bufs, idx, taps, n_big, after):
    n = len(bufs)
    arrs = list(bufs) + ([taps] if taps is not None else [])
    m = len(arrs)

    def body(*refs):
        ici_s, ici_r = refs[0], refs[1]
        d_send, d_recv = refs[m + 3], refs[m + 4]
        outs = refs[m + 5:]
        x, y, c, chips = _place()
        me = 2 * x + y
        for i in range(m):
            a = idx[i] if i < n else n_big
            for jj, (px, py) in enumerate(chips):
                k = 3 * a + jj
                if i < n:
                    h = outs[i].shape[1] // 2
                    mine, blk = outs[i].at[me, pl.ds(c * h, h)], outs[i].at[2 * px + py, pl.ds(c * h, h)]
                else:
                    mine, blk = outs[i].at[me], outs[i].at[2 * px + py]
                _remote(mine, mine, ici_s.at[k], ici_r.at[k], (px, py, c)).wait_send()
                _remote(blk, blk, ici_s.at[k], ici_r.at[k], (px, py, c)).wait_recv()
                if i < n:
                    _remote(blk, blk, d_send.at[3 * i + jj], d_recv.at[3 * i + jj], (x, y, 1 - c)).start()

    return _pallas(
        body, name=f"gather_pass_on_{g}",
        out_shape=[pltpu.SemaphoreType.DMA((3 * n,)), pltpu.SemaphoreType.DMA((3 * n,))] + [pltpu.HBM(b.shape, b.dtype) for b in arrs],
        in_specs=[SEM, SEM] + [HBM] * m + [ANY], out_specs=[SEM, SEM] + [HBM] * m,
        input_output_aliases={2 + i: 2 + i for i in range(m)},
        compiler_params=pltpu.CompilerParams(has_side_effects=EFFECT),
    )(ici_send, ici_recv, *arrs, after)


def _ag_wait(g, d_send, d_recv, arrs, n, after):
    m = len(arrs)

    def body(*refs):
        d_s, d_r = refs[0], refs[1]
        outs = refs[m + 3:]
        x, y, c, chips = _place()
        for i in range(n):
            h = outs[i].shape[1] // 2
            for jj, (px, py) in enumerate(chips):
                sent = outs[i].at[2 * px + py, pl.ds(c * h, h)]
                got = outs[i].at[2 * px + py, pl.ds((1 - c) * h, h)]
                _remote(sent, sent, d_s.at[3 * i + jj], d_r.at[3 * i + jj], (x, y, 1 - c)).wait_send()
                _remote(got, got, d_s.at[3 * i + jj], d_r.at[3 * i + jj], (x, y, 1 - c)).wait_recv()

    return _pallas(
        body, name=f"gather_wait_{g}", out_shape=[pltpu.HBM(b.shape, b.dtype) for b in arrs],
        in_specs=[SEM, SEM] + [HBM] * m + [ANY], out_specs=[HBM] * m,
        input_output_aliases={2 + i: i for i in range(m)},
        compiler_params=pltpu.CompilerParams(has_side_effects=EFFECT),
    )(d_send, d_recv, *arrs, after)


def _pair_start(g, gs, after):
    n = len(gs)
    zones = [lax.empty((4, a.shape[1] // 2, a.shape[2]), a.dtype) for a in gs]
    extra = [] if after is None else [after]

    def body(*refs):
        k0 = 2 * n + len(extra)
        send_sems, recv_sems = refs[k0], refs[k0 + 1]
        src, dst = refs[k0 + 2:k0 + 2 + n], refs[k0 + 2 + n:k0 + 2 + 2 * n]
        token = refs[k0 + 2 + 2 * n]
        x, y, c, _ = _place()
        for a in range(n):
            h = src[a].shape[1] // 2
            _remote(src[a].at[:, pl.ds((1 - c) * h, h)], dst[a], send_sems.at[a], recv_sems.at[a], (x, y, 1 - c)).start()
        token[...] = jnp.zeros_like(token)

    return _pallas(
        body, name=f"grad_pair_start_{g}",
        out_shape=[pltpu.SemaphoreType.DMA((n,)), pltpu.SemaphoreType.DMA((n,))]
        + [pltpu.HBM(a.shape, a.dtype) for a in gs + zones] + [_sds((8, 128), F32)],
        in_specs=[HBM] * (2 * n) + [ANY] * len(extra),
        out_specs=[SEM, SEM] + [HBM] * (2 * n) + [pl.BlockSpec(memory_space=pltpu.VMEM)],
        input_output_aliases={i: 2 + i for i in range(2 * n)},
        compiler_params=pltpu.CompilerParams(has_side_effects=EFFECT),
    )(*[_in_hbm(a) for a in gs + zones], *extra)


def _pair_wait(g, send, recv, gs, zones):
    n = len(gs)

    def body(*refs):
        s_ref, r_ref = refs[0], refs[1]
        outs = refs[2 + 2 * n:]
        src, dst = outs[:n], outs[n:]
        x, y, c, _ = _place()
        for a in range(n):
            h = src[a].shape[1] // 2
            _remote(src[a].at[:, pl.ds((1 - c) * h, h)], dst[a], s_ref.at[a], r_ref.at[a], (x, y, 1 - c)).wait()

    return _pallas(
        body, name=f"grad_pair_wait_{g}", out_shape=[pltpu.HBM(a.shape, a.dtype) for a in gs + zones],
        in_specs=[SEM, SEM] + [HBM] * (2 * n), out_specs=[HBM] * (2 * n),
        input_output_aliases={2 + i: i for i in range(2 * n)},
        compiler_params=pltpu.CompilerParams(has_side_effects=EFFECT),
    )(send, recv, *gs, *zones)


def _pair_add(gs, others, c_arr):
    n = len(gs)

    def body(c_ref, *refs):
        for g_ref, o_ref, out_ref in zip(refs[:n], refs[n:2 * n], refs[2 * n:]):
            out_ref[...] = (g_ref[...].astype(F32) + o_ref[...].astype(F32)).astype(BF16)

    half = lambda a: pl.BlockSpec((1, a.shape[1] // 2, a.shape[2]), lambda k, c_ref: (k, c_ref[0], 0))
    whole = lambda a: pl.BlockSpec((1,) + a.shape[1:], lambda k, c_ref: (k, 0, 0))
    grid_spec = pltpu.PrefetchScalarGridSpec(
        num_scalar_prefetch=1, grid=(4,), in_specs=[half(a) for a in gs] + [whole(o) for o in others],
        out_specs=[whole(o) for o in others])
    return _pallas(body, name="grad_pair_add", grid_spec=grid_spec, out_shape=[_sds(o.shape, BF16) for o in others],
                   compiler_params=_cp(("parallel",)))(c_arr, *gs, *others)


def _chip_start(g, ss):
    n = len(ss)
    zones = [lax.empty((3,) + s.shape[1:], s.dtype) for s in ss]

    def body(*refs):
        send_sems, recv_sems = refs[2 * n], refs[2 * n + 1]
        src, dst = refs[2 * n + 2:3 * n + 2], refs[3 * n + 2:4 * n + 2]
        token = refs[4 * n + 2]
        x, y, c, chips = _place()
        for a in range(n):
            for jj, (px, py) in enumerate(chips):
                k = 3 * a + jj
                _remote(src[a].at[2 * px + py], dst[a].at[jj], send_sems.at[k], recv_sems.at[k], (px, py, c)).start()
        token[...] = jnp.zeros_like(token)

    return _pallas(
        body, name=f"grad_chip_start_{g}",
        out_shape=[pltpu.SemaphoreType.DMA((3 * n,)), pltpu.SemaphoreType.DMA((3 * n,))]
        + [pltpu.HBM(a.shape, a.dtype) for a in ss + zones] + [_sds((8, 128), F32)],
        in_specs=[HBM] * (2 * n), out_specs=[SEM, SEM] + [HBM] * (2 * n) + [pl.BlockSpec(memory_space=pltpu.VMEM)],
        input_output_aliases={i: 2 + i for i in range(2 * n)},
        compiler_params=pltpu.CompilerParams(has_side_effects=EFFECT),
    )(*[_in_hbm(a) for a in ss + zones])


def _chip_wait(sends, recvs, counts, ss, zones, after):
    nb, n = len(sends), len(ss)

    def body(*refs):
        s_refs, r_refs = refs[:nb], refs[nb:2 * nb]
        outs = refs[2 * nb + 2 * n + 1:]
        src, dst = outs[:n], outs[n:]
        x, y, c, chips = _place()
        a = 0
        for b in range(nb):
            for i in range(counts[b]):
                for jj, (px, py) in enumerate(chips):
                    k = 3 * i + jj
                    _remote(src[a].at[2 * px + py], dst[a].at[jj], s_refs[b].at[k], r_refs[b].at[k], (px, py, c)).wait()
                a += 1

    return _pallas(
        body, name="grad_chip_wait", out_shape=[pltpu.HBM(a.shape, a.dtype) for a in ss + zones],
        in_specs=[SEM] * (2 * nb) + [HBM] * (2 * n) + [ANY], out_specs=[HBM] * (2 * n),
        input_output_aliases={2 * nb + i: i for i in range(2 * n)},
        compiler_params=pltpu.CompilerParams(has_side_effects=EFFECT),
    )(*sends, *recvs, *ss, *zones, after)


def _chip_sum(s, r, where, dest, l, L):
    _, h, C = s.shape
    tr = h // 2

    def body(k_ref, s_ref, r_ref, *rest):
        out_ref = rest[-1]
        acc = s_ref[0].astype(F32)
        for jj in range(3):
            acc = acc + r_ref[jj].astype(F32)
        out_ref[...] = acc

    in_specs = [pl.BlockSpec((1, tr, C), lambda i, k_ref: (k_ref[0], i, 0)), pl.BlockSpec((3, tr, C), lambda i, k_ref: (0, i, 0))]
    args = [where, s, r]
    alias = {}
    if dest is not None:
        in_specs.append(ANY)
        args.append(dest)
        alias = {3: 0}
    grid_spec = pltpu.PrefetchScalarGridSpec(
        num_scalar_prefetch=1, grid=(2,), in_specs=in_specs,
        out_specs=pl.BlockSpec((None, tr, C), lambda i, k_ref: (l, 2 * k_ref[1] + i, 0)))
    return _pallas(body, name="grad_chip_sum", grid_spec=grid_spec, out_shape=_sds((L, 2 * h, C), F32),
                   input_output_aliases=alias, compiler_params=_cp(("arbitrary",)))(*args)


def _share_start(tag, bufs, layout):
    n, n_buf = len(layout), len(bufs)

    def body(*refs):
        send_sems, recv_sems = refs[n_buf], refs[n_buf + 1]
        outs = refs[n_buf + 2:]
        x, y, c, _ = _place()
        for a, (o, l) in enumerate(layout):
            h = outs[o].shape[1] // 2
            blk = outs[o].at[l, pl.ds(c * h, h)]
            _remote(blk, blk, send_sems.at[a], recv_sems.at[a], (x, y, 1 - c)).start()

    return _pallas(
        body, name=f"grad_share_start_{tag}",
        out_shape=[pltpu.SemaphoreType.DMA((n,)), pltpu.SemaphoreType.DMA((n,))] + [pltpu.HBM(b.shape, b.dtype) for b in bufs],
        in_specs=[HBM] * n_buf, out_specs=[SEM, SEM] + [HBM] * n_buf, input_output_aliases={o: 2 + o for o in range(n_buf)},
        compiler_params=pltpu.CompilerParams(has_side_effects=EFFECT),
    )(*[_in_hbm(b) for b in bufs])


def _share_wait(tag, send, recv, bufs, layout, after):
    n_buf = len(bufs)

    def body(*refs):
        s_ref, r_ref = refs[0], refs[1]
        outs = refs[n_buf + 3:]
        x, y, c, _ = _place()
        for a, (o, l) in enumerate(layout):
            h = outs[o].shape[1] // 2
            mine, theirs = outs[o].at[l, pl.ds(c * h, h)], outs[o].at[l, pl.ds((1 - c) * h, h)]
            _remote(mine, mine, s_ref.at[a], r_ref.at[a], (x, y, 1 - c)).wait_send()
            _remote(theirs, theirs, s_ref.at[a], r_ref.at[a], (x, y, 1 - c)).wait_recv()

    return _pallas(
        body, name=f"grad_share_wait_{tag}", out_shape=[pltpu.HBM(b.shape, b.dtype) for b in bufs],
        in_specs=[SEM, SEM] + [HBM] * n_buf + [ANY], out_specs=[HBM] * n_buf,
        input_output_aliases={2 + o: o for o in range(n_buf)},
        compiler_params=pltpu.CompilerParams(has_side_effects=EFFECT),
    )(send, recv, *bufs, after)


def _small_all_reduce(packed):
    P, L = packed.shape

    def body(in_ref, out_ref, slots, send_sems, recv_sems):
        x, y, c, _ = _place()
        me = 4 * x + 2 * y + c
        slots[me] = in_ref[...]
        cps = []
        for r in range(1, 8):
            px = 1 - x if r & 4 else x
            py = 1 - y if r & 2 else y
            pc = 1 - c if r & 1 else c
            cps.append(_remote(in_ref, slots.at[me], send_sems.at[r - 1], recv_sems.at[r - 1], (px, py, pc)))
        for cp in cps:
            cp.start()
        for r in range(1, 8):
            px = 1 - x if r & 4 else x
            py = 1 - y if r & 2 else y
            pc = 1 - c if r & 1 else c
            blk = slots.at[4 * px + 2 * py + pc]
            _remote(blk, blk, send_sems.at[r - 1], recv_sems.at[r - 1], (px, py, pc)).wait_recv()
        for cp in cps:
            cp.wait_send()
        acc = slots[0]
        for k in range(1, 8):
            acc = acc + slots[k]
        out_ref[...] = acc

    vm = pl.BlockSpec(memory_space=pltpu.VMEM)
    return _pallas(body, name="small_all_reduce", in_specs=[vm], out_specs=vm, out_shape=_sds((P, L), F32),
                   scratch_shapes=[pltpu.VMEM((8, P, L), F32), pltpu.SemaphoreType.DMA((7,)), pltpu.SemaphoreType.DMA((7,))])(packed)


def _adamw_math(w, g, m, v):
    m = ADAM_B1 * m + (1.0 - ADAM_B1) * g
    v = ADAM_B2 * v + (1.0 - ADAM_B2) * (g * g)
    m_hat = m / (1.0 - ADAM_B1 ** ADAM_STEP)
    v_hat = v / (1.0 - ADAM_B2 ** ADAM_STEP)
    delta = -ADAM_LR * (m_hat / (jnp.sqrt(v_hat) + ADAM_EPS) + ADAM_WD * w)
    return delta, m, v


def _adamw(w, g, m, v):
    shape = w.shape
    C = shape[-1]
    rows = math.prod(shape[:-1])
    tr = next(t for t in (512, 352, 256, 128, 64, 32, 16, 8, rows) if rows % t == 0)
    w2, g2, m2, v2 = (a.reshape(rows, C) for a in (w, g, m, v))

    def body(w_ref, g_ref, m_ref, v_ref, go_ref, d_ref, nm_ref, nv_ref):
        gv = g_ref[...]
        d, nm, nv = _adamw_math(w_ref[...], gv, m_ref[...], v_ref[...])
        go_ref[...] = gv
        d_ref[...] = d
        nm_ref[...] = nm
        nv_ref[...] = nv

    blk = pl.BlockSpec((tr, C), lambda i: (i, 0))
    outs = _pallas(body, name="adamw", grid=(rows // tr,), in_specs=[blk] * 4, out_specs=[blk] * 4,
                   out_shape=[_sds((rows, C), F32)] * 4, compiler_params=_cp(("parallel",)))(w2, g2, m2, v2)
    return tuple(o.reshape(shape) for o in outs)


WEIGHTS = ["ffn1_norm", "ffn1_w_gate", "ffn1_w_up", "ffn1_w_down", "mix_norm", "ffn2_norm", "ffn2_w_gate", "ffn2_w_up",
           "ffn2_w_down", "ev_w_in", "ev_b_f", "ev_conv_w", "ev_conv_b", "ev_conv_norm", "ev_q_norm", "ev_k_norm",
           "ev_w_out", "od_w_in", "od_conv_w", "od_w_out"]
BIG = ([("ffn1_w_gate", 0), ("ffn1_w_up", 0), ("ffn1_w_down", 0), ("ev_w_in", 0), ("ev_w_out", 0),
        ("ffn2_w_gate", 0), ("ffn2_w_up", 0), ("ffn2_w_down", 0)]
       + [("ffn1_w_gate", 1), ("ffn1_w_up", 1), ("ffn1_w_down", 1), ("od_w_in", 0), ("od_w_out", 0),
          ("ffn2_w_gate", 1), ("ffn2_w_up", 1), ("ffn2_w_down", 1)])
TRANSPOSED = ("ffn1_w_gate", "ffn1_w_up", "ffn2_w_gate", "ffn2_w_up")
SHARED_LAST = ("ffn1_w_gate", "ffn1_w_up", "ffn1_w_down", "ev_w_in", "ev_w_out")
BLOCKS = [("ffn1", 0), ("ev", 0), ("ffn2", 0), ("ffn1", 1), ("od", 0), ("ffn2", 1)]
BLOCK_OF = {(name, l): (name.split("_w_")[0], l) for name, l in BIG}
BIG_NAMES = ["ffn1_w_gate", "ffn1_w_up", "ffn1_w_down", "ffn2_w_gate", "ffn2_w_up", "ffn2_w_down",
             "ev_w_in", "ev_w_out", "od_w_in", "od_w_out"]
SMALL = [("ffn1_norm", 16), ("mix_norm", 16), ("ffn2_norm", 16), ("ev_b_f", 8), ("ev_conv_w", 128), ("ev_conv_b", 8),
         ("ev_conv_norm", 8), ("ev_q_norm", 8), ("ev_k_norm", 8), ("od_conv_w", 24)]


def _to_lanes(a, rows):
    flat = a.reshape(-1)
    return jnp.pad(flat, (0, rows * 128 - flat.shape[0])).reshape(rows, 128)


def kernel(x, ffn1_norm, ffn1_w_gate, ffn1_w_up, ffn1_w_down, mix_norm, ffn2_norm, ffn2_w_gate, ffn2_w_up, ffn2_w_down, ev_w_in, ev_b_f, ev_conv_w, ev_conv_b, ev_conv_norm, ev_q_norm, ev_k_norm, ev_w_out, od_w_in, od_conv_w, od_w_out, loss_target, m_ffn1_norm, m_ffn1_w_gate, m_ffn1_w_up, m_ffn1_w_down, m_mix_norm, m_ffn2_norm, m_ffn2_w_gate, m_ffn2_w_up, m_ffn2_w_down, m_ev_w_in, m_ev_b_f, m_ev_conv_w, m_ev_conv_b, m_ev_conv_norm, m_ev_q_norm, m_ev_k_norm, m_ev_w_out, m_od_w_in, m_od_conv_w, m_od_w_out, v_ffn1_norm, v_ffn1_w_gate, v_ffn1_w_up, v_ffn1_w_down, v_mix_norm, v_ffn2_norm, v_ffn2_w_gate, v_ffn2_w_up, v_ffn2_w_down, v_ev_w_in, v_ev_b_f, v_ev_conv_w, v_ev_conv_b, v_ev_conv_norm, v_ev_q_norm, v_ev_k_norm, v_ev_w_out, v_od_w_in, v_od_conv_w, v_od_w_out):
    P = dict(ffn1_norm=ffn1_norm, ffn1_w_gate=ffn1_w_gate, ffn1_w_up=ffn1_w_up, ffn1_w_down=ffn1_w_down, mix_norm=mix_norm,
             ffn2_norm=ffn2_norm, ffn2_w_gate=ffn2_w_gate, ffn2_w_up=ffn2_w_up, ffn2_w_down=ffn2_w_down, ev_w_in=ev_w_in,
             ev_b_f=ev_b_f, ev_conv_w=ev_conv_w, ev_conv_b=ev_conv_b, ev_conv_norm=ev_conv_norm, ev_q_norm=ev_q_norm,
             ev_k_norm=ev_k_norm, ev_w_out=ev_w_out, od_w_in=od_w_in, od_conv_w=od_conv_w, od_w_out=od_w_out)
    M = dict(zip(WEIGHTS, [m_ffn1_norm, m_ffn1_w_gate, m_ffn1_w_up, m_ffn1_w_down, m_mix_norm, m_ffn2_norm, m_ffn2_w_gate,
                           m_ffn2_w_up, m_ffn2_w_down, m_ev_w_in, m_ev_b_f, m_ev_conv_w, m_ev_conv_b, m_ev_conv_norm,
                           m_ev_q_norm, m_ev_k_norm, m_ev_w_out, m_od_w_in, m_od_conv_w, m_od_w_out]))
    V = dict(zip(WEIGHTS, [v_ffn1_norm, v_ffn1_w_gate, v_ffn1_w_up, v_ffn1_w_down, v_mix_norm, v_ffn2_norm, v_ffn2_w_gate,
                           v_ffn2_w_up, v_ffn2_w_down, v_ev_w_in, v_ev_b_f, v_ev_conv_w, v_ev_conv_b, v_ev_conv_norm,
                           v_ev_q_norm, v_ev_k_norm, v_ev_w_out, v_od_w_in, v_od_conv_w, v_od_w_out]))
    for name in TRANSPOSED:
        P[name], M[name], V[name] = (jnp.swapaxes(a, 1, 2) for a in (P[name], M[name], V[name]))
    S, D = x.shape[1], x.shape[2]
    chip = 2 * lax.axis_index("x") + lax.axis_index("y")
    core = lax.axis_index("c")

    def own_slot(shard):
        return lax.dynamic_update_slice(lax.empty((4,) + shard.shape, shard.dtype), shard[None], (chip, 0, 0))

    taps = jnp.concatenate([_to_lanes(_pad_rows(ev_conv_w[0], 32), 32), _to_lanes(_pad_rows(od_conv_w[0], 8), 16)], axis=0)
    first = [i for i, k in enumerate(BIG) if BLOCK_OF[k] == BLOCKS[0]]
    rest = [i for i in range(len(BIG)) if i not in first]
    send0, recv0, *bufs0 = _ag_start("first", [own_slot(P[BIG[i][0]][BIG[i][1]].astype(BF16)) for i in first], False)
    zero = bufs0.pop()[0, 0]
    send1, recv1, *bufs1 = _ag_start("rest", [own_slot((P[BIG[i][0]][BIG[i][1]] + zero).astype(BF16)) for i in rest]
                                     + [own_slot(taps)], True)
    bufs1.pop()
    cols = lambda a: a.transpose(1, 0, 2).reshape(a.shape[1], 4 * a.shape[2])
    W = {k: P[k] for k in ("ffn1_norm", "mix_norm", "ffn2_norm", "ev_b_f", "ev_q_norm", "ev_k_norm")}
    W["ev_conv_b"], W["ev_conv_norm"] = ev_conv_b, ev_conv_norm
    for tag in ("ffn1", "ffn2"):
        for kind in ("_w_gate", "_w_up", "_w_down"):
            W[tag + kind] = [None, None]
    passing = {}

    def pass_on(g, after):
        idx = [i for i, k in enumerate(BIG) if BLOCK_OF[k] == BLOCKS[g]]
        keys = [BIG[i] for i in idx] + (["taps"] if BLOCKS[g] == ("ev", 0) else [])
        send, recv, bufs, members = (send0, recv0, bufs0, first) if g == 0 else (send1, recv1, bufs1, rest)
        local = [members.index(i) for i in idx]
        passing[g] = (keys, _ag_mid(g, send, recv, [bufs[i] for i in local], local,
                                    bufs1[-1] if BLOCKS[g] == ("ev", 0) else None, len(rest), after))

    def need(block, after):
        g = BLOCKS.index(block)
        if g not in passing:
            pass_on(g, after)
        keys, (d_send, d_recv, *thru) = passing.pop(g)
        got = dict(zip(keys, _ag_wait(g, d_send, d_recv, thru, len(keys) - ("taps" in keys), after)))
        if 1 <= g < len(BLOCKS) - 1:
            pass_on(g + 1, after)
        for key, a in got.items():
            if key == "taps":
                continue
            name, l = key
            if name.startswith("ffn"):
                W[name][l] = a
            elif name.endswith("_w_in"):
                W[name] = cols(a)
            elif name.endswith("_w_out"):
                W[name] = a.reshape(4 * a.shape[1], D)
        if block == ("ev", 0):
            taps_all = got["taps"]
            W["ev_conv_w"] = cols(taps_all[:, :32].reshape(4, 32, 128))[:CONV_A_WIDTH]
            W["od_conv_w"] = cols(taps_all[:, 32:48].reshape(4, 8, 256))[:CONV_C_WIDTH]

    rows = lambda a: a.reshape(4, a.shape[0] // 4, a.shape[1])
    colsh = lambda a: a.reshape(a.shape[0], 4, a.shape[1] // 4).transpose(1, 0, 2)
    c_arr = core.reshape(1).astype(jnp.int32)
    where = jnp.stack([chip, core]).astype(jnp.int32)
    in_flight = []

    def done(block, block_grads):
        g = BLOCKS.index(block)
        keys = list(block_grads)
        gs = []
        for name, l in keys:
            a = block_grads[(name, l)]
            gs.append(colsh(a) if name == "ev_w_in" else rows(a) if name.endswith("_w_out") else a)
        for item in list(pairs):
            to_chips(item)
        send, recv, *rest = _pair_start(g, gs, chained.get("token"))
        n = len(keys)
        pairs.append((g, keys, send, recv, rest[:n], rest[n:2 * n]))
        if g == 0:
            to_chips(pairs[0])
        chained["token"] = rest[-1] if g else chained["token"]
        return chained["token"][0:1, 0:1]

    pairs, chained = [], {}

    def to_chips(item):
        pairs.remove(item)
        g, keys, send, recv, gs, zones = item
        n = len(keys)
        done_ = _pair_wait(g, send, recv, gs, zones)
        sums = list(_pair_add(list(done_[:n]), list(done_[n:]), c_arr))
        send2, recv2, *rest = _chip_start(g, sums)
        in_flight.append((keys, send2, recv2, rest[:n], rest[n:2 * n]))
        chained["token"] = rest[-1]

    loss, grad_x, grads = _local_step(x[0], loss_target[0], W, need, done)

    order = [k for keys, *_ in in_flight for k in keys]
    landed = _chip_wait([f[1] for f in in_flight], [f[2] for f in in_flight], [len(f[0]) for f in in_flight],
                        [a for f in in_flight for a in f[3]], [a for f in in_flight for a in f[4]], grad_x)
    sums, recvd = landed[:len(order)], landed[len(order):]
    stacked, shares = {}, []
    for tag, names in (("a", [n for n in BIG_NAMES if n not in SHARED_LAST]), ("b", list(SHARED_LAST))):
        for (name, l), s, r in zip(order, sums, recvd):
            if name in names:
                stacked[name] = _chip_sum(s, r, where, stacked.get(name), l, P[name].shape[0])
        layout = [(names.index(name), l) for name, l in order if name in names]
        send, recv, *thru = _share_start(tag, [stacked[name] for name in names], layout)
        shares.append((tag, names, send, recv, thru, layout))

    def small_grad(name):
        if name.endswith("_norm") and name[:3] in ("ffn", "mix"):
            return jnp.concatenate([grads[(name, 0)], grads[(name, 1)]], axis=0)
        return grads[(name, 0)]

    packed = jnp.concatenate([_to_lanes(small_grad(name), r) for name, r in SMALL], axis=0)
    total = _small_all_reduce(packed)
    small_grads, at = {}, 0
    for name, r in SMALL:
        part = total[at:at + r].reshape(-1)
        at += r
        if name == "ev_conv_w":
            full_g = part[:CONV_A_WIDTH * D_CONV].reshape(CONV_A_WIDTH, D_CONV)
            small_grads[name] = lax.dynamic_slice_in_dim(full_g, chip * (D_CONV // 4), D_CONV // 4, axis=1)[None]
        elif name == "od_conv_w":
            full_g = part[:CONV_C_WIDTH * D].reshape(CONV_C_WIDTH, D)
            small_grads[name] = lax.dynamic_slice_in_dim(full_g, chip * (D // 4), D // 4, axis=1)[None]
        else:
            small_grads[name] = part[:math.prod(P[name].shape)].reshape(P[name].shape)

    results = {}

    def update(name, g):
        outs = _adamw(P[name], g, M[name], V[name])
        results[name] = tuple(jnp.swapaxes(a, 1, 2) for a in outs) if name in TRANSPOSED else outs

    for name, _ in SMALL:
        update(name, small_grads[name])
    after = shares[-1][4][0]
    for tag, names, send, recv, thru, layout in shares:
        for name, g in zip(names, _share_wait(tag, send, recv, thru, layout, after)):
            update(name, g)
        after = results[names[-1]][1]
    loss_all = lax.psum(loss[0, 0], ("x", "y", "c"))
    return (loss_all, grad_x[None], *[results[name][k] for k in range(4) for name in WEIGHTS])
```

```python
import functools
import math

import jax
import jax.numpy as jnp
from jax import lax
from jax.experimental import pallas as pl
from jax.experimental.pallas import tpu as pltpu

F32, BF16 = jnp.float32, jnp.bfloat16
EPS = 1e-6
FFN_RES = 0.5
N_HEADS, HEAD_DIM = 8, 64
D_CONV = 512
D_ATTN = N_HEADS * HEAD_DIM
CONV_A_WIDTH, CONV_C_WIDTH = 31, 3
ADAM_LR, ADAM_B1, ADAM_B2, ADAM_EPS, ADAM_WD, ADAM_STEP = 0.001, 0.9, 0.999, 1e-08, 0.01, 10
MESH = pl.DeviceIdType.MESH
ANY = pl.BlockSpec(memory_space=pl.ANY)

TOK_TILE = 512
FFN_TILE = 1024
DW_TILE = 1024
ATT_TILE = 1024
QKN_TILE = 2048
HALO_A, HALO_C = 32, 16
SUBLANES = 8
CONV_ROWS = 64
SCAN_BLK = 256
MIB = 2 ** 20


def _pallas(body, **kw):
    return pl.pallas_call(body, **kw)


def _cp(sem=None, vmem_mib=48):
    return pltpu.CompilerParams(dimension_semantics=sem, vmem_limit_bytes=vmem_mib * MIB)


def _dot(a, b):
    return jnp.dot(a, b, preferred_element_type=F32)


def _dot_nt(a, b):
    return lax.dot_general(a, b, (((1,), (1,)), ((), ())), preferred_element_type=F32)


def _dot_tn(a, b):
    return lax.dot_general(a, b, (((0,), (0,)), ((), ())), preferred_element_type=F32)


def _sds(shape, dtype):
    return jax.ShapeDtypeStruct(shape, dtype)


def _rms(x):
    return lax.rsqrt(jnp.mean(x * x, axis=-1, keepdims=True) + EPS)


def _rms_bwd(dy, x, g):
    r = _rms(x)
    xh = x * r
    dxh = dy * g
    dx = r * (dxh - xh * jnp.mean(dxh * xh, axis=-1, keepdims=True))
    return dx, xh


def _silu_grad(z):
    s = jax.nn.sigmoid(z)
    return s * (1.0 + z * (1.0 - s))


def _ffn_fwd(x, g, wg, wu, wd):
    S, D = x.shape
    nc, Fs, _ = wd.shape
    tm = min(FFN_TILE, S)

    def body(x_ref, g_ref, wg_ref, wu_ref, wd_ref, out_ref, xn_ref, G_ref, U_ref, acc_ref):
        j = pl.program_id(1)

        @pl.when(j == 0)
        def _():
            xv = x_ref[...]
            xn_ref[...] = (xv * _rms(xv) * g_ref[...]).astype(BF16)
            acc_ref[...] = jnp.zeros_like(acc_ref)

        xn = xn_ref[...]
        G = _dot_nt(xn, wg_ref[0])
        U = _dot_nt(xn, wu_ref[0])
        G_ref[0] = G.astype(BF16)
        U_ref[0] = U.astype(BF16)
        H = (G * jax.nn.sigmoid(G) * U).astype(BF16)
        acc_ref[...] += _dot(H, wd_ref[0])

        @pl.when(j == nc - 1)
        def _():
            out_ref[...] = x_ref[...] + FFN_RES * acc_ref[...]

    row = pl.BlockSpec((tm, D), lambda i, j: (i, 0))
    return _pallas(
        body, name="ffn_fwd", grid=(S // tm, nc),
        in_specs=[row, pl.BlockSpec((1, D), lambda i, j: (0, 0)),
                  pl.BlockSpec((1, Fs, D), lambda i, j: (j, 0, 0)), pl.BlockSpec((1, Fs, D), lambda i, j: (j, 0, 0)),
                  pl.BlockSpec((1, Fs, D), lambda i, j: (j, 0, 0))],
        out_specs=[row, row, pl.BlockSpec((1, tm, Fs), lambda i, j: (j, i, 0)),
                   pl.BlockSpec((1, tm, Fs), lambda i, j: (j, i, 0))],
        out_shape=[_sds((S, D), F32), _sds((S, D), BF16), _sds((nc, S, Fs), BF16), _sds((nc, S, Fs), BF16)],
        scratch_shapes=[pltpu.VMEM((tm, D), F32)],
        compiler_params=_cp(("parallel", "arbitrary"), 56),
    )(x, g, wg, wu, wd)


def _ffn_bwd_w(dout, xn, G, U, wd):
    S, D = dout.shape
    nc, _, Fs = G.shape
    tm = min(DW_TILE, S)
    nt = S // tm
    sub = min(TOK_TILE, tm)

    def body(do_ref, xn_ref, G_ref, U_ref, wd_ref, dwg_ref, dwu_ref, dwd_ref, dG_ref, dU_ref, ag, au, ad, do_s, H_s):
        i = pl.program_id(1)

        @pl.when(i == 0)
        def _():
            ag[...] = jnp.zeros_like(ag)
            au[...] = jnp.zeros_like(au)
            ad[...] = jnp.zeros_like(ad)

        for r in range(0, tm, sub):
            rows = pl.ds(r, sub)
            do = (FFN_RES * do_ref[rows, :]).astype(BF16)
            do_s[rows, :] = do
            Gv = G_ref[0, rows, :].astype(F32)
            Uv = U_ref[0, rows, :].astype(F32)
            dH = _dot_nt(do, wd_ref[0])
            sg = jax.nn.sigmoid(Gv)
            act = Gv * sg
            H_s[rows, :] = (act * Uv).astype(BF16)
            dU_ref[0, rows, :] = (dH * act).astype(BF16)
            dG_ref[0, rows, :] = (dH * Uv * (sg * (1.0 + Gv * (1.0 - sg)))).astype(BF16)
        xnv = xn_ref[...]
        ag[...] += _dot_tn(dG_ref[0], xnv)
        au[...] += _dot_tn(dU_ref[0], xnv)
        ad[...] += _dot_tn(H_s[...], do_s[...])

        @pl.when(i == nt - 1)
        def _():
            dwg_ref[0] = ag[...].astype(BF16)
            dwu_ref[0] = au[...].astype(BF16)
            dwd_ref[0] = ad[...].astype(BF16)

    row = pl.BlockSpec((tm, D), lambda j, i: (i, 0))
    hid = pl.BlockSpec((1, tm, Fs), lambda j, i: (j, i, 0))
    wrow = pl.BlockSpec((1, Fs, D), lambda j, i: (j, 0, 0))
    return _pallas(
        body, name="ffn_bwd_w", grid=(nc, nt),
        in_specs=[row, row, hid, hid, wrow],
        out_specs=[wrow, wrow, wrow, hid, hid],
        out_shape=[_sds((nc, Fs, D), BF16)] * 3 + [_sds((nc, S, Fs), BF16)] * 2,
        scratch_shapes=[pltpu.VMEM((Fs, D), F32)] * 3 + [pltpu.VMEM((tm, D), BF16), pltpu.VMEM((tm, Fs), BF16)],
        compiler_params=_cp(("parallel", "arbitrary"), 56),
    )(dout, xn, G, U, wd)


def _norm_in_bwd(dzs, ws, x, g, dres, w_rows=False):
    S, D = x.shape
    nc = dzs[0].shape[0]
    n = len(dzs)
    tm = TOK_TILE

    def body(*refs):
        dz_refs, w_refs = refs[:n], refs[n:2 * n]
        x_ref, g_ref, dres_ref, dx_ref, dg_ref, acc_ref = refs[2 * n:]
        i, j = pl.program_id(0), pl.program_id(1)

        @pl.when(j == 0)
        def _():
            acc_ref[...] = jnp.zeros_like(acc_ref)

        @pl.when((i == 0) & (j == 0))
        def _():
            dg_ref[...] = jnp.zeros_like(dg_ref)

        for dz_ref, w_ref in zip(dz_refs, w_refs):
            acc_ref[...] += _dot(dz_ref[0], w_ref[0]) if w_rows else _dot_nt(dz_ref[0], w_ref[0])

        @pl.when(j == nc - 1)
        def _():
            dxn = acc_ref[...]
            dx, xh = _rms_bwd(dxn, x_ref[...], g_ref[...])
            dx_ref[...] = dx + dres_ref[...]
            dg_ref[...] += jnp.sum(dxn * xh, axis=0, keepdims=True)

    row = pl.BlockSpec((tm, D), lambda i, j: (i, 0))
    one = pl.BlockSpec((1, D), lambda i, j: (0, 0))
    in_specs = [pl.BlockSpec((1, tm, dz.shape[2]), lambda i, j: (j, i, 0)) for dz in dzs]
    in_specs += [pl.BlockSpec((1,) + w.shape[1:], lambda i, j: (j, 0, 0)) for w in ws]
    return _pallas(
        body, name="norm_in_bwd", grid=(S // tm, nc),
        in_specs=in_specs + [row, one, row], out_specs=[row, one],
        out_shape=[_sds((S, D), F32), _sds((1, D), F32)],
        scratch_shapes=[pltpu.VMEM((tm, D), F32)],
        compiler_params=_cp(("arbitrary", "arbitrary")),
    )(*dzs, *ws, x, g, dres)


def _norm_proj(x, g, w, w2=None):
    S, D = x.shape
    N = w.shape[1]
    tm = TOK_TILE

    def body(*refs):
        if w2 is None:
            x_ref, g_ref, w_ref, h_ref, z_ref = refs
        else:
            x_ref, g_ref, w_ref, w2_ref, h_ref, z_ref, z2_ref = refs
        xv = x_ref[...]
        h = (xv * _rms(xv) * g_ref[...]).astype(BF16)
        h_ref[...] = h
        z_ref[...] = _dot(h, w_ref[...]).astype(BF16)
        if w2 is not None:
            z2_ref[...] = _dot(h, w2_ref[...])

    row = pl.BlockSpec((tm, D), lambda i: (i, 0))
    in_specs = [row, pl.BlockSpec((1, D), lambda i: (0, 0)), pl.BlockSpec((D, N), lambda i: (0, 0))]
    out_specs = [row, pl.BlockSpec((tm, N), lambda i: (i, 0))]
    out_shape = [_sds((S, D), BF16), _sds((S, N), BF16)]
    args = [x, g, w]
    if w2 is not None:
        N2 = w2.shape[1]
        in_specs.append(pl.BlockSpec((D, N2), lambda i: (0, 0)))
        out_specs.append(pl.BlockSpec((tm, N2), lambda i: (i, 0)))
        out_shape.append(_sds((S, N2), F32))
        args.append(w2)
    return _pallas(body, name="norm_proj", grid=(S // tm,), in_specs=in_specs, out_specs=out_specs,
                   out_shape=out_shape, compiler_params=_cp(("parallel",)))(*args)


def _proj_res(acts, ws, res):
    S, D = res.shape
    n = len(acts)
    tm = TOK_TILE

    def body(*refs):
        a_refs, w_refs = refs[:n], refs[n:2 * n]
        res_ref, out_ref = refs[2 * n:]
        acc = res_ref[...]
        for a_ref, w_ref in zip(a_refs, w_refs):
            acc = acc + _dot(a_ref[...], w_ref[...])
        out_ref[...] = acc

    row = pl.BlockSpec((tm, D), lambda i: (i, 0))
    in_specs = [pl.BlockSpec((tm, a.shape[1]), lambda i: (i, 0)) for a in acts]
    in_specs += [pl.BlockSpec(w.shape, lambda i: (0, 0)) for w in ws]
    return _pallas(body, name="proj_res", grid=(S // tm,), in_specs=in_specs + [row], out_specs=row,
                   out_shape=_sds((S, D), F32), compiler_params=_cp(("parallel",)))(*acts, *ws, res)


def _matmul_nt(a, w, after=None):
    S, K = a.shape
    M = w.shape[0]
    tm = TOK_TILE

    def body(a_ref, w_ref, *rest):
        rest[-1][...] = _dot_nt(a_ref[...].astype(BF16), w_ref[...])

    extra = [] if after is None else [after]
    return _pallas(body, name="matmul_nt", grid=(S // tm,),
                   in_specs=[pl.BlockSpec((tm, K), lambda i: (i, 0)), pl.BlockSpec((M, K), lambda i: (0, 0))] + [ANY] * len(extra),
                   out_specs=pl.BlockSpec((tm, M), lambda i: (i, 0)), out_shape=_sds((S, M), F32),
                   compiler_params=_cp(("parallel",)))(a, w, *extra)


def _matmul_tn(a, b, tn):
    S, M = a.shape
    N = b.shape[1]
    tm = min(DW_TILE, S)
    nt = S // tm

    def body(a_ref, b_ref, o_ref, acc_ref):
        i = pl.program_id(1)

        @pl.when(i == 0)
        def _():
            acc_ref[...] = jnp.zeros_like(acc_ref)

        acc_ref[...] += _dot_tn(a_ref[...].astype(BF16), b_ref[...].astype(BF16))

        @pl.when(i == nt - 1)
        def _():
            o_ref[0] = acc_ref[...].astype(BF16)

    return _pallas(body, name="matmul_tn", grid=(N // tn, nt),
                   in_specs=[pl.BlockSpec((tm, M), lambda j, i: (i, 0)), pl.BlockSpec((tm, tn), lambda j, i: (i, j))],
                   out_specs=pl.BlockSpec((1, M, tn), lambda j, i: (j, 0, 0)), out_shape=_sds((N // tn, M, tn), BF16),
                   scratch_shapes=[pltpu.VMEM((M, tn), F32)],
                   compiler_params=_cp(("parallel", "arbitrary")))(a, b)


def _fill_shifts(win, rows):
    for b in range(1, SUBLANES):
        win[b, pl.ds(0, rows - SUBLANES), :] = win[0, pl.ds(b, rows - SUBLANES), :]


def _tap(win, offset, n, base=0):
    start = base + (offset - offset % SUBLANES)
    if not isinstance(start, int):
        start = pl.multiple_of(start, SUBLANES)
    return win[offset % SUBLANES, pl.ds(start, n), :]


def _conv_a_fwd(z, cw, cb, cn):
    S = z.shape[0]
    C = D_CONV
    tm = TOK_TILE
    hb = tm // HALO_A

    def body(u_ref, gt_ref, up_ref, gp_ref, cw_ref, cb_ref, cn_ref, a_ref, a1_ref, win):
        i = pl.program_id(0)
        prev = up_ref[...].astype(F32) * jax.nn.sigmoid(gp_ref[...].astype(F32))
        win[0, pl.ds(0, HALO_A), :] = jnp.where(i == 0, 0.0, prev)
        win[0, pl.ds(HALO_A, tm), :] = u_ref[...].astype(F32) * jax.nn.sigmoid(gt_ref[...].astype(F32))
        _fill_shifts(win, tm + HALO_A)

        acc = jnp.zeros((tm, C), F32)
        for k in range(CONV_A_WIDTH):
            acc = acc + cw_ref[k:k + 1, :] * _tap(win, HALO_A - (CONV_A_WIDTH - 1) + k, tm)
        a1 = acc + cb_ref[...]
        a1_ref[...] = a1
        a2 = a1 * _rms(a1) * cn_ref[...]
        a_ref[...] = (a2 * jax.nn.sigmoid(a2)).astype(BF16)

    cur = lambda c: pl.BlockSpec((tm, C), lambda i, c=c: (i, c))
    prv = lambda c: pl.BlockSpec((HALO_A, C), lambda i, c=c: (jnp.maximum(i * hb - 1, 0), c))
    vec = pl.BlockSpec((1, C), lambda i: (0, 0))
    return _pallas(body, name="conv_a_fwd", grid=(S // tm,),
                   in_specs=[cur(0), cur(1), prv(0), prv(1), pl.BlockSpec((32, C), lambda i: (0, 0)), vec, vec],
                   out_specs=[pl.BlockSpec((tm, C), lambda i: (i, 0)), pl.BlockSpec((tm, C), lambda i: (i, 0))],
                   out_shape=[_sds((S, C), BF16), _sds((S, C), F32)],
                   scratch_shapes=[pltpu.VMEM((SUBLANES, tm + HALO_A, C), F32)],
                   compiler_params=_cp(("parallel",)))(z, z, z, z, cw, cb, cn)


def _conv_a_bwd(da, a1, z, cw, cn):
    S = z.shape[0]
    C = D_CONV
    tm = TOK_TILE
    hb = tm // HALO_A
    nt = S // tm
    W = CONV_A_WIDTH

    def body(da_ref, a1_ref, dan_ref, a1n_ref, u_ref, gt_ref, up_ref, gp_ref, cw_ref, cn_ref,
             duz_ref, dcw_ref, dcb_ref, dcn_ref, win, dwin):
        i = pl.program_id(0)

        @pl.when(i == 0)
        def _():
            dcw_ref[...] = jnp.zeros_like(dcw_ref)
            dcb_ref[...] = jnp.zeros_like(dcb_ref)
            dcn_ref[...] = jnp.zeros_like(dcn_ref)

        cnv = cn_ref[...]

        def da1_of(dav, a1v):
            a2 = a1v * _rms(a1v) * cnv
            da2 = dav * _silu_grad(a2)
            dx, xh = _rms_bwd(da2, a1v, cnv)
            return dx, da2 * xh

        da1, dcn_t = da1_of(da_ref[...], a1_ref[...])
        da1n, _ = da1_of(dan_ref[...], a1n_ref[...])
        dwin[0, pl.ds(0, tm), :] = da1
        dwin[0, pl.ds(tm, HALO_A), :] = jnp.where(i == nt - 1, 0.0, da1n)
        _fill_shifts(dwin, tm + HALO_A)
        dcb_ref[...] += jnp.sum(da1, axis=0, keepdims=True)
        dcn_ref[...] += jnp.sum(dcn_t, axis=0, keepdims=True)

        prev = up_ref[...].astype(F32) * jax.nn.sigmoid(gp_ref[...].astype(F32))
        win[0, pl.ds(0, HALO_A), :] = jnp.where(i == 0, 0.0, prev)
        win[0, pl.ds(HALO_A, tm), :] = u_ref[...].astype(F32) * jax.nn.sigmoid(gt_ref[...].astype(F32))
        _fill_shifts(win, tm + HALO_A)

        def rows_block(rb, carry):
            r0 = pl.multiple_of(rb * CONV_ROWS, CONV_ROWS)
            rows = pl.ds(r0, CONV_ROWS)
            da1_b = dwin[0, rows, :]
            da0 = jnp.zeros((CONV_ROWS, C), F32)
            for k in range(W):
                da0 = da0 + cw_ref[k:k + 1, :] * _tap(dwin, W - 1 - k, CONV_ROWS, r0)
                dcw_ref[k:k + 1, :] += jnp.sum(da1_b * _tap(win, HALO_A - (W - 1) + k, CONV_ROWS, r0), axis=0, keepdims=True)
            u = u_ref[rows, :].astype(F32)
            sg = jax.nn.sigmoid(gt_ref[rows, :].astype(F32))
            duz_ref[rows, 0:C] = (da0 * sg).astype(BF16)
            duz_ref[rows, C:2 * C] = (da0 * u * sg * (1.0 - sg)).astype(BF16)
            return carry

        lax.fori_loop(0, tm // CONV_ROWS, rows_block, 0)

    cur = lambda c: pl.BlockSpec((tm, C), lambda i, c=c: (i, c))
    prv = lambda c: pl.BlockSpec((HALO_A, C), lambda i, c=c: (jnp.maximum(i * hb - 1, 0), c))
    nxt = pl.BlockSpec((HALO_A, C), lambda i: (jnp.minimum((i + 1) * hb, S // HALO_A - 1), 0))
    vec = pl.BlockSpec((1, C), lambda i: (0, 0))
    return _pallas(body, name="conv_a_bwd", grid=(nt,),
                   in_specs=[cur(0), cur(0), nxt, nxt, cur(0), cur(1), prv(0), prv(1),
                             pl.BlockSpec((32, C), lambda i: (0, 0)), vec],
                   out_specs=[pl.BlockSpec((tm, 2 * C), lambda i: (i, 0)), pl.BlockSpec((32, C), lambda i: (0, 0)), vec, vec],
                   out_shape=[_sds((S, 2 * C), BF16), _sds((32, C), F32), _sds((1, C), F32), _sds((1, C), F32)],
                   scratch_shapes=[pltpu.VMEM((SUBLANES, tm + HALO_A, C), F32)] * 2,
                   compiler_params=_cp(("arbitrary",)))(da, a1, da, a1, z, z, z, z, cw, cn)


def _forget_scan(fl, bf):
    S, L = fl.shape
    B = SCAN_BLK

    def body(fl_ref, bf_ref, flb_ref, F_ref):
        tri = (lax.broadcasted_iota(jnp.int32, (B, B), 0) >= lax.broadcasted_iota(jnp.int32, (B, B), 1)).astype(F32)

        def step(c, carry):
            rows = pl.ds(pl.multiple_of(c * B, B), B)
            v = fl_ref[rows, :] + bf_ref[...]
            flb_ref[rows, :] = v
            lf = jnp.minimum(v, 0.0) - jnp.log1p(jnp.exp(-jnp.abs(v)))
            cs = jnp.dot(tri, lf, precision=lax.Precision.HIGHEST, preferred_element_type=F32) + carry
            F_ref[rows, :] = cs
            return cs[B - 1:B, :]

        lax.fori_loop(0, S // B, step, jnp.zeros((1, L), F32))

    return _pallas(body, name="forget_scan", out_shape=[_sds((S, L), F32), _sds((S, L), F32)],
                   compiler_params=_cp())(fl, bf)


def _forget_scan_bwd(dF, flb):
    S, L = dF.shape
    B = SCAN_BLK
    nb = S // B

    def body(dF_ref, flb_ref, dfl_ref, db_ref):
        tri = (lax.broadcasted_iota(jnp.int32, (B, B), 0) <= lax.broadcasted_iota(jnp.int32, (B, B), 1)).astype(F32)

        def step(t, carry):
            carry_cs, db = carry
            rows = pl.ds(pl.multiple_of((nb - 1 - t) * B, B), B)
            cs = jnp.dot(tri, dF_ref[rows, :], precision=lax.Precision.HIGHEST, preferred_element_type=F32) + carry_cs
            dfl = cs * jax.nn.sigmoid(-flb_ref[rows, :])
            dfl_ref[rows, :] = dfl
            return cs[0:1, :], db + jnp.sum(dfl, axis=0, keepdims=True)

        _, db = lax.fori_loop(0, nb, step, (jnp.zeros((1, L), F32), jnp.zeros((1, L), F32)))
        db_ref[...] = db

    return _pallas(body, name="forget_scan_bwd", out_shape=[_sds((S, L), F32), _sds((1, L), F32)],
                   compiler_params=_cp())(dF, flb)


NEG = -1e30


def _causal_mask(t):
    return lax.broadcasted_iota(jnp.int32, (t, t), 0) >= lax.broadcasted_iota(jnp.int32, (t, t), 1)


AUG = 128
C_F, C_ONE, C_LSE = 64, 67, 70


def _split3(f):
    a = f.astype(BF16).astype(F32)
    r = f - a
    b = r.astype(BF16).astype(F32)
    return a, b, r - b


def _put3(lane, base, parts, other):
    out = other
    for k, p in enumerate(parts):
        out = jnp.where(lane == base + k, p, out)
    return out


def _ones3(lane, base):
    return (lane >= base) & (lane < base + 3)


def _lane_ids(rows):
    return lax.broadcasted_iota(jnp.int32, (rows, AUG), 1)


def _pair_rms(x, lo):
    sq = x * x
    ms_a = jnp.sum(jnp.where(lo, sq, 0.0), axis=-1, keepdims=True) * (1.0 / HEAD_DIM)
    ms_b = jnp.sum(jnp.where(lo, 0.0, sq), axis=-1, keepdims=True) * (1.0 / HEAD_DIM)
    return jnp.where(lo, lax.rsqrt(ms_a + EPS), lax.rsqrt(ms_b + EPS))


def _qkv_prep(z, Fc, qw, kw):
    S = z.shape[0]
    tp = min(QKN_TILE, S)
    scale = 1.0 / math.sqrt(HEAD_DIM)

    def body(zq_ref, zk_ref, zv_ref, F_ref, qw_ref, kw_ref, q_ref, k_ref, v_ref):
        j = pl.program_id(0)
        lane = _lane_ids(tp)
        lo = lane < HEAD_DIM
        Fv = F_ref[...]
        xq = zq_ref[...].astype(F32)
        xk = zk_ref[...].astype(F32)
        qn = xq * _pair_rms(xq, lo) * qw_ref[...] * scale
        kn = xk * _pair_rms(xk, lo) * kw_ref[...]
        vv = zv_ref[...].astype(F32)
        for half in range(2):
            take = (lambda a: a) if half == 0 else (lambda a: pltpu.roll(a, HEAD_DIM, 1))
            fp = _split3(jnp.sum(jnp.where(lane == 2 * j + half, Fv, 0.0), axis=-1, keepdims=True))
            qx = _put3(lane, C_F, fp, jnp.where(_ones3(lane, C_ONE), 1.0, 0.0))
            kx = _put3(lane, C_ONE, [-p for p in fp], jnp.where(_ones3(lane, C_F) | _ones3(lane, C_LSE), 1.0, 0.0))
            vx = jnp.where(_ones3(lane, C_F), 1.0, 0.0)
            q_ref[half] = jnp.where(lo, take(qn), qx).astype(BF16)
            k_ref[half] = jnp.where(lo, take(kn), kx).astype(BF16)
            v_ref[half] = jnp.where(lo, take(vv), vx).astype(BF16)

    col = lambda c0: pl.BlockSpec((tp, AUG), lambda j, i, c0=c0: (i, c0 + j))
    vec = pl.BlockSpec((1, AUG), lambda j, i: (0, 0))
    out = pl.BlockSpec((2, tp, AUG), lambda j, i: (j, i, 0))
    return _pallas(body, name="qkv_prep", grid=(N_HEADS // 2, S // tp),
                   in_specs=[col(8), col(12), col(16), pl.BlockSpec((tp, AUG), lambda j, i: (i, 0)), vec, vec],
                   out_specs=[out, out, out], out_shape=[_sds((N_HEADS, S, AUG), BF16)] * 3,
                   compiler_params=_cp(("parallel", "parallel")))(z, z, z, Fc, qw, kw)


def _fox_fwd(q_aug, k_aug, v_aug):
    H, S, A = q_aug.shape
    t = ATT_TILE
    nq = S // t

    def body(q_ref, k_ref, v_ref, o_ref, q2_ref):
        i = pl.program_id(1)
        q = q_ref[0]

        def tile(j, carry, diag):
            m, acc = carry
            rows = pl.ds(pl.multiple_of(j * t, t), t)
            s = _dot_nt(q, k_ref[0, rows, :])
            if diag:
                s = jnp.where(_causal_mask(t), s, NEG)
            m_new = jnp.maximum(m, jnp.max(s, axis=-1, keepdims=True))
            p = jnp.exp(s - m_new)
            acc = jnp.exp(m - m_new) * acc + _dot(p.astype(BF16), v_ref[0, rows, :])
            return m_new, acc

        init = (jnp.full((t, 1), NEG, F32), jnp.zeros((t, A), F32))
        carry = lax.fori_loop(0, i, lambda j, c: tile(j, c, False), init)
        m, acc = tile(i, carry, True)
        lane = _lane_ids(t)
        l = jnp.sum(jnp.where(lane == C_F, acc, 0.0), axis=-1, keepdims=True)
        o_ref[0] = (acc / l).astype(BF16)
        lse = m + jnp.log(l)
        q2_ref[0] = (q.astype(F32) + _put3(lane, C_LSE, [-p for p in _split3(lse)], 0.0)).astype(BF16)

    qblk = pl.BlockSpec((1, t, A), lambda h, i: (h, i, 0))
    full = pl.BlockSpec((1, S, A), lambda h, i: (h, 0, 0))
    return _pallas(body, name="fox_fwd", grid=(H, nq), in_specs=[qblk, full, full], out_specs=[qblk, qblk],
                   out_shape=[_sds((H, S, A), BF16)] * 2, compiler_params=_cp(("parallel", "parallel")))(q_aug, k_aug, v_aug)


def _do_prep(dcat, o_aug):
    S = dcat.shape[0]
    tp = min(QKN_TILE, S)

    def body(d_ref, o_ref, out_ref):
        lane = _lane_ids(tp)
        lo = lane < HEAD_DIM
        x = d_ref[...]
        for half in range(2):
            d = jnp.where(lo, x if half == 0 else pltpu.roll(x, HEAD_DIM, 1), 0.0)
            delta = jnp.sum(d * o_ref[half].astype(F32), axis=-1, keepdims=True)
            out_ref[half] = jnp.where(lo, d, _put3(lane, C_F, [-p for p in _split3(delta)], 0.0)).astype(BF16)

    pair = pl.BlockSpec((2, tp, AUG), lambda j, i: (j, i, 0))
    return _pallas(body, name="do_prep", grid=(N_HEADS // 2, S // tp),
                   in_specs=[pl.BlockSpec((tp, AUG), lambda j, i: (i, D_CONV // AUG + j)), pair], out_specs=pair,
                   out_shape=_sds((N_HEADS, S, AUG), BF16), compiler_params=_cp(("parallel", "parallel")))(dcat, o_aug)


def _fox_bwd(q2, k_aug, v_aug, do_aug):
    H, S, A = q2.shape
    t = ATT_TILE
    nq = S // t

    def body(q_ref, k_ref, v_ref, do_ref, dq_ref, dk_ref, dv_ref):
        j = pl.program_id(1)

        @pl.when(j == 0)
        def _():
            dq_ref[...] = jnp.zeros_like(dq_ref)

        k = k_ref[0]
        vv = v_ref[0]

        def tile(i, carry, diag):
            dk, dv = carry
            rows = pl.ds(pl.multiple_of(i * t, t), t)
            q = q_ref[0, rows, :]
            dov = do_ref[0, rows, :]
            s = _dot_nt(q, k)
            if diag:
                s = jnp.where(_causal_mask(t), s, NEG)
            p = jnp.exp(s)
            dv = dv + _dot_tn(p.astype(BF16), dov)
            dsb = (p * _dot_nt(dov, vv)).astype(BF16)
            dq_ref[0, rows, :] += _dot(dsb, k)
            dk = dk + _dot_tn(dsb, q)
            return dk, dv

        init = (jnp.zeros((t, A), F32), jnp.zeros((t, A), F32))
        carry = tile(j, init, True)
        dk, dv = lax.fori_loop(j + 1, nq, lambda i, c: tile(i, c, False), carry)
        dk_ref[0] = dk
        dv_ref[0] = dv

    full = pl.BlockSpec((1, S, A), lambda h, j: (h, 0, 0))
    kblk = pl.BlockSpec((1, t, A), lambda h, j: (h, j, 0))
    return _pallas(body, name="fox_bwd", grid=(H, nq), in_specs=[full, kblk, kblk, full], out_specs=[full, kblk, kblk],
                   out_shape=[_sds((H, S, A), F32)] * 3,
                   compiler_params=_cp(("parallel", "arbitrary")))(q2, k_aug, v_aug, do_aug)


def _qkv_bwd(dq, dk, dv, z, qw, kw):
    S = z.shape[0]
    tp = min(QKN_TILE, S)
    scale = 1.0 / math.sqrt(HEAD_DIM)

    def body(dq_ref, dk_ref, dv_ref, zq_ref, zk_ref, qw_ref, kw_ref, dqf_ref, dkf_ref, dvf_ref, dF_ref, dqw_ref, dkw_ref):
        i, j = pl.program_id(0), pl.program_id(1)
        lane = _lane_ids(tp)
        lo = lane < HEAD_DIM

        @pl.when((i == 0) & (j == 0))
        def _():
            dqw_ref[...] = jnp.zeros_like(dqw_ref)
            dkw_ref[...] = jnp.zeros_like(dkw_ref)

        def pair(ref):
            return jnp.where(lo, ref[0], pltpu.roll(ref[1], HEAD_DIM, 1))

        def norm_bwd(g, x, w):
            r = _pair_rms(x, lo)
            xh = x * r
            dxh = g * w
            tt = dxh * xh
            mean_a = jnp.sum(jnp.where(lo, tt, 0.0), axis=-1, keepdims=True) * (1.0 / HEAD_DIM)
            mean_b = jnp.sum(jnp.where(lo, 0.0, tt), axis=-1, keepdims=True) * (1.0 / HEAD_DIM)
            return r * (dxh - xh * jnp.where(lo, mean_a, mean_b)), g * xh

        dxq, gq = norm_bwd(pair(dq_ref) * scale, zq_ref[...].astype(F32), qw_ref[...])
        dqf_ref[...] = dxq.astype(BF16)
        dqw_ref[...] += jnp.sum(gq, axis=0, keepdims=True)
        dxk, gk = norm_bwd(pair(dk_ref), zk_ref[...].astype(F32), kw_ref[...])
        dkf_ref[...] = dxk.astype(BF16)
        dkw_ref[...] += jnp.sum(gk, axis=0, keepdims=True)
        dvf_ref[...] = pair(dv_ref).astype(BF16)

        contrib = jnp.zeros((tp, AUG), F32)
        for half in range(2):
            df = (jnp.sum(jnp.where(lane == C_F, dq_ref[half], 0.0), axis=-1, keepdims=True)
                  - jnp.sum(jnp.where(lane == C_ONE, dk_ref[half], 0.0), axis=-1, keepdims=True))
            contrib = jnp.where(lane == 2 * j + half, df, contrib)

        @pl.when(j == 0)
        def _():
            dF_ref[...] = contrib

        @pl.when(j > 0)
        def _():
            dF_ref[...] += contrib

    pairb = pl.BlockSpec((2, tp, AUG), lambda i, j: (j, i, 0))
    col = lambda c0: pl.BlockSpec((tp, AUG), lambda i, j, c0=c0: (i, c0 + j))
    vec = pl.BlockSpec((1, AUG), lambda i, j: (0, 0))
    flat = pl.BlockSpec((tp, AUG), lambda i, j: (i, j))
    return _pallas(body, name="qkv_bwd", grid=(S // tp, N_HEADS // 2),
                   in_specs=[pairb, pairb, pairb, col(8), col(12), vec, vec],
                   out_specs=[flat, flat, flat, pl.BlockSpec((tp, AUG), lambda i, j: (i, 0)), vec, vec],
                   out_shape=[_sds((S, D_ATTN), BF16)] * 3 + [_sds((S, AUG), F32), _sds((1, AUG), F32), _sds((1, AUG), F32)],
                   compiler_params=_cp(("arbitrary", "arbitrary")))(dq, dk, dv, z, z, qw, kw)


def _proj_res_heads(a, wa, o_aug, wo, res):
    S, D = res.shape
    H = o_aug.shape[0]
    tm = TOK_TILE

    def body(a_ref, wa_ref, o_ref, wo_ref, res_ref, out_ref):
        acc = res_ref[...] + _dot(a_ref[...], wa_ref[...])
        for h in range(H):
            acc = acc + _dot(o_ref[h], wo_ref[h])
        out_ref[...] = acc

    row = pl.BlockSpec((tm, D), lambda i: (i, 0))
    return _pallas(body, name="proj_res_heads", grid=(S // tm,),
                   in_specs=[pl.BlockSpec((tm, a.shape[1]), lambda i: (i, 0)), pl.BlockSpec(wa.shape, lambda i: (0, 0)),
                             pl.BlockSpec((H, tm, AUG), lambda i: (0, i, 0)), pl.BlockSpec(wo.shape, lambda i: (0, 0, 0)), row],
                   out_specs=row, out_shape=_sds((S, D), F32), compiler_params=_cp(("parallel",)))(a, wa, o_aug, wo, res)


def _heads_tn(o_aug, d):
    H, S, A = o_aug.shape
    D = d.shape[1]
    tm = min(DW_TILE, S)
    nt = S // tm

    def body(o_ref, d_ref, out_ref, acc_ref):
        i = pl.program_id(0)

        @pl.when(i == 0)
        def _():
            acc_ref[...] = jnp.zeros_like(acc_ref)

        dv = d_ref[...].astype(BF16)
        for h in range(H):
            acc_ref[h] += _dot_tn(o_ref[h], dv)

        @pl.when(i == nt - 1)
        def _():
            out_ref[...] = acc_ref[...].astype(BF16)

    return _pallas(body, name="heads_tn", grid=(nt,),
                   in_specs=[pl.BlockSpec((H, tm, A), lambda i: (0, i, 0)), pl.BlockSpec((tm, D), lambda i: (i, 0))],
                   out_specs=pl.BlockSpec((H, A, D), lambda i: (0, 0, 0)), out_shape=_sds((H, A, D), BF16),
                   scratch_shapes=[pltpu.VMEM((H, A, D), F32)], compiler_params=_cp(("arbitrary",)))(o_aug, d)


def _odd_mid_fwd(z, cw):
    S = z.shape[0]
    D = z.shape[1] // 3
    tm = TOK_TILE
    hb = tm // HALO_C
    W = CONV_C_WIDTH

    def body(gb_ref, gc_ref, hh_ref, gcp_ref, hhp_ref, cw_ref, y_ref, win):
        i = pl.program_id(0)
        prev = gcp_ref[...].astype(F32) * hhp_ref[...].astype(F32)
        win[pl.ds(0, HALO_C), :] = jnp.where(i == 0, 0.0, prev)
        win[pl.ds(HALO_C, tm), :] = gc_ref[...].astype(F32) * hh_ref[...].astype(F32)
        c1 = jnp.zeros((tm, D), F32)
        for k in range(W):
            c1 = c1 + cw_ref[k:k + 1, :] * win[pl.ds(HALO_C - (W - 1) + k, tm), :]
        y_ref[...] = (gb_ref[...].astype(F32) * c1).astype(BF16)

    cur = lambda c: pl.BlockSpec((tm, D), lambda i, c=c: (i, c))
    prv = lambda c: pl.BlockSpec((HALO_C, D), lambda i, c=c: (jnp.maximum(i * hb - 1, 0), c))
    return _pallas(body, name="odd_mid_fwd", grid=(S // tm,),
                   in_specs=[cur(0), cur(1), cur(2), prv(1), prv(2), pl.BlockSpec((8, D), lambda i: (0, 0))],
                   out_specs=pl.BlockSpec((tm, D), lambda i: (i, 0)), out_shape=_sds((S, D), BF16),
                   scratch_shapes=[pltpu.VMEM((tm + HALO_C, D), F32)],
                   compiler_params=_cp(("parallel",)))(z, z, z, z, z, cw)


def _odd_mid_bwd(dy, z, cw):
    S = z.shape[0]
    D = z.shape[1] // 3
    tm = TOK_TILE
    hb = tm // HALO_C
    nt = S // tm
    W = CONV_C_WIDTH

    def body(dy_ref, dyn_ref, gb_ref, gbn_ref, gc_ref, hh_ref, gcp_ref, hhp_ref, cw_ref, dz_ref, dcw_ref, win, dwin):
        i = pl.program_id(0)

        @pl.when(i == 0)
        def _():
            dcw_ref[...] = jnp.zeros_like(dcw_ref)

        gc = gc_ref[...].astype(F32)
        hh = hh_ref[...].astype(F32)
        prev = gcp_ref[...].astype(F32) * hhp_ref[...].astype(F32)
        win[pl.ds(0, HALO_C), :] = jnp.where(i == 0, 0.0, prev)
        win[pl.ds(HALO_C, tm), :] = gc * hh
        dyv = dy_ref[...]
        dc1 = dyv * gb_ref[...].astype(F32)
        dwin[pl.ds(0, tm), :] = dc1
        dwin[pl.ds(tm, HALO_C), :] = jnp.where(i == nt - 1, 0.0, dyn_ref[...] * gbn_ref[...].astype(F32))
        c1 = jnp.zeros((tm, D), F32)
        dc0 = jnp.zeros((tm, D), F32)
        for k in range(W):
            tap = win[pl.ds(HALO_C - (W - 1) + k, tm), :]
            c1 = c1 + cw_ref[k:k + 1, :] * tap
            dc0 = dc0 + cw_ref[k:k + 1, :] * dwin[pl.ds(W - 1 - k, tm), :]
            dcw_ref[k:k + 1, :] += jnp.sum(dc1 * tap, axis=0, keepdims=True)
        dz_ref[:, 0:D] = (dyv * c1).astype(BF16)
        dz_ref[:, D:2 * D] = (dc0 * hh).astype(BF16)
        dz_ref[:, 2 * D:3 * D] = (dc0 * gc).astype(BF16)

    cur = lambda c: pl.BlockSpec((tm, D), lambda i, c=c: (i, c))
    prv = lambda c: pl.BlockSpec((HALO_C, D), lambda i, c=c: (jnp.maximum(i * hb - 1, 0), c))
    nxt = pl.BlockSpec((HALO_C, D), lambda i: (jnp.minimum((i + 1) * hb, S // HALO_C - 1), 0))
    return _pallas(body, name="odd_mid_bwd", grid=(nt,),
                   in_specs=[cur(0), nxt, cur(0), nxt, cur(1), cur(2), prv(1), prv(2), pl.BlockSpec((8, D), lambda i: (0, 0))],
                   out_specs=[pl.BlockSpec((tm, 3 * D), lambda i: (i, 0)), pl.BlockSpec((8, D), lambda i: (0, 0))],
                   out_shape=[_sds((S, 3 * D), BF16), _sds((8, D), F32)],
                   scratch_shapes=[pltpu.VMEM((tm + HALO_C, D), F32), pltpu.VMEM((tm + HALO_C, D), F32)],
                   compiler_params=_cp(("arbitrary",)))(dy, dy, z, z, z, z, z, z, cw)


def _loss_head(y, tgt):
    S, D = y.shape
    tm = TOK_TILE

    def body(y_ref, t_ref, dy_ref, l_ref):
        @pl.when(pl.program_id(0) == 0)
        def _():
            l_ref[...] = jnp.zeros_like(l_ref)

        e = y_ref[...] - t_ref[...]
        dy_ref[...] = e * (1.0 / D)
        l_ref[...] += jnp.sum(jnp.sum(e * e, axis=-1, keepdims=True), axis=0, keepdims=True) * (0.5 / D)

    row = pl.BlockSpec((tm, D), lambda i: (i, 0))
    return _pallas(body, name="loss_head", grid=(S // tm,), in_specs=[row, row],
                   out_specs=[row, pl.BlockSpec((1, 1), lambda i: (0, 0))],
                   out_shape=[_sds((S, D), F32), _sds((1, 1), F32)],
                   compiler_params=_cp(("arbitrary",)))(y, tgt)


def _pad_rows(a, rows):
    return jnp.pad(a, ((0, rows - a.shape[0]), (0, 0)))


def _local_step(x, tgt, W, need=lambda block, after: None, done=lambda block, block_grads: None):
    S, D = x.shape
    grads = {}
    saved = {}

    def gain_after(gain, token):
        return gain if token is None else gain + token

    def ffn_f(tag, l, xin):
        need((tag, l), xin)
        out, xn, G, U = _ffn_fwd(xin, W[tag + "_norm"][l:l + 1], W[tag + "_w_gate"][l], W[tag + "_w_up"][l],
                                 W[tag + "_w_down"][l])
        saved[(tag, l)] = (xin, xn, G, U)
        return out

    def ffn_b(tag, l, dout):
        xin, xn, G, U = saved[(tag, l)]
        keys = [(tag + "_w_gate", l), (tag + "_w_up", l), (tag + "_w_down", l)]
        *dws, dG, dU = _ffn_bwd_w(dout, xn, G, U, W[tag + "_w_down"][l])
        big = dict(zip(keys, dws))
        grads.update(big)
        token = done((tag, l), big)
        dx, dg = _norm_in_bwd([dG, dU], [W[tag + "_w_gate"][l], W[tag + "_w_up"][l]], xin,
                              gain_after(W[tag + "_norm"][l:l + 1], token), dout, w_rows=True)
        grads[(tag + "_norm", l)] = dg
        return dx

    x0a = ffn_f("ffn1", 0, x)
    need(("ev", 0), x0a)
    w_in = W["ev_w_in"]
    w_main, w_f = w_in[:, :2560], jnp.pad(w_in[:, 2560:], ((0, 0), (0, 120)))
    h0, z0, fl = _norm_proj(x0a, W["mix_norm"][0:1], w_main, w_f)
    cw_a = _pad_rows(W["ev_conv_w"], 32)
    a_act, a1 = _conv_a_fwd(z0, cw_a, W["ev_conv_b"], W["ev_conv_norm"])
    flb, Fc = _forget_scan(fl, jnp.pad(W["ev_b_f"], ((0, 0), (0, 120))))
    qw2, kw2 = jnp.tile(W["ev_q_norm"], (1, 2)), jnp.tile(W["ev_k_norm"], (1, 2))
    q_aug, k_aug, v_aug = _qkv_prep(z0, Fc, qw2, kw2)
    o_aug, q_lse = _fox_fwd(q_aug, k_aug, v_aug)
    w_out_e = W["ev_w_out"]
    w_out_o = jnp.pad(w_out_e[D_CONV:].reshape(N_HEADS, HEAD_DIM, D), ((0, 0), (0, AUG - HEAD_DIM), (0, 0)))
    x0b = _proj_res_heads(a_act, w_out_e[:D_CONV], o_aug, w_out_o, x0a)
    x0c = ffn_f("ffn2", 0, x0b)
    x1a = ffn_f("ffn1", 1, x0c)
    need(("od", 0), x1a)
    h1, z1 = _norm_proj(x1a, W["mix_norm"][1:2], W["od_w_in"])
    cw_c = _pad_rows(W["od_conv_w"], 8)
    y1 = _odd_mid_fwd(z1, cw_c)
    x1b = _proj_res([y1], [W["od_w_out"]], x1a)
    x1c = ffn_f("ffn2", 1, x1b)
    dy, loss = _loss_head(x1c, tgt)

    d = ffn_b("ffn2", 1, dy)
    dy1 = _matmul_nt(d, W["od_w_out"])
    grads[("od_w_out", 0)] = _matmul_tn(y1, d, D)[0]
    dz1, dcw_c = _odd_mid_bwd(dy1, z1, cw_c)
    grads[("od_conv_w", 0)] = dcw_c[:CONV_C_WIDTH]
    grads[("od_w_in", 0)] = _matmul_tn(h1, dz1, 3 * D // 4)
    token = done(("od", 0), {k: grads[k] for k in (("od_w_out", 0), ("od_w_in", 0))})
    d, dg = _norm_in_bwd([dz1[None]], [W["od_w_in"][None]], x1a, gain_after(W["mix_norm"][1:2], token), d)
    grads[("mix_norm", 1)] = dg
    d = ffn_b("ffn1", 1, d)
    d = ffn_b("ffn2", 0, d)
    dcat = _matmul_nt(d, w_out_e)
    grads[("ev_w_out", 0)] = jnp.concatenate([_matmul_tn(a_act, d, D)[0],
                                              _heads_tn(o_aug, d)[:, :HEAD_DIM].reshape(D_ATTN, D)], axis=0)
    duz, dcw_a, dcb, dcn = _conv_a_bwd(dcat, a1, z0, cw_a, W["ev_conv_norm"])
    grads[("ev_conv_w", 0)] = dcw_a[:CONV_A_WIDTH]
    grads[("ev_conv_b", 0)] = dcb
    grads[("ev_conv_norm", 0)] = dcn
    dq_a, dk_a, dv_a = _fox_bwd(q_lse, k_aug, v_aug, _do_prep(dcat, o_aug))
    dqf, dkf, dvf, dF, dqw, dkw = _qkv_bwd(dq_a, dk_a, dv_a, z0, qw2, kw2)
    grads[("ev_q_norm", 0)] = dqw[:, :HEAD_DIM] + dqw[:, HEAD_DIM:]
    grads[("ev_k_norm", 0)] = dkw[:, :HEAD_DIM] + dkw[:, HEAD_DIM:]
    dfl, dbf = _forget_scan_bwd(dF, flb)
    grads[("ev_b_f", 0)] = dbf[:, :N_HEADS]
    dz0 = jnp.concatenate([duz, dqf, dkf, dvf], axis=1)
    dflb = dfl.astype(BF16)
    gmain = _matmul_tn(h0, dz0, 640)
    gmain = gmain.transpose(1, 0, 2).reshape(D, 2560)
    gf = _matmul_tn(h0, dflb, 128)[0][:, :N_HEADS]
    grads[("ev_w_in", 0)] = jnp.concatenate([gmain, gf], axis=1)
    token = done(("ev", 0), {k: grads[k] for k in (("ev_w_out", 0), ("ev_w_in", 0))})
    d, dg = _norm_in_bwd([dz0[None], dflb[None]], [w_main[None], w_f[None]], x0a, gain_after(W["mix_norm"][0:1], token), d)
    grads[("mix_norm", 0)] = dg
    d = ffn_b("ffn1", 0, d)
    return loss, d, grads


def _place():
    x, y, c = lax.axis_index("x"), lax.axis_index("y"), lax.axis_index("c")
    chips = [(1 - x, y), (x, 1 - y), (1 - x, 1 - y)]
    return x, y, c, chips


def _remote(src, dst, send_sem, recv_sem, to):
    return pltpu.make_async_remote_copy(src_ref=src, dst_ref=dst, send_sem=send_sem, recv_sem=recv_sem,
                                        device_id=to, device_id_type=MESH)


HBM = pl.BlockSpec(memory_space=pltpu.HBM)
SEM = pl.BlockSpec(memory_space=pltpu.SEMAPHORE)
EFFECT = pltpu.SideEffectType.DATAFLOW_SIDE_EFFECTING


def _in_hbm(a):
    return pltpu.with_memory_space_constraint(a, pltpu.HBM)


def _ag_start(tag, bufs, with_taps):
    n = len(bufs)
    order = ([n - 1] + list(range(n - 1))) if with_taps else list(range(n))

    def body(*refs):
        send_sems, recv_sems = refs[n], refs[n + 1]
        outs, token = refs[n + 2:2 * n + 2], refs[2 * n + 2]
        x, y, c, chips = _place()
        me = 2 * x + y
        for a in order:
            if with_taps and a == n - 1:
                blk = outs[a].at[me]
            else:
                h = outs[a].shape[1] // 2
                blk = outs[a].at[me, pl.ds(c * h, h)]
            for jj, (px, py) in enumerate(chips):
                _remote(blk, blk, send_sems.at[3 * a + jj], recv_sems.at[3 * a + jj], (px, py, c)).start()
        token[...] = jnp.zeros_like(token)

    return _pallas(
        body, name=f"gather_start_{tag}",
        out_shape=[pltpu.SemaphoreType.DMA((3 * n,)), pltpu.SemaphoreType.DMA((3 * n,))]
        + [pltpu.HBM(b.shape, b.dtype) for b in bufs] + [_sds((8, 128), F32)],
        in_specs=[HBM] * n, out_specs=[SEM, SEM] + [HBM] * n + [pl.BlockSpec(memory_space=pltpu.VMEM)],
        input_output_aliases={a: 2 + a for a in range(n)},
        compiler_params=pltpu.CompilerParams(has_side_effects=EFFECT),
    )(*[_in_hbm(b) for b in bufs])


def _ag_mid(g, ici_send, ici_recv, bufs, idx, taps, n_big, after):
    n = len(bufs)
    arrs = list(bufs) + ([taps] if taps is not None else [])
    m = len(arrs)

    def body(*refs):
        ici_s, ici_r = refs[0], refs[1]
        d_send, d_recv = refs[m + 3], refs[m + 4]
        outs = refs[m + 5:]
        x, y, c, chips = _place()
        me = 2 * x + y
        for i in range(m):
            a = idx[i] if i < n else n_big
            for jj, (px, py) in enumerate(chips):
                k = 3 * a + jj
                if i < n:
                    h = outs[i].shape[1] // 2
                    mine, blk = outs[i].at[me, pl.ds(c * h, h)], outs[i].at[2 * px + py, pl.ds(c * h, h)]
                else:
                    mine, blk = outs[i].at[me], outs[i].at[2 * px + py]
                _remote(mine, mine, ici_s.at[k], ici_r.at[k], (px, py, c)).wait_send()
                _remote(blk, blk, ici_s.at[k], ici_r.at[k], (px, py, c)).wait_recv()
                if i < n:
                    _remote(blk, blk, d_send.at[3 * i + jj], d_recv.at[3 * i + jj], (x, y, 1 - c)).start()

    return _pallas(
        body, name=f"gather_pass_on_{g}",
        out_shape=[pltpu.SemaphoreType.DMA((3 * n,)), pltpu.SemaphoreType.DMA((3 * n,))] + [pltpu.HBM(b.shape, b.dtype) for b in arrs],
        in_specs=[SEM, SEM] + [HBM] * m + [ANY], out_specs=[SEM, SEM] + [HBM] * m,
        input_output_aliases={2 + i: 2 + i for i in range(m)},
        compiler_params=pltpu.CompilerParams(has_side_effects=EFFECT),
    )(ici_send, ici_recv, *arrs, after)


def _ag_wait(g, d_send, d_recv, arrs, n, after):
    m = len(arrs)

    def body(*refs):
        d_s, d_r = refs[0], refs[1]
        outs = refs[m + 3:]
        x, y, c, chips = _place()
        for i in range(n):
            h = outs[i].shape[1] // 2
            for jj, (px, py) in enumerate(chips):
                sent = outs[i].at[2 * px + py, pl.ds(c * h, h)]
                got = outs[i].at[2 * px + py, pl.ds((1 - c) * h, h)]
                _remote(sent, sent, d_s.at[3 * i + jj], d_r.at[3 * i + jj], (x, y, 1 - c)).wait_send()
                _remote(got, got, d_s.at[3 * i + jj], d_r.at[3 * i + jj], (x, y, 1 - c)).wait_recv()

    return _pallas(
        body, name=f"gather_wait_{g}", out_shape=[pltpu.HBM(b.shape, b.dtype) for b in arrs],
        in_specs=[SEM, SEM] + [HBM] * m + [ANY], out_specs=[HBM] * m,
        input_output_aliases={2 + i: i for i in range(m)},
        compiler_params=pltpu.CompilerParams(has_side_effects=EFFECT),
    )(d_send, d_recv, *arrs, after)


def _pair_start(g, gs, after):
    n = len(gs)
    zones = [lax.empty((4, a.shape[1] // 2, a.shape[2]), a.dtype) for a in gs]
    extra = [] if after is None else [after]

    def body(*refs):
        k0 = 2 * n + len(extra)
        send_sems, recv_sems = refs[k0], refs[k0 + 1]
        src, dst = refs[k0 + 2:k0 + 2 + n], refs[k0 + 2 + n:k0 + 2 + 2 * n]
        token = refs[k0 + 2 + 2 * n]
        x, y, c, _ = _place()
        for a in range(n):
            h = src[a].shape[1] // 2
            _remote(src[a].at[:, pl.ds((1 - c) * h, h)], dst[a], send_sems.at[a], recv_sems.at[a], (x, y, 1 - c)).start()
        token[...] = jnp.zeros_like(token)

    return _pallas(
        body, name=f"grad_pair_start_{g}",
        out_shape=[pltpu.SemaphoreType.DMA((n,)), pltpu.SemaphoreType.DMA((n,))]
        + [pltpu.HBM(a.shape, a.dtype) for a in gs + zones] + [_sds((8, 128), F32)],
        in_specs=[HBM] * (2 * n) + [ANY] * len(extra),
        out_specs=[SEM, SEM] + [HBM] * (2 * n) + [pl.BlockSpec(memory_space=pltpu.VMEM)],
        input_output_aliases={i: 2 + i for i in range(2 * n)},
        compiler_params=pltpu.CompilerParams(has_side_effects=EFFECT),
    )(*[_in_hbm(a) for a in gs + zones], *extra)


def _pair_wait(g, send, recv, gs, zones):
    n = len(gs)

    def body(*refs):
        s_ref, r_ref = refs[0], refs[1]
        outs = refs[2 + 2 * n:]
        src, dst = outs[:n], outs[n:]
        x, y, c, _ = _place()
        for a in range(n):
            h = src[a].shape[1] // 2
            _remote(src[a].at[:, pl.ds((1 - c) * h, h)], dst[a], s_ref.at[a], r_ref.at[a], (x, y, 1 - c)).wait()

    return _pallas(
        body, name=f"grad_pair_wait_{g}", out_shape=[pltpu.HBM(a.shape, a.dtype) for a in gs + zones],
        in_specs=[SEM, SEM] + [HBM] * (2 * n), out_specs=[HBM] * (2 * n),
        input_output_aliases={2 + i: i for i in range(2 * n)},
        compiler_params=pltpu.CompilerParams(has_side_effects=EFFECT),
    )(send, recv, *gs, *zones)


def _pair_add(gs, others, c_arr):
    n = len(gs)

    def body(c_ref, *refs):
        for g_ref, o_ref, out_ref in zip(refs[:n], refs[n:2 * n], refs[2 * n:]):
            out_ref[...] = (g_ref[...].astype(F32) + o_ref[...].astype(F32)).astype(BF16)

    half = lambda a: pl.BlockSpec((1, a.shape[1] // 2, a.shape[2]), lambda k, c_ref: (k, c_ref[0], 0))
    whole = lambda a: pl.BlockSpec((1,) + a.shape[1:], lambda k, c_ref: (k, 0, 0))
    grid_spec = pltpu.PrefetchScalarGridSpec(
        num_scalar_prefetch=1, grid=(4,), in_specs=[half(a) for a in gs] + [whole(o) for o in others],
        out_specs=[whole(o) for o in others])
    return _pallas(body, name="grad_pair_add", grid_spec=grid_spec, out_shape=[_sds(o.shape, BF16) for o in others],
                   compiler_params=_cp(("parallel",)))(c_arr, *gs, *others)


def _chip_start(g, ss):
    n = len(ss)
    zones = [lax.empty((3,) + s.shape[1:], s.dtype) for s in ss]

    def body(*refs):
        send_sems, recv_sems = refs[2 * n], refs[2 * n + 1]
        src, dst = refs[2 * n + 2:3 * n + 2], refs[3 * n + 2:4 * n + 2]
        token = refs[4 * n + 2]
        x, y, c, chips = _place()
        for a in range(n):
            for jj, (px, py) in enumerate(chips):
                k = 3 * a + jj
                _remote(src[a].at[2 * px + py], dst[a].at[jj], send_sems.at[k], recv_sems.at[k], (px, py, c)).start()
        token[...] = jnp.zeros_like(token)

    return _pallas(
        body, name=f"grad_chip_start_{g}",
        out_shape=[pltpu.SemaphoreType.DMA((3 * n,)), pltpu.SemaphoreType.DMA((3 * n,))]
        + [pltpu.HBM(a.shape, a.dtype) for a in ss + zones] + [_sds((8, 128), F32)],
        in_specs=[HBM] * (2 * n), out_specs=[SEM, SEM] + [HBM] * (2 * n) + [pl.BlockSpec(memory_space=pltpu.VMEM)],
        input_output_aliases={i: 2 + i for i in range(2 * n)},
        compiler_params=pltpu.CompilerParams(has_side_effects=EFFECT),
    )(*[_in_hbm(a) for a in ss + zones])


def _chip_wait(tag, sends, recvs, counts, ss, zones, after):
    nb, n = len(sends), len(ss)

    def body(*refs):
        s_refs, r_refs = refs[:nb], refs[nb:2 * nb]
        outs = refs[2 * nb + 2 * n + 1:]
        src, dst = outs[:n], outs[n:]
        x, y, c, chips = _place()
        a = 0
        for b in range(nb):
            for i in range(counts[b]):
                for jj, (px, py) in enumerate(chips):
                    k = 3 * i + jj
                    _remote(src[a].at[2 * px + py], dst[a].at[jj], s_refs[b].at[k], r_refs[b].at[k], (px, py, c)).wait()
                a += 1

    return _pallas(
        body, name=f"grad_chip_wait_{tag}", out_shape=[pltpu.HBM(a.shape, a.dtype) for a in ss + zones],
        in_specs=[SEM] * (2 * nb) + [HBM] * (2 * n) + [ANY], out_specs=[HBM] * (2 * n),
        input_output_aliases={2 * nb + i: i for i in range(2 * n)},
        compiler_params=pltpu.CompilerParams(has_side_effects=EFFECT),
    )(*sends, *recvs, *ss, *zones, after)


def _chip_sum(s, r, where, dest, l, L):
    _, h, C = s.shape
    tr = h // 2

    def body(k_ref, s_ref, r_ref, *rest):
        out_ref = rest[-1]
        acc = s_ref[0].astype(F32)
        for jj in range(3):
            acc = acc + r_ref[jj].astype(F32)
        out_ref[...] = acc

    in_specs = [pl.BlockSpec((1, tr, C), lambda i, k_ref: (k_ref[0], i, 0)), pl.BlockSpec((3, tr, C), lambda i, k_ref: (0, i, 0))]
    args = [where, s, r]
    alias = {}
    if dest is not None:
        in_specs.append(ANY)
        args.append(dest)
        alias = {3: 0}
    grid_spec = pltpu.PrefetchScalarGridSpec(
        num_scalar_prefetch=1, grid=(2,), in_specs=in_specs,
        out_specs=pl.BlockSpec((None, tr, C), lambda i, k_ref: (l, 2 * k_ref[1] + i, 0)))
    return _pallas(body, name="grad_chip_sum", grid_spec=grid_spec, out_shape=_sds((L, 2 * h, C), F32),
                   input_output_aliases=alias, compiler_params=_cp(("arbitrary",)))(*args)


def _share_start(tag, bufs, layout):
    n, n_buf = len(layout), len(bufs)

    def body(*refs):
        send_sems, recv_sems = refs[n_buf], refs[n_buf + 1]
        outs = refs[n_buf + 2:]
        x, y, c, _ = _place()
        for a, (o, l) in enumerate(layout):
            h = outs[o].shape[1] // 2
            blk = outs[o].at[l, pl.ds(c * h, h)]
            _remote(blk, blk, send_sems.at[a], recv_sems.at[a], (x, y, 1 - c)).start()

    return _pallas(
        body, name=f"grad_share_start_{tag}",
        out_shape=[pltpu.SemaphoreType.DMA((n,)), pltpu.SemaphoreType.DMA((n,))] + [pltpu.HBM(b.shape, b.dtype) for b in bufs],
        in_specs=[HBM] * n_buf, out_specs=[SEM, SEM] + [HBM] * n_buf, input_output_aliases={o: 2 + o for o in range(n_buf)},
        compiler_params=pltpu.CompilerParams(has_side_effects=EFFECT),
    )(*[_in_hbm(b) for b in bufs])


def _share_wait(tag, send, recv, bufs, layout, after):
    n_buf = len(bufs)

    def body(*refs):
        s_ref, r_ref = refs[0], refs[1]
        outs = refs[n_buf + 3:]
        x, y, c, _ = _place()
        for a, (o, l) in enumerate(layout):
            h = outs[o].shape[1] // 2
            mine, theirs = outs[o].at[l, pl.ds(c * h, h)], outs[o].at[l, pl.ds((1 - c) * h, h)]
            _remote(mine, mine, s_ref.at[a], r_ref.at[a], (x, y, 1 - c)).wait_send()
            _remote(theirs, theirs, s_ref.at[a], r_ref.at[a], (x, y, 1 - c)).wait_recv()

    return _pallas(
        body, name=f"grad_share_wait_{tag}", out_shape=[pltpu.HBM(b.shape, b.dtype) for b in bufs],
        in_specs=[SEM, SEM] + [HBM] * n_buf + [ANY], out_specs=[HBM] * n_buf,
        input_output_aliases={2 + o: o for o in range(n_buf)},
        compiler_params=pltpu.CompilerParams(has_side_effects=EFFECT),
    )(send, recv, *bufs, after)


def _small_all_reduce(packed):
    P, L = packed.shape

    def body(in_ref, out_ref, slots, send_sems, recv_sems):
        x, y, c, _ = _place()
        me = 4 * x + 2 * y + c
        slots[me] = in_ref[...]
        cps = []
        for r in range(1, 8):
            px = 1 - x if r & 4 else x
            py = 1 - y if r & 2 else y
            pc = 1 - c if r & 1 else c
            cps.append(_remote(in_ref, slots.at[me], send_sems.at[r - 1], recv_sems.at[r - 1], (px, py, pc)))
        for cp in cps:
            cp.start()
        for r in range(1, 8):
            px = 1 - x if r & 4 else x
            py = 1 - y if r & 2 else y
            pc = 1 - c if r & 1 else c
            blk = slots.at[4 * px + 2 * py + pc]
            _remote(blk, blk, send_sems.at[r - 1], recv_sems.at[r - 1], (px, py, pc)).wait_recv()
        for cp in cps:
            cp.wait_send()
        acc = slots[0]
        for k in range(1, 8):
            acc = acc + slots[k]
        out_ref[...] = acc

    vm = pl.BlockSpec(memory_space=pltpu.VMEM)
    return _pallas(body, name="small_all_reduce", in_specs=[vm], out_specs=vm, out_shape=_sds((P, L), F32),
                   scratch_shapes=[pltpu.VMEM((8, P, L), F32), pltpu.SemaphoreType.DMA((7,)), pltpu.SemaphoreType.DMA((7,))])(packed)


def _adamw_math(w, g, m, v):
    m = ADAM_B1 * m + (1.0 - ADAM_B1) * g
    v = ADAM_B2 * v + (1.0 - ADAM_B2) * (g * g)
    m_hat = m / (1.0 - ADAM_B1 ** ADAM_STEP)
    v_hat = v / (1.0 - ADAM_B2 ** ADAM_STEP)
    delta = -ADAM_LR * (m_hat / (jnp.sqrt(v_hat) + ADAM_EPS) + ADAM_WD * w)
    return delta, m, v


def _adamw(w, g, m, v):
    shape = w.shape
    C = shape[-1]
    rows = math.prod(shape[:-1])
    tr = next(t for t in (512, 352, 256, 128, 64, 32, 16, 8, rows) if rows % t == 0)
    w2, g2, m2, v2 = (a.reshape(rows, C) for a in (w, g, m, v))

    def body(w_ref, g_ref, m_ref, v_ref, go_ref, d_ref, nm_ref, nv_ref):
        gv = g_ref[...]
        d, nm, nv = _adamw_math(w_ref[...], gv, m_ref[...], v_ref[...])
        go_ref[...] = gv
        d_ref[...] = d
        nm_ref[...] = nm
        nv_ref[...] = nv

    blk = pl.BlockSpec((tr, C), lambda i: (i, 0))
    outs = _pallas(body, name="adamw", grid=(rows // tr,), in_specs=[blk] * 4, out_specs=[blk] * 4,
                   out_shape=[_sds((rows, C), F32)] * 4, compiler_params=_cp(("parallel",)))(w2, g2, m2, v2)
    return tuple(o.reshape(shape) for o in outs)


WEIGHTS = ["ffn1_norm", "ffn1_w_gate", "ffn1_w_up", "ffn1_w_down", "mix_norm", "ffn2_norm", "ffn2_w_gate", "ffn2_w_up",
           "ffn2_w_down", "ev_w_in", "ev_b_f", "ev_conv_w", "ev_conv_b", "ev_conv_norm", "ev_q_norm", "ev_k_norm",
           "ev_w_out", "od_w_in", "od_conv_w", "od_w_out"]
BIG = ([("ffn1_w_gate", 0), ("ffn1_w_up", 0), ("ffn1_w_down", 0), ("ev_w_in", 0), ("ev_w_out", 0),
        ("ffn2_w_gate", 0), ("ffn2_w_up", 0), ("ffn2_w_down", 0)]
       + [("ffn1_w_gate", 1), ("ffn1_w_up", 1), ("ffn1_w_down", 1), ("od_w_in", 0), ("od_w_out", 0),
          ("ffn2_w_gate", 1), ("ffn2_w_up", 1), ("ffn2_w_down", 1)])
TRANSPOSED = ("ffn1_w_gate", "ffn1_w_up", "ffn2_w_gate", "ffn2_w_up")
SHARED_LAST = ("ffn1_w_gate", "ffn1_w_up", "ffn1_w_down", "ev_w_in", "ev_w_out")
BLOCKS = [("ffn1", 0), ("ev", 0), ("ffn2", 0), ("ffn1", 1), ("od", 0), ("ffn2", 1)]
BLOCK_OF = {(name, l): (name.split("_w_")[0], l) for name, l in BIG}
BIG_NAMES = ["ffn1_w_gate", "ffn1_w_up", "ffn1_w_down", "ffn2_w_gate", "ffn2_w_up", "ffn2_w_down",
             "ev_w_in", "ev_w_out", "od_w_in", "od_w_out"]
SMALL = [("ffn1_norm", 16), ("mix_norm", 16), ("ffn2_norm", 16), ("ev_b_f", 8), ("ev_conv_w", 128), ("ev_conv_b", 8),
         ("ev_conv_norm", 8), ("ev_q_norm", 8), ("ev_k_norm", 8), ("od_conv_w", 24)]


def _to_lanes(a, rows):
    flat = a.reshape(-1)
    return jnp.pad(flat, (0, rows * 128 - flat.shape[0])).reshape(rows, 128)


def kernel(x, ffn1_norm, ffn1_w_gate, ffn1_w_up, ffn1_w_down, mix_norm, ffn2_norm, ffn2_w_gate, ffn2_w_up, ffn2_w_down, ev_w_in, ev_b_f, ev_conv_w, ev_conv_b, ev_conv_norm, ev_q_norm, ev_k_norm, ev_w_out, od_w_in, od_conv_w, od_w_out, loss_target, m_ffn1_norm, m_ffn1_w_gate, m_ffn1_w_up, m_ffn1_w_down, m_mix_norm, m_ffn2_norm, m_ffn2_w_gate, m_ffn2_w_up, m_ffn2_w_down, m_ev_w_in, m_ev_b_f, m_ev_conv_w, m_ev_conv_b, m_ev_conv_norm, m_ev_q_norm, m_ev_k_norm, m_ev_w_out, m_od_w_in, m_od_conv_w, m_od_w_out, v_ffn1_norm, v_ffn1_w_gate, v_ffn1_w_up, v_ffn1_w_down, v_mix_norm, v_ffn2_norm, v_ffn2_w_gate, v_ffn2_w_up, v_ffn2_w_down, v_ev_w_in, v_ev_b_f, v_ev_conv_w, v_ev_conv_b, v_ev_conv_norm, v_ev_q_norm, v_ev_k_norm, v_ev_w_out, v_od_w_in, v_od_conv_w, v_od_w_out):
    P = dict(ffn1_norm=ffn1_norm, ffn1_w_gate=ffn1_w_gate, ffn1_w_up=ffn1_w_up, ffn1_w_down=ffn1_w_down, mix_norm=mix_norm,
             ffn2_norm=ffn2_norm, ffn2_w_gate=ffn2_w_gate, ffn2_w_up=ffn2_w_up, ffn2_w_down=ffn2_w_down, ev_w_in=ev_w_in,
             ev_b_f=ev_b_f, ev_conv_w=ev_conv_w, ev_conv_b=ev_conv_b, ev_conv_norm=ev_conv_norm, ev_q_norm=ev_q_norm,
             ev_k_norm=ev_k_norm, ev_w_out=ev_w_out, od_w_in=od_w_in, od_conv_w=od_conv_w, od_w_out=od_w_out)
    M = dict(zip(WEIGHTS, [m_ffn1_norm, m_ffn1_w_gate, m_ffn1_w_up, m_ffn1_w_down, m_mix_norm, m_ffn2_norm, m_ffn2_w_gate,
                           m_ffn2_w_up, m_ffn2_w_down, m_ev_w_in, m_ev_b_f, m_ev_conv_w, m_ev_conv_b, m_ev_conv_norm,
                           m_ev_q_norm, m_ev_k_norm, m_ev_w_out, m_od_w_in, m_od_conv_w, m_od_w_out]))
    V = dict(zip(WEIGHTS, [v_ffn1_norm, v_ffn1_w_gate, v_ffn1_w_up, v_ffn1_w_down, v_mix_norm, v_ffn2_norm, v_ffn2_w_gate,
                           v_ffn2_w_up, v_ffn2_w_down, v_ev_w_in, v_ev_b_f, v_ev_conv_w, v_ev_conv_b, v_ev_conv_norm,
                           v_ev_q_norm, v_ev_k_norm, v_ev_w_out, v_od_w_in, v_od_conv_w, v_od_w_out]))
    for name in TRANSPOSED:
        P[name], M[name], V[name] = (jnp.swapaxes(a, 1, 2) for a in (P[name], M[name], V[name]))
    S, D = x.shape[1], x.shape[2]
    chip = 2 * lax.axis_index("x") + lax.axis_index("y")
    core = lax.axis_index("c")

    def own_slot(shard):
        return lax.dynamic_update_slice(lax.empty((4,) + shard.shape, shard.dtype), shard[None], (chip, 0, 0))

    taps = jnp.concatenate([_to_lanes(_pad_rows(ev_conv_w[0], 32), 32), _to_lanes(_pad_rows(od_conv_w[0], 8), 16)], axis=0)
    first = [i for i, k in enumerate(BIG) if BLOCK_OF[k] in BLOCKS[:2]]
    rest = [i for i in range(len(BIG)) if i not in first]
    send0, recv0, *bufs0 = _ag_start("first", [own_slot(P[BIG[i][0]][BIG[i][1]].astype(BF16)) for i in first]
                                     + [own_slot(taps)], True)
    zero = bufs0.pop()[0, 0]
    send1, recv1, *bufs1 = _ag_start("rest", [own_slot((P[BIG[i][0]][BIG[i][1]] + zero).astype(BF16)) for i in rest], False)
    bufs1.pop()
    cols = lambda a: a.transpose(1, 0, 2).reshape(a.shape[1], 4 * a.shape[2])
    W = {k: P[k] for k in ("ffn1_norm", "mix_norm", "ffn2_norm", "ev_b_f", "ev_q_norm", "ev_k_norm")}
    W["ev_conv_b"], W["ev_conv_norm"] = ev_conv_b, ev_conv_norm
    for tag in ("ffn1", "ffn2"):
        for kind in ("_w_gate", "_w_up", "_w_down"):
            W[tag + kind] = [None, None]
    passing = {}

    def pass_on(g, after):
        idx = [i for i, k in enumerate(BIG) if BLOCK_OF[k] == BLOCKS[g]]
        keys = [BIG[i] for i in idx] + (["taps"] if BLOCKS[g] == ("ev", 0) else [])
        send, recv, bufs, members = (send0, recv0, bufs0, first) if g < 2 else (send1, recv1, bufs1, rest)
        local = [members.index(i) for i in idx]
        passing[g] = (keys, _ag_mid(g, send, recv, [bufs[i] for i in local], local,
                                    bufs0[-1] if BLOCKS[g] == ("ev", 0) else None, len(first), after))

    def need(block, after):
        g = BLOCKS.index(block)
        if g not in passing:
            pass_on(g, after)
        keys, (d_send, d_recv, *thru) = passing.pop(g)
        got = dict(zip(keys, _ag_wait(g, d_send, d_recv, thru, len(keys) - ("taps" in keys), after)))
        if 1 <= g < len(BLOCKS) - 1:
            pass_on(g + 1, after)
        for key, a in got.items():
            if key == "taps":
                continue
            name, l = key
            if name.startswith("ffn"):
                W[name][l] = a
            elif name.endswith("_w_in"):
                W[name] = cols(a)
            elif name.endswith("_w_out"):
                W[name] = a.reshape(4 * a.shape[1], D)
        if block == ("ev", 0):
            taps_all = got["taps"]
            W["ev_conv_w"] = cols(taps_all[:, :32].reshape(4, 32, 128))[:CONV_A_WIDTH]
            W["od_conv_w"] = cols(taps_all[:, 32:48].reshape(4, 8, 256))[:CONV_C_WIDTH]

    rows = lambda a: a.reshape(4, a.shape[0] // 4, a.shape[1])
    colsh = lambda a: a.reshape(a.shape[0], 4, a.shape[1] // 4).transpose(1, 0, 2)
    c_arr = core.reshape(1).astype(jnp.int32)
    where = jnp.stack([chip, core]).astype(jnp.int32)
    in_flight = []

    def done(block, block_grads):
        g = BLOCKS.index(block)
        keys = list(block_grads)
        gs = []
        for name, l in keys:
            a = block_grads[(name, l)]
            gs.append(colsh(a) if name == "ev_w_in" else rows(a) if name.endswith("_w_out") else a)
        for item in list(pairs):
            to_chips(item)
        if g == 0:
            land("early", in_flight[:-1], chained["token"])
            chained["token"] = share("a", [n for n in BIG_NAMES if n not in SHARED_LAST])
        send, recv, *rest = _pair_start(g, gs, chained.get("token"))
        n = len(keys)
        pairs.append((g, keys, send, recv, rest[:n], rest[n:2 * n]))
        if g == 0:
            to_chips(pairs[0])
        chained["token"] = rest[-1] if g else chained["token"]
        return chained["token"][0:1, 0:1]

    pairs, chained = [], {}
    stacked, shares, landed_blocks = {}, [], []

    def land(tag, flights, after):
        order = [k for keys, *_ in flights for k in keys]
        landed = _chip_wait(tag, [f[1] for f in flights], [f[2] for f in flights], [len(f[0]) for f in flights],
                            [a for f in flights for a in f[3]], [a for f in flights for a in f[4]], after)
        landed_blocks.extend(flights)
        for (name, l), s, r in zip(order, landed[:len(order)], landed[len(order):]):
            stacked[name] = _chip_sum(s, r, where, stacked.get(name), l, P[name].shape[0])

    def share(tag, names):
        layout = [(o, l) for o, name in enumerate(names) for l in range(P[name].shape[0])]
        send, recv, *thru = _share_start(tag, [stacked[name] for name in names], layout)
        shares.append((tag, names, send, recv, thru, layout))
        return thru[0]

    def to_chips(item):
        pairs.remove(item)
        g, keys, send, recv, gs, zones = item
        n = len(keys)
        done_ = _pair_wait(g, send, recv, gs, zones)
        sums = list(_pair_add(list(done_[:n]), list(done_[n:]), c_arr))
        send2, recv2, *rest = _chip_start(g, sums)
        in_flight.append((keys, send2, recv2, rest[:n], rest[n:2 * n]))
        chained["token"] = rest[-1]

    loss, grad_x, grads = _local_step(x[0], loss_target[0], W, need, done)

    land("late", [f for f in in_flight if not any(f is b for b in landed_blocks)], grad_x)
    share("b", list(SHARED_LAST))

    def small_grad(name):
        if name.endswith("_norm") and name[:3] in ("ffn", "mix"):
            return jnp.concatenate([grads[(name, 0)], grads[(name, 1)]], axis=0)
        return grads[(name, 0)]

    packed = jnp.concatenate([_to_lanes(small_grad(name), r) for name, r in SMALL], axis=0)
    total = _small_all_reduce(packed)
    small_grads, at = {}, 0
    for name, r in SMALL:
        part = total[at:at + r].reshape(-1)
        at += r
        if name == "ev_conv_w":
            full_g = part[:CONV_A_WIDTH * D_CONV].reshape(CONV_A_WIDTH, D_CONV)
            small_grads[name] = lax.dynamic_slice_in_dim(full_g, chip * (D_CONV // 4), D_CONV // 4, axis=1)[None]
        elif name == "od_conv_w":
            full_g = part[:CONV_C_WIDTH * D].reshape(CONV_C_WIDTH, D)
            small_grads[name] = lax.dynamic_slice_in_dim(full_g, chip * (D // 4), D // 4, axis=1)[None]
        else:
            small_grads[name] = part[:math.prod(P[name].shape)].reshape(P[name].shape)

    results = {}

    def update(name, g):
        outs = _adamw(P[name], g, M[name], V[name])
        results[name] = tuple(jnp.swapaxes(a, 1, 2) for a in outs) if name in TRANSPOSED else outs

    for name, _ in SMALL:
        update(name, small_grads[name])
    after = shares[-1][4][0]
    for tag, names, send, recv, thru, layout in shares:
        for name, g in zip(names, _share_wait(tag, send, recv, thru, layout, after)):
            update(name, g)
        after = results[names[-1]][1]
    loss_all = lax.psum(loss[0, 0], ("x", "y", "c"))
    return (loss_all, grad_x[None], *[results[name][k] for k in range(4) for name in WEIGHTS])
```

```python
import functools
import math

import jax
import jax.numpy as jnp
from jax import lax
from jax.experimental import pallas as pl
from jax.experimental.pallas import tpu as pltpu

F32, BF16 = jnp.float32, jnp.bfloat16
EPS = 1e-6
FFN_RES = 0.5
N_HEADS, HEAD_DIM = 8, 64
D_CONV = 512
D_ATTN = N_HEADS * HEAD_DIM
CONV_A_WIDTH, CONV_C_WIDTH = 31, 3
ADAM_LR, ADAM_B1, ADAM_B2, ADAM_EPS, ADAM_WD, ADAM_STEP = 0.001, 0.9, 0.999, 1e-08, 0.01, 10
MESH = pl.DeviceIdType.MESH
ANY = pl.BlockSpec(memory_space=pl.ANY)

TOK_TILE = 512
FFN_TILE = 1024
DW_TILE = 1024
ATT_TILE = 1024
QKN_TILE = 2048
HALO_A, HALO_C = 32, 16
SUBLANES = 8
CONV_ROWS = 64
SCAN_BLK = 256
MIB = 2 ** 20


def _pallas(body, **kw):
    return pl.pallas_call(body, **kw)


def _cp(sem=None, vmem_mib=48):
    return pltpu.CompilerParams(dimension_semantics=sem, vmem_limit_bytes=vmem_mib * MIB)


def _dot(a, b):
    return jnp.dot(a, b, preferred_element_type=F32)


def _dot_nt(a, b):
    return lax.dot_general(a, b, (((1,), (1,)), ((), ())), preferred_element_type=F32)


def _dot_tn(a, b):
    return lax.dot_general(a, b, (((0,), (0,)), ((), ())), preferred_element_type=F32)


def _sds(shape, dtype):
    return jax.ShapeDtypeStruct(shape, dtype)


def _rms(x):
    return lax.rsqrt(jnp.mean(x * x, axis=-1, keepdims=True) + EPS)


def _rms_bwd(dy, x, g):
    r = _rms(x)
    xh = x * r
    dxh = dy * g
    dx = r * (dxh - xh * jnp.mean(dxh * xh, axis=-1, keepdims=True))
    return dx, xh


def _silu_grad(z):
    s = jax.nn.sigmoid(z)
    return s * (1.0 + z * (1.0 - s))


def _ffn_fwd(x, g, wg, wu, wd):
    S, D = x.shape
    nc, Fs, _ = wd.shape
    tm = min(FFN_TILE, S)

    def body(x_ref, g_ref, wg_ref, wu_ref, wd_ref, out_ref, xn_ref, G_ref, U_ref, acc_ref):
        j = pl.program_id(1)

        @pl.when(j == 0)
        def _():
            xv = x_ref[...]
            xn_ref[...] = (xv * _rms(xv) * g_ref[...]).astype(BF16)
            acc_ref[...] = jnp.zeros_like(acc_ref)

        xn = xn_ref[...]
        G = _dot_nt(xn, wg_ref[0])
        U = _dot_nt(xn, wu_ref[0])
        G_ref[0] = G.astype(BF16)
        U_ref[0] = U.astype(BF16)
        H = (G * jax.nn.sigmoid(G) * U).astype(BF16)
        acc_ref[...] += _dot(H, wd_ref[0])

        @pl.when(j == nc - 1)
        def _():
            out_ref[...] = x_ref[...] + FFN_RES * acc_ref[...]

    row = pl.BlockSpec((tm, D), lambda i, j: (i, 0))
    return _pallas(
        body, name="ffn_fwd", grid=(S // tm, nc),
        in_specs=[row, pl.BlockSpec((1, D), lambda i, j: (0, 0)),
                  pl.BlockSpec((1, Fs, D), lambda i, j: (j, 0, 0)), pl.BlockSpec((1, Fs, D), lambda i, j: (j, 0, 0)),
                  pl.BlockSpec((1, Fs, D), lambda i, j: (j, 0, 0))],
        out_specs=[row, row, pl.BlockSpec((1, tm, Fs), lambda i, j: (j, i, 0)),
                   pl.BlockSpec((1, tm, Fs), lambda i, j: (j, i, 0))],
        out_shape=[_sds((S, D), F32), _sds((S, D), BF16), _sds((nc, S, Fs), BF16), _sds((nc, S, Fs), BF16)],
        scratch_shapes=[pltpu.VMEM((tm, D), F32)],
        compiler_params=_cp(("parallel", "arbitrary"), 56),
    )(x, g, wg, wu, wd)


def _ffn_bwd_w(dout, xn, G, U, wd):
    S, D = dout.shape
    nc, _, Fs = G.shape
    tm = min(DW_TILE, S)
    nt = S // tm
    sub = min(TOK_TILE, tm)

    def body(do_ref, xn_ref, G_ref, U_ref, wd_ref, dwg_ref, dwu_ref, dwd_ref, dG_ref, dU_ref, ag, au, ad, do_s, H_s):
        i = pl.program_id(1)

        @pl.when(i == 0)
        def _():
            ag[...] = jnp.zeros_like(ag)
            au[...] = jnp.zeros_like(au)
            ad[...] = jnp.zeros_like(ad)

        for r in range(0, tm, sub):
            rows = pl.ds(r, sub)
            do = (FFN_RES * do_ref[rows, :]).astype(BF16)
            do_s[rows, :] = do
            Gv = G_ref[0, rows, :].astype(F32)
            Uv = U_ref[0, rows, :].astype(F32)
            dH = _dot_nt(do, wd_ref[0])
            sg = jax.nn.sigmoid(Gv)
            act = Gv * sg
            H_s[rows, :] = (act * Uv).astype(BF16)
            dU_ref[0, rows, :] = (dH * act).astype(BF16)
            dG_ref[0, rows, :] = (dH * Uv * (sg * (1.0 + Gv * (1.0 - sg)))).astype(BF16)
        xnv = xn_ref[...]
        ag[...] += _dot_tn(dG_ref[0], xnv)
        au[...] += _dot_tn(dU_ref[0], xnv)
        ad[...] += _dot_tn(H_s[...], do_s[...])

        @pl.when(i == nt - 1)
        def _():
            dwg_ref[0] = ag[...].astype(BF16)
            dwu_ref[0] = au[...].astype(BF16)
            dwd_ref[0] = ad[...].astype(BF16)

    row = pl.BlockSpec((tm, D), lambda j, i: (i, 0))
    hid = pl.BlockSpec((1, tm, Fs), lambda j, i: (j, i, 0))
    wrow = pl.BlockSpec((1, Fs, D), lambda j, i: (j, 0, 0))
    return _pallas(
        body, name="ffn_bwd_w", grid=(nc, nt),
        in_specs=[row, row, hid, hid, wrow],
        out_specs=[wrow, wrow, wrow, hid, hid],
        out_shape=[_sds((nc, Fs, D), BF16)] * 3 + [_sds((nc, S, Fs), BF16)] * 2,
        scratch_shapes=[pltpu.VMEM((Fs, D), F32)] * 3 + [pltpu.VMEM((tm, D), BF16), pltpu.VMEM((tm, Fs), BF16)],
        compiler_params=_cp(("parallel", "arbitrary"), 56),
    )(dout, xn, G, U, wd)


def _norm_in_bwd(dzs, ws, x, g, dres, w_rows=False):
    S, D = x.shape
    nc = dzs[0].shape[0]
    n = len(dzs)
    tm = TOK_TILE

    def body(*refs):
        dz_refs, w_refs = refs[:n], refs[n:2 * n]
        x_ref, g_ref, dres_ref, dx_ref, dg_ref, acc_ref = refs[2 * n:]
        i, j = pl.program_id(0), pl.program_id(1)

        @pl.when(j == 0)
        def _():
            acc_ref[...] = jnp.zeros_like(acc_ref)

        @pl.when((i == 0) & (j == 0))
        def _():
            dg_ref[...] = jnp.zeros_like(dg_ref)

        for dz_ref, w_ref in zip(dz_refs, w_refs):
            acc_ref[...] += _dot(dz_ref[0], w_ref[0]) if w_rows else _dot_nt(dz_ref[0], w_ref[0])

        @pl.when(j == nc - 1)
        def _():
            dxn = acc_ref[...]
            dx, xh = _rms_bwd(dxn, x_ref[...], g_ref[...])
            dx_ref[...] = dx + dres_ref[...]
            dg_ref[...] += jnp.sum(dxn * xh, axis=0, keepdims=True)

    row = pl.BlockSpec((tm, D), lambda i, j: (i, 0))
    one = pl.BlockSpec((1, D), lambda i, j: (0, 0))
    in_specs = [pl.BlockSpec((1, tm, dz.shape[2]), lambda i, j: (j, i, 0)) for dz in dzs]
    in_specs += [pl.BlockSpec((1,) + w.shape[1:], lambda i, j: (j, 0, 0)) for w in ws]
    return _pallas(
        body, name="norm_in_bwd", grid=(S // tm, nc),
        in_specs=in_specs + [row, one, row], out_specs=[row, one],
        out_shape=[_sds((S, D), F32), _sds((1, D), F32)],
        scratch_shapes=[pltpu.VMEM((tm, D), F32)],
        compiler_params=_cp(("arbitrary", "arbitrary")),
    )(*dzs, *ws, x, g, dres)


def _norm_proj(x, g, w, w2=None):
    S, D = x.shape
    N = w.shape[1]
    tm = TOK_TILE

    def body(*refs):
        if w2 is None:
            x_ref, g_ref, w_ref, h_ref, z_ref = refs
        else:
            x_ref, g_ref, w_ref, w2_ref, h_ref, z_ref, z2_ref = refs
        xv = x_ref[...]
        h = (xv * _rms(xv) * g_ref[...]).astype(BF16)
        h_ref[...] = h
        z_ref[...] = _dot(h, w_ref[...]).astype(BF16)
        if w2 is not None:
            z2_ref[...] = _dot(h, w2_ref[...])

    row = pl.BlockSpec((tm, D), lambda i: (i, 0))
    in_specs = [row, pl.BlockSpec((1, D), lambda i: (0, 0)), pl.BlockSpec((D, N), lambda i: (0, 0))]
    out_specs = [row, pl.BlockSpec((tm, N), lambda i: (i, 0))]
    out_shape = [_sds((S, D), BF16), _sds((S, N), BF16)]
    args = [x, g, w]
    if w2 is not None:
        N2 = w2.shape[1]
        in_specs.append(pl.BlockSpec((D, N2), lambda i: (0, 0)))
        out_specs.append(pl.BlockSpec((tm, N2), lambda i: (i, 0)))
        out_shape.append(_sds((S, N2), F32))
        args.append(w2)
    return _pallas(body, name="norm_proj", grid=(S // tm,), in_specs=in_specs, out_specs=out_specs,
                   out_shape=out_shape, compiler_params=_cp(("parallel",)))(*args)


def _proj_res(acts, ws, res):
    S, D = res.shape
    n = len(acts)
    tm = TOK_TILE

    def body(*refs):
        a_refs, w_refs = refs[:n], refs[n:2 * n]
        res_ref, out_ref = refs[2 * n:]
        acc = res_ref[...]
        for a_ref, w_ref in zip(a_refs, w_refs):
            acc = acc + _dot(a_ref[...], w_ref[...])
        out_ref[...] = acc

    row = pl.BlockSpec((tm, D), lambda i: (i, 0))
    in_specs = [pl.BlockSpec((tm, a.shape[1]), lambda i: (i, 0)) for a in acts]
    in_specs += [pl.BlockSpec(w.shape, lambda i: (0, 0)) for w in ws]
    return _pallas(body, name="proj_res", grid=(S // tm,), in_specs=in_specs + [row], out_specs=row,
                   out_shape=_sds((S, D), F32), compiler_params=_cp(("parallel",)))(*acts, *ws, res)


def _matmul_nt(a, w, after=None):
    S, K = a.shape
    M = w.shape[0]
    tm = TOK_TILE

    def body(a_ref, w_ref, *rest):
        rest[-1][...] = _dot_nt(a_ref[...].astype(BF16), w_ref[...])

    extra = [] if after is None else [after]
    return _pallas(body, name="matmul_nt", grid=(S // tm,),
                   in_specs=[pl.BlockSpec((tm, K), lambda i: (i, 0)), pl.BlockSpec((M, K), lambda i: (0, 0))] + [ANY] * len(extra),
                   out_specs=pl.BlockSpec((tm, M), lambda i: (i, 0)), out_shape=_sds((S, M), F32),
                   compiler_params=_cp(("parallel",)))(a, w, *extra)


def _matmul_tn(a, b, tn):
    S, M = a.shape
    N = b.shape[1]
    tm = min(DW_TILE, S)
    nt = S // tm

    def body(a_ref, b_ref, o_ref, acc_ref):
        i = pl.program_id(1)

        @pl.when(i == 0)
        def _():
            acc_ref[...] = jnp.zeros_like(acc_ref)

        acc_ref[...] += _dot_tn(a_ref[...].astype(BF16), b_ref[...].astype(BF16))

        @pl.when(i == nt - 1)
        def _():
            o_ref[0] = acc_ref[...].astype(BF16)

    return _pallas(body, name="matmul_tn", grid=(N // tn, nt),
                   in_specs=[pl.BlockSpec((tm, M), lambda j, i: (i, 0)), pl.BlockSpec((tm, tn), lambda j, i: (i, j))],
                   out_specs=pl.BlockSpec((1, M, tn), lambda j, i: (j, 0, 0)), out_shape=_sds((N // tn, M, tn), BF16),
                   scratch_shapes=[pltpu.VMEM((M, tn), F32)],
                   compiler_params=_cp(("parallel", "arbitrary")))(a, b)


def _fill_shifts(win, rows):
    for b in range(1, SUBLANES):
        win[b, pl.ds(0, rows - SUBLANES), :] = win[0, pl.ds(b, rows - SUBLANES), :]


def _tap(win, offset, n, base=0):
    start = base + (offset - offset % SUBLANES)
    if not isinstance(start, int):
        start = pl.multiple_of(start, SUBLANES)
    return win[offset % SUBLANES, pl.ds(start, n), :]


def _conv_a_fwd(z, cw, cb, cn):
    S = z.shape[0]
    C = D_CONV
    tm = TOK_TILE
    hb = tm // HALO_A

    def body(u_ref, gt_ref, up_ref, gp_ref, cw_ref, cb_ref, cn_ref, a_ref, a1_ref, win):
        i = pl.program_id(0)
        prev = up_ref[...].astype(F32) * jax.nn.sigmoid(gp_ref[...].astype(F32))
        win[0, pl.ds(0, HALO_A), :] = jnp.where(i == 0, 0.0, prev)
        win[0, pl.ds(HALO_A, tm), :] = u_ref[...].astype(F32) * jax.nn.sigmoid(gt_ref[...].astype(F32))
        _fill_shifts(win, tm + HALO_A)

        acc = jnp.zeros((tm, C), F32)
        for k in range(CONV_A_WIDTH):
            acc = acc + cw_ref[k:k + 1, :] * _tap(win, HALO_A - (CONV_A_WIDTH - 1) + k, tm)
        a1 = acc + cb_ref[...]
        a1_ref[...] = a1
        a2 = a1 * _rms(a1) * cn_ref[...]
        a_ref[...] = (a2 * jax.nn.sigmoid(a2)).astype(BF16)

    cur = lambda c: pl.BlockSpec((tm, C), lambda i, c=c: (i, c))
    prv = lambda c: pl.BlockSpec((HALO_A, C), lambda i, c=c: (jnp.maximum(i * hb - 1, 0), c))
    vec = pl.BlockSpec((1, C), lambda i: (0, 0))
    return _pallas(body, name="conv_a_fwd", grid=(S // tm,),
                   in_specs=[cur(0), cur(1), prv(0), prv(1), pl.BlockSpec((32, C), lambda i: (0, 0)), vec, vec],
                   out_specs=[pl.BlockSpec((tm, C), lambda i: (i, 0)), pl.BlockSpec((tm, C), lambda i: (i, 0))],
                   out_shape=[_sds((S, C), BF16), _sds((S, C), F32)],
                   scratch_shapes=[pltpu.VMEM((SUBLANES, tm + HALO_A, C), F32)],
                   compiler_params=_cp(("parallel",)))(z, z, z, z, cw, cb, cn)


def _conv_a_bwd(da, a1, z, cw, cn):
    S = z.shape[0]
    C = D_CONV
    tm = TOK_TILE
    hb = tm // HALO_A
    nt = S // tm
    W = CONV_A_WIDTH

    def body(da_ref, a1_ref, dan_ref, a1n_ref, u_ref, gt_ref, up_ref, gp_ref, cw_ref, cn_ref,
             duz_ref, dcw_ref, dcb_ref, dcn_ref, win, dwin):
        i = pl.program_id(0)

        @pl.when(i == 0)
        def _():
            dcw_ref[...] = jnp.zeros_like(dcw_ref)
            dcb_ref[...] = jnp.zeros_like(dcb_ref)
            dcn_ref[...] = jnp.zeros_like(dcn_ref)

        cnv = cn_ref[...]

        def da1_of(dav, a1v):
            a2 = a1v * _rms(a1v) * cnv
            da2 = dav * _silu_grad(a2)
            dx, xh = _rms_bwd(da2, a1v, cnv)
            return dx, da2 * xh

        da1, dcn_t = da1_of(da_ref[...], a1_ref[...])
        da1n, _ = da1_of(dan_ref[...], a1n_ref[...])
        dwin[0, pl.ds(0, tm), :] = da1
        dwin[0, pl.ds(tm, HALO_A), :] = jnp.where(i == nt - 1, 0.0, da1n)
        _fill_shifts(dwin, tm + HALO_A)
        dcb_ref[...] += jnp.sum(da1, axis=0, keepdims=True)
        dcn_ref[...] += jnp.sum(dcn_t, axis=0, keepdims=True)

        prev = up_ref[...].astype(F32) * jax.nn.sigmoid(gp_ref[...].astype(F32))
        win[0, pl.ds(0, HALO_A), :] = jnp.where(i == 0, 0.0, prev)
        win[0, pl.ds(HALO_A, tm), :] = u_ref[...].astype(F32) * jax.nn.sigmoid(gt_ref[...].astype(F32))
        _fill_shifts(win, tm + HALO_A)

        def rows_block(rb, carry):
            r0 = pl.multiple_of(rb * CONV_ROWS, CONV_ROWS)
            rows = pl.ds(r0, CONV_ROWS)
            da1_b = dwin[0, rows, :]
            da0 = jnp.zeros((CONV_ROWS, C), F32)
            for k in range(W):
                da0 = da0 + cw_ref[k:k + 1, :] * _tap(dwin, W - 1 - k, CONV_ROWS, r0)
                dcw_ref[k:k + 1, :] += jnp.sum(da1_b * _tap(win, HALO_A - (W - 1) + k, CONV_ROWS, r0), axis=0, keepdims=True)
            u = u_ref[rows, :].astype(F32)
            sg = jax.nn.sigmoid(gt_ref[rows, :].astype(F32))
            duz_ref[rows, 0:C] = (da0 * sg).astype(BF16)
            duz_ref[rows, C:2 * C] = (da0 * u * sg * (1.0 - sg)).astype(BF16)
            return carry

        lax.fori_loop(0, tm // CONV_ROWS, rows_block, 0)

    cur = lambda c: pl.BlockSpec((tm, C), lambda i, c=c: (i, c))
    prv = lambda c: pl.BlockSpec((HALO_A, C), lambda i, c=c: (jnp.maximum(i * hb - 1, 0), c))
    nxt = pl.BlockSpec((HALO_A, C), lambda i: (jnp.minimum((i + 1) * hb, S // HALO_A - 1), 0))
    vec = pl.BlockSpec((1, C), lambda i: (0, 0))
    return _pallas(body, name="conv_a_bwd", grid=(nt,),
                   in_specs=[cur(0), cur(0), nxt, nxt, cur(0), cur(1), prv(0), prv(1),
                             pl.BlockSpec((32, C), lambda i: (0, 0)), vec],
                   out_specs=[pl.BlockSpec((tm, 2 * C), lambda i: (i, 0)), pl.BlockSpec((32, C), lambda i: (0, 0)), vec, vec],
                   out_shape=[_sds((S, 2 * C), BF16), _sds((32, C), F32), _sds((1, C), F32), _sds((1, C), F32)],
                   scratch_shapes=[pltpu.VMEM((SUBLANES, tm + HALO_A, C), F32)] * 2,
                   compiler_params=_cp(("arbitrary",)))(da, a1, da, a1, z, z, z, z, cw, cn)


def _forget_scan(fl, bf):
    S, L = fl.shape
    B = SCAN_BLK

    def body(fl_ref, bf_ref, flb_ref, F_ref):
        tri = (lax.broadcasted_iota(jnp.int32, (B, B), 0) >= lax.broadcasted_iota(jnp.int32, (B, B), 1)).astype(F32)

        def step(c, carry):
            rows = pl.ds(pl.multiple_of(c * B, B), B)
            v = fl_ref[rows, :] + bf_ref[...]
            flb_ref[rows, :] = v
            lf = jnp.minimum(v, 0.0) - jnp.log1p(jnp.exp(-jnp.abs(v)))
            cs = jnp.dot(tri, lf, precision=lax.Precision.HIGHEST, preferred_element_type=F32) + carry
            F_ref[rows, :] = cs
            return cs[B - 1:B, :]

        lax.fori_loop(0, S // B, step, jnp.zeros((1, L), F32))

    return _pallas(body, name="forget_scan", out_shape=[_sds((S, L), F32), _sds((S, L), F32)],
                   compiler_params=_cp())(fl, bf)


def _forget_scan_bwd(dF, flb):
    S, L = dF.shape
    B = SCAN_BLK
    nb = S // B

    def body(dF_ref, flb_ref, dfl_ref, db_ref):
        tri = (lax.broadcasted_iota(jnp.int32, (B, B), 0) <= lax.broadcasted_iota(jnp.int32, (B, B), 1)).astype(F32)

        def step(t, carry):
            carry_cs, db = carry
            rows = pl.ds(pl.multiple_of((nb - 1 - t) * B, B), B)
            cs = jnp.dot(tri, dF_ref[rows, :], precision=lax.Precision.HIGHEST, preferred_element_type=F32) + carry_cs
            dfl = cs * jax.nn.sigmoid(-flb_ref[rows, :])
            dfl_ref[rows, :] = dfl
            return cs[0:1, :], db + jnp.sum(dfl, axis=0, keepdims=True)

        _, db = lax.fori_loop(0, nb, step, (jnp.zeros((1, L), F32), jnp.zeros((1, L), F32)))
        db_ref[...] = db

    return _pallas(body, name="forget_scan_bwd", out_shape=[_sds((S, L), F32), _sds((1, L), F32)],
                   compiler_params=_cp())(dF, flb)


NEG = -1e30


def _causal_mask(t):
    return lax.broadcasted_iota(jnp.int32, (t, t), 0) >= lax.broadcasted_iota(jnp.int32, (t, t), 1)


AUG = 128
C_F, C_ONE, C_LSE = 64, 67, 70


def _split3(f):
    a = f.astype(BF16).astype(F32)
    r = f - a
    b = r.astype(BF16).astype(F32)
    return a, b, r - b


def _put3(lane, base, parts, other):
    out = other
    for k, p in enumerate(parts):
        out = jnp.where(lane == base + k, p, out)
    return out


def _ones3(lane, base):
    return (lane >= base) & (lane < base + 3)


def _lane_ids(rows):
    return lax.broadcasted_iota(jnp.int32, (rows, AUG), 1)


def _pair_rms(x, lo):
    sq = x * x
    ms_a = jnp.sum(jnp.where(lo, sq, 0.0), axis=-1, keepdims=True) * (1.0 / HEAD_DIM)
    ms_b = jnp.sum(jnp.where(lo, 0.0, sq), axis=-1, keepdims=True) * (1.0 / HEAD_DIM)
    return jnp.where(lo, lax.rsqrt(ms_a + EPS), lax.rsqrt(ms_b + EPS))


def _qkv_prep(z, Fc, qw, kw):
    S = z.shape[0]
    tp = min(QKN_TILE, S)
    scale = 1.0 / math.sqrt(HEAD_DIM)

    def body(zq_ref, zk_ref, zv_ref, F_ref, qw_ref, kw_ref, q_ref, k_ref, v_ref):
        j = pl.program_id(0)
        lane = _lane_ids(tp)
        lo = lane < HEAD_DIM
        Fv = F_ref[...]
        xq = zq_ref[...].astype(F32)
        xk = zk_ref[...].astype(F32)
        qn = xq * _pair_rms(xq, lo) * qw_ref[...] * scale
        kn = xk * _pair_rms(xk, lo) * kw_ref[...]
        vv = zv_ref[...].astype(F32)
        for half in range(2):
            take = (lambda a: a) if half == 0 else (lambda a: pltpu.roll(a, HEAD_DIM, 1))
            fp = _split3(jnp.sum(jnp.where(lane == 2 * j + half, Fv, 0.0), axis=-1, keepdims=True))
            qx = _put3(lane, C_F, fp, jnp.where(_ones3(lane, C_ONE), 1.0, 0.0))
            kx = _put3(lane, C_ONE, [-p for p in fp], jnp.where(_ones3(lane, C_F) | _ones3(lane, C_LSE), 1.0, 0.0))
            vx = jnp.where(_ones3(lane, C_F), 1.0, 0.0)
            q_ref[half] = jnp.where(lo, take(qn), qx).astype(BF16)
            k_ref[half] = jnp.where(lo, take(kn), kx).astype(BF16)
            v_ref[half] = jnp.where(lo, take(vv), vx).astype(BF16)

    col = lambda c0: pl.BlockSpec((tp, AUG), lambda j, i, c0=c0: (i, c0 + j))
    vec = pl.BlockSpec((1, AUG), lambda j, i: (0, 0))
    out = pl.BlockSpec((2, tp, AUG), lambda j, i: (j, i, 0))
    return _pallas(body, name="qkv_prep", grid=(N_HEADS // 2, S // tp),
                   in_specs=[col(8), col(12), col(16), pl.BlockSpec((tp, AUG), lambda j, i: (i, 0)), vec, vec],
                   out_specs=[out, out, out], out_shape=[_sds((N_HEADS, S, AUG), BF16)] * 3,
                   compiler_params=_cp(("parallel", "parallel")))(z, z, z, Fc, qw, kw)


def _fox_fwd(q_aug, k_aug, v_aug):
    H, S, A = q_aug.shape
    t = ATT_TILE
    nq = S // t

    def body(q_ref, k_ref, v_ref, o_ref, q2_ref):
        i = pl.program_id(1)
        q = q_ref[0]

        def tile(j, carry, diag):
            m, acc = carry
            rows = pl.ds(pl.multiple_of(j * t, t), t)
            s = _dot_nt(q, k_ref[0, rows, :])
            if diag:
                s = jnp.where(_causal_mask(t), s, NEG)
            m_new = jnp.maximum(m, jnp.max(s, axis=-1, keepdims=True))
            p = jnp.exp(s - m_new)
            acc = jnp.exp(m - m_new) * acc + _dot(p.astype(BF16), v_ref[0, rows, :])
            return m_new, acc

        init = (jnp.full((t, 1), NEG, F32), jnp.zeros((t, A), F32))
        carry = lax.fori_loop(0, i, lambda j, c: tile(j, c, False), init)
        m, acc = tile(i, carry, True)
        lane = _lane_ids(t)
        l = jnp.sum(jnp.where(lane == C_F, acc, 0.0), axis=-1, keepdims=True)
        o_ref[0] = (acc / l).astype(BF16)
        lse = m + jnp.log(l)
        q2_ref[0] = (q.astype(F32) + _put3(lane, C_LSE, [-p for p in _split3(lse)], 0.0)).astype(BF16)

    qblk = pl.BlockSpec((1, t, A), lambda h, i: (h, i, 0))
    full = pl.BlockSpec((1, S, A), lambda h, i: (h, 0, 0))
    return _pallas(body, name="fox_fwd", grid=(H, nq), in_specs=[qblk, full, full], out_specs=[qblk, qblk],
                   out_shape=[_sds((H, S, A), BF16)] * 2, compiler_params=_cp(("parallel", "parallel")))(q_aug, k_aug, v_aug)


def _do_prep(dcat, o_aug):
    S = dcat.shape[0]
    tp = min(QKN_TILE, S)

    def body(d_ref, o_ref, out_ref):
        lane = _lane_ids(tp)
        lo = lane < HEAD_DIM
        x = d_ref[...]
        for half in range(2):
            d = jnp.where(lo, x if half == 0 else pltpu.roll(x, HEAD_DIM, 1), 0.0)
            delta = jnp.sum(d * o_ref[half].astype(F32), axis=-1, keepdims=True)
            out_ref[half] = jnp.where(lo, d, _put3(lane, C_F, [-p for p in _split3(delta)], 0.0)).astype(BF16)

    pair = pl.BlockSpec((2, tp, AUG), lambda j, i: (j, i, 0))
    return _pallas(body, name="do_prep", grid=(N_HEADS // 2, S // tp),
                   in_specs=[pl.BlockSpec((tp, AUG), lambda j, i: (i, D_CONV // AUG + j)), pair], out_specs=pair,
                   out_shape=_sds((N_HEADS, S, AUG), BF16), compiler_params=_cp(("parallel", "parallel")))(dcat, o_aug)


def _fox_bwd(q2, k_aug, v_aug, do_aug):
    H, S, A = q2.shape
    t = ATT_TILE
    nq = S // t

    def body(q_ref, k_ref, v_ref, do_ref, dq_ref, dk_ref, dv_ref):
        j = pl.program_id(1)

        @pl.when(j == 0)
        def _():
            dq_ref[...] = jnp.zeros_like(dq_ref)

        k = k_ref[0]
        vv = v_ref[0]

        def tile(i, carry, diag):
            dk, dv = carry
            rows = pl.ds(pl.multiple_of(i * t, t), t)
            q = q_ref[0, rows, :]
            dov = do_ref[0, rows, :]
            s = _dot_nt(q, k)
            if diag:
                s = jnp.where(_causal_mask(t), s, NEG)
            p = jnp.exp(s)
            dv = dv + _dot_tn(p.astype(BF16), dov)
            dsb = (p * _dot_nt(dov, vv)).astype(BF16)
            dq_ref[0, rows, :] += _dot(dsb, k)
            dk = dk + _dot_tn(dsb, q)
            return dk, dv

        init = (jnp.zeros((t, A), F32), jnp.zeros((t, A), F32))
        carry = tile(j, init, True)
        dk, dv = lax.fori_loop(j + 1, nq, lambda i, c: tile(i, c, False), carry)
        dk_ref[0] = dk
        dv_ref[0] = dv

    full = pl.BlockSpec((1, S, A), lambda h, j: (h, 0, 0))
    kblk = pl.BlockSpec((1, t, A), lambda h, j: (h, j, 0))
    return _pallas(body, name="fox_bwd", grid=(H, nq), in_specs=[full, kblk, kblk, full], out_specs=[full, kblk, kblk],
                   out_shape=[_sds((H, S, A), F32)] * 3,
                   compiler_params=_cp(("parallel", "arbitrary")))(q2, k_aug, v_aug, do_aug)


def _qkv_bwd(dq, dk, dv, z, qw, kw):
    S = z.shape[0]
    tp = min(QKN_TILE, S)
    scale = 1.0 / math.sqrt(HEAD_DIM)

    def body(dq_ref, dk_ref, dv_ref, zq_ref, zk_ref, qw_ref, kw_ref, dqf_ref, dkf_ref, dvf_ref, dF_ref, dqw_ref, dkw_ref):
        i, j = pl.program_id(0), pl.program_id(1)
        lane = _lane_ids(tp)
        lo = lane < HEAD_DIM

        @pl.when((i == 0) & (j == 0))
        def _():
            dqw_ref[...] = jnp.zeros_like(dqw_ref)
            dkw_ref[...] = jnp.zeros_like(dkw_ref)

        def pair(ref):
            return jnp.where(lo, ref[0], pltpu.roll(ref[1], HEAD_DIM, 1))

        def norm_bwd(g, x, w):
            r = _pair_rms(x, lo)
            xh = x * r
            dxh = g * w
            tt = dxh * xh
            mean_a = jnp.sum(jnp.where(lo, tt, 0.0), axis=-1, keepdims=True) * (1.0 / HEAD_DIM)
            mean_b = jnp.sum(jnp.where(lo, 0.0, tt), axis=-1, keepdims=True) * (1.0 / HEAD_DIM)
            return r * (dxh - xh * jnp.where(lo, mean_a, mean_b)), g * xh

        dxq, gq = norm_bwd(pair(dq_ref) * scale, zq_ref[...].astype(F32), qw_ref[...])
        dqf_ref[...] = dxq.astype(BF16)
        dqw_ref[...] += jnp.sum(gq, axis=0, keepdims=True)
        dxk, gk = norm_bwd(pair(dk_ref), zk_ref[...].astype(F32), kw_ref[...])
        dkf_ref[...] = dxk.astype(BF16)
        dkw_ref[...] += jnp.sum(gk, axis=0, keepdims=True)
        dvf_ref[...] = pair(dv_ref).astype(BF16)

        contrib = jnp.zeros((tp, AUG), F32)
        for half in range(2):
            df = (jnp.sum(jnp.where(lane == C_F, dq_ref[half], 0.0), axis=-1, keepdims=True)
                  - jnp.sum(jnp.where(lane == C_ONE, dk_ref[half], 0.0), axis=-1, keepdims=True))
            contrib = jnp.where(lane == 2 * j + half, df, contrib)

        @pl.when(j == 0)
        def _():
            dF_ref[...] = contrib

        @pl.when(j > 0)
        def _():
            dF_ref[...] += contrib

    pairb = pl.BlockSpec((2, tp, AUG), lambda i, j: (j, i, 0))
    col = lambda c0: pl.BlockSpec((tp, AUG), lambda i, j, c0=c0: (i, c0 + j))
    vec = pl.BlockSpec((1, AUG), lambda i, j: (0, 0))
    flat = pl.BlockSpec((tp, AUG), lambda i, j: (i, j))
    return _pallas(body, name="qkv_bwd", grid=(S // tp, N_HEADS // 2),
                   in_specs=[pairb, pairb, pairb, col(8), col(12), vec, vec],
                   out_specs=[flat, flat, flat, pl.BlockSpec((tp, AUG), lambda i, j: (i, 0)), vec, vec],
                   out_shape=[_sds((S, D_ATTN), BF16)] * 3 + [_sds((S, AUG), F32), _sds((1, AUG), F32), _sds((1, AUG), F32)],
                   compiler_params=_cp(("arbitrary", "arbitrary")))(dq, dk, dv, z, z, qw, kw)


def _proj_res_heads(a, wa, o_aug, wo, res):
    S, D = res.shape
    H = o_aug.shape[0]
    tm = TOK_TILE

    def body(a_ref, wa_ref, o_ref, wo_ref, res_ref, out_ref):
        acc = res_ref[...] + _dot(a_ref[...], wa_ref[...])
        for h in range(H):
            acc = acc + _dot(o_ref[h], wo_ref[h])
        out_ref[...] = acc

    row = pl.BlockSpec((tm, D), lambda i: (i, 0))
    return _pallas(body, name="proj_res_heads", grid=(S // tm,),
                   in_specs=[pl.BlockSpec((tm, a.shape[1]), lambda i: (i, 0)), pl.BlockSpec(wa.shape, lambda i: (0, 0)),
                             pl.BlockSpec((H, tm, AUG), lambda i: (0, i, 0)), pl.BlockSpec(wo.shape, lambda i: (0, 0, 0)), row],
                   out_specs=row, out_shape=_sds((S, D), F32), compiler_params=_cp(("parallel",)))(a, wa, o_aug, wo, res)


def _heads_tn(o_aug, d):
    H, S, A = o_aug.shape
    D = d.shape[1]
    tm = min(DW_TILE, S)
    nt = S // tm

    def body(o_ref, d_ref, out_ref, acc_ref):
        i = pl.program_id(0)

        @pl.when(i == 0)
        def _():
            acc_ref[...] = jnp.zeros_like(acc_ref)

        dv = d_ref[...].astype(BF16)
        for h in range(H):
            acc_ref[h] += _dot_tn(o_ref[h], dv)

        @pl.when(i == nt - 1)
        def _():
            out_ref[...] = acc_ref[...].astype(BF16)

    return _pallas(body, name="heads_tn", grid=(nt,),
                   in_specs=[pl.BlockSpec((H, tm, A), lambda i: (0, i, 0)), pl.BlockSpec((tm, D), lambda i: (i, 0))],
                   out_specs=pl.BlockSpec((H, A, D), lambda i: (0, 0, 0)), out_shape=_sds((H, A, D), BF16),
                   scratch_shapes=[pltpu.VMEM((H, A, D), F32)], compiler_params=_cp(("arbitrary",)))(o_aug, d)


def _odd_mid_fwd(z, cw):
    S = z.shape[0]
    D = z.shape[1] // 3
    tm = TOK_TILE
    hb = tm // HALO_C
    W = CONV_C_WIDTH

    def body(gb_ref, gc_ref, hh_ref, gcp_ref, hhp_ref, cw_ref, y_ref, win):
        i = pl.program_id(0)
        prev = gcp_ref[...].astype(F32) * hhp_ref[...].astype(F32)
        win[pl.ds(0, HALO_C), :] = jnp.where(i == 0, 0.0, prev)
        win[pl.ds(HALO_C, tm), :] = gc_ref[...].astype(F32) * hh_ref[...].astype(F32)
        c1 = jnp.zeros((tm, D), F32)
        for k in range(W):
            c1 = c1 + cw_ref[k:k + 1, :] * win[pl.ds(HALO_C - (W - 1) + k, tm), :]
        y_ref[...] = (gb_ref[...].astype(F32) * c1).astype(BF16)

    cur = lambda c: pl.BlockSpec((tm, D), lambda i, c=c: (i, c))
    prv = lambda c: pl.BlockSpec((HALO_C, D), lambda i, c=c: (jnp.maximum(i * hb - 1, 0), c))
    return _pallas(body, name="odd_mid_fwd", grid=(S // tm,),
                   in_specs=[cur(0), cur(1), cur(2), prv(1), prv(2), pl.BlockSpec((8, D), lambda i: (0, 0))],
                   out_specs=pl.BlockSpec((tm, D), lambda i: (i, 0)), out_shape=_sds((S, D), BF16),
                   scratch_shapes=[pltpu.VMEM((tm + HALO_C, D), F32)],
                   compiler_params=_cp(("parallel",)))(z, z, z, z, z, cw)


def _odd_mid_bwd(dy, z, cw):
    S = z.shape[0]
    D = z.shape[1] // 3
    tm = TOK_TILE
    hb = tm // HALO_C
    nt = S // tm
    W = CONV_C_WIDTH

    def body(dy_ref, dyn_ref, gb_ref, gbn_ref, gc_ref, hh_ref, gcp_ref, hhp_ref, cw_ref, dz_ref, dcw_ref, win, dwin):
        i = pl.program_id(0)

        @pl.when(i == 0)
        def _():
            dcw_ref[...] = jnp.zeros_like(dcw_ref)

        gc = gc_ref[...].astype(F32)
        hh = hh_ref[...].astype(F32)
        prev = gcp_ref[...].astype(F32) * hhp_ref[...].astype(F32)
        win[pl.ds(0, HALO_C), :] = jnp.where(i == 0, 0.0, prev)
        win[pl.ds(HALO_C, tm), :] = gc * hh
        dyv = dy_ref[...]
        dc1 = dyv * gb_ref[...].astype(F32)
        dwin[pl.ds(0, tm), :] = dc1
        dwin[pl.ds(tm, HALO_C), :] = jnp.where(i == nt - 1, 0.0, dyn_ref[...] * gbn_ref[...].astype(F32))
        c1 = jnp.zeros((tm, D), F32)
        dc0 = jnp.zeros((tm, D), F32)
        for k in range(W):
            tap = win[pl.ds(HALO_C - (W - 1) + k, tm), :]
            c1 = c1 + cw_ref[k:k + 1, :] * tap
            dc0 = dc0 + cw_ref[k:k + 1, :] * dwin[pl.ds(W - 1 - k, tm), :]
            dcw_ref[k:k + 1, :] += jnp.sum(dc1 * tap, axis=0, keepdims=True)
        dz_ref[:, 0:D] = (dyv * c1).astype(BF16)
        dz_ref[:, D:2 * D] = (dc0 * hh).astype(BF16)
        dz_ref[:, 2 * D:3 * D] = (dc0 * gc).astype(BF16)

    cur = lambda c: pl.BlockSpec((tm, D), lambda i, c=c: (i, c))
    prv = lambda c: pl.BlockSpec((HALO_C, D), lambda i, c=c: (jnp.maximum(i * hb - 1, 0), c))
    nxt = pl.BlockSpec((HALO_C, D), lambda i: (jnp.minimum((i + 1) * hb, S // HALO_C - 1), 0))
    return _pallas(body, name="odd_mid_bwd", grid=(nt,),
                   in_specs=[cur(0), nxt, cur(0), nxt, cur(1), cur(2), prv(1), prv(2), pl.BlockSpec((8, D), lambda i: (0, 0))],
                   out_specs=[pl.BlockSpec((tm, 3 * D), lambda i: (i, 0)), pl.BlockSpec((8, D), lambda i: (0, 0))],
                   out_shape=[_sds((S, 3 * D), BF16), _sds((8, D), F32)],
                   scratch_shapes=[pltpu.VMEM((tm + HALO_C, D), F32), pltpu.VMEM((tm + HALO_C, D), F32)],
                   compiler_params=_cp(("arbitrary",)))(dy, dy, z, z, z, z, z, z, cw)


def _loss_head(y, tgt):
    S, D = y.shape
    tm = TOK_TILE

    def body(y_ref, t_ref, dy_ref, l_ref):
        @pl.when(pl.program_id(0) == 0)
        def _():
            l_ref[...] = jnp.zeros_like(l_ref)

        e = y_ref[...] - t_ref[...]
        dy_ref[...] = e * (1.0 / D)
        l_ref[...] += jnp.sum(jnp.sum(e * e, axis=-1, keepdims=True), axis=0, keepdims=True) * (0.5 / D)

    row = pl.BlockSpec((tm, D), lambda i: (i, 0))
    return _pallas(body, name="loss_head", grid=(S // tm,), in_specs=[row, row],
                   out_specs=[row, pl.BlockSpec((1, 1), lambda i: (0, 0))],
                   out_shape=[_sds((S, D), F32), _sds((1, 1), F32)],
                   compiler_params=_cp(("arbitrary",)))(y, tgt)


def _pad_rows(a, rows):
    return jnp.pad(a, ((0, rows - a.shape[0]), (0, 0)))


def _local_step(x, tgt, W, need=lambda block, after: None, done=lambda block, block_grads: None):
    S, D = x.shape
    grads = {}
    saved = {}

    def gain_after(gain, token):
        return gain if token is None else gain + token

    def ffn_f(tag, l, xin):
        need((tag, l), xin)
        out, xn, G, U = _ffn_fwd(xin, W[tag + "_norm"][l:l + 1], W[tag + "_w_gate"][l], W[tag + "_w_up"][l],
                                 W[tag + "_w_down"][l])
        saved[(tag, l)] = (xin, xn, G, U)
        return out

    def ffn_b(tag, l, dout):
        xin, xn, G, U = saved[(tag, l)]
        keys = [(tag + "_w_gate", l), (tag + "_w_up", l), (tag + "_w_down", l)]
        *dws, dG, dU = _ffn_bwd_w(dout, xn, G, U, W[tag + "_w_down"][l])
        big = dict(zip(keys, dws))
        grads.update(big)
        token = done((tag, l), big)
        dx, dg = _norm_in_bwd([dG, dU], [W[tag + "_w_gate"][l], W[tag + "_w_up"][l]], xin,
                              gain_after(W[tag + "_norm"][l:l + 1], token), dout, w_rows=True)
        grads[(tag + "_norm", l)] = dg
        return dx

    x0a = ffn_f("ffn1", 0, x)
    need(("ev", 0), x0a)
    w_in = W["ev_w_in"]
    w_main, w_f = w_in[:, :2560], jnp.pad(w_in[:, 2560:], ((0, 0), (0, 120)))
    h0, z0, fl = _norm_proj(x0a, W["mix_norm"][0:1], w_main, w_f)
    cw_a = _pad_rows(W["ev_conv_w"], 32)
    a_act, a1 = _conv_a_fwd(z0, cw_a, W["ev_conv_b"], W["ev_conv_norm"])
    flb, Fc = _forget_scan(fl, jnp.pad(W["ev_b_f"], ((0, 0), (0, 120))))
    qw2, kw2 = jnp.tile(W["ev_q_norm"], (1, 2)), jnp.tile(W["ev_k_norm"], (1, 2))
    q_aug, k_aug, v_aug = _qkv_prep(z0, Fc, qw2, kw2)
    o_aug, q_lse = _fox_fwd(q_aug, k_aug, v_aug)
    w_out_e = W["ev_w_out"]
    w_out_o = jnp.pad(w_out_e[D_CONV:].reshape(N_HEADS, HEAD_DIM, D), ((0, 0), (0, AUG - HEAD_DIM), (0, 0)))
    x0b = _proj_res_heads(a_act, w_out_e[:D_CONV], o_aug, w_out_o, x0a)
    x0c = ffn_f("ffn2", 0, x0b)
    x1a = ffn_f("ffn1", 1, x0c)
    need(("od", 0), x1a)
    h1, z1 = _norm_proj(x1a, W["mix_norm"][1:2], W["od_w_in"])
    cw_c = _pad_rows(W["od_conv_w"], 8)
    y1 = _odd_mid_fwd(z1, cw_c)
    x1b = _proj_res([y1], [W["od_w_out"]], x1a)
    x1c = ffn_f("ffn2", 1, x1b)
    dy, loss = _loss_head(x1c, tgt)

    d = ffn_b("ffn2", 1, dy)
    dy1 = _matmul_nt(d, W["od_w_out"])
    grads[("od_w_out", 0)] = _matmul_tn(y1, d, D)[0]
    dz1, dcw_c = _odd_mid_bwd(dy1, z1, cw_c)
    grads[("od_conv_w", 0)] = dcw_c[:CONV_C_WIDTH]
    grads[("od_w_in", 0)] = _matmul_tn(h1, dz1, 3 * D // 4)
    token = done(("od", 0), {k: grads[k] for k in (("od_w_out", 0), ("od_w_in", 0))})
    d, dg = _norm_in_bwd([dz1[None]], [W["od_w_in"][None]], x1a, gain_after(W["mix_norm"][1:2], token), d)
    grads[("mix_norm", 1)] = dg
    d = ffn_b("ffn1", 1, d)
    d = ffn_b("ffn2", 0, d)
    dcat = _matmul_nt(d, w_out_e)
    grads[("ev_w_out", 0)] = jnp.concatenate([_matmul_tn(a_act, d, D)[0],
                                              _heads_tn(o_aug, d)[:, :HEAD_DIM].reshape(D_ATTN, D)], axis=0)
    duz, dcw_a, dcb, dcn = _conv_a_bwd(dcat, a1, z0, cw_a, W["ev_conv_norm"])
    grads[("ev_conv_w", 0)] = dcw_a[:CONV_A_WIDTH]
    grads[("ev_conv_b", 0)] = dcb
    grads[("ev_conv_norm", 0)] = dcn
    dq_a, dk_a, dv_a = _fox_bwd(q_lse, k_aug, v_aug, _do_prep(dcat, o_aug))
    dqf, dkf, dvf, dF, dqw, dkw = _qkv_bwd(dq_a, dk_a, dv_a, z0, qw2, kw2)
    grads[("ev_q_norm", 0)] = dqw[:, :HEAD_DIM] + dqw[:, HEAD_DIM:]
    grads[("ev_k_norm", 0)] = dkw[:, :HEAD_DIM] + dkw[:, HEAD_DIM:]
    dfl, dbf = _forget_scan_bwd(dF, flb)
    grads[("ev_b_f", 0)] = dbf[:, :N_HEADS]
    dz0 = jnp.concatenate([duz, dqf, dkf, dvf], axis=1)
    dflb = dfl.astype(BF16)
    gmain = _matmul_tn(h0, dz0, 640)
    gmain = gmain.transpose(1, 0, 2).reshape(D, 2560)
    gf = _matmul_tn(h0, dflb, 128)[0][:, :N_HEADS]
    grads[("ev_w_in", 0)] = jnp.concatenate([gmain, gf], axis=1)
    token = done(("ev", 0), {k: grads[k] for k in (("ev_w_out", 0), ("ev_w_in", 0))})
    d, dg = _norm_in_bwd([dz0[None], dflb[None]], [w_main[None], w_f[None]], x0a, gain_after(W["mix_norm"][0:1], token), d)
    grads[("mix_norm", 0)] = dg
    d = ffn_b("ffn1", 0, d)
    return loss, d, grads


def _place():
    x, y, c = lax.axis_index("x"), lax.axis_index("y"), lax.axis_index("c")
    chips = [(1 - x, y), (x, 1 - y), (1 - x, 1 - y)]
    return x, y, c, chips


def _remote(src, dst, send_sem, recv_sem, to):
    return pltpu.make_async_remote_copy(src_ref=src, dst_ref=dst, send_sem=send_sem, recv_sem=recv_sem,
                                        device_id=to, device_id_type=MESH)


HBM = pl.BlockSpec(memory_space=pltpu.HBM)
SEM = pl.BlockSpec(memory_space=pltpu.SEMAPHORE)
EFFECT = pltpu.SideEffectType.DATAFLOW_SIDE_EFFECTING


def _in_hbm(a):
    return pltpu.with_memory_space_constraint(a, pltpu.HBM)


def _ag_start(tag, bufs, with_taps):
    n = len(bufs)
    order = ([n - 1] + list(range(n - 1))) if with_taps else list(range(n))

    def body(*refs):
        send_sems, recv_sems = refs[n], refs[n + 1]
        outs, token = refs[n + 2:2 * n + 2], refs[2 * n + 2]
        x, y, c, chips = _place()
        me = 2 * x + y
        for a in order:
            if with_taps and a == n - 1:
                blk = outs[a].at[me]
            else:
                h = outs[a].shape[1] // 2
                blk = outs[a].at[me, pl.ds(c * h, h)]
            for jj, (px, py) in enumerate(chips):
                _remote(blk, blk, send_sems.at[3 * a + jj], recv_sems.at[3 * a + jj], (px, py, c)).start()
        token[...] = jnp.zeros_like(token)

    return _pallas(
        body, name=f"gather_start_{tag}",
        out_shape=[pltpu.SemaphoreType.DMA((3 * n,)), pltpu.SemaphoreType.DMA((3 * n,))]
        + [pltpu.HBM(b.shape, b.dtype) for b in bufs] + [_sds((8, 128), F32)],
        in_specs=[HBM] * n, out_specs=[SEM, SEM] + [HBM] * n + [pl.BlockSpec(memory_space=pltpu.VMEM)],
        input_output_aliases={a: 2 + a for a in range(n)},
        compiler_params=pltpu.CompilerParams(has_side_effects=EFFECT),
    )(*[_in_hbm(b) for b in bufs])


def _ag_mid(g, ici_send, ici_recv, bufs, idx, taps, n_big, after):
    n = len(bufs)
    arrs = list(bufs) + ([taps] if taps is not None else [])
    m = len(arrs)

    def body(*refs):
        ici_s, ici_r = refs[0], refs[1]
        d_send, d_recv = refs[m + 3], refs[m + 4]
        outs = refs[m + 5:]
        x, y, c, chips = _place()
        me = 2 * x + y
        for i in range(m):
            a = idx[i] if i < n else n_big
            for jj, (px, py) in enumerate(chips):
                k = 3 * a + jj
                if i < n:
                    h = outs[i].shape[1] // 2
                    mine, blk = outs[i].at[me, pl.ds(c * h, h)], outs[i].at[2 * px + py, pl.ds(c * h, h)]
                else:
                    mine, blk = outs[i].at[me], outs[i].at[2 * px + py]
                _remote(mine, mine, ici_s.at[k], ici_r.at[k], (px, py, c)).wait_send()
                _remote(blk, blk, ici_s.at[k], ici_r.at[k], (px, py, c)).wait_recv()
                if i < n:
                    _remote(blk, blk, d_send.at[3 * i + jj], d_recv.at[3 * i + jj], (x, y, 1 - c)).start()

    return _pallas(
        body, name=f"gather_pass_on_{g}",
        out_shape=[pltpu.SemaphoreType.DMA((3 * n,)), pltpu.SemaphoreType.DMA((3 * n,))] + [pltpu.HBM(b.shape, b.dtype) for b in arrs],
        in_specs=[SEM, SEM] + [HBM] * m + [ANY], out_specs=[SEM, SEM] + [HBM] * m,
        input_output_aliases={2 + i: 2 + i for i in range(m)},
        compiler_params=pltpu.CompilerParams(has_side_effects=EFFECT),
    )(ici_send, ici_recv, *arrs, after)


def _ag_wait(g, d_send, d_recv, arrs, n, after):
    m = len(arrs)

    def body(*refs):
        d_s, d_r = refs[0], refs[1]
        outs = refs[m + 3:]
        x, y, c, chips = _place()
        for i in range(n):
            h = outs[i].shape[1] // 2
            for jj, (px, py) in enumerate(chips):
                sent = outs[i].at[2 * px + py, pl.ds(c * h, h)]
                got = outs[i].at[2 * px + py, pl.ds((1 - c) * h, h)]
                _remote(sent, sent, d_s.at[3 * i + jj], d_r.at[3 * i + jj], (x, y, 1 - c)).wait_send()
                _remote(got, got, d_s.at[3 * i + jj], d_r.at[3 * i + jj], (x, y, 1 - c)).wait_recv()

    return _pallas(
        body, name=f"gather_wait_{g}", out_shape=[pltpu.HBM(b.shape, b.dtype) for b in arrs],
        in_specs=[SEM, SEM] + [HBM] * m + [ANY], out_specs=[HBM] * m,
        input_output_aliases={2 + i: i for i in range(m)},
        compiler_params=pltpu.CompilerParams(has_side_effects=EFFECT),
    )(d_send, d_recv, *arrs, after)


def _pair_start(g, gs, after):
    n = len(gs)
    zones = [lax.empty((4, a.shape[1] // 2, a.shape[2]), a.dtype) for a in gs]
    extra = [] if after is None else [after]

    def body(*refs):
        k0 = 2 * n + len(extra)
        send_sems, recv_sems = refs[k0], refs[k0 + 1]
        src, dst = refs[k0 + 2:k0 + 2 + n], refs[k0 + 2 + n:k0 + 2 + 2 * n]
        token = refs[k0 + 2 + 2 * n]
        x, y, c, _ = _place()
        for a in range(n):
            h = src[a].shape[1] // 2
            _remote(src[a].at[:, pl.ds((1 - c) * h, h)], dst[a], send_sems.at[a], recv_sems.at[a], (x, y, 1 - c)).start()
        token[...] = jnp.zeros_like(token)

    return _pallas(
        body, name=f"grad_pair_start_{g}",
        out_shape=[pltpu.SemaphoreType.DMA((n,)), pltpu.SemaphoreType.DMA((n,))]
        + [pltpu.HBM(a.shape, a.dtype) for a in gs + zones] + [_sds((8, 128), F32)],
        in_specs=[HBM] * (2 * n) + [ANY] * len(extra),
        out_specs=[SEM, SEM] + [HBM] * (2 * n) + [pl.BlockSpec(memory_space=pltpu.VMEM)],
        input_output_aliases={i: 2 + i for i in range(2 * n)},
        compiler_params=pltpu.CompilerParams(has_side_effects=EFFECT),
    )(*[_in_hbm(a) for a in gs + zones], *extra)


def _pair_wait(g, send, recv, gs, zones):
    n = len(gs)

    def body(*refs):
        s_ref, r_ref = refs[0], refs[1]
        outs = refs[2 + 2 * n:]
        src, dst = outs[:n], outs[n:]
        x, y, c, _ = _place()
        for a in range(n):
            h = src[a].shape[1] // 2
            _remote(src[a].at[:, pl.ds((1 - c) * h, h)], dst[a], s_ref.at[a], r_ref.at[a], (x, y, 1 - c)).wait()

    return _pallas(
        body, name=f"grad_pair_wait_{g}", out_shape=[pltpu.HBM(a.shape, a.dtype) for a in gs + zones],
        in_specs=[SEM, SEM] + [HBM] * (2 * n), out_specs=[HBM] * (2 * n),
        input_output_aliases={2 + i: i for i in range(2 * n)},
        compiler_params=pltpu.CompilerParams(has_side_effects=EFFECT),
    )(send, recv, *gs, *zones)


def _pair_add(gs, others, c_arr):
    n = len(gs)

    def body(c_ref, *refs):
        for g_ref, o_ref, out_ref in zip(refs[:n], refs[n:2 * n], refs[2 * n:]):
            out_ref[...] = (g_ref[...].astype(F32) + o_ref[...].astype(F32)).astype(BF16)

    half = lambda a: pl.BlockSpec((1, a.shape[1] // 2, a.shape[2]), lambda k, c_ref: (k, c_ref[0], 0))
    whole = lambda a: pl.BlockSpec((1,) + a.shape[1:], lambda k, c_ref: (k, 0, 0))
    grid_spec = pltpu.PrefetchScalarGridSpec(
        num_scalar_prefetch=1, grid=(4,), in_specs=[half(a) for a in gs] + [whole(o) for o in others],
        out_specs=[whole(o) for o in others])
    return _pallas(body, name="grad_pair_add", grid_spec=grid_spec, out_shape=[_sds(o.shape, BF16) for o in others],
                   compiler_params=_cp(("parallel",)))(c_arr, *gs, *others)


def _chip_start(g, ss):
    n = len(ss)
    zones = [lax.empty((3,) + s.shape[1:], s.dtype) for s in ss]

    def body(*refs):
        send_sems, recv_sems = refs[2 * n], refs[2 * n + 1]
        src, dst = refs[2 * n + 2:3 * n + 2], refs[3 * n + 2:4 * n + 2]
        token = refs[4 * n + 2]
        x, y, c, chips = _place()
        for a in range(n):
            for jj, (px, py) in enumerate(chips):
                k = 3 * a + jj
                _remote(src[a].at[2 * px + py], dst[a].at[jj], send_sems.at[k], recv_sems.at[k], (px, py, c)).start()
        token[...] = jnp.zeros_like(token)

    return _pallas(
        body, name=f"grad_chip_start_{g}",
        out_shape=[pltpu.SemaphoreType.DMA((3 * n,)), pltpu.SemaphoreType.DMA((3 * n,))]
        + [pltpu.HBM(a.shape, a.dtype) for a in ss + zones] + [_sds((8, 128), F32)],
        in_specs=[HBM] * (2 * n), out_specs=[SEM, SEM] + [HBM] * (2 * n) + [pl.BlockSpec(memory_space=pltpu.VMEM)],
        input_output_aliases={i: 2 + i for i in range(2 * n)},
        compiler_params=pltpu.CompilerParams(has_side_effects=EFFECT),
    )(*[_in_hbm(a) for a in ss + zones])


def _chip_wait(tag, sends, recvs, counts, ss, zones, after):
    nb, n = len(sends), len(ss)

    def body(*refs):
        s_refs, r_refs = refs[:nb], refs[nb:2 * nb]
        outs = refs[2 * nb + 2 * n + 1:]
        src, dst = outs[:n], outs[n:]
        x, y, c, chips = _place()
        a = 0
        for b in range(nb):
            for i in range(counts[b]):
                for jj, (px, py) in enumerate(chips):
                    k = 3 * i + jj
                    _remote(src[a].at[2 * px + py], dst[a].at[jj], s_refs[b].at[k], r_refs[b].at[k], (px, py, c)).wait()
                a += 1

    return _pallas(
        body, name=f"grad_chip_wait_{tag}", out_shape=[pltpu.HBM(a.shape, a.dtype) for a in ss + zones],
        in_specs=[SEM] * (2 * nb) + [HBM] * (2 * n) + [ANY], out_specs=[HBM] * (2 * n),
        input_output_aliases={2 * nb + i: i for i in range(2 * n)},
        compiler_params=pltpu.CompilerParams(has_side_effects=EFFECT),
    )(*sends, *recvs, *ss, *zones, after)


def _chip_sum(s, r, where, dest, l, L):
    _, h, C = s.shape
    tr = h // 2

    def body(k_ref, s_ref, r_ref, *rest):
        out_ref = rest[-1]
        acc = s_ref[0].astype(F32)
        for jj in range(3):
            acc = acc + r_ref[jj].astype(F32)
        out_ref[...] = acc

    in_specs = [pl.BlockSpec((1, tr, C), lambda i, k_ref: (k_ref[0], i, 0)), pl.BlockSpec((3, tr, C), lambda i, k_ref: (0, i, 0))]
    args = [where, s, r]
    alias = {}
    if dest is not None:
        in_specs.append(ANY)
        args.append(dest)
        alias = {3: 0}
    grid_spec = pltpu.PrefetchScalarGridSpec(
        num_scalar_prefetch=1, grid=(2,), in_specs=in_specs,
        out_specs=pl.BlockSpec((None, tr, C), lambda i, k_ref: (l, 2 * k_ref[1] + i, 0)))
    return _pallas(body, name="grad_chip_sum", grid_spec=grid_spec, out_shape=_sds((L, 2 * h, C), F32),
                   input_output_aliases=alias, compiler_params=_cp(("arbitrary",)))(*args)


def _share_start(tag, bufs, layout):
    n, n_buf = len(layout), len(bufs)

    def body(*refs):
        send_sems, recv_sems = refs[n_buf], refs[n_buf + 1]
        outs = refs[n_buf + 2:]
        x, y, c, _ = _place()
        for a, (o, l) in enumerate(layout):
            h = outs[o].shape[1] // 2
            blk = outs[o].at[l, pl.ds(c * h, h)]
            _remote(blk, blk, send_sems.at[a], recv_sems.at[a], (x, y, 1 - c)).start()

    return _pallas(
        body, name=f"grad_share_start_{tag}",
        out_shape=[pltpu.SemaphoreType.DMA((n,)), pltpu.SemaphoreType.DMA((n,))] + [pltpu.HBM(b.shape, b.dtype) for b in bufs],
        in_specs=[HBM] * n_buf, out_specs=[SEM, SEM] + [HBM] * n_buf, input_output_aliases={o: 2 + o for o in range(n_buf)},
        compiler_params=pltpu.CompilerParams(has_side_effects=EFFECT),
    )(*[_in_hbm(b) for b in bufs])


def _share_wait(tag, send, recv, bufs, layout, after):
    n_buf = len(bufs)

    def body(*refs):
        s_ref, r_ref = refs[0], refs[1]
        outs = refs[n_buf + 3:]
        x, y, c, _ = _place()
        for a, (o, l) in enumerate(layout):
            h = outs[o].shape[1] // 2
            mine, theirs = outs[o].at[l, pl.ds(c * h, h)], outs[o].at[l, pl.ds((1 - c) * h, h)]
            _remote(mine, mine, s_ref.at[a], r_ref.at[a], (x, y, 1 - c)).wait_send()
            _remote(theirs, theirs, s_ref.at[a], r_ref.at[a], (x, y, 1 - c)).wait_recv()

    return _pallas(
        body, name=f"grad_share_wait_{tag}", out_shape=[pltpu.HBM(b.shape, b.dtype) for b in bufs],
        in_specs=[SEM, SEM] + [HBM] * n_buf + [ANY], out_specs=[HBM] * n_buf,
        input_output_aliases={2 + o: o for o in range(n_buf)},
        compiler_params=pltpu.CompilerParams(has_side_effects=EFFECT),
    )(send, recv, *bufs, after)


def _small_all_reduce(packed):
    P, L = packed.shape

    def body(in_ref, out_ref, slots, send_sems, recv_sems):
        x, y, c, _ = _place()
        me = 4 * x + 2 * y + c
        slots[me] = in_ref[...]
        cps = []
        for r in range(1, 8):
            px = 1 - x if r & 4 else x
            py = 1 - y if r & 2 else y
            pc = 1 - c if r & 1 else c
            cps.append(_remote(in_ref, slots.at[me], send_sems.at[r - 1], recv_sems.at[r - 1], (px, py, pc)))
        for cp in cps:
            cp.start()
        for r in range(1, 8):
            px = 1 - x if r & 4 else x
            py = 1 - y if r & 2 else y
            pc = 1 - c if r & 1 else c
            blk = slots.at[4 * px + 2 * py + pc]
            _remote(blk, blk, send_sems.at[r - 1], recv_sems.at[r - 1], (px, py, pc)).wait_recv()
        for cp in cps:
            cp.wait_send()
        acc = slots[0]
        for k in range(1, 8):
            acc = acc + slots[k]
        out_ref[...] = acc

    vm = pl.BlockSpec(memory_space=pltpu.VMEM)
    return _pallas(body, name="small_all_reduce", in_specs=[vm], out_specs=vm, out_shape=_sds((P, L), F32),
                   scratch_shapes=[pltpu.VMEM((8, P, L), F32), pltpu.SemaphoreType.DMA((7,)), pltpu.SemaphoreType.DMA((7,))])(packed)


def _adamw_math(w, g, m, v):
    m = ADAM_B1 * m + (1.0 - ADAM_B1) * g
    v = ADAM_B2 * v + (1.0 - ADAM_B2) * (g * g)
    m_hat = m / (1.0 - ADAM_B1 ** ADAM_STEP)
    v_hat = v / (1.0 - ADAM_B2 ** ADAM_STEP)
    delta = -ADAM_LR * (m_hat / (jnp.sqrt(v_hat) + ADAM_EPS) + ADAM_WD * w)
    return delta, m, v


def _adamw(w, g, m, v):
    shape = w.shape
    C = shape[-1]
    rows = math.prod(shape[:-1])
    tr = next(t for t in (512, 352, 256, 128, 64, 32, 16, 8, rows) if rows % t == 0)
    w2, g2, m2, v2 = (a.reshape(rows, C) for a in (w, g, m, v))

    def body(w_ref, g_ref, m_ref, v_ref, go_ref, d_ref, nm_ref, nv_ref):
        gv = g_ref[...]
        d, nm, nv = _adamw_math(w_ref[...], gv, m_ref[...], v_ref[...])
        go_ref[...] = gv
        d_ref[...] = d
        nm_ref[...] = nm
        nv_ref[...] = nv

    blk = pl.BlockSpec((tr, C), lambda i: (i, 0))
    outs = _pallas(body, name="adamw", grid=(rows // tr,), in_specs=[blk] * 4, out_specs=[blk] * 4,
                   out_shape=[_sds((rows, C), F32)] * 4, compiler_params=_cp(("parallel",)))(w2, g2, m2, v2)
    return tuple(o.reshape(shape) for o in outs)


WEIGHTS = ["ffn1_norm", "ffn1_w_gate", "ffn1_w_up", "ffn1_w_down", "mix_norm", "ffn2_norm", "ffn2_w_gate", "ffn2_w_up",
           "ffn2_w_down", "ev_w_in", "ev_b_f", "ev_conv_w", "ev_conv_b", "ev_conv_norm", "ev_q_norm", "ev_k_norm",
           "ev_w_out", "od_w_in", "od_conv_w", "od_w_out"]
BIG = ([("ffn1_w_gate", 0), ("ffn1_w_up", 0), ("ffn1_w_down", 0), ("ev_w_in", 0), ("ev_w_out", 0),
        ("ffn2_w_gate", 0), ("ffn2_w_up", 0), ("ffn2_w_down", 0)]
       + [("ffn1_w_gate", 1), ("ffn1_w_up", 1), ("ffn1_w_down", 1), ("od_w_in", 0), ("od_w_out", 0),
          ("ffn2_w_gate", 1), ("ffn2_w_up", 1), ("ffn2_w_down", 1)])
TRANSPOSED = ("ffn1_w_gate", "ffn1_w_up", "ffn2_w_gate", "ffn2_w_up")
SHARED_LAST = ("ffn1_w_gate", "ffn1_w_up", "ffn1_w_down", "ev_w_in", "ev_w_out")
BLOCKS = [("ffn1", 0), ("ev", 0), ("ffn2", 0), ("ffn1", 1), ("od", 0), ("ffn2", 1)]
BLOCK_OF = {(name, l): (name.split("_w_")[0], l) for name, l in BIG}
BIG_NAMES = ["ffn1_w_gate", "ffn1_w_up", "ffn1_w_down", "ffn2_w_gate", "ffn2_w_up", "ffn2_w_down",
             "ev_w_in", "ev_w_out", "od_w_in", "od_w_out"]
SMALL = [("ffn1_norm", 16), ("mix_norm", 16), ("ffn2_norm", 16), ("ev_b_f", 8), ("ev_conv_w", 128), ("ev_conv_b", 8),
         ("ev_conv_norm", 8), ("ev_q_norm", 8), ("ev_k_norm", 8), ("od_conv_w", 24)]


def _to_lanes(a, rows):
    flat = a.reshape(-1)
    return jnp.pad(flat, (0, rows * 128 - flat.shape[0])).reshape(rows, 128)


def kernel(x, ffn1_norm, ffn1_w_gate, ffn1_w_up, ffn1_w_down, mix_norm, ffn2_norm, ffn2_w_gate, ffn2_w_up, ffn2_w_down, ev_w_in, ev_b_f, ev_conv_w, ev_conv_b, ev_conv_norm, ev_q_norm, ev_k_norm, ev_w_out, od_w_in, od_conv_w, od_w_out, loss_target, m_ffn1_norm, m_ffn1_w_gate, m_ffn1_w_up, m_ffn1_w_down, m_mix_norm, m_ffn2_norm, m_ffn2_w_gate, m_ffn2_w_up, m_ffn2_w_down, m_ev_w_in, m_ev_b_f, m_ev_conv_w, m_ev_conv_b, m_ev_conv_norm, m_ev_q_norm, m_ev_k_norm, m_ev_w_out, m_od_w_in, m_od_conv_w, m_od_w_out, v_ffn1_norm, v_ffn1_w_gate, v_ffn1_w_up, v_ffn1_w_down, v_mix_norm, v_ffn2_norm, v_ffn2_w_gate, v_ffn2_w_up, v_ffn2_w_down, v_ev_w_in, v_ev_b_f, v_ev_conv_w, v_ev_conv_b, v_ev_conv_norm, v_ev_q_norm, v_ev_k_norm, v_ev_w_out, v_od_w_in, v_od_conv_w, v_od_w_out):
    P = dict(ffn1_norm=ffn1_norm, ffn1_w_gate=ffn1_w_gate, ffn1_w_up=ffn1_w_up, ffn1_w_down=ffn1_w_down, mix_norm=mix_norm,
             ffn2_norm=ffn2_norm, ffn2_w_gate=ffn2_w_gate, ffn2_w_up=ffn2_w_up, ffn2_w_down=ffn2_w_down, ev_w_in=ev_w_in,
             ev_b_f=ev_b_f, ev_conv_w=ev_conv_w, ev_conv_b=ev_conv_b, ev_conv_norm=ev_conv_norm, ev_q_norm=ev_q_norm,
             ev_k_norm=ev_k_norm, ev_w_out=ev_w_out, od_w_in=od_w_in, od_conv_w=od_conv_w, od_w_out=od_w_out)
    M = dict(zip(WEIGHTS, [m_ffn1_norm, m_ffn1_w_gate, m_ffn1_w_up, m_ffn1_w_down, m_mix_norm, m_ffn2_norm, m_ffn2_w_gate,
                           m_ffn2_w_up, m_ffn2_w_down, m_ev_w_in, m_ev_b_f, m_ev_conv_w, m_ev_conv_b, m_ev_conv_norm,
                           m_ev_q_norm, m_ev_k_norm, m_ev_w_out, m_od_w_in, m_od_conv_w, m_od_w_out]))
    V = dict(zip(WEIGHTS, [v_ffn1_norm, v_ffn1_w_gate, v_ffn1_w_up, v_ffn1_w_down, v_mix_norm, v_ffn2_norm, v_ffn2_w_gate,
                           v_ffn2_w_up, v_ffn2_w_down, v_ev_w_in, v_ev_b_f, v_ev_conv_w, v_ev_conv_b, v_ev_conv_norm,
                           v_ev_q_norm, v_ev_k_norm, v_ev_w_out, v_od_w_in, v_od_conv_w, v_od_w_out]))
    for name in TRANSPOSED:
        P[name], M[name], V[name] = (jnp.swapaxes(a, 1, 2) for a in (P[name], M[name], V[name]))
    S, D = x.shape[1], x.shape[2]
    chip = 2 * lax.axis_index("x") + lax.axis_index("y")
    core = lax.axis_index("c")

    def own_slot(shard):
        return lax.dynamic_update_slice(lax.empty((4,) + shard.shape, shard.dtype), shard[None], (chip, 0, 0))

    taps = jnp.concatenate([_to_lanes(_pad_rows(ev_conv_w[0], 32), 32), _to_lanes(_pad_rows(od_conv_w[0], 8), 16)], axis=0)
    first = [i for i, k in enumerate(BIG) if BLOCK_OF[k] in BLOCKS[:2]]
    rest = [i for i in range(len(BIG)) if i not in first]
    send0, recv0, *bufs0 = _ag_start("first", [own_slot(P[BIG[i][0]][BIG[i][1]].astype(BF16)) for i in first]
                                     + [own_slot(taps)], True)
    zero = bufs0.pop()[0, 0]
    send1, recv1, *bufs1 = _ag_start("rest", [own_slot((P[BIG[i][0]][BIG[i][1]] + zero).astype(BF16)) for i in rest], False)
    bufs1.pop()
    cols = lambda a: a.transpose(1, 0, 2).reshape(a.shape[1], 4 * a.shape[2])
    W = {k: P[k] for k in ("ffn1_norm", "mix_norm", "ffn2_norm", "ev_b_f", "ev_q_norm", "ev_k_norm")}
    W["ev_conv_b"], W["ev_conv_norm"] = ev_conv_b, ev_conv_norm
    for tag in ("ffn1", "ffn2"):
        for kind in ("_w_gate", "_w_up", "_w_down"):
            W[tag + kind] = [None, None]
    passing = {}

    def pass_on(g, after):
        idx = [i for i, k in enumerate(BIG) if BLOCK_OF[k] == BLOCKS[g]]
        keys = [BIG[i] for i in idx] + (["taps"] if BLOCKS[g] == ("ev", 0) else [])
        send, recv, bufs, members = (send0, recv0, bufs0, first) if g < 2 else (send1, recv1, bufs1, rest)
        local = [members.index(i) for i in idx]
        passing[g] = (keys, _ag_mid(g, send, recv, [bufs[i] for i in local], local,
                                    bufs0[-1] if BLOCKS[g] == ("ev", 0) else None, len(first), after))

    def need(block, after):
        g = BLOCKS.index(block)
        if g not in passing:
            pass_on(g, bufs1[0] if g == 0 else after)
        keys, (d_send, d_recv, *thru) = passing.pop(g)
        got = dict(zip(keys, _ag_wait(g, d_send, d_recv, thru, len(keys) - ("taps" in keys), after)))
        if 1 <= g < len(BLOCKS) - 1:
            pass_on(g + 1, after)
        for key, a in got.items():
            if key == "taps":
                continue
            name, l = key
            if name.startswith("ffn"):
                W[name][l] = a
            elif name.endswith("_w_in"):
                W[name] = cols(a)
            elif name.endswith("_w_out"):
                W[name] = a.reshape(4 * a.shape[1], D)
        if block == ("ev", 0):
            taps_all = got["taps"]
            W["ev_conv_w"] = cols(taps_all[:, :32].reshape(4, 32, 128))[:CONV_A_WIDTH]
            W["od_conv_w"] = cols(taps_all[:, 32:48].reshape(4, 8, 256))[:CONV_C_WIDTH]

    rows = lambda a: a.reshape(4, a.shape[0] // 4, a.shape[1])
    colsh = lambda a: a.reshape(a.shape[0], 4, a.shape[1] // 4).transpose(1, 0, 2)
    c_arr = core.reshape(1).astype(jnp.int32)
    where = jnp.stack([chip, core]).astype(jnp.int32)
    in_flight = []

    def done(block, block_grads):
        g = BLOCKS.index(block)
        keys = list(block_grads)
        gs = []
        for name, l in keys:
            a = block_grads[(name, l)]
            gs.append(colsh(a) if name == "ev_w_in" else rows(a) if name.endswith("_w_out") else a)
        for item in list(pairs):
            to_chips(item)
        send, recv, *rest = _pair_start(g, gs, chained.get("token"))
        n = len(keys)
        pairs.append((g, keys, send, recv, rest[:n], rest[n:2 * n]))
        if g == 0:
            to_chips(pairs[0])
        chained["token"] = rest[-1] if g else chained["token"]
        return chained["token"][0:1, 0:1]

    pairs, chained = [], {}

    def to_chips(item):
        pairs.remove(item)
        g, keys, send, recv, gs, zones = item
        n = len(keys)
        done_ = _pair_wait(g, send, recv, gs, zones)
        sums = list(_pair_add(list(done_[:n]), list(done_[n:]), c_arr))
        send2, recv2, *rest = _chip_start(g, sums)
        in_flight.append((keys, send2, recv2, rest[:n], rest[n:2 * n]))
        chained["token"] = rest[-1]

    loss, grad_x, grads = _local_step(x[0], loss_target[0], W, need, done)

    order = [k for keys, *_ in in_flight for k in keys]
    landed = _chip_wait("all", [f[1] for f in in_flight], [f[2] for f in in_flight], [len(f[0]) for f in in_flight],
                        [a for f in in_flight for a in f[3]], [a for f in in_flight for a in f[4]], grad_x)
    sums, recvd = landed[:len(order)], landed[len(order):]
    stacked, shares = {}, []
    for tag, names in (("a", [n for n in BIG_NAMES if n not in SHARED_LAST]), ("b", list(SHARED_LAST))):
        for (name, l), s, r in zip(order, sums, recvd):
            if name in names:
                stacked[name] = _chip_sum(s, r, where, stacked.get(name), l, P[name].shape[0])
        layout = [(names.index(name), l) for name, l in order if name in names]
        send, recv, *thru = _share_start(tag, [stacked[name] for name in names], layout)
        shares.append((tag, names, send, recv, thru, layout))

    def small_grad(name):
        if name.endswith("_norm") and name[:3] in ("ffn", "mix"):
            return jnp.concatenate([grads[(name, 0)], grads[(name, 1)]], axis=0)
        return grads[(name, 0)]

    packed = jnp.concatenate([_to_lanes(small_grad(name), r) for name, r in SMALL], axis=0)
    total = _small_all_reduce(packed)
    small_grads, at = {}, 0
    for name, r in SMALL:
        part = total[at:at + r].reshape(-1)
        at += r
        if name == "ev_conv_w":
            full_g = part[:CONV_A_WIDTH * D_CONV].reshape(CONV_A_WIDTH, D_CONV)
            small_grads[name] = lax.dynamic_slice_in_dim(full_g, chip * (D_CONV // 4), D_CONV // 4, axis=1)[None]
        elif name == "od_conv_w":
            full_g = part[:CONV_C_WIDTH * D].reshape(CONV_C_WIDTH, D)
            small_grads[name] = lax.dynamic_slice_in_dim(full_g, chip * (D // 4), D // 4, axis=1)[None]
        else:
            small_grads[name] = part[:math.prod(P[name].shape)].reshape(P[name].shape)

    results = {}

    def update(name, g):
        outs = _adamw(P[name], g, M[name], V[name])
        results[name] = tuple(jnp.swapaxes(a, 1, 2) for a in outs) if name in TRANSPOSED else outs

    for name, _ in SMALL:
        update(name, small_grads[name])
    after = shares[-1][4][0]
    for tag, names, send, recv, thru, layout in shares:
        for name, g in zip(names, _share_wait(tag, send, recv, thru, layout, after)):
            update(name, g)
        after = results[names[-1]][1]
    loss_all = lax.psum(loss[0, 0], ("x", "y", "c"))
    return (loss_all, grad_x[None], *[results[name][k] for k in range(4) for name in WEIGHTS])
```

```python
import functools
import math

import jax
import jax.numpy as jnp
from jax import lax
from jax.experimental import pallas as pl
from jax.experimental.pallas import tpu as pltpu

F32, BF16 = jnp.float32, jnp.bfloat16
EPS = 1e-6
FFN_RES = 0.5
N_HEADS, HEAD_DIM = 8, 64
D_CONV = 512
D_ATTN = N_HEADS * HEAD_DIM
CONV_A_WIDTH, CONV_C_WIDTH = 31, 3
ADAM_LR, ADAM_B1, ADAM_B2, ADAM_EPS, ADAM_WD, ADAM_STEP = 0.001, 0.9, 0.999, 1e-08, 0.01, 10
MESH = pl.DeviceIdType.MESH
ANY = pl.BlockSpec(memory_space=pl.ANY)

TOK_TILE = 512
FFN_TILE = 1024
DW_TILE = 1024
ATT_TILE = 1024
QKN_TILE = 2048
HALO_A, HALO_C = 32, 16
SUBLANES = 8
CONV_ROWS = 64
SCAN_BLK = 256
MIB = 2 ** 20


def _pallas(body, **kw):
    return pl.pallas_call(body, **kw)


def _cp(sem=None, vmem_mib=48):
    return pltpu.CompilerParams(dimension_semantics=sem, vmem_limit_bytes=vmem_mib * MIB)


def _dot(a, b):
    return jnp.dot(a, b, preferred_element_type=F32)


def _dot_nt(a, b):
    return lax.dot_general(a, b, (((1,), (1,)), ((), ())), preferred_element_type=F32)


def _dot_tn(a, b):
    return lax.dot_general(a, b, (((0,), (0,)), ((), ())), preferred_element_type=F32)


def _sds(shape, dtype):
    return jax.ShapeDtypeStruct(shape, dtype)


def _rms(x):
    return lax.rsqrt(jnp.mean(x * x, axis=-1, keepdims=True) + EPS)


def _rms_bwd(dy, x, g):
    r = _rms(x)
    xh = x * r
    dxh = dy * g
    dx = r * (dxh - xh * jnp.mean(dxh * xh, axis=-1, keepdims=True))
    return dx, xh


def _silu_grad(z):
    s = jax.nn.sigmoid(z)
    return s * (1.0 + z * (1.0 - s))


def _ffn_fwd(x, g, wg, wu, wd):
    S, D = x.shape
    nc, Fs, _ = wd.shape
    tm = min(FFN_TILE, S)

    def body(x_ref, g_ref, wg_ref, wu_ref, wd_ref, out_ref, xn_ref, G_ref, U_ref, acc_ref):
        j = pl.program_id(1)

        @pl.when(j == 0)
        def _():
            xv = x_ref[...]
            xn_ref[...] = (xv * _rms(xv) * g_ref[...]).astype(BF16)
            acc_ref[...] = jnp.zeros_like(acc_ref)

        xn = xn_ref[...]
        G = _dot_nt(xn, wg_ref[0])
        U = _dot_nt(xn, wu_ref[0])
        G_ref[0] = G.astype(BF16)
        U_ref[0] = U.astype(BF16)
        H = (G * jax.nn.sigmoid(G) * U).astype(BF16)
        acc_ref[...] += _dot(H, wd_ref[0])

        @pl.when(j == nc - 1)
        def _():
            out_ref[...] = x_ref[...] + FFN_RES * acc_ref[...]

    row = pl.BlockSpec((tm, D), lambda i, j: (i, 0))
    return _pallas(
        body, name="ffn_fwd", grid=(S // tm, nc),
        in_specs=[row, pl.BlockSpec((1, D), lambda i, j: (0, 0)),
                  pl.BlockSpec((1, Fs, D), lambda i, j: (j, 0, 0)), pl.BlockSpec((1, Fs, D), lambda i, j: (j, 0, 0)),
                  pl.BlockSpec((1, Fs, D), lambda i, j: (j, 0, 0))],
        out_specs=[row, row, pl.BlockSpec((1, tm, Fs), lambda i, j: (j, i, 0)),
                   pl.BlockSpec((1, tm, Fs), lambda i, j: (j, i, 0))],
        out_shape=[_sds((S, D), F32), _sds((S, D), BF16), _sds((nc, S, Fs), BF16), _sds((nc, S, Fs), BF16)],
        scratch_shapes=[pltpu.VMEM((tm, D), F32)],
        compiler_params=_cp(("parallel", "arbitrary"), 56),
    )(x, g, wg, wu, wd)


def _ffn_bwd_w(dout, xn, G, U, wd):
    S, D = dout.shape
    nc, _, Fs = G.shape
    tm = min(DW_TILE, S)
    nt = S // tm
    sub = min(TOK_TILE, tm)

    def body(do_ref, xn_ref, G_ref, U_ref, wd_ref, dwg_ref, dwu_ref, dwd_ref, dG_ref, dU_ref, ag, au, ad, do_s, H_s):
        i = pl.program_id(1)

        @pl.when(i == 0)
        def _():
            ag[...] = jnp.zeros_like(ag)
            au[...] = jnp.zeros_like(au)
            ad[...] = jnp.zeros_like(ad)

        for r in range(0, tm, sub):
            rows = pl.ds(r, sub)
            do = (FFN_RES * do_ref[rows, :]).astype(BF16)
            do_s[rows, :] = do
            Gv = G_ref[0, rows, :].astype(F32)
            Uv = U_ref[0, rows, :].astype(F32)
            dH = _dot_nt(do, wd_ref[0])
            sg = jax.nn.sigmoid(Gv)
            act = Gv * sg
            H_s[rows, :] = (act * Uv).astype(BF16)
            dU_ref[0, rows, :] = (dH * act).astype(BF16)
            dG_ref[0, rows, :] = (dH * Uv * (sg * (1.0 + Gv * (1.0 - sg)))).astype(BF16)
        xnv = xn_ref[...]
        ag[...] += _dot_tn(dG_ref[0], xnv)
        au[...] += _dot_tn(dU_ref[0], xnv)
        ad[...] += _dot_tn(H_s[...], do_s[...])

        @pl.when(i == nt - 1)
        def _():
            dwg_ref[0] = ag[...].astype(BF16)
            dwu_ref[0] = au[...].astype(BF16)
            dwd_ref[0] = ad[...].astype(BF16)

    row = pl.BlockSpec((tm, D), lambda j, i: (i, 0))
    hid = pl.BlockSpec((1, tm, Fs), lambda j, i: (j, i, 0))
    wrow = pl.BlockSpec((1, Fs, D), lambda j, i: (j, 0, 0))
    return _pallas(
        body, name="ffn_bwd_w", grid=(nc, nt),
        in_specs=[row, row, hid, hid, wrow],
        out_specs=[wrow, wrow, wrow, hid, hid],
        out_shape=[_sds((nc, Fs, D), BF16)] * 3 + [_sds((nc, S, Fs), BF16)] * 2,
        scratch_shapes=[pltpu.VMEM((Fs, D), F32)] * 3 + [pltpu.VMEM((tm, D), BF16), pltpu.VMEM((tm, Fs), BF16)],
        compiler_params=_cp(("parallel", "arbitrary"), 56),
    )(dout, xn, G, U, wd)


def _norm_in_bwd(dzs, ws, x, g, dres, w_rows=False):
    S, D = x.shape
    nc = dzs[0].shape[0]
    n = len(dzs)
    tm = TOK_TILE

    def body(*refs):
        dz_refs, w_refs = refs[:n], refs[n:2 * n]
        x_ref, g_ref, dres_ref, dx_ref, dg_ref, acc_ref = refs[2 * n:]
        i, j = pl.program_id(0), pl.program_id(1)

        @pl.when(j == 0)
        def _():
            acc_ref[...] = jnp.zeros_like(acc_ref)

        @pl.when((i == 0) & (j == 0))
        def _():
            dg_ref[...] = jnp.zeros_like(dg_ref)

        for dz_ref, w_ref in zip(dz_refs, w_refs):
            acc_ref[...] += _dot(dz_ref[0], w_ref[0]) if w_rows else _dot_nt(dz_ref[0], w_ref[0])

        @pl.when(j == nc - 1)
        def _():
            dxn = acc_ref[...]
            dx, xh = _rms_bwd(dxn, x_ref[...], g_ref[...])
            dx_ref[...] = dx + dres_ref[...]
            dg_ref[...] += jnp.sum(dxn * xh, axis=0, keepdims=True)

    row = pl.BlockSpec((tm, D), lambda i, j: (i, 0))
    one = pl.BlockSpec((1, D), lambda i, j: (0, 0))
    in_specs = [pl.BlockSpec((1, tm, dz.shape[2]), lambda i, j: (j, i, 0)) for dz in dzs]
    in_specs += [pl.BlockSpec((1,) + w.shape[1:], lambda i, j: (j, 0, 0)) for w in ws]
    return _pallas(
        body, name="norm_in_bwd", grid=(S // tm, nc),
        in_specs=in_specs + [row, one, row], out_specs=[row, one],
        out_shape=[_sds((S, D), F32), _sds((1, D), F32)],
        scratch_shapes=[pltpu.VMEM((tm, D), F32)],
        compiler_params=_cp(("arbitrary", "arbitrary")),
    )(*dzs, *ws, x, g, dres)


def _norm_proj(x, g, w, w2=None):
    S, D = x.shape
    N = w.shape[1]
    tm = TOK_TILE

    def body(*refs):
        if w2 is None:
            x_ref, g_ref, w_ref, h_ref, z_ref = refs
        else:
            x_ref, g_ref, w_ref, w2_ref, h_ref, z_ref, z2_ref = refs
        xv = x_ref[...]
        h = (xv * _rms(xv) * g_ref[...]).astype(BF16)
        h_ref[...] = h
        z_ref[...] = _dot(h, w_ref[...]).astype(BF16)
        if w2 is not None:
            z2_ref[...] = _dot(h, w2_ref[...])

    row = pl.BlockSpec((tm, D), lambda i: (i, 0))
    in_specs = [row, pl.BlockSpec((1, D), lambda i: (0, 0)), pl.BlockSpec((D, N), lambda i: (0, 0))]
    out_specs = [row, pl.BlockSpec((tm, N), lambda i: (i, 0))]
    out_shape = [_sds((S, D), BF16), _sds((S, N), BF16)]
    args = [x, g, w]
    if w2 is not None:
        N2 = w2.shape[1]
        in_specs.append(pl.BlockSpec((D, N2), lambda i: (0, 0)))
        out_specs.append(pl.BlockSpec((tm, N2), lambda i: (i, 0)))
        out_shape.append(_sds((S, N2), F32))
        args.append(w2)
    return _pallas(body, name="norm_proj", grid=(S // tm,), in_specs=in_specs, out_specs=out_specs,
                   out_shape=out_shape, compiler_params=_cp(("parallel",)))(*args)


def _proj_res(acts, ws, res):
    S, D = res.shape
    n = len(acts)
    tm = TOK_TILE

    def body(*refs):
        a_refs, w_refs = refs[:n], refs[n:2 * n]
        res_ref, out_ref = refs[2 * n:]
        acc = res_ref[...]
        for a_ref, w_ref in zip(a_refs, w_refs):
            acc = acc + _dot(a_ref[...], w_ref[...])
        out_ref[...] = acc

    row = pl.BlockSpec((tm, D), lambda i: (i, 0))
    in_specs = [pl.BlockSpec((tm, a.shape[1]), lambda i: (i, 0)) for a in acts]
    in_specs += [pl.BlockSpec(w.shape, lambda i: (0, 0)) for w in ws]
    return _pallas(body, name="proj_res", grid=(S // tm,), in_specs=in_specs + [row], out_specs=row,
                   out_shape=_sds((S, D), F32), compiler_params=_cp(("parallel",)))(*acts, *ws, res)


def _matmul_nt(a, w, after=None):
    S, K = a.shape
    M = w.shape[0]
    tm = TOK_TILE

    def body(a_ref, w_ref, *rest):
        rest[-1][...] = _dot_nt(a_ref[...].astype(BF16), w_ref[...])

    extra = [] if after is None else [after]
    return _pallas(body, name="matmul_nt", grid=(S // tm,),
                   in_specs=[pl.BlockSpec((tm, K), lambda i: (i, 0)), pl.BlockSpec((M, K), lambda i: (0, 0))] + [ANY] * len(extra),
                   out_specs=pl.BlockSpec((tm, M), lambda i: (i, 0)), out_shape=_sds((S, M), F32),
                   compiler_params=_cp(("parallel",)))(a, w, *extra)


def _matmul_tn(a, b, tn):
    S, M = a.shape
    N = b.shape[1]
    tm = min(DW_TILE, S)
    nt = S // tm

    def body(a_ref, b_ref, o_ref, acc_ref):
        i = pl.program_id(1)

        @pl.when(i == 0)
        def _():
            acc_ref[...] = jnp.zeros_like(acc_ref)

        acc_ref[...] += _dot_tn(a_ref[...].astype(BF16), b_ref[...].astype(BF16))

        @pl.when(i == nt - 1)
        def _():
            o_ref[0] = acc_ref[...].astype(BF16)

    return _pallas(body, name="matmul_tn", grid=(N // tn, nt),
                   in_specs=[pl.BlockSpec((tm, M), lambda j, i: (i, 0)), pl.BlockSpec((tm, tn), lambda j, i: (i, j))],
                   out_specs=pl.BlockSpec((1, M, tn), lambda j, i: (j, 0, 0)), out_shape=_sds((N // tn, M, tn), BF16),
                   scratch_shapes=[pltpu.VMEM((M, tn), F32)],
                   compiler_params=_cp(("parallel", "arbitrary")))(a, b)


def _fill_shifts(win, rows):
    for b in range(1, SUBLANES):
        win[b, pl.ds(0, rows - SUBLANES), :] = win[0, pl.ds(b, rows - SUBLANES), :]


def _tap(win, offset, n, base=0):
    start = base + (offset - offset % SUBLANES)
    if not isinstance(start, int):
        start = pl.multiple_of(start, SUBLANES)
    return win[offset % SUBLANES, pl.ds(start, n), :]


def _conv_a_fwd(z, cw, cb, cn):
    S = z.shape[0]
    C = D_CONV
    tm = TOK_TILE
    hb = tm // HALO_A

    def body(u_ref, gt_ref, up_ref, gp_ref, cw_ref, cb_ref, cn_ref, a_ref, a1_ref, win):
        i = pl.program_id(0)
        prev = up_ref[...].astype(F32) * jax.nn.sigmoid(gp_ref[...].astype(F32))
        win[0, pl.ds(0, HALO_A), :] = jnp.where(i == 0, 0.0, prev)
        win[0, pl.ds(HALO_A, tm), :] = u_ref[...].astype(F32) * jax.nn.sigmoid(gt_ref[...].astype(F32))
        _fill_shifts(win, tm + HALO_A)

        acc = jnp.zeros((tm, C), F32)
        for k in range(CONV_A_WIDTH):
            acc = acc + cw_ref[k:k + 1, :] * _tap(win, HALO_A - (CONV_A_WIDTH - 1) + k, tm)
        a1 = acc + cb_ref[...]
        a1_ref[...] = a1
        a2 = a1 * _rms(a1) * cn_ref[...]
        a_ref[...] = (a2 * jax.nn.sigmoid(a2)).astype(BF16)

    cur = lambda c: pl.BlockSpec((tm, C), lambda i, c=c: (i, c))
    prv = lambda c: pl.BlockSpec((HALO_A, C), lambda i, c=c: (jnp.maximum(i * hb - 1, 0), c))
    vec = pl.BlockSpec((1, C), lambda i: (0, 0))
    return _pallas(body, name="conv_a_fwd", grid=(S // tm,),
                   in_specs=[cur(0), cur(1), prv(0), prv(1), pl.BlockSpec((32, C), lambda i: (0, 0)), vec, vec],
                   out_specs=[pl.BlockSpec((tm, C), lambda i: (i, 0)), pl.BlockSpec((tm, C), lambda i: (i, 0))],
                   out_shape=[_sds((S, C), BF16), _sds((S, C), F32)],
                   scratch_shapes=[pltpu.VMEM((SUBLANES, tm + HALO_A, C), F32)],
                   compiler_params=_cp(("parallel",)))(z, z, z, z, cw, cb, cn)


def _conv_a_bwd(da, a1, z, cw, cn):
    S = z.shape[0]
    C = D_CONV
    tm = TOK_TILE
    hb = tm // HALO_A
    nt = S // tm
    W = CONV_A_WIDTH

    def body(da_ref, a1_ref, dan_ref, a1n_ref, u_ref, gt_ref, up_ref, gp_ref, cw_ref, cn_ref,
             duz_ref, dcw_ref, dcb_ref, dcn_ref, win, dwin):
        i = pl.program_id(0)

        @pl.when(i == 0)
        def _():
            dcw_ref[...] = jnp.zeros_like(dcw_ref)
            dcb_ref[...] = jnp.zeros_like(dcb_ref)
            dcn_ref[...] = jnp.zeros_like(dcn_ref)

        cnv = cn_ref[...]

        def da1_of(dav, a1v):
            a2 = a1v * _rms(a1v) * cnv
            da2 = dav * _silu_grad(a2)
            dx, xh = _rms_bwd(da2, a1v, cnv)
            return dx, da2 * xh

        da1, dcn_t = da1_of(da_ref[...], a1_ref[...])
        da1n, _ = da1_of(dan_ref[...], a1n_ref[...])
        dwin[0, pl.ds(0, tm), :] = da1
        dwin[0, pl.ds(tm, HALO_A), :] = jnp.where(i == nt - 1, 0.0, da1n)
        _fill_shifts(dwin, tm + HALO_A)
        dcb_ref[...] += jnp.sum(da1, axis=0, keepdims=True)
        dcn_ref[...] += jnp.sum(dcn_t, axis=0, keepdims=True)

        prev = up_ref[...].astype(F32) * jax.nn.sigmoid(gp_ref[...].astype(F32))
        win[0, pl.ds(0, HALO_A), :] = jnp.where(i == 0, 0.0, prev)
        win[0, pl.ds(HALO_A, tm), :] = u_ref[...].astype(F32) * jax.nn.sigmoid(gt_ref[...].astype(F32))
        _fill_shifts(win, tm + HALO_A)

        def rows_block(rb, carry):
            r0 = pl.multiple_of(rb * CONV_ROWS, CONV_ROWS)
            rows = pl.ds(r0, CONV_ROWS)
            da1_b = dwin[0, rows, :]
            da0 = jnp.zeros((CONV_ROWS, C), F32)
            for k in range(W):
                da0 = da0 + cw_ref[k:k + 1, :] * _tap(dwin, W - 1 - k, CONV_ROWS, r0)
                dcw_ref[k:k + 1, :] += jnp.sum(da1_b * _tap(win, HALO_A - (W - 1) + k, CONV_ROWS, r0), axis=0, keepdims=True)
            u = u_ref[rows, :].astype(F32)
            sg = jax.nn.sigmoid(gt_ref[rows, :].astype(F32))
            duz_ref[rows, 0:C] = (da0 * sg).astype(BF16)
            duz_ref[rows, C:2 * C] = (da0 * u * sg * (1.0 - sg)).astype(BF16)
            return carry

        lax.fori_loop(0, tm // CONV_ROWS, rows_block, 0)

    cur = lambda c: pl.BlockSpec((tm, C), lambda i, c=c: (i, c))
    prv = lambda c: pl.BlockSpec((HALO_A, C), lambda i, c=c: (jnp.maximum(i * hb - 1, 0), c))
    nxt = pl.BlockSpec((HALO_A, C), lambda i: (jnp.minimum((i + 1) * hb, S // HALO_A - 1), 0))
    vec = pl.BlockSpec((1, C), lambda i: (0, 0))
    return _pallas(body, name="conv_a_bwd", grid=(nt,),
                   in_specs=[cur(0), cur(0), nxt, nxt, cur(0), cur(1), prv(0), prv(1),
                             pl.BlockSpec((32, C), lambda i: (0, 0)), vec],
                   out_specs=[pl.BlockSpec((tm, 2 * C), lambda i: (i, 0)), pl.BlockSpec((32, C), lambda i: (0, 0)), vec, vec],
                   out_shape=[_sds((S, 2 * C), BF16), _sds((32, C), F32), _sds((1, C), F32), _sds((1, C), F32)],
                   scratch_shapes=[pltpu.VMEM((SUBLANES, tm + HALO_A, C), F32)] * 2,
                   compiler_params=_cp(("arbitrary",)))(da, a1, da, a1, z, z, z, z, cw, cn)


def _forget_scan(fl, bf):
    S, L = fl.shape
    B = SCAN_BLK

    def body(fl_ref, bf_ref, flb_ref, F_ref):
        tri = (lax.broadcasted_iota(jnp.int32, (B, B), 0) >= lax.broadcasted_iota(jnp.int32, (B, B), 1)).astype(F32)

        def step(c, carry):
            rows = pl.ds(pl.multiple_of(c * B, B), B)
            v = fl_ref[rows, :] + bf_ref[...]
            flb_ref[rows, :] = v
            lf = jnp.minimum(v, 0.0) - jnp.log1p(jnp.exp(-jnp.abs(v)))
            cs = jnp.dot(tri, lf, precision=lax.Precision.HIGHEST, preferred_element_type=F32) + carry
            F_ref[rows, :] = cs
            return cs[B - 1:B, :]

        lax.fori_loop(0, S // B, step, jnp.zeros((1, L), F32))

    return _pallas(body, name="forget_scan", out_shape=[_sds((S, L), F32), _sds((S, L), F32)],
                   compiler_params=_cp())(fl, bf)


def _forget_scan_bwd(dF, flb):
    S, L = dF.shape
    B = SCAN_BLK
    nb = S // B

    def body(dF_ref, flb_ref, dfl_ref, db_ref):
        tri = (lax.broadcasted_iota(jnp.int32, (B, B), 0) <= lax.broadcasted_iota(jnp.int32, (B, B), 1)).astype(F32)

        def step(t, carry):
            carry_cs, db = carry
            rows = pl.ds(pl.multiple_of((nb - 1 - t) * B, B), B)
            cs = jnp.dot(tri, dF_ref[rows, :], precision=lax.Precision.HIGHEST, preferred_element_type=F32) + carry_cs
            dfl = cs * jax.nn.sigmoid(-flb_ref[rows, :])
            dfl_ref[rows, :] = dfl
            return cs[0:1, :], db + jnp.sum(dfl, axis=0, keepdims=True)

        _, db = lax.fori_loop(0, nb, step, (jnp.zeros((1, L), F32), jnp.zeros((1, L), F32)))
        db_ref[...] = db

    return _pallas(body, name="forget_scan_bwd", out_shape=[_sds((S, L), F32), _sds((1, L), F32)],
                   compiler_params=_cp())(dF, flb)


NEG = -1e30


def _causal_mask(t):
    return lax.broadcasted_iota(jnp.int32, (t, t), 0) >= lax.broadcasted_iota(jnp.int32, (t, t), 1)


AUG = 128
C_F, C_ONE, C_LSE = 64, 67, 70


def _split3(f):
    a = f.astype(BF16).astype(F32)
    r = f - a
    b = r.astype(BF16).astype(F32)
    return a, b, r - b


def _put3(lane, base, parts, other):
    out = other
    for k, p in enumerate(parts):
        out = jnp.where(lane == base + k, p, out)
    return out


def _ones3(lane, base):
    return (lane >= base) & (lane < base + 3)


def _lane_ids(rows):
    return lax.broadcasted_iota(jnp.int32, (rows, AUG), 1)


def _pair_rms(x, lo):
    sq = x * x
    ms_a = jnp.sum(jnp.where(lo, sq, 0.0), axis=-1, keepdims=True) * (1.0 / HEAD_DIM)
    ms_b = jnp.sum(jnp.where(lo, 0.0, sq), axis=-1, keepdims=True) * (1.0 / HEAD_DIM)
    return jnp.where(lo, lax.rsqrt(ms_a + EPS), lax.rsqrt(ms_b + EPS))


def _qkv_prep(z, Fc, qw, kw):
    S = z.shape[0]
    tp = min(QKN_TILE, S)
    scale = 1.0 / math.sqrt(HEAD_DIM)

    def body(zq_ref, zk_ref, zv_ref, F_ref, qw_ref, kw_ref, q_ref, k_ref, v_ref):
        j = pl.program_id(0)
        lane = _lane_ids(tp)
        lo = lane < HEAD_DIM
        Fv = F_ref[...]
        xq = zq_ref[...].astype(F32)
        xk = zk_ref[...].astype(F32)
        qn = xq * _pair_rms(xq, lo) * qw_ref[...] * scale
        kn = xk * _pair_rms(xk, lo) * kw_ref[...]
        vv = zv_ref[...].astype(F32)
        for half in range(2):
            take = (lambda a: a) if half == 0 else (lambda a: pltpu.roll(a, HEAD_DIM, 1))
            fp = _split3(jnp.sum(jnp.where(lane == 2 * j + half, Fv, 0.0), axis=-1, keepdims=True))
            qx = _put3(lane, C_F, fp, jnp.where(_ones3(lane, C_ONE), 1.0, 0.0))
            kx = _put3(lane, C_ONE, [-p for p in fp], jnp.where(_ones3(lane, C_F) | _ones3(lane, C_LSE), 1.0, 0.0))
            vx = jnp.where(_ones3(lane, C_F), 1.0, 0.0)
            q_ref[half] = jnp.where(lo, take(qn), qx).astype(BF16)
            k_ref[half] = jnp.where(lo, take(kn), kx).astype(BF16)
            v_ref[half] = jnp.where(lo, take(vv), vx).astype(BF16)

    col = lambda c0: pl.BlockSpec((tp, AUG), lambda j, i, c0=c0: (i, c0 + j))
    vec = pl.BlockSpec((1, AUG), lambda j, i: (0, 0))
    out = pl.BlockSpec((2, tp, AUG), lambda j, i: (j, i, 0))
    return _pallas(body, name="qkv_prep", grid=(N_HEADS // 2, S // tp),
                   in_specs=[col(8), col(12), col(16), pl.BlockSpec((tp, AUG), lambda j, i: (i, 0)), vec, vec],
                   out_specs=[out, out, out], out_shape=[_sds((N_HEADS, S, AUG), BF16)] * 3,
                   compiler_params=_cp(("parallel", "parallel")))(z, z, z, Fc, qw, kw)


def _fox_fwd(q_aug, k_aug, v_aug):
    H, S, A = q_aug.shape
    t = ATT_TILE
    nq = S // t

    def body(q_ref, k_ref, v_ref, o_ref, q2_ref):
        i = pl.program_id(1)
        q = q_ref[0]

        def tile(j, carry, diag):
            m, acc = carry
            rows = pl.ds(pl.multiple_of(j * t, t), t)
            s = _dot_nt(q, k_ref[0, rows, :])
            if diag:
                s = jnp.where(_causal_mask(t), s, NEG)
            m_new = jnp.maximum(m, jnp.max(s, axis=-1, keepdims=True))
            p = jnp.exp(s - m_new)
            acc = jnp.exp(m - m_new) * acc + _dot(p.astype(BF16), v_ref[0, rows, :])
            return m_new, acc

        init = (jnp.full((t, 1), NEG, F32), jnp.zeros((t, A), F32))
        carry = lax.fori_loop(0, i, lambda j, c: tile(j, c, False), init)
        m, acc = tile(i, carry, True)
        lane = _lane_ids(t)
        l = jnp.sum(jnp.where(lane == C_F, acc, 0.0), axis=-1, keepdims=True)
        o_ref[0] = (acc / l).astype(BF16)
        lse = m + jnp.log(l)
        q2_ref[0] = (q.astype(F32) + _put3(lane, C_LSE, [-p for p in _split3(lse)], 0.0)).astype(BF16)

    qblk = pl.BlockSpec((1, t, A), lambda h, i: (h, i, 0))
    full = pl.BlockSpec((1, S, A), lambda h, i: (h, 0, 0))
    return _pallas(body, name="fox_fwd", grid=(H, nq), in_specs=[qblk, full, full], out_specs=[qblk, qblk],
                   out_shape=[_sds((H, S, A), BF16)] * 2, compiler_params=_cp(("parallel", "parallel")))(q_aug, k_aug, v_aug)


def _do_prep(dcat, o_aug):
    S = dcat.shape[0]
    tp = min(QKN_TILE, S)

    def body(d_ref, o_ref, out_ref):
        lane = _lane_ids(tp)
        lo = lane < HEAD_DIM
        x = d_ref[...]
        for half in range(2):
            d = jnp.where(lo, x if half == 0 else pltpu.roll(x, HEAD_DIM, 1), 0.0)
            delta = jnp.sum(d * o_ref[half].astype(F32), axis=-1, keepdims=True)
            out_ref[half] = jnp.where(lo, d, _put3(lane, C_F, [-p for p in _split3(delta)], 0.0)).astype(BF16)

    pair = pl.BlockSpec((2, tp, AUG), lambda j, i: (j, i, 0))
    return _pallas(body, name="do_prep", grid=(N_HEADS // 2, S // tp),
                   in_specs=[pl.BlockSpec((tp, AUG), lambda j, i: (i, D_CONV // AUG + j)), pair], out_specs=pair,
                   out_shape=_sds((N_HEADS, S, AUG), BF16), compiler_params=_cp(("parallel", "parallel")))(dcat, o_aug)


def _fox_bwd(q2, k_aug, v_aug, do_aug):
    H, S, A = q2.shape
    t = ATT_TILE
    nq = S // t

    def body(q_ref, k_ref, v_ref, do_ref, dq_ref, dk_ref, dv_ref):
        j = pl.program_id(1)

        @pl.when(j == 0)
        def _():
            dq_ref[...] = jnp.zeros_like(dq_ref)

        k = k_ref[0]
        vv = v_ref[0]

        def tile(i, carry, diag):
            dk, dv = carry
            rows = pl.ds(pl.multiple_of(i * t, t), t)
            q = q_ref[0, rows, :]
            dov = do_ref[0, rows, :]
            s = _dot_nt(q, k)
            if diag:
                s = jnp.where(_causal_mask(t), s, NEG)
            p = jnp.exp(s)
            dv = dv + _dot_tn(p.astype(BF16), dov)
            dsb = (p * _dot_nt(dov, vv)).astype(BF16)
            dq_ref[0, rows, :] += _dot(dsb, k)
            dk = dk + _dot_tn(dsb, q)
            return dk, dv

        init = (jnp.zeros((t, A), F32), jnp.zeros((t, A), F32))
        carry = tile(j, init, True)
        dk, dv = lax.fori_loop(j + 1, nq, lambda i, c: tile(i, c, False), carry)
        dk_ref[0] = dk
        dv_ref[0] = dv

    full = pl.BlockSpec((1, S, A), lambda h, j: (h, 0, 0))
    kblk = pl.BlockSpec((1, t, A), lambda h, j: (h, j, 0))
    return _pallas(body, name="fox_bwd", grid=(H, nq), in_specs=[full, kblk, kblk, full], out_specs=[full, kblk, kblk],
                   out_shape=[_sds((H, S, A), F32)] * 3,
                   compiler_params=_cp(("parallel", "arbitrary")))(q2, k_aug, v_aug, do_aug)


def _qkv_bwd(dq, dk, dv, z, qw, kw):
    S = z.shape[0]
    tp = min(QKN_TILE, S)
    scale = 1.0 / math.sqrt(HEAD_DIM)

    def body(dq_ref, dk_ref, dv_ref, zq_ref, zk_ref, qw_ref, kw_ref, dqf_ref, dkf_ref, dvf_ref, dF_ref, dqw_ref, dkw_ref):
        i, j = pl.program_id(0), pl.program_id(1)
        lane = _lane_ids(tp)
        lo = lane < HEAD_DIM

        @pl.when((i == 0) & (j == 0))
        def _():
            dqw_ref[...] = jnp.zeros_like(dqw_ref)
            dkw_ref[...] = jnp.zeros_like(dkw_ref)

        def pair(ref):
            return jnp.where(lo, ref[0], pltpu.roll(ref[1], HEAD_DIM, 1))

        def norm_bwd(g, x, w):
            r = _pair_rms(x, lo)
            xh = x * r
            dxh = g * w
            tt = dxh * xh
            mean_a = jnp.sum(jnp.where(lo, tt, 0.0), axis=-1, keepdims=True) * (1.0 / HEAD_DIM)
            mean_b = jnp.sum(jnp.where(lo, 0.0, tt), axis=-1, keepdims=True) * (1.0 / HEAD_DIM)
            return r * (dxh - xh * jnp.where(lo, mean_a, mean_b)), g * xh

        dxq, gq = norm_bwd(pair(dq_ref) * scale, zq_ref[...].astype(F32), qw_ref[...])
        dqf_ref[...] = dxq.astype(BF16)
        dqw_ref[...] += jnp.sum(gq, axis=0, keepdims=True)
        dxk, gk = norm_bwd(pair(dk_ref), zk_ref[...].astype(F32), kw_ref[...])
        dkf_ref[...] = dxk.astype(BF16)
        dkw_ref[...] += jnp.sum(gk, axis=0, keepdims=True)
        dvf_ref[...] = pair(dv_ref).astype(BF16)

        contrib = jnp.zeros((tp, AUG), F32)
        for half in range(2):
            df = (jnp.sum(jnp.where(lane == C_F, dq_ref[half], 0.0), axis=-1, keepdims=True)
                  - jnp.sum(jnp.where(lane == C_ONE, dk_ref[half], 0.0), axis=-1, keepdims=True))
            contrib = jnp.where(lane == 2 * j + half, df, contrib)

        @pl.when(j == 0)
        def _():
            dF_ref[...] = contrib

        @pl.when(j > 0)
        def _():
            dF_ref[...] += contrib

    pairb = pl.BlockSpec((2, tp, AUG), lambda i, j: (j, i, 0))
    col = lambda c0: pl.BlockSpec((tp, AUG), lambda i, j, c0=c0: (i, c0 + j))
    vec = pl.BlockSpec((1, AUG), lambda i, j: (0, 0))
    flat = pl.BlockSpec((tp, AUG), lambda i, j: (i, j))
    return _pallas(body, name="qkv_bwd", grid=(S // tp, N_HEADS // 2),
                   in_specs=[pairb, pairb, pairb, col(8), col(12), vec, vec],
                   out_specs=[flat, flat, flat, pl.BlockSpec((tp, AUG), lambda i, j: (i, 0)), vec, vec],
                   out_shape=[_sds((S, D_ATTN), BF16)] * 3 + [_sds((S, AUG), F32), _sds((1, AUG), F32), _sds((1, AUG), F32)],
                   compiler_params=_cp(("arbitrary", "arbitrary")))(dq, dk, dv, z, z, qw, kw)


def _proj_res_heads(a, wa, o_aug, wo, res):
    S, D = res.shape
    H = o_aug.shape[0]
    tm = TOK_TILE

    def body(a_ref, wa_ref, o_ref, wo_ref, res_ref, out_ref):
        acc = res_ref[...] + _dot(a_ref[...], wa_ref[...])
        for h in range(H):
            acc = acc + _dot(o_ref[h], wo_ref[h])
        out_ref[...] = acc

    row = pl.BlockSpec((tm, D), lambda i: (i, 0))
    return _pallas(body, name="proj_res_heads", grid=(S // tm,),
                   in_specs=[pl.BlockSpec((tm, a.shape[1]), lambda i: (i, 0)), pl.BlockSpec(wa.shape, lambda i: (0, 0)),
                             pl.BlockSpec((H, tm, AUG), lambda i: (0, i, 0)), pl.BlockSpec(wo.shape, lambda i: (0, 0, 0)), row],
                   out_specs=row, out_shape=_sds((S, D), F32), compiler_params=_cp(("parallel",)))(a, wa, o_aug, wo, res)


def _heads_tn(o_aug, d):
    H, S, A = o_aug.shape
    D = d.shape[1]
    tm = min(DW_TILE, S)
    nt = S // tm

    def body(o_ref, d_ref, out_ref, acc_ref):
        i = pl.program_id(0)

        @pl.when(i == 0)
        def _():
            acc_ref[...] = jnp.zeros_like(acc_ref)

        dv = d_ref[...].astype(BF16)
        for h in range(H):
            acc_ref[h] += _dot_tn(o_ref[h], dv)

        @pl.when(i == nt - 1)
        def _():
            out_ref[...] = acc_ref[...].astype(BF16)

    return _pallas(body, name="heads_tn", grid=(nt,),
                   in_specs=[pl.BlockSpec((H, tm, A), lambda i: (0, i, 0)), pl.BlockSpec((tm, D), lambda i: (i, 0))],
                   out_specs=pl.BlockSpec((H, A, D), lambda i: (0, 0, 0)), out_shape=_sds((H, A, D), BF16),
                   scratch_shapes=[pltpu.VMEM((H, A, D), F32)], compiler_params=_cp(("arbitrary",)))(o_aug, d)


def _odd_mid_fwd(z, cw):
    S = z.shape[0]
    D = z.shape[1] // 3
    tm = TOK_TILE
    hb = tm // HALO_C
    W = CONV_C_WIDTH

    def body(gb_ref, gc_ref, hh_ref, gcp_ref, hhp_ref, cw_ref, y_ref, win):
        i = pl.program_id(0)
        prev = gcp_ref[...].astype(F32) * hhp_ref[...].astype(F32)
        win[pl.ds(0, HALO_C), :] = jnp.where(i == 0, 0.0, prev)
        win[pl.ds(HALO_C, tm), :] = gc_ref[...].astype(F32) * hh_ref[...].astype(F32)
        c1 = jnp.zeros((tm, D), F32)
        for k in range(W):
            c1 = c1 + cw_ref[k:k + 1, :] * win[pl.ds(HALO_C - (W - 1) + k, tm), :]
        y_ref[...] = (gb_ref[...].astype(F32) * c1).astype(BF16)

    cur = lambda c: pl.BlockSpec((tm, D), lambda i, c=c: (i, c))
    prv = lambda c: pl.BlockSpec((HALO_C, D), lambda i, c=c: (jnp.maximum(i * hb - 1, 0), c))
    return _pallas(body, name="odd_mid_fwd", grid=(S // tm,),
                   in_specs=[cur(0), cur(1), cur(2), prv(1), prv(2), pl.BlockSpec((8, D), lambda i: (0, 0))],
                   out_specs=pl.BlockSpec((tm, D), lambda i: (i, 0)), out_shape=_sds((S, D), BF16),
                   scratch_shapes=[pltpu.VMEM((tm + HALO_C, D), F32)],
                   compiler_params=_cp(("parallel",)))(z, z, z, z, z, cw)


def _odd_mid_bwd(dy, z, cw):
    S = z.shape[0]
    D = z.shape[1] // 3
    tm = TOK_TILE
    hb = tm // HALO_C
    nt = S // tm
    W = CONV_C_WIDTH

    def body(dy_ref, dyn_ref, gb_ref, gbn_ref, gc_ref, hh_ref, gcp_ref, hhp_ref, cw_ref, dz_ref, dcw_ref, win, dwin):
        i = pl.program_id(0)

        @pl.when(i == 0)
        def _():
            dcw_ref[...] = jnp.zeros_like(dcw_ref)

        gc = gc_ref[...].astype(F32)
        hh = hh_ref[...].astype(F32)
        prev = gcp_ref[...].astype(F32) * hhp_ref[...].astype(F32)
        win[pl.ds(0, HALO_C), :] = jnp.where(i == 0, 0.0, prev)
        win[pl.ds(HALO_C, tm), :] = gc * hh
        dyv = dy_ref[...]
        dc1 = dyv * gb_ref[...].astype(F32)
        dwin[pl.ds(0, tm), :] = dc1
        dwin[pl.ds(tm, HALO_C), :] = jnp.where(i == nt - 1, 0.0, dyn_ref[...] * gbn_ref[...].astype(F32))
        c1 = jnp.zeros((tm, D), F32)
        dc0 = jnp.zeros((tm, D), F32)
        for k in range(W):
            tap = win[pl.ds(HALO_C - (W - 1) + k, tm), :]
            c1 = c1 + cw_ref[k:k + 1, :] * tap
            dc0 = dc0 + cw_ref[k:k + 1, :] * dwin[pl.ds(W - 1 - k, tm), :]
            dcw_ref[k:k + 1, :] += jnp.sum(dc1 * tap, axis=0, keepdims=True)
        dz_ref[:, 0:D] = (dyv * c1).astype(BF16)
        dz_ref[:, D:2 * D] = (dc0 * hh).astype(BF16)
        dz_ref[:, 2 * D:3 * D] = (dc0 * gc).astype(BF16)

    cur = lambda c: pl.BlockSpec((tm, D), lambda i, c=c: (i, c))
    prv = lambda c: pl.BlockSpec((HALO_C, D), lambda i, c=c: (jnp.maximum(i * hb - 1, 0), c))
    nxt = pl.BlockSpec((HALO_C, D), lambda i: (jnp.minimum((i + 1) * hb, S // HALO_C - 1), 0))
    return _pallas(body, name="odd_mid_bwd", grid=(nt,),
                   in_specs=[cur(0), nxt, cur(0), nxt, cur(1), cur(2), prv(1), prv(2), pl.BlockSpec((8, D), lambda i: (0, 0))],
                   out_specs=[pl.BlockSpec((tm, 3 * D), lambda i: (i, 0)), pl.BlockSpec((8, D), lambda i: (0, 0))],
                   out_shape=[_sds((S, 3 * D), BF16), _sds((8, D), F32)],
                   scratch_shapes=[pltpu.VMEM((tm + HALO_C, D), F32), pltpu.VMEM((tm + HALO_C, D), F32)],
                   compiler_params=_cp(("arbitrary",)))(dy, dy, z, z, z, z, z, z, cw)


def _loss_head(y, tgt):
    S, D = y.shape
    tm = TOK_TILE

    def body(y_ref, t_ref, dy_ref, l_ref):
        @pl.when(pl.program_id(0) == 0)
        def _():
            l_ref[...] = jnp.zeros_like(l_ref)

        e = y_ref[...] - t_ref[...]
        dy_ref[...] = e * (1.0 / D)
        l_ref[...] += jnp.sum(jnp.sum(e * e, axis=-1, keepdims=True), axis=0, keepdims=True) * (0.5 / D)

    row = pl.BlockSpec((tm, D), lambda i: (i, 0))
    return _pallas(body, name="loss_head", grid=(S // tm,), in_specs=[row, row],
                   out_specs=[row, pl.BlockSpec((1, 1), lambda i: (0, 0))],
                   out_shape=[_sds((S, D), F32), _sds((1, 1), F32)],
                   compiler_params=_cp(("arbitrary",)))(y, tgt)


def _pad_rows(a, rows):
    return jnp.pad(a, ((0, rows - a.shape[0]), (0, 0)))


def _local_step(x, tgt, W, need=lambda block, after: None, done=lambda block, block_grads: None):
    S, D = x.shape
    grads = {}
    saved = {}

    def gain_after(gain, token):
        return gain if token is None else gain + token

    def ffn_f(tag, l, xin):
        need((tag, l), xin)
        out, xn, G, U = _ffn_fwd(xin, W[tag + "_norm"][l:l + 1], W[tag + "_w_gate"][l], W[tag + "_w_up"][l],
                                 W[tag + "_w_down"][l])
        saved[(tag, l)] = (xin, xn, G, U)
        return out

    def ffn_b(tag, l, dout):
        xin, xn, G, U = saved[(tag, l)]
        keys = [(tag + "_w_gate", l), (tag + "_w_up", l), (tag + "_w_down", l)]
        *dws, dG, dU = _ffn_bwd_w(dout, xn, G, U, W[tag + "_w_down"][l])
        big = dict(zip(keys, dws))
        grads.update(big)
        token = done((tag, l), big)
        dx, dg = _norm_in_bwd([dG, dU], [W[tag + "_w_gate"][l], W[tag + "_w_up"][l]], xin,
                              gain_after(W[tag + "_norm"][l:l + 1], token), dout, w_rows=True)
        grads[(tag + "_norm", l)] = dg
        return dx

    x0a = ffn_f("ffn1", 0, x)
    need(("ev", 0), x0a)
    w_in = W["ev_w_in"]
    w_main, w_f = w_in[:, :2560], jnp.pad(w_in[:, 2560:], ((0, 0), (0, 120)))
    h0, z0, fl = _norm_proj(x0a, W["mix_norm"][0:1], w_main, w_f)
    cw_a = _pad_rows(W["ev_conv_w"], 32)
    a_act, a1 = _conv_a_fwd(z0, cw_a, W["ev_conv_b"], W["ev_conv_norm"])
    flb, Fc = _forget_scan(fl, jnp.pad(W["ev_b_f"], ((0, 0), (0, 120))))
    qw2, kw2 = jnp.tile(W["ev_q_norm"], (1, 2)), jnp.tile(W["ev_k_norm"], (1, 2))
    q_aug, k_aug, v_aug = _qkv_prep(z0, Fc, qw2, kw2)
    o_aug, q_lse = _fox_fwd(q_aug, k_aug, v_aug)
    w_out_e = W["ev_w_out"]
    w_out_o = jnp.pad(w_out_e[D_CONV:].reshape(N_HEADS, HEAD_DIM, D), ((0, 0), (0, AUG - HEAD_DIM), (0, 0)))
    x0b = _proj_res_heads(a_act, w_out_e[:D_CONV], o_aug, w_out_o, x0a)
    x0c = ffn_f("ffn2", 0, x0b)
    x1a = ffn_f("ffn1", 1, x0c)
    need(("od", 0), x1a)
    h1, z1 = _norm_proj(x1a, W["mix_norm"][1:2], W["od_w_in"])
    cw_c = _pad_rows(W["od_conv_w"], 8)
    y1 = _odd_mid_fwd(z1, cw_c)
    x1b = _proj_res([y1], [W["od_w_out"]], x1a)
    x1c = ffn_f("ffn2", 1, x1b)
    dy, loss = _loss_head(x1c, tgt)

    d = ffn_b("ffn2", 1, dy)
    dy1 = _matmul_nt(d, W["od_w_out"])
    grads[("od_w_out", 0)] = _matmul_tn(y1, d, D)[0]
    dz1, dcw_c = _odd_mid_bwd(dy1, z1, cw_c)
    grads[("od_conv_w", 0)] = dcw_c[:CONV_C_WIDTH]
    grads[("od_w_in", 0)] = _matmul_tn(h1, dz1, 3 * D // 4)
    token = done(("od", 0), {k: grads[k] for k in (("od_w_out", 0), ("od_w_in", 0))})
    d, dg = _norm_in_bwd([dz1[None]], [W["od_w_in"][None]], x1a, gain_after(W["mix_norm"][1:2], token), d)
    grads[("mix_norm", 1)] = dg
    d = ffn_b("ffn1", 1, d)
    d = ffn_b("ffn2", 0, d)
    dcat = _matmul_nt(d, w_out_e)
    grads[("ev_w_out", 0)] = jnp.concatenate([_matmul_tn(a_act, d, D)[0],
                                              _heads_tn(o_aug, d)[:, :HEAD_DIM].reshape(D_ATTN, D)], axis=0)
    duz, dcw_a, dcb, dcn = _conv_a_bwd(dcat, a1, z0, cw_a, W["ev_conv_norm"])
    grads[("ev_conv_w", 0)] = dcw_a[:CONV_A_WIDTH]
    grads[("ev_conv_b", 0)] = dcb
    grads[("ev_conv_norm", 0)] = dcn
    dq_a, dk_a, dv_a = _fox_bwd(q_lse, k_aug, v_aug, _do_prep(dcat, o_aug))
    dqf, dkf, dvf, dF, dqw, dkw = _qkv_bwd(dq_a, dk_a, dv_a, z0, qw2, kw2)
    grads[("ev_q_norm", 0)] = dqw[:, :HEAD_DIM] + dqw[:, HEAD_DIM:]
    grads[("ev_k_norm", 0)] = dkw[:, :HEAD_DIM] + dkw[:, HEAD_DIM:]
    dfl, dbf = _forget_scan_bwd(dF, flb)
    grads[("ev_b_f", 0)] = dbf[:, :N_HEADS]
    dz0 = jnp.concatenate([duz, dqf, dkf, dvf], axis=1)
    dflb = dfl.astype(BF16)
    gmain = _matmul_tn(h0, dz0, 640)
    gmain = gmain.transpose(1, 0, 2).reshape(D, 2560)
    gf = _matmul_tn(h0, dflb, 128)[0][:, :N_HEADS]
    grads[("ev_w_in", 0)] = jnp.concatenate([gmain, gf], axis=1)
    token = done(("ev", 0), {k: grads[k] for k in (("ev_w_out", 0), ("ev_w_in", 0))})
    d, dg = _norm_in_bwd([dz0[None], dflb[None]], [w_main[None], w_f[None]], x0a, gain_after(W["mix_norm"][0:1], token), d)
    grads[("mix_norm", 0)] = dg
    d = ffn_b("ffn1", 0, d)
    return loss, d, grads


def _place():
    x, y, c = lax.axis_index("x"), lax.axis_index("y"), lax.axis_index("c")
    chips = [(1 - x, y), (x, 1 - y), (1 - x, 1 - y)]
    return x, y, c, chips


def _remote(src, dst, send_sem, recv_sem, to):
    return pltpu.make_async_remote_copy(src_ref=src, dst_ref=dst, send_sem=send_sem, recv_sem=recv_sem,
                                        device_id=to, device_id_type=MESH)


HBM = pl.BlockSpec(memory_space=pltpu.HBM)
SEM = pl.BlockSpec(memory_space=pltpu.SEMAPHORE)
EFFECT = pltpu.SideEffectType.DATAFLOW_SIDE_EFFECTING


def _in_hbm(a):
    return pltpu.with_memory_space_constraint(a, pltpu.HBM)


def _ag_start(tag, bufs, with_taps):
    n = len(bufs)
    order = ([n - 1] + list(range(n - 1))) if with_taps else list(range(n))

    def body(*refs):
        send_sems, recv_sems = refs[n], refs[n + 1]
        outs, token = refs[n + 2:2 * n + 2], refs[2 * n + 2]
        x, y, c, chips = _place()
        me = 2 * x + y
        for a in order:
            if with_taps and a == n - 1:
                blk = outs[a].at[me]
            else:
                h = outs[a].shape[1] // 2
                blk = outs[a].at[me, pl.ds(c * h, h)]
            for jj, (px, py) in enumerate(chips):
                _remote(blk, blk, send_sems.at[3 * a + jj], recv_sems.at[3 * a + jj], (px, py, c)).start()
        token[...] = jnp.zeros_like(token)

    return _pallas(
        body, name=f"gather_start_{tag}",
        out_shape=[pltpu.SemaphoreType.DMA((3 * n,)), pltpu.SemaphoreType.DMA((3 * n,))]
        + [pltpu.HBM(b.shape, b.dtype) for b in bufs] + [_sds((8, 128), F32)],
        in_specs=[HBM] * n, out_specs=[SEM, SEM] + [HBM] * n + [pl.BlockSpec(memory_space=pltpu.VMEM)],
        input_output_aliases={a: 2 + a for a in range(n)},
        compiler_params=pltpu.CompilerParams(has_side_effects=EFFECT),
    )(*[_in_hbm(b) for b in bufs])


def _ag_mid(g, ici_send, ici_recv, bufs, idx, taps, n_big, after):
    n = len(bufs)
    arrs = list(bufs) + ([taps] if taps is not None else [])
    m = len(arrs)

    def body(*refs):
        ici_s, ici_r = refs[0], refs[1]
        d_send, d_recv = refs[m + 3], refs[m + 4]
        outs = refs[m + 5:]
        x, y, c, chips = _place()
        me = 2 * x + y
        for i in range(m):
            a = idx[i] if i < n else n_big
            for jj, (px, py) in enumerate(chips):
                k = 3 * a + jj
                if i < n:
                    h = outs[i].shape[1] // 2
                    mine, blk = outs[i].at[me, pl.ds(c * h, h)], outs[i].at[2 * px + py, pl.ds(c * h, h)]
                else:
                    mine, blk = outs[i].at[me], outs[i].at[2 * px + py]
                _remote(mine, mine, ici_s.at[k], ici_r.at[k], (px, py, c)).wait_send()
                _remote(blk, blk, ici_s.at[k], ici_r.at[k], (px, py, c)).wait_recv()
                if i < n:
                    _remote(blk, blk, d_send.at[3 * i + jj], d_recv.at[3 * i + jj], (x, y, 1 - c)).start()

    return _pallas(
        body, name=f"gather_pass_on_{g}",
        out_shape=[pltpu.SemaphoreType.DMA((3 * n,)), pltpu.SemaphoreType.DMA((3 * n,))] + [pltpu.HBM(b.shape, b.dtype) for b in arrs],
        in_specs=[SEM, SEM] + [HBM] * m + [ANY], out_specs=[SEM, SEM] + [HBM] * m,
        input_output_aliases={2 + i: 2 + i for i in range(m)},
        compiler_params=pltpu.CompilerParams(has_side_effects=EFFECT),
    )(ici_send, ici_recv, *arrs, after)


def _ag_wait(g, d_send, d_recv, arrs, n, after):
    m = len(arrs)

    def body(*refs):
        d_s, d_r = refs[0], refs[1]
        outs = refs[m + 3:]
        x, y, c, chips = _place()
        for i in range(n):
            h = outs[i].shape[1] // 2
            for jj, (px, py) in enumerate(chips):
                sent = outs[i].at[2 * px + py, pl.ds(c * h, h)]
                got = outs[i].at[2 * px + py, pl.ds((1 - c) * h, h)]
                _remote(sent, sent, d_s.at[3 * i + jj], d_r.at[3 * i + jj], (x, y, 1 - c)).wait_send()
                _remote(got, got, d_s.at[3 * i + jj], d_r.at[3 * i + jj], (x, y, 1 - c)).wait_recv()

    return _pallas(
        body, name=f"gather_wait_{g}", out_shape=[pltpu.HBM(b.shape, b.dtype) for b in arrs],
        in_specs=[SEM, SEM] + [HBM] * m + [ANY], out_specs=[HBM] * m,
        input_output_aliases={2 + i: i for i in range(m)},
        compiler_params=pltpu.CompilerParams(has_side_effects=EFFECT),
    )(d_send, d_recv, *arrs, after)


def _pair_start(g, gs, after):
    n = len(gs)
    zones = [lax.empty((4, a.shape[1] // 2, a.shape[2]), a.dtype) for a in gs]
    extra = [] if after is None else [after]

    def body(*refs):
        k0 = 2 * n + len(extra)
        send_sems, recv_sems = refs[k0], refs[k0 + 1]
        src, dst = refs[k0 + 2:k0 + 2 + n], refs[k0 + 2 + n:k0 + 2 + 2 * n]
        token = refs[k0 + 2 + 2 * n]
        x, y, c, _ = _place()
        for a in range(n):
            h = src[a].shape[1] // 2
            _remote(src[a].at[:, pl.ds((1 - c) * h, h)], dst[a], send_sems.at[a], recv_sems.at[a], (x, y, 1 - c)).start()
        token[...] = jnp.zeros_like(token)

    return _pallas(
        body, name=f"grad_pair_start_{g}",
        out_shape=[pltpu.SemaphoreType.DMA((n,)), pltpu.SemaphoreType.DMA((n,))]
        + [pltpu.HBM(a.shape, a.dtype) for a in gs + zones] + [_sds((8, 128), F32)],
        in_specs=[HBM] * (2 * n) + [ANY] * len(extra),
        out_specs=[SEM, SEM] + [HBM] * (2 * n) + [pl.BlockSpec(memory_space=pltpu.VMEM)],
        input_output_aliases={i: 2 + i for i in range(2 * n)},
        compiler_params=pltpu.CompilerParams(has_side_effects=EFFECT),
    )(*[_in_hbm(a) for a in gs + zones], *extra)


def _pair_wait(g, send, recv, gs, zones):
    n = len(gs)

    def body(*refs):
        s_ref, r_ref = refs[0], refs[1]
        outs = refs[2 + 2 * n:]
        src, dst = outs[:n], outs[n:]
        x, y, c, _ = _place()
        for a in range(n):
            h = src[a].shape[1] // 2
            _remote(src[a].at[:, pl.ds((1 - c) * h, h)], dst[a], s_ref.at[a], r_ref.at[a], (x, y, 1 - c)).wait()

    return _pallas(
        body, name=f"grad_pair_wait_{g}", out_shape=[pltpu.HBM(a.shape, a.dtype) for a in gs + zones],
        in_specs=[SEM, SEM] + [HBM] * (2 * n), out_specs=[HBM] * (2 * n),
        input_output_aliases={2 + i: i for i in range(2 * n)},
        compiler_params=pltpu.CompilerParams(has_side_effects=EFFECT),
    )(send, recv, *gs, *zones)


def _pair_add(gs, others, c_arr):
    n = len(gs)

    def body(c_ref, *refs):
        for g_ref, o_ref, out_ref in zip(refs[:n], refs[n:2 * n], refs[2 * n:]):
            out_ref[...] = (g_ref[...].astype(F32) + o_ref[...].astype(F32)).astype(BF16)

    half = lambda a: pl.BlockSpec((1, a.shape[1] // 4, a.shape[2]), lambda k, i, c_ref: (k, 2 * c_ref[0] + i, 0))
    whole = lambda a: pl.BlockSpec((1, a.shape[1] // 2, a.shape[2]), lambda k, i, c_ref: (k, i, 0))
    grid_spec = pltpu.PrefetchScalarGridSpec(
        num_scalar_prefetch=1, grid=(4, 2), in_specs=[half(a) for a in gs] + [whole(o) for o in others],
        out_specs=[whole(o) for o in others])
    return _pallas(body, name="grad_pair_add", grid_spec=grid_spec, out_shape=[_sds(o.shape, BF16) for o in others],
                   compiler_params=_cp(("parallel", "parallel")))(c_arr, *gs, *others)


def _chip_start(g, ss):
    n = len(ss)
    zones = [lax.empty((3,) + s.shape[1:], s.dtype) for s in ss]

    def body(*refs):
        send_sems, recv_sems = refs[2 * n], refs[2 * n + 1]
        src, dst = refs[2 * n + 2:3 * n + 2], refs[3 * n + 2:4 * n + 2]
        token = refs[4 * n + 2]
        x, y, c, chips = _place()
        for a in range(n):
            for jj, (px, py) in enumerate(chips):
                k = 3 * a + jj
                _remote(src[a].at[2 * px + py], dst[a].at[jj], send_sems.at[k], recv_sems.at[k], (px, py, c)).start()
        token[...] = jnp.zeros_like(token)

    return _pallas(
        body, name=f"grad_chip_start_{g}",
        out_shape=[pltpu.SemaphoreType.DMA((3 * n,)), pltpu.SemaphoreType.DMA((3 * n,))]
        + [pltpu.HBM(a.shape, a.dtype) for a in ss + zones] + [_sds((8, 128), F32)],
        in_specs=[HBM] * (2 * n), out_specs=[SEM, SEM] + [HBM] * (2 * n) + [pl.BlockSpec(memory_space=pltpu.VMEM)],
        input_output_aliases={i: 2 + i for i in range(2 * n)},
        compiler_params=pltpu.CompilerParams(has_side_effects=EFFECT),
    )(*[_in_hbm(a) for a in ss + zones])


def _chip_wait(tag, sends, recvs, counts, ss, zones, after):
    nb, n = len(sends), len(ss)

    def body(*refs):
        s_refs, r_refs = refs[:nb], refs[nb:2 * nb]
        outs = refs[2 * nb + 2 * n + 1:]
        src, dst = outs[:n], outs[n:]
        x, y, c, chips = _place()
        a = 0
        for b in range(nb):
            for i in range(counts[b]):
                for jj, (px, py) in enumerate(chips):
                    k = 3 * i + jj
                    _remote(src[a].at[2 * px + py], dst[a].at[jj], s_refs[b].at[k], r_refs[b].at[k], (px, py, c)).wait()
                a += 1

    return _pallas(
        body, name=f"grad_chip_wait_{tag}", out_shape=[pltpu.HBM(a.shape, a.dtype) for a in ss + zones],
        in_specs=[SEM] * (2 * nb) + [HBM] * (2 * n) + [ANY], out_specs=[HBM] * (2 * n),
        input_output_aliases={2 * nb + i: i for i in range(2 * n)},
        compiler_params=pltpu.CompilerParams(has_side_effects=EFFECT),
    )(*sends, *recvs, *ss, *zones, after)


def _chip_sum(s, r, where, dest, l, L):
    _, h, C = s.shape
    tr = h // 2

    def body(k_ref, s_ref, r_ref, *rest):
        out_ref = rest[-1]
        acc = s_ref[0].astype(F32)
        for jj in range(3):
            acc = acc + r_ref[jj].astype(F32)
        out_ref[...] = acc

    in_specs = [pl.BlockSpec((1, tr, C), lambda i, k_ref: (k_ref[0], i, 0)), pl.BlockSpec((3, tr, C), lambda i, k_ref: (0, i, 0))]
    args = [where, s, r]
    alias = {}
    if dest is not None:
        in_specs.append(ANY)
        args.append(dest)
        alias = {3: 0}
    grid_spec = pltpu.PrefetchScalarGridSpec(
        num_scalar_prefetch=1, grid=(2,), in_specs=in_specs,
        out_specs=pl.BlockSpec((None, tr, C), lambda i, k_ref: (l, 2 * k_ref[1] + i, 0)))
    return _pallas(body, name="grad_chip_sum", grid_spec=grid_spec, out_shape=_sds((L, 2 * h, C), F32),
                   input_output_aliases=alias, compiler_params=_cp(("arbitrary",)))(*args)


def _share_start(tag, bufs, layout):
    n, n_buf = len(layout), len(bufs)

    def body(*refs):
        send_sems, recv_sems = refs[n_buf], refs[n_buf + 1]
        outs = refs[n_buf + 2:]
        x, y, c, _ = _place()
        for a, (o, l) in enumerate(layout):
            h = outs[o].shape[1] // 2
            blk = outs[o].at[l, pl.ds(c * h, h)]
            _remote(blk, blk, send_sems.at[a], recv_sems.at[a], (x, y, 1 - c)).start()

    return _pallas(
        body, name=f"grad_share_start_{tag}",
        out_shape=[pltpu.SemaphoreType.DMA((n,)), pltpu.SemaphoreType.DMA((n,))] + [pltpu.HBM(b.shape, b.dtype) for b in bufs],
        in_specs=[HBM] * n_buf, out_specs=[SEM, SEM] + [HBM] * n_buf, input_output_aliases={o: 2 + o for o in range(n_buf)},
        compiler_params=pltpu.CompilerParams(has_side_effects=EFFECT),
    )(*[_in_hbm(b) for b in bufs])


def _share_wait(tag, send, recv, bufs, layout, after):
    n_buf = len(bufs)

    def body(*refs):
        s_ref, r_ref = refs[0], refs[1]
        outs = refs[n_buf + 3:]
        x, y, c, _ = _place()
        for a, (o, l) in enumerate(layout):
            h = outs[o].shape[1] // 2
            mine, theirs = outs[o].at[l, pl.ds(c * h, h)], outs[o].at[l, pl.ds((1 - c) * h, h)]
            _remote(mine, mine, s_ref.at[a], r_ref.at[a], (x, y, 1 - c)).wait_send()
            _remote(theirs, theirs, s_ref.at[a], r_ref.at[a], (x, y, 1 - c)).wait_recv()

    return _pallas(
        body, name=f"grad_share_wait_{tag}", out_shape=[pltpu.HBM(b.shape, b.dtype) for b in bufs],
        in_specs=[SEM, SEM] + [HBM] * n_buf + [ANY], out_specs=[HBM] * n_buf,
        input_output_aliases={2 + o: o for o in range(n_buf)},
        compiler_params=pltpu.CompilerParams(has_side_effects=EFFECT),
    )(send, recv, *bufs, after)


def _small_all_reduce(packed, after):
    P, L = packed.shape

    def body(in_ref, after_ref, out_ref, slots, send_sems, recv_sems):
        x, y, c, _ = _place()
        me = 4 * x + 2 * y + c
        slots[me] = in_ref[...]
        cps = []
        for r in range(1, 8):
            px = 1 - x if r & 4 else x
            py = 1 - y if r & 2 else y
            pc = 1 - c if r & 1 else c
            cps.append(_remote(in_ref, slots.at[me], send_sems.at[r - 1], recv_sems.at[r - 1], (px, py, pc)))
        for cp in cps:
            cp.start()
        for r in range(1, 8):
            px = 1 - x if r & 4 else x
            py = 1 - y if r & 2 else y
            pc = 1 - c if r & 1 else c
            blk = slots.at[4 * px + 2 * py + pc]
            _remote(blk, blk, send_sems.at[r - 1], recv_sems.at[r - 1], (px, py, pc)).wait_recv()
        for cp in cps:
            cp.wait_send()
        acc = slots[0]
        for k in range(1, 8):
            acc = acc + slots[k]
        out_ref[...] = acc

    vm = pl.BlockSpec(memory_space=pltpu.VMEM)
    return _pallas(body, name="small_all_reduce", in_specs=[vm, ANY], out_specs=vm, out_shape=_sds((P, L), F32),
                   scratch_shapes=[pltpu.VMEM((8, P, L), F32), pltpu.SemaphoreType.DMA((7,)),
                                   pltpu.SemaphoreType.DMA((7,))])(packed, after)


def _adamw_math(w, g, m, v):
    m = ADAM_B1 * m + (1.0 - ADAM_B1) * g
    v = ADAM_B2 * v + (1.0 - ADAM_B2) * (g * g)
    m_hat = m / (1.0 - ADAM_B1 ** ADAM_STEP)
    v_hat = v / (1.0 - ADAM_B2 ** ADAM_STEP)
    delta = -ADAM_LR * (m_hat / (jnp.sqrt(v_hat) + ADAM_EPS) + ADAM_WD * w)
    return delta, m, v


def _adamw(w, g, m, v):
    shape = w.shape
    C = shape[-1]
    rows = math.prod(shape[:-1])
    tr = next(t for t in (512, 352, 256, 128, 64, 32, 16, 8, rows) if rows % t == 0)
    w2, g2, m2, v2 = (a.reshape(rows, C) for a in (w, g, m, v))

    def body(w_ref, g_ref, m_ref, v_ref, go_ref, d_ref, nm_ref, nv_ref):
        gv = g_ref[...]
        d, nm, nv = _adamw_math(w_ref[...], gv, m_ref[...], v_ref[...])
        go_ref[...] = gv
        d_ref[...] = d
        nm_ref[...] = nm
        nv_ref[...] = nv

    blk = pl.BlockSpec((tr, C), lambda i: (i, 0))
    outs = _pallas(body, name="adamw", grid=(rows // tr,), in_specs=[blk] * 4, out_specs=[blk] * 4,
                   out_shape=[_sds((rows, C), F32)] * 4, compiler_params=_cp(("parallel",)))(w2, g2, m2, v2)
    return tuple(o.reshape(shape) for o in outs)


WEIGHTS = ["ffn1_norm", "ffn1_w_gate", "ffn1_w_up", "ffn1_w_down", "mix_norm", "ffn2_norm", "ffn2_w_gate", "ffn2_w_up",
           "ffn2_w_down", "ev_w_in", "ev_b_f", "ev_conv_w", "ev_conv_b", "ev_conv_norm", "ev_q_norm", "ev_k_norm",
           "ev_w_out", "od_w_in", "od_conv_w", "od_w_out"]
BIG = ([("ffn1_w_gate", 0), ("ffn1_w_up", 0), ("ffn1_w_down", 0), ("ev_w_in", 0), ("ev_w_out", 0),
        ("ffn2_w_gate", 0), ("ffn2_w_up", 0), ("ffn2_w_down", 0)]
       + [("ffn1_w_gate", 1), ("ffn1_w_up", 1), ("ffn1_w_down", 1), ("od_w_in", 0), ("od_w_out", 0),
          ("ffn2_w_gate", 1), ("ffn2_w_up", 1), ("ffn2_w_down", 1)])
TRANSPOSED = ("ffn1_w_gate", "ffn1_w_up", "ffn2_w_gate", "ffn2_w_up")
SHARED_LAST = ("ffn1_w_gate", "ffn1_w_up", "ffn1_w_down", "ev_w_in", "ev_w_out")
BLOCKS = [("ffn1", 0), ("ev", 0), ("ffn2", 0), ("ffn1", 1), ("od", 0), ("ffn2", 1)]
BLOCK_OF = {(name, l): (name.split("_w_")[0], l) for name, l in BIG}
BIG_NAMES = ["ffn1_w_gate", "ffn1_w_up", "ffn1_w_down", "ffn2_w_gate", "ffn2_w_up", "ffn2_w_down",
             "ev_w_in", "ev_w_out", "od_w_in", "od_w_out"]
SMALL = [("ffn1_norm", 16), ("mix_norm", 16), ("ffn2_norm", 16), ("ev_b_f", 8), ("ev_conv_w", 128), ("ev_conv_b", 8),
         ("ev_conv_norm", 8), ("ev_q_norm", 8), ("ev_k_norm", 8), ("od_conv_w", 24)]


def _to_lanes(a, rows):
    flat = a.reshape(-1)
    return jnp.pad(flat, (0, rows * 128 - flat.shape[0])).reshape(rows, 128)


def kernel(x, ffn1_norm, ffn1_w_gate, ffn1_w_up, ffn1_w_down, mix_norm, ffn2_norm, ffn2_w_gate, ffn2_w_up, ffn2_w_down, ev_w_in, ev_b_f, ev_conv_w, ev_conv_b, ev_conv_norm, ev_q_norm, ev_k_norm, ev_w_out, od_w_in, od_conv_w, od_w_out, loss_target, m_ffn1_norm, m_ffn1_w_gate, m_ffn1_w_up, m_ffn1_w_down, m_mix_norm, m_ffn2_norm, m_ffn2_w_gate, m_ffn2_w_up, m_ffn2_w_down, m_ev_w_in, m_ev_b_f, m_ev_conv_w, m_ev_conv_b, m_ev_conv_norm, m_ev_q_norm, m_ev_k_norm, m_ev_w_out, m_od_w_in, m_od_conv_w, m_od_w_out, v_ffn1_norm, v_ffn1_w_gate, v_ffn1_w_up, v_ffn1_w_down, v_mix_norm, v_ffn2_norm, v_ffn2_w_gate, v_ffn2_w_up, v_ffn2_w_down, v_ev_w_in, v_ev_b_f, v_ev_conv_w, v_ev_conv_b, v_ev_conv_norm, v_ev_q_norm, v_ev_k_norm, v_ev_w_out, v_od_w_in, v_od_conv_w, v_od_w_out):
    P = dict(ffn1_norm=ffn1_norm, ffn1_w_gate=ffn1_w_gate, ffn1_w_up=ffn1_w_up, ffn1_w_down=ffn1_w_down, mix_norm=mix_norm,
             ffn2_norm=ffn2_norm, ffn2_w_gate=ffn2_w_gate, ffn2_w_up=ffn2_w_up, ffn2_w_down=ffn2_w_down, ev_w_in=ev_w_in,
             ev_b_f=ev_b_f, ev_conv_w=ev_conv_w, ev_conv_b=ev_conv_b, ev_conv_norm=ev_conv_norm, ev_q_norm=ev_q_norm,
             ev_k_norm=ev_k_norm, ev_w_out=ev_w_out, od_w_in=od_w_in, od_conv_w=od_conv_w, od_w_out=od_w_out)
    M = dict(zip(WEIGHTS, [m_ffn1_norm, m_ffn1_w_gate, m_ffn1_w_up, m_ffn1_w_down, m_mix_norm, m_ffn2_norm, m_ffn2_w_gate,
                           m_ffn2_w_up, m_ffn2_w_down, m_ev_w_in, m_ev_b_f, m_ev_conv_w, m_ev_conv_b, m_ev_conv_norm,
                           m_ev_q_norm, m_ev_k_norm, m_ev_w_out, m_od_w_in, m_od_conv_w, m_od_w_out]))
    V = dict(zip(WEIGHTS, [v_ffn1_norm, v_ffn1_w_gate, v_ffn1_w_up, v_ffn1_w_down, v_mix_norm, v_ffn2_norm, v_ffn2_w_gate,
                           v_ffn2_w_up, v_ffn2_w_down, v_ev_w_in, v_ev_b_f, v_ev_conv_w, v_ev_conv_b, v_ev_conv_norm,
                           v_ev_q_norm, v_ev_k_norm, v_ev_w_out, v_od_w_in, v_od_conv_w, v_od_w_out]))
    for name in TRANSPOSED:
        P[name], M[name], V[name] = (jnp.swapaxes(a, 1, 2) for a in (P[name], M[name], V[name]))
    S, D = x.shape[1], x.shape[2]
    chip = 2 * lax.axis_index("x") + lax.axis_index("y")
    core = lax.axis_index("c")

    def own_slot(shard):
        return lax.dynamic_update_slice(lax.empty((4,) + shard.shape, shard.dtype), shard[None], (chip, 0, 0))

    taps = jnp.concatenate([_to_lanes(_pad_rows(ev_conv_w[0], 32), 32), _to_lanes(_pad_rows(od_conv_w[0], 8), 16)], axis=0)
    first = [i for i, k in enumerate(BIG) if BLOCK_OF[k] in BLOCKS[:2]]
    rest = [i for i in range(len(BIG)) if i not in first]
    send0, recv0, *bufs0 = _ag_start("first", [own_slot(P[BIG[i][0]][BIG[i][1]].astype(BF16)) for i in first]
                                     + [own_slot(taps)], True)
    zero = bufs0.pop()[0, 0]
    send1, recv1, *bufs1 = _ag_start("rest", [own_slot((P[BIG[i][0]][BIG[i][1]] + zero).astype(BF16)) for i in rest], False)
    bufs1.pop()
    cols = lambda a: a.transpose(1, 0, 2).reshape(a.shape[1], 4 * a.shape[2])
    W = {k: P[k] for k in ("ffn1_norm", "mix_norm", "ffn2_norm", "ev_b_f", "ev_q_norm", "ev_k_norm")}
    W["ev_conv_b"], W["ev_conv_norm"] = ev_conv_b, ev_conv_norm
    for tag in ("ffn1", "ffn2"):
        for kind in ("_w_gate", "_w_up", "_w_down"):
            W[tag + kind] = [None, None]
    passing = {}

    def pass_on(g, after):
        idx = [i for i, k in enumerate(BIG) if BLOCK_OF[k] == BLOCKS[g]]
        keys = [BIG[i] for i in idx] + (["taps"] if BLOCKS[g] == ("ev", 0) else [])
        send, recv, bufs, members = (send0, recv0, bufs0, first) if g < 2 else (send1, recv1, bufs1, rest)
        local = [members.index(i) for i in idx]
        passing[g] = (keys, _ag_mid(g, send, recv, [bufs[i] for i in local], local,
                                    bufs0[-1] if BLOCKS[g] == ("ev", 0) else None, len(first), after))

    def need(block, after):
        g = BLOCKS.index(block)
        if g not in passing:
            pass_on(g, bufs1[0] if g == 0 else after)
        keys, (d_send, d_recv, *thru) = passing.pop(g)
        got = dict(zip(keys, _ag_wait(g, d_send, d_recv, thru, len(keys) - ("taps" in keys), after)))
        if 1 <= g < len(BLOCKS) - 1:
            pass_on(g + 1, after)
        for key, a in got.items():
            if key == "taps":
                continue
            name, l = key
            if name.startswith("ffn"):
                W[name][l] = a
            elif name.endswith("_w_in"):
                W[name] = cols(a)
            elif name.endswith("_w_out"):
                W[name] = a.reshape(4 * a.shape[1], D)
        if block == ("ev", 0):
            taps_all = got["taps"]
            W["ev_conv_w"] = cols(taps_all[:, :32].reshape(4, 32, 128))[:CONV_A_WIDTH]
            W["od_conv_w"] = cols(taps_all[:, 32:48].reshape(4, 8, 256))[:CONV_C_WIDTH]

    rows = lambda a: a.reshape(4, a.shape[0] // 4, a.shape[1])
    colsh = lambda a: a.reshape(a.shape[0], 4, a.shape[1] // 4).transpose(1, 0, 2)
    c_arr = core.reshape(1).astype(jnp.int32)
    where = jnp.stack([chip, core]).astype(jnp.int32)
    in_flight = []

    def done(block, block_grads):
        g = BLOCKS.index(block)
        keys = list(block_grads)
        gs = []
        for name, l in keys:
            a = block_grads[(name, l)]
            gs.append(colsh(a) if name == "ev_w_in" else rows(a) if name.endswith("_w_out") else a)
        for item in list(pairs):
            to_chips(item)
        send, recv, *rest = _pair_start(g, gs, chained.get("token"))
        n = len(keys)
        pairs.append((g, keys, send, recv, rest[:n], rest[n:2 * n]))
        if g == 0:
            to_chips(pairs[0])
        chained["token"] = rest[-1] if g else chained["token"]
        return chained["token"][0:1, 0:1]

    pairs, chained = [], {}

    def to_chips(item):
        pairs.remove(item)
        g, keys, send, recv, gs, zones = item
        n = len(keys)
        done_ = _pair_wait(g, send, recv, gs, zones)
        sums = list(_pair_add(list(done_[:n]), list(done_[n:]), c_arr))
        send2, recv2, *rest = _chip_start(g, sums)
        in_flight.append((keys, send2, recv2, rest[:n], rest[n:2 * n]))
        chained["token"] = rest[-1]

    loss, grad_x, grads = _local_step(x[0], loss_target[0], W, need, done)

    order = [k for keys, *_ in in_flight for k in keys]
    landed = _chip_wait("all", [f[1] for f in in_flight], [f[2] for f in in_flight], [len(f[0]) for f in in_flight],
                        [a for f in in_flight for a in f[3]], [a for f in in_flight for a in f[4]], grad_x)
    sums, recvd = landed[:len(order)], landed[len(order):]
    stacked, shares = {}, []
    for tag, names in (("a", [n for n in BIG_NAMES if n not in SHARED_LAST]), ("b", list(SHARED_LAST))):
        for (name, l), s, r in zip(order, sums, recvd):
            if name in names:
                stacked[name] = _chip_sum(s, r, where, stacked.get(name), l, P[name].shape[0])
        layout = [(names.index(name), l) for name, l in order if name in names]
        send, recv, *thru = _share_start(tag, [stacked[name] for name in names], layout)
        shares.append((tag, names, send, recv, thru, layout))

    def small_grad(name):
        if name.endswith("_norm") and name[:3] in ("ffn", "mix"):
            return jnp.concatenate([grads[(name, 0)], grads[(name, 1)]], axis=0)
        return grads[(name, 0)]

    packed = jnp.concatenate([_to_lanes(small_grad(name), r) for name, r in SMALL], axis=0)
    total = _small_all_reduce(packed, shares[-1][4][0])
    small_grads, at = {}, 0
    for name, r in SMALL:
        part = total[at:at + r].reshape(-1)
        at += r
        if name == "ev_conv_w":
            full_g = part[:CONV_A_WIDTH * D_CONV].reshape(CONV_A_WIDTH, D_CONV)
            small_grads[name] = lax.dynamic_slice_in_dim(full_g, chip * (D_CONV // 4), D_CONV // 4, axis=1)[None]
        elif name == "od_conv_w":
            full_g = part[:CONV_C_WIDTH * D].reshape(CONV_C_WIDTH, D)
            small_grads[name] = lax.dynamic_slice_in_dim(full_g, chip * (D // 4), D // 4, axis=1)[None]
        else:
            small_grads[name] = part[:math.prod(P[name].shape)].reshape(P[name].shape)

    results = {}

    def update(name, g):
        outs = _adamw(P[name], g, M[name], V[name])
        results[name] = tuple(jnp.swapaxes(a, 1, 2) for a in outs) if name in TRANSPOSED else outs

    for name, _ in SMALL:
        update(name, small_grads[name])
    after = results[SMALL[-1][0]][1]
    for tag, names, send, recv, thru, layout in shares:
        for name, g in zip(names, _share_wait(tag, send, recv, thru, layout, after)):
            update(name, g)
        after = results[names[-1]][1]
    loss_all = lax.psum(loss[0, 0], ("x", "y", "c"))
    return (loss_all, grad_x[None], *[results[name][k] for k in range(4) for name in WEIGHTS])
```

```python
import functools
import math

import jax
import jax.numpy as jnp
from jax import lax
from jax.experimental import pallas as pl
from jax.experimental.pallas import tpu as pltpu

F32, BF16 = jnp.float32, jnp.bfloat16
EPS = 1e-6
FFN_RES = 0.5
N_HEADS, HEAD_DIM = 8, 64
D_CONV = 512
D_ATTN = N_HEADS * HEAD_DIM
CONV_A_WIDTH, CONV_C_WIDTH = 31, 3
ADAM_LR, ADAM_B1, ADAM_B2, ADAM_EPS, ADAM_WD, ADAM_STEP = 0.001, 0.9, 0.999, 1e-08, 0.01, 10
MESH = pl.DeviceIdType.MESH
ANY = pl.BlockSpec(memory_space=pl.ANY)

TOK_TILE = 512
FFN_TILE = 512
DW_TILE = 1024
ATT_TILE = 1024
QKN_TILE = 2048
HALO_A, HALO_C = 32, 16
SUBLANES = 8
CONV_ROWS = 64
SCAN_BLK = 256
MIB = 2 ** 20


def _pallas(body, **kw):
    return pl.pallas_call(body, **kw)


def _cp(sem=None, vmem_mib=48):
    return pltpu.CompilerParams(dimension_semantics=sem, vmem_limit_bytes=vmem_mib * MIB)


def _dot(a, b):
    return jnp.dot(a, b, preferred_element_type=F32)


def _dot_nt(a, b):
    return lax.dot_general(a, b, (((1,), (1,)), ((), ())), preferred_element_type=F32)


def _dot_tn(a, b):
    return lax.dot_general(a, b, (((0,), (0,)), ((), ())), preferred_element_type=F32)


def _sds(shape, dtype):
    return jax.ShapeDtypeStruct(shape, dtype)


def _rms(x):
    return lax.rsqrt(jnp.mean(x * x, axis=-1, keepdims=True) + EPS)


def _rms_bwd(dy, x, g):
    r = _rms(x)
    xh = x * r
    dxh = dy * g
    dx = r * (dxh - xh * jnp.mean(dxh * xh, axis=-1, keepdims=True))
    return dx, xh


def _silu_grad(z):
    s = jax.nn.sigmoid(z)
    return s * (1.0 + z * (1.0 - s))


def _ffn_fwd(x, g, wg, wu, wd):
    S, D = x.shape
    nc, Fs, _ = wd.shape
    tm = min(FFN_TILE, S)
    per = 2 if nc % 2 == 0 else 1
    steps = nc // per

    def body(x_ref, g_ref, wg_ref, wu_ref, wd_ref, out_ref, xn_ref, G_ref, U_ref, acc_ref):
        j = pl.program_id(1)

        @pl.when(j == 0)
        def _():
            xv = x_ref[...]
            xn_ref[...] = (xv * _rms(xv) * g_ref[...]).astype(BF16)
            acc_ref[...] = jnp.zeros_like(acc_ref)

        xn = xn_ref[...]
        part = None
        for k in range(per):
            G = _dot_nt(xn, wg_ref[k])
            U = _dot_nt(xn, wu_ref[k])
            G_ref[k] = G.astype(BF16)
            U_ref[k] = U.astype(BF16)
            term = _dot((G * jax.nn.sigmoid(G) * U).astype(BF16), wd_ref[k])
            part = term if part is None else part + term
        acc_ref[...] += part

        @pl.when(j == steps - 1)
        def _():
            out_ref[...] = x_ref[...] + FFN_RES * acc_ref[...]

    row = pl.BlockSpec((tm, D), lambda i, j: (i, 0))
    wblk = pl.BlockSpec((per, Fs, D), lambda i, j: (j, 0, 0))
    hid = pl.BlockSpec((per, tm, Fs), lambda i, j: (j, i, 0))
    return _pallas(
        body, name="ffn_fwd", grid=(S // tm, steps),
        in_specs=[row, pl.BlockSpec((1, D), lambda i, j: (0, 0)), wblk, wblk, wblk],
        out_specs=[row, row, hid, hid],
        out_shape=[_sds((S, D), F32), _sds((S, D), BF16), _sds((nc, S, Fs), BF16), _sds((nc, S, Fs), BF16)],
        scratch_shapes=[pltpu.VMEM((tm, D), F32)],
        compiler_params=_cp(("parallel", "arbitrary"), 56),
    )(x, g, wg, wu, wd)


def _ffn_bwd_w(dout, xn, G, U, wd):
    S, D = dout.shape
    nc, _, Fs = G.shape
    tm = min(DW_TILE, S)
    nt = S // tm
    sub = min(TOK_TILE, tm)

    def body(do_ref, xn_ref, G_ref, U_ref, wd_ref, dwg_ref, dwu_ref, dwd_ref, dG_ref, dU_ref, ag, au, ad, do_s, H_s):
        i = pl.program_id(1)

        @pl.when(i == 0)
        def _():
            ag[...] = jnp.zeros_like(ag)
            au[...] = jnp.zeros_like(au)
            ad[...] = jnp.zeros_like(ad)

        for r in range(0, tm, sub):
            rows = pl.ds(r, sub)
            do = (FFN_RES * do_ref[rows, :]).astype(BF16)
            do_s[rows, :] = do
            Gv = G_ref[0, rows, :].astype(F32)
            Uv = U_ref[0, rows, :].astype(F32)
            dH = _dot_nt(do, wd_ref[0])
            sg = jax.nn.sigmoid(Gv)
            act = Gv * sg
            H_s[rows, :] = (act * Uv).astype(BF16)
            dU_ref[0, rows, :] = (dH * act).astype(BF16)
            dG_ref[0, rows, :] = (dH * Uv * (sg * (1.0 + Gv * (1.0 - sg)))).astype(BF16)
        xnv = xn_ref[...]
        ag[...] += _dot_tn(dG_ref[0], xnv)
        au[...] += _dot_tn(dU_ref[0], xnv)
        ad[...] += _dot_tn(H_s[...], do_s[...])

        @pl.when(i == nt - 1)
        def _():
            dwg_ref[0] = ag[...].astype(BF16)
            dwu_ref[0] = au[...].astype(BF16)
            dwd_ref[0] = ad[...].astype(BF16)

    row = pl.BlockSpec((tm, D), lambda j, i: (i, 0))
    hid = pl.BlockSpec((1, tm, Fs), lambda j, i: (j, i, 0))
    wrow = pl.BlockSpec((1, Fs, D), lambda j, i: (j, 0, 0))
    return _pallas(
        body, name="ffn_bwd_w", grid=(nc, nt),
        in_specs=[row, row, hid, hid, wrow],
        out_specs=[wrow, wrow, wrow, hid, hid],
        out_shape=[_sds((nc, Fs, D), BF16)] * 3 + [_sds((nc, S, Fs), BF16)] * 2,
        scratch_shapes=[pltpu.VMEM((Fs, D), F32)] * 3 + [pltpu.VMEM((tm, D), BF16), pltpu.VMEM((tm, Fs), BF16)],
        compiler_params=_cp(("parallel", "arbitrary"), 56),
    )(dout, xn, G, U, wd)


def _norm_in_bwd(dzs, ws, x, g, dres, w_rows=False):
    S, D = x.shape
    nc = dzs[0].shape[0]
    n = len(dzs)
    tm = TOK_TILE
    per = nc
    steps = nc // per

    def body(*refs):
        dz_refs, w_refs = refs[:n], refs[n:2 * n]
        x_ref, g_ref, dres_ref, dx_ref, dg_ref, acc_ref = refs[2 * n:]
        i, j = pl.program_id(0), pl.program_id(1)

        @pl.when(j == 0)
        def _():
            acc_ref[...] = jnp.zeros_like(acc_ref)

        @pl.when((i == 0) & (j == 0))
        def _():
            dg_ref[...] = jnp.zeros_like(dg_ref)

        part = None
        for dz_ref, w_ref in zip(dz_refs, w_refs):
            for k in range(per):
                term = _dot(dz_ref[k], w_ref[k]) if w_rows else _dot_nt(dz_ref[k], w_ref[k])
                part = term if part is None else part + term
        acc_ref[...] += part

        @pl.when(j == steps - 1)
        def _():
            dxn = acc_ref[...]
            dx, xh = _rms_bwd(dxn, x_ref[...], g_ref[...])
            dx_ref[...] = dx + dres_ref[...]
            dg_ref[...] += jnp.sum(dxn * xh, axis=0, keepdims=True)

    row = pl.BlockSpec((tm, D), lambda i, j: (i, 0))
    one = pl.BlockSpec((1, D), lambda i, j: (0, 0))
    in_specs = [pl.BlockSpec((per, tm, dz.shape[2]), lambda i, j: (j, i, 0)) for dz in dzs]
    in_specs += [pl.BlockSpec((per,) + w.shape[1:], lambda i, j: (j, 0, 0)) for w in ws]
    return _pallas(
        body, name="norm_in_bwd", grid=(S // tm, steps),
        in_specs=in_specs + [row, one, row], out_specs=[row, one],
        out_shape=[_sds((S, D), F32), _sds((1, D), F32)],
        scratch_shapes=[pltpu.VMEM((tm, D), F32)],
        compiler_params=_cp(("arbitrary", "arbitrary")),
    )(*dzs, *ws, x, g, dres)


def _norm_proj(x, g, w, w2=None):
    S, D = x.shape
    N = w.shape[1]
    tm = TOK_TILE

    def body(*refs):
        if w2 is None:
            x_ref, g_ref, w_ref, h_ref, z_ref = refs
        else:
            x_ref, g_ref, w_ref, w2_ref, h_ref, z_ref, z2_ref = refs
        xv = x_ref[...]
        h = (xv * _rms(xv) * g_ref[...]).astype(BF16)
        h_ref[...] = h
        z_ref[...] = _dot(h, w_ref[...]).astype(BF16)
        if w2 is not None:
            z2_ref[...] = _dot(h, w2_ref[...])

    row = pl.BlockSpec((tm, D), lambda i: (i, 0))
    in_specs = [row, pl.BlockSpec((1, D), lambda i: (0, 0)), pl.BlockSpec((D, N), lambda i: (0, 0))]
    out_specs = [row, pl.BlockSpec((tm, N), lambda i: (i, 0))]
    out_shape = [_sds((S, D), BF16), _sds((S, N), BF16)]
    args = [x, g, w]
    if w2 is not None:
        N2 = w2.shape[1]
        in_specs.append(pl.BlockSpec((D, N2), lambda i: (0, 0)))
        out_specs.append(pl.BlockSpec((tm, N2), lambda i: (i, 0)))
        out_shape.append(_sds((S, N2), F32))
        args.append(w2)
    return _pallas(body, name="norm_proj", grid=(S // tm,), in_specs=in_specs, out_specs=out_specs,
                   out_shape=out_shape, compiler_params=_cp(("parallel",)))(*args)


def _proj_res(acts, ws, res):
    S, D = res.shape
    n = len(acts)
    tm = TOK_TILE

    def body(*refs):
        a_refs, w_refs = refs[:n], refs[n:2 * n]
        res_ref, out_ref = refs[2 * n:]
        acc = res_ref[...]
        for a_ref, w_ref in zip(a_refs, w_refs):
            acc = acc + _dot(a_ref[...], w_ref[...])
        out_ref[...] = acc

    row = pl.BlockSpec((tm, D), lambda i: (i, 0))
    in_specs = [pl.BlockSpec((tm, a.shape[1]), lambda i: (i, 0)) for a in acts]
    in_specs += [pl.BlockSpec(w.shape, lambda i: (0, 0)) for w in ws]
    return _pallas(body, name="proj_res", grid=(S // tm,), in_specs=in_specs + [row], out_specs=row,
                   out_shape=_sds((S, D), F32), compiler_params=_cp(("parallel",)))(*acts, *ws, res)


def _matmul_nt(a, w, after=None):
    S, K = a.shape
    M = w.shape[0]
    tm = TOK_TILE

    def body(a_ref, w_ref, *rest):
        rest[-1][...] = _dot_nt(a_ref[...].astype(BF16), w_ref[...])

    extra = [] if after is None else [after]
    return _pallas(body, name="matmul_nt", grid=(S // tm,),
                   in_specs=[pl.BlockSpec((tm, K), lambda i: (i, 0)), pl.BlockSpec((M, K), lambda i: (0, 0))] + [ANY] * len(extra),
                   out_specs=pl.BlockSpec((tm, M), lambda i: (i, 0)), out_shape=_sds((S, M), F32),
                   compiler_params=_cp(("parallel",)))(a, w, *extra)


def _matmul_tn(a, b, tn):
    S, M = a.shape
    N = b.shape[1]
    tm = min(DW_TILE, S)
    nt = S // tm

    def body(a_ref, b_ref, o_ref, acc_ref):
        i = pl.program_id(1)

        @pl.when(i == 0)
        def _():
            acc_ref[...] = jnp.zeros_like(acc_ref)

        acc_ref[...] += _dot_tn(a_ref[...].astype(BF16), b_ref[...].astype(BF16))

        @pl.when(i == nt - 1)
        def _():
            o_ref[0] = acc_ref[...].astype(BF16)

    return _pallas(body, name="matmul_tn", grid=(N // tn, nt),
                   in_specs=[pl.BlockSpec((tm, M), lambda j, i: (i, 0)), pl.BlockSpec((tm, tn), lambda j, i: (i, j))],
                   out_specs=pl.BlockSpec((1, M, tn), lambda j, i: (j, 0, 0)), out_shape=_sds((N // tn, M, tn), BF16),
                   scratch_shapes=[pltpu.VMEM((M, tn), F32)],
                   compiler_params=_cp(("parallel", "arbitrary")))(a, b)


def _fill_shifts(win, rows):
    for b in range(1, SUBLANES):
        win[b, pl.ds(0, rows - SUBLANES), :] = win[0, pl.ds(b, rows - SUBLANES), :]


def _tap(win, offset, n, base=0):
    start = base + (offset - offset % SUBLANES)
    if not isinstance(start, int):
        start = pl.multiple_of(start, SUBLANES)
    return win[offset % SUBLANES, pl.ds(start, n), :]


def _conv_a_fwd(z, cw, cb, cn):
    S = z.shape[0]
    C = D_CONV
    tm = TOK_TILE
    hb = tm // HALO_A

    def body(u_ref, gt_ref, up_ref, gp_ref, cw_ref, cb_ref, cn_ref, a_ref, a1_ref, win):
        i = pl.program_id(0)
        prev = up_ref[...].astype(F32) * jax.nn.sigmoid(gp_ref[...].astype(F32))
        win[0, pl.ds(0, HALO_A), :] = jnp.where(i == 0, 0.0, prev)
        win[0, pl.ds(HALO_A, tm), :] = u_ref[...].astype(F32) * jax.nn.sigmoid(gt_ref[...].astype(F32))
        _fill_shifts(win, tm + HALO_A)

        acc = jnp.zeros((tm, C), F32)
        for k in range(CONV_A_WIDTH):
            acc = acc + cw_ref[k:k + 1, :] * _tap(win, HALO_A - (CONV_A_WIDTH - 1) + k, tm)
        a1 = acc + cb_ref[...]
        a1_ref[...] = a1
        a2 = a1 * _rms(a1) * cn_ref[...]
        a_ref[...] = (a2 * jax.nn.sigmoid(a2)).astype(BF16)

    cur = lambda c: pl.BlockSpec((tm, C), lambda i, c=c: (i, c))
    prv = lambda c: pl.BlockSpec((HALO_A, C), lambda i, c=c: (jnp.maximum(i * hb - 1, 0), c))
    vec = pl.BlockSpec((1, C), lambda i: (0, 0))
    return _pallas(body, name="conv_a_fwd", grid=(S // tm,),
                   in_specs=[cur(0), cur(1), prv(0), prv(1), pl.BlockSpec((32, C), lambda i: (0, 0)), vec, vec],
                   out_specs=[pl.BlockSpec((tm, C), lambda i: (i, 0)), pl.BlockSpec((tm, C), lambda i: (i, 0))],
                   out_shape=[_sds((S, C), BF16), _sds((S, C), F32)],
                   scratch_shapes=[pltpu.VMEM((SUBLANES, tm + HALO_A, C), F32)],
                   compiler_params=_cp(("parallel",)))(z, z, z, z, cw, cb, cn)


def _conv_a_bwd(da, a1, z, cw, cn):
    S = z.shape[0]
    C = D_CONV
    tm = TOK_TILE
    hb = tm // HALO_A
    nt = S // tm
    W = CONV_A_WIDTH

    def body(da_ref, a1_ref, dan_ref, a1n_ref, u_ref, gt_ref, up_ref, gp_ref, cw_ref, cn_ref,
             duz_ref, dcw_ref, dcb_ref, dcn_ref, win, dwin):
        i = pl.program_id(0)

        @pl.when(i == 0)
        def _():
            dcw_ref[...] = jnp.zeros_like(dcw_ref)
            dcb_ref[...] = jnp.zeros_like(dcb_ref)
            dcn_ref[...] = jnp.zeros_like(dcn_ref)

        cnv = cn_ref[...]

        def da1_of(dav, a1v):
            a2 = a1v * _rms(a1v) * cnv
            da2 = dav * _silu_grad(a2)
            dx, xh = _rms_bwd(da2, a1v, cnv)
            return dx, da2 * xh

        da1, dcn_t = da1_of(da_ref[...], a1_ref[...])
        da1n, _ = da1_of(dan_ref[...], a1n_ref[...])
        dwin[0, pl.ds(0, tm), :] = da1
        dwin[0, pl.ds(tm, HALO_A), :] = jnp.where(i == nt - 1, 0.0, da1n)
        _fill_shifts(dwin, tm + HALO_A)
        dcb_ref[...] += jnp.sum(da1, axis=0, keepdims=True)
        dcn_ref[...] += jnp.sum(dcn_t, axis=0, keepdims=True)

        prev = up_ref[...].astype(F32) * jax.nn.sigmoid(gp_ref[...].astype(F32))
        win[0, pl.ds(0, HALO_A), :] = jnp.where(i == 0, 0.0, prev)
        win[0, pl.ds(HALO_A, tm), :] = u_ref[...].astype(F32) * jax.nn.sigmoid(gt_ref[...].astype(F32))
        _fill_shifts(win, tm + HALO_A)

        def rows_block(rb, carry):
            r0 = pl.multiple_of(rb * CONV_ROWS, CONV_ROWS)
            rows = pl.ds(r0, CONV_ROWS)
            da1_b = dwin[0, rows, :]
            da0 = jnp.zeros((CONV_ROWS, C), F32)
            for k in range(W):
                da0 = da0 + cw_ref[k:k + 1, :] * _tap(dwin, W - 1 - k, CONV_ROWS, r0)
                dcw_ref[k:k + 1, :] += jnp.sum(da1_b * _tap(win, HALO_A - (W - 1) + k, CONV_ROWS, r0), axis=0, keepdims=True)
            u = u_ref[rows, :].astype(F32)
            sg = jax.nn.sigmoid(gt_ref[rows, :].astype(F32))
            duz_ref[rows, 0:C] = (da0 * sg).astype(BF16)
            duz_ref[rows, C:2 * C] = (da0 * u * sg * (1.0 - sg)).astype(BF16)
            return carry

        lax.fori_loop(0, tm // CONV_ROWS, rows_block, 0)

    cur = lambda c: pl.BlockSpec((tm, C), lambda i, c=c: (i, c))
    prv = lambda c: pl.BlockSpec((HALO_A, C), lambda i, c=c: (jnp.maximum(i * hb - 1, 0), c))
    nxt = pl.BlockSpec((HALO_A, C), lambda i: (jnp.minimum((i + 1) * hb, S // HALO_A - 1), 0))
    vec = pl.BlockSpec((1, C), lambda i: (0, 0))
    return _pallas(body, name="conv_a_bwd", grid=(nt,),
                   in_specs=[cur(0), cur(0), nxt, nxt, cur(0), cur(1), prv(0), prv(1),
                             pl.BlockSpec((32, C), lambda i: (0, 0)), vec],
                   out_specs=[pl.BlockSpec((tm, 2 * C), lambda i: (i, 0)), pl.BlockSpec((32, C), lambda i: (0, 0)), vec, vec],
                   out_shape=[_sds((S, 2 * C), BF16), _sds((32, C), F32), _sds((1, C), F32), _sds((1, C), F32)],
                   scratch_shapes=[pltpu.VMEM((SUBLANES, tm + HALO_A, C), F32)] * 2,
                   compiler_params=_cp(("arbitrary",)))(da, a1, da, a1, z, z, z, z, cw, cn)


def _forget_scan(fl, bf):
    S, L = fl.shape
    B = SCAN_BLK

    def body(fl_ref, bf_ref, flb_ref, F_ref):
        tri = (lax.broadcasted_iota(jnp.int32, (B, B), 0) >= lax.broadcasted_iota(jnp.int32, (B, B), 1)).astype(F32)

        def step(c, carry):
            rows = pl.ds(pl.multiple_of(c * B, B), B)
            v = fl_ref[rows, :] + bf_ref[...]
            flb_ref[rows, :] = v
            lf = jnp.minimum(v, 0.0) - jnp.log1p(jnp.exp(-jnp.abs(v)))
            cs = jnp.dot(tri, lf, precision=lax.Precision.HIGHEST, preferred_element_type=F32) + carry
            F_ref[rows, :] = cs
            return cs[B - 1:B, :]

        lax.fori_loop(0, S // B, step, jnp.zeros((1, L), F32))

    return _pallas(body, name="forget_scan", out_shape=[_sds((S, L), F32), _sds((S, L), F32)],
                   compiler_params=_cp())(fl, bf)


def _forget_scan_bwd(dF, flb):
    S, L = dF.shape
    B = SCAN_BLK
    nb = S // B

    def body(dF_ref, flb_ref, dfl_ref, db_ref):
        tri = (lax.broadcasted_iota(jnp.int32, (B, B), 0) <= lax.broadcasted_iota(jnp.int32, (B, B), 1)).astype(F32)

        def step(t, carry):
            carry_cs, db = carry
            rows = pl.ds(pl.multiple_of((nb - 1 - t) * B, B), B)
            cs = jnp.dot(tri, dF_ref[rows, :], precision=lax.Precision.HIGHEST, preferred_element_type=F32) + carry_cs
            dfl = cs * jax.nn.sigmoid(-flb_ref[rows, :])
            dfl_ref[rows, :] = dfl
            return cs[0:1, :], db + jnp.sum(dfl, axis=0, keepdims=True)

        _, db = lax.fori_loop(0, nb, step, (jnp.zeros((1, L), F32), jnp.zeros((1, L), F32)))
        db_ref[...] = db

    return _pallas(body, name="forget_scan_bwd", out_shape=[_sds((S, L), F32), _sds((1, L), F32)],
                   compiler_params=_cp())(dF, flb)


NEG = -1e30


def _causal_mask(t):
    return lax.broadcasted_iota(jnp.int32, (t, t), 0) >= lax.broadcasted_iota(jnp.int32, (t, t), 1)


AUG = 128
C_F, C_ONE, C_LSE = 64, 67, 70


def _split3(f):
    a = f.astype(BF16).astype(F32)
    r = f - a
    b = r.astype(BF16).astype(F32)
    return a, b, r - b


def _put3(lane, base, parts, other):
    out = other
    for k, p in enumerate(parts):
        out = jnp.where(lane == base + k, p, out)
    return out


def _ones3(lane, base):
    return (lane >= base) & (lane < base + 3)


def _lane_ids(rows):
    return lax.broadcasted_iota(jnp.int32, (rows, AUG), 1)


def _pair_rms(x, lo):
    sq = x * x
    ms_a = jnp.sum(jnp.where(lo, sq, 0.0), axis=-1, keepdims=True) * (1.0 / HEAD_DIM)
    ms_b = jnp.sum(jnp.where(lo, 0.0, sq), axis=-1, keepdims=True) * (1.0 / HEAD_DIM)
    return jnp.where(lo, lax.rsqrt(ms_a + EPS), lax.rsqrt(ms_b + EPS))


def _qkv_prep(z, Fc, qw, kw):
    S = z.shape[0]
    tp = min(QKN_TILE, S)
    scale = 1.0 / math.sqrt(HEAD_DIM)

    def body(zq_ref, zk_ref, zv_ref, F_ref, qw_ref, kw_ref, q_ref, k_ref, v_ref):
        j = pl.program_id(0)
        lane = _lane_ids(tp)
        lo = lane < HEAD_DIM
        Fv = F_ref[...]
        xq = zq_ref[...].astype(F32)
        xk = zk_ref[...].astype(F32)
        qn = xq * _pair_rms(xq, lo) * qw_ref[...] * scale
        kn = xk * _pair_rms(xk, lo) * kw_ref[...]
        vv = zv_ref[...].astype(F32)
        for half in range(2):
            take = (lambda a: a) if half == 0 else (lambda a: pltpu.roll(a, HEAD_DIM, 1))
            fp = _split3(jnp.sum(jnp.where(lane == 2 * j + half, Fv, 0.0), axis=-1, keepdims=True))
            qx = _put3(lane, C_F, fp, jnp.where(_ones3(lane, C_ONE), 1.0, 0.0))
            kx = _put3(lane, C_ONE, [-p for p in fp], jnp.where(_ones3(lane, C_F) | _ones3(lane, C_LSE), 1.0, 0.0))
            vx = jnp.where(_ones3(lane, C_F), 1.0, 0.0)
            q_ref[half] = jnp.where(lo, take(qn), qx).astype(BF16)
            k_ref[half] = jnp.where(lo, take(kn), kx).astype(BF16)
            v_ref[half] = jnp.where(lo, take(vv), vx).astype(BF16)

    col = lambda c0: pl.BlockSpec((tp, AUG), lambda j, i, c0=c0: (i, c0 + j))
    vec = pl.BlockSpec((1, AUG), lambda j, i: (0, 0))
    out = pl.BlockSpec((2, tp, AUG), lambda j, i: (j, i, 0))
    return _pallas(body, name="qkv_prep", grid=(N_HEADS // 2, S // tp),
                   in_specs=[col(8), col(12), col(16), pl.BlockSpec((tp, AUG), lambda j, i: (i, 0)), vec, vec],
                   out_specs=[out, out, out], out_shape=[_sds((N_HEADS, S, AUG), BF16)] * 3,
                   compiler_params=_cp(("parallel", "parallel")))(z, z, z, Fc, qw, kw)


def _fox_fwd(q_aug, k_aug, v_aug):
    H, S, A = q_aug.shape
    t = ATT_TILE
    nq = S // t

    def body(q_ref, k_ref, v_ref, o_ref, q2_ref):
        i = pl.program_id(1)
        q = q_ref[0]

        def tile(j, carry, diag):
            m, acc = carry
            rows = pl.ds(pl.multiple_of(j * t, t), t)
            s = _dot_nt(q, k_ref[0, rows, :])
            if diag:
                s = jnp.where(_causal_mask(t), s, NEG)
            m_new = jnp.maximum(m, jnp.max(s, axis=-1, keepdims=True))
            p = jnp.exp(s - m_new)
            acc = jnp.exp(m - m_new) * acc + _dot(p.astype(BF16), v_ref[0, rows, :])
            return m_new, acc

        init = (jnp.full((t, 1), NEG, F32), jnp.zeros((t, A), F32))
        carry = lax.fori_loop(0, i, lambda j, c: tile(j, c, False), init)
        m, acc = tile(i, carry, True)
        lane = _lane_ids(t)
        l = jnp.sum(jnp.where(lane == C_F, acc, 0.0), axis=-1, keepdims=True)
        o_ref[0] = (acc / l).astype(BF16)
        lse = m + jnp.log(l)
        q2_ref[0] = (q.astype(F32) + _put3(lane, C_LSE, [-p for p in _split3(lse)], 0.0)).astype(BF16)

    qblk = pl.BlockSpec((1, t, A), lambda h, i: (h, i, 0))
    full = pl.BlockSpec((1, S, A), lambda h, i: (h, 0, 0))
    return _pallas(body, name="fox_fwd", grid=(H, nq), in_specs=[qblk, full, full], out_specs=[qblk, qblk],
                   out_shape=[_sds((H, S, A), BF16)] * 2, compiler_params=_cp(("parallel", "parallel")))(q_aug, k_aug, v_aug)


def _do_prep(dcat, o_aug):
    S = dcat.shape[0]
    tp = min(QKN_TILE, S)

    def body(d_ref, o_ref, out_ref):
        lane = _lane_ids(tp)
        lo = lane < HEAD_DIM
        x = d_ref[...]
        for half in range(2):
            d = jnp.where(lo, x if half == 0 else pltpu.roll(x, HEAD_DIM, 1), 0.0)
            delta = jnp.sum(d * o_ref[half].astype(F32), axis=-1, keepdims=True)
            out_ref[half] = jnp.where(lo, d, _put3(lane, C_F, [-p for p in _split3(delta)], 0.0)).astype(BF16)

    pair = pl.BlockSpec((2, tp, AUG), lambda j, i: (j, i, 0))
    return _pallas(body, name="do_prep", grid=(N_HEADS // 2, S // tp),
                   in_specs=[pl.BlockSpec((tp, AUG), lambda j, i: (i, D_CONV // AUG + j)), pair], out_specs=pair,
                   out_shape=_sds((N_HEADS, S, AUG), BF16), compiler_params=_cp(("parallel", "parallel")))(dcat, o_aug)


def _fox_bwd(q2, k_aug, v_aug, do_aug):
    H, S, A = q2.shape
    t = ATT_TILE
    nq = S // t

    def body(q_ref, k_ref, v_ref, do_ref, dq_ref, dk_ref, dv_ref):
        j = pl.program_id(1)

        @pl.when(j == 0)
        def _():
            dq_ref[...] = jnp.zeros_like(dq_ref)

        k = k_ref[0]
        vv = v_ref[0]

        def tile(i, carry, diag):
            dk, dv = carry
            rows = pl.ds(pl.multiple_of(i * t, t), t)
            q = q_ref[0, rows, :]
            dov = do_ref[0, rows, :]
            s = _dot_nt(q, k)
            if diag:
                s = jnp.where(_causal_mask(t), s, NEG)
            p = jnp.exp(s)
            dv = dv + _dot_tn(p.astype(BF16), dov)
            dsb = (p * _dot_nt(dov, vv)).astype(BF16)
            dq_ref[0, rows, :] += _dot(dsb, k)
            dk = dk + _dot_tn(dsb, q)
            return dk, dv

        init = (jnp.zeros((t, A), F32), jnp.zeros((t, A), F32))
        carry = tile(j, init, True)
        dk, dv = lax.fori_loop(j + 1, nq, lambda i, c: tile(i, c, False), carry)
        dk_ref[0] = dk
        dv_ref[0] = dv

    full = pl.BlockSpec((1, S, A), lambda h, j: (h, 0, 0))
    kblk = pl.BlockSpec((1, t, A), lambda h, j: (h, j, 0))
    return _pallas(body, name="fox_bwd", grid=(H, nq), in_specs=[full, kblk, kblk, full], out_specs=[full, kblk, kblk],
                   out_shape=[_sds((H, S, A), F32)] * 3,
                   compiler_params=_cp(("parallel", "arbitrary")))(q2, k_aug, v_aug, do_aug)


def _qkv_bwd(dq, dk, dv, z, qw, kw):
    S = z.shape[0]
    tp = min(QKN_TILE, S)
    scale = 1.0 / math.sqrt(HEAD_DIM)

    def body(dq_ref, dk_ref, dv_ref, zq_ref, zk_ref, qw_ref, kw_ref, dqf_ref, dkf_ref, dvf_ref, dF_ref, dqw_ref, dkw_ref):
        i, j = pl.program_id(0), pl.program_id(1)
        lane = _lane_ids(tp)
        lo = lane < HEAD_DIM

        @pl.when((i == 0) & (j == 0))
        def _():
            dqw_ref[...] = jnp.zeros_like(dqw_ref)
            dkw_ref[...] = jnp.zeros_like(dkw_ref)

        def pair(ref):
            return jnp.where(lo, ref[0], pltpu.roll(ref[1], HEAD_DIM, 1))

        def norm_bwd(g, x, w):
            r = _pair_rms(x, lo)
            xh = x * r
            dxh = g * w
            tt = dxh * xh
            mean_a = jnp.sum(jnp.where(lo, tt, 0.0), axis=-1, keepdims=True) * (1.0 / HEAD_DIM)
            mean_b = jnp.sum(jnp.where(lo, 0.0, tt), axis=-1, keepdims=True) * (1.0 / HEAD_DIM)
            return r * (dxh - xh * jnp.where(lo, mean_a, mean_b)), g * xh

        dxq, gq = norm_bwd(pair(dq_ref) * scale, zq_ref[...].astype(F32), qw_ref[...])
        dqf_ref[...] = dxq.astype(BF16)
        dqw_ref[...] += jnp.sum(gq, axis=0, keepdims=True)
        dxk, gk = norm_bwd(pair(dk_ref), zk_ref[...].astype(F32), kw_ref[...])
        dkf_ref[...] = dxk.astype(BF16)
        dkw_ref[...] += jnp.sum(gk, axis=0, keepdims=True)
        dvf_ref[...] = pair(dv_ref).astype(BF16)

        contrib = jnp.zeros((tp, AUG), F32)
        for half in range(2):
            df = (jnp.sum(jnp.where(lane == C_F, dq_ref[half], 0.0), axis=-1, keepdims=True)
                  - jnp.sum(jnp.where(lane == C_ONE, dk_ref[half], 0.0), axis=-1, keepdims=True))
            contrib = jnp.where(lane == 2 * j + half, df, contrib)

        @pl.when(j == 0)
        def _():
            dF_ref[...] = contrib

        @pl.when(j > 0)
        def _():
            dF_ref[...] += contrib

    pairb = pl.BlockSpec((2, tp, AUG), lambda i, j: (j, i, 0))
    col = lambda c0: pl.BlockSpec((tp, AUG), lambda i, j, c0=c0: (i, c0 + j))
    vec = pl.BlockSpec((1, AUG), lambda i, j: (0, 0))
    flat = pl.BlockSpec((tp, AUG), lambda i, j: (i, j))
    return _pallas(body, name="qkv_bwd", grid=(S // tp, N_HEADS // 2),
                   in_specs=[pairb, pairb, pairb, col(8), col(12), vec, vec],
                   out_specs=[flat, flat, flat, pl.BlockSpec((tp, AUG), lambda i, j: (i, 0)), vec, vec],
                   out_shape=[_sds((S, D_ATTN), BF16)] * 3 + [_sds((S, AUG), F32), _sds((1, AUG), F32), _sds((1, AUG), F32)],
                   compiler_params=_cp(("arbitrary", "arbitrary")))(dq, dk, dv, z, z, qw, kw)


def _proj_res_heads(a, wa, o_aug, wo, res):
    S, D = res.shape
    H = o_aug.shape[0]
    tm = TOK_TILE

    def body(a_ref, wa_ref, o_ref, wo_ref, res_ref, out_ref):
        acc = res_ref[...] + _dot(a_ref[...], wa_ref[...])
        for h in range(H):
            acc = acc + _dot(o_ref[h], wo_ref[h])
        out_ref[...] = acc

    row = pl.BlockSpec((tm, D), lambda i: (i, 0))
    return _pallas(body, name="proj_res_heads", grid=(S // tm,),
                   in_specs=[pl.BlockSpec((tm, a.shape[1]), lambda i: (i, 0)), pl.BlockSpec(wa.shape, lambda i: (0, 0)),
                             pl.BlockSpec((H, tm, AUG), lambda i: (0, i, 0)), pl.BlockSpec(wo.shape, lambda i: (0, 0, 0)), row],
                   out_specs=row, out_shape=_sds((S, D), F32), compiler_params=_cp(("parallel",)))(a, wa, o_aug, wo, res)


def _heads_tn(o_aug, d):
    H, S, A = o_aug.shape
    D = d.shape[1]
    tm = min(DW_TILE, S)
    nt = S // tm

    def body(o_ref, d_ref, out_ref, acc_ref):
        i = pl.program_id(0)

        @pl.when(i == 0)
        def _():
            acc_ref[...] = jnp.zeros_like(acc_ref)

        dv = d_ref[...].astype(BF16)
        for h in range(H):
            acc_ref[h] += _dot_tn(o_ref[h], dv)

        @pl.when(i == nt - 1)
        def _():
            out_ref[...] = acc_ref[...].astype(BF16)

    return _pallas(body, name="heads_tn", grid=(nt,),
                   in_specs=[pl.BlockSpec((H, tm, A), lambda i: (0, i, 0)), pl.BlockSpec((tm, D), lambda i: (i, 0))],
                   out_specs=pl.BlockSpec((H, A, D), lambda i: (0, 0, 0)), out_shape=_sds((H, A, D), BF16),
                   scratch_shapes=[pltpu.VMEM((H, A, D), F32)], compiler_params=_cp(("arbitrary",)))(o_aug, d)


def _odd_mid_fwd(z, cw):
    S = z.shape[0]
    D = z.shape[1] // 3
    tm = TOK_TILE
    hb = tm // HALO_C
    W = CONV_C_WIDTH

    def body(gb_ref, gc_ref, hh_ref, gcp_ref, hhp_ref, cw_ref, y_ref, win):
        i = pl.program_id(0)
        prev = gcp_ref[...].astype(F32) * hhp_ref[...].astype(F32)
        win[pl.ds(0, HALO_C), :] = jnp.where(i == 0, 0.0, prev)
        win[pl.ds(HALO_C, tm), :] = gc_ref[...].astype(F32) * hh_ref[...].astype(F32)
        c1 = jnp.zeros((tm, D), F32)
        for k in range(W):
            c1 = c1 + cw_ref[k:k + 1, :] * win[pl.ds(HALO_C - (W - 1) + k, tm), :]
        y_ref[...] = (gb_ref[...].astype(F32) * c1).astype(BF16)

    cur = lambda c: pl.BlockSpec((tm, D), lambda i, c=c: (i, c))
    prv = lambda c: pl.BlockSpec((HALO_C, D), lambda i, c=c: (jnp.maximum(i * hb - 1, 0), c))
    return _pallas(body, name="odd_mid_fwd", grid=(S // tm,),
                   in_specs=[cur(0), cur(1), cur(2), prv(1), prv(2), pl.BlockSpec((8, D), lambda i: (0, 0))],
                   out_specs=pl.BlockSpec((tm, D), lambda i: (i, 0)), out_shape=_sds((S, D), BF16),
                   scratch_shapes=[pltpu.VMEM((tm + HALO_C, D), F32)],
                   compiler_params=_cp(("parallel",)))(z, z, z, z, z, cw)


def _odd_mid_bwd(dy, z, cw):
    S = z.shape[0]
    D = z.shape[1] // 3
    tm = TOK_TILE
    hb = tm // HALO_C
    nt = S // tm
    W = CONV_C_WIDTH

    def body(dy_ref, dyn_ref, gb_ref, gbn_ref, gc_ref, hh_ref, gcp_ref, hhp_ref, cw_ref, dz_ref, dcw_ref, win, dwin):
        i = pl.program_id(0)

        @pl.when(i == 0)
        def _():
            dcw_ref[...] = jnp.zeros_like(dcw_ref)

        gc = gc_ref[...].astype(F32)
        hh = hh_ref[...].astype(F32)
        prev = gcp_ref[...].astype(F32) * hhp_ref[...].astype(F32)
        win[pl.ds(0, HALO_C), :] = jnp.where(i == 0, 0.0, prev)
        win[pl.ds(HALO_C, tm), :] = gc * hh
        dyv = dy_ref[...]
        dc1 = dyv * gb_ref[...].astype(F32)
        dwin[pl.ds(0, tm), :] = dc1
        dwin[pl.ds(tm, HALO_C), :] = jnp.where(i == nt - 1, 0.0, dyn_ref[...] * gbn_ref[...].astype(F32))
        c1 = jnp.zeros((tm, D), F32)
        dc0 = jnp.zeros((tm, D), F32)
        for k in range(W):
            tap = win[pl.ds(HALO_C - (W - 1) + k, tm), :]
            c1 = c1 + cw_ref[k:k + 1, :] * tap
            dc0 = dc0 + cw_ref[k:k + 1, :] * dwin[pl.ds(W - 1 - k, tm), :]
            dcw_ref[k:k + 1, :] += jnp.sum(dc1 * tap, axis=0, keepdims=True)
        dz_ref[:, 0:D] = (dyv * c1).astype(BF16)
        dz_ref[:, D:2 * D] = (dc0 * hh).astype(BF16)
        dz_ref[:, 2 * D:3 * D] = (dc0 * gc).astype(BF16)

    cur = lambda c: pl.BlockSpec((tm, D), lambda i, c=c: (i, c))
    prv = lambda c: pl.BlockSpec((HALO_C, D), lambda i, c=c: (jnp.maximum(i * hb - 1, 0), c))
    nxt = pl.BlockSpec((HALO_C, D), lambda i: (jnp.minimum((i + 1) * hb, S // HALO_C - 1), 0))
    return _pallas(body, name="odd_mid_bwd", grid=(nt,),
                   in_specs=[cur(0), nxt, cur(0), nxt, cur(1), cur(2), prv(1), prv(2), pl.BlockSpec((8, D), lambda i: (0, 0))],
                   out_specs=[pl.BlockSpec((tm, 3 * D), lambda i: (i, 0)), pl.BlockSpec((8, D), lambda i: (0, 0))],
                   out_shape=[_sds((S, 3 * D), BF16), _sds((8, D), F32)],
                   scratch_shapes=[pltpu.VMEM((tm + HALO_C, D), F32), pltpu.VMEM((tm + HALO_C, D), F32)],
                   compiler_params=_cp(("arbitrary",)))(dy, dy, z, z, z, z, z, z, cw)


def _loss_head(y, tgt):
    S, D = y.shape
    tm = TOK_TILE

    def body(y_ref, t_ref, dy_ref, l_ref):
        @pl.when(pl.program_id(0) == 0)
        def _():
            l_ref[...] = jnp.zeros_like(l_ref)

        e = y_ref[...] - t_ref[...]
        dy_ref[...] = e * (1.0 / D)
        l_ref[...] += jnp.sum(jnp.sum(e * e, axis=-1, keepdims=True), axis=0, keepdims=True) * (0.5 / D)

    row = pl.BlockSpec((tm, D), lambda i: (i, 0))
    return _pallas(body, name="loss_head", grid=(S // tm,), in_specs=[row, row],
                   out_specs=[row, pl.BlockSpec((1, 1), lambda i: (0, 0))],
                   out_shape=[_sds((S, D), F32), _sds((1, 1), F32)],
                   compiler_params=_cp(("arbitrary",)))(y, tgt)


def _pad_rows(a, rows):
    return jnp.pad(a, ((0, rows - a.shape[0]), (0, 0)))


def _local_step(x, tgt, W, need=lambda block, after: None, done=lambda block, block_grads: None):
    S, D = x.shape
    grads = {}
    saved = {}

    def gain_after(gain, token):
        return gain if token is None else gain + token

    def ffn_f(tag, l, xin):
        need((tag, l), xin)
        out, xn, G, U = _ffn_fwd(xin, W[tag + "_norm"][l:l + 1], W[tag + "_w_gate"][l], W[tag + "_w_up"][l],
                                 W[tag + "_w_down"][l])
        saved[(tag, l)] = (xin, xn, G, U)
        return out

    def ffn_b(tag, l, dout):
        xin, xn, G, U = saved[(tag, l)]
        keys = [(tag + "_w_gate", l), (tag + "_w_up", l), (tag + "_w_down", l)]
        *dws, dG, dU = _ffn_bwd_w(dout, xn, G, U, W[tag + "_w_down"][l])
        big = dict(zip(keys, dws))
        grads.update(big)
        token = done((tag, l), big)
        dx, dg = _norm_in_bwd([dG, dU], [W[tag + "_w_gate"][l], W[tag + "_w_up"][l]], xin,
                              gain_after(W[tag + "_norm"][l:l + 1], token), dout, w_rows=True)
        grads[(tag + "_norm", l)] = dg
        return dx

    x0a = ffn_f("ffn1", 0, x)
    need(("ev", 0), x0a)
    w_in = W["ev_w_in"]
    w_main, w_f = w_in[:, :2560], jnp.pad(w_in[:, 2560:], ((0, 0), (0, 120)))
    h0, z0, fl = _norm_proj(x0a, W["mix_norm"][0:1], w_main, w_f)
    cw_a = _pad_rows(W["ev_conv_w"], 32)
    a_act, a1 = _conv_a_fwd(z0, cw_a, W["ev_conv_b"], W["ev_conv_norm"])
    flb, Fc = _forget_scan(fl, jnp.pad(W["ev_b_f"], ((0, 0), (0, 120))))
    qw2, kw2 = jnp.tile(W["ev_q_norm"], (1, 2)), jnp.tile(W["ev_k_norm"], (1, 2))
    q_aug, k_aug, v_aug = _qkv_prep(z0, Fc, qw2, kw2)
    o_aug, q_lse = _fox_fwd(q_aug, k_aug, v_aug)
    w_out_e = W["ev_w_out"]
    w_out_o = jnp.pad(w_out_e[D_CONV:].reshape(N_HEADS, HEAD_DIM, D), ((0, 0), (0, AUG - HEAD_DIM), (0, 0)))
    x0b = _proj_res_heads(a_act, w_out_e[:D_CONV], o_aug, w_out_o, x0a)
    x0c = ffn_f("ffn2", 0, x0b)
    x1a = ffn_f("ffn1", 1, x0c)
    need(("od", 0), x1a)
    h1, z1 = _norm_proj(x1a, W["mix_norm"][1:2], W["od_w_in"])
    cw_c = _pad_rows(W["od_conv_w"], 8)
    y1 = _odd_mid_fwd(z1, cw_c)
    x1b = _proj_res([y1], [W["od_w_out"]], x1a)
    x1c = ffn_f("ffn2", 1, x1b)
    dy, loss = _loss_head(x1c, tgt)

    d = ffn_b("ffn2", 1, dy)
    dy1 = _matmul_nt(d, W["od_w_out"])
    grads[("od_w_out", 0)] = _matmul_tn(y1, d, D)[0]
    dz1, dcw_c = _odd_mid_bwd(dy1, z1, cw_c)
    grads[("od_conv_w", 0)] = dcw_c[:CONV_C_WIDTH]
    grads[("od_w_in", 0)] = _matmul_tn(h1, dz1, 3 * D // 4)
    token = done(("od", 0), {k: grads[k] for k in (("od_w_out", 0), ("od_w_in", 0))})
    d, dg = _norm_in_bwd([dz1[None]], [W["od_w_in"][None]], x1a, gain_after(W["mix_norm"][1:2], token), d)
    grads[("mix_norm", 1)] = dg
    d = ffn_b("ffn1", 1, d)
    d = ffn_b("ffn2", 0, d)
    dcat = _matmul_nt(d, w_out_e)
    grads[("ev_w_out", 0)] = jnp.concatenate([_matmul_tn(a_act, d, D)[0],
                                              _heads_tn(o_aug, d)[:, :HEAD_DIM].reshape(D_ATTN, D)], axis=0)
    duz, dcw_a, dcb, dcn = _conv_a_bwd(dcat, a1, z0, cw_a, W["ev_conv_norm"])
    grads[("ev_conv_w", 0)] = dcw_a[:CONV_A_WIDTH]
    grads[("ev_conv_b", 0)] = dcb
    grads[("ev_conv_norm", 0)] = dcn
    dq_a, dk_a, dv_a = _fox_bwd(q_lse, k_aug, v_aug, _do_prep(dcat, o_aug))
    dqf, dkf, dvf, dF, dqw, dkw = _qkv_bwd(dq_a, dk_a, dv_a, z0, qw2, kw2)
    grads[("ev_q_norm", 0)] = dqw[:, :HEAD_DIM] + dqw[:, HEAD_DIM:]
    grads[("ev_k_norm", 0)] = dkw[:, :HEAD_DIM] + dkw[:, HEAD_DIM:]
    dfl, dbf = _forget_scan_bwd(dF, flb)
    grads[("ev_b_f", 0)] = dbf[:, :N_HEADS]
    dz0 = jnp.concatenate([duz, dqf, dkf, dvf], axis=1)
    dflb = dfl.astype(BF16)
    gmain = _matmul_tn(h0, dz0, 640)
    gmain = gmain.transpose(1, 0, 2).reshape(D, 2560)
    gf = _matmul_tn(h0, dflb, 128)[0][:, :N_HEADS]
    grads[("ev_w_in", 0)] = jnp.concatenate([gmain, gf], axis=1)
    token = done(("ev", 0), {k: grads[k] for k in (("ev_w_out", 0), ("ev_w_in", 0))})
    d, dg = _norm_in_bwd([dz0[None], dflb[None]], [w_main[None], w_f[None]], x0a, gain_after(W["mix_norm"][0:1], token), d)
    grads[("mix_norm", 0)] = dg
    d = ffn_b("ffn1", 0, d)
    return loss, d, grads


def _place():
    x, y, c = lax.axis_index("x"), lax.axis_index("y"), lax.axis_index("c")
    chips = [(1 - x, y), (x, 1 - y), (1 - x, 1 - y)]
    return x, y, c, chips


def _remote(src, dst, send_sem, recv_sem, to):
    return pltpu.make_async_remote_copy(src_ref=src, dst_ref=dst, send_sem=send_sem, recv_sem=recv_sem,
                                        device_id=to, device_id_type=MESH)


HBM = pl.BlockSpec(memory_space=pltpu.HBM)
SEM = pl.BlockSpec(memory_space=pltpu.SEMAPHORE)
EFFECT = pltpu.SideEffectType.DATAFLOW_SIDE_EFFECTING


def _in_hbm(a):
    return pltpu.with_memory_space_constraint(a, pltpu.HBM)


def _ag_start(tag, bufs, with_taps):
    n = len(bufs)
    order = ([n - 1] + list(range(n - 1))) if with_taps else list(range(n))

    def body(*refs):
        send_sems, recv_sems = refs[n], refs[n + 1]
        outs, token = refs[n + 2:2 * n + 2], refs[2 * n + 2]
        x, y, c, chips = _place()
        me = 2 * x + y
        for a in order:
            if with_taps and a == n - 1:
                blk = outs[a].at[me]
            else:
                h = outs[a].shape[1] // 2
                blk = outs[a].at[me, pl.ds(c * h, h)]
            for jj, (px, py) in enumerate(chips):
                _remote(blk, blk, send_sems.at[3 * a + jj], recv_sems.at[3 * a + jj], (px, py, c)).start()
        token[...] = jnp.zeros_like(token)

    return _pallas(
        body, name=f"gather_start_{tag}",
        out_shape=[pltpu.SemaphoreType.DMA((3 * n,)), pltpu.SemaphoreType.DMA((3 * n,))]
        + [pltpu.HBM(b.shape, b.dtype) for b in bufs] + [_sds((8, 128), F32)],
        in_specs=[HBM] * n, out_specs=[SEM, SEM] + [HBM] * n + [pl.BlockSpec(memory_space=pltpu.VMEM)],
        input_output_aliases={a: 2 + a for a in range(n)},
        compiler_params=pltpu.CompilerParams(has_side_effects=EFFECT),
    )(*[_in_hbm(b) for b in bufs])


def _ag_mid(g, ici_send, ici_recv, bufs, idx, taps, n_big, after):
    n = len(bufs)
    arrs = list(bufs) + ([taps] if taps is not None else [])
    m = len(arrs)

    def body(*refs):
        ici_s, ici_r = refs[0], refs[1]
        d_send, d_recv = refs[m + 3], refs[m + 4]
        outs = refs[m + 5:]
        x, y, c, chips = _place()
        me = 2 * x + y
        for i in range(m):
            a = idx[i] if i < n else n_big
            for jj, (px, py) in enumerate(chips):
                k = 3 * a + jj
                if i < n:
                    h = outs[i].shape[1] // 2
                    mine, blk = outs[i].at[me, pl.ds(c * h, h)], outs[i].at[2 * px + py, pl.ds(c * h, h)]
                else:
                    mine, blk = outs[i].at[me], outs[i].at[2 * px + py]
                _remote(mine, mine, ici_s.at[k], ici_r.at[k], (px, py, c)).wait_send()
                _remote(blk, blk, ici_s.at[k], ici_r.at[k], (px, py, c)).wait_recv()
                if i < n:
                    _remote(blk, blk, d_send.at[3 * i + jj], d_recv.at[3 * i + jj], (x, y, 1 - c)).start()

    return _pallas(
        body, name=f"gather_pass_on_{g}",
        out_shape=[pltpu.SemaphoreType.DMA((3 * n,)), pltpu.SemaphoreType.DMA((3 * n,))] + [pltpu.HBM(b.shape, b.dtype) for b in arrs],
        in_specs=[SEM, SEM] + [HBM] * m + [ANY], out_specs=[SEM, SEM] + [HBM] * m,
        input_output_aliases={2 + i: 2 + i for i in range(m)},
        compiler_params=pltpu.CompilerParams(has_side_effects=EFFECT),
    )(ici_send, ici_recv, *arrs, after)


def _ag_wait(g, d_send, d_recv, arrs, n, after):
    m = len(arrs)

    def body(*refs):
        d_s, d_r = refs[0], refs[1]
        outs = refs[m + 3:]
        x, y, c, chips = _place()
        for i in range(n):
            h = outs[i].shape[1] // 2
            for jj, (px, py) in enumerate(chips):
                sent = outs[i].at[2 * px + py, pl.ds(c * h, h)]
                got = outs[i].at[2 * px + py, pl.ds((1 - c) * h, h)]
                _remote(sent, sent, d_s.at[3 * i + jj], d_r.at[3 * i + jj], (x, y, 1 - c)).wait_send()
                _remote(got, got, d_s.at[3 * i + jj], d_r.at[3 * i + jj], (x, y, 1 - c)).wait_recv()

    return _pallas(
        body, name=f"gather_wait_{g}", out_shape=[pltpu.HBM(b.shape, b.dtype) for b in arrs],
        in_specs=[SEM, SEM] + [HBM] * m + [ANY], out_specs=[HBM] * m,
        input_output_aliases={2 + i: i for i in range(m)},
        compiler_params=pltpu.CompilerParams(has_side_effects=EFFECT),
    )(d_send, d_recv, *arrs, after)


def _pair_start(g, gs, after):
    n = len(gs)
    zones = [lax.empty((4, a.shape[1] // 2, a.shape[2]), a.dtype) for a in gs]
    extra = [] if after is None else [after]

    def body(*refs):
        k0 = 2 * n + len(extra)
        send_sems, recv_sems = refs[k0], refs[k0 + 1]
        src, dst = refs[k0 + 2:k0 + 2 + n], refs[k0 + 2 + n:k0 + 2 + 2 * n]
        token = refs[k0 + 2 + 2 * n]
        x, y, c, _ = _place()
        for a in range(n):
            h = src[a].shape[1] // 2
            _remote(src[a].at[:, pl.ds((1 - c) * h, h)], dst[a], send_sems.at[a], recv_sems.at[a], (x, y, 1 - c)).start()
        token[...] = jnp.zeros_like(token)

    return _pallas(
        body, name=f"grad_pair_start_{g}",
        out_shape=[pltpu.SemaphoreType.DMA((n,)), pltpu.SemaphoreType.DMA((n,))]
        + [pltpu.HBM(a.shape, a.dtype) for a in gs + zones] + [_sds((8, 128), F32)],
        in_specs=[HBM] * (2 * n) + [ANY] * len(extra),
        out_specs=[SEM, SEM] + [HBM] * (2 * n) + [pl.BlockSpec(memory_space=pltpu.VMEM)],
        input_output_aliases={i: 2 + i for i in range(2 * n)},
        compiler_params=pltpu.CompilerParams(has_side_effects=EFFECT),
    )(*[_in_hbm(a) for a in gs + zones], *extra)


def _pair_wait(g, send, recv, gs, zones):
    n = len(gs)

    def body(*refs):
        s_ref, r_ref = refs[0], refs[1]
        outs = refs[2 + 2 * n:]
        src, dst = outs[:n], outs[n:]
        x, y, c, _ = _place()
        for a in range(n):
            h = src[a].shape[1] // 2
            _remote(src[a].at[:, pl.ds((1 - c) * h, h)], dst[a], s_ref.at[a], r_ref.at[a], (x, y, 1 - c)).wait()

    return _pallas(
        body, name=f"grad_pair_wait_{g}", out_shape=[pltpu.HBM(a.shape, a.dtype) for a in gs + zones],
        in_specs=[SEM, SEM] + [HBM] * (2 * n), out_specs=[HBM] * (2 * n),
        input_output_aliases={2 + i: i for i in range(2 * n)},
        compiler_params=pltpu.CompilerParams(has_side_effects=EFFECT),
    )(send, recv, *gs, *zones)


def _pair_add(gs, others, c_arr):
    n = len(gs)

    def body(c_ref, *refs):
        for g_ref, o_ref, out_ref in zip(refs[:n], refs[n:2 * n], refs[2 * n:]):
            out_ref[...] = (g_ref[...].astype(F32) + o_ref[...].astype(F32)).astype(BF16)

    half = lambda a: pl.BlockSpec((1, a.shape[1] // 2, a.shape[2]), lambda k, c_ref: (k, c_ref[0], 0))
    whole = lambda a: pl.BlockSpec((1,) + a.shape[1:], lambda k, c_ref: (k, 0, 0))
    grid_spec = pltpu.PrefetchScalarGridSpec(
        num_scalar_prefetch=1, grid=(4,), in_specs=[half(a) for a in gs] + [whole(o) for o in others],
        out_specs=[whole(o) for o in others])
    return _pallas(body, name="grad_pair_add", grid_spec=grid_spec, out_shape=[_sds(o.shape, BF16) for o in others],
                   compiler_params=_cp(("parallel",)))(c_arr, *gs, *others)


def _chip_start(g, ss):
    n = len(ss)
    zones = [lax.empty((3,) + s.shape[1:], s.dtype) for s in ss]

    def body(*refs):
        send_sems, recv_sems = refs[2 * n], refs[2 * n + 1]
        src, dst = refs[2 * n + 2:3 * n + 2], refs[3 * n + 2:4 * n + 2]
        token = refs[4 * n + 2]
        x, y, c, chips = _place()
        for a in range(n):
            for jj, (px, py) in enumerate(chips):
                k = 3 * a + jj
                _remote(src[a].at[2 * px + py], dst[a].at[jj], send_sems.at[k], recv_sems.at[k], (px, py, c)).start()
        token[...] = jnp.zeros_like(token)

    return _pallas(
        body, name=f"grad_chip_start_{g}",
        out_shape=[pltpu.SemaphoreType.DMA((3 * n,)), pltpu.SemaphoreType.DMA((3 * n,))]
        + [pltpu.HBM(a.shape, a.dtype) for a in ss + zones] + [_sds((8, 128), F32)],
        in_specs=[HBM] * (2 * n), out_specs=[SEM, SEM] + [HBM] * (2 * n) + [pl.BlockSpec(memory_space=pltpu.VMEM)],
        input_output_aliases={i: 2 + i for i in range(2 * n)},
        compiler_params=pltpu.CompilerParams(has_side_effects=EFFECT),
    )(*[_in_hbm(a) for a in ss + zones])


def _chip_wait(tag, sends, recvs, counts, ss, zones, after):
    nb, n = len(sends), len(ss)

    def body(*refs):
        s_refs, r_refs = refs[:nb], refs[nb:2 * nb]
        outs = refs[2 * nb + 2 * n + 1:]
        src, dst = outs[:n], outs[n:]
        x, y, c, chips = _place()
        a = 0
        for b in range(nb):
            for i in range(counts[b]):
                for jj, (px, py) in enumerate(chips):
                    k = 3 * i + jj
                    _remote(src[a].at[2 * px + py], dst[a].at[jj], s_refs[b].at[k], r_refs[b].at[k], (px, py, c)).wait()
                a += 1

    return _pallas(
        body, name=f"grad_chip_wait_{tag}", out_shape=[pltpu.HBM(a.shape, a.dtype) for a in ss + zones],
        in_specs=[SEM] * (2 * nb) + [HBM] * (2 * n) + [ANY], out_specs=[HBM] * (2 * n),
        input_output_aliases={2 * nb + i: i for i in range(2 * n)},
        compiler_params=pltpu.CompilerParams(has_side_effects=EFFECT),
    )(*sends, *recvs, *ss, *zones, after)


def _chip_sum(s, r, where, dest, l, L):
    _, h, C = s.shape
    tr = h // 2

    def body(k_ref, s_ref, r_ref, *rest):
        out_ref = rest[-1]
        acc = s_ref[0].astype(F32)
        for jj in range(3):
            acc = acc + r_ref[jj].astype(F32)
        out_ref[...] = acc

    in_specs = [pl.BlockSpec((1, tr, C), lambda i, k_ref: (k_ref[0], i, 0)), pl.BlockSpec((3, tr, C), lambda i, k_ref: (0, i, 0))]
    args = [where, s, r]
    alias = {}
    if dest is not None:
        in_specs.append(ANY)
        args.append(dest)
        alias = {3: 0}
    grid_spec = pltpu.PrefetchScalarGridSpec(
        num_scalar_prefetch=1, grid=(2,), in_specs=in_specs,
        out_specs=pl.BlockSpec((None, tr, C), lambda i, k_ref: (l, 2 * k_ref[1] + i, 0)))
    return _pallas(body, name="grad_chip_sum", grid_spec=grid_spec, out_shape=_sds((L, 2 * h, C), F32),
                   input_output_aliases=alias, compiler_params=_cp(("arbitrary",)))(*args)


def _share_start(tag, bufs, layout):
    n, n_buf = len(layout), len(bufs)

    def body(*refs):
        send_sems, recv_sems = refs[n_buf], refs[n_buf + 1]
        outs = refs[n_buf + 2:]
        x, y, c, _ = _place()
        for a, (o, l) in enumerate(layout):
            h = outs[o].shape[1] // 2
            blk = outs[o].at[l, pl.ds(c * h, h)]
            _remote(blk, blk, send_sems.at[a], recv_sems.at[a], (x, y, 1 - c)).start()

    return _pallas(
        body, name=f"grad_share_start_{tag}",
        out_shape=[pltpu.SemaphoreType.DMA((n,)), pltpu.SemaphoreType.DMA((n,))] + [pltpu.HBM(b.shape, b.dtype) for b in bufs],
        in_specs=[HBM] * n_buf, out_specs=[SEM, SEM] + [HBM] * n_buf, input_output_aliases={o: 2 + o for o in range(n_buf)},
        compiler_params=pltpu.CompilerParams(has_side_effects=EFFECT),
    )(*[_in_hbm(b) for b in bufs])


def _share_wait(tag, send, recv, bufs, layout, after):
    n_buf = len(bufs)

    def body(*refs):
        s_ref, r_ref = refs[0], refs[1]
        outs = refs[n_buf + 3:]
        x, y, c, _ = _place()
        for a, (o, l) in enumerate(layout):
            h = outs[o].shape[1] // 2
            mine, theirs = outs[o].at[l, pl.ds(c * h, h)], outs[o].at[l, pl.ds((1 - c) * h, h)]
            _remote(mine, mine, s_ref.at[a], r_ref.at[a], (x, y, 1 - c)).wait_send()
            _remote(theirs, theirs, s_ref.at[a], r_ref.at[a], (x, y, 1 - c)).wait_recv()

    return _pallas(
        body, name=f"grad_share_wait_{tag}", out_shape=[pltpu.HBM(b.shape, b.dtype) for b in bufs],
        in_specs=[SEM, SEM] + [HBM] * n_buf + [ANY], out_specs=[HBM] * n_buf,
        input_output_aliases={2 + o: o for o in range(n_buf)},
        compiler_params=pltpu.CompilerParams(has_side_effects=EFFECT),
    )(send, recv, *bufs, after)


def _small_all_reduce(packed, after):
    P, L = packed.shape

    def body(in_ref, after_ref, out_ref, slots, send_sems, recv_sems):
        x, y, c, _ = _place()
        me = 4 * x + 2 * y + c
        slots[me] = in_ref[...]
        cps = []
        for r in range(1, 8):
            px = 1 - x if r & 4 else x
            py = 1 - y if r & 2 else y
            pc = 1 - c if r & 1 else c
            cps.append(_remote(in_ref, slots.at[me], send_sems.at[r - 1], recv_sems.at[r - 1], (px, py, pc)))
        for cp in cps:
            cp.start()
        for r in range(1, 8):
            px = 1 - x if r & 4 else x
            py = 1 - y if r & 2 else y
            pc = 1 - c if r & 1 else c
            blk = slots.at[4 * px + 2 * py + pc]
            _remote(blk, blk, send_sems.at[r - 1], recv_sems.at[r - 1], (px, py, pc)).wait_recv()
        for cp in cps:
            cp.wait_send()
        acc = slots[0]
        for k in range(1, 8):
            acc = acc + slots[k]
        out_ref[...] = acc

    vm = pl.BlockSpec(memory_space=pltpu.VMEM)
    return _pallas(body, name="small_all_reduce", in_specs=[vm, ANY], out_specs=vm, out_shape=_sds((P, L), F32),
                   scratch_shapes=[pltpu.VMEM((8, P, L), F32), pltpu.SemaphoreType.DMA((7,)),
                                   pltpu.SemaphoreType.DMA((7,))])(packed, after)


def _adamw_math(w, g, m, v):
    m = ADAM_B1 * m + (1.0 - ADAM_B1) * g
    v = ADAM_B2 * v + (1.0 - ADAM_B2) * (g * g)
    m_hat = m / (1.0 - ADAM_B1 ** ADAM_STEP)
    v_hat = v / (1.0 - ADAM_B2 ** ADAM_STEP)
    delta = -ADAM_LR * (m_hat / (jnp.sqrt(v_hat) + ADAM_EPS) + ADAM_WD * w)
    return delta, m, v


def _adamw(w, g, m, v):
    shape = w.shape
    C = shape[-1]
    rows = math.prod(shape[:-1])
    tr = next(t for t in (512, 352, 256, 128, 64, 32, 16, 8, rows) if rows % t == 0)
    w2, g2, m2, v2 = (a.reshape(rows, C) for a in (w, g, m, v))

    def body(w_ref, g_ref, m_ref, v_ref, go_ref, d_ref, nm_ref, nv_ref):
        gv = g_ref[...]
        d, nm, nv = _adamw_math(w_ref[...], gv, m_ref[...], v_ref[...])
        go_ref[...] = gv
        d_ref[...] = d
        nm_ref[...] = nm
        nv_ref[...] = nv

    blk = pl.BlockSpec((tr, C), lambda i: (i, 0))
    outs = _pallas(body, name="adamw", grid=(rows // tr,), in_specs=[blk] * 4, out_specs=[blk] * 4,
                   out_shape=[_sds((rows, C), F32)] * 4, compiler_params=_cp(("parallel",)))(w2, g2, m2, v2)
    return tuple(o.reshape(shape) for o in outs)


WEIGHTS = ["ffn1_norm", "ffn1_w_gate", "ffn1_w_up", "ffn1_w_down", "mix_norm", "ffn2_norm", "ffn2_w_gate", "ffn2_w_up",
           "ffn2_w_down", "ev_w_in", "ev_b_f", "ev_conv_w", "ev_conv_b", "ev_conv_norm", "ev_q_norm", "ev_k_norm",
           "ev_w_out", "od_w_in", "od_conv_w", "od_w_out"]
BIG = ([("ffn1_w_gate", 0), ("ffn1_w_up", 0), ("ffn1_w_down", 0), ("ev_w_in", 0), ("ev_w_out", 0),
        ("ffn2_w_gate", 0), ("ffn2_w_up", 0), ("ffn2_w_down", 0)]
       + [("ffn1_w_gate", 1), ("ffn1_w_up", 1), ("ffn1_w_down", 1), ("od_w_in", 0), ("od_w_out", 0),
          ("ffn2_w_gate", 1), ("ffn2_w_up", 1), ("ffn2_w_down", 1)])
TRANSPOSED = ("ffn1_w_gate", "ffn1_w_up", "ffn2_w_gate", "ffn2_w_up")
SHARED_LAST = ("ffn1_w_gate", "ffn1_w_up", "ffn1_w_down", "ev_w_in", "ev_w_out")
BLOCKS = [("ffn1", 0), ("ev", 0), ("ffn2", 0), ("ffn1", 1), ("od", 0), ("ffn2", 1)]
BLOCK_OF = {(name, l): (name.split("_w_")[0], l) for name, l in BIG}
BIG_NAMES = ["ffn1_w_gate", "ffn1_w_up", "ffn1_w_down", "ffn2_w_gate", "ffn2_w_up", "ffn2_w_down",
             "ev_w_in", "ev_w_out", "od_w_in", "od_w_out"]
SMALL = [("ffn1_norm", 16), ("mix_norm", 16), ("ffn2_norm", 16), ("ev_b_f", 8), ("ev_conv_w", 128), ("ev_conv_b", 8),
         ("ev_conv_norm", 8), ("ev_q_norm", 8), ("ev_k_norm", 8), ("od_conv_w", 24)]


def _to_lanes(a, rows):
    flat = a.reshape(-1)
    return jnp.pad(flat, (0, rows * 128 - flat.shape[0])).reshape(rows, 128)


def kernel(x, ffn1_norm, ffn1_w_gate, ffn1_w_up, ffn1_w_down, mix_norm, ffn2_norm, ffn2_w_gate, ffn2_w_up, ffn2_w_down, ev_w_in, ev_b_f, ev_conv_w, ev_conv_b, ev_conv_norm, ev_q_norm, ev_k_norm, ev_w_out, od_w_in, od_conv_w, od_w_out, loss_target, m_ffn1_norm, m_ffn1_w_gate, m_ffn1_w_up, m_ffn1_w_down, m_mix_norm, m_ffn2_norm, m_ffn2_w_gate, m_ffn2_w_up, m_ffn2_w_down, m_ev_w_in, m_ev_b_f, m_ev_conv_w, m_ev_conv_b, m_ev_conv_norm, m_ev_q_norm, m_ev_k_norm, m_ev_w_out, m_od_w_in, m_od_conv_w, m_od_w_out, v_ffn1_norm, v_ffn1_w_gate, v_ffn1_w_up, v_ffn1_w_down, v_mix_norm, v_ffn2_norm, v_ffn2_w_gate, v_ffn2_w_up, v_ffn2_w_down, v_ev_w_in, v_ev_b_f, v_ev_conv_w, v_ev_conv_b, v_ev_conv_norm, v_ev_q_norm, v_ev_k_norm, v_ev_w_out, v_od_w_in, v_od_conv_w, v_od_w_out):
    P = dict(ffn1_norm=ffn1_norm, ffn1_w_gate=ffn1_w_gate, ffn1_w_up=ffn1_w_up, ffn1_w_down=ffn1_w_down, mix_norm=mix_norm,
             ffn2_norm=ffn2_norm, ffn2_w_gate=ffn2_w_gate, ffn2_w_up=ffn2_w_up, ffn2_w_down=ffn2_w_down, ev_w_in=ev_w_in,
             ev_b_f=ev_b_f, ev_conv_w=ev_conv_w, ev_conv_b=ev_conv_b, ev_conv_norm=ev_conv_norm, ev_q_norm=ev_q_norm,
             ev_k_norm=ev_k_norm, ev_w_out=ev_w_out, od_w_in=od_w_in, od_conv_w=od_conv_w, od_w_out=od_w_out)
    M = dict(zip(WEIGHTS, [m_ffn1_norm, m_ffn1_w_gate, m_ffn1_w_up, m_ffn1_w_down, m_mix_norm, m_ffn2_norm, m_ffn2_w_gate,
                           m_ffn2_w_up, m_ffn2_w_down, m_ev_w_in, m_ev_b_f, m_ev_conv_w, m_ev_conv_b, m_ev_conv_norm,
                           m_ev_q_norm, m_ev_k_norm, m_ev_w_out, m_od_w_in, m_od_conv_w, m_od_w_out]))
    V = dict(zip(WEIGHTS, [v_ffn1_norm, v_ffn1_w_gate, v_ffn1_w_up, v_ffn1_w_down, v_mix_norm, v_ffn2_norm, v_ffn2_w_gate,
                           v_ffn2_w_up, v_ffn2_w_down, v_ev_w_in, v_ev_b_f, v_ev_conv_w, v_ev_conv_b, v_ev_conv_norm,
                           v_ev_q_norm, v_ev_k_norm, v_ev_w_out, v_od_w_in, v_od_conv_w, v_od_w_out]))
    for name in TRANSPOSED:
        P[name], M[name], V[name] = (jnp.swapaxes(a, 1, 2) for a in (P[name], M[name], V[name]))
    S, D = x.shape[1], x.shape[2]
    chip = 2 * lax.axis_index("x") + lax.axis_index("y")
    core = lax.axis_index("c")

    def own_slot(shard):
        return lax.dynamic_update_slice(lax.empty((4,) + shard.shape, shard.dtype), shard[None], (chip, 0, 0))

    taps = jnp.concatenate([_to_lanes(_pad_rows(ev_conv_w[0], 32), 32), _to_lanes(_pad_rows(od_conv_w[0], 8), 16)], axis=0)
    first = [i for i, k in enumerate(BIG) if BLOCK_OF[k] in BLOCKS[:2]]
    rest = [i for i in range(len(BIG)) if i not in first]
    send0, recv0, *bufs0 = _ag_start("first", [own_slot(P[BIG[i][0]][BIG[i][1]].astype(BF16)) for i in first]
                                     + [own_slot(taps)], True)
    zero = bufs0.pop()[0, 0]
    send1, recv1, *bufs1 = _ag_start("rest", [own_slot((P[BIG[i][0]][BIG[i][1]] + zero).astype(BF16)) for i in rest], False)
    bufs1.pop()
    cols = lambda a: a.transpose(1, 0, 2).reshape(a.shape[1], 4 * a.shape[2])
    W = {k: P[k] for k in ("ffn1_norm", "mix_norm", "ffn2_norm", "ev_b_f", "ev_q_norm", "ev_k_norm")}
    W["ev_conv_b"], W["ev_conv_norm"] = ev_conv_b, ev_conv_norm
    for tag in ("ffn1", "ffn2"):
        for kind in ("_w_gate", "_w_up", "_w_down"):
            W[tag + kind] = [None, None]
    passing = {}

    def pass_on(g, after):
        idx = [i for i, k in enumerate(BIG) if BLOCK_OF[k] == BLOCKS[g]]
        keys = [BIG[i] for i in idx] + (["taps"] if BLOCKS[g] == ("ev", 0) else [])
        send, recv, bufs, members = (send0, recv0, bufs0, first) if g < 2 else (send1, recv1, bufs1, rest)
        local = [members.index(i) for i in idx]
        passing[g] = (keys, _ag_mid(g, send, recv, [bufs[i] for i in local], local,
                                    bufs0[-1] if BLOCKS[g] == ("ev", 0) else None, len(first), after))

    def need(block, after):
        g = BLOCKS.index(block)
        if g not in passing:
            pass_on(g, bufs1[0] if g == 0 else after)
        keys, (d_send, d_recv, *thru) = passing.pop(g)
        got = dict(zip(keys, _ag_wait(g, d_send, d_recv, thru, len(keys) - ("taps" in keys), after)))
        if 1 <= g < len(BLOCKS) - 1:
            pass_on(g + 1, after)
        for key, a in got.items():
            if key == "taps":
                continue
            name, l = key
            if name.startswith("ffn"):
                W[name][l] = a
            elif name.endswith("_w_in"):
                W[name] = cols(a)
            elif name.endswith("_w_out"):
                W[name] = a.reshape(4 * a.shape[1], D)
        if block == ("ev", 0):
            taps_all = got["taps"]
            W["ev_conv_w"] = cols(taps_all[:, :32].reshape(4, 32, 128))[:CONV_A_WIDTH]
            W["od_conv_w"] = cols(taps_all[:, 32:48].reshape(4, 8, 256))[:CONV_C_WIDTH]

    rows = lambda a: a.reshape(4, a.shape[0] // 4, a.shape[1])
    colsh = lambda a: a.reshape(a.shape[0], 4, a.shape[1] // 4).transpose(1, 0, 2)
    c_arr = core.reshape(1).astype(jnp.int32)
    where = jnp.stack([chip, core]).astype(jnp.int32)
    in_flight = []

    def done(block, block_grads):
        g = BLOCKS.index(block)
        keys = list(block_grads)
        gs = []
        for name, l in keys:
            a = block_grads[(name, l)]
            gs.append(colsh(a) if name == "ev_w_in" else rows(a) if name.endswith("_w_out") else a)
        for item in list(pairs):
            to_chips(item)
        send, recv, *rest = _pair_start(g, gs, chained.get("token"))
        n = len(keys)
        pairs.append((g, keys, send, recv, rest[:n], rest[n:2 * n]))
        if g == 0:
            to_chips(pairs[0])
        chained["token"] = rest[-1] if g else chained["token"]
        return chained["token"][0:1, 0:1]

    pairs, chained = [], {}

    def to_chips(item):
        pairs.remove(item)
        g, keys, send, recv, gs, zones = item
        n = len(keys)
        done_ = _pair_wait(g, send, recv, gs, zones)
        sums = list(_pair_add(list(done_[:n]), list(done_[n:]), c_arr))
        send2, recv2, *rest = _chip_start(g, sums)
        in_flight.append((keys, send2, recv2, rest[:n], rest[n:2 * n]))
        chained["token"] = rest[-1]

    loss, grad_x, grads = _local_step(x[0], loss_target[0], W, need, done)

    order = [k for keys, *_ in in_flight for k in keys]
    landed = _chip_wait("all", [f[1] for f in in_flight], [f[2] for f in in_flight], [len(f[0]) for f in in_flight],
                        [a for f in in_flight for a in f[3]], [a for f in in_flight for a in f[4]], grad_x)
    sums, recvd = landed[:len(order)], landed[len(order):]
    stacked, shares = {}, []
    for tag, names in (("a", [n for n in BIG_NAMES if n not in SHARED_LAST]), ("b", list(SHARED_LAST))):
        for (name, l), s, r in zip(order, sums, recvd):
            if name in names:
                stacked[name] = _chip_sum(s, r, where, stacked.get(name), l, P[name].shape[0])
        layout = [(names.index(name), l) for name, l in order if name in names]
        send, recv, *thru = _share_start(tag, [stacked[name] for name in names], layout)
        shares.append((tag, names, send, recv, thru, layout))

    def small_grad(name):
        if name.endswith("_norm") and name[:3] in ("ffn", "mix"):
            return jnp.concatenate([grads[(name, 0)], grads[(name, 1)]], axis=0)
        return grads[(name, 0)]

    packed = jnp.concatenate([_to_lanes(small_grad(name), r) for name, r in SMALL], axis=0)
    total = _small_all_reduce(packed, shares[-1][4][0])
    small_grads, at = {}, 0
    for name, r in SMALL:
        part = total[at:at + r].reshape(-1)
        at += r
        if name == "ev_conv_w":
            full_g = part[:CONV_A_WIDTH * D_CONV].reshape(CONV_A_WIDTH, D_CONV)
            small_grads[name] = lax.dynamic_slice_in_dim(full_g, chip * (D_CONV // 4), D_CONV // 4, axis=1)[None]
        elif name == "od_conv_w":
            full_g = part[:CONV_C_WIDTH * D].reshape(CONV_C_WIDTH, D)
            small_grads[name] = lax.dynamic_slice_in_dim(full_g, chip * (D // 4), D // 4, axis=1)[None]
        else:
            small_grads[name] = part[:math.prod(P[name].shape)].reshape(P[name].shape)

    results = {}

    def update(name, g):
        outs = _adamw(P[name], g, M[name], V[name])
        results[name] = tuple(jnp.swapaxes(a, 1, 2) for a in outs) if name in TRANSPOSED else outs

    for name, _ in SMALL:
        update(name, small_grads[name])
    after = results[SMALL[-1][0]][1]
    for tag, names, send, recv, thru, layout in shares:
        for name, g in zip(names, _share_wait(tag, send, recv, thru, layout, after)):
            update(name, g)
        after = results[names[-1]][1]
    loss_all = lax.psum(loss[0, 0], ("x", "y", "c"))
    return (loss_all, grad_x[None], *[results[name][k] for k in range(4) for name in WEIGHTS])
```

```python
import functools
import math

import jax
import jax.numpy as jnp
from jax import lax
from jax.experimental import pallas as pl
from jax.experimental.pallas import tpu as pltpu

F32, BF16 = jnp.float32, jnp.bfloat16
EPS = 1e-6
FFN_RES = 0.5
N_HEADS, HEAD_DIM = 8, 64
D_CONV = 512
D_ATTN = N_HEADS * HEAD_DIM
CONV_A_WIDTH, CONV_C_WIDTH = 31, 3
ADAM_LR, ADAM_B1, ADAM_B2, ADAM_EPS, ADAM_WD, ADAM_STEP = 0.001, 0.9, 0.999, 1e-08, 0.01, 10
MESH = pl.DeviceIdType.MESH
ANY = pl.BlockSpec(memory_space=pl.ANY)

TOK_TILE = 512
FFN_TILE = 512
DW_TILE = 1024
ATT_TILE = 1024
QKN_TILE = 2048
HALO_A, HALO_C = 32, 16
SUBLANES = 8
CONV_ROWS = 64
SCAN_BLK = 256
MIB = 2 ** 20


def _pallas(body, **kw):
    return pl.pallas_call(body, **kw)


def _cp(sem=None, vmem_mib=48):
    return pltpu.CompilerParams(dimension_semantics=sem, vmem_limit_bytes=vmem_mib * MIB)


def _dot(a, b):
    return jnp.dot(a, b, preferred_element_type=F32)


def _dot_nt(a, b):
    return lax.dot_general(a, b, (((1,), (1,)), ((), ())), preferred_element_type=F32)


def _dot_tn(a, b):
    return lax.dot_general(a, b, (((0,), (0,)), ((), ())), preferred_element_type=F32)


def _sds(shape, dtype):
    return jax.ShapeDtypeStruct(shape, dtype)


def _rms(x):
    return lax.rsqrt(jnp.mean(x * x, axis=-1, keepdims=True) + EPS)


def _rms_bwd(dy, x, g):
    r = _rms(x)
    xh = x * r
    dxh = dy * g
    dx = r * (dxh - xh * jnp.mean(dxh * xh, axis=-1, keepdims=True))
    return dx, xh


def _silu_grad(z):
    s = jax.nn.sigmoid(z)
    return s * (1.0 + z * (1.0 - s))


def _ffn_fwd(x, g, wg, wu, wd):
    S, D = x.shape
    nc, Fs, _ = wd.shape
    tm = min(FFN_TILE, S)
    per = nc
    steps = nc // per

    def body(x_ref, g_ref, wg_ref, wu_ref, wd_ref, out_ref, xn_ref, G_ref, U_ref, acc_ref):
        j = pl.program_id(1)

        @pl.when(j == 0)
        def _():
            xv = x_ref[...]
            xn_ref[...] = (xv * _rms(xv) * g_ref[...]).astype(BF16)
            acc_ref[...] = jnp.zeros_like(acc_ref)

        xn = xn_ref[...]
        part = None
        for k in range(per):
            G = _dot_nt(xn, wg_ref[k])
            U = _dot_nt(xn, wu_ref[k])
            G_ref[k] = G.astype(BF16)
            U_ref[k] = U.astype(BF16)
            term = _dot((G * jax.nn.sigmoid(G) * U).astype(BF16), wd_ref[k])
            part = term if part is None else part + term
        acc_ref[...] += part

        @pl.when(j == steps - 1)
        def _():
            out_ref[...] = x_ref[...] + FFN_RES * acc_ref[...]

    row = pl.BlockSpec((tm, D), lambda i, j: (i, 0))
    wblk = pl.BlockSpec((per, Fs, D), lambda i, j: (j, 0, 0), pipeline_mode=pl.Buffered(1))
    hid = pl.BlockSpec((per, tm, Fs), lambda i, j: (j, i, 0))
    return _pallas(
        body, name="ffn_fwd", grid=(S // tm, steps),
        in_specs=[row, pl.BlockSpec((1, D), lambda i, j: (0, 0)), wblk, wblk, wblk],
        out_specs=[row, row, hid, hid],
        out_shape=[_sds((S, D), F32), _sds((S, D), BF16), _sds((nc, S, Fs), BF16), _sds((nc, S, Fs), BF16)],
        scratch_shapes=[pltpu.VMEM((tm, D), F32)],
        compiler_params=_cp(("parallel", "arbitrary"), 56),
    )(x, g, wg, wu, wd)


def _ffn_bwd_w(dout, xn, G, U, wd):
    S, D = dout.shape
    nc, _, Fs = G.shape
    tm = min(DW_TILE, S)
    nt = S // tm
    sub = min(TOK_TILE, tm)

    def body(do_ref, xn_ref, G_ref, U_ref, wd_ref, dwg_ref, dwu_ref, dwd_ref, dG_ref, dU_ref, ag, au, ad, do_s, H_s):
        i = pl.program_id(1)

        @pl.when(i == 0)
        def _():
            ag[...] = jnp.zeros_like(ag)
            au[...] = jnp.zeros_like(au)
            ad[...] = jnp.zeros_like(ad)

        for r in range(0, tm, sub):
            rows = pl.ds(r, sub)
            do = (FFN_RES * do_ref[rows, :]).astype(BF16)
            do_s[rows, :] = do
            Gv = G_ref[0, rows, :].astype(F32)
            Uv = U_ref[0, rows, :].astype(F32)
            dH = _dot_nt(do, wd_ref[0])
            sg = jax.nn.sigmoid(Gv)
            act = Gv * sg
            H_s[rows, :] = (act * Uv).astype(BF16)
            dU_ref[0, rows, :] = (dH * act).astype(BF16)
            dG_ref[0, rows, :] = (dH * Uv * (sg * (1.0 + Gv * (1.0 - sg)))).astype(BF16)
        xnv = xn_ref[...]
        ag[...] += _dot_tn(dG_ref[0], xnv)
        au[...] += _dot_tn(dU_ref[0], xnv)
        ad[...] += _dot_tn(H_s[...], do_s[...])

        @pl.when(i == nt - 1)
        def _():
            dwg_ref[0] = ag[...].astype(BF16)
            dwu_ref[0] = au[...].astype(BF16)
            dwd_ref[0] = ad[...].astype(BF16)

    row = pl.BlockSpec((tm, D), lambda j, i: (i, 0))
    hid = pl.BlockSpec((1, tm, Fs), lambda j, i: (j, i, 0))
    wrow = pl.BlockSpec((1, Fs, D), lambda j, i: (j, 0, 0))
    return _pallas(
        body, name="ffn_bwd_w", grid=(nc, nt),
        in_specs=[row, row, hid, hid, wrow],
        out_specs=[wrow, wrow, wrow, hid, hid],
        out_shape=[_sds((nc, Fs, D), BF16)] * 3 + [_sds((nc, S, Fs), BF16)] * 2,
        scratch_shapes=[pltpu.VMEM((Fs, D), F32)] * 3 + [pltpu.VMEM((tm, D), BF16), pltpu.VMEM((tm, Fs), BF16)],
        compiler_params=_cp(("parallel", "arbitrary"), 56),
    )(dout, xn, G, U, wd)


def _norm_in_bwd(dzs, ws, x, g, dres, w_rows=False):
    S, D = x.shape
    nc = dzs[0].shape[0]
    n = len(dzs)
    tm = TOK_TILE
    per = nc
    steps = nc // per

    def body(*refs):
        dz_refs, w_refs = refs[:n], refs[n:2 * n]
        x_ref, g_ref, dres_ref, dx_ref, dg_ref, acc_ref = refs[2 * n:]
        i, j = pl.program_id(0), pl.program_id(1)

        @pl.when(j == 0)
        def _():
            acc_ref[...] = jnp.zeros_like(acc_ref)

        @pl.when((i == 0) & (j == 0))
        def _():
            dg_ref[...] = jnp.zeros_like(dg_ref)

        part = None
        for dz_ref, w_ref in zip(dz_refs, w_refs):
            for k in range(per):
                term = _dot(dz_ref[k], w_ref[k]) if w_rows else _dot_nt(dz_ref[k], w_ref[k])
                part = term if part is None else part + term
        acc_ref[...] += part

        @pl.when(j == steps - 1)
        def _():
            dxn = acc_ref[...]
            dx, xh = _rms_bwd(dxn, x_ref[...], g_ref[...])
            dx_ref[...] = dx + dres_ref[...]
            dg_ref[...] += jnp.sum(dxn * xh, axis=0, keepdims=True)

    row = pl.BlockSpec((tm, D), lambda i, j: (i, 0))
    one = pl.BlockSpec((1, D), lambda i, j: (0, 0))
    in_specs = [pl.BlockSpec((per, tm, dz.shape[2]), lambda i, j: (j, i, 0)) for dz in dzs]
    in_specs += [pl.BlockSpec((per,) + w.shape[1:], lambda i, j: (j, 0, 0)) for w in ws]
    return _pallas(
        body, name="norm_in_bwd", grid=(S // tm, steps),
        in_specs=in_specs + [row, one, row], out_specs=[row, one],
        out_shape=[_sds((S, D), F32), _sds((1, D), F32)],
        scratch_shapes=[pltpu.VMEM((tm, D), F32)],
        compiler_params=_cp(("arbitrary", "arbitrary")),
    )(*dzs, *ws, x, g, dres)


def _norm_proj(x, g, w, w2=None):
    S, D = x.shape
    N = w.shape[1]
    tm = TOK_TILE

    def body(*refs):
        if w2 is None:
            x_ref, g_ref, w_ref, h_ref, z_ref = refs
        else:
            x_ref, g_ref, w_ref, w2_ref, h_ref, z_ref, z2_ref = refs
        xv = x_ref[...]
        h = (xv * _rms(xv) * g_ref[...]).astype(BF16)
        h_ref[...] = h
        z_ref[...] = _dot(h, w_ref[...]).astype(BF16)
        if w2 is not None:
            z2_ref[...] = _dot(h, w2_ref[...])

    row = pl.BlockSpec((tm, D), lambda i: (i, 0))
    in_specs = [row, pl.BlockSpec((1, D), lambda i: (0, 0)), pl.BlockSpec((D, N), lambda i: (0, 0))]
    out_specs = [row, pl.BlockSpec((tm, N), lambda i: (i, 0))]
    out_shape = [_sds((S, D), BF16), _sds((S, N), BF16)]
    args = [x, g, w]
    if w2 is not None:
        N2 = w2.shape[1]
        in_specs.append(pl.BlockSpec((D, N2), lambda i: (0, 0)))
        out_specs.append(pl.BlockSpec((tm, N2), lambda i: (i, 0)))
        out_shape.append(_sds((S, N2), F32))
        args.append(w2)
    return _pallas(body, name="norm_proj", grid=(S // tm,), in_specs=in_specs, out_specs=out_specs,
                   out_shape=out_shape, compiler_params=_cp(("parallel",)))(*args)


def _proj_res(acts, ws, res):
    S, D = res.shape
    n = len(acts)
    tm = TOK_TILE

    def body(*refs):
        a_refs, w_refs = refs[:n], refs[n:2 * n]
        res_ref, out_ref = refs[2 * n:]
        acc = res_ref[...]
        for a_ref, w_ref in zip(a_refs, w_refs):
            acc = acc + _dot(a_ref[...], w_ref[...])
        out_ref[...] = acc

    row = pl.BlockSpec((tm, D), lambda i: (i, 0))
    in_specs = [pl.BlockSpec((tm, a.shape[1]), lambda i: (i, 0)) for a in acts]
    in_specs += [pl.BlockSpec(w.shape, lambda i: (0, 0)) for w in ws]
    return _pallas(body, name="proj_res", grid=(S // tm,), in_specs=in_specs + [row], out_specs=row,
                   out_shape=_sds((S, D), F32), compiler_params=_cp(("parallel",)))(*acts, *ws, res)


def _matmul_nt(a, w, after=None):
    S, K = a.shape
    M = w.shape[0]
    tm = TOK_TILE

    def body(a_ref, w_ref, *rest):
        rest[-1][...] = _dot_nt(a_ref[...].astype(BF16), w_ref[...])

    extra = [] if after is None else [after]
    return _pallas(body, name="matmul_nt", grid=(S // tm,),
                   in_specs=[pl.BlockSpec((tm, K), lambda i: (i, 0)), pl.BlockSpec((M, K), lambda i: (0, 0))] + [ANY] * len(extra),
                   out_specs=pl.BlockSpec((tm, M), lambda i: (i, 0)), out_shape=_sds((S, M), F32),
                   compiler_params=_cp(("parallel",)))(a, w, *extra)


def _matmul_tn(a, b, tn):
    S, M = a.shape
    N = b.shape[1]
    tm = min(DW_TILE, S)
    nt = S // tm

    def body(a_ref, b_ref, o_ref, acc_ref):
        i = pl.program_id(1)

        @pl.when(i == 0)
        def _():
            acc_ref[...] = jnp.zeros_like(acc_ref)

        acc_ref[...] += _dot_tn(a_ref[...].astype(BF16), b_ref[...].astype(BF16))

        @pl.when(i == nt - 1)
        def _():
            o_ref[0] = acc_ref[...].astype(BF16)

    return _pallas(body, name="matmul_tn", grid=(N // tn, nt),
                   in_specs=[pl.BlockSpec((tm, M), lambda j, i: (i, 0)), pl.BlockSpec((tm, tn), lambda j, i: (i, j))],
                   out_specs=pl.BlockSpec((1, M, tn), lambda j, i: (j, 0, 0)), out_shape=_sds((N // tn, M, tn), BF16),
                   scratch_shapes=[pltpu.VMEM((M, tn), F32)],
                   compiler_params=_cp(("parallel", "arbitrary")))(a, b)


def _fill_shifts(win, rows):
    for b in range(1, SUBLANES):
        win[b, pl.ds(0, rows - SUBLANES), :] = win[0, pl.ds(b, rows - SUBLANES), :]


def _tap(win, offset, n, base=0):
    start = base + (offset - offset % SUBLANES)
    if not isinstance(start, int):
        start = pl.multiple_of(start, SUBLANES)
    return win[offset % SUBLANES, pl.ds(start, n), :]


def _conv_a_fwd(z, cw, cb, cn):
    S = z.shape[0]
    C = D_CONV
    tm = TOK_TILE
    hb = tm // HALO_A

    def body(u_ref, gt_ref, up_ref, gp_ref, cw_ref, cb_ref, cn_ref, a_ref, a1_ref, win):
        i = pl.program_id(0)
        prev = up_ref[...].astype(F32) * jax.nn.sigmoid(gp_ref[...].astype(F32))
        win[0, pl.ds(0, HALO_A), :] = jnp.where(i == 0, 0.0, prev)
        win[0, pl.ds(HALO_A, tm), :] = u_ref[...].astype(F32) * jax.nn.sigmoid(gt_ref[...].astype(F32))
        _fill_shifts(win, tm + HALO_A)

        acc = jnp.zeros((tm, C), F32)
        for k in range(CONV_A_WIDTH):
            acc = acc + cw_ref[k:k + 1, :] * _tap(win, HALO_A - (CONV_A_WIDTH - 1) + k, tm)
        a1 = acc + cb_ref[...]
        a1_ref[...] = a1
        a2 = a1 * _rms(a1) * cn_ref[...]
        a_ref[...] = (a2 * jax.nn.sigmoid(a2)).astype(BF16)

    cur = lambda c: pl.BlockSpec((tm, C), lambda i, c=c: (i, c))
    prv = lambda c: pl.BlockSpec((HALO_A, C), lambda i, c=c: (jnp.maximum(i * hb - 1, 0), c))
    vec = pl.BlockSpec((1, C), lambda i: (0, 0))
    return _pallas(body, name="conv_a_fwd", grid=(S // tm,),
                   in_specs=[cur(0), cur(1), prv(0), prv(1), pl.BlockSpec((32, C), lambda i: (0, 0)), vec, vec],
                   out_specs=[pl.BlockSpec((tm, C), lambda i: (i, 0)), pl.BlockSpec((tm, C), lambda i: (i, 0))],
                   out_shape=[_sds((S, C), BF16), _sds((S, C), F32)],
                   scratch_shapes=[pltpu.VMEM((SUBLANES, tm + HALO_A, C), F32)],
                   compiler_params=_cp(("parallel",)))(z, z, z, z, cw, cb, cn)


def _conv_a_bwd(da, a1, z, cw, cn):
    S = z.shape[0]
    C = D_CONV
    tm = TOK_TILE
    hb = tm // HALO_A
    nt = S // tm
    W = CONV_A_WIDTH

    def body(da_ref, a1_ref, dan_ref, a1n_ref, u_ref, gt_ref, up_ref, gp_ref, cw_ref, cn_ref,
             duz_ref, dcw_ref, dcb_ref, dcn_ref, win, dwin):
        i = pl.program_id(0)

        @pl.when(i == 0)
        def _():
            dcw_ref[...] = jnp.zeros_like(dcw_ref)
            dcb_ref[...] = jnp.zeros_like(dcb_ref)
            dcn_ref[...] = jnp.zeros_like(dcn_ref)

        cnv = cn_ref[...]

        def da1_of(dav, a1v):
            a2 = a1v * _rms(a1v) * cnv
            da2 = dav * _silu_grad(a2)
            dx, xh = _rms_bwd(da2, a1v, cnv)
            return dx, da2 * xh

        da1, dcn_t = da1_of(da_ref[...], a1_ref[...])
        da1n, _ = da1_of(dan_ref[...], a1n_ref[...])
        dwin[0, pl.ds(0, tm), :] = da1
        dwin[0, pl.ds(tm, HALO_A), :] = jnp.where(i == nt - 1, 0.0, da1n)
        _fill_shifts(dwin, tm + HALO_A)
        dcb_ref[...] += jnp.sum(da1, axis=0, keepdims=True)
        dcn_ref[...] += jnp.sum(dcn_t, axis=0, keepdims=True)

        prev = up_ref[...].astype(F32) * jax.nn.sigmoid(gp_ref[...].astype(F32))
        win[0, pl.ds(0, HALO_A), :] = jnp.where(i == 0, 0.0, prev)
        win[0, pl.ds(HALO_A, tm), :] = u_ref[...].astype(F32) * jax.nn.sigmoid(gt_ref[...].astype(F32))
        _fill_shifts(win, tm + HALO_A)

        def rows_block(rb, carry):
            r0 = pl.multiple_of(rb * CONV_ROWS, CONV_ROWS)
            rows = pl.ds(r0, CONV_ROWS)
            da1_b = dwin[0, rows, :]
            da0 = jnp.zeros((CONV_ROWS, C), F32)
            for k in range(W):
                da0 = da0 + cw_ref[k:k + 1, :] * _tap(dwin, W - 1 - k, CONV_ROWS, r0)
                dcw_ref[k:k + 1, :] += jnp.sum(da1_b * _tap(win, HALO_A - (W - 1) + k, CONV_ROWS, r0), axis=0, keepdims=True)
            u = u_ref[rows, :].astype(F32)
            sg = jax.nn.sigmoid(gt_ref[rows, :].astype(F32))
            duz_ref[rows, 0:C] = (da0 * sg).astype(BF16)
            duz_ref[rows, C:2 * C] = (da0 * u * sg * (1.0 - sg)).astype(BF16)
            return carry

        lax.fori_loop(0, tm // CONV_ROWS, rows_block, 0)

    cur = lambda c: pl.BlockSpec((tm, C), lambda i, c=c: (i, c))
    prv = lambda c: pl.BlockSpec((HALO_A, C), lambda i, c=c: (jnp.maximum(i * hb - 1, 0), c))
    nxt = pl.BlockSpec((HALO_A, C), lambda i: (jnp.minimum((i + 1) * hb, S // HALO_A - 1), 0))
    vec = pl.BlockSpec((1, C), lambda i: (0, 0))
    return _pallas(body, name="conv_a_bwd", grid=(nt,),
                   in_specs=[cur(0), cur(0), nxt, nxt, cur(0), cur(1), prv(0), prv(1),
                             pl.BlockSpec((32, C), lambda i: (0, 0)), vec],
                   out_specs=[pl.BlockSpec((tm, 2 * C), lambda i: (i, 0)), pl.BlockSpec((32, C), lambda i: (0, 0)), vec, vec],
                   out_shape=[_sds((S, 2 * C), BF16), _sds((32, C), F32), _sds((1, C), F32), _sds((1, C), F32)],
                   scratch_shapes=[pltpu.VMEM((SUBLANES, tm + HALO_A, C), F32)] * 2,
                   compiler_params=_cp(("arbitrary",)))(da, a1, da, a1, z, z, z, z, cw, cn)


def _forget_scan(fl, bf):
    S, L = fl.shape
    B = SCAN_BLK

    def body(fl_ref, bf_ref, flb_ref, F_ref):
        tri = (lax.broadcasted_iota(jnp.int32, (B, B), 0) >= lax.broadcasted_iota(jnp.int32, (B, B), 1)).astype(F32)

        def step(c, carry):
            rows = pl.ds(pl.multiple_of(c * B, B), B)
            v = fl_ref[rows, :] + bf_ref[...]
            flb_ref[rows, :] = v
            lf = jnp.minimum(v, 0.0) - jnp.log1p(jnp.exp(-jnp.abs(v)))
            cs = jnp.dot(tri, lf, precision=lax.Precision.HIGHEST, preferred_element_type=F32) + carry
            F_ref[rows, :] = cs
            return cs[B - 1:B, :]

        lax.fori_loop(0, S // B, step, jnp.zeros((1, L), F32))

    return _pallas(body, name="forget_scan", out_shape=[_sds((S, L), F32), _sds((S, L), F32)],
                   compiler_params=_cp())(fl, bf)


def _forget_scan_bwd(dF, flb):
    S, L = dF.shape
    B = SCAN_BLK
    nb = S // B

    def body(dF_ref, flb_ref, dfl_ref, db_ref):
        tri = (lax.broadcasted_iota(jnp.int32, (B, B), 0) <= lax.broadcasted_iota(jnp.int32, (B, B), 1)).astype(F32)

        def step(t, carry):
            carry_cs, db = carry
            rows = pl.ds(pl.multiple_of((nb - 1 - t) * B, B), B)
            cs = jnp.dot(tri, dF_ref[rows, :], precision=lax.Precision.HIGHEST, preferred_element_type=F32) + carry_cs
            dfl = cs * jax.nn.sigmoid(-flb_ref[rows, :])
            dfl_ref[rows, :] = dfl
            return cs[0:1, :], db + jnp.sum(dfl, axis=0, keepdims=True)

        _, db = lax.fori_loop(0, nb, step, (jnp.zeros((1, L), F32), jnp.zeros((1, L), F32)))
        db_ref[...] = db

    return _pallas(body, name="forget_scan_bwd", out_shape=[_sds((S, L), F32), _sds((1, L), F32)],
                   compiler_params=_cp())(dF, flb)


NEG = -1e30


def _causal_mask(t):
    return lax.broadcasted_iota(jnp.int32, (t, t), 0) >= lax.broadcasted_iota(jnp.int32, (t, t), 1)


AUG = 128
C_F, C_ONE, C_LSE = 64, 67, 70


def _split3(f):
    a = f.astype(BF16).astype(F32)
    r = f - a
    b = r.astype(BF16).astype(F32)
    return a, b, r - b


def _put3(lane, base, parts, other):
    out = other
    for k, p in enumerate(parts):
        out = jnp.where(lane == base + k, p, out)
    return out


def _ones3(lane, base):
    return (lane >= base) & (lane < base + 3)


def _lane_ids(rows):
    return lax.broadcasted_iota(jnp.int32, (rows, AUG), 1)


def _pair_rms(x, lo):
    sq = x * x
    ms_a = jnp.sum(jnp.where(lo, sq, 0.0), axis=-1, keepdims=True) * (1.0 / HEAD_DIM)
    ms_b = jnp.sum(jnp.where(lo, 0.0, sq), axis=-1, keepdims=True) * (1.0 / HEAD_DIM)
    return jnp.where(lo, lax.rsqrt(ms_a + EPS), lax.rsqrt(ms_b + EPS))


def _qkv_prep(z, Fc, qw, kw):
    S = z.shape[0]
    tp = min(QKN_TILE, S)
    scale = 1.0 / math.sqrt(HEAD_DIM)

    def body(zq_ref, zk_ref, zv_ref, F_ref, qw_ref, kw_ref, q_ref, k_ref, v_ref):
        j = pl.program_id(0)
        lane = _lane_ids(tp)
        lo = lane < HEAD_DIM
        Fv = F_ref[...]
        xq = zq_ref[...].astype(F32)
        xk = zk_ref[...].astype(F32)
        qn = xq * _pair_rms(xq, lo) * qw_ref[...] * scale
        kn = xk * _pair_rms(xk, lo) * kw_ref[...]
        vv = zv_ref[...].astype(F32)
        for half in range(2):
            take = (lambda a: a) if half == 0 else (lambda a: pltpu.roll(a, HEAD_DIM, 1))
            fp = _split3(jnp.sum(jnp.where(lane == 2 * j + half, Fv, 0.0), axis=-1, keepdims=True))
            qx = _put3(lane, C_F, fp, jnp.where(_ones3(lane, C_ONE), 1.0, 0.0))
            kx = _put3(lane, C_ONE, [-p for p in fp], jnp.where(_ones3(lane, C_F) | _ones3(lane, C_LSE), 1.0, 0.0))
            vx = jnp.where(_ones3(lane, C_F), 1.0, 0.0)
            q_ref[half] = jnp.where(lo, take(qn), qx).astype(BF16)
            k_ref[half] = jnp.where(lo, take(kn), kx).astype(BF16)
            v_ref[half] = jnp.where(lo, take(vv), vx).astype(BF16)

    col = lambda c0: pl.BlockSpec((tp, AUG), lambda j, i, c0=c0: (i, c0 + j))
    vec = pl.BlockSpec((1, AUG), lambda j, i: (0, 0))
    out = pl.BlockSpec((2, tp, AUG), lambda j, i: (j, i, 0))
    return _pallas(body, name="qkv_prep", grid=(N_HEADS // 2, S // tp),
                   in_specs=[col(8), col(12), col(16), pl.BlockSpec((tp, AUG), lambda j, i: (i, 0)), vec, vec],
                   out_specs=[out, out, out], out_shape=[_sds((N_HEADS, S, AUG), BF16)] * 3,
                   compiler_params=_cp(("parallel", "parallel")))(z, z, z, Fc, qw, kw)


def _fox_fwd(q_aug, k_aug, v_aug):
    H, S, A = q_aug.shape
    t = ATT_TILE
    nq = S // t

    def body(q_ref, k_ref, v_ref, o_ref, q2_ref):
        i = pl.program_id(1)
        q = q_ref[0]

        def tile(j, carry, diag):
            m, acc = carry
            rows = pl.ds(pl.multiple_of(j * t, t), t)
            s = _dot_nt(q, k_ref[0, rows, :])
            if diag:
                s = jnp.where(_causal_mask(t), s, NEG)
            m_new = jnp.maximum(m, jnp.max(s, axis=-1, keepdims=True))
            p = jnp.exp(s - m_new)
            acc = jnp.exp(m - m_new) * acc + _dot(p.astype(BF16), v_ref[0, rows, :])
            return m_new, acc

        init = (jnp.full((t, 1), NEG, F32), jnp.zeros((t, A), F32))
        carry = lax.fori_loop(0, i, lambda j, c: tile(j, c, False), init)
        m, acc = tile(i, carry, True)
        lane = _lane_ids(t)
        l = jnp.sum(jnp.where(lane == C_F, acc, 0.0), axis=-1, keepdims=True)
        o_ref[0] = (acc / l).astype(BF16)
        lse = m + jnp.log(l)
        q2_ref[0] = (q.astype(F32) + _put3(lane, C_LSE, [-p for p in _split3(lse)], 0.0)).astype(BF16)

    qblk = pl.BlockSpec((1, t, A), lambda h, i: (h, i, 0))
    full = pl.BlockSpec((1, S, A), lambda h, i: (h, 0, 0))
    return _pallas(body, name="fox_fwd", grid=(H, nq), in_specs=[qblk, full, full], out_specs=[qblk, qblk],
                   out_shape=[_sds((H, S, A), BF16)] * 2, compiler_params=_cp(("parallel", "parallel")))(q_aug, k_aug, v_aug)


def _do_prep(dcat, o_aug):
    S = dcat.shape[0]
    tp = min(QKN_TILE, S)

    def body(d_ref, o_ref, out_ref):
        lane = _lane_ids(tp)
        lo = lane < HEAD_DIM
        x = d_ref[...]
        for half in range(2):
            d = jnp.where(lo, x if half == 0 else pltpu.roll(x, HEAD_DIM, 1), 0.0)
            delta = jnp.sum(d * o_ref[half].astype(F32), axis=-1, keepdims=True)
            out_ref[half] = jnp.where(lo, d, _put3(lane, C_F, [-p for p in _split3(delta)], 0.0)).astype(BF16)

    pair = pl.BlockSpec((2, tp, AUG), lambda j, i: (j, i, 0))
    return _pallas(body, name="do_prep", grid=(N_HEADS // 2, S // tp),
                   in_specs=[pl.BlockSpec((tp, AUG), lambda j, i: (i, D_CONV // AUG + j)), pair], out_specs=pair,
                   out_shape=_sds((N_HEADS, S, AUG), BF16), compiler_params=_cp(("parallel", "parallel")))(dcat, o_aug)


def _fox_bwd(q2, k_aug, v_aug, do_aug):
    H, S, A = q2.shape
    t = ATT_TILE
    nq = S // t

    def body(q_ref, k_ref, v_ref, do_ref, dq_ref, dk_ref, dv_ref):
        j = pl.program_id(1)

        @pl.when(j == 0)
        def _():
            dq_ref[...] = jnp.zeros_like(dq_ref)

        k = k_ref[0]
        vv = v_ref[0]

        def tile(i, carry, diag):
            dk, dv = carry
            rows = pl.ds(pl.multiple_of(i * t, t), t)
            q = q_ref[0, rows, :]
            dov = do_ref[0, rows, :]
            s = _dot_nt(q, k)
            if diag:
                s = jnp.where(_causal_mask(t), s, NEG)
            p = jnp.exp(s)
            dv = dv + _dot_tn(p.astype(BF16), dov)
            dsb = (p * _dot_nt(dov, vv)).astype(BF16)
            dq_ref[0, rows, :] += _dot(dsb, k)
            dk = dk + _dot_tn(dsb, q)
            return dk, dv

        init = (jnp.zeros((t, A), F32), jnp.zeros((t, A), F32))
        carry = tile(j, init, True)
        dk, dv = lax.fori_loop(j + 1, nq, lambda i, c: tile(i, c, False), carry)
        dk_ref[0] = dk
        dv_ref[0] = dv

    full = pl.BlockSpec((1, S, A), lambda h, j: (h, 0, 0))
    kblk = pl.BlockSpec((1, t, A), lambda h, j: (h, j, 0))
    return _pallas(body, name="fox_bwd", grid=(H, nq), in_specs=[full, kblk, kblk, full], out_specs=[full, kblk, kblk],
                   out_shape=[_sds((H, S, A), F32)] * 3,
                   compiler_params=_cp(("parallel", "arbitrary")))(q2, k_aug, v_aug, do_aug)


def _qkv_bwd(dq, dk, dv, z, qw, kw):
    S = z.shape[0]
    tp = min(QKN_TILE, S)
    scale = 1.0 / math.sqrt(HEAD_DIM)

    def body(dq_ref, dk_ref, dv_ref, zq_ref, zk_ref, qw_ref, kw_ref, dqf_ref, dkf_ref, dvf_ref, dF_ref, dqw_ref, dkw_ref):
        i, j = pl.program_id(0), pl.program_id(1)
        lane = _lane_ids(tp)
        lo = lane < HEAD_DIM

        @pl.when((i == 0) & (j == 0))
        def _():
            dqw_ref[...] = jnp.zeros_like(dqw_ref)
            dkw_ref[...] = jnp.zeros_like(dkw_ref)

        def pair(ref):
            return jnp.where(lo, ref[0], pltpu.roll(ref[1], HEAD_DIM, 1))

        def norm_bwd(g, x, w):
            r = _pair_rms(x, lo)
            xh = x * r
            dxh = g * w
            tt = dxh * xh
            mean_a = jnp.sum(jnp.where(lo, tt, 0.0), axis=-1, keepdims=True) * (1.0 / HEAD_DIM)
            mean_b = jnp.sum(jnp.where(lo, 0.0, tt), axis=-1, keepdims=True) * (1.0 / HEAD_DIM)
            return r * (dxh - xh * jnp.where(lo, mean_a, mean_b)), g * xh

        dxq, gq = norm_bwd(pair(dq_ref) * scale, zq_ref[...].astype(F32), qw_ref[...])
        dqf_ref[...] = dxq.astype(BF16)
        dqw_ref[...] += jnp.sum(gq, axis=0, keepdims=True)
        dxk, gk = norm_bwd(pair(dk_ref), zk_ref[...].astype(F32), kw_ref[...])
        dkf_ref[...] = dxk.astype(BF16)
        dkw_ref[...] += jnp.sum(gk, axis=0, keepdims=True)
        dvf_ref[...] = pair(dv_ref).astype(BF16)

        contrib = jnp.zeros((tp, AUG), F32)
        for half in range(2):
            df = (jnp.sum(jnp.where(lane == C_F, dq_ref[half], 0.0), axis=-1, keepdims=True)
                  - jnp.sum(jnp.where(lane == C_ONE, dk_ref[half], 0.0), axis=-1, keepdims=True))
            contrib = jnp.where(lane == 2 * j + half, df, contrib)

        @pl.when(j == 0)
        def _():
            dF_ref[...] = contrib

        @pl.when(j > 0)
        def _():
            dF_ref[...] += contrib

    pairb = pl.BlockSpec((2, tp, AUG), lambda i, j: (j, i, 0))
    col = lambda c0: pl.BlockSpec((tp, AUG), lambda i, j, c0=c0: (i, c0 + j))
    vec = pl.BlockSpec((1, AUG), lambda i, j: (0, 0))
    flat = pl.BlockSpec((tp, AUG), lambda i, j: (i, j))
    return _pallas(body, name="qkv_bwd", grid=(S // tp, N_HEADS // 2),
                   in_specs=[pairb, pairb, pairb, col(8), col(12), vec, vec],
                   out_specs=[flat, flat, flat, pl.BlockSpec((tp, AUG), lambda i, j: (i, 0)), vec, vec],
                   out_shape=[_sds((S, D_ATTN), BF16)] * 3 + [_sds((S, AUG), F32), _sds((1, AUG), F32), _sds((1, AUG), F32)],
                   compiler_params=_cp(("arbitrary", "arbitrary")))(dq, dk, dv, z, z, qw, kw)


def _proj_res_heads(a, wa, o_aug, wo, res):
    S, D = res.shape
    H = o_aug.shape[0]
    tm = TOK_TILE

    def body(a_ref, wa_ref, o_ref, wo_ref, res_ref, out_ref):
        acc = res_ref[...] + _dot(a_ref[...], wa_ref[...])
        for h in range(H):
            acc = acc + _dot(o_ref[h], wo_ref[h])
        out_ref[...] = acc

    row = pl.BlockSpec((tm, D), lambda i: (i, 0))
    return _pallas(body, name="proj_res_heads", grid=(S // tm,),
                   in_specs=[pl.BlockSpec((tm, a.shape[1]), lambda i: (i, 0)), pl.BlockSpec(wa.shape, lambda i: (0, 0)),
                             pl.BlockSpec((H, tm, AUG), lambda i: (0, i, 0)), pl.BlockSpec(wo.shape, lambda i: (0, 0, 0)), row],
                   out_specs=row, out_shape=_sds((S, D), F32), compiler_params=_cp(("parallel",)))(a, wa, o_aug, wo, res)


def _heads_tn(o_aug, d):
    H, S, A = o_aug.shape
    D = d.shape[1]
    tm = min(DW_TILE, S)
    nt = S // tm

    def body(o_ref, d_ref, out_ref, acc_ref):
        i = pl.program_id(0)

        @pl.when(i == 0)
        def _():
            acc_ref[...] = jnp.zeros_like(acc_ref)

        dv = d_ref[...].astype(BF16)
        for h in range(H):
            acc_ref[h] += _dot_tn(o_ref[h], dv)

        @pl.when(i == nt - 1)
        def _():
            out_ref[...] = acc_ref[...].astype(BF16)

    return _pallas(body, name="heads_tn", grid=(nt,),
                   in_specs=[pl.BlockSpec((H, tm, A), lambda i: (0, i, 0)), pl.BlockSpec((tm, D), lambda i: (i, 0))],
                   out_specs=pl.BlockSpec((H, A, D), lambda i: (0, 0, 0)), out_shape=_sds((H, A, D), BF16),
                   scratch_shapes=[pltpu.VMEM((H, A, D), F32)], compiler_params=_cp(("arbitrary",)))(o_aug, d)


def _odd_mid_fwd(z, cw):
    S = z.shape[0]
    D = z.shape[1] // 3
    tm = TOK_TILE
    hb = tm // HALO_C
    W = CONV_C_WIDTH

    def body(gb_ref, gc_ref, hh_ref, gcp_ref, hhp_ref, cw_ref, y_ref, win):
        i = pl.program_id(0)
        prev = gcp_ref[...].astype(F32) * hhp_ref[...].astype(F32)
        win[pl.ds(0, HALO_C), :] = jnp.where(i == 0, 0.0, prev)
        win[pl.ds(HALO_C, tm), :] = gc_ref[...].astype(F32) * hh_ref[...].astype(F32)
        c1 = jnp.zeros((tm, D), F32)
        for k in range(W):
            c1 = c1 + cw_ref[k:k + 1, :] * win[pl.ds(HALO_C - (W - 1) + k, tm), :]
        y_ref[...] = (gb_ref[...].astype(F32) * c1).astype(BF16)

    cur = lambda c: pl.BlockSpec((tm, D), lambda i, c=c: (i, c))
    prv = lambda c: pl.BlockSpec((HALO_C, D), lambda i, c=c: (jnp.maximum(i * hb - 1, 0), c))
    return _pallas(body, name="odd_mid_fwd", grid=(S // tm,),
                   in_specs=[cur(0), cur(1), cur(2), prv(1), prv(2), pl.BlockSpec((8, D), lambda i: (0, 0))],
                   out_specs=pl.BlockSpec((tm, D), lambda i: (i, 0)), out_shape=_sds((S, D), BF16),
                   scratch_shapes=[pltpu.VMEM((tm + HALO_C, D), F32)],
                   compiler_params=_cp(("parallel",)))(z, z, z, z, z, cw)


def _odd_mid_bwd(dy, z, cw):
    S = z.shape[0]
    D = z.shape[1] // 3
    tm = TOK_TILE
    hb = tm // HALO_C
    nt = S // tm
    W = CONV_C_WIDTH

    def body(dy_ref, dyn_ref, gb_ref, gbn_ref, gc_ref, hh_ref, gcp_ref, hhp_ref, cw_ref, dz_ref, dcw_ref, win, dwin):
        i = pl.program_id(0)

        @pl.when(i == 0)
        def _():
            dcw_ref[...] = jnp.zeros_like(dcw_ref)

        gc = gc_ref[...].astype(F32)
        hh = hh_ref[...].astype(F32)
        prev = gcp_ref[...].astype(F32) * hhp_ref[...].astype(F32)
        win[pl.ds(0, HALO_C), :] = jnp.where(i == 0, 0.0, prev)
        win[pl.ds(HALO_C, tm), :] = gc * hh
        dyv = dy_ref[...]
        dc1 = dyv * gb_ref[...].astype(F32)
        dwin[pl.ds(0, tm), :] = dc1
        dwin[pl.ds(tm, HALO_C), :] = jnp.where(i == nt - 1, 0.0, dyn_ref[...] * gbn_ref[...].astype(F32))
        c1 = jnp.zeros((tm, D), F32)
        dc0 = jnp.zeros((tm, D), F32)
        for k in range(W):
            tap = win[pl.ds(HALO_C - (W - 1) + k, tm), :]
            c1 = c1 + cw_ref[k:k + 1, :] * tap
            dc0 = dc0 + cw_ref[k:k + 1, :] * dwin[pl.ds(W - 1 - k, tm), :]
            dcw_ref[k:k + 1, :] += jnp.sum(dc1 * tap, axis=0, keepdims=True)
        dz_ref[:, 0:D] = (dyv * c1).astype(BF16)
        dz_ref[:, D:2 * D] = (dc0 * hh).astype(BF16)
        dz_ref[:, 2 * D:3 * D] = (dc0 * gc).astype(BF16)

    cur = lambda c: pl.BlockSpec((tm, D), lambda i, c=c: (i, c))
    prv = lambda c: pl.BlockSpec((HALO_C, D), lambda i, c=c: (jnp.maximum(i * hb - 1, 0), c))
    nxt = pl.BlockSpec((HALO_C, D), lambda i: (jnp.minimum((i + 1) * hb, S // HALO_C - 1), 0))
    return _pallas(body, name="odd_mid_bwd", grid=(nt,),
                   in_specs=[cur(0), nxt, cur(0), nxt, cur(1), cur(2), prv(1), prv(2), pl.BlockSpec((8, D), lambda i: (0, 0))],
                   out_specs=[pl.BlockSpec((tm, 3 * D), lambda i: (i, 0)), pl.BlockSpec((8, D), lambda i: (0, 0))],
                   out_shape=[_sds((S, 3 * D), BF16), _sds((8, D), F32)],
                   scratch_shapes=[pltpu.VMEM((tm + HALO_C, D), F32), pltpu.VMEM((tm + HALO_C, D), F32)],
                   compiler_params=_cp(("arbitrary",)))(dy, dy, z, z, z, z, z, z, cw)


def _loss_head(y, tgt):
    S, D = y.shape
    tm = TOK_TILE

    def body(y_ref, t_ref, dy_ref, l_ref):
        @pl.when(pl.program_id(0) == 0)
        def _():
            l_ref[...] = jnp.zeros_like(l_ref)

        e = y_ref[...] - t_ref[...]
        dy_ref[...] = e * (1.0 / D)
        l_ref[...] += jnp.sum(jnp.sum(e * e, axis=-1, keepdims=True), axis=0, keepdims=True) * (0.5 / D)

    row = pl.BlockSpec((tm, D), lambda i: (i, 0))
    return _pallas(body, name="loss_head", grid=(S // tm,), in_specs=[row, row],
                   out_specs=[row, pl.BlockSpec((1, 1), lambda i: (0, 0))],
                   out_shape=[_sds((S, D), F32), _sds((1, 1), F32)],
                   compiler_params=_cp(("arbitrary",)))(y, tgt)


def _pad_rows(a, rows):
    return jnp.pad(a, ((0, rows - a.shape[0]), (0, 0)))


def _local_step(x, tgt, W, need=lambda block, after: None, done=lambda block, block_grads: None):
    S, D = x.shape
    grads = {}
    saved = {}

    def gain_after(gain, token):
        return gain if token is None else gain + token

    def ffn_f(tag, l, xin):
        need((tag, l), xin)
        out, xn, G, U = _ffn_fwd(xin, W[tag + "_norm"][l:l + 1], W[tag + "_w_gate"][l], W[tag + "_w_up"][l],
                                 W[tag + "_w_down"][l])
        saved[(tag, l)] = (xin, xn, G, U)
        return out

    def ffn_b(tag, l, dout):
        xin, xn, G, U = saved[(tag, l)]
        keys = [(tag + "_w_gate", l), (tag + "_w_up", l), (tag + "_w_down", l)]
        *dws, dG, dU = _ffn_bwd_w(dout, xn, G, U, W[tag + "_w_down"][l])
        big = dict(zip(keys, dws))
        grads.update(big)
        token = done((tag, l), big)
        dx, dg = _norm_in_bwd([dG, dU], [W[tag + "_w_gate"][l], W[tag + "_w_up"][l]], xin,
                              gain_after(W[tag + "_norm"][l:l + 1], token), dout, w_rows=True)
        grads[(tag + "_norm", l)] = dg
        return dx

    x0a = ffn_f("ffn1", 0, x)
    need(("ev", 0), x0a)
    w_in = W["ev_w_in"]
    w_main, w_f = w_in[:, :2560], jnp.pad(w_in[:, 2560:], ((0, 0), (0, 120)))
    h0, z0, fl = _norm_proj(x0a, W["mix_norm"][0:1], w_main, w_f)
    cw_a = _pad_rows(W["ev_conv_w"], 32)
    a_act, a1 = _conv_a_fwd(z0, cw_a, W["ev_conv_b"], W["ev_conv_norm"])
    flb, Fc = _forget_scan(fl, jnp.pad(W["ev_b_f"], ((0, 0), (0, 120))))
    qw2, kw2 = jnp.tile(W["ev_q_norm"], (1, 2)), jnp.tile(W["ev_k_norm"], (1, 2))
    q_aug, k_aug, v_aug = _qkv_prep(z0, Fc, qw2, kw2)
    o_aug, q_lse = _fox_fwd(q_aug, k_aug, v_aug)
    w_out_e = W["ev_w_out"]
    w_out_o = jnp.pad(w_out_e[D_CONV:].reshape(N_HEADS, HEAD_DIM, D), ((0, 0), (0, AUG - HEAD_DIM), (0, 0)))
    x0b = _proj_res_heads(a_act, w_out_e[:D_CONV], o_aug, w_out_o, x0a)
    x0c = ffn_f("ffn2", 0, x0b)
    x1a = ffn_f("ffn1", 1, x0c)
    need(("od", 0), x1a)
    h1, z1 = _norm_proj(x1a, W["mix_norm"][1:2], W["od_w_in"])
    cw_c = _pad_rows(W["od_conv_w"], 8)
    y1 = _odd_mid_fwd(z1, cw_c)
    x1b = _proj_res([y1], [W["od_w_out"]], x1a)
    x1c = ffn_f("ffn2", 1, x1b)
    dy, loss = _loss_head(x1c, tgt)

    d = ffn_b("ffn2", 1, dy)
    dy1 = _matmul_nt(d, W["od_w_out"])
    grads[("od_w_out", 0)] = _matmul_tn(y1, d, D)[0]
    dz1, dcw_c = _odd_mid_bwd(dy1, z1, cw_c)
    grads[("od_conv_w", 0)] = dcw_c[:CONV_C_WIDTH]
    grads[("od_w_in", 0)] = _matmul_tn(h1, dz1, 3 * D // 4)
    token = done(("od", 0), {k: grads[k] for k in (("od_w_out", 0), ("od_w_in", 0))})
    d, dg = _norm_in_bwd([dz1[None]], [W["od_w_in"][None]], x1a, gain_after(W["mix_norm"][1:2], token), d)
    grads[("mix_norm", 1)] = dg
    d = ffn_b("ffn1", 1, d)
    d = ffn_b("ffn2", 0, d)
    dcat = _matmul_nt(d, w_out_e)
    grads[("ev_w_out", 0)] = jnp.concatenate([_matmul_tn(a_act, d, D)[0],
                                              _heads_tn(o_aug, d)[:, :HEAD_DIM].reshape(D_ATTN, D)], axis=0)
    duz, dcw_a, dcb, dcn = _conv_a_bwd(dcat, a1, z0, cw_a, W["ev_conv_norm"])
    grads[("ev_conv_w", 0)] = dcw_a[:CONV_A_WIDTH]
    grads[("ev_conv_b", 0)] = dcb
    grads[("ev_conv_norm", 0)] = dcn
    dq_a, dk_a, dv_a = _fox_bwd(q_lse, k_aug, v_aug, _do_prep(dcat, o_aug))
    dqf, dkf, dvf, dF, dqw, dkw = _qkv_bwd(dq_a, dk_a, dv_a, z0, qw2, kw2)
    grads[("ev_q_norm", 0)] = dqw[:, :HEAD_DIM] + dqw[:, HEAD_DIM:]
    grads[("ev_k_norm", 0)] = dkw[:, :HEAD_DIM] + dkw[:, HEAD_DIM:]
    dfl, dbf = _forget_scan_bwd(dF, flb)
    grads[("ev_b_f", 0)] = dbf[:, :N_HEADS]
    dz0 = jnp.concatenate([duz, dqf, dkf, dvf], axis=1)
    dflb = dfl.astype(BF16)
    gmain = _matmul_tn(h0, dz0, 640)
    gmain = gmain.transpose(1, 0, 2).reshape(D, 2560)
    gf = _matmul_tn(h0, dflb, 128)[0][:, :N_HEADS]
    grads[("ev_w_in", 0)] = jnp.concatenate([gmain, gf], axis=1)
    token = done(("ev", 0), {k: grads[k] for k in (("ev_w_out", 0), ("ev_w_in", 0))})
    d, dg = _norm_in_bwd([dz0[None], dflb[None]], [w_main[None], w_f[None]], x0a, gain_after(W["mix_norm"][0:1], token), d)
    grads[("mix_norm", 0)] = dg
    d = ffn_b("ffn1", 0, d)
    return loss, d, grads


def _place():
    x, y, c = lax.axis_index("x"), lax.axis_index("y"), lax.axis_index("c")
    chips = [(1 - x, y), (x, 1 - y), (1 - x, 1 - y)]
    return x, y, c, chips


def _remote(src, dst, send_sem, recv_sem, to):
    return pltpu.make_async_remote_copy(src_ref=src, dst_ref=dst, send_sem=send_sem, recv_sem=recv_sem,
                                        device_id=to, device_id_type=MESH)


HBM = pl.BlockSpec(memory_space=pltpu.HBM)
SEM = pl.BlockSpec(memory_space=pltpu.SEMAPHORE)
EFFECT = pltpu.SideEffectType.DATAFLOW_SIDE_EFFECTING


def _in_hbm(a):
    return pltpu.with_memory_space_constraint(a, pltpu.HBM)


def _ag_start(tag, bufs, with_taps):
    n = len(bufs)
    order = ([n - 1] + list(range(n - 1))) if with_taps else list(range(n))

    def body(*refs):
        send_sems, recv_sems = refs[n], refs[n + 1]
        outs, token = refs[n + 2:2 * n + 2], refs[2 * n + 2]
        x, y, c, chips = _place()
        me = 2 * x + y
        for a in order:
            if with_taps and a == n - 1:
                blk = outs[a].at[me]
            else:
                h = outs[a].shape[1] // 2
                blk = outs[a].at[me, pl.ds(c * h, h)]
            for jj, (px, py) in enumerate(chips):
                _remote(blk, blk, send_sems.at[3 * a + jj], recv_sems.at[3 * a + jj], (px, py, c)).start()
        token[...] = jnp.zeros_like(token)

    return _pallas(
        body, name=f"gather_start_{tag}",
        out_shape=[pltpu.SemaphoreType.DMA((3 * n,)), pltpu.SemaphoreType.DMA((3 * n,))]
        + [pltpu.HBM(b.shape, b.dtype) for b in bufs] + [_sds((8, 128), F32)],
        in_specs=[HBM] * n, out_specs=[SEM, SEM] + [HBM] * n + [pl.BlockSpec(memory_space=pltpu.VMEM)],
        input_output_aliases={a: 2 + a for a in range(n)},
        compiler_params=pltpu.CompilerParams(has_side_effects=EFFECT),
    )(*[_in_hbm(b) for b in bufs])


def _ag_mid(g, ici_send, ici_recv, bufs, idx, taps, n_big, after):
    n = len(bufs)
    arrs = list(bufs) + ([taps] if taps is not None else [])
    m = len(arrs)

    def body(*refs):
        ici_s, ici_r = refs[0], refs[1]
        d_send, d_recv = refs[m + 3], refs[m + 4]
        outs = refs[m + 5:]
        x, y, c, chips = _place()
        me = 2 * x + y
        for i in range(m):
            a = idx[i] if i < n else n_big
            for jj, (px, py) in enumerate(chips):
                k = 3 * a + jj
                if i < n:
                    h = outs[i].shape[1] // 2
                    mine, blk = outs[i].at[me, pl.ds(c * h, h)], outs[i].at[2 * px + py, pl.ds(c * h, h)]
                else:
                    mine, blk = outs[i].at[me], outs[i].at[2 * px + py]
                _remote(mine, mine, ici_s.at[k], ici_r.at[k], (px, py, c)).wait_send()
                _remote(blk, blk, ici_s.at[k], ici_r.at[k], (px, py, c)).wait_recv()
                if i < n:
                    _remote(blk, blk, d_send.at[3 * i + jj], d_recv.at[3 * i + jj], (x, y, 1 - c)).start()

    return _pallas(
        body, name=f"gather_pass_on_{g}",
        out_shape=[pltpu.SemaphoreType.DMA((3 * n,)), pltpu.SemaphoreType.DMA((3 * n,))] + [pltpu.HBM(b.shape, b.dtype) for b in arrs],
        in_specs=[SEM, SEM] + [HBM] * m + [ANY], out_specs=[SEM, SEM] + [HBM] * m,
        input_output_aliases={2 + i: 2 + i for i in range(m)},
        compiler_params=pltpu.CompilerParams(has_side_effects=EFFECT),
    )(ici_send, ici_recv, *arrs, after)


def _ag_wait(g, d_send, d_recv, arrs, n, after):
    m = len(arrs)

    def body(*refs):
        d_s, d_r = refs[0], refs[1]
        outs = refs[m + 3:]
        x, y, c, chips = _place()
        for i in range(n):
            h = outs[i].shape[1] // 2
            for jj, (px, py) in enumerate(chips):
                sent = outs[i].at[2 * px + py, pl.ds(c * h, h)]
                got = outs[i].at[2 * px + py, pl.ds((1 - c) * h, h)]
                _remote(sent, sent, d_s.at[3 * i + jj], d_r.at[3 * i + jj], (x, y, 1 - c)).wait_send()
                _remote(got, got, d_s.at[3 * i + jj], d_r.at[3 * i + jj], (x, y, 1 - c)).wait_recv()

    return _pallas(
        body, name=f"gather_wait_{g}", out_shape=[pltpu.HBM(b.shape, b.dtype) for b in arrs],
        in_specs=[SEM, SEM] + [HBM] * m + [ANY], out_specs=[HBM] * m,
        input_output_aliases={2 + i: i for i in range(m)},
        compiler_params=pltpu.CompilerParams(has_side_effects=EFFECT),
    )(d_send, d_recv, *arrs, after)


def _pair_start(g, gs, after):
    n = len(gs)
    zones = [lax.empty((4, a.shape[1] // 2, a.shape[2]), a.dtype) for a in gs]
    extra = [] if after is None else [after]

    def body(*refs):
        k0 = 2 * n + len(extra)
        send_sems, recv_sems = refs[k0], refs[k0 + 1]
        src, dst = refs[k0 + 2:k0 + 2 + n], refs[k0 + 2 + n:k0 + 2 + 2 * n]
        token = refs[k0 + 2 + 2 * n]
        x, y, c, _ = _place()
        for a in range(n):
            h = src[a].shape[1] // 2
            _remote(src[a].at[:, pl.ds((1 - c) * h, h)], dst[a], send_sems.at[a], recv_sems.at[a], (x, y, 1 - c)).start()
        token[...] = jnp.zeros_like(token)

    return _pallas(
        body, name=f"grad_pair_start_{g}",
        out_shape=[pltpu.SemaphoreType.DMA((n,)), pltpu.SemaphoreType.DMA((n,))]
        + [pltpu.HBM(a.shape, a.dtype) for a in gs + zones] + [_sds((8, 128), F32)],
        in_specs=[HBM] * (2 * n) + [ANY] * len(extra),
        out_specs=[SEM, SEM] + [HBM] * (2 * n) + [pl.BlockSpec(memory_space=pltpu.VMEM)],
        input_output_aliases={i: 2 + i for i in range(2 * n)},
        compiler_params=pltpu.CompilerParams(has_side_effects=EFFECT),
    )(*[_in_hbm(a) for a in gs + zones], *extra)


def _pair_wait(g, send, recv, gs, zones):
    n = len(gs)

    def body(*refs):
        s_ref, r_ref = refs[0], refs[1]
        outs = refs[2 + 2 * n:]
        src, dst = outs[:n], outs[n:]
        x, y, c, _ = _place()
        for a in range(n):
            h = src[a].shape[1] // 2
            _remote(src[a].at[:, pl.ds((1 - c) * h, h)], dst[a], s_ref.at[a], r_ref.at[a], (x, y, 1 - c)).wait()

    return _pallas(
        body, name=f"grad_pair_wait_{g}", out_shape=[pltpu.HBM(a.shape, a.dtype) for a in gs + zones],
        in_specs=[SEM, SEM] + [HBM] * (2 * n), out_specs=[HBM] * (2 * n),
        input_output_aliases={2 + i: i for i in range(2 * n)},
        compiler_params=pltpu.CompilerParams(has_side_effects=EFFECT),
    )(send, recv, *gs, *zones)


def _pair_add(gs, others, c_arr):
    n = len(gs)

    def body(c_ref, *refs):
        for g_ref, o_ref, out_ref in zip(refs[:n], refs[n:2 * n], refs[2 * n:]):
            out_ref[...] = (g_ref[...].astype(F32) + o_ref[...].astype(F32)).astype(BF16)

    half = lambda a: pl.BlockSpec((1, a.shape[1] // 2, a.shape[2]), lambda k, c_ref: (k, c_ref[0], 0))
    whole = lambda a: pl.BlockSpec((1,) + a.shape[1:], lambda k, c_ref: (k, 0, 0))
    grid_spec = pltpu.PrefetchScalarGridSpec(
        num_scalar_prefetch=1, grid=(4,), in_specs=[half(a) for a in gs] + [whole(o) for o in others],
        out_specs=[whole(o) for o in others])
    return _pallas(body, name="grad_pair_add", grid_spec=grid_spec, out_shape=[_sds(o.shape, BF16) for o in others],
                   compiler_params=_cp(("parallel",)))(c_arr, *gs, *others)


def _chip_start(g, ss):
    n = len(ss)
    zones = [lax.empty((3,) + s.shape[1:], s.dtype) for s in ss]

    def body(*refs):
        send_sems, recv_sems = refs[2 * n], refs[2 * n + 1]
        src, dst = refs[2 * n + 2:3 * n + 2], refs[3 * n + 2:4 * n + 2]
        token = refs[4 * n + 2]
        x, y, c, chips = _place()
        for a in range(n):
            for jj, (px, py) in enumerate(chips):
                k = 3 * a + jj
                _remote(src[a].at[2 * px + py], dst[a].at[jj], send_sems.at[k], recv_sems.at[k], (px, py, c)).start()
        token[...] = jnp.zeros_like(token)

    return _pallas(
        body, name=f"grad_chip_start_{g}",
        out_shape=[pltpu.SemaphoreType.DMA((3 * n,)), pltpu.SemaphoreType.DMA((3 * n,))]
        + [pltpu.HBM(a.shape, a.dtype) for a in ss + zones] + [_sds((8, 128), F32)],
        in_specs=[HBM] * (2 * n), out_specs=[SEM, SEM] + [HBM] * (2 * n) + [pl.BlockSpec(memory_space=pltpu.VMEM)],
        input_output_aliases={i: 2 + i for i in range(2 * n)},
        compiler_params=pltpu.CompilerParams(has_side_effects=EFFECT),
    )(*[_in_hbm(a) for a in ss + zones])


def _chip_wait(tag, sends, recvs, counts, ss, zones, after):
    nb, n = len(sends), len(ss)

    def body(*refs):
        s_refs, r_refs = refs[:nb], refs[nb:2 * nb]
        outs = refs[2 * nb + 2 * n + 1:]
        src, dst = outs[:n], outs[n:]
        x, y, c, chips = _place()
        a = 0
        for b in range(nb):
            for i in range(counts[b]):
                for jj, (px, py) in enumerate(chips):
                    k = 3 * i + jj
                    _remote(src[a].at[2 * px + py], dst[a].at[jj], s_refs[b].at[k], r_refs[b].at[k], (px, py, c)).wait()
                a += 1

    return _pallas(
        body, name=f"grad_chip_wait_{tag}", out_shape=[pltpu.HBM(a.shape, a.dtype) for a in ss + zones],
        in_specs=[SEM] * (2 * nb) + [HBM] * (2 * n) + [ANY], out_specs=[HBM] * (2 * n),
        input_output_aliases={2 * nb + i: i for i in range(2 * n)},
        compiler_params=pltpu.CompilerParams(has_side_effects=EFFECT),
    )(*sends, *recvs, *ss, *zones, after)


def _chip_sum(s, r, where, dest, l, L):
    _, h, C = s.shape
    tr = h // 2

    def body(k_ref, s_ref, r_ref, *rest):
        out_ref = rest[-1]
        acc = s_ref[0].astype(F32)
        for jj in range(3):
            acc = acc + r_ref[jj].astype(F32)
        out_ref[...] = acc

    in_specs = [pl.BlockSpec((1, tr, C), lambda i, k_ref: (k_ref[0], i, 0)), pl.BlockSpec((3, tr, C), lambda i, k_ref: (0, i, 0))]
    args = [where, s, r]
    alias = {}
    if dest is not None:
        in_specs.append(ANY)
        args.append(dest)
        alias = {3: 0}
    grid_spec = pltpu.PrefetchScalarGridSpec(
        num_scalar_prefetch=1, grid=(2,), in_specs=in_specs,
        out_specs=pl.BlockSpec((None, tr, C), lambda i, k_ref: (l, 2 * k_ref[1] + i, 0)))
    return _pallas(body, name="grad_chip_sum", grid_spec=grid_spec, out_shape=_sds((L, 2 * h, C), F32),
                   input_output_aliases=alias, compiler_params=_cp(("arbitrary",)))(*args)


def _share_start(tag, bufs, layout):
    n, n_buf = len(layout), len(bufs)

    def body(*refs):
        send_sems, recv_sems = refs[n_buf], refs[n_buf + 1]
        outs = refs[n_buf + 2:]
        x, y, c, _ = _place()
        for a, (o, l) in enumerate(layout):
            h = outs[o].shape[1] // 2
            blk = outs[o].at[l, pl.ds(c * h, h)]
            _remote(blk, blk, send_sems.at[a], recv_sems.at[a], (x, y, 1 - c)).start()

    return _pallas(
        body, name=f"grad_share_start_{tag}",
        out_shape=[pltpu.SemaphoreType.DMA((n,)), pltpu.SemaphoreType.DMA((n,))] + [pltpu.HBM(b.shape, b.dtype) for b in bufs],
        in_specs=[HBM] * n_buf, out_specs=[SEM, SEM] + [HBM] * n_buf, input_output_aliases={o: 2 + o for o in range(n_buf)},
        compiler_params=pltpu.CompilerParams(has_side_effects=EFFECT),
    )(*[_in_hbm(b) for b in bufs])


def _share_wait(tag, send, recv, bufs, layout, after):
    n_buf = len(bufs)

    def body(*refs):
        s_ref, r_ref = refs[0], refs[1]
        outs = refs[n_buf + 3:]
        x, y, c, _ = _place()
        for a, (o, l) in enumerate(layout):
            h = outs[o].shape[1] // 2
            mine, theirs = outs[o].at[l, pl.ds(c * h, h)], outs[o].at[l, pl.ds((1 - c) * h, h)]
            _remote(mine, mine, s_ref.at[a], r_ref.at[a], (x, y, 1 - c)).wait_send()
            _remote(theirs, theirs, s_ref.at[a], r_ref.at[a], (x, y, 1 - c)).wait_recv()

    return _pallas(
        body, name=f"grad_share_wait_{tag}", out_shape=[pltpu.HBM(b.shape, b.dtype) for b in bufs],
        in_specs=[SEM, SEM] + [HBM] * n_buf + [ANY], out_specs=[HBM] * n_buf,
        input_output_aliases={2 + o: o for o in range(n_buf)},
        compiler_params=pltpu.CompilerParams(has_side_effects=EFFECT),
    )(send, recv, *bufs, after)


def _small_all_reduce(packed, after):
    P, L = packed.shape

    def body(in_ref, after_ref, out_ref, slots, send_sems, recv_sems):
        x, y, c, _ = _place()
        me = 4 * x + 2 * y + c
        slots[me] = in_ref[...]
        cps = []
        for r in range(1, 8):
            px = 1 - x if r & 4 else x
            py = 1 - y if r & 2 else y
            pc = 1 - c if r & 1 else c
            cps.append(_remote(in_ref, slots.at[me], send_sems.at[r - 1], recv_sems.at[r - 1], (px, py, pc)))
        for cp in cps:
            cp.start()
        for r in range(1, 8):
            px = 1 - x if r & 4 else x
            py = 1 - y if r & 2 else y
            pc = 1 - c if r & 1 else c
            blk = slots.at[4 * px + 2 * py + pc]
            _remote(blk, blk, send_sems.at[r - 1], recv_sems.at[r - 1], (px, py, pc)).wait_recv()
        for cp in cps:
            cp.wait_send()
        acc = slots[0]
        for k in range(1, 8):
            acc = acc + slots[k]
        out_ref[...] = acc

    vm = pl.BlockSpec(memory_space=pltpu.VMEM)
    return _pallas(body, name="small_all_reduce", in_specs=[vm, ANY], out_specs=vm, out_shape=_sds((P, L), F32),
                   scratch_shapes=[pltpu.VMEM((8, P, L), F32), pltpu.SemaphoreType.DMA((7,)),
                                   pltpu.SemaphoreType.DMA((7,))])(packed, after)


def _adamw_math(w, g, m, v):
    m = ADAM_B1 * m + (1.0 - ADAM_B1) * g
    v = ADAM_B2 * v + (1.0 - ADAM_B2) * (g * g)
    m_hat = m / (1.0 - ADAM_B1 ** ADAM_STEP)
    v_hat = v / (1.0 - ADAM_B2 ** ADAM_STEP)
    delta = -ADAM_LR * (m_hat / (jnp.sqrt(v_hat) + ADAM_EPS) + ADAM_WD * w)
    return delta, m, v


def _adamw(w, g, m, v):
    shape = w.shape
    C = shape[-1]
    rows = math.prod(shape[:-1])
    tr = next(t for t in (512, 352, 256, 128, 64, 32, 16, 8, rows) if rows % t == 0)
    w2, g2, m2, v2 = (a.reshape(rows, C) for a in (w, g, m, v))

    def body(w_ref, g_ref, m_ref, v_ref, go_ref, d_ref, nm_ref, nv_ref):
        gv = g_ref[...]
        d, nm, nv = _adamw_math(w_ref[...], gv, m_ref[...], v_ref[...])
        go_ref[...] = gv
        d_ref[...] = d
        nm_ref[...] = nm
        nv_ref[...] = nv

    blk = pl.BlockSpec((tr, C), lambda i: (i, 0))
    outs = _pallas(body, name="adamw", grid=(rows // tr,), in_specs=[blk] * 4, out_specs=[blk] * 4,
                   out_shape=[_sds((rows, C), F32)] * 4, compiler_params=_cp(("parallel",)))(w2, g2, m2, v2)
    return tuple(o.reshape(shape) for o in outs)


WEIGHTS = ["ffn1_norm", "ffn1_w_gate", "ffn1_w_up", "ffn1_w_down", "mix_norm", "ffn2_norm", "ffn2_w_gate", "ffn2_w_up",
           "ffn2_w_down", "ev_w_in", "ev_b_f", "ev_conv_w", "ev_conv_b", "ev_conv_norm", "ev_q_norm", "ev_k_norm",
           "ev_w_out", "od_w_in", "od_conv_w", "od_w_out"]
BIG = ([("ffn1_w_gate", 0), ("ffn1_w_up", 0), ("ffn1_w_down", 0), ("ev_w_in", 0), ("ev_w_out", 0),
        ("ffn2_w_gate", 0), ("ffn2_w_up", 0), ("ffn2_w_down", 0)]
       + [("ffn1_w_gate", 1), ("ffn1_w_up", 1), ("ffn1_w_down", 1), ("od_w_in", 0), ("od_w_out", 0),
          ("ffn2_w_gate", 1), ("ffn2_w_up", 1), ("ffn2_w_down", 1)])
TRANSPOSED = ("ffn1_w_gate", "ffn1_w_up", "ffn2_w_gate", "ffn2_w_up")
SHARED_LAST = ("ffn1_w_gate", "ffn1_w_up", "ffn1_w_down", "ev_w_in", "ev_w_out")
BLOCKS = [("ffn1", 0), ("ev", 0), ("ffn2", 0), ("ffn1", 1), ("od", 0), ("ffn2", 1)]
BLOCK_OF = {(name, l): (name.split("_w_")[0], l) for name, l in BIG}
BIG_NAMES = ["ffn1_w_gate", "ffn1_w_up", "ffn1_w_down", "ffn2_w_gate", "ffn2_w_up", "ffn2_w_down",
             "ev_w_in", "ev_w_out", "od_w_in", "od_w_out"]
SMALL = [("ffn1_norm", 16), ("mix_norm", 16), ("ffn2_norm", 16), ("ev_b_f", 8), ("ev_conv_w", 128), ("ev_conv_b", 8),
         ("ev_conv_norm", 8), ("ev_q_norm", 8), ("ev_k_norm", 8), ("od_conv_w", 24)]


def _to_lanes(a, rows):
    flat = a.reshape(-1)
    return jnp.pad(flat, (0, rows * 128 - flat.shape[0])).reshape(rows, 128)


def kernel(x, ffn1_norm, ffn1_w_gate, ffn1_w_up, ffn1_w_down, mix_norm, ffn2_norm, ffn2_w_gate, ffn2_w_up, ffn2_w_down, ev_w_in, ev_b_f, ev_conv_w, ev_conv_b, ev_conv_norm, ev_q_norm, ev_k_norm, ev_w_out, od_w_in, od_conv_w, od_w_out, loss_target, m_ffn1_norm, m_ffn1_w_gate, m_ffn1_w_up, m_ffn1_w_down, m_mix_norm, m_ffn2_norm, m_ffn2_w_gate, m_ffn2_w_up, m_ffn2_w_down, m_ev_w_in, m_ev_b_f, m_ev_conv_w, m_ev_conv_b, m_ev_conv_norm, m_ev_q_norm, m_ev_k_norm, m_ev_w_out, m_od_w_in, m_od_conv_w, m_od_w_out, v_ffn1_norm, v_ffn1_w_gate, v_ffn1_w_up, v_ffn1_w_down, v_mix_norm, v_ffn2_norm, v_ffn2_w_gate, v_ffn2_w_up, v_ffn2_w_down, v_ev_w_in, v_ev_b_f, v_ev_conv_w, v_ev_conv_b, v_ev_conv_norm, v_ev_q_norm, v_ev_k_norm, v_ev_w_out, v_od_w_in, v_od_conv_w, v_od_w_out):
    P = dict(ffn1_norm=ffn1_norm, ffn1_w_gate=ffn1_w_gate, ffn1_w_up=ffn1_w_up, ffn1_w_down=ffn1_w_down, mix_norm=mix_norm,
             ffn2_norm=ffn2_norm, ffn2_w_gate=ffn2_w_gate, ffn2_w_up=ffn2_w_up, ffn2_w_down=ffn2_w_down, ev_w_in=ev_w_in,
             ev_b_f=ev_b_f, ev_conv_w=ev_conv_w, ev_conv_b=ev_conv_b, ev_conv_norm=ev_conv_norm, ev_q_norm=ev_q_norm,
             ev_k_norm=ev_k_norm, ev_w_out=ev_w_out, od_w_in=od_w_in, od_conv_w=od_conv_w, od_w_out=od_w_out)
    M = dict(zip(WEIGHTS, [m_ffn1_norm, m_ffn1_w_gate, m_ffn1_w_up, m_ffn1_w_down, m_mix_norm, m_ffn2_norm, m_ffn2_w_gate,
                           m_ffn2_w_up, m_ffn2_w_down, m_ev_w_in, m_ev_b_f, m_ev_conv_w, m_ev_conv_b, m_ev_conv_norm,
                           m_ev_q_norm, m_ev_k_norm, m_ev_w_out, m_od_w_in, m_od_conv_w, m_od_w_out]))
    V = dict(zip(WEIGHTS, [v_ffn1_norm, v_ffn1_w_gate, v_ffn1_w_up, v_ffn1_w_down, v_mix_norm, v_ffn2_norm, v_ffn2_w_gate,
                           v_ffn2_w_up, v_ffn2_w_down, v_ev_w_in, v_ev_b_f, v_ev_conv_w, v_ev_conv_b, v_ev_conv_norm,
                           v_ev_q_norm, v_ev_k_norm, v_ev_w_out, v_od_w_in, v_od_conv_w, v_od_w_out]))
    for name in TRANSPOSED:
        P[name], M[name], V[name] = (jnp.swapaxes(a, 1, 2) for a in (P[name], M[name], V[name]))
    S, D = x.shape[1], x.shape[2]
    chip = 2 * lax.axis_index("x") + lax.axis_index("y")
    core = lax.axis_index("c")

    def own_slot(shard):
        return lax.dynamic_update_slice(lax.empty((4,) + shard.shape, shard.dtype), shard[None], (chip, 0, 0))

    taps = jnp.concatenate([_to_lanes(_pad_rows(ev_conv_w[0], 32), 32), _to_lanes(_pad_rows(od_conv_w[0], 8), 16)], axis=0)
    first = [i for i, k in enumerate(BIG) if BLOCK_OF[k] in BLOCKS[:2]]
    rest = [i for i in range(len(BIG)) if i not in first]
    send0, recv0, *bufs0 = _ag_start("first", [own_slot(P[BIG[i][0]][BIG[i][1]].astype(BF16)) for i in first]
                                     + [own_slot(taps)], True)
    zero = bufs0.pop()[0, 0]
    send1, recv1, *bufs1 = _ag_start("rest", [own_slot((P[BIG[i][0]][BIG[i][1]] + zero).astype(BF16)) for i in rest], False)
    bufs1.pop()
    cols = lambda a: a.transpose(1, 0, 2).reshape(a.shape[1], 4 * a.shape[2])
    W = {k: P[k] for k in ("ffn1_norm", "mix_norm", "ffn2_norm", "ev_b_f", "ev_q_norm", "ev_k_norm")}
    W["ev_conv_b"], W["ev_conv_norm"] = ev_conv_b, ev_conv_norm
    for tag in ("ffn1", "ffn2"):
        for kind in ("_w_gate", "_w_up", "_w_down"):
            W[tag + kind] = [None, None]
    passing = {}

    def pass_on(g, after):
        idx = [i for i, k in enumerate(BIG) if BLOCK_OF[k] == BLOCKS[g]]
        keys = [BIG[i] for i in idx] + (["taps"] if BLOCKS[g] == ("ev", 0) else [])
        send, recv, bufs, members = (send0, recv0, bufs0, first) if g < 2 else (send1, recv1, bufs1, rest)
        local = [members.index(i) for i in idx]
        passing[g] = (keys, _ag_mid(g, send, recv, [bufs[i] for i in local], local,
                                    bufs0[-1] if BLOCKS[g] == ("ev", 0) else None, len(first), after))

    def need(block, after):
        g = BLOCKS.index(block)
        if g not in passing:
            pass_on(g, bufs1[0] if g == 0 else after)
        keys, (d_send, d_recv, *thru) = passing.pop(g)
        got = dict(zip(keys, _ag_wait(g, d_send, d_recv, thru, len(keys) - ("taps" in keys), after)))
        if 1 <= g < len(BLOCKS) - 1:
            pass_on(g + 1, after)
        for key, a in got.items():
            if key == "taps":
                continue
            name, l = key
            if name.startswith("ffn"):
                W[name][l] = a
            elif name.endswith("_w_in"):
                W[name] = cols(a)
            elif name.endswith("_w_out"):
                W[name] = a.reshape(4 * a.shape[1], D)
        if block == ("ev", 0):
            taps_all = got["taps"]
            W["ev_conv_w"] = cols(taps_all[:, :32].reshape(4, 32, 128))[:CONV_A_WIDTH]
            W["od_conv_w"] = cols(taps_all[:, 32:48].reshape(4, 8, 256))[:CONV_C_WIDTH]

    rows = lambda a: a.reshape(4, a.shape[0] // 4, a.shape[1])
    colsh = lambda a: a.reshape(a.shape[0], 4, a.shape[1] // 4).transpose(1, 0, 2)
    c_arr = core.reshape(1).astype(jnp.int32)
    where = jnp.stack([chip, core]).astype(jnp.int32)
    in_flight = []

    def done(block, block_grads):
        g = BLOCKS.index(block)
        keys = list(block_grads)
        gs = []
        for name, l in keys:
            a = block_grads[(name, l)]
            gs.append(colsh(a) if name == "ev_w_in" else rows(a) if name.endswith("_w_out") else a)
        for item in list(pairs):
            to_chips(item)
        send, recv, *rest = _pair_start(g, gs, chained.get("token"))
        n = len(keys)
        pairs.append((g, keys, send, recv, rest[:n], rest[n:2 * n]))
        if g == 0:
            to_chips(pairs[0])
        chained["token"] = rest[-1] if g else chained["token"]
        return chained["token"][0:1, 0:1]

    pairs, chained = [], {}

    def to_chips(item):
        pairs.remove(item)
        g, keys, send, recv, gs, zones = item
        n = len(keys)
        done_ = _pair_wait(g, send, recv, gs, zones)
        sums = list(_pair_add(list(done_[:n]), list(done_[n:]), c_arr))
        send2, recv2, *rest = _chip_start(g, sums)
        in_flight.append((keys, send2, recv2, rest[:n], rest[n:2 * n]))
        chained["token"] = rest[-1]

    loss, grad_x, grads = _local_step(x[0], loss_target[0], W, need, done)

    order = [k for keys, *_ in in_flight for k in keys]
    landed = _chip_wait("all", [f[1] for f in in_flight], [f[2] for f in in_flight], [len(f[0]) for f in in_flight],
                        [a for f in in_flight for a in f[3]], [a for f in in_flight for a in f[4]], grad_x)
    sums, recvd = landed[:len(order)], landed[len(order):]
    stacked, shares = {}, []
    for tag, names in (("a", [n for n in BIG_NAMES if n not in SHARED_LAST]), ("b", list(SHARED_LAST))):
        for (name, l), s, r in zip(order, sums, recvd):
            if name in names:
                stacked[name] = _chip_sum(s, r, where, stacked.get(name), l, P[name].shape[0])
        layout = [(names.index(name), l) for name, l in order if name in names]
        send, recv, *thru = _share_start(tag, [stacked[name] for name in names], layout)
        shares.append((tag, names, send, recv, thru, layout))

    def small_grad(name):
        if name.endswith("_norm") and name[:3] in ("ffn", "mix"):
            return jnp.concatenate([grads[(name, 0)], grads[(name, 1)]], axis=0)
        return grads[(name, 0)]

    packed = jnp.concatenate([_to_lanes(small_grad(name), r) for name, r in SMALL], axis=0)
    total = _small_all_reduce(packed, shares[-1][4][0])
    small_grads, at = {}, 0
    for name, r in SMALL:
        part = total[at:at + r].reshape(-1)
        at += r
        if name == "ev_conv_w":
            full_g = part[:CONV_A_WIDTH * D_CONV].reshape(CONV_A_WIDTH, D_CONV)
            small_grads[name] = lax.dynamic_slice_in_dim(full_g, chip * (D_CONV // 4), D_CONV // 4, axis=1)[None]
        elif name == "od_conv_w":
            full_g = part[:CONV_C_WIDTH * D].reshape(CONV_C_WIDTH, D)
            small_grads[name] = lax.dynamic_slice_in_dim(full_g, chip * (D // 4), D // 4, axis=1)[None]
        else:
            small_grads[name] = part[:math.prod(P[name].shape)].reshape(P[name].shape)

    results = {}

    def update(name, g):
        outs = _adamw(P[name], g, M[name], V[name])
        results[name] = tuple(jnp.swapaxes(a, 1, 2) for a in outs) if name in TRANSPOSED else outs

    for name, _ in SMALL:
        update(name, small_grads[name])
    after = results[SMALL[-1][0]][1]
    for tag, names, send, recv, thru, layout in shares:
        for name, g in zip(names, _share_wait(tag, send, recv, thru, layout, after)):
            update(name, g)
        after = results[names[-1]][1]
    loss_all = lax.psum(loss[0, 0], ("x", "y", "c"))
    return (loss_all, grad_x[None], *[results[name][k] for k in range(4) for name in WEIGHTS])
```

```python
import functools
import math

import jax
import jax.numpy as jnp
from jax import lax
from jax.experimental import pallas as pl
from jax.experimental.pallas import tpu as pltpu

F32, BF16 = jnp.float32, jnp.bfloat16
EPS = 1e-6
FFN_RES = 0.5
N_HEADS, HEAD_DIM = 8, 64
D_CONV = 512
D_ATTN = N_HEADS * HEAD_DIM
CONV_A_WIDTH, CONV_C_WIDTH = 31, 3
ADAM_LR, ADAM_B1, ADAM_B2, ADAM_EPS, ADAM_WD, ADAM_STEP = 0.001, 0.9, 0.999, 1e-08, 0.01, 10
MESH = pl.DeviceIdType.MESH
ANY = pl.BlockSpec(memory_space=pl.ANY)

TOK_TILE = 512
FFN_TILE = 512
DW_TILE = 1024
ATT_TILE = 1024
QKN_TILE = 2048
HALO_A, HALO_C = 32, 16
SUBLANES = 8
CONV_ROWS = 64
ODD_ROWS = 16
SCAN_BLK = 256
MIB = 2 ** 20


def _pallas(body, **kw):
    return pl.pallas_call(body, **kw)


def _cp(sem=None, vmem_mib=48):
    return pltpu.CompilerParams(dimension_semantics=sem, vmem_limit_bytes=vmem_mib * MIB)


def _dot(a, b):
    return jnp.dot(a, b, preferred_element_type=F32)


def _dot_nt(a, b):
    return lax.dot_general(a, b, (((1,), (1,)), ((), ())), preferred_element_type=F32)


def _dot_tn(a, b):
    return lax.dot_general(a, b, (((0,), (0,)), ((), ())), preferred_element_type=F32)


def _sds(shape, dtype):
    return jax.ShapeDtypeStruct(shape, dtype)


def _rms(x):
    return lax.rsqrt(jnp.mean(x * x, axis=-1, keepdims=True) + EPS)


def _rms_bwd(dy, x, g):
    r = _rms(x)
    xh = x * r
    dxh = dy * g
    dx = r * (dxh - xh * jnp.mean(dxh * xh, axis=-1, keepdims=True))
    return dx, xh


def _silu_grad(z):
    s = jax.nn.sigmoid(z)
    return s * (1.0 + z * (1.0 - s))


def _ffn_fwd(x, g, wg, wu, wd):
    S, D = x.shape
    nc, Fs, _ = wd.shape
    tm = min(FFN_TILE, S)
    per = nc
    steps = nc // per

    def body(x_ref, g_ref, wg_ref, wu_ref, wd_ref, out_ref, xn_ref, G_ref, U_ref, acc_ref):
        j = pl.program_id(1)

        @pl.when(j == 0)
        def _():
            xv = x_ref[...]
            xn_ref[...] = (xv * _rms(xv) * g_ref[...]).astype(BF16)
            acc_ref[...] = jnp.zeros_like(acc_ref)

        xn = xn_ref[...]
        part = None
        for k in range(per):
            G = _dot_nt(xn, wg_ref[k])
            U = _dot_nt(xn, wu_ref[k])
            G_ref[k] = G.astype(BF16)
            U_ref[k] = U.astype(BF16)
            term = _dot((G * jax.nn.sigmoid(G) * U).astype(BF16), wd_ref[k])
            part = term if part is None else part + term
        acc_ref[...] += part

        @pl.when(j == steps - 1)
        def _():
            out_ref[...] = x_ref[...] + FFN_RES * acc_ref[...]

    row = pl.BlockSpec((tm, D), lambda i, j: (i, 0))
    wblk = pl.BlockSpec((per, Fs, D), lambda i, j: (j, 0, 0), pipeline_mode=pl.Buffered(1))
    hid = pl.BlockSpec((per, tm, Fs), lambda i, j: (j, i, 0))
    return _pallas(
        body, name="ffn_fwd", grid=(S // tm, steps),
        in_specs=[row, pl.BlockSpec((1, D), lambda i, j: (0, 0)), wblk, wblk, wblk],
        out_specs=[row, row, hid, hid],
        out_shape=[_sds((S, D), F32), _sds((S, D), BF16), _sds((nc, S, Fs), BF16), _sds((nc, S, Fs), BF16)],
        scratch_shapes=[pltpu.VMEM((tm, D), F32)],
        compiler_params=_cp(("parallel", "arbitrary"), 56),
    )(x, g, wg, wu, wd)


def _ffn_bwd_w(dout, xn, G, U, wd):
    S, D = dout.shape
    nc, _, Fs = G.shape
    tm = min(DW_TILE, S)
    nt = S // tm
    sub = min(TOK_TILE, tm)

    def body(do_ref, xn_ref, G_ref, U_ref, wd_ref, dwg_ref, dwu_ref, dwd_ref, dG_ref, dU_ref, ag, au, ad, do_s, H_s):
        i = pl.program_id(1)

        @pl.when(i == 0)
        def _():
            ag[...] = jnp.zeros_like(ag)
            au[...] = jnp.zeros_like(au)
            ad[...] = jnp.zeros_like(ad)

        for r in range(0, tm, sub):
            rows = pl.ds(r, sub)
            do = (FFN_RES * do_ref[rows, :]).astype(BF16)
            do_s[rows, :] = do
            Gv = G_ref[0, rows, :].astype(F32)
            Uv = U_ref[0, rows, :].astype(F32)
            dH = _dot_nt(do, wd_ref[0])
            sg = jax.nn.sigmoid(Gv)
            act = Gv * sg
            H_s[rows, :] = (act * Uv).astype(BF16)
            dU_ref[0, rows, :] = (dH * act).astype(BF16)
            dG_ref[0, rows, :] = (dH * Uv * (sg * (1.0 + Gv * (1.0 - sg)))).astype(BF16)
        xnv = xn_ref[...]
        ag[...] += _dot_tn(dG_ref[0], xnv)
        au[...] += _dot_tn(dU_ref[0], xnv)
        ad[...] += _dot_tn(H_s[...], do_s[...])

        @pl.when(i == nt - 1)
        def _():
            dwg_ref[0] = ag[...].astype(BF16)
            dwu_ref[0] = au[...].astype(BF16)
            dwd_ref[0] = ad[...].astype(BF16)

    row = pl.BlockSpec((tm, D), lambda j, i: (i, 0))
    hid = pl.BlockSpec((1, tm, Fs), lambda j, i: (j, i, 0))
    wrow = pl.BlockSpec((1, Fs, D), lambda j, i: (j, 0, 0))
    return _pallas(
        body, name="ffn_bwd_w", grid=(nc, nt),
        in_specs=[row, row, hid, hid, wrow],
        out_specs=[wrow, wrow, wrow, hid, hid],
        out_shape=[_sds((nc, Fs, D), BF16)] * 3 + [_sds((nc, S, Fs), BF16)] * 2,
        scratch_shapes=[pltpu.VMEM((Fs, D), F32)] * 3 + [pltpu.VMEM((tm, D), BF16), pltpu.VMEM((tm, Fs), BF16)],
        compiler_params=_cp(("parallel", "arbitrary"), 56),
    )(dout, xn, G, U, wd)


def _norm_in_bwd(dzs, ws, x, g, dres, w_rows=False):
    S, D = x.shape
    nc = dzs[0].shape[0]
    n = len(dzs)
    tm = TOK_TILE
    per = nc
    steps = nc // per

    def body(*refs):
        dz_refs, w_refs = refs[:n], refs[n:2 * n]
        x_ref, g_ref, dres_ref, dx_ref, dg_ref, acc_ref = refs[2 * n:]
        i, j = pl.program_id(0), pl.program_id(1)

        @pl.when(j == 0)
        def _():
            acc_ref[...] = jnp.zeros_like(acc_ref)

        @pl.when((i == 0) & (j == 0))
        def _():
            dg_ref[...] = jnp.zeros_like(dg_ref)

        part = None
        for dz_ref, w_ref in zip(dz_refs, w_refs):
            for k in range(per):
                term = _dot(dz_ref[k], w_ref[k]) if w_rows else _dot_nt(dz_ref[k], w_ref[k])
                part = term if part is None else part + term
        acc_ref[...] += part

        @pl.when(j == steps - 1)
        def _():
            dxn = acc_ref[...]
            dx, xh = _rms_bwd(dxn, x_ref[...], g_ref[...])
            dx_ref[...] = dx + dres_ref[...]
            dg_ref[...] += jnp.sum(dxn * xh, axis=0, keepdims=True)

    row = pl.BlockSpec((tm, D), lambda i, j: (i, 0))
    one = pl.BlockSpec((1, D), lambda i, j: (0, 0))
    in_specs = [pl.BlockSpec((per, tm, dz.shape[2]), lambda i, j: (j, i, 0)) for dz in dzs]
    in_specs += [pl.BlockSpec((per,) + w.shape[1:], lambda i, j: (j, 0, 0)) for w in ws]
    return _pallas(
        body, name="norm_in_bwd", grid=(S // tm, steps),
        in_specs=in_specs + [row, one, row], out_specs=[row, one],
        out_shape=[_sds((S, D), F32), _sds((1, D), F32)],
        scratch_shapes=[pltpu.VMEM((tm, D), F32)],
        compiler_params=_cp(("arbitrary", "arbitrary")),
    )(*dzs, *ws, x, g, dres)


def _norm_proj(x, g, w, w2=None):
    S, D = x.shape
    N = w.shape[1]
    tm = TOK_TILE

    def body(*refs):
        if w2 is None:
            x_ref, g_ref, w_ref, h_ref, z_ref = refs
        else:
            x_ref, g_ref, w_ref, w2_ref, h_ref, z_ref, z2_ref = refs
        xv = x_ref[...]
        h = (xv * _rms(xv) * g_ref[...]).astype(BF16)
        h_ref[...] = h
        z_ref[...] = _dot(h, w_ref[...]).astype(BF16)
        if w2 is not None:
            z2_ref[...] = _dot(h, w2_ref[...])

    row = pl.BlockSpec((tm, D), lambda i: (i, 0))
    in_specs = [row, pl.BlockSpec((1, D), lambda i: (0, 0)), pl.BlockSpec((D, N), lambda i: (0, 0))]
    out_specs = [row, pl.BlockSpec((tm, N), lambda i: (i, 0))]
    out_shape = [_sds((S, D), BF16), _sds((S, N), BF16)]
    args = [x, g, w]
    if w2 is not None:
        N2 = w2.shape[1]
        in_specs.append(pl.BlockSpec((D, N2), lambda i: (0, 0)))
        out_specs.append(pl.BlockSpec((tm, N2), lambda i: (i, 0)))
        out_shape.append(_sds((S, N2), F32))
        args.append(w2)
    return _pallas(body, name="norm_proj", grid=(S // tm,), in_specs=in_specs, out_specs=out_specs,
                   out_shape=out_shape, compiler_params=_cp(("parallel",)))(*args)


def _proj_res(acts, ws, res):
    S, D = res.shape
    n = len(acts)
    tm = TOK_TILE

    def body(*refs):
        a_refs, w_refs = refs[:n], refs[n:2 * n]
        res_ref, out_ref = refs[2 * n:]
        acc = res_ref[...]
        for a_ref, w_ref in zip(a_refs, w_refs):
            acc = acc + _dot(a_ref[...], w_ref[...])
        out_ref[...] = acc

    row = pl.BlockSpec((tm, D), lambda i: (i, 0))
    in_specs = [pl.BlockSpec((tm, a.shape[1]), lambda i: (i, 0)) for a in acts]
    in_specs += [pl.BlockSpec(w.shape, lambda i: (0, 0)) for w in ws]
    return _pallas(body, name="proj_res", grid=(S // tm,), in_specs=in_specs + [row], out_specs=row,
                   out_shape=_sds((S, D), F32), compiler_params=_cp(("parallel",)))(*acts, *ws, res)


def _matmul_nt(a, w, after=None):
    S, K = a.shape
    M = w.shape[0]
    tm = TOK_TILE

    def body(a_ref, w_ref, *rest):
        rest[-1][...] = _dot_nt(a_ref[...].astype(BF16), w_ref[...])

    extra = [] if after is None else [after]
    return _pallas(body, name="matmul_nt", grid=(S // tm,),
                   in_specs=[pl.BlockSpec((tm, K), lambda i: (i, 0)), pl.BlockSpec((M, K), lambda i: (0, 0))] + [ANY] * len(extra),
                   out_specs=pl.BlockSpec((tm, M), lambda i: (i, 0)), out_shape=_sds((S, M), F32),
                   compiler_params=_cp(("parallel",)))(a, w, *extra)


def _matmul_tn(a, b, tn):
    S, M = a.shape
    N = b.shape[1]
    tm = min(DW_TILE, S)
    nt = S // tm

    def body(a_ref, b_ref, o_ref, acc_ref):
        i = pl.program_id(1)

        @pl.when(i == 0)
        def _():
            acc_ref[...] = jnp.zeros_like(acc_ref)

        acc_ref[...] += _dot_tn(a_ref[...].astype(BF16), b_ref[...].astype(BF16))

        @pl.when(i == nt - 1)
        def _():
            o_ref[0] = acc_ref[...].astype(BF16)

    return _pallas(body, name="matmul_tn", grid=(N // tn, nt),
                   in_specs=[pl.BlockSpec((tm, M), lambda j, i: (i, 0)), pl.BlockSpec((tm, tn), lambda j, i: (i, j))],
                   out_specs=pl.BlockSpec((1, M, tn), lambda j, i: (j, 0, 0)), out_shape=_sds((N // tn, M, tn), BF16),
                   scratch_shapes=[pltpu.VMEM((M, tn), F32)],
                   compiler_params=_cp(("parallel", "arbitrary")))(a, b)


ALL_SHIFTS = tuple(range(SUBLANES))


def _fill_shifts(win, rows, shifts=ALL_SHIFTS):
    for i, b in enumerate(shifts):
        if b:
            win[i, pl.ds(0, rows - SUBLANES), :] = win[0, pl.ds(b, rows - SUBLANES), :]


def _tap(win, offset, n, base=0, shifts=ALL_SHIFTS):
    start = base + (offset - offset % SUBLANES)
    if not isinstance(start, int):
        start = pl.multiple_of(start, SUBLANES)
    return win[shifts.index(offset % SUBLANES), pl.ds(start, n), :]


def _conv_a_fwd(z, cw, cb, cn):
    S = z.shape[0]
    C = D_CONV
    tm = TOK_TILE
    hb = tm // HALO_A

    def body(u_ref, gt_ref, up_ref, gp_ref, cw_ref, cb_ref, cn_ref, a_ref, a1_ref, win):
        i = pl.program_id(0)
        prev = up_ref[...].astype(F32) * jax.nn.sigmoid(gp_ref[...].astype(F32))
        win[0, pl.ds(0, HALO_A), :] = jnp.where(i == 0, 0.0, prev)
        win[0, pl.ds(HALO_A, tm), :] = u_ref[...].astype(F32) * jax.nn.sigmoid(gt_ref[...].astype(F32))
        _fill_shifts(win, tm + HALO_A)

        acc = jnp.zeros((tm, C), F32)
        for k in range(CONV_A_WIDTH):
            acc = acc + cw_ref[k:k + 1, :] * _tap(win, HALO_A - (CONV_A_WIDTH - 1) + k, tm)
        a1 = acc + cb_ref[...]
        a1_ref[...] = a1
        a2 = a1 * _rms(a1) * cn_ref[...]
        a_ref[...] = (a2 * jax.nn.sigmoid(a2)).astype(BF16)

    cur = lambda c: pl.BlockSpec((tm, C), lambda i, c=c: (i, c))
    prv = lambda c: pl.BlockSpec((HALO_A, C), lambda i, c=c: (jnp.maximum(i * hb - 1, 0), c))
    vec = pl.BlockSpec((1, C), lambda i: (0, 0))
    return _pallas(body, name="conv_a_fwd", grid=(S // tm,),
                   in_specs=[cur(0), cur(1), prv(0), prv(1), pl.BlockSpec((32, C), lambda i: (0, 0)), vec, vec],
                   out_specs=[pl.BlockSpec((tm, C), lambda i: (i, 0)), pl.BlockSpec((tm, C), lambda i: (i, 0))],
                   out_shape=[_sds((S, C), BF16), _sds((S, C), F32)],
                   scratch_shapes=[pltpu.VMEM((SUBLANES, tm + HALO_A, C), F32)],
                   compiler_params=_cp(("parallel",)))(z, z, z, z, cw, cb, cn)


def _conv_a_bwd(da, a1, z, cw, cn):
    S = z.shape[0]
    C = D_CONV
    tm = TOK_TILE
    hb = tm // HALO_A
    nt = S // tm
    W = CONV_A_WIDTH

    def body(da_ref, a1_ref, dan_ref, a1n_ref, u_ref, gt_ref, up_ref, gp_ref, cw_ref, cn_ref,
             duz_ref, dcw_ref, dcb_ref, dcn_ref, win, dwin):
        i = pl.program_id(0)

        @pl.when(i == 0)
        def _():
            dcw_ref[...] = jnp.zeros_like(dcw_ref)
            dcb_ref[...] = jnp.zeros_like(dcb_ref)
            dcn_ref[...] = jnp.zeros_like(dcn_ref)

        cnv = cn_ref[...]

        def da1_of(dav, a1v):
            a2 = a1v * _rms(a1v) * cnv
            da2 = dav * _silu_grad(a2)
            dx, xh = _rms_bwd(da2, a1v, cnv)
            return dx, da2 * xh

        da1, dcn_t = da1_of(da_ref[...], a1_ref[...])
        da1n, _ = da1_of(dan_ref[...], a1n_ref[...])
        dwin[0, pl.ds(0, tm), :] = da1
        dwin[0, pl.ds(tm, HALO_A), :] = jnp.where(i == nt - 1, 0.0, da1n)
        _fill_shifts(dwin, tm + HALO_A)
        dcb_ref[...] += jnp.sum(da1, axis=0, keepdims=True)
        dcn_ref[...] += jnp.sum(dcn_t, axis=0, keepdims=True)

        prev = up_ref[...].astype(F32) * jax.nn.sigmoid(gp_ref[...].astype(F32))
        win[0, pl.ds(0, HALO_A), :] = jnp.where(i == 0, 0.0, prev)
        win[0, pl.ds(HALO_A, tm), :] = u_ref[...].astype(F32) * jax.nn.sigmoid(gt_ref[...].astype(F32))
        _fill_shifts(win, tm + HALO_A)

        def rows_block(rb, carry):
            r0 = pl.multiple_of(rb * CONV_ROWS, CONV_ROWS)
            rows = pl.ds(r0, CONV_ROWS)
            da1_b = dwin[0, rows, :]
            da0 = jnp.zeros((CONV_ROWS, C), F32)
            for k in range(W):
                da0 = da0 + cw_ref[k:k + 1, :] * _tap(dwin, W - 1 - k, CONV_ROWS, r0)
                dcw_ref[k:k + 1, :] += jnp.sum(da1_b * _tap(win, HALO_A - (W - 1) + k, CONV_ROWS, r0), axis=0, keepdims=True)
            u = u_ref[rows, :].astype(F32)
            sg = jax.nn.sigmoid(gt_ref[rows, :].astype(F32))
            duz_ref[rows, 0:C] = (da0 * sg).astype(BF16)
            duz_ref[rows, C:2 * C] = (da0 * u * sg * (1.0 - sg)).astype(BF16)
            return carry

        lax.fori_loop(0, tm // CONV_ROWS, rows_block, 0)

    cur = lambda c: pl.BlockSpec((tm, C), lambda i, c=c: (i, c))
    prv = lambda c: pl.BlockSpec((HALO_A, C), lambda i, c=c: (jnp.maximum(i * hb - 1, 0), c))
    nxt = pl.BlockSpec((HALO_A, C), lambda i: (jnp.minimum((i + 1) * hb, S // HALO_A - 1), 0))
    vec = pl.BlockSpec((1, C), lambda i: (0, 0))
    return _pallas(body, name="conv_a_bwd", grid=(nt,),
                   in_specs=[cur(0), cur(0), nxt, nxt, cur(0), cur(1), prv(0), prv(1),
                             pl.BlockSpec((32, C), lambda i: (0, 0)), vec],
                   out_specs=[pl.BlockSpec((tm, 2 * C), lambda i: (i, 0)), pl.BlockSpec((32, C), lambda i: (0, 0)), vec, vec],
                   out_shape=[_sds((S, 2 * C), BF16), _sds((32, C), F32), _sds((1, C), F32), _sds((1, C), F32)],
                   scratch_shapes=[pltpu.VMEM((SUBLANES, tm + HALO_A, C), F32)] * 2,
                   compiler_params=_cp(("arbitrary",)))(da, a1, da, a1, z, z, z, z, cw, cn)


def _forget_scan(fl, bf):
    S, L = fl.shape
    B = SCAN_BLK

    def body(fl_ref, bf_ref, flb_ref, F_ref):
        tri = (lax.broadcasted_iota(jnp.int32, (B, B), 0) >= lax.broadcasted_iota(jnp.int32, (B, B), 1)).astype(F32)

        def step(c, carry):
            rows = pl.ds(pl.multiple_of(c * B, B), B)
            v = fl_ref[rows, :] + bf_ref[...]
            flb_ref[rows, :] = v
            lf = jnp.minimum(v, 0.0) - jnp.log1p(jnp.exp(-jnp.abs(v)))
            cs = jnp.dot(tri, lf, precision=lax.Precision.HIGHEST, preferred_element_type=F32) + carry
            F_ref[rows, :] = cs
            return cs[B - 1:B, :]

        lax.fori_loop(0, S // B, step, jnp.zeros((1, L), F32))

    return _pallas(body, name="forget_scan", out_shape=[_sds((S, L), F32), _sds((S, L), F32)],
                   compiler_params=_cp())(fl, bf)


def _forget_scan_bwd(dF, flb):
    S, L = dF.shape
    B = SCAN_BLK
    nb = S // B

    def body(dF_ref, flb_ref, dfl_ref, db_ref):
        tri = (lax.broadcasted_iota(jnp.int32, (B, B), 0) <= lax.broadcasted_iota(jnp.int32, (B, B), 1)).astype(F32)

        def step(t, carry):
            carry_cs, db = carry
            rows = pl.ds(pl.multiple_of((nb - 1 - t) * B, B), B)
            cs = jnp.dot(tri, dF_ref[rows, :], precision=lax.Precision.HIGHEST, preferred_element_type=F32) + carry_cs
            dfl = cs * jax.nn.sigmoid(-flb_ref[rows, :])
            dfl_ref[rows, :] = dfl
            return cs[0:1, :], db + jnp.sum(dfl, axis=0, keepdims=True)

        _, db = lax.fori_loop(0, nb, step, (jnp.zeros((1, L), F32), jnp.zeros((1, L), F32)))
        db_ref[...] = db

    return _pallas(body, name="forget_scan_bwd", out_shape=[_sds((S, L), F32), _sds((1, L), F32)],
                   compiler_params=_cp())(dF, flb)


NEG = -1e30


def _causal_mask(t):
    return lax.broadcasted_iota(jnp.int32, (t, t), 0) >= lax.broadcasted_iota(jnp.int32, (t, t), 1)


AUG = 128
C_F, C_ONE, C_LSE = 64, 67, 70


def _split3(f):
    a = f.astype(BF16).astype(F32)
    r = f - a
    b = r.astype(BF16).astype(F32)
    return a, b, r - b


def _put3(lane, base, parts, other):
    out = other
    for k, p in enumerate(parts):
        out = jnp.where(lane == base + k, p, out)
    return out


def _ones3(lane, base):
    return (lane >= base) & (lane < base + 3)


def _lane_ids(rows):
    return lax.broadcasted_iota(jnp.int32, (rows, AUG), 1)


def _pair_rms(x, lo):
    sq = x * x
    ms_a = jnp.sum(jnp.where(lo, sq, 0.0), axis=-1, keepdims=True) * (1.0 / HEAD_DIM)
    ms_b = jnp.sum(jnp.where(lo, 0.0, sq), axis=-1, keepdims=True) * (1.0 / HEAD_DIM)
    return jnp.where(lo, lax.rsqrt(ms_a + EPS), lax.rsqrt(ms_b + EPS))


def _qkv_prep(z, Fc, qw, kw):
    S = z.shape[0]
    tp = min(QKN_TILE, S)
    scale = 1.0 / math.sqrt(HEAD_DIM)

    def body(zq_ref, zk_ref, zv_ref, F_ref, qw_ref, kw_ref, q_ref, k_ref, v_ref):
        j = pl.program_id(0)
        lane = _lane_ids(tp)
        lo = lane < HEAD_DIM
        Fv = F_ref[...]
        xq = zq_ref[...].astype(F32)
        xk = zk_ref[...].astype(F32)
        qn = xq * _pair_rms(xq, lo) * qw_ref[...] * scale
        kn = xk * _pair_rms(xk, lo) * kw_ref[...]
        vv = zv_ref[...].astype(F32)
        for half in range(2):
            take = (lambda a: a) if half == 0 else (lambda a: pltpu.roll(a, HEAD_DIM, 1))
            fp = _split3(jnp.sum(jnp.where(lane == 2 * j + half, Fv, 0.0), axis=-1, keepdims=True))
            qx = _put3(lane, C_F, fp, jnp.where(_ones3(lane, C_ONE), 1.0, 0.0))
            kx = _put3(lane, C_ONE, [-p for p in fp], jnp.where(_ones3(lane, C_F) | _ones3(lane, C_LSE), 1.0, 0.0))
            vx = jnp.where(_ones3(lane, C_F), 1.0, 0.0)
            q_ref[half] = jnp.where(lo, take(qn), qx).astype(BF16)
            k_ref[half] = jnp.where(lo, take(kn), kx).astype(BF16)
            v_ref[half] = jnp.where(lo, take(vv), vx).astype(BF16)

    col = lambda c0: pl.BlockSpec((tp, AUG), lambda j, i, c0=c0: (i, c0 + j))
    vec = pl.BlockSpec((1, AUG), lambda j, i: (0, 0))
    out = pl.BlockSpec((2, tp, AUG), lambda j, i: (j, i, 0))
    return _pallas(body, name="qkv_prep", grid=(N_HEADS // 2, S // tp),
                   in_specs=[col(8), col(12), col(16), pl.BlockSpec((tp, AUG), lambda j, i: (i, 0)), vec, vec],
                   out_specs=[out, out, out], out_shape=[_sds((N_HEADS, S, AUG), BF16)] * 3,
                   compiler_params=_cp(("parallel", "parallel")))(z, z, z, Fc, qw, kw)


def _fox_fwd(q_aug, k_aug, v_aug):
    H, S, A = q_aug.shape
    t = ATT_TILE
    nq = S // t

    def body(q_ref, k_ref, v_ref, o_ref, q2_ref):
        i = pl.program_id(1)
        q = q_ref[0]

        def tile(j, carry, diag):
            m, acc = carry
            rows = pl.ds(pl.multiple_of(j * t, t), t)
            s = _dot_nt(q, k_ref[0, rows, :])
            if diag:
                s = jnp.where(_causal_mask(t), s, NEG)
            m_new = jnp.maximum(m, jnp.max(s, axis=-1, keepdims=True))
            p = jnp.exp(s - m_new)
            acc = jnp.exp(m - m_new) * acc + _dot(p.astype(BF16), v_ref[0, rows, :])
            return m_new, acc

        init = (jnp.full((t, 1), NEG, F32), jnp.zeros((t, A), F32))
        carry = lax.fori_loop(0, i, lambda j, c: tile(j, c, False), init)
        m, acc = tile(i, carry, True)
        lane = _lane_ids(t)
        l = jnp.sum(jnp.where(lane == C_F, acc, 0.0), axis=-1, keepdims=True)
        o_ref[0] = (acc / l).astype(BF16)
        lse = m + jnp.log(l)
        q2_ref[0] = (q.astype(F32) + _put3(lane, C_LSE, [-p for p in _split3(lse)], 0.0)).astype(BF16)

    qblk = pl.BlockSpec((1, t, A), lambda h, i: (h, i, 0))
    full = pl.BlockSpec((1, S, A), lambda h, i: (h, 0, 0))
    return _pallas(body, name="fox_fwd", grid=(H, nq), in_specs=[qblk, full, full], out_specs=[qblk, qblk],
                   out_shape=[_sds((H, S, A), BF16)] * 2, compiler_params=_cp(("parallel", "parallel")))(q_aug, k_aug, v_aug)


def _do_prep(dcat, o_aug):
    S = dcat.shape[0]
    tp = min(QKN_TILE, S)

    def body(d_ref, o_ref, out_ref):
        lane = _lane_ids(tp)
        lo = lane < HEAD_DIM
        x = d_ref[...]
        for half in range(2):
            d = jnp.where(lo, x if half == 0 else pltpu.roll(x, HEAD_DIM, 1), 0.0)
            delta = jnp.sum(d * o_ref[half].astype(F32), axis=-1, keepdims=True)
            out_ref[half] = jnp.where(lo, d, _put3(lane, C_F, [-p for p in _split3(delta)], 0.0)).astype(BF16)

    pair = pl.BlockSpec((2, tp, AUG), lambda j, i: (j, i, 0))
    return _pallas(body, name="do_prep", grid=(N_HEADS // 2, S // tp),
                   in_specs=[pl.BlockSpec((tp, AUG), lambda j, i: (i, D_CONV // AUG + j)), pair], out_specs=pair,
                   out_shape=_sds((N_HEADS, S, AUG), BF16), compiler_params=_cp(("parallel", "parallel")))(dcat, o_aug)


def _fox_bwd(q2, k_aug, v_aug, do_aug):
    H, S, A = q2.shape
    t = ATT_TILE
    nq = S // t

    def body(q_ref, k_ref, v_ref, do_ref, dq_ref, dk_ref, dv_ref):
        j = pl.program_id(1)

        @pl.when(j == 0)
        def _():
            dq_ref[...] = jnp.zeros_like(dq_ref)

        k = k_ref[0]
        vv = v_ref[0]

        def tile(i, carry, diag):
            dk, dv = carry
            rows = pl.ds(pl.multiple_of(i * t, t), t)
            q = q_ref[0, rows, :]
            dov = do_ref[0, rows, :]
            s = _dot_nt(q, k)
            if diag:
                s = jnp.where(_causal_mask(t), s, NEG)
            p = jnp.exp(s)
            dv = dv + _dot_tn(p.astype(BF16), dov)
            dsb = (p * _dot_nt(dov, vv)).astype(BF16)
            dq_ref[0, rows, :] += _dot(dsb, k)
            dk = dk + _dot_tn(dsb, q)
            return dk, dv

        init = (jnp.zeros((t, A), F32), jnp.zeros((t, A), F32))
        carry = tile(j, init, True)
        dk, dv = lax.fori_loop(j + 1, nq, lambda i, c: tile(i, c, False), carry)
        dk_ref[0] = dk
        dv_ref[0] = dv

    full = pl.BlockSpec((1, S, A), lambda h, j: (h, 0, 0))
    kblk = pl.BlockSpec((1, t, A), lambda h, j: (h, j, 0))
    return _pallas(body, name="fox_bwd", grid=(H, nq), in_specs=[full, kblk, kblk, full], out_specs=[full, kblk, kblk],
                   out_shape=[_sds((H, S, A), F32)] * 3,
                   compiler_params=_cp(("parallel", "arbitrary")))(q2, k_aug, v_aug, do_aug)


def _qkv_bwd(dq, dk, dv, z, qw, kw):
    S = z.shape[0]
    tp = min(QKN_TILE, S)
    scale = 1.0 / math.sqrt(HEAD_DIM)

    def body(dq_ref, dk_ref, dv_ref, zq_ref, zk_ref, qw_ref, kw_ref, dqf_ref, dkf_ref, dvf_ref, dF_ref, dqw_ref, dkw_ref):
        i, j = pl.program_id(0), pl.program_id(1)
        lane = _lane_ids(tp)
        lo = lane < HEAD_DIM

        @pl.when((i == 0) & (j == 0))
        def _():
            dqw_ref[...] = jnp.zeros_like(dqw_ref)
            dkw_ref[...] = jnp.zeros_like(dkw_ref)

        def pair(ref):
            return jnp.where(lo, ref[0], pltpu.roll(ref[1], HEAD_DIM, 1))

        def norm_bwd(g, x, w):
            r = _pair_rms(x, lo)
            xh = x * r
            dxh = g * w
            tt = dxh * xh
            mean_a = jnp.sum(jnp.where(lo, tt, 0.0), axis=-1, keepdims=True) * (1.0 / HEAD_DIM)
            mean_b = jnp.sum(jnp.where(lo, 0.0, tt), axis=-1, keepdims=True) * (1.0 / HEAD_DIM)
            return r * (dxh - xh * jnp.where(lo, mean_a, mean_b)), g * xh

        dxq, gq = norm_bwd(pair(dq_ref) * scale, zq_ref[...].astype(F32), qw_ref[...])
        dqf_ref[...] = dxq.astype(BF16)
        dqw_ref[...] += jnp.sum(gq, axis=0, keepdims=True)
        dxk, gk = norm_bwd(pair(dk_ref), zk_ref[...].astype(F32), kw_ref[...])
        dkf_ref[...] = dxk.astype(BF16)
        dkw_ref[...] += jnp.sum(gk, axis=0, keepdims=True)
        dvf_ref[...] = pair(dv_ref).astype(BF16)

        contrib = jnp.zeros((tp, AUG), F32)
        for half in range(2):
            df = (jnp.sum(jnp.where(lane == C_F, dq_ref[half], 0.0), axis=-1, keepdims=True)
                  - jnp.sum(jnp.where(lane == C_ONE, dk_ref[half], 0.0), axis=-1, keepdims=True))
            contrib = jnp.where(lane == 2 * j + half, df, contrib)

        @pl.when(j == 0)
        def _():
            dF_ref[...] = contrib

        @pl.when(j > 0)
        def _():
            dF_ref[...] += contrib

    pairb = pl.BlockSpec((2, tp, AUG), lambda i, j: (j, i, 0))
    col = lambda c0: pl.BlockSpec((tp, AUG), lambda i, j, c0=c0: (i, c0 + j))
    vec = pl.BlockSpec((1, AUG), lambda i, j: (0, 0))
    flat = pl.BlockSpec((tp, AUG), lambda i, j: (i, j))
    return _pallas(body, name="qkv_bwd", grid=(S // tp, N_HEADS // 2),
                   in_specs=[pairb, pairb, pairb, col(8), col(12), vec, vec],
                   out_specs=[flat, flat, flat, pl.BlockSpec((tp, AUG), lambda i, j: (i, 0)), vec, vec],
                   out_shape=[_sds((S, D_ATTN), BF16)] * 3 + [_sds((S, AUG), F32), _sds((1, AUG), F32), _sds((1, AUG), F32)],
                   compiler_params=_cp(("arbitrary", "arbitrary")))(dq, dk, dv, z, z, qw, kw)


def _proj_res_heads(a, wa, o_aug, wo, res):
    S, D = res.shape
    H = o_aug.shape[0]
    tm = TOK_TILE

    def body(a_ref, wa_ref, o_ref, wo_ref, res_ref, out_ref):
        acc = res_ref[...] + _dot(a_ref[...], wa_ref[...])
        for h in range(H):
            acc = acc + _dot(o_ref[h], wo_ref[h])
        out_ref[...] = acc

    row = pl.BlockSpec((tm, D), lambda i: (i, 0))
    return _pallas(body, name="proj_res_heads", grid=(S // tm,),
                   in_specs=[pl.BlockSpec((tm, a.shape[1]), lambda i: (i, 0)), pl.BlockSpec(wa.shape, lambda i: (0, 0)),
                             pl.BlockSpec((H, tm, AUG), lambda i: (0, i, 0)), pl.BlockSpec(wo.shape, lambda i: (0, 0, 0)), row],
                   out_specs=row, out_shape=_sds((S, D), F32), compiler_params=_cp(("parallel",)))(a, wa, o_aug, wo, res)


def _heads_tn(o_aug, d):
    H, S, A = o_aug.shape
    D = d.shape[1]
    tm = min(DW_TILE, S)
    nt = S // tm

    def body(o_ref, d_ref, out_ref, acc_ref):
        i = pl.program_id(0)

        @pl.when(i == 0)
        def _():
            acc_ref[...] = jnp.zeros_like(acc_ref)

        dv = d_ref[...].astype(BF16)
        for h in range(H):
            acc_ref[h] += _dot_tn(o_ref[h], dv)

        @pl.when(i == nt - 1)
        def _():
            out_ref[...] = acc_ref[...].astype(BF16)

    return _pallas(body, name="heads_tn", grid=(nt,),
                   in_specs=[pl.BlockSpec((H, tm, A), lambda i: (0, i, 0)), pl.BlockSpec((tm, D), lambda i: (i, 0))],
                   out_specs=pl.BlockSpec((H, A, D), lambda i: (0, 0, 0)), out_shape=_sds((H, A, D), BF16),
                   scratch_shapes=[pltpu.VMEM((H, A, D), F32)], compiler_params=_cp(("arbitrary",)))(o_aug, d)


def _odd_mid_fwd(z, cw):
    S = z.shape[0]
    D = z.shape[1] // 3
    tm = TOK_TILE
    hb = tm // HALO_C
    W = CONV_C_WIDTH

    def body(gb_ref, gc_ref, hh_ref, gcp_ref, hhp_ref, cw_ref, y_ref, win):
        i = pl.program_id(0)
        prev = gcp_ref[...].astype(F32) * hhp_ref[...].astype(F32)
        win[pl.ds(0, HALO_C), :] = jnp.where(i == 0, 0.0, prev)
        win[pl.ds(HALO_C, tm), :] = gc_ref[...].astype(F32) * hh_ref[...].astype(F32)
        c1 = jnp.zeros((tm, D), F32)
        for k in range(W):
            c1 = c1 + cw_ref[k:k + 1, :] * win[pl.ds(HALO_C - (W - 1) + k, tm), :]
        y_ref[...] = (gb_ref[...].astype(F32) * c1).astype(BF16)

    cur = lambda c: pl.BlockSpec((tm, D), lambda i, c=c: (i, c))
    prv = lambda c: pl.BlockSpec((HALO_C, D), lambda i, c=c: (jnp.maximum(i * hb - 1, 0), c))
    return _pallas(body, name="odd_mid_fwd", grid=(S // tm,),
                   in_specs=[cur(0), cur(1), cur(2), prv(1), prv(2), pl.BlockSpec((8, D), lambda i: (0, 0))],
                   out_specs=pl.BlockSpec((tm, D), lambda i: (i, 0)), out_shape=_sds((S, D), BF16),
                   scratch_shapes=[pltpu.VMEM((tm + HALO_C, D), F32)],
                   compiler_params=_cp(("parallel",)))(z, z, z, z, z, cw)


def _odd_mid_bwd(dy, z, cw):
    S = z.shape[0]
    D = z.shape[1] // 3
    tm = TOK_TILE
    hb = tm // HALO_C
    nt = S // tm
    W = CONV_C_WIDTH
    shifts_w = (0,) + tuple(sorted({(HALO_C - (W - 1) + k) % SUBLANES for k in range(W)} - {0}))
    shifts_d = (0,) + tuple(sorted({(W - 1 - k) % SUBLANES for k in range(W)} - {0}))

    def body(dy_ref, dyn_ref, gb_ref, gbn_ref, gc_ref, hh_ref, gcp_ref, hhp_ref, cw_ref, dz_ref, dcw_ref, win, dwin):
        i = pl.program_id(0)

        @pl.when(i == 0)
        def _():
            dcw_ref[...] = jnp.zeros_like(dcw_ref)

        prev = gcp_ref[...].astype(F32) * hhp_ref[...].astype(F32)
        win[0, pl.ds(0, HALO_C), :] = jnp.where(i == 0, 0.0, prev)
        win[0, pl.ds(HALO_C, tm), :] = gc_ref[...].astype(F32) * hh_ref[...].astype(F32)
        _fill_shifts(win, tm + HALO_C, shifts_w)
        dwin[0, pl.ds(0, tm), :] = dy_ref[...] * gb_ref[...].astype(F32)
        dwin[0, pl.ds(tm, HALO_C), :] = jnp.where(i == nt - 1, 0.0, dyn_ref[...] * gbn_ref[...].astype(F32))
        _fill_shifts(dwin, tm + HALO_C, shifts_d)

        def rows_block(rb, carry):
            r0 = pl.multiple_of(rb * ODD_ROWS, ODD_ROWS)
            rows = pl.ds(r0, ODD_ROWS)
            dc1 = dwin[0, rows, :]
            c1 = jnp.zeros((ODD_ROWS, D), F32)
            dc0 = jnp.zeros((ODD_ROWS, D), F32)
            for k in range(W):
                tap = _tap(win, HALO_C - (W - 1) + k, ODD_ROWS, r0, shifts_w)
                c1 = c1 + cw_ref[k:k + 1, :] * tap
                dc0 = dc0 + cw_ref[k:k + 1, :] * _tap(dwin, W - 1 - k, ODD_ROWS, r0, shifts_d)
                dcw_ref[k:k + 1, :] += jnp.sum(dc1 * tap, axis=0, keepdims=True)
            dz_ref[rows, 0:D] = (dy_ref[rows, :] * c1).astype(BF16)
            dz_ref[rows, D:2 * D] = (dc0 * hh_ref[rows, :].astype(F32)).astype(BF16)
            dz_ref[rows, 2 * D:3 * D] = (dc0 * gc_ref[rows, :].astype(F32)).astype(BF16)
            return carry

        lax.fori_loop(0, tm // ODD_ROWS, rows_block, 0)

    cur = lambda c: pl.BlockSpec((tm, D), lambda i, c=c: (i, c))
    prv = lambda c: pl.BlockSpec((HALO_C, D), lambda i, c=c: (jnp.maximum(i * hb - 1, 0), c))
    nxt = pl.BlockSpec((HALO_C, D), lambda i: (jnp.minimum((i + 1) * hb, S // HALO_C - 1), 0))
    return _pallas(body, name="odd_mid_bwd", grid=(nt,),
                   in_specs=[cur(0), nxt, cur(0), nxt, cur(1), cur(2), prv(1), prv(2), pl.BlockSpec((8, D), lambda i: (0, 0))],
                   out_specs=[pl.BlockSpec((tm, 3 * D), lambda i: (i, 0)), pl.BlockSpec((8, D), lambda i: (0, 0))],
                   out_shape=[_sds((S, 3 * D), BF16), _sds((8, D), F32)],
                   scratch_shapes=[pltpu.VMEM((len(shifts_w), tm + HALO_C, D), F32),
                                   pltpu.VMEM((len(shifts_d), tm + HALO_C, D), F32)],
                   compiler_params=_cp(("arbitrary",)))(dy, dy, z, z, z, z, z, z, cw)


def _loss_head(y, tgt):
    S, D = y.shape
    tm = TOK_TILE

    def body(y_ref, t_ref, dy_ref, l_ref):
        @pl.when(pl.program_id(0) == 0)
        def _():
            l_ref[...] = jnp.zeros_like(l_ref)

        e = y_ref[...] - t_ref[...]
        dy_ref[...] = e * (1.0 / D)
        l_ref[...] += jnp.sum(jnp.sum(e * e, axis=-1, keepdims=True), axis=0, keepdims=True) * (0.5 / D)

    row = pl.BlockSpec((tm, D), lambda i: (i, 0))
    return _pallas(body, name="loss_head", grid=(S // tm,), in_specs=[row, row],
                   out_specs=[row, pl.BlockSpec((1, 1), lambda i: (0, 0))],
                   out_shape=[_sds((S, D), F32), _sds((1, 1), F32)],
                   compiler_params=_cp(("arbitrary",)))(y, tgt)


def _pad_rows(a, rows):
    return jnp.pad(a, ((0, rows - a.shape[0]), (0, 0)))


def _local_step(x, tgt, W, need=lambda block, after: None, done=lambda block, block_grads: None):
    S, D = x.shape
    grads = {}
    saved = {}

    def gain_after(gain, token):
        return gain if token is None else gain + token

    def ffn_f(tag, l, xin):
        need((tag, l), xin)
        out, xn, G, U = _ffn_fwd(xin, W[tag + "_norm"][l:l + 1], W[tag + "_w_gate"][l], W[tag + "_w_up"][l],
                                 W[tag + "_w_down"][l])
        saved[(tag, l)] = (xin, xn, G, U)
        return out

    def ffn_b(tag, l, dout):
        xin, xn, G, U = saved[(tag, l)]
        keys = [(tag + "_w_gate", l), (tag + "_w_up", l), (tag + "_w_down", l)]
        *dws, dG, dU = _ffn_bwd_w(dout, xn, G, U, W[tag + "_w_down"][l])
        big = dict(zip(keys, dws))
        grads.update(big)
        token = done((tag, l), big)
        dx, dg = _norm_in_bwd([dG, dU], [W[tag + "_w_gate"][l], W[tag + "_w_up"][l]], xin,
                              gain_after(W[tag + "_norm"][l:l + 1], token), dout, w_rows=True)
        grads[(tag + "_norm", l)] = dg
        return dx

    x0a = ffn_f("ffn1", 0, x)
    need(("ev", 0), x0a)
    w_in = W["ev_w_in"]
    w_main, w_f = w_in[:, :2560], jnp.pad(w_in[:, 2560:], ((0, 0), (0, 120)))
    h0, z0, fl = _norm_proj(x0a, W["mix_norm"][0:1], w_main, w_f)
    cw_a = _pad_rows(W["ev_conv_w"], 32)
    a_act, a1 = _conv_a_fwd(z0, cw_a, W["ev_conv_b"], W["ev_conv_norm"])
    flb, Fc = _forget_scan(fl, jnp.pad(W["ev_b_f"], ((0, 0), (0, 120))))
    qw2, kw2 = jnp.tile(W["ev_q_norm"], (1, 2)), jnp.tile(W["ev_k_norm"], (1, 2))
    q_aug, k_aug, v_aug = _qkv_prep(z0, Fc, qw2, kw2)
    o_aug, q_lse = _fox_fwd(q_aug, k_aug, v_aug)
    w_out_e = W["ev_w_out"]
    w_out_o = jnp.pad(w_out_e[D_CONV:].reshape(N_HEADS, HEAD_DIM, D), ((0, 0), (0, AUG - HEAD_DIM), (0, 0)))
    x0b = _proj_res_heads(a_act, w_out_e[:D_CONV], o_aug, w_out_o, x0a)
    x0c = ffn_f("ffn2", 0, x0b)
    x1a = ffn_f("ffn1", 1, x0c)
    need(("od", 0), x1a)
    h1, z1 = _norm_proj(x1a, W["mix_norm"][1:2], W["od_w_in"])
    cw_c = _pad_rows(W["od_conv_w"], 8)
    y1 = _odd_mid_fwd(z1, cw_c)
    x1b = _proj_res([y1], [W["od_w_out"]], x1a)
    x1c = ffn_f("ffn2", 1, x1b)
    dy, loss = _loss_head(x1c, tgt)

    d = ffn_b("ffn2", 1, dy)
    dy1 = _matmul_nt(d, W["od_w_out"])
    grads[("od_w_out", 0)] = _matmul_tn(y1, d, D)[0]
    dz1, dcw_c = _odd_mid_bwd(dy1, z1, cw_c)
    grads[("od_conv_w", 0)] = dcw_c[:CONV_C_WIDTH]
    grads[("od_w_in", 0)] = _matmul_tn(h1, dz1, 3 * D // 4)
    token = done(("od", 0), {k: grads[k] for k in (("od_w_out", 0), ("od_w_in", 0))})
    d, dg = _norm_in_bwd([dz1[None]], [W["od_w_in"][None]], x1a, gain_after(W["mix_norm"][1:2], token), d)
    grads[("mix_norm", 1)] = dg
    d = ffn_b("ffn1", 1, d)
    d = ffn_b("ffn2", 0, d)
    dcat = _matmul_nt(d, w_out_e)
    grads[("ev_w_out", 0)] = jnp.concatenate([_matmul_tn(a_act, d, D)[0],
                                              _heads_tn(o_aug, d)[:, :HEAD_DIM].reshape(D_ATTN, D)], axis=0)
    duz, dcw_a, dcb, dcn = _conv_a_bwd(dcat, a1, z0, cw_a, W["ev_conv_norm"])
    grads[("ev_conv_w", 0)] = dcw_a[:CONV_A_WIDTH]
    grads[("ev_conv_b", 0)] = dcb
    grads[("ev_conv_norm", 0)] = dcn
    dq_a, dk_a, dv_a = _fox_bwd(q_lse, k_aug, v_aug, _do_prep(dcat, o_aug))
    dqf, dkf, dvf, dF, dqw, dkw = _qkv_bwd(dq_a, dk_a, dv_a, z0, qw2, kw2)
    grads[("ev_q_norm", 0)] = dqw[:, :HEAD_DIM] + dqw[:, HEAD_DIM:]
    grads[("ev_k_norm", 0)] = dkw[:, :HEAD_DIM] + dkw[:, HEAD_DIM:]
    dfl, dbf = _forget_scan_bwd(dF, flb)
    grads[("ev_b_f", 0)] = dbf[:, :N_HEADS]
    dz0 = jnp.concatenate([duz, dqf, dkf, dvf], axis=1)
    dflb = dfl.astype(BF16)
    gmain = _matmul_tn(h0, dz0, 640)
    gmain = gmain.transpose(1, 0, 2).reshape(D, 2560)
    gf = _matmul_tn(h0, dflb, 128)[0][:, :N_HEADS]
    grads[("ev_w_in", 0)] = jnp.concatenate([gmain, gf], axis=1)
    token = done(("ev", 0), {k: grads[k] for k in (("ev_w_out", 0), ("ev_w_in", 0))})
    d, dg = _norm_in_bwd([dz0[None], dflb[None]], [w_main[None], w_f[None]], x0a, gain_after(W["mix_norm"][0:1], token), d)
    grads[("mix_norm", 0)] = dg
    d = ffn_b("ffn1", 0, d)
    return loss, d, grads


def _place():
    x, y, c = lax.axis_index("x"), lax.axis_index("y"), lax.axis_index("c")
    chips = [(1 - x, y), (x, 1 - y), (1 - x, 1 - y)]
    return x, y, c, chips


def _remote(src, dst, send_sem, recv_sem, to):
    return pltpu.make_async_remote_copy(src_ref=src, dst_ref=dst, send_sem=send_sem, recv_sem=recv_sem,
                                        device_id=to, device_id_type=MESH)


HBM = pl.BlockSpec(memory_space=pltpu.HBM)
SEM = pl.BlockSpec(memory_space=pltpu.SEMAPHORE)
EFFECT = pltpu.SideEffectType.DATAFLOW_SIDE_EFFECTING


def _in_hbm(a):
    return pltpu.with_memory_space_constraint(a, pltpu.HBM)


def _ag_start(tag, bufs, with_taps):
    n = len(bufs)
    order = ([n - 1] + list(range(n - 1))) if with_taps else list(range(n))

    def body(*refs):
        send_sems, recv_sems = refs[n], refs[n + 1]
        outs, token = refs[n + 2:2 * n + 2], refs[2 * n + 2]
        x, y, c, chips = _place()
        me = 2 * x + y
        for a in order:
            if with_taps and a == n - 1:
                blk = outs[a].at[me]
            else:
                h = outs[a].shape[1] // 2
                blk = outs[a].at[me, pl.ds(c * h, h)]
            for jj, (px, py) in enumerate(chips):
                _remote(blk, blk, send_sems.at[3 * a + jj], recv_sems.at[3 * a + jj], (px, py, c)).start()
        token[...] = jnp.zeros_like(token)

    return _pallas(
        body, name=f"gather_start_{tag}",
        out_shape=[pltpu.SemaphoreType.DMA((3 * n,)), pltpu.SemaphoreType.DMA((3 * n,))]
        + [pltpu.HBM(b.shape, b.dtype) for b in bufs] + [_sds((8, 128), F32)],
        in_specs=[HBM] * n, out_specs=[SEM, SEM] + [HBM] * n + [pl.BlockSpec(memory_space=pltpu.VMEM)],
        input_output_aliases={a: 2 + a for a in range(n)},
        compiler_params=pltpu.CompilerParams(has_side_effects=EFFECT),
    )(*[_in_hbm(b) for b in bufs])


def _ag_mid(g, ici_send, ici_recv, bufs, idx, taps, n_big, after):
    n = len(bufs)
    arrs = list(bufs) + ([taps] if taps is not None else [])
    m = len(arrs)

    def body(*refs):
        ici_s, ici_r = refs[0], refs[1]
        d_send, d_recv = refs[m + 3], refs[m + 4]
        outs = refs[m + 5:]
        x, y, c, chips = _place()
        me = 2 * x + y
        for i in range(m):
            a = idx[i] if i < n else n_big
            for jj, (px, py) in enumerate(chips):
                k = 3 * a + jj
                if i < n:
                    h = outs[i].shape[1] // 2
                    mine, blk = outs[i].at[me, pl.ds(c * h, h)], outs[i].at[2 * px + py, pl.ds(c * h, h)]
                else:
                    mine, blk = outs[i].at[me], outs[i].at[2 * px + py]
                _remote(mine, mine, ici_s.at[k], ici_r.at[k], (px, py, c)).wait_send()
                _remote(blk, blk, ici_s.at[k], ici_r.at[k], (px, py, c)).wait_recv()
                if i < n:
                    _remote(blk, blk, d_send.at[3 * i + jj], d_recv.at[3 * i + jj], (x, y, 1 - c)).start()

    return _pallas(
        body, name=f"gather_pass_on_{g}",
        out_shape=[pltpu.SemaphoreType.DMA((3 * n,)), pltpu.SemaphoreType.DMA((3 * n,))] + [pltpu.HBM(b.shape, b.dtype) for b in arrs],
        in_specs=[SEM, SEM] + [HBM] * m + [ANY], out_specs=[SEM, SEM] + [HBM] * m,
        input_output_aliases={2 + i: 2 + i for i in range(m)},
        compiler_params=pltpu.CompilerParams(has_side_effects=EFFECT),
    )(ici_send, ici_recv, *arrs, after)


def _ag_wait(g, d_send, d_recv, arrs, n, after):
    m = len(arrs)

    def body(*refs):
        d_s, d_r = refs[0], refs[1]
        outs = refs[m + 3:]
        x, y, c, chips = _place()
        for i in range(n):
            h = outs[i].shape[1] // 2
            for jj, (px, py) in enumerate(chips):
                sent = outs[i].at[2 * px + py, pl.ds(c * h, h)]
                got = outs[i].at[2 * px + py, pl.ds((1 - c) * h, h)]
                _remote(sent, sent, d_s.at[3 * i + jj], d_r.at[3 * i + jj], (x, y, 1 - c)).wait_send()
                _remote(got, got, d_s.at[3 * i + jj], d_r.at[3 * i + jj], (x, y, 1 - c)).wait_recv()

    return _pallas(
        body, name=f"gather_wait_{g}", out_shape=[pltpu.HBM(b.shape, b.dtype) for b in arrs],
        in_specs=[SEM, SEM] + [HBM] * m + [ANY], out_specs=[HBM] * m,
        input_output_aliases={2 + i: i for i in range(m)},
        compiler_params=pltpu.CompilerParams(has_side_effects=EFFECT),
    )(d_send, d_recv, *arrs, after)


def _pair_start(g, gs, after):
    n = len(gs)
    zones = [lax.empty((4, a.shape[1] // 2, a.shape[2]), a.dtype) for a in gs]
    extra = [] if after is None else [after]

    def body(*refs):
        k0 = 2 * n + len(extra)
        send_sems, recv_sems = refs[k0], refs[k0 + 1]
        src, dst = refs[k0 + 2:k0 + 2 + n], refs[k0 + 2 + n:k0 + 2 + 2 * n]
        token = refs[k0 + 2 + 2 * n]
        x, y, c, _ = _place()
        for a in range(n):
            h = src[a].shape[1] // 2
            _remote(src[a].at[:, pl.ds((1 - c) * h, h)], dst[a], send_sems.at[a], recv_sems.at[a], (x, y, 1 - c)).start()
        token[...] = jnp.zeros_like(token)

    return _pallas(
        body, name=f"grad_pair_start_{g}",
        out_shape=[pltpu.SemaphoreType.DMA((n,)), pltpu.SemaphoreType.DMA((n,))]
        + [pltpu.HBM(a.shape, a.dtype) for a in gs + zones] + [_sds((8, 128), F32)],
        in_specs=[HBM] * (2 * n) + [ANY] * len(extra),
        out_specs=[SEM, SEM] + [HBM] * (2 * n) + [pl.BlockSpec(memory_space=pltpu.VMEM)],
        input_output_aliases={i: 2 + i for i in range(2 * n)},
        compiler_params=pltpu.CompilerParams(has_side_effects=EFFECT),
    )(*[_in_hbm(a) for a in gs + zones], *extra)


def _pair_wait(g, send, recv, gs, zones):
    n = len(gs)

    def body(*refs):
        s_ref, r_ref = refs[0], refs[1]
        outs = refs[2 + 2 * n:]
        src, dst = outs[:n], outs[n:]
        x, y, c, _ = _place()
        for a in range(n):
            h = src[a].shape[1] // 2
            _remote(src[a].at[:, pl.ds((1 - c) * h, h)], dst[a], s_ref.at[a], r_ref.at[a], (x, y, 1 - c)).wait()

    return _pallas(
        body, name=f"grad_pair_wait_{g}", out_shape=[pltpu.HBM(a.shape, a.dtype) for a in gs + zones],
        in_specs=[SEM, SEM] + [HBM] * (2 * n), out_specs=[HBM] * (2 * n),
        input_output_aliases={2 + i: i for i in range(2 * n)},
        compiler_params=pltpu.CompilerParams(has_side_effects=EFFECT),
    )(send, recv, *gs, *zones)


def _pair_add(gs, others, c_arr):
    n = len(gs)

    def body(c_ref, *refs):
        for g_ref, o_ref, out_ref in zip(refs[:n], refs[n:2 * n], refs[2 * n:]):
            out_ref[...] = (g_ref[...].astype(F32) + o_ref[...].astype(F32)).astype(BF16)

    half = lambda a: pl.BlockSpec((1, a.shape[1] // 2, a.shape[2]), lambda k, c_ref: (k, c_ref[0], 0))
    whole = lambda a: pl.BlockSpec((1,) + a.shape[1:], lambda k, c_ref: (k, 0, 0))
    grid_spec = pltpu.PrefetchScalarGridSpec(
        num_scalar_prefetch=1, grid=(4,), in_specs=[half(a) for a in gs] + [whole(o) for o in others],
        out_specs=[whole(o) for o in others])
    return _pallas(body, name="grad_pair_add", grid_spec=grid_spec, out_shape=[_sds(o.shape, BF16) for o in others],
                   compiler_params=_cp(("parallel",)))(c_arr, *gs, *others)


def _chip_start(g, ss):
    n = len(ss)
    zones = [lax.empty((3,) + s.shape[1:], s.dtype) for s in ss]

    def body(*refs):
        send_sems, recv_sems = refs[2 * n], refs[2 * n + 1]
        src, dst = refs[2 * n + 2:3 * n + 2], refs[3 * n + 2:4 * n + 2]
        token = refs[4 * n + 2]
        x, y, c, chips = _place()
        for a in range(n):
            for jj, (px, py) in enumerate(chips):
                k = 3 * a + jj
                _remote(src[a].at[2 * px + py], dst[a].at[jj], send_sems.at[k], recv_sems.at[k], (px, py, c)).start()
        token[...] = jnp.zeros_like(token)

    return _pallas(
        body, name=f"grad_chip_start_{g}",
        out_shape=[pltpu.SemaphoreType.DMA((3 * n,)), pltpu.SemaphoreType.DMA((3 * n,))]
        + [pltpu.HBM(a.shape, a.dtype) for a in ss + zones] + [_sds((8, 128), F32)],
        in_specs=[HBM] * (2 * n), out_specs=[SEM, SEM] + [HBM] * (2 * n) + [pl.BlockSpec(memory_space=pltpu.VMEM)],
        input_output_aliases={i: 2 + i for i in range(2 * n)},
        compiler_params=pltpu.CompilerParams(has_side_effects=EFFECT),
    )(*[_in_hbm(a) for a in ss + zones])


def _chip_wait(tag, sends, recvs, counts, ss, zones, after):
    nb, n = len(sends), len(ss)

    def body(*refs):
        s_refs, r_refs = refs[:nb], refs[nb:2 * nb]
        outs = refs[2 * nb + 2 * n + 1:]
        src, dst = outs[:n], outs[n:]
        x, y, c, chips = _place()
        a = 0
        for b in range(nb):
            for i in range(counts[b]):
                for jj, (px, py) in enumerate(chips):
                    k = 3 * i + jj
                    _remote(src[a].at[2 * px + py], dst[a].at[jj], s_refs[b].at[k], r_refs[b].at[k], (px, py, c)).wait()
                a += 1

    return _pallas(
        body, name=f"grad_chip_wait_{tag}", out_shape=[pltpu.HBM(a.shape, a.dtype) for a in ss + zones],
        in_specs=[SEM] * (2 * nb) + [HBM] * (2 * n) + [ANY], out_specs=[HBM] * (2 * n),
        input_output_aliases={2 * nb + i: i for i in range(2 * n)},
        compiler_params=pltpu.CompilerParams(has_side_effects=EFFECT),
    )(*sends, *recvs, *ss, *zones, after)


def _chip_sum(s, r, where, dest, l, L):
    _, h, C = s.shape
    tr = h // 2

    def body(k_ref, s_ref, r_ref, *rest):
        out_ref = rest[-1]
        acc = s_ref[0].astype(F32)
        for jj in range(3):
            acc = acc + r_ref[jj].astype(F32)
        out_ref[...] = acc

    in_specs = [pl.BlockSpec((1, tr, C), lambda i, k_ref: (k_ref[0], i, 0)), pl.BlockSpec((3, tr, C), lambda i, k_ref: (0, i, 0))]
    args = [where, s, r]
    alias = {}
    if dest is not None:
        in_specs.append(ANY)
        args.append(dest)
        alias = {3: 0}
    grid_spec = pltpu.PrefetchScalarGridSpec(
        num_scalar_prefetch=1, grid=(2,), in_specs=in_specs,
        out_specs=pl.BlockSpec((None, tr, C), lambda i, k_ref: (l, 2 * k_ref[1] + i, 0)))
    return _pallas(body, name="grad_chip_sum", grid_spec=grid_spec, out_shape=_sds((L, 2 * h, C), F32),
                   input_output_aliases=alias, compiler_params=_cp(("arbitrary",)))(*args)


def _share_start(tag, bufs, layout):
    n, n_buf = len(layout), len(bufs)

    def body(*refs):
        send_sems, recv_sems = refs[n_buf], refs[n_buf + 1]
        outs = refs[n_buf + 2:]
        x, y, c, _ = _place()
        for a, (o, l) in enumerate(layout):
            h = outs[o].shape[1] // 2
            blk = outs[o].at[l, pl.ds(c * h, h)]
            _remote(blk, blk, send_sems.at[a], recv_sems.at[a], (x, y, 1 - c)).start()

    return _pallas(
        body, name=f"grad_share_start_{tag}",
        out_shape=[pltpu.SemaphoreType.DMA((n,)), pltpu.SemaphoreType.DMA((n,))] + [pltpu.HBM(b.shape, b.dtype) for b in bufs],
        in_specs=[HBM] * n_buf, out_specs=[SEM, SEM] + [HBM] * n_buf, input_output_aliases={o: 2 + o for o in range(n_buf)},
        compiler_params=pltpu.CompilerParams(has_side_effects=EFFECT),
    )(*[_in_hbm(b) for b in bufs])


def _share_wait(tag, send, recv, bufs, layout, after):
    n_buf = len(bufs)

    def body(*refs):
        s_ref, r_ref = refs[0], refs[1]
        outs = refs[n_buf + 3:]
        x, y, c, _ = _place()
        for a, (o, l) in enumerate(layout):
            h = outs[o].shape[1] // 2
            mine, theirs = outs[o].at[l, pl.ds(c * h, h)], outs[o].at[l, pl.ds((1 - c) * h, h)]
            _remote(mine, mine, s_ref.at[a], r_ref.at[a], (x, y, 1 - c)).wait_send()
            _remote(theirs, theirs, s_ref.at[a], r_ref.at[a], (x, y, 1 - c)).wait_recv()

    return _pallas(
        body, name=f"grad_share_wait_{tag}", out_shape=[pltpu.HBM(b.shape, b.dtype) for b in bufs],
        in_specs=[SEM, SEM] + [HBM] * n_buf + [ANY], out_specs=[HBM] * n_buf,
        input_output_aliases={2 + o: o for o in range(n_buf)},
        compiler_params=pltpu.CompilerParams(has_side_effects=EFFECT),
    )(send, recv, *bufs, after)


def _small_all_reduce(packed, after):
    P, L = packed.shape

    def body(in_ref, after_ref, out_ref, slots, send_sems, recv_sems):
        x, y, c, _ = _place()
        me = 4 * x + 2 * y + c
        slots[me] = in_ref[...]
        cps = []
        for r in range(1, 8):
            px = 1 - x if r & 4 else x
            py = 1 - y if r & 2 else y
            pc = 1 - c if r & 1 else c
            cps.append(_remote(in_ref, slots.at[me], send_sems.at[r - 1], recv_sems.at[r - 1], (px, py, pc)))
        for cp in cps:
            cp.start()
        for r in range(1, 8):
            px = 1 - x if r & 4 else x
            py = 1 - y if r & 2 else y
            pc = 1 - c if r & 1 else c
            blk = slots.at[4 * px + 2 * py + pc]
            _remote(blk, blk, send_sems.at[r - 1], recv_sems.at[r - 1], (px, py, pc)).wait_recv()
        for cp in cps:
            cp.wait_send()
        acc = slots[0]
        for k in range(1, 8):
            acc = acc + slots[k]
        out_ref[...] = acc

    vm = pl.BlockSpec(memory_space=pltpu.VMEM)
    return _pallas(body, name="small_all_reduce", in_specs=[vm, ANY], out_specs=vm, out_shape=_sds((P, L), F32),
                   scratch_shapes=[pltpu.VMEM((8, P, L), F32), pltpu.SemaphoreType.DMA((7,)),
                                   pltpu.SemaphoreType.DMA((7,))])(packed, after)


def _adamw_math(w, g, m, v):
    m = ADAM_B1 * m + (1.0 - ADAM_B1) * g
    v = ADAM_B2 * v + (1.0 - ADAM_B2) * (g * g)
    m_hat = m / (1.0 - ADAM_B1 ** ADAM_STEP)
    v_hat = v / (1.0 - ADAM_B2 ** ADAM_STEP)
    delta = -ADAM_LR * (m_hat / (jnp.sqrt(v_hat) + ADAM_EPS) + ADAM_WD * w)
    return delta, m, v


def _adamw(w, g, m, v):
    shape = w.shape
    C = shape[-1]
    rows = math.prod(shape[:-1])
    tr = next(t for t in (512, 352, 256, 128, 64, 32, 16, 8, rows) if rows % t == 0)
    w2, g2, m2, v2 = (a.reshape(rows, C) for a in (w, g, m, v))

    def body(w_ref, g_ref, m_ref, v_ref, go_ref, d_ref, nm_ref, nv_ref):
        gv = g_ref[...]
        d, nm, nv = _adamw_math(w_ref[...], gv, m_ref[...], v_ref[...])
        go_ref[...] = gv
        d_ref[...] = d
        nm_ref[...] = nm
        nv_ref[...] = nv

    blk = pl.BlockSpec((tr, C), lambda i: (i, 0))
    outs = _pallas(body, name="adamw", grid=(rows // tr,), in_specs=[blk] * 4, out_specs=[blk] * 4,
                   out_shape=[_sds((rows, C), F32)] * 4, compiler_params=_cp(("parallel",)))(w2, g2, m2, v2)
    return tuple(o.reshape(shape) for o in outs)


WEIGHTS = ["ffn1_norm", "ffn1_w_gate", "ffn1_w_up", "ffn1_w_down", "mix_norm", "ffn2_norm", "ffn2_w_gate", "ffn2_w_up",
           "ffn2_w_down", "ev_w_in", "ev_b_f", "ev_conv_w", "ev_conv_b", "ev_conv_norm", "ev_q_norm", "ev_k_norm",
           "ev_w_out", "od_w_in", "od_conv_w", "od_w_out"]
BIG = ([("ffn1_w_gate", 0), ("ffn1_w_up", 0), ("ffn1_w_down", 0), ("ev_w_in", 0), ("ev_w_out", 0),
        ("ffn2_w_gate", 0), ("ffn2_w_up", 0), ("ffn2_w_down", 0)]
       + [("ffn1_w_gate", 1), ("ffn1_w_up", 1), ("ffn1_w_down", 1), ("od_w_in", 0), ("od_w_out", 0),
          ("ffn2_w_gate", 1), ("ffn2_w_up", 1), ("ffn2_w_down", 1)])
TRANSPOSED = ("ffn1_w_gate", "ffn1_w_up", "ffn2_w_gate", "ffn2_w_up")
SHARED_LAST = ("ffn1_w_gate", "ffn1_w_up", "ffn1_w_down", "ev_w_in", "ev_w_out")
BLOCKS = [("ffn1", 0), ("ev", 0), ("ffn2", 0), ("ffn1", 1), ("od", 0), ("ffn2", 1)]
BLOCK_OF = {(name, l): (name.split("_w_")[0], l) for name, l in BIG}
BIG_NAMES = ["ffn1_w_gate", "ffn1_w_up", "ffn1_w_down", "ffn2_w_gate", "ffn2_w_up", "ffn2_w_down",
             "ev_w_in", "ev_w_out", "od_w_in", "od_w_out"]
SMALL = [("ffn1_norm", 16), ("mix_norm", 16), ("ffn2_norm", 16), ("ev_b_f", 8), ("ev_conv_w", 128), ("ev_conv_b", 8),
         ("ev_conv_norm", 8), ("ev_q_norm", 8), ("ev_k_norm", 8), ("od_conv_w", 24)]


def _to_lanes(a, rows):
    flat = a.reshape(-1)
    return jnp.pad(flat, (0, rows * 128 - flat.shape[0])).reshape(rows, 128)


def kernel(x, ffn1_norm, ffn1_w_gate, ffn1_w_up, ffn1_w_down, mix_norm, ffn2_norm, ffn2_w_gate, ffn2_w_up, ffn2_w_down, ev_w_in, ev_b_f, ev_conv_w, ev_conv_b, ev_conv_norm, ev_q_norm, ev_k_norm, ev_w_out, od_w_in, od_conv_w, od_w_out, loss_target, m_ffn1_norm, m_ffn1_w_gate, m_ffn1_w_up, m_ffn1_w_down, m_mix_norm, m_ffn2_norm, m_ffn2_w_gate, m_ffn2_w_up, m_ffn2_w_down, m_ev_w_in, m_ev_b_f, m_ev_conv_w, m_ev_conv_b, m_ev_conv_norm, m_ev_q_norm, m_ev_k_norm, m_ev_w_out, m_od_w_in, m_od_conv_w, m_od_w_out, v_ffn1_norm, v_ffn1_w_gate, v_ffn1_w_up, v_ffn1_w_down, v_mix_norm, v_ffn2_norm, v_ffn2_w_gate, v_ffn2_w_up, v_ffn2_w_down, v_ev_w_in, v_ev_b_f, v_ev_conv_w, v_ev_conv_b, v_ev_conv_norm, v_ev_q_norm, v_ev_k_norm, v_ev_w_out, v_od_w_in, v_od_conv_w, v_od_w_out):
    P = dict(ffn1_norm=ffn1_norm, ffn1_w_gate=ffn1_w_gate, ffn1_w_up=ffn1_w_up, ffn1_w_down=ffn1_w_down, mix_norm=mix_norm,
             ffn2_norm=ffn2_norm, ffn2_w_gate=ffn2_w_gate, ffn2_w_up=ffn2_w_up, ffn2_w_down=ffn2_w_down, ev_w_in=ev_w_in,
             ev_b_f=ev_b_f, ev_conv_w=ev_conv_w, ev_conv_b=ev_conv_b, ev_conv_norm=ev_conv_norm, ev_q_norm=ev_q_norm,
             ev_k_norm=ev_k_norm, ev_w_out=ev_w_out, od_w_in=od_w_in, od_conv_w=od_conv_w, od_w_out=od_w_out)
    M = dict(zip(WEIGHTS, [m_ffn1_norm, m_ffn1_w_gate, m_ffn1_w_up, m_ffn1_w_down, m_mix_norm, m_ffn2_norm, m_ffn2_w_gate,
                           m_ffn2_w_up, m_ffn2_w_down, m_ev_w_in, m_ev_b_f, m_ev_conv_w, m_ev_conv_b, m_ev_conv_norm,
                           m_ev_q_norm, m_ev_k_norm, m_ev_w_out, m_od_w_in, m_od_conv_w, m_od_w_out]))
    V = dict(zip(WEIGHTS, [v_ffn1_norm, v_ffn1_w_gate, v_ffn1_w_up, v_ffn1_w_down, v_mix_norm, v_ffn2_norm, v_ffn2_w_gate,
                           v_ffn2_w_up, v_ffn2_w_down, v_ev_w_in, v_ev_b_f, v_ev_conv_w, v_ev_conv_b, v_ev_conv_norm,
                           v_ev_q_norm, v_ev_k_norm, v_ev_w_out, v_od_w_in, v_od_conv_w, v_od_w_out]))
    for name in TRANSPOSED:
        P[name], M[name], V[name] = (jnp.swapaxes(a, 1, 2) for a in (P[name], M[name], V[name]))
    S, D = x.shape[1], x.shape[2]
    chip = 2 * lax.axis_index("x") + lax.axis_index("y")
    core = lax.axis_index("c")

    def own_slot(shard):
        return lax.dynamic_update_slice(lax.empty((4,) + shard.shape, shard.dtype), shard[None], (chip, 0, 0))

    taps = jnp.concatenate([_to_lanes(_pad_rows(ev_conv_w[0], 32), 32), _to_lanes(_pad_rows(od_conv_w[0], 8), 16)], axis=0)
    first = [i for i, k in enumerate(BIG) if BLOCK_OF[k] in BLOCKS[:2]]
    rest = [i for i in range(len(BIG)) if i not in first]
    send0, recv0, *bufs0 = _ag_start("first", [own_slot(P[BIG[i][0]][BIG[i][1]].astype(BF16)) for i in first]
                                     + [own_slot(taps)], True)
    zero = bufs0.pop()[0, 0]
    send1, recv1, *bufs1 = _ag_start("rest", [own_slot((P[BIG[i][0]][BIG[i][1]] + zero).astype(BF16)) for i in rest], False)
    bufs1.pop()
    cols = lambda a: a.transpose(1, 0, 2).reshape(a.shape[1], 4 * a.shape[2])
    W = {k: P[k] for k in ("ffn1_norm", "mix_norm", "ffn2_norm", "ev_b_f", "ev_q_norm", "ev_k_norm")}
    W["ev_conv_b"], W["ev_conv_norm"] = ev_conv_b, ev_conv_norm
    for tag in ("ffn1", "ffn2"):
        for kind in ("_w_gate", "_w_up", "_w_down"):
            W[tag + kind] = [None, None]
    passing = {}

    def pass_on(g, after):
        idx = [i for i, k in enumerate(BIG) if BLOCK_OF[k] == BLOCKS[g]]
        keys = [BIG[i] for i in idx] + (["taps"] if BLOCKS[g] == ("ev", 0) else [])
        send, recv, bufs, members = (send0, recv0, bufs0, first) if g < 2 else (send1, recv1, bufs1, rest)
        local = [members.index(i) for i in idx]
        passing[g] = (keys, _ag_mid(g, send, recv, [bufs[i] for i in local], local,
                                    bufs0[-1] if BLOCKS[g] == ("ev", 0) else None, len(first), after))

    def need(block, after):
        g = BLOCKS.index(block)
        if g not in passing:
            pass_on(g, bufs1[0] if g == 0 else after)
        keys, (d_send, d_recv, *thru) = passing.pop(g)
        got = dict(zip(keys, _ag_wait(g, d_send, d_recv, thru, len(keys) - ("taps" in keys), after)))
        if 1 <= g < len(BLOCKS) - 1:
            pass_on(g + 1, after)
        for key, a in got.items():
            if key == "taps":
                continue
            name, l = key
            if name.startswith("ffn"):
                W[name][l] = a
            elif name.endswith("_w_in"):
                W[name] = cols(a)
            elif name.endswith("_w_out"):
                W[name] = a.reshape(4 * a.shape[1], D)
        if block == ("ev", 0):
            taps_all = got["taps"]
            W["ev_conv_w"] = cols(taps_all[:, :32].reshape(4, 32, 128))[:CONV_A_WIDTH]
            W["od_conv_w"] = cols(taps_all[:, 32:48].reshape(4, 8, 256))[:CONV_C_WIDTH]

    rows = lambda a: a.reshape(4, a.shape[0] // 4, a.shape[1])
    colsh = lambda a: a.reshape(a.shape[0], 4, a.shape[1] // 4).transpose(1, 0, 2)
    c_arr = core.reshape(1).astype(jnp.int32)
    where = jnp.stack([chip, core]).astype(jnp.int32)
    in_flight = []

    def done(block, block_grads):
        g = BLOCKS.index(block)
        keys = list(block_grads)
        gs = []
        for name, l in keys:
            a = block_grads[(name, l)]
            gs.append(colsh(a) if name == "ev_w_in" else rows(a) if name.endswith("_w_out") else a)
        for item in list(pairs):
            to_chips(item)
        send, recv, *rest = _pair_start(g, gs, chained.get("token"))
        n = len(keys)
        pairs.append((g, keys, send, recv, rest[:n], rest[n:2 * n]))
        if g == 0:
            to_chips(pairs[0])
        chained["token"] = rest[-1] if g else chained["token"]
        return chained["token"][0:1, 0:1]

    pairs, chained = [], {}

    def to_chips(item):
        pairs.remove(item)
        g, keys, send, recv, gs, zones = item
        n = len(keys)
        done_ = _pair_wait(g, send, recv, gs, zones)
        sums = list(_pair_add(list(done_[:n]), list(done_[n:]), c_arr))
        send2, recv2, *rest = _chip_start(g, sums)
        in_flight.append((keys, send2, recv2, rest[:n], rest[n:2 * n]))
        chained["token"] = rest[-1]

    loss, grad_x, grads = _local_step(x[0], loss_target[0], W, need, done)

    order = [k for keys, *_ in in_flight for k in keys]
    landed = _chip_wait("all", [f[1] for f in in_flight], [f[2] for f in in_flight], [len(f[0]) for f in in_flight],
                        [a for f in in_flight for a in f[3]], [a for f in in_flight for a in f[4]], grad_x)
    sums, recvd = landed[:len(order)], landed[len(order):]
    stacked, shares = {}, []
    for tag, names in (("a", [n for n in BIG_NAMES if n not in SHARED_LAST]), ("b", list(SHARED_LAST))):
        for (name, l), s, r in zip(order, sums, recvd):
            if name in names:
                stacked[name] = _chip_sum(s, r, where, stacked.get(name), l, P[name].shape[0])
        layout = [(names.index(name), l) for name, l in order if name in names]
        send, recv, *thru = _share_start(tag, [stacked[name] for name in names], layout)
        shares.append((tag, names, send, recv, thru, layout))

    def small_grad(name):
        if name.endswith("_norm") and name[:3] in ("ffn", "mix"):
            return jnp.concatenate([grads[(name, 0)], grads[(name, 1)]], axis=0)
        return grads[(name, 0)]

    packed = jnp.concatenate([_to_lanes(small_grad(name), r) for name, r in SMALL], axis=0)
    total = _small_all_reduce(packed, shares[-1][4][0])
    small_grads, at = {}, 0
    for name, r in SMALL:
        part = total[at:at + r].reshape(-1)
        at += r
        if name == "ev_conv_w":
            full_g = part[:CONV_A_WIDTH * D_CONV].reshape(CONV_A_WIDTH, D_CONV)
            small_grads[name] = lax.dynamic_slice_in_dim(full_g, chip * (D_CONV // 4), D_CONV // 4, axis=1)[None]
        elif name == "od_conv_w":
            full_g = part[:CONV_C_WIDTH * D].reshape(CONV_C_WIDTH, D)
            small_grads[name] = lax.dynamic_slice_in_dim(full_g, chip * (D // 4), D // 4, axis=1)[None]
        else:
            small_grads[name] = part[:math.prod(P[name].shape)].reshape(P[name].shape)

    results = {}

    def update(name, g):
        outs = _adamw(P[name], g, M[name], V[name])
        results[name] = tuple(jnp.swapaxes(a, 1, 2) for a in outs) if name in TRANSPOSED else outs

    for name, _ in SMALL:
        update(name, small_grads[name])
    after = results[SMALL[-1][0]][1]
    for tag, names, send, recv, thru, layout in shares:
        for name, g in zip(names, _share_wait(tag, send, recv, thru, layout, after)):
            update(name, g)
        after = results[names[-1]][1]
    loss_all = lax.psum(loss[0, 0], ("x", "y", "c"))
    return (loss_all, grad_x[None], *[results[name][k] for k in range(4) for name in WEIGHTS])
```

```python
import functools
import math

import jax
import jax.numpy as jnp
from jax import lax
from jax.experimental import pallas as pl
from jax.experimental.pallas import tpu as pltpu

F32, BF16 = jnp.float32, jnp.bfloat16
EPS = 1e-6
FFN_RES = 0.5
N_HEADS, HEAD_DIM = 8, 64
D_CONV = 512
D_ATTN = N_HEADS * HEAD_DIM
CONV_A_WIDTH, CONV_C_WIDTH = 31, 3
ADAM_LR, ADAM_B1, ADAM_B2, ADAM_EPS, ADAM_WD, ADAM_STEP = 0.001, 0.9, 0.999, 1e-08, 0.01, 10
MESH = pl.DeviceIdType.MESH
ANY = pl.BlockSpec(memory_space=pl.ANY)

TOK_TILE = 512
FFN_TILE = 512
DW_TILE = 1024
ATT_TILE = 1024
QKN_TILE = 2048
HALO_A, HALO_C = 32, 16
SUBLANES = 8
CONV_ROWS = 64
ODD_ROWS = 16
SCAN_BLK = 256
MIB = 2 ** 20


def _pallas(body, **kw):
    return pl.pallas_call(body, **kw)


def _cp(sem=None, vmem_mib=48):
    return pltpu.CompilerParams(dimension_semantics=sem, vmem_limit_bytes=vmem_mib * MIB)


def _dot(a, b):
    return jnp.dot(a, b, preferred_element_type=F32)


def _dot_nt(a, b):
    return lax.dot_general(a, b, (((1,), (1,)), ((), ())), preferred_element_type=F32)


def _dot_tn(a, b):
    return lax.dot_general(a, b, (((0,), (0,)), ((), ())), preferred_element_type=F32)


def _sds(shape, dtype):
    return jax.ShapeDtypeStruct(shape, dtype)


def _rms(x):
    return lax.rsqrt(jnp.mean(x * x, axis=-1, keepdims=True) + EPS)


def _rms_bwd(dy, x, g):
    r = _rms(x)
    xh = x * r
    dxh = dy * g
    dx = r * (dxh - xh * jnp.mean(dxh * xh, axis=-1, keepdims=True))
    return dx, xh


def _silu_grad(z):
    s = jax.nn.sigmoid(z)
    return s * (1.0 + z * (1.0 - s))


def _ffn_fwd(x, g, wg, wu, wd):
    S, D = x.shape
    nc, Fs, _ = wd.shape
    tm = min(FFN_TILE, S)
    per = nc
    steps = nc // per

    def body(x_ref, g_ref, wg_ref, wu_ref, wd_ref, out_ref, xn_ref, G_ref, U_ref, acc_ref):
        j = pl.program_id(1)

        @pl.when(j == 0)
        def _():
            xv = x_ref[...]
            xn_ref[...] = (xv * _rms(xv) * g_ref[...]).astype(BF16)
            acc_ref[...] = jnp.zeros_like(acc_ref)

        xn = xn_ref[...]
        part = None
        for k in range(per):
            G = _dot_nt(xn, wg_ref[k])
            U = _dot_nt(xn, wu_ref[k])
            G_ref[k] = G.astype(BF16)
            U_ref[k] = U.astype(BF16)
            term = _dot((G * jax.nn.sigmoid(G) * U).astype(BF16), wd_ref[k])
            part = term if part is None else part + term
        acc_ref[...] += part

        @pl.when(j == steps - 1)
        def _():
            out_ref[...] = x_ref[...] + FFN_RES * acc_ref[...]

    row = pl.BlockSpec((tm, D), lambda i, j: (i, 0))
    wblk = pl.BlockSpec((per, Fs, D), lambda i, j: (j, 0, 0), pipeline_mode=pl.Buffered(1))
    hid = pl.BlockSpec((per, tm, Fs), lambda i, j: (j, i, 0))
    return _pallas(
        body, name="ffn_fwd", grid=(S // tm, steps),
        in_specs=[row, pl.BlockSpec((1, D), lambda i, j: (0, 0)), wblk, wblk, wblk],
        out_specs=[row, row, hid, hid],
        out_shape=[_sds((S, D), F32), _sds((S, D), BF16), _sds((nc, S, Fs), BF16), _sds((nc, S, Fs), BF16)],
        scratch_shapes=[pltpu.VMEM((tm, D), F32)],
        compiler_params=_cp(("parallel", "arbitrary"), 56),
    )(x, g, wg, wu, wd)


def _ffn_bwd_w(dout, xn, G, U, wd):
    S, D = dout.shape
    nc, _, Fs = G.shape
    tm = min(DW_TILE, S)
    nt = S // tm
    sub = min(TOK_TILE, tm)

    def body(do_ref, xn_ref, G_ref, U_ref, wd_ref, dwg_ref, dwu_ref, dwd_ref, dG_ref, dU_ref, ag, au, ad, do_s, H_s):
        i = pl.program_id(1)

        @pl.when(i == 0)
        def _():
            ag[...] = jnp.zeros_like(ag)
            au[...] = jnp.zeros_like(au)
            ad[...] = jnp.zeros_like(ad)

        for r in range(0, tm, sub):
            rows = pl.ds(r, sub)
            do = (FFN_RES * do_ref[rows, :]).astype(BF16)
            do_s[rows, :] = do
            Gv = G_ref[0, rows, :].astype(F32)
            Uv = U_ref[0, rows, :].astype(F32)
            dH = _dot_nt(do, wd_ref[0])
            sg = jax.nn.sigmoid(Gv)
            act = Gv * sg
            H_s[rows, :] = (act * Uv).astype(BF16)
            dU_ref[0, rows, :] = (dH * act).astype(BF16)
            dG_ref[0, rows, :] = (dH * Uv * (sg * (1.0 + Gv * (1.0 - sg)))).astype(BF16)
        xnv = xn_ref[...]
        ag[...] += _dot_tn(dG_ref[0], xnv)
        au[...] += _dot_tn(dU_ref[0], xnv)
        ad[...] += _dot_tn(H_s[...], do_s[...])

        @pl.when(i == nt - 1)
        def _():
            dwg_ref[0] = ag[...].astype(BF16)
            dwu_ref[0] = au[...].astype(BF16)
            dwd_ref[0] = ad[...].astype(BF16)

    row = pl.BlockSpec((tm, D), lambda j, i: (i, 0))
    hid = pl.BlockSpec((1, tm, Fs), lambda j, i: (j, i, 0))
    wrow = pl.BlockSpec((1, Fs, D), lambda j, i: (j, 0, 0))
    return _pallas(
        body, name="ffn_bwd_w", grid=(nc, nt),
        in_specs=[row, row, hid, hid, wrow],
        out_specs=[wrow, wrow, wrow, hid, hid],
        out_shape=[_sds((nc, Fs, D), BF16)] * 3 + [_sds((nc, S, Fs), BF16)] * 2,
        scratch_shapes=[pltpu.VMEM((Fs, D), F32)] * 3 + [pltpu.VMEM((tm, D), BF16), pltpu.VMEM((tm, Fs), BF16)],
        compiler_params=_cp(("parallel", "arbitrary"), 56),
    )(dout, xn, G, U, wd)


def _norm_in_bwd(dzs, ws, x, g, dres, w_rows=False):
    S, D = x.shape
    nc = dzs[0].shape[0]
    n = len(dzs)
    tm = TOK_TILE
    per = nc
    steps = nc // per

    def body(*refs):
        dz_refs, w_refs = refs[:n], refs[n:2 * n]
        x_ref, g_ref, dres_ref, dx_ref, dg_ref, acc_ref = refs[2 * n:]
        i, j = pl.program_id(0), pl.program_id(1)

        @pl.when(j == 0)
        def _():
            acc_ref[...] = jnp.zeros_like(acc_ref)

        @pl.when((i == 0) & (j == 0))
        def _():
            dg_ref[...] = jnp.zeros_like(dg_ref)

        part = None
        for dz_ref, w_ref in zip(dz_refs, w_refs):
            for k in range(per):
                term = _dot(dz_ref[k], w_ref[k]) if w_rows else _dot_nt(dz_ref[k], w_ref[k])
                part = term if part is None else part + term
        acc_ref[...] += part

        @pl.when(j == steps - 1)
        def _():
            dxn = acc_ref[...]
            dx, xh = _rms_bwd(dxn, x_ref[...], g_ref[...])
            dx_ref[...] = dx + dres_ref[...]
            dg_ref[...] += jnp.sum(dxn * xh, axis=0, keepdims=True)

    row = pl.BlockSpec((tm, D), lambda i, j: (i, 0))
    one = pl.BlockSpec((1, D), lambda i, j: (0, 0))
    in_specs = [pl.BlockSpec((per, tm, dz.shape[2]), lambda i, j: (j, i, 0)) for dz in dzs]
    in_specs += [pl.BlockSpec((per,) + w.shape[1:], lambda i, j: (j, 0, 0)) for w in ws]
    return _pallas(
        body, name="norm_in_bwd", grid=(S // tm, steps),
        in_specs=in_specs + [row, one, row], out_specs=[row, one],
        out_shape=[_sds((S, D), F32), _sds((1, D), F32)],
        scratch_shapes=[pltpu.VMEM((tm, D), F32)],
        compiler_params=_cp(("arbitrary", "arbitrary")),
    )(*dzs, *ws, x, g, dres)


def _norm_proj(x, g, w, w2=None):
    S, D = x.shape
    N = w.shape[1]
    tm = TOK_TILE

    def body(*refs):
        if w2 is None:
            x_ref, g_ref, w_ref, h_ref, z_ref = refs
        else:
            x_ref, g_ref, w_ref, w2_ref, h_ref, z_ref, z2_ref = refs
        xv = x_ref[...]
        h = (xv * _rms(xv) * g_ref[...]).astype(BF16)
        h_ref[...] = h
        z_ref[...] = _dot(h, w_ref[...]).astype(BF16)
        if w2 is not None:
            z2_ref[...] = _dot(h, w2_ref[...])

    row = pl.BlockSpec((tm, D), lambda i: (i, 0))
    in_specs = [row, pl.BlockSpec((1, D), lambda i: (0, 0)), pl.BlockSpec((D, N), lambda i: (0, 0))]
    out_specs = [row, pl.BlockSpec((tm, N), lambda i: (i, 0))]
    out_shape = [_sds((S, D), BF16), _sds((S, N), BF16)]
    args = [x, g, w]
    if w2 is not None:
        N2 = w2.shape[1]
        in_specs.append(pl.BlockSpec((D, N2), lambda i: (0, 0)))
        out_specs.append(pl.BlockSpec((tm, N2), lambda i: (i, 0)))
        out_shape.append(_sds((S, N2), F32))
        args.append(w2)
    return _pallas(body, name="norm_proj", grid=(S // tm,), in_specs=in_specs, out_specs=out_specs,
                   out_shape=out_shape, compiler_params=_cp(("parallel",)))(*args)


def _proj_res(acts, ws, res):
    S, D = res.shape
    n = len(acts)
    tm = TOK_TILE

    def body(*refs):
        a_refs, w_refs = refs[:n], refs[n:2 * n]
        res_ref, out_ref = refs[2 * n:]
        acc = res_ref[...]
        for a_ref, w_ref in zip(a_refs, w_refs):
            acc = acc + _dot(a_ref[...], w_ref[...])
        out_ref[...] = acc

    row = pl.BlockSpec((tm, D), lambda i: (i, 0))
    in_specs = [pl.BlockSpec((tm, a.shape[1]), lambda i: (i, 0)) for a in acts]
    in_specs += [pl.BlockSpec(w.shape, lambda i: (0, 0)) for w in ws]
    return _pallas(body, name="proj_res", grid=(S // tm,), in_specs=in_specs + [row], out_specs=row,
                   out_shape=_sds((S, D), F32), compiler_params=_cp(("parallel",)))(*acts, *ws, res)


def _matmul_nt(a, w, after=None):
    S, K = a.shape
    M = w.shape[0]
    tm = TOK_TILE

    def body(a_ref, w_ref, *rest):
        rest[-1][...] = _dot_nt(a_ref[...].astype(BF16), w_ref[...])

    extra = [] if after is None else [after]
    return _pallas(body, name="matmul_nt", grid=(S // tm,),
                   in_specs=[pl.BlockSpec((tm, K), lambda i: (i, 0)), pl.BlockSpec((M, K), lambda i: (0, 0))] + [ANY] * len(extra),
                   out_specs=pl.BlockSpec((tm, M), lambda i: (i, 0)), out_shape=_sds((S, M), F32),
                   compiler_params=_cp(("parallel",)))(a, w, *extra)


def _matmul_tn(a, b, tn):
    S, M = a.shape
    N = b.shape[1]
    tm = min(DW_TILE, S)
    nt = S // tm

    def body(a_ref, b_ref, o_ref, acc_ref):
        i = pl.program_id(1)

        @pl.when(i == 0)
        def _():
            acc_ref[...] = jnp.zeros_like(acc_ref)

        acc_ref[...] += _dot_tn(a_ref[...].astype(BF16), b_ref[...].astype(BF16))

        @pl.when(i == nt - 1)
        def _():
            o_ref[0] = acc_ref[...].astype(BF16)

    return _pallas(body, name="matmul_tn", grid=(N // tn, nt),
                   in_specs=[pl.BlockSpec((tm, M), lambda j, i: (i, 0)), pl.BlockSpec((tm, tn), lambda j, i: (i, j))],
                   out_specs=pl.BlockSpec((1, M, tn), lambda j, i: (j, 0, 0)), out_shape=_sds((N // tn, M, tn), BF16),
                   scratch_shapes=[pltpu.VMEM((M, tn), F32)],
                   compiler_params=_cp(("parallel", "arbitrary")))(a, b)


ALL_SHIFTS = tuple(range(SUBLANES))


def _fill_shifts(win, rows, shifts=ALL_SHIFTS):
    for i, b in enumerate(shifts):
        if b:
            win[i, pl.ds(0, rows - SUBLANES), :] = win[0, pl.ds(b, rows - SUBLANES), :]


def _tap(win, offset, n, base=0, shifts=ALL_SHIFTS):
    start = base + (offset - offset % SUBLANES)
    if not isinstance(start, int):
        start = pl.multiple_of(start, SUBLANES)
    return win[shifts.index(offset % SUBLANES), pl.ds(start, n), :]


def _conv_a_fwd(z, cw, cb, cn):
    S = z.shape[0]
    C = D_CONV
    tm = TOK_TILE
    hb = tm // HALO_A

    def body(u_ref, gt_ref, up_ref, gp_ref, cw_ref, cb_ref, cn_ref, a_ref, a1_ref, win):
        i = pl.program_id(0)
        prev = up_ref[...].astype(F32) * jax.nn.sigmoid(gp_ref[...].astype(F32))
        win[0, pl.ds(0, HALO_A), :] = jnp.where(i == 0, 0.0, prev)
        win[0, pl.ds(HALO_A, tm), :] = u_ref[...].astype(F32) * jax.nn.sigmoid(gt_ref[...].astype(F32))
        _fill_shifts(win, tm + HALO_A)

        acc = jnp.zeros((tm, C), F32)
        for k in range(CONV_A_WIDTH):
            acc = acc + cw_ref[k:k + 1, :] * _tap(win, HALO_A - (CONV_A_WIDTH - 1) + k, tm)
        a1 = acc + cb_ref[...]
        a1_ref[...] = a1
        a2 = a1 * _rms(a1) * cn_ref[...]
        a_ref[...] = (a2 * jax.nn.sigmoid(a2)).astype(BF16)

    cur = lambda c: pl.BlockSpec((tm, C), lambda i, c=c: (i, c))
    prv = lambda c: pl.BlockSpec((HALO_A, C), lambda i, c=c: (jnp.maximum(i * hb - 1, 0), c))
    vec = pl.BlockSpec((1, C), lambda i: (0, 0))
    return _pallas(body, name="conv_a_fwd", grid=(S // tm,),
                   in_specs=[cur(0), cur(1), prv(0), prv(1), pl.BlockSpec((32, C), lambda i: (0, 0)), vec, vec],
                   out_specs=[pl.BlockSpec((tm, C), lambda i: (i, 0)), pl.BlockSpec((tm, C), lambda i: (i, 0))],
                   out_shape=[_sds((S, C), BF16), _sds((S, C), F32)],
                   scratch_shapes=[pltpu.VMEM((SUBLANES, tm + HALO_A, C), F32)],
                   compiler_params=_cp(("parallel",)))(z, z, z, z, cw, cb, cn)


def _conv_a_bwd(da, a1, z, cw, cn):
    S = z.shape[0]
    C = D_CONV
    tm = TOK_TILE
    hb = tm // HALO_A
    nt = S // tm
    W = CONV_A_WIDTH

    def body(da_ref, a1_ref, dan_ref, a1n_ref, u_ref, gt_ref, up_ref, gp_ref, cw_ref, cn_ref,
             duz_ref, dcw_ref, dcb_ref, dcn_ref, win, dwin):
        i = pl.program_id(0)

        @pl.when(i == 0)
        def _():
            dcw_ref[...] = jnp.zeros_like(dcw_ref)
            dcb_ref[...] = jnp.zeros_like(dcb_ref)
            dcn_ref[...] = jnp.zeros_like(dcn_ref)

        cnv = cn_ref[...]

        def da1_of(dav, a1v):
            a2 = a1v * _rms(a1v) * cnv
            da2 = dav * _silu_grad(a2)
            dx, xh = _rms_bwd(da2, a1v, cnv)
            return dx, da2 * xh

        da1, dcn_t = da1_of(da_ref[...], a1_ref[...])
        da1n, _ = da1_of(dan_ref[...], a1n_ref[...])
        dwin[0, pl.ds(0, tm), :] = da1
        dwin[0, pl.ds(tm, HALO_A), :] = jnp.where(i == nt - 1, 0.0, da1n)
        _fill_shifts(dwin, tm + HALO_A)
        dcb_ref[...] += jnp.sum(da1, axis=0, keepdims=True)
        dcn_ref[...] += jnp.sum(dcn_t, axis=0, keepdims=True)

        prev = up_ref[...].astype(F32) * jax.nn.sigmoid(gp_ref[...].astype(F32))
        win[0, pl.ds(0, HALO_A), :] = jnp.where(i == 0, 0.0, prev)
        win[0, pl.ds(HALO_A, tm), :] = u_ref[...].astype(F32) * jax.nn.sigmoid(gt_ref[...].astype(F32))
        _fill_shifts(win, tm + HALO_A)

        def rows_block(rb, carry):
            r0 = pl.multiple_of(rb * CONV_ROWS, CONV_ROWS)
            rows = pl.ds(r0, CONV_ROWS)
            da1_b = dwin[0, rows, :]
            da0 = jnp.zeros((CONV_ROWS, C), F32)
            for k in range(W):
                da0 = da0 + cw_ref[k:k + 1, :] * _tap(dwin, W - 1 - k, CONV_ROWS, r0)
                dcw_ref[k:k + 1, :] += jnp.sum(da1_b * _tap(win, HALO_A - (W - 1) + k, CONV_ROWS, r0), axis=0, keepdims=True)
            u = u_ref[rows, :].astype(F32)
            sg = jax.nn.sigmoid(gt_ref[rows, :].astype(F32))
            duz_ref[rows, 0:C] = (da0 * sg).astype(BF16)
            duz_ref[rows, C:2 * C] = (da0 * u * sg * (1.0 - sg)).astype(BF16)
            return carry

        lax.fori_loop(0, tm // CONV_ROWS, rows_block, 0)

    cur = lambda c: pl.BlockSpec((tm, C), lambda i, c=c: (i, c))
    prv = lambda c: pl.BlockSpec((HALO_A, C), lambda i, c=c: (jnp.maximum(i * hb - 1, 0), c))
    nxt = pl.BlockSpec((HALO_A, C), lambda i: (jnp.minimum((i + 1) * hb, S // HALO_A - 1), 0))
    vec = pl.BlockSpec((1, C), lambda i: (0, 0))
    return _pallas(body, name="conv_a_bwd", grid=(nt,),
                   in_specs=[cur(0), cur(0), nxt, nxt, cur(0), cur(1), prv(0), prv(1),
                             pl.BlockSpec((32, C), lambda i: (0, 0)), vec],
                   out_specs=[pl.BlockSpec((tm, 2 * C), lambda i: (i, 0)), pl.BlockSpec((32, C), lambda i: (0, 0)), vec, vec],
                   out_shape=[_sds((S, 2 * C), BF16), _sds((32, C), F32), _sds((1, C), F32), _sds((1, C), F32)],
                   scratch_shapes=[pltpu.VMEM((SUBLANES, tm + HALO_A, C), F32)] * 2,
                   compiler_params=_cp(("arbitrary",)))(da, a1, da, a1, z, z, z, z, cw, cn)


def _forget_scan(fl, bf):
    S, L = fl.shape
    B = SCAN_BLK

    def body(fl_ref, bf_ref, flb_ref, F_ref):
        tri = (lax.broadcasted_iota(jnp.int32, (B, B), 0) >= lax.broadcasted_iota(jnp.int32, (B, B), 1)).astype(F32)

        def step(c, carry):
            rows = pl.ds(pl.multiple_of(c * B, B), B)
            v = fl_ref[rows, :] + bf_ref[...]
            flb_ref[rows, :] = v
            lf = jnp.minimum(v, 0.0) - jnp.log1p(jnp.exp(-jnp.abs(v)))
            cs = jnp.dot(tri, lf, precision=lax.Precision.HIGHEST, preferred_element_type=F32) + carry
            F_ref[rows, :] = cs
            return cs[B - 1:B, :]

        lax.fori_loop(0, S // B, step, jnp.zeros((1, L), F32))

    return _pallas(body, name="forget_scan", out_shape=[_sds((S, L), F32), _sds((S, L), F32)],
                   compiler_params=_cp())(fl, bf)


def _forget_scan_bwd(dF, flb):
    S, L = dF.shape
    B = SCAN_BLK
    nb = S // B

    def body(dF_ref, flb_ref, dfl_ref, db_ref):
        tri = (lax.broadcasted_iota(jnp.int32, (B, B), 0) <= lax.broadcasted_iota(jnp.int32, (B, B), 1)).astype(F32)

        def step(t, carry):
            carry_cs, db = carry
            rows = pl.ds(pl.multiple_of((nb - 1 - t) * B, B), B)
            cs = jnp.dot(tri, dF_ref[rows, :], precision=lax.Precision.HIGHEST, preferred_element_type=F32) + carry_cs
            dfl = cs * jax.nn.sigmoid(-flb_ref[rows, :])
            dfl_ref[rows, :] = dfl
            return cs[0:1, :], db + jnp.sum(dfl, axis=0, keepdims=True)

        _, db = lax.fori_loop(0, nb, step, (jnp.zeros((1, L), F32), jnp.zeros((1, L), F32)))
        db_ref[...] = db

    return _pallas(body, name="forget_scan_bwd", out_shape=[_sds((S, L), F32), _sds((1, L), F32)],
                   compiler_params=_cp())(dF, flb)


NEG = -1e30


def _causal_mask(t):
    return lax.broadcasted_iota(jnp.int32, (t, t), 0) >= lax.broadcasted_iota(jnp.int32, (t, t), 1)


AUG = 128
C_F, C_ONE, C_LSE = 64, 67, 70


def _split3(f):
    a = f.astype(BF16).astype(F32)
    r = f - a
    b = r.astype(BF16).astype(F32)
    return a, b, r - b


def _put3(lane, base, parts, other):
    out = other
    for k, p in enumerate(parts):
        out = jnp.where(lane == base + k, p, out)
    return out


def _ones3(lane, base):
    return (lane >= base) & (lane < base + 3)


def _lane_ids(rows):
    return lax.broadcasted_iota(jnp.int32, (rows, AUG), 1)


def _pair_rms(x, lo):
    sq = x * x
    ms_a = jnp.sum(jnp.where(lo, sq, 0.0), axis=-1, keepdims=True) * (1.0 / HEAD_DIM)
    ms_b = jnp.sum(jnp.where(lo, 0.0, sq), axis=-1, keepdims=True) * (1.0 / HEAD_DIM)
    return jnp.where(lo, lax.rsqrt(ms_a + EPS), lax.rsqrt(ms_b + EPS))


def _qkv_prep(z, Fc, qw, kw):
    S = z.shape[0]
    tp = min(QKN_TILE, S)
    scale = 1.0 / math.sqrt(HEAD_DIM)

    def body(zq_ref, zk_ref, zv_ref, F_ref, qw_ref, kw_ref, q_ref, k_ref, v_ref):
        j = pl.program_id(0)
        lane = _lane_ids(tp)
        lo = lane < HEAD_DIM
        Fv = F_ref[...]
        xq = zq_ref[...].astype(F32)
        xk = zk_ref[...].astype(F32)
        qn = xq * _pair_rms(xq, lo) * qw_ref[...] * scale
        kn = xk * _pair_rms(xk, lo) * kw_ref[...]
        vv = zv_ref[...].astype(F32)
        for half in range(2):
            take = (lambda a: a) if half == 0 else (lambda a: pltpu.roll(a, HEAD_DIM, 1))
            fp = _split3(jnp.sum(jnp.where(lane == 2 * j + half, Fv, 0.0), axis=-1, keepdims=True))
            qx = _put3(lane, C_F, fp, jnp.where(_ones3(lane, C_ONE), 1.0, 0.0))
            kx = _put3(lane, C_ONE, [-p for p in fp], jnp.where(_ones3(lane, C_F) | _ones3(lane, C_LSE), 1.0, 0.0))
            vx = jnp.where(_ones3(lane, C_F), 1.0, 0.0)
            q_ref[half] = jnp.where(lo, take(qn), qx).astype(BF16)
            k_ref[half] = jnp.where(lo, take(kn), kx).astype(BF16)
            v_ref[half] = jnp.where(lo, take(vv), vx).astype(BF16)

    col = lambda c0: pl.BlockSpec((tp, AUG), lambda j, i, c0=c0: (i, c0 + j))
    vec = pl.BlockSpec((1, AUG), lambda j, i: (0, 0))
    out = pl.BlockSpec((2, tp, AUG), lambda j, i: (j, i, 0))
    return _pallas(body, name="qkv_prep", grid=(N_HEADS // 2, S // tp),
                   in_specs=[col(8), col(12), col(16), pl.BlockSpec((tp, AUG), lambda j, i: (i, 0)), vec, vec],
                   out_specs=[out, out, out], out_shape=[_sds((N_HEADS, S, AUG), BF16)] * 3,
                   compiler_params=_cp(("parallel", "parallel")))(z, z, z, Fc, qw, kw)


def _fox_fwd(q_aug, k_aug, v_aug):
    H, S, A = q_aug.shape
    t = ATT_TILE
    nq = S // t

    def body(q_ref, k_ref, v_ref, o_ref, q2_ref):
        i = pl.program_id(1)
        q = q_ref[0]

        def tile(j, carry, diag):
            m, acc = carry
            rows = pl.ds(pl.multiple_of(j * t, t), t)
            s = _dot_nt(q, k_ref[0, rows, :])
            if diag:
                s = jnp.where(_causal_mask(t), s, NEG)
            m_new = jnp.maximum(m, jnp.max(s, axis=-1, keepdims=True))
            p = jnp.exp(s - m_new)
            acc = jnp.exp(m - m_new) * acc + _dot(p.astype(BF16), v_ref[0, rows, :])
            return m_new, acc

        init = (jnp.full((t, 1), NEG, F32), jnp.zeros((t, A), F32))
        carry = lax.fori_loop(0, i, lambda j, c: tile(j, c, False), init)
        m, acc = tile(i, carry, True)
        lane = _lane_ids(t)
        l = jnp.sum(jnp.where(lane == C_F, acc, 0.0), axis=-1, keepdims=True)
        o_ref[0] = (acc / l).astype(BF16)
        lse = m + jnp.log(l)
        q2_ref[0] = (q.astype(F32) + _put3(lane, C_LSE, [-p for p in _split3(lse)], 0.0)).astype(BF16)

    qblk = pl.BlockSpec((1, t, A), lambda h, i: (h, i, 0))
    full = pl.BlockSpec((1, S, A), lambda h, i: (h, 0, 0))
    return _pallas(body, name="fox_fwd", grid=(H, nq), in_specs=[qblk, full, full], out_specs=[qblk, qblk],
                   out_shape=[_sds((H, S, A), BF16)] * 2, compiler_params=_cp(("parallel", "parallel")))(q_aug, k_aug, v_aug)


def _do_prep(dcat, o_aug):
    S = dcat.shape[0]
    tp = min(QKN_TILE, S)

    def body(d_ref, o_ref, out_ref):
        lane = _lane_ids(tp)
        lo = lane < HEAD_DIM
        x = d_ref[...]
        for half in range(2):
            d = jnp.where(lo, x if half == 0 else pltpu.roll(x, HEAD_DIM, 1), 0.0)
            delta = jnp.sum(d * o_ref[half].astype(F32), axis=-1, keepdims=True)
            out_ref[half] = jnp.where(lo, d, _put3(lane, C_F, [-p for p in _split3(delta)], 0.0)).astype(BF16)

    pair = pl.BlockSpec((2, tp, AUG), lambda j, i: (j, i, 0))
    return _pallas(body, name="do_prep", grid=(N_HEADS // 2, S // tp),
                   in_specs=[pl.BlockSpec((tp, AUG), lambda j, i: (i, D_CONV // AUG + j)), pair], out_specs=pair,
                   out_shape=_sds((N_HEADS, S, AUG), BF16), compiler_params=_cp(("parallel", "parallel")))(dcat, o_aug)


def _fox_bwd(q2, k_aug, v_aug, do_aug):
    H, S, A = q2.shape
    t = ATT_TILE
    nq = S // t

    def body(q_ref, k_ref, v_ref, do_ref, dq_ref, dk_ref, dv_ref):
        j = pl.program_id(1)

        @pl.when(j == 0)
        def _():
            dq_ref[...] = jnp.zeros_like(dq_ref)

        k = k_ref[0]
        vv = v_ref[0]

        def tile(i, carry, diag):
            dk, dv = carry
            rows = pl.ds(pl.multiple_of(i * t, t), t)
            q = q_ref[0, rows, :]
            dov = do_ref[0, rows, :]
            s = _dot_nt(q, k)
            if diag:
                s = jnp.where(_causal_mask(t), s, NEG)
            p = jnp.exp(s)
            dv = dv + _dot_tn(p.astype(BF16), dov)
            dsb = (p * _dot_nt(dov, vv)).astype(BF16)
            dq_ref[0, rows, :] += _dot(dsb, k)
            dk = dk + _dot_tn(dsb, q)
            return dk, dv

        init = (jnp.zeros((t, A), F32), jnp.zeros((t, A), F32))
        carry = tile(j, init, True)
        dk, dv = lax.fori_loop(j + 1, nq, lambda i, c: tile(i, c, False), carry)
        dk_ref[0] = dk
        dv_ref[0] = dv

    full = pl.BlockSpec((1, S, A), lambda h, j: (h, 0, 0))
    kblk = pl.BlockSpec((1, t, A), lambda h, j: (h, j, 0))
    return _pallas(body, name="fox_bwd", grid=(H, nq), in_specs=[full, kblk, kblk, full], out_specs=[full, kblk, kblk],
                   out_shape=[_sds((H, S, A), F32)] * 3,
                   compiler_params=_cp(("parallel", "arbitrary")))(q2, k_aug, v_aug, do_aug)


def _qkv_bwd(dq, dk, dv, z, qw, kw):
    S = z.shape[0]
    tp = min(QKN_TILE, S)
    scale = 1.0 / math.sqrt(HEAD_DIM)

    def body(dq_ref, dk_ref, dv_ref, zq_ref, zk_ref, qw_ref, kw_ref, dqf_ref, dkf_ref, dvf_ref, dF_ref, dqw_ref, dkw_ref):
        i, j = pl.program_id(0), pl.program_id(1)
        lane = _lane_ids(tp)
        lo = lane < HEAD_DIM

        @pl.when((i == 0) & (j == 0))
        def _():
            dqw_ref[...] = jnp.zeros_like(dqw_ref)
            dkw_ref[...] = jnp.zeros_like(dkw_ref)

        def pair(ref):
            return jnp.where(lo, ref[0], pltpu.roll(ref[1], HEAD_DIM, 1))

        def norm_bwd(g, x, w):
            r = _pair_rms(x, lo)
            xh = x * r
            dxh = g * w
            tt = dxh * xh
            mean_a = jnp.sum(jnp.where(lo, tt, 0.0), axis=-1, keepdims=True) * (1.0 / HEAD_DIM)
            mean_b = jnp.sum(jnp.where(lo, 0.0, tt), axis=-1, keepdims=True) * (1.0 / HEAD_DIM)
            return r * (dxh - xh * jnp.where(lo, mean_a, mean_b)), g * xh

        dxq, gq = norm_bwd(pair(dq_ref) * scale, zq_ref[...].astype(F32), qw_ref[...])
        dqf_ref[...] = dxq.astype(BF16)
        dqw_ref[...] += jnp.sum(gq, axis=0, keepdims=True)
        dxk, gk = norm_bwd(pair(dk_ref), zk_ref[...].astype(F32), kw_ref[...])
        dkf_ref[...] = dxk.astype(BF16)
        dkw_ref[...] += jnp.sum(gk, axis=0, keepdims=True)
        dvf_ref[...] = pair(dv_ref).astype(BF16)

        contrib = jnp.zeros((tp, AUG), F32)
        for half in range(2):
            df = (jnp.sum(jnp.where(lane == C_F, dq_ref[half], 0.0), axis=-1, keepdims=True)
                  - jnp.sum(jnp.where(lane == C_ONE, dk_ref[half], 0.0), axis=-1, keepdims=True))
            contrib = jnp.where(lane == 2 * j + half, df, contrib)

        @pl.when(j == 0)
        def _():
            dF_ref[...] = contrib

        @pl.when(j > 0)
        def _():
            dF_ref[...] += contrib

    pairb = pl.BlockSpec((2, tp, AUG), lambda i, j: (j, i, 0))
    col = lambda c0: pl.BlockSpec((tp, AUG), lambda i, j, c0=c0: (i, c0 + j))
    vec = pl.BlockSpec((1, AUG), lambda i, j: (0, 0))
    flat = pl.BlockSpec((tp, AUG), lambda i, j: (i, j))
    return _pallas(body, name="qkv_bwd", grid=(S // tp, N_HEADS // 2),
                   in_specs=[pairb, pairb, pairb, col(8), col(12), vec, vec],
                   out_specs=[flat, flat, flat, pl.BlockSpec((tp, AUG), lambda i, j: (i, 0)), vec, vec],
                   out_shape=[_sds((S, D_ATTN), BF16)] * 3 + [_sds((S, AUG), F32), _sds((1, AUG), F32), _sds((1, AUG), F32)],
                   compiler_params=_cp(("arbitrary", "arbitrary")))(dq, dk, dv, z, z, qw, kw)


def _proj_res_heads(a, wa, o_aug, wo, res):
    S, D = res.shape
    H = o_aug.shape[0]
    tm = TOK_TILE

    def body(a_ref, wa_ref, o_ref, wo_ref, res_ref, out_ref):
        acc = res_ref[...] + _dot(a_ref[...], wa_ref[...])
        for h in range(H):
            acc = acc + _dot(o_ref[h], wo_ref[h])
        out_ref[...] = acc

    row = pl.BlockSpec((tm, D), lambda i: (i, 0))
    return _pallas(body, name="proj_res_heads", grid=(S // tm,),
                   in_specs=[pl.BlockSpec((tm, a.shape[1]), lambda i: (i, 0)), pl.BlockSpec(wa.shape, lambda i: (0, 0)),
                             pl.BlockSpec((H, tm, AUG), lambda i: (0, i, 0)), pl.BlockSpec(wo.shape, lambda i: (0, 0, 0)), row],
                   out_specs=row, out_shape=_sds((S, D), F32), compiler_params=_cp(("parallel",)))(a, wa, o_aug, wo, res)


def _heads_tn(o_aug, d):
    H, S, A = o_aug.shape
    D = d.shape[1]
    tm = min(DW_TILE, S)
    nt = S // tm

    def body(o_ref, d_ref, out_ref, acc_ref):
        i = pl.program_id(0)

        @pl.when(i == 0)
        def _():
            acc_ref[...] = jnp.zeros_like(acc_ref)

        dv = d_ref[...].astype(BF16)
        for h in range(H):
            acc_ref[h] += _dot_tn(o_ref[h], dv)

        @pl.when(i == nt - 1)
        def _():
            out_ref[...] = acc_ref[...].astype(BF16)

    return _pallas(body, name="heads_tn", grid=(nt,),
                   in_specs=[pl.BlockSpec((H, tm, A), lambda i: (0, i, 0)), pl.BlockSpec((tm, D), lambda i: (i, 0))],
                   out_specs=pl.BlockSpec((H, A, D), lambda i: (0, 0, 0)), out_shape=_sds((H, A, D), BF16),
                   scratch_shapes=[pltpu.VMEM((H, A, D), F32)], compiler_params=_cp(("arbitrary",)))(o_aug, d)


def _odd_mid_fwd(z, cw):
    S = z.shape[0]
    D = z.shape[1] // 3
    tm = TOK_TILE
    hb = tm // HALO_C
    W = CONV_C_WIDTH

    def body(gb_ref, gc_ref, hh_ref, gcp_ref, hhp_ref, cw_ref, y_ref, win):
        i = pl.program_id(0)
        prev = gcp_ref[...].astype(F32) * hhp_ref[...].astype(F32)
        win[pl.ds(0, HALO_C), :] = jnp.where(i == 0, 0.0, prev)
        win[pl.ds(HALO_C, tm), :] = gc_ref[...].astype(F32) * hh_ref[...].astype(F32)
        c1 = jnp.zeros((tm, D), F32)
        for k in range(W):
            c1 = c1 + cw_ref[k:k + 1, :] * win[pl.ds(HALO_C - (W - 1) + k, tm), :]
        y_ref[...] = (gb_ref[...].astype(F32) * c1).astype(BF16)

    cur = lambda c: pl.BlockSpec((tm, D), lambda i, c=c: (i, c))
    prv = lambda c: pl.BlockSpec((HALO_C, D), lambda i, c=c: (jnp.maximum(i * hb - 1, 0), c))
    return _pallas(body, name="odd_mid_fwd", grid=(S // tm,),
                   in_specs=[cur(0), cur(1), cur(2), prv(1), prv(2), pl.BlockSpec((8, D), lambda i: (0, 0))],
                   out_specs=pl.BlockSpec((tm, D), lambda i: (i, 0)), out_shape=_sds((S, D), BF16),
                   scratch_shapes=[pltpu.VMEM((tm + HALO_C, D), F32)],
                   compiler_params=_cp(("parallel",)))(z, z, z, z, z, cw)


def _odd_mid_bwd(dy, z, cw):
    S = z.shape[0]
    D = z.shape[1] // 3
    tm = TOK_TILE
    hb = tm // HALO_C
    nt = S // tm
    W = CONV_C_WIDTH
    shifts_w = (0,) + tuple(sorted({(HALO_C - (W - 1) + k) % SUBLANES for k in range(W)} - {0}))
    shifts_d = (0,) + tuple(sorted({(W - 1 - k) % SUBLANES for k in range(W)} - {0}))

    def body(dy_ref, dyn_ref, gb_ref, gbn_ref, gc_ref, hh_ref, gcp_ref, hhp_ref, cw_ref, dz_ref, dcw_ref, win, dwin):
        i = pl.program_id(0)

        @pl.when(i == 0)
        def _():
            dcw_ref[...] = jnp.zeros_like(dcw_ref)

        prev = gcp_ref[...].astype(F32) * hhp_ref[...].astype(F32)
        win[0, pl.ds(0, HALO_C), :] = jnp.where(i == 0, 0.0, prev)
        win[0, pl.ds(HALO_C, tm), :] = gc_ref[...].astype(F32) * hh_ref[...].astype(F32)
        _fill_shifts(win, tm + HALO_C, shifts_w)
        dwin[0, pl.ds(0, tm), :] = dy_ref[...] * gb_ref[...].astype(F32)
        dwin[0, pl.ds(tm, HALO_C), :] = jnp.where(i == nt - 1, 0.0, dyn_ref[...] * gbn_ref[...].astype(F32))
        _fill_shifts(dwin, tm + HALO_C, shifts_d)

        def rows_block(rb, carry):
            r0 = pl.multiple_of(rb * ODD_ROWS, ODD_ROWS)
            rows = pl.ds(r0, ODD_ROWS)
            dc1 = dwin[0, rows, :]
            c1 = jnp.zeros((ODD_ROWS, D), F32)
            dc0 = jnp.zeros((ODD_ROWS, D), F32)
            for k in range(W):
                tap = _tap(win, HALO_C - (W - 1) + k, ODD_ROWS, r0, shifts_w)
                c1 = c1 + cw_ref[k:k + 1, :] * tap
                dc0 = dc0 + cw_ref[k:k + 1, :] * _tap(dwin, W - 1 - k, ODD_ROWS, r0, shifts_d)
                dcw_ref[k:k + 1, :] += jnp.sum(dc1 * tap, axis=0, keepdims=True)
            dz_ref[rows, 0:D] = (dy_ref[rows, :] * c1).astype(BF16)
            dz_ref[rows, D:2 * D] = (dc0 * hh_ref[rows, :].astype(F32)).astype(BF16)
            dz_ref[rows, 2 * D:3 * D] = (dc0 * gc_ref[rows, :].astype(F32)).astype(BF16)
            return carry

        lax.fori_loop(0, tm // ODD_ROWS, rows_block, 0)

    cur = lambda c: pl.BlockSpec((tm, D), lambda i, c=c: (i, c))
    prv = lambda c: pl.BlockSpec((HALO_C, D), lambda i, c=c: (jnp.maximum(i * hb - 1, 0), c))
    nxt = pl.BlockSpec((HALO_C, D), lambda i: (jnp.minimum((i + 1) * hb, S // HALO_C - 1), 0))
    return _pallas(body, name="odd_mid_bwd", grid=(nt,),
                   in_specs=[cur(0), nxt, cur(0), nxt, cur(1), cur(2), prv(1), prv(2), pl.BlockSpec((8, D), lambda i: (0, 0))],
                   out_specs=[pl.BlockSpec((tm, 3 * D), lambda i: (i, 0)), pl.BlockSpec((8, D), lambda i: (0, 0))],
                   out_shape=[_sds((S, 3 * D), BF16), _sds((8, D), F32)],
                   scratch_shapes=[pltpu.VMEM((len(shifts_w), tm + HALO_C, D), F32),
                                   pltpu.VMEM((len(shifts_d), tm + HALO_C, D), F32)],
                   compiler_params=_cp(("arbitrary",)))(dy, dy, z, z, z, z, z, z, cw)


def _loss_head(y, tgt):
    S, D = y.shape
    tm = TOK_TILE

    def body(y_ref, t_ref, dy_ref, l_ref):
        @pl.when(pl.program_id(0) == 0)
        def _():
            l_ref[...] = jnp.zeros_like(l_ref)

        e = y_ref[...] - t_ref[...]
        dy_ref[...] = e * (1.0 / D)
        l_ref[...] += jnp.sum(jnp.sum(e * e, axis=-1, keepdims=True), axis=0, keepdims=True) * (0.5 / D)

    row = pl.BlockSpec((tm, D), lambda i: (i, 0))
    return _pallas(body, name="loss_head", grid=(S // tm,), in_specs=[row, row],
                   out_specs=[row, pl.BlockSpec((1, 1), lambda i: (0, 0))],
                   out_shape=[_sds((S, D), F32), _sds((1, 1), F32)],
                   compiler_params=_cp(("arbitrary",)))(y, tgt)


def _pad_rows(a, rows):
    return jnp.pad(a, ((0, rows - a.shape[0]), (0, 0)))


def _local_step(x, tgt, W, need=lambda block, after: None, done=lambda block, block_grads: None):
    S, D = x.shape
    grads = {}
    saved = {}

    def gain_after(gain, token):
        return gain if token is None else gain + token

    def ffn_f(tag, l, xin):
        need((tag, l), xin)
        out, xn, G, U = _ffn_fwd(xin, W[tag + "_norm"][l:l + 1], W[tag + "_w_gate"][l], W[tag + "_w_up"][l],
                                 W[tag + "_w_down"][l])
        saved[(tag, l)] = (xin, xn, G, U)
        return out

    def ffn_b(tag, l, dout):
        xin, xn, G, U = saved[(tag, l)]
        keys = [(tag + "_w_gate", l), (tag + "_w_up", l), (tag + "_w_down", l)]
        *dws, dG, dU = _ffn_bwd_w(dout, xn, G, U, W[tag + "_w_down"][l])
        big = dict(zip(keys, dws))
        grads.update(big)
        token = done((tag, l), big)
        dx, dg = _norm_in_bwd([dG, dU], [W[tag + "_w_gate"][l], W[tag + "_w_up"][l]], xin,
                              gain_after(W[tag + "_norm"][l:l + 1], token), dout, w_rows=True)
        grads[(tag + "_norm", l)] = dg
        return dx

    x0a = ffn_f("ffn1", 0, x)
    need(("ev", 0), x0a)
    w_in = W["ev_w_in"]
    w_main, w_f = w_in[:, :2560], jnp.pad(w_in[:, 2560:], ((0, 0), (0, 120)))
    h0, z0, fl = _norm_proj(x0a, W["mix_norm"][0:1], w_main, w_f)
    cw_a = _pad_rows(W["ev_conv_w"], 32)
    a_act, a1 = _conv_a_fwd(z0, cw_a, W["ev_conv_b"], W["ev_conv_norm"])
    flb, Fc = _forget_scan(fl, jnp.pad(W["ev_b_f"], ((0, 0), (0, 120))))
    qw2, kw2 = jnp.tile(W["ev_q_norm"], (1, 2)), jnp.tile(W["ev_k_norm"], (1, 2))
    q_aug, k_aug, v_aug = _qkv_prep(z0, Fc, qw2, kw2)
    o_aug, q_lse = _fox_fwd(q_aug, k_aug, v_aug)
    w_out_e = W["ev_w_out"]
    w_out_o = jnp.pad(w_out_e[D_CONV:].reshape(N_HEADS, HEAD_DIM, D), ((0, 0), (0, AUG - HEAD_DIM), (0, 0)))
    x0b = _proj_res_heads(a_act, w_out_e[:D_CONV], o_aug, w_out_o, x0a)
    x0c = ffn_f("ffn2", 0, x0b)
    x1a = ffn_f("ffn1", 1, x0c)
    need(("od", 0), x1a)
    h1, z1 = _norm_proj(x1a, W["mix_norm"][1:2], W["od_w_in"])
    cw_c = _pad_rows(W["od_conv_w"], 8)
    y1 = _odd_mid_fwd(z1, cw_c)
    x1b = _proj_res([y1], [W["od_w_out"]], x1a)
    x1c = ffn_f("ffn2", 1, x1b)
    dy, loss = _loss_head(x1c, tgt)

    d = ffn_b("ffn2", 1, dy)
    dy1 = _matmul_nt(d, W["od_w_out"])
    grads[("od_w_out", 0)] = _matmul_tn(y1, d, D)[0]
    dz1, dcw_c = _odd_mid_bwd(dy1, z1, cw_c)
    grads[("od_conv_w", 0)] = dcw_c[:CONV_C_WIDTH]
    grads[("od_w_in", 0)] = _matmul_tn(h1, dz1, 3 * D // 4)
    token = done(("od", 0), {k: grads[k] for k in (("od_w_out", 0), ("od_w_in", 0))})
    d, dg = _norm_in_bwd([dz1[None]], [W["od_w_in"][None]], x1a, gain_after(W["mix_norm"][1:2], token), d)
    grads[("mix_norm", 1)] = dg
    d = ffn_b("ffn1", 1, d)
    d = ffn_b("ffn2", 0, d)
    dcat = _matmul_nt(d, w_out_e)
    grads[("ev_w_out", 0)] = jnp.concatenate([_matmul_tn(a_act, d, D)[0],
                                              _heads_tn(o_aug, d)[:, :HEAD_DIM].reshape(D_ATTN, D)], axis=0)
    duz, dcw_a, dcb, dcn = _conv_a_bwd(dcat, a1, z0, cw_a, W["ev_conv_norm"])
    grads[("ev_conv_w", 0)] = dcw_a[:CONV_A_WIDTH]
    grads[("ev_conv_b", 0)] = dcb
    grads[("ev_conv_norm", 0)] = dcn
    dq_a, dk_a, dv_a = _fox_bwd(q_lse, k_aug, v_aug, _do_prep(dcat, o_aug))
    dqf, dkf, dvf, dF, dqw, dkw = _qkv_bwd(dq_a, dk_a, dv_a, z0, qw2, kw2)
    grads[("ev_q_norm", 0)] = dqw[:, :HEAD_DIM] + dqw[:, HEAD_DIM:]
    grads[("ev_k_norm", 0)] = dkw[:, :HEAD_DIM] + dkw[:, HEAD_DIM:]
    dfl, dbf = _forget_scan_bwd(dF, flb)
    grads[("ev_b_f", 0)] = dbf[:, :N_HEADS]
    dz0 = jnp.concatenate([duz, dqf, dkf, dvf], axis=1)
    dflb = dfl.astype(BF16)
    gmain = _matmul_tn(h0, dz0, 640)
    gmain = gmain.transpose(1, 0, 2).reshape(D, 2560)
    gf = _matmul_tn(h0, dflb, 128)[0][:, :N_HEADS]
    grads[("ev_w_in", 0)] = jnp.concatenate([gmain, gf], axis=1)
    token = done(("ev", 0), {k: grads[k] for k in (("ev_w_out", 0), ("ev_w_in", 0))})
    d, dg = _norm_in_bwd([dz0[None], dflb[None]], [w_main[None], w_f[None]], x0a, gain_after(W["mix_norm"][0:1], token), d)
    grads[("mix_norm", 0)] = dg
    d = ffn_b("ffn1", 0, d)
    return loss, d, grads


def _place():
    x, y, c = lax.axis_index("x"), lax.axis_index("y"), lax.axis_index("c")
    chips = [(1 - x, y), (x, 1 - y), (1 - x, 1 - y)]
    return x, y, c, chips


def _remote(src, dst, send_sem, recv_sem, to):
    return pltpu.make_async_remote_copy(src_ref=src, dst_ref=dst, send_sem=send_sem, recv_sem=recv_sem,
                                        device_id=to, device_id_type=MESH)


HBM = pl.BlockSpec(memory_space=pltpu.HBM)
SEM = pl.BlockSpec(memory_space=pltpu.SEMAPHORE)
EFFECT = pltpu.SideEffectType.DATAFLOW_SIDE_EFFECTING


def _in_hbm(a):
    return pltpu.with_memory_space_constraint(a, pltpu.HBM)


def _ag_start(tag, bufs, with_taps):
    n = len(bufs)
    order = ([n - 1] + list(range(n - 1))) if with_taps else list(range(n))

    def body(*refs):
        send_sems, recv_sems = refs[n], refs[n + 1]
        outs, token = refs[n + 2:2 * n + 2], refs[2 * n + 2]
        x, y, c, chips = _place()
        me = 2 * x + y
        for a in order:
            if with_taps and a == n - 1:
                blk = outs[a].at[me]
            else:
                h = outs[a].shape[1] // 2
                blk = outs[a].at[me, pl.ds(c * h, h)]
            for jj, (px, py) in enumerate(chips):
                _remote(blk, blk, send_sems.at[3 * a + jj], recv_sems.at[3 * a + jj], (px, py, c)).start()
        token[...] = jnp.zeros_like(token)

    return _pallas(
        body, name=f"gather_start_{tag}",
        out_shape=[pltpu.SemaphoreType.DMA((3 * n,)), pltpu.SemaphoreType.DMA((3 * n,))]
        + [pltpu.HBM(b.shape, b.dtype) for b in bufs] + [_sds((8, 128), F32)],
        in_specs=[HBM] * n, out_specs=[SEM, SEM] + [HBM] * n + [pl.BlockSpec(memory_space=pltpu.VMEM)],
        input_output_aliases={a: 2 + a for a in range(n)},
        compiler_params=pltpu.CompilerParams(has_side_effects=EFFECT),
    )(*[_in_hbm(b) for b in bufs])


def _ag_mid(g, ici_send, ici_recv, bufs, idx, taps, n_big, after):
    n = len(bufs)
    arrs = list(bufs) + ([taps] if taps is not None else [])
    m = len(arrs)

    def body(*refs):
        ici_s, ici_r = refs[0], refs[1]
        d_send, d_recv = refs[m + 3], refs[m + 4]
        outs = refs[m + 5:]
        x, y, c, chips = _place()
        me = 2 * x + y
        for i in range(m):
            a = idx[i] if i < n else n_big
            for jj, (px, py) in enumerate(chips):
                k = 3 * a + jj
                if i < n:
                    h = outs[i].shape[1] // 2
                    mine, blk = outs[i].at[me, pl.ds(c * h, h)], outs[i].at[2 * px + py, pl.ds(c * h, h)]
                else:
                    mine, blk = outs[i].at[me], outs[i].at[2 * px + py]
                _remote(mine, mine, ici_s.at[k], ici_r.at[k], (px, py, c)).wait_send()
                _remote(blk, blk, ici_s.at[k], ici_r.at[k], (px, py, c)).wait_recv()
                if i < n:
                    _remote(blk, blk, d_send.at[3 * i + jj], d_recv.at[3 * i + jj], (x, y, 1 - c)).start()

    return _pallas(
        body, name=f"gather_pass_on_{g}",
        out_shape=[pltpu.SemaphoreType.DMA((3 * n,)), pltpu.SemaphoreType.DMA((3 * n,))] + [pltpu.HBM(b.shape, b.dtype) for b in arrs],
        in_specs=[SEM, SEM] + [HBM] * m + [ANY], out_specs=[SEM, SEM] + [HBM] * m,
        input_output_aliases={2 + i: 2 + i for i in range(m)},
        compiler_params=pltpu.CompilerParams(has_side_effects=EFFECT),
    )(ici_send, ici_recv, *arrs, after)


def _ag_wait(g, d_send, d_recv, arrs, n, after):
    m = len(arrs)

    def body(*refs):
        d_s, d_r = refs[0], refs[1]
        outs = refs[m + 3:]
        x, y, c, chips = _place()
        for i in range(n):
            h = outs[i].shape[1] // 2
            for jj, (px, py) in enumerate(chips):
                sent = outs[i].at[2 * px + py, pl.ds(c * h, h)]
                got = outs[i].at[2 * px + py, pl.ds((1 - c) * h, h)]
                _remote(sent, sent, d_s.at[3 * i + jj], d_r.at[3 * i + jj], (x, y, 1 - c)).wait_send()
                _remote(got, got, d_s.at[3 * i + jj], d_r.at[3 * i + jj], (x, y, 1 - c)).wait_recv()

    return _pallas(
        body, name=f"gather_wait_{g}", out_shape=[pltpu.HBM(b.shape, b.dtype) for b in arrs],
        in_specs=[SEM, SEM] + [HBM] * m + [ANY], out_specs=[HBM] * m,
        input_output_aliases={2 + i: i for i in range(m)},
        compiler_params=pltpu.CompilerParams(has_side_effects=EFFECT),
    )(d_send, d_recv, *arrs, after)


def _pair_start(g, gs, after):
    n = len(gs)
    zones = [lax.empty((4, a.shape[1] // 2, a.shape[2]), a.dtype) for a in gs]
    extra = [] if after is None else [after]

    def body(*refs):
        k0 = 2 * n + len(extra)
        send_sems, recv_sems = refs[k0], refs[k0 + 1]
        src, dst = refs[k0 + 2:k0 + 2 + n], refs[k0 + 2 + n:k0 + 2 + 2 * n]
        token = refs[k0 + 2 + 2 * n]
        x, y, c, _ = _place()
        for a in range(n):
            h = src[a].shape[1] // 2
            _remote(src[a].at[:, pl.ds((1 - c) * h, h)], dst[a], send_sems.at[a], recv_sems.at[a], (x, y, 1 - c)).start()
        token[...] = jnp.zeros_like(token)

    return _pallas(
        body, name=f"grad_pair_start_{g}",
        out_shape=[pltpu.SemaphoreType.DMA((n,)), pltpu.SemaphoreType.DMA((n,))]
        + [pltpu.HBM(a.shape, a.dtype) for a in gs + zones] + [_sds((8, 128), F32)],
        in_specs=[HBM] * (2 * n) + [ANY] * len(extra),
        out_specs=[SEM, SEM] + [HBM] * (2 * n) + [pl.BlockSpec(memory_space=pltpu.VMEM)],
        input_output_aliases={i: 2 + i for i in range(2 * n)},
        compiler_params=pltpu.CompilerParams(has_side_effects=EFFECT),
    )(*[_in_hbm(a) for a in gs + zones], *extra)


def _pair_wait(g, send, recv, gs, zones):
    n = len(gs)

    def body(*refs):
        s_ref, r_ref = refs[0], refs[1]
        outs = refs[2 + 2 * n:]
        src, dst = outs[:n], outs[n:]
        x, y, c, _ = _place()
        for a in range(n):
            h = src[a].shape[1] // 2
            _remote(src[a].at[:, pl.ds((1 - c) * h, h)], dst[a], s_ref.at[a], r_ref.at[a], (x, y, 1 - c)).wait()

    return _pallas(
        body, name=f"grad_pair_wait_{g}", out_shape=[pltpu.HBM(a.shape, a.dtype) for a in gs + zones],
        in_specs=[SEM, SEM] + [HBM] * (2 * n), out_specs=[HBM] * (2 * n),
        input_output_aliases={2 + i: i for i in range(2 * n)},
        compiler_params=pltpu.CompilerParams(has_side_effects=EFFECT),
    )(send, recv, *gs, *zones)


def _pair_add(gs, others, c_arr):
    n = len(gs)

    def body(c_ref, *refs):
        for g_ref, o_ref, out_ref in zip(refs[:n], refs[n:2 * n], refs[2 * n:]):
            out_ref[...] = (g_ref[...].astype(F32) + o_ref[...].astype(F32)).astype(BF16)

    half = lambda a: pl.BlockSpec((1, a.shape[1] // 2, a.shape[2]), lambda k, c_ref: (k, c_ref[0], 0))
    whole = lambda a: pl.BlockSpec((1,) + a.shape[1:], lambda k, c_ref: (k, 0, 0))
    grid_spec = pltpu.PrefetchScalarGridSpec(
        num_scalar_prefetch=1, grid=(4,), in_specs=[half(a) for a in gs] + [whole(o) for o in others],
        out_specs=[whole(o) for o in others])
    return _pallas(body, name="grad_pair_add", grid_spec=grid_spec, out_shape=[_sds(o.shape, BF16) for o in others],
                   compiler_params=_cp(("parallel",)))(c_arr, *gs, *others)


def _chip_start(g, ss):
    n = len(ss)
    zones = [lax.empty((3,) + s.shape[1:], s.dtype) for s in ss]

    def body(*refs):
        send_sems, recv_sems = refs[2 * n], refs[2 * n + 1]
        src, dst = refs[2 * n + 2:3 * n + 2], refs[3 * n + 2:4 * n + 2]
        token = refs[4 * n + 2]
        x, y, c, chips = _place()
        for a in range(n):
            for jj, (px, py) in enumerate(chips):
                k = 3 * a + jj
                _remote(src[a].at[2 * px + py], dst[a].at[jj], send_sems.at[k], recv_sems.at[k], (px, py, c)).start()
        token[...] = jnp.zeros_like(token)

    return _pallas(
        body, name=f"grad_chip_start_{g}",
        out_shape=[pltpu.SemaphoreType.DMA((3 * n,)), pltpu.SemaphoreType.DMA((3 * n,))]
        + [pltpu.HBM(a.shape, a.dtype) for a in ss + zones] + [_sds((8, 128), F32)],
        in_specs=[HBM] * (2 * n), out_specs=[SEM, SEM] + [HBM] * (2 * n) + [pl.BlockSpec(memory_space=pltpu.VMEM)],
        input_output_aliases={i: 2 + i for i in range(2 * n)},
        compiler_params=pltpu.CompilerParams(has_side_effects=EFFECT),
    )(*[_in_hbm(a) for a in ss + zones])


def _chip_wait(tag, sends, recvs, counts, ss, zones, after):
    nb, n = len(sends), len(ss)

    def body(*refs):
        s_refs, r_refs = refs[:nb], refs[nb:2 * nb]
        outs = refs[2 * nb + 2 * n + 1:]
        src, dst = outs[:n], outs[n:]
        x, y, c, chips = _place()
        a = 0
        for b in range(nb):
            for i in range(counts[b]):
                for jj, (px, py) in enumerate(chips):
                    k = 3 * i + jj
                    _remote(src[a].at[2 * px + py], dst[a].at[jj], s_refs[b].at[k], r_refs[b].at[k], (px, py, c)).wait()
                a += 1

    return _pallas(
        body, name=f"grad_chip_wait_{tag}", out_shape=[pltpu.HBM(a.shape, a.dtype) for a in ss + zones],
        in_specs=[SEM] * (2 * nb) + [HBM] * (2 * n) + [ANY], out_specs=[HBM] * (2 * n),
        input_output_aliases={2 * nb + i: i for i in range(2 * n)},
        compiler_params=pltpu.CompilerParams(has_side_effects=EFFECT),
    )(*sends, *recvs, *ss, *zones, after)


def _chip_sum(s, r, where, dest, l, L):
    _, h, C = s.shape
    tr = h // 2

    def body(k_ref, s_ref, r_ref, *rest):
        out_ref = rest[-1]
        acc = s_ref[0].astype(F32)
        for jj in range(3):
            acc = acc + r_ref[jj].astype(F32)
        out_ref[...] = acc

    in_specs = [pl.BlockSpec((1, tr, C), lambda i, k_ref: (k_ref[0], i, 0)), pl.BlockSpec((3, tr, C), lambda i, k_ref: (0, i, 0))]
    args = [where, s, r]
    alias = {}
    if dest is not None:
        in_specs.append(ANY)
        args.append(dest)
        alias = {3: 0}
    grid_spec = pltpu.PrefetchScalarGridSpec(
        num_scalar_prefetch=1, grid=(2,), in_specs=in_specs,
        out_specs=pl.BlockSpec((None, tr, C), lambda i, k_ref: (l, 2 * k_ref[1] + i, 0)))
    return _pallas(body, name="grad_chip_sum", grid_spec=grid_spec, out_shape=_sds((L, 2 * h, C), F32),
                   input_output_aliases=alias, compiler_params=_cp(("arbitrary",)))(*args)


def _share_start(tag, bufs, layout):
    n, n_buf = len(layout), len(bufs)

    def body(*refs):
        send_sems, recv_sems = refs[n_buf], refs[n_buf + 1]
        outs = refs[n_buf + 2:]
        x, y, c, _ = _place()
        for a, (o, l) in enumerate(layout):
            h = outs[o].shape[1] // 2
            blk = outs[o].at[l, pl.ds(c * h, h)]
            _remote(blk, blk, send_sems.at[a], recv_sems.at[a], (x, y, 1 - c)).start()

    return _pallas(
        body, name=f"grad_share_start_{tag}",
        out_shape=[pltpu.SemaphoreType.DMA((n,)), pltpu.SemaphoreType.DMA((n,))] + [pltpu.HBM(b.shape, b.dtype) for b in bufs],
        in_specs=[HBM] * n_buf, out_specs=[SEM, SEM] + [HBM] * n_buf, input_output_aliases={o: 2 + o for o in range(n_buf)},
        compiler_params=pltpu.CompilerParams(has_side_effects=EFFECT),
    )(*[_in_hbm(b) for b in bufs])


def _share_wait(tag, send, recv, bufs, layout, after):
    n_buf = len(bufs)

    def body(*refs):
        s_ref, r_ref = refs[0], refs[1]
        outs = refs[n_buf + 3:]
        x, y, c, _ = _place()
        for a, (o, l) in enumerate(layout):
            h = outs[o].shape[1] // 2
            mine, theirs = outs[o].at[l, pl.ds(c * h, h)], outs[o].at[l, pl.ds((1 - c) * h, h)]
            _remote(mine, mine, s_ref.at[a], r_ref.at[a], (x, y, 1 - c)).wait_send()
            _remote(theirs, theirs, s_ref.at[a], r_ref.at[a], (x, y, 1 - c)).wait_recv()

    return _pallas(
        body, name=f"grad_share_wait_{tag}", out_shape=[pltpu.HBM(b.shape, b.dtype) for b in bufs],
        in_specs=[SEM, SEM] + [HBM] * n_buf + [ANY], out_specs=[HBM] * n_buf,
        input_output_aliases={2 + o: o for o in range(n_buf)},
        compiler_params=pltpu.CompilerParams(has_side_effects=EFFECT),
    )(send, recv, *bufs, after)


def _small_all_reduce(packed, after):
    P, L = packed.shape

    def body(in_ref, after_ref, out_ref, slots, send_sems, recv_sems):
        x, y, c, _ = _place()
        me = 4 * x + 2 * y + c
        slots[me] = in_ref[...]
        cps = []
        for r in range(1, 8):
            px = 1 - x if r & 4 else x
            py = 1 - y if r & 2 else y
            pc = 1 - c if r & 1 else c
            cps.append(_remote(in_ref, slots.at[me], send_sems.at[r - 1], recv_sems.at[r - 1], (px, py, pc)))
        for cp in cps:
            cp.start()
        for r in range(1, 8):
            px = 1 - x if r & 4 else x
            py = 1 - y if r & 2 else y
            pc = 1 - c if r & 1 else c
            blk = slots.at[4 * px + 2 * py + pc]
            _remote(blk, blk, send_sems.at[r - 1], recv_sems.at[r - 1], (px, py, pc)).wait_recv()
        for cp in cps:
            cp.wait_send()
        acc = slots[0]
        for k in range(1, 8):
            acc = acc + slots[k]
        out_ref[...] = acc

    vm = pl.BlockSpec(memory_space=pltpu.VMEM)
    return _pallas(body, name="small_all_reduce", in_specs=[vm, ANY], out_specs=vm, out_shape=_sds((P, L), F32),
                   scratch_shapes=[pltpu.VMEM((8, P, L), F32), pltpu.SemaphoreType.DMA((7,)),
                                   pltpu.SemaphoreType.DMA((7,))])(packed, after)


def _adamw_math(w, g, m, v):
    m = ADAM_B1 * m + (1.0 - ADAM_B1) * g
    v = ADAM_B2 * v + (1.0 - ADAM_B2) * (g * g)
    m_hat = m / (1.0 - ADAM_B1 ** ADAM_STEP)
    v_hat = v / (1.0 - ADAM_B2 ** ADAM_STEP)
    delta = -ADAM_LR * (m_hat / (jnp.sqrt(v_hat) + ADAM_EPS) + ADAM_WD * w)
    return delta, m, v


def _adamw(w, g, m, v):
    shape = w.shape
    C = shape[-1]
    rows = math.prod(shape[:-1])
    tr = next(t for t in (512, 352, 256, 128, 64, 32, 16, 8, rows) if rows % t == 0)
    w2, g2, m2, v2 = (a.reshape(rows, C) for a in (w, g, m, v))

    def body(w_ref, g_ref, m_ref, v_ref, go_ref, d_ref, nm_ref, nv_ref):
        gv = g_ref[...]
        d, nm, nv = _adamw_math(w_ref[...], gv, m_ref[...], v_ref[...])
        go_ref[...] = gv
        d_ref[...] = d
        nm_ref[...] = nm
        nv_ref[...] = nv

    blk = pl.BlockSpec((tr, C), lambda i: (i, 0))
    outs = _pallas(body, name="adamw", grid=(rows // tr,), in_specs=[blk] * 4, out_specs=[blk] * 4,
                   out_shape=[_sds((rows, C), F32)] * 4, compiler_params=_cp(("parallel",)))(w2, g2, m2, v2)
    return tuple(o.reshape(shape) for o in outs)


WEIGHTS = ["ffn1_norm", "ffn1_w_gate", "ffn1_w_up", "ffn1_w_down", "mix_norm", "ffn2_norm", "ffn2_w_gate", "ffn2_w_up",
           "ffn2_w_down", "ev_w_in", "ev_b_f", "ev_conv_w", "ev_conv_b", "ev_conv_norm", "ev_q_norm", "ev_k_norm",
           "ev_w_out", "od_w_in", "od_conv_w", "od_w_out"]
BIG = ([("ffn1_w_gate", 0), ("ffn1_w_up", 0), ("ffn1_w_down", 0), ("ev_w_in", 0), ("ev_w_out", 0),
        ("ffn2_w_gate", 0), ("ffn2_w_up", 0), ("ffn2_w_down", 0)]
       + [("ffn1_w_gate", 1), ("ffn1_w_up", 1), ("ffn1_w_down", 1), ("od_w_in", 0), ("od_w_out", 0),
          ("ffn2_w_gate", 1), ("ffn2_w_up", 1), ("ffn2_w_down", 1)])
TRANSPOSED = ("ffn1_w_gate", "ffn1_w_up", "ffn2_w_gate", "ffn2_w_up")
SHARED_LAST = ("ffn1_w_gate", "ffn1_w_up", "ffn1_w_down", "ev_w_in", "ev_w_out")
BLOCKS = [("ffn1", 0), ("ev", 0), ("ffn2", 0), ("ffn1", 1), ("od", 0), ("ffn2", 1)]
BLOCK_OF = {(name, l): (name.split("_w_")[0], l) for name, l in BIG}
BIG_NAMES = ["ffn1_w_gate", "ffn1_w_up", "ffn1_w_down", "ffn2_w_gate", "ffn2_w_up", "ffn2_w_down",
             "ev_w_in", "ev_w_out", "od_w_in", "od_w_out"]
SMALL = [("ffn1_norm", 16), ("mix_norm", 16), ("ffn2_norm", 16), ("ev_b_f", 8), ("ev_conv_w", 128), ("ev_conv_b", 8),
         ("ev_conv_norm", 8), ("ev_q_norm", 8), ("ev_k_norm", 8), ("od_conv_w", 24)]


def _to_lanes(a, rows):
    flat = a.reshape(-1)
    return jnp.pad(flat, (0, rows * 128 - flat.shape[0])).reshape(rows, 128)


def kernel(x, ffn1_norm, ffn1_w_gate, ffn1_w_up, ffn1_w_down, mix_norm, ffn2_norm, ffn2_w_gate, ffn2_w_up, ffn2_w_down, ev_w_in, ev_b_f, ev_conv_w, ev_conv_b, ev_conv_norm, ev_q_norm, ev_k_norm, ev_w_out, od_w_in, od_conv_w, od_w_out, loss_target, m_ffn1_norm, m_ffn1_w_gate, m_ffn1_w_up, m_ffn1_w_down, m_mix_norm, m_ffn2_norm, m_ffn2_w_gate, m_ffn2_w_up, m_ffn2_w_down, m_ev_w_in, m_ev_b_f, m_ev_conv_w, m_ev_conv_b, m_ev_conv_norm, m_ev_q_norm, m_ev_k_norm, m_ev_w_out, m_od_w_in, m_od_conv_w, m_od_w_out, v_ffn1_norm, v_ffn1_w_gate, v_ffn1_w_up, v_ffn1_w_down, v_mix_norm, v_ffn2_norm, v_ffn2_w_gate, v_ffn2_w_up, v_ffn2_w_down, v_ev_w_in, v_ev_b_f, v_ev_conv_w, v_ev_conv_b, v_ev_conv_norm, v_ev_q_norm, v_ev_k_norm, v_ev_w_out, v_od_w_in, v_od_conv_w, v_od_w_out):
    P = dict(ffn1_norm=ffn1_norm, ffn1_w_gate=ffn1_w_gate, ffn1_w_up=ffn1_w_up, ffn1_w_down=ffn1_w_down, mix_norm=mix_norm,
             ffn2_norm=ffn2_norm, ffn2_w_gate=ffn2_w_gate, ffn2_w_up=ffn2_w_up, ffn2_w_down=ffn2_w_down, ev_w_in=ev_w_in,
             ev_b_f=ev_b_f, ev_conv_w=ev_conv_w, ev_conv_b=ev_conv_b, ev_conv_norm=ev_conv_norm, ev_q_norm=ev_q_norm,
             ev_k_norm=ev_k_norm, ev_w_out=ev_w_out, od_w_in=od_w_in, od_conv_w=od_conv_w, od_w_out=od_w_out)
    M = dict(zip(WEIGHTS, [m_ffn1_norm, m_ffn1_w_gate, m_ffn1_w_up, m_ffn1_w_down, m_mix_norm, m_ffn2_norm, m_ffn2_w_gate,
                           m_ffn2_w_up, m_ffn2_w_down, m_ev_w_in, m_ev_b_f, m_ev_conv_w, m_ev_conv_b, m_ev_conv_norm,
                           m_ev_q_norm, m_ev_k_norm, m_ev_w_out, m_od_w_in, m_od_conv_w, m_od_w_out]))
    V = dict(zip(WEIGHTS, [v_ffn1_norm, v_ffn1_w_gate, v_ffn1_w_up, v_ffn1_w_down, v_mix_norm, v_ffn2_norm, v_ffn2_w_gate,
                           v_ffn2_w_up, v_ffn2_w_down, v_ev_w_in, v_ev_b_f, v_ev_conv_w, v_ev_conv_b, v_ev_conv_norm,
                           v_ev_q_norm, v_ev_k_norm, v_ev_w_out, v_od_w_in, v_od_conv_w, v_od_w_out]))
    for name in TRANSPOSED:
        P[name], M[name], V[name] = (jnp.swapaxes(a, 1, 2) for a in (P[name], M[name], V[name]))
    S, D = x.shape[1], x.shape[2]
    chip = 2 * lax.axis_index("x") + lax.axis_index("y")
    core = lax.axis_index("c")

    def own_slot(shard):
        return lax.dynamic_update_slice(lax.empty((4,) + shard.shape, shard.dtype), shard[None], (chip, 0, 0))

    taps = jnp.concatenate([_to_lanes(_pad_rows(ev_conv_w[0], 32), 32), _to_lanes(_pad_rows(od_conv_w[0], 8), 16)], axis=0)
    first = [i for i, k in enumerate(BIG) if BLOCK_OF[k] in BLOCKS[:2]]
    rest = [i for i in range(len(BIG)) if i not in first]
    send0, recv0, *bufs0 = _ag_start("first", [own_slot(P[BIG[i][0]][BIG[i][1]].astype(BF16)) for i in first]
                                     + [own_slot(taps)], True)
    zero = bufs0.pop()[0, 0]
    send1, recv1, *bufs1 = _ag_start("rest", [own_slot((P[BIG[i][0]][BIG[i][1]] + zero).astype(BF16)) for i in rest], False)
    bufs1.pop()
    cols = lambda a: a.transpose(1, 0, 2).reshape(a.shape[1], 4 * a.shape[2])
    W = {k: P[k] for k in ("ffn1_norm", "mix_norm", "ffn2_norm", "ev_b_f", "ev_q_norm", "ev_k_norm")}
    W["ev_conv_b"], W["ev_conv_norm"] = ev_conv_b, ev_conv_norm
    for tag in ("ffn1", "ffn2"):
        for kind in ("_w_gate", "_w_up", "_w_down"):
            W[tag + kind] = [None, None]
    passing = {}

    def pass_on(g, after):
        idx = [i for i, k in enumerate(BIG) if BLOCK_OF[k] == BLOCKS[g]]
        keys = [BIG[i] for i in idx] + (["taps"] if BLOCKS[g] == ("ev", 0) else [])
        send, recv, bufs, members = (send0, recv0, bufs0, first) if g < 2 else (send1, recv1, bufs1, rest)
        local = [members.index(i) for i in idx]
        passing[g] = (keys, _ag_mid(g, send, recv, [bufs[i] for i in local], local,
                                    bufs0[-1] if BLOCKS[g] == ("ev", 0) else None, len(first), after))

    def need(block, after):
        g = BLOCKS.index(block)
        if g not in passing:
            pass_on(g, bufs1[0] if g == 0 else after)
        keys, (d_send, d_recv, *thru) = passing.pop(g)
        got = dict(zip(keys, _ag_wait(g, d_send, d_recv, thru, len(keys) - ("taps" in keys), after)))
        if 1 <= g < len(BLOCKS) - 1:
            pass_on(g + 1, after)
        for key, a in got.items():
            if key == "taps":
                continue
            name, l = key
            if name.startswith("ffn"):
                W[name][l] = a
            elif name.endswith("_w_in"):
                W[name] = cols(a)
            elif name.endswith("_w_out"):
                W[name] = a.reshape(4 * a.shape[1], D)
        if block == ("ev", 0):
            taps_all = got["taps"]
            W["ev_conv_w"] = cols(taps_all[:, :32].reshape(4, 32, 128))[:CONV_A_WIDTH]
            W["od_conv_w"] = cols(taps_all[:, 32:48].reshape(4, 8, 256))[:CONV_C_WIDTH]

    rows = lambda a: a.reshape(4, a.shape[0] // 4, a.shape[1])
    colsh = lambda a: a.reshape(a.shape[0], 4, a.shape[1] // 4).transpose(1, 0, 2)
    c_arr = core.reshape(1).astype(jnp.int32)
    where = jnp.stack([chip, core]).astype(jnp.int32)
    in_flight = []

    def done(block, block_grads):
        g = BLOCKS.index(block)
        keys = list(block_grads)
        gs = []
        for name, l in keys:
            a = block_grads[(name, l)]
            gs.append(colsh(a) if name == "ev_w_in" else rows(a) if name.endswith("_w_out") else a)
        for item in list(pairs):
            to_chips(item)
        send, recv, *rest = _pair_start(g, gs, chained.get("token"))
        n = len(keys)
        pairs.append((g, keys, send, recv, rest[:n], rest[n:2 * n]))
        if g == 0:
            to_chips(pairs[0])
        chained["token"] = rest[-1] if g else chained["token"]
        return chained["token"][0:1, 0:1]

    pairs, chained = [], {}

    def to_chips(item):
        pairs.remove(item)
        g, keys, send, recv, gs, zones = item
        n = len(keys)
        done_ = _pair_wait(g, send, recv, gs, zones)
        sums = list(_pair_add(list(done_[:n]), list(done_[n:]), c_arr))
        send2, recv2, *rest = _chip_start(g, sums)
        in_flight.append((keys, send2, recv2, rest[:n], rest[n:2 * n]))
        chained["token"] = rest[-1]

    loss, grad_x, grads = _local_step(x[0], loss_target[0], W, need, done)

    stacked, shares = {}, []

    def land(tag, flights, after):
        keys = [k for ks, *_ in flights for k in ks]
        landed = _chip_wait(tag, [f[1] for f in flights], [f[2] for f in flights], [len(f[0]) for f in flights],
                            [a for f in flights for a in f[3]], [a for f in flights for a in f[4]], after)
        return list(zip(keys, landed[:len(keys)], landed[len(keys):]))

    def reduce(items, names):
        for (name, l), s, r in items:
            if name in names:
                stacked[name] = _chip_sum(s, r, where, stacked.get(name), l, P[name].shape[0])

    def share(tag, names):
        layout = [(o, l) for o, name in enumerate(names) for l in range(P[name].shape[0])]
        send, recv, *thru = _share_start(tag, [stacked[name] for name in names], layout)
        shares.append((tag, names, send, recv, thru, layout))

    names_a, names_b = [n for n in BIG_NAMES if n not in SHARED_LAST], list(SHARED_LAST)
    early = land("early", in_flight[:-1], grad_x)
    reduce(early, names_a)
    share("a", names_a)
    reduce(early, names_b)
    reduce(land("last", in_flight[-1:], stacked[names_b[-1]]), names_b)
    share("b", names_b)

    def small_grad(name):
        if name.endswith("_norm") and name[:3] in ("ffn", "mix"):
            return jnp.concatenate([grads[(name, 0)], grads[(name, 1)]], axis=0)
        return grads[(name, 0)]

    packed = jnp.concatenate([_to_lanes(small_grad(name), r) for name, r in SMALL], axis=0)
    total = _small_all_reduce(packed, shares[-1][4][0])
    small_grads, at = {}, 0
    for name, r in SMALL:
        part = total[at:at + r].reshape(-1)
        at += r
        if name == "ev_conv_w":
            full_g = part[:CONV_A_WIDTH * D_CONV].reshape(CONV_A_WIDTH, D_CONV)
            small_grads[name] = lax.dynamic_slice_in_dim(full_g, chip * (D_CONV // 4), D_CONV // 4, axis=1)[None]
        elif name == "od_conv_w":
            full_g = part[:CONV_C_WIDTH * D].reshape(CONV_C_WIDTH, D)
            small_grads[name] = lax.dynamic_slice_in_dim(full_g, chip * (D // 4), D // 4, axis=1)[None]
        else:
            small_grads[name] = part[:math.prod(P[name].shape)].reshape(P[name].shape)

    results = {}

    def update(name, g):
        outs = _adamw(P[name], g, M[name], V[name])
        results[name] = tuple(jnp.swapaxes(a, 1, 2) for a in outs) if name in TRANSPOSED else outs

    for name, _ in SMALL:
        update(name, small_grads[name])
    after = results[SMALL[-1][0]][1]
    for tag, names, send, recv, thru, layout in shares:
        for name, g in zip(names, _share_wait(tag, send, recv, thru, layout, after)):
            update(name, g)
        after = results[names[-1]][1]
    loss_all = lax.psum(loss[0, 0], ("x", "y", "c"))
    return (loss_all, grad_x[None], *[results[name][k] for k in range(4) for name in WEIGHTS])
```

```python
import functools
import math

import jax
import jax.numpy as jnp
from jax import lax
from jax.experimental import pallas as pl
from jax.experimental.pallas import tpu as pltpu

F32, BF16 = jnp.float32, jnp.bfloat16
EPS = 1e-6
FFN_RES = 0.5
N_HEADS, HEAD_DIM = 8, 64
D_CONV = 512
D_ATTN = N_HEADS * HEAD_DIM
CONV_A_WIDTH, CONV_C_WIDTH = 31, 3
ADAM_LR, ADAM_B1, ADAM_B2, ADAM_EPS, ADAM_WD, ADAM_STEP = 0.001, 0.9, 0.999, 1e-08, 0.01, 10
MESH = pl.DeviceIdType.MESH
ANY = pl.BlockSpec(memory_space=pl.ANY)

TOK_TILE = 512
FFN_TILE = 512
DW_TILE = 1024
ATT_TILE = 1024
QKN_TILE = 2048
HALO_A, HALO_C = 32, 16
SUBLANES = 8
CONV_ROWS = 64
ODD_ROWS = 16
SCAN_BLK = 256
MIB = 2 ** 20


def _pallas(body, **kw):
    return pl.pallas_call(body, **kw)


def _cp(sem=None, vmem_mib=48):
    return pltpu.CompilerParams(dimension_semantics=sem, vmem_limit_bytes=vmem_mib * MIB)


def _dot(a, b):
    return jnp.dot(a, b, preferred_element_type=F32)


def _dot_nt(a, b):
    return lax.dot_general(a, b, (((1,), (1,)), ((), ())), preferred_element_type=F32)


def _dot_tn(a, b):
    return lax.dot_general(a, b, (((0,), (0,)), ((), ())), preferred_element_type=F32)


def _sds(shape, dtype):
    return jax.ShapeDtypeStruct(shape, dtype)


def _rms(x):
    return lax.rsqrt(jnp.mean(x * x, axis=-1, keepdims=True) + EPS)


def _rms_bwd(dy, x, g):
    r = _rms(x)
    xh = x * r
    dxh = dy * g
    dx = r * (dxh - xh * jnp.mean(dxh * xh, axis=-1, keepdims=True))
    return dx, xh


def _silu_grad(z):
    s = jax.nn.sigmoid(z)
    return s * (1.0 + z * (1.0 - s))


def _ffn_fwd(x, g, wg, wu, wd):
    S, D = x.shape
    nc, Fs, _ = wd.shape
    tm = min(FFN_TILE, S)
    per = nc
    steps = nc // per

    def body(x_ref, g_ref, wg_ref, wu_ref, wd_ref, out_ref, xn_ref, G_ref, U_ref, acc_ref):
        j = pl.program_id(1)

        @pl.when(j == 0)
        def _():
            xv = x_ref[...]
            xn_ref[...] = (xv * _rms(xv) * g_ref[...]).astype(BF16)
            acc_ref[...] = jnp.zeros_like(acc_ref)

        xn = xn_ref[...]
        part = None
        for k in range(per):
            G = _dot_nt(xn, wg_ref[k])
            U = _dot_nt(xn, wu_ref[k])
            G_ref[k] = G.astype(BF16)
            U_ref[k] = U.astype(BF16)
            term = _dot((G * jax.nn.sigmoid(G) * U).astype(BF16), wd_ref[k])
            part = term if part is None else part + term
        acc_ref[...] += part

        @pl.when(j == steps - 1)
        def _():
            out_ref[...] = x_ref[...] + FFN_RES * acc_ref[...]

    row = pl.BlockSpec((tm, D), lambda i, j: (i, 0))
    wblk = pl.BlockSpec((per, Fs, D), lambda i, j: (j, 0, 0), pipeline_mode=pl.Buffered(1))
    hid = pl.BlockSpec((per, tm, Fs), lambda i, j: (j, i, 0))
    return _pallas(
        body, name="ffn_fwd", grid=(S // tm, steps),
        in_specs=[row, pl.BlockSpec((1, D), lambda i, j: (0, 0)), wblk, wblk, wblk],
        out_specs=[row, row, hid, hid],
        out_shape=[_sds((S, D), F32), _sds((S, D), BF16), _sds((nc, S, Fs), BF16), _sds((nc, S, Fs), BF16)],
        scratch_shapes=[pltpu.VMEM((tm, D), F32)],
        compiler_params=_cp(("parallel", "arbitrary"), 56),
    )(x, g, wg, wu, wd)


def _ffn_bwd_w(dout, xn, G, U, wd):
    S, D = dout.shape
    nc, _, Fs = G.shape
    tm = min(TOK_TILE, S)
    nt = S // tm

    def body(do_ref, xn_ref, G_ref, U_ref, wd_ref, dwg_ref, dwu_ref, dwd_ref, dG_ref, dU_ref, ag, au, ad, do_s, g_s, u_s, h_s):
        i = pl.program_id(1)
        cur, old = i % 2, 1 - i % 2

        @pl.when(i == 0)
        def _():
            ag[...] = jnp.zeros_like(ag)
            au[...] = jnp.zeros_like(au)
            ad[...] = jnp.zeros_like(ad)
            do_s[1] = jnp.zeros_like(do_s[1])
            g_s[1] = jnp.zeros_like(g_s[1])
            u_s[1] = jnp.zeros_like(u_s[1])
            h_s[1] = jnp.zeros_like(h_s[1])

        xnv = xn_ref[...]
        ag[...] += _dot_tn(g_s[old], xnv)
        au[...] += _dot_tn(u_s[old], xnv)
        ad[...] += _dot_tn(h_s[old], do_s[old])

        do = (FFN_RES * do_ref[...]).astype(BF16)
        Gv = G_ref[0].astype(F32)
        Uv = U_ref[0].astype(F32)
        dH = _dot_nt(do, wd_ref[0])
        sg = jax.nn.sigmoid(Gv)
        act = Gv * sg
        dU = (dH * act).astype(BF16)
        dG = (dH * Uv * (sg * (1.0 + Gv * (1.0 - sg)))).astype(BF16)
        do_s[cur] = do
        h_s[cur] = (act * Uv).astype(BF16)
        u_s[cur] = dU
        g_s[cur] = dG
        dU_ref[0] = dU
        dG_ref[0] = dG

        @pl.when(i == nt)
        def _():
            dwg_ref[0] = ag[...].astype(BF16)
            dwu_ref[0] = au[...].astype(BF16)
            dwd_ref[0] = ad[...].astype(BF16)

    now = lambda i: jnp.minimum(i, nt - 1)
    row = pl.BlockSpec((tm, D), lambda j, i: (now(i), 0))
    lag = pl.BlockSpec((tm, D), lambda j, i: (jnp.maximum(i - 1, 0), 0))
    hid = pl.BlockSpec((1, tm, Fs), lambda j, i: (j, now(i), 0))
    wrow = pl.BlockSpec((1, Fs, D), lambda j, i: (j, 0, 0))
    return _pallas(
        body, name="ffn_bwd_w", grid=(nc, nt + 1),
        in_specs=[row, lag, hid, hid, wrow],
        out_specs=[wrow, wrow, wrow, hid, hid],
        out_shape=[_sds((nc, Fs, D), BF16)] * 3 + [_sds((nc, S, Fs), BF16)] * 2,
        scratch_shapes=[pltpu.VMEM((Fs, D), F32)] * 3 + [pltpu.VMEM((2, tm, D), BF16)] + [pltpu.VMEM((2, tm, Fs), BF16)] * 3,
        compiler_params=_cp(("parallel", "arbitrary"), 56),
    )(dout, xn, G, U, wd)


def _norm_in_bwd(dzs, ws, x, g, dres, w_rows=False):
    S, D = x.shape
    nc = dzs[0].shape[0]
    n = len(dzs)
    tm = TOK_TILE
    per = nc
    steps = nc // per

    def body(*refs):
        dz_refs, w_refs = refs[:n], refs[n:2 * n]
        x_ref, g_ref, dres_ref, dx_ref, dg_ref, acc_ref = refs[2 * n:]
        i, j = pl.program_id(0), pl.program_id(1)

        @pl.when(j == 0)
        def _():
            acc_ref[...] = jnp.zeros_like(acc_ref)

        @pl.when((i == 0) & (j == 0))
        def _():
            dg_ref[...] = jnp.zeros_like(dg_ref)

        part = None
        for dz_ref, w_ref in zip(dz_refs, w_refs):
            for k in range(per):
                term = _dot(dz_ref[k], w_ref[k]) if w_rows else _dot_nt(dz_ref[k], w_ref[k])
                part = term if part is None else part + term
        acc_ref[...] += part

        @pl.when(j == steps - 1)
        def _():
            dxn = acc_ref[...]
            dx, xh = _rms_bwd(dxn, x_ref[...], g_ref[...])
            dx_ref[...] = dx + dres_ref[...]
            dg_ref[...] += jnp.sum(dxn * xh, axis=0, keepdims=True)

    row = pl.BlockSpec((tm, D), lambda i, j: (i, 0))
    one = pl.BlockSpec((1, D), lambda i, j: (0, 0))
    in_specs = [pl.BlockSpec((per, tm, dz.shape[2]), lambda i, j: (j, i, 0)) for dz in dzs]
    in_specs += [pl.BlockSpec((per,) + w.shape[1:], lambda i, j: (j, 0, 0)) for w in ws]
    return _pallas(
        body, name="norm_in_bwd", grid=(S // tm, steps),
        in_specs=in_specs + [row, one, row], out_specs=[row, one],
        out_shape=[_sds((S, D), F32), _sds((1, D), F32)],
        scratch_shapes=[pltpu.VMEM((tm, D), F32)],
        compiler_params=_cp(("arbitrary", "arbitrary")),
    )(*dzs, *ws, x, g, dres)


def _norm_proj(x, g, w, w2=None):
    S, D = x.shape
    N = w.shape[1]
    tm = TOK_TILE

    def body(*refs):
        if w2 is None:
            x_ref, g_ref, w_ref, h_ref, z_ref = refs
        else:
            x_ref, g_ref, w_ref, w2_ref, h_ref, z_ref, z2_ref = refs
        xv = x_ref[...]
        h = (xv * _rms(xv) * g_ref[...]).astype(BF16)
        h_ref[...] = h
        z_ref[...] = _dot(h, w_ref[...]).astype(BF16)
        if w2 is not None:
            z2_ref[...] = _dot(h, w2_ref[...])

    row = pl.BlockSpec((tm, D), lambda i: (i, 0))
    in_specs = [row, pl.BlockSpec((1, D), lambda i: (0, 0)), pl.BlockSpec((D, N), lambda i: (0, 0))]
    out_specs = [row, pl.BlockSpec((tm, N), lambda i: (i, 0))]
    out_shape = [_sds((S, D), BF16), _sds((S, N), BF16)]
    args = [x, g, w]
    if w2 is not None:
        N2 = w2.shape[1]
        in_specs.append(pl.BlockSpec((D, N2), lambda i: (0, 0)))
        out_specs.append(pl.BlockSpec((tm, N2), lambda i: (i, 0)))
        out_shape.append(_sds((S, N2), F32))
        args.append(w2)
    return _pallas(body, name="norm_proj", grid=(S // tm,), in_specs=in_specs, out_specs=out_specs,
                   out_shape=out_shape, compiler_params=_cp(("parallel",)))(*args)


def _proj_res(acts, ws, res):
    S, D = res.shape
    n = len(acts)
    tm = TOK_TILE

    def body(*refs):
        a_refs, w_refs = refs[:n], refs[n:2 * n]
        res_ref, out_ref = refs[2 * n:]
        acc = res_ref[...]
        for a_ref, w_ref in zip(a_refs, w_refs):
            acc = acc + _dot(a_ref[...], w_ref[...])
        out_ref[...] = acc

    row = pl.BlockSpec((tm, D), lambda i: (i, 0))
    in_specs = [pl.BlockSpec((tm, a.shape[1]), lambda i: (i, 0)) for a in acts]
    in_specs += [pl.BlockSpec(w.shape, lambda i: (0, 0)) for w in ws]
    return _pallas(body, name="proj_res", grid=(S // tm,), in_specs=in_specs + [row], out_specs=row,
                   out_shape=_sds((S, D), F32), compiler_params=_cp(("parallel",)))(*acts, *ws, res)


def _matmul_nt(a, w, after=None):
    S, K = a.shape
    M = w.shape[0]
    tm = TOK_TILE

    def body(a_ref, w_ref, *rest):
        rest[-1][...] = _dot_nt(a_ref[...].astype(BF16), w_ref[...])

    extra = [] if after is None else [after]
    return _pallas(body, name="matmul_nt", grid=(S // tm,),
                   in_specs=[pl.BlockSpec((tm, K), lambda i: (i, 0)), pl.BlockSpec((M, K), lambda i: (0, 0))] + [ANY] * len(extra),
                   out_specs=pl.BlockSpec((tm, M), lambda i: (i, 0)), out_shape=_sds((S, M), F32),
                   compiler_params=_cp(("parallel",)))(a, w, *extra)


def _matmul_tn(a, b, tn):
    S, M = a.shape
    N = b.shape[1]
    tm = min(DW_TILE, S)
    nt = S // tm

    def body(a_ref, b_ref, o_ref, acc_ref):
        i = pl.program_id(1)

        @pl.when(i == 0)
        def _():
            acc_ref[...] = jnp.zeros_like(acc_ref)

        acc_ref[...] += _dot_tn(a_ref[...].astype(BF16), b_ref[...].astype(BF16))

        @pl.when(i == nt - 1)
        def _():
            o_ref[0] = acc_ref[...].astype(BF16)

    return _pallas(body, name="matmul_tn", grid=(N // tn, nt),
                   in_specs=[pl.BlockSpec((tm, M), lambda j, i: (i, 0)), pl.BlockSpec((tm, tn), lambda j, i: (i, j))],
                   out_specs=pl.BlockSpec((1, M, tn), lambda j, i: (j, 0, 0)), out_shape=_sds((N // tn, M, tn), BF16),
                   scratch_shapes=[pltpu.VMEM((M, tn), F32)],
                   compiler_params=_cp(("parallel", "arbitrary")))(a, b)


ALL_SHIFTS = tuple(range(SUBLANES))


def _fill_shifts(win, rows, shifts=ALL_SHIFTS):
    for i, b in enumerate(shifts):
        if b:
            win[i, pl.ds(0, rows - SUBLANES), :] = win[0, pl.ds(b, rows - SUBLANES), :]


def _tap(win, offset, n, base=0, shifts=ALL_SHIFTS):
    start = base + (offset - offset % SUBLANES)
    if not isinstance(start, int):
        start = pl.multiple_of(start, SUBLANES)
    return win[shifts.index(offset % SUBLANES), pl.ds(start, n), :]


def _conv_a_fwd(z, cw, cb, cn):
    S = z.shape[0]
    C = D_CONV
    tm = TOK_TILE
    hb = tm // HALO_A

    def body(u_ref, gt_ref, up_ref, gp_ref, cw_ref, cb_ref, cn_ref, a_ref, a1_ref, win):
        i = pl.program_id(0)
        prev = up_ref[...].astype(F32) * jax.nn.sigmoid(gp_ref[...].astype(F32))
        win[0, pl.ds(0, HALO_A), :] = jnp.where(i == 0, 0.0, prev)
        win[0, pl.ds(HALO_A, tm), :] = u_ref[...].astype(F32) * jax.nn.sigmoid(gt_ref[...].astype(F32))
        _fill_shifts(win, tm + HALO_A)

        acc = jnp.zeros((tm, C), F32)
        for k in range(CONV_A_WIDTH):
            acc = acc + cw_ref[k:k + 1, :] * _tap(win, HALO_A - (CONV_A_WIDTH - 1) + k, tm)
        a1 = acc + cb_ref[...]
        a1_ref[...] = a1
        a2 = a1 * _rms(a1) * cn_ref[...]
        a_ref[...] = (a2 * jax.nn.sigmoid(a2)).astype(BF16)

    cur = lambda c: pl.BlockSpec((tm, C), lambda i, c=c: (i, c))
    prv = lambda c: pl.BlockSpec((HALO_A, C), lambda i, c=c: (jnp.maximum(i * hb - 1, 0), c))
    vec = pl.BlockSpec((1, C), lambda i: (0, 0))
    return _pallas(body, name="conv_a_fwd", grid=(S // tm,),
                   in_specs=[cur(0), cur(1), prv(0), prv(1), pl.BlockSpec((32, C), lambda i: (0, 0)), vec, vec],
                   out_specs=[pl.BlockSpec((tm, C), lambda i: (i, 0)), pl.BlockSpec((tm, C), lambda i: (i, 0))],
                   out_shape=[_sds((S, C), BF16), _sds((S, C), F32)],
                   scratch_shapes=[pltpu.VMEM((SUBLANES, tm + HALO_A, C), F32)],
                   compiler_params=_cp(("parallel",)))(z, z, z, z, cw, cb, cn)


def _conv_a_bwd(da, a1, z, cw, cn):
    S = z.shape[0]
    C = D_CONV
    tm = TOK_TILE
    hb = tm // HALO_A
    nt = S // tm
    W = CONV_A_WIDTH

    def body(da_ref, a1_ref, dan_ref, a1n_ref, u_ref, gt_ref, up_ref, gp_ref, cw_ref, cn_ref,
             duz_ref, dcw_ref, dcb_ref, dcn_ref, win, dwin):
        i = pl.program_id(0)

        @pl.when(i == 0)
        def _():
            dcw_ref[...] = jnp.zeros_like(dcw_ref)
            dcb_ref[...] = jnp.zeros_like(dcb_ref)
            dcn_ref[...] = jnp.zeros_like(dcn_ref)

        cnv = cn_ref[...]

        def da1_of(dav, a1v):
            a2 = a1v * _rms(a1v) * cnv
            da2 = dav * _silu_grad(a2)
            dx, xh = _rms_bwd(da2, a1v, cnv)
            return dx, da2 * xh

        da1, dcn_t = da1_of(da_ref[...], a1_ref[...])
        da1n, _ = da1_of(dan_ref[...], a1n_ref[...])
        dwin[0, pl.ds(0, tm), :] = da1
        dwin[0, pl.ds(tm, HALO_A), :] = jnp.where(i == nt - 1, 0.0, da1n)
        _fill_shifts(dwin, tm + HALO_A)
        dcb_ref[...] += jnp.sum(da1, axis=0, keepdims=True)
        dcn_ref[...] += jnp.sum(dcn_t, axis=0, keepdims=True)

        prev = up_ref[...].astype(F32) * jax.nn.sigmoid(gp_ref[...].astype(F32))
        win[0, pl.ds(0, HALO_A), :] = jnp.where(i == 0, 0.0, prev)
        win[0, pl.ds(HALO_A, tm), :] = u_ref[...].astype(F32) * jax.nn.sigmoid(gt_ref[...].astype(F32))
        _fill_shifts(win, tm + HALO_A)

        def rows_block(rb, carry):
            r0 = pl.multiple_of(rb * CONV_ROWS, CONV_ROWS)
            rows = pl.ds(r0, CONV_ROWS)
            da1_b = dwin[0, rows, :]
            da0 = jnp.zeros((CONV_ROWS, C), F32)
            for k in range(W):
                da0 = da0 + cw_ref[k:k + 1, :] * _tap(dwin, W - 1 - k, CONV_ROWS, r0)
                dcw_ref[k:k + 1, :] += jnp.sum(da1_b * _tap(win, HALO_A - (W - 1) + k, CONV_ROWS, r0), axis=0, keepdims=True)
            u = u_ref[rows, :].astype(F32)
            sg = jax.nn.sigmoid(gt_ref[rows, :].astype(F32))
            duz_ref[rows, 0:C] = (da0 * sg).astype(BF16)
            duz_ref[rows, C:2 * C] = (da0 * u * sg * (1.0 - sg)).astype(BF16)
            return carry

        lax.fori_loop(0, tm // CONV_ROWS, rows_block, 0)

    cur = lambda c: pl.BlockSpec((tm, C), lambda i, c=c: (i, c))
    prv = lambda c: pl.BlockSpec((HALO_A, C), lambda i, c=c: (jnp.maximum(i * hb - 1, 0), c))
    nxt = pl.BlockSpec((HALO_A, C), lambda i: (jnp.minimum((i + 1) * hb, S // HALO_A - 1), 0))
    vec = pl.BlockSpec((1, C), lambda i: (0, 0))
    return _pallas(body, name="conv_a_bwd", grid=(nt,),
                   in_specs=[cur(0), cur(0), nxt, nxt, cur(0), cur(1), prv(0), prv(1),
                             pl.BlockSpec((32, C), lambda i: (0, 0)), vec],
                   out_specs=[pl.BlockSpec((tm, 2 * C), lambda i: (i, 0)), pl.BlockSpec((32, C), lambda i: (0, 0)), vec, vec],
                   out_shape=[_sds((S, 2 * C), BF16), _sds((32, C), F32), _sds((1, C), F32), _sds((1, C), F32)],
                   scratch_shapes=[pltpu.VMEM((SUBLANES, tm + HALO_A, C), F32)] * 2,
                   compiler_params=_cp(("arbitrary",)))(da, a1, da, a1, z, z, z, z, cw, cn)


def _forget_scan(fl, bf):
    S, L = fl.shape
    B = SCAN_BLK

    def body(fl_ref, bf_ref, flb_ref, F_ref):
        tri = (lax.broadcasted_iota(jnp.int32, (B, B), 0) >= lax.broadcasted_iota(jnp.int32, (B, B), 1)).astype(F32)

        def step(c, carry):
            rows = pl.ds(pl.multiple_of(c * B, B), B)
            v = fl_ref[rows, :] + bf_ref[...]
            flb_ref[rows, :] = v
            lf = jnp.minimum(v, 0.0) - jnp.log1p(jnp.exp(-jnp.abs(v)))
            cs = jnp.dot(tri, lf, precision=lax.Precision.HIGHEST, preferred_element_type=F32) + carry
            F_ref[rows, :] = cs
            return cs[B - 1:B, :]

        lax.fori_loop(0, S // B, step, jnp.zeros((1, L), F32))

    return _pallas(body, name="forget_scan", out_shape=[_sds((S, L), F32), _sds((S, L), F32)],
                   compiler_params=_cp())(fl, bf)


def _forget_scan_bwd(dF, flb):
    S, L = dF.shape
    B = SCAN_BLK
    nb = S // B

    def body(dF_ref, flb_ref, dfl_ref, db_ref):
        tri = (lax.broadcasted_iota(jnp.int32, (B, B), 0) <= lax.broadcasted_iota(jnp.int32, (B, B), 1)).astype(F32)

        def step(t, carry):
            carry_cs, db = carry
            rows = pl.ds(pl.multiple_of((nb - 1 - t) * B, B), B)
            cs = jnp.dot(tri, dF_ref[rows, :], precision=lax.Precision.HIGHEST, preferred_element_type=F32) + carry_cs
            dfl = cs * jax.nn.sigmoid(-flb_ref[rows, :])
            dfl_ref[rows, :] = dfl
            return cs[0:1, :], db + jnp.sum(dfl, axis=0, keepdims=True)

        _, db = lax.fori_loop(0, nb, step, (jnp.zeros((1, L), F32), jnp.zeros((1, L), F32)))
        db_ref[...] = db

    return _pallas(body, name="forget_scan_bwd", out_shape=[_sds((S, L), F32), _sds((1, L), F32)],
                   compiler_params=_cp())(dF, flb)


NEG = -1e30


def _causal_mask(t):
    return lax.broadcasted_iota(jnp.int32, (t, t), 0) >= lax.broadcasted_iota(jnp.int32, (t, t), 1)


AUG = 128
C_F, C_ONE, C_LSE = 64, 67, 70


def _split3(f):
    a = f.astype(BF16).astype(F32)
    r = f - a
    b = r.astype(BF16).astype(F32)
    return a, b, r - b


def _put3(lane, base, parts, other):
    out = other
    for k, p in enumerate(parts):
        out = jnp.where(lane == base + k, p, out)
    return out


def _ones3(lane, base):
    return (lane >= base) & (lane < base + 3)


def _lane_ids(rows):
    return lax.broadcasted_iota(jnp.int32, (rows, AUG), 1)


def _pair_rms(x, lo):
    sq = x * x
    ms_a = jnp.sum(jnp.where(lo, sq, 0.0), axis=-1, keepdims=True) * (1.0 / HEAD_DIM)
    ms_b = jnp.sum(jnp.where(lo, 0.0, sq), axis=-1, keepdims=True) * (1.0 / HEAD_DIM)
    return jnp.where(lo, lax.rsqrt(ms_a + EPS), lax.rsqrt(ms_b + EPS))


def _qkv_prep(z, Fc, qw, kw):
    S = z.shape[0]
    tp = min(QKN_TILE, S)
    scale = 1.0 / math.sqrt(HEAD_DIM)

    def body(zq_ref, zk_ref, zv_ref, F_ref, qw_ref, kw_ref, q_ref, k_ref, v_ref):
        j = pl.program_id(0)
        lane = _lane_ids(tp)
        lo = lane < HEAD_DIM
        Fv = F_ref[...]
        xq = zq_ref[...].astype(F32)
        xk = zk_ref[...].astype(F32)
        qn = xq * _pair_rms(xq, lo) * qw_ref[...] * scale
        kn = xk * _pair_rms(xk, lo) * kw_ref[...]
        vv = zv_ref[...].astype(F32)
        for half in range(2):
            take = (lambda a: a) if half == 0 else (lambda a: pltpu.roll(a, HEAD_DIM, 1))
            fp = _split3(jnp.sum(jnp.where(lane == 2 * j + half, Fv, 0.0), axis=-1, keepdims=True))
            qx = _put3(lane, C_F, fp, jnp.where(_ones3(lane, C_ONE), 1.0, 0.0))
            kx = _put3(lane, C_ONE, [-p for p in fp], jnp.where(_ones3(lane, C_F) | _ones3(lane, C_LSE), 1.0, 0.0))
            vx = jnp.where(_ones3(lane, C_F), 1.0, 0.0)
            q_ref[half] = jnp.where(lo, take(qn), qx).astype(BF16)
            k_ref[half] = jnp.where(lo, take(kn), kx).astype(BF16)
            v_ref[half] = jnp.where(lo, take(vv), vx).astype(BF16)

    col = lambda c0: pl.BlockSpec((tp, AUG), lambda j, i, c0=c0: (i, c0 + j))
    vec = pl.BlockSpec((1, AUG), lambda j, i: (0, 0))
    out = pl.BlockSpec((2, tp, AUG), lambda j, i: (j, i, 0))
    return _pallas(body, name="qkv_prep", grid=(N_HEADS // 2, S // tp),
                   in_specs=[col(8), col(12), col(16), pl.BlockSpec((tp, AUG), lambda j, i: (i, 0)), vec, vec],
                   out_specs=[out, out, out], out_shape=[_sds((N_HEADS, S, AUG), BF16)] * 3,
                   compiler_params=_cp(("parallel", "parallel")))(z, z, z, Fc, qw, kw)


def _fox_fwd(q_aug, k_aug, v_aug):
    H, S, A = q_aug.shape
    t = ATT_TILE
    nq = S // t

    def body(q_ref, k_ref, v_ref, o_ref, q2_ref):
        i = pl.program_id(1)
        q = q_ref[0]

        def tile(j, carry, diag):
            m, acc = carry
            rows = pl.ds(pl.multiple_of(j * t, t), t)
            s = _dot_nt(q, k_ref[0, rows, :])
            if diag:
                s = jnp.where(_causal_mask(t), s, NEG)
            m_new = jnp.maximum(m, jnp.max(s, axis=-1, keepdims=True))
            p = jnp.exp(s - m_new)
            acc = jnp.exp(m - m_new) * acc + _dot(p.astype(BF16), v_ref[0, rows, :])
            return m_new, acc

        init = (jnp.full((t, 1), NEG, F32), jnp.zeros((t, A), F32))
        carry = lax.fori_loop(0, i, lambda j, c: tile(j, c, False), init)
        m, acc = tile(i, carry, True)
        lane = _lane_ids(t)
        l = jnp.sum(jnp.where(lane == C_F, acc, 0.0), axis=-1, keepdims=True)
        o_ref[0] = (acc / l).astype(BF16)
        lse = m + jnp.log(l)
        q2_ref[0] = (q.astype(F32) + _put3(lane, C_LSE, [-p for p in _split3(lse)], 0.0)).astype(BF16)

    qblk = pl.BlockSpec((1, t, A), lambda h, i: (h, i, 0))
    full = pl.BlockSpec((1, S, A), lambda h, i: (h, 0, 0))
    return _pallas(body, name="fox_fwd", grid=(H, nq), in_specs=[qblk, full, full], out_specs=[qblk, qblk],
                   out_shape=[_sds((H, S, A), BF16)] * 2, compiler_params=_cp(("parallel", "parallel")))(q_aug, k_aug, v_aug)


def _do_prep(dcat, o_aug):
    S = dcat.shape[0]
    tp = min(QKN_TILE, S)

    def body(d_ref, o_ref, out_ref):
        lane = _lane_ids(tp)
        lo = lane < HEAD_DIM
        x = d_ref[...]
        for half in range(2):
            d = jnp.where(lo, x if half == 0 else pltpu.roll(x, HEAD_DIM, 1), 0.0)
            delta = jnp.sum(d * o_ref[half].astype(F32), axis=-1, keepdims=True)
            out_ref[half] = jnp.where(lo, d, _put3(lane, C_F, [-p for p in _split3(delta)], 0.0)).astype(BF16)

    pair = pl.BlockSpec((2, tp, AUG), lambda j, i: (j, i, 0))
    return _pallas(body, name="do_prep", grid=(N_HEADS // 2, S // tp),
                   in_specs=[pl.BlockSpec((tp, AUG), lambda j, i: (i, D_CONV // AUG + j)), pair], out_specs=pair,
                   out_shape=_sds((N_HEADS, S, AUG), BF16), compiler_params=_cp(("parallel", "parallel")))(dcat, o_aug)


def _fox_bwd(q2, k_aug, v_aug, do_aug):
    H, S, A = q2.shape
    t = ATT_TILE
    nq = S // t

    def body(q_ref, k_ref, v_ref, do_ref, dq_ref, dk_ref, dv_ref):
        j = pl.program_id(1)

        @pl.when(j == 0)
        def _():
            dq_ref[...] = jnp.zeros_like(dq_ref)

        k = k_ref[0]
        vv = v_ref[0]

        def tile(i, carry, diag):
            dk, dv = carry
            rows = pl.ds(pl.multiple_of(i * t, t), t)
            q = q_ref[0, rows, :]
            dov = do_ref[0, rows, :]
            s = _dot_nt(q, k)
            if diag:
                s = jnp.where(_causal_mask(t), s, NEG)
            p = jnp.exp(s)
            dv = dv + _dot_tn(p.astype(BF16), dov)
            dsb = (p * _dot_nt(dov, vv)).astype(BF16)
            dq_ref[0, rows, :] += _dot(dsb, k)
            dk = dk + _dot_tn(dsb, q)
            return dk, dv

        init = (jnp.zeros((t, A), F32), jnp.zeros((t, A), F32))
        carry = tile(j, init, True)
        dk, dv = lax.fori_loop(j + 1, nq, lambda i, c: tile(i, c, False), carry)
        dk_ref[0] = dk
        dv_ref[0] = dv

    full = pl.BlockSpec((1, S, A), lambda h, j: (h, 0, 0))
    kblk = pl.BlockSpec((1, t, A), lambda h, j: (h, j, 0))
    return _pallas(body, name="fox_bwd", grid=(H, nq), in_specs=[full, kblk, kblk, full], out_specs=[full, kblk, kblk],
                   out_shape=[_sds((H, S, A), F32)] * 3,
                   compiler_params=_cp(("parallel", "arbitrary")))(q2, k_aug, v_aug, do_aug)


def _qkv_bwd(dq, dk, dv, z, qw, kw):
    S = z.shape[0]
    tp = min(QKN_TILE, S)
    scale = 1.0 / math.sqrt(HEAD_DIM)

    def body(dq_ref, dk_ref, dv_ref, zq_ref, zk_ref, qw_ref, kw_ref, dqf_ref, dkf_ref, dvf_ref, dF_ref, dqw_ref, dkw_ref):
        i, j = pl.program_id(0), pl.program_id(1)
        lane = _lane_ids(tp)
        lo = lane < HEAD_DIM

        @pl.when((i == 0) & (j == 0))
        def _():
            dqw_ref[...] = jnp.zeros_like(dqw_ref)
            dkw_ref[...] = jnp.zeros_like(dkw_ref)

        def pair(ref):
            return jnp.where(lo, ref[0], pltpu.roll(ref[1], HEAD_DIM, 1))

        def norm_bwd(g, x, w):
            r = _pair_rms(x, lo)
            xh = x * r
            dxh = g * w
            tt = dxh * xh
            mean_a = jnp.sum(jnp.where(lo, tt, 0.0), axis=-1, keepdims=True) * (1.0 / HEAD_DIM)
            mean_b = jnp.sum(jnp.where(lo, 0.0, tt), axis=-1, keepdims=True) * (1.0 / HEAD_DIM)
            return r * (dxh - xh * jnp.where(lo, mean_a, mean_b)), g * xh

        dxq, gq = norm_bwd(pair(dq_ref) * scale, zq_ref[...].astype(F32), qw_ref[...])
        dqf_ref[...] = dxq.astype(BF16)
        dqw_ref[...] += jnp.sum(gq, axis=0, keepdims=True)
        dxk, gk = norm_bwd(pair(dk_ref), zk_ref[...].astype(F32), kw_ref[...])
        dkf_ref[...] = dxk.astype(BF16)
        dkw_ref[...] += jnp.sum(gk, axis=0, keepdims=True)
        dvf_ref[...] = pair(dv_ref).astype(BF16)

        contrib = jnp.zeros((tp, AUG), F32)
        for half in range(2):
            df = (jnp.sum(jnp.where(lane == C_F, dq_ref[half], 0.0), axis=-1, keepdims=True)
                  - jnp.sum(jnp.where(lane == C_ONE, dk_ref[half], 0.0), axis=-1, keepdims=True))
            contrib = jnp.where(lane == 2 * j + half, df, contrib)

        @pl.when(j == 0)
        def _():
            dF_ref[...] = contrib

        @pl.when(j > 0)
        def _():
            dF_ref[...] += contrib

    pairb = pl.BlockSpec((2, tp, AUG), lambda i, j: (j, i, 0))
    col = lambda c0: pl.BlockSpec((tp, AUG), lambda i, j, c0=c0: (i, c0 + j))
    vec = pl.BlockSpec((1, AUG), lambda i, j: (0, 0))
    flat = pl.BlockSpec((tp, AUG), lambda i, j: (i, j))
    return _pallas(body, name="qkv_bwd", grid=(S // tp, N_HEADS // 2),
                   in_specs=[pairb, pairb, pairb, col(8), col(12), vec, vec],
                   out_specs=[flat, flat, flat, pl.BlockSpec((tp, AUG), lambda i, j: (i, 0)), vec, vec],
                   out_shape=[_sds((S, D_ATTN), BF16)] * 3 + [_sds((S, AUG), F32), _sds((1, AUG), F32), _sds((1, AUG), F32)],
                   compiler_params=_cp(("arbitrary", "arbitrary")))(dq, dk, dv, z, z, qw, kw)


def _proj_res_heads(a, wa, o_aug, wo, res):
    S, D = res.shape
    H = o_aug.shape[0]
    tm = TOK_TILE

    def body(a_ref, wa_ref, o_ref, wo_ref, res_ref, out_ref):
        acc = res_ref[...] + _dot(a_ref[...], wa_ref[...])
        for h in range(H):
            acc = acc + _dot(o_ref[h], wo_ref[h])
        out_ref[...] = acc

    row = pl.BlockSpec((tm, D), lambda i: (i, 0))
    return _pallas(body, name="proj_res_heads", grid=(S // tm,),
                   in_specs=[pl.BlockSpec((tm, a.shape[1]), lambda i: (i, 0)), pl.BlockSpec(wa.shape, lambda i: (0, 0)),
                             pl.BlockSpec((H, tm, AUG), lambda i: (0, i, 0)), pl.BlockSpec(wo.shape, lambda i: (0, 0, 0)), row],
                   out_specs=row, out_shape=_sds((S, D), F32), compiler_params=_cp(("parallel",)))(a, wa, o_aug, wo, res)


def _heads_tn(o_aug, d):
    H, S, A = o_aug.shape
    D = d.shape[1]
    tm = min(DW_TILE, S)
    nt = S // tm

    def body(o_ref, d_ref, out_ref, acc_ref):
        i = pl.program_id(0)

        @pl.when(i == 0)
        def _():
            acc_ref[...] = jnp.zeros_like(acc_ref)

        dv = d_ref[...].astype(BF16)
        for h in range(H):
            acc_ref[h] += _dot_tn(o_ref[h], dv)

        @pl.when(i == nt - 1)
        def _():
            out_ref[...] = acc_ref[...].astype(BF16)

    return _pallas(body, name="heads_tn", grid=(nt,),
                   in_specs=[pl.BlockSpec((H, tm, A), lambda i: (0, i, 0)), pl.BlockSpec((tm, D), lambda i: (i, 0))],
                   out_specs=pl.BlockSpec((H, A, D), lambda i: (0, 0, 0)), out_shape=_sds((H, A, D), BF16),
                   scratch_shapes=[pltpu.VMEM((H, A, D), F32)], compiler_params=_cp(("arbitrary",)))(o_aug, d)


def _odd_mid_fwd(z, cw):
    S = z.shape[0]
    D = z.shape[1] // 3
    tm = TOK_TILE
    hb = tm // HALO_C
    W = CONV_C_WIDTH

    def body(gb_ref, gc_ref, hh_ref, gcp_ref, hhp_ref, cw_ref, y_ref, win):
        i = pl.program_id(0)
        prev = gcp_ref[...].astype(F32) * hhp_ref[...].astype(F32)
        win[pl.ds(0, HALO_C), :] = jnp.where(i == 0, 0.0, prev)
        win[pl.ds(HALO_C, tm), :] = gc_ref[...].astype(F32) * hh_ref[...].astype(F32)
        c1 = jnp.zeros((tm, D), F32)
        for k in range(W):
            c1 = c1 + cw_ref[k:k + 1, :] * win[pl.ds(HALO_C - (W - 1) + k, tm), :]
        y_ref[...] = (gb_ref[...].astype(F32) * c1).astype(BF16)

    cur = lambda c: pl.BlockSpec((tm, D), lambda i, c=c: (i, c))
    prv = lambda c: pl.BlockSpec((HALO_C, D), lambda i, c=c: (jnp.maximum(i * hb - 1, 0), c))
    return _pallas(body, name="odd_mid_fwd", grid=(S // tm,),
                   in_specs=[cur(0), cur(1), cur(2), prv(1), prv(2), pl.BlockSpec((8, D), lambda i: (0, 0))],
                   out_specs=pl.BlockSpec((tm, D), lambda i: (i, 0)), out_shape=_sds((S, D), BF16),
                   scratch_shapes=[pltpu.VMEM((tm + HALO_C, D), F32)],
                   compiler_params=_cp(("parallel",)))(z, z, z, z, z, cw)


def _odd_mid_bwd(dy, z, cw):
    S = z.shape[0]
    D = z.shape[1] // 3
    tm = TOK_TILE
    hb = tm // HALO_C
    nt = S // tm
    W = CONV_C_WIDTH
    shifts_w = (0,) + tuple(sorted({(HALO_C - (W - 1) + k) % SUBLANES for k in range(W)} - {0}))
    shifts_d = (0,) + tuple(sorted({(W - 1 - k) % SUBLANES for k in range(W)} - {0}))

    def body(dy_ref, dyn_ref, gb_ref, gbn_ref, gc_ref, hh_ref, gcp_ref, hhp_ref, cw_ref, dz_ref, dcw_ref, win, dwin):
        i = pl.program_id(0)

        @pl.when(i == 0)
        def _():
            dcw_ref[...] = jnp.zeros_like(dcw_ref)

        prev = gcp_ref[...].astype(F32) * hhp_ref[...].astype(F32)
        win[0, pl.ds(0, HALO_C), :] = jnp.where(i == 0, 0.0, prev)
        win[0, pl.ds(HALO_C, tm), :] = gc_ref[...].astype(F32) * hh_ref[...].astype(F32)
        _fill_shifts(win, tm + HALO_C, shifts_w)
        dwin[0, pl.ds(0, tm), :] = dy_ref[...] * gb_ref[...].astype(F32)
        dwin[0, pl.ds(tm, HALO_C), :] = jnp.where(i == nt - 1, 0.0, dyn_ref[...] * gbn_ref[...].astype(F32))
        _fill_shifts(dwin, tm + HALO_C, shifts_d)

        def rows_block(rb, carry):
            r0 = pl.multiple_of(rb * ODD_ROWS, ODD_ROWS)
            rows = pl.ds(r0, ODD_ROWS)
            dc1 = dwin[0, rows, :]
            c1 = jnp.zeros((ODD_ROWS, D), F32)
            dc0 = jnp.zeros((ODD_ROWS, D), F32)
            for k in range(W):
                tap = _tap(win, HALO_C - (W - 1) + k, ODD_ROWS, r0, shifts_w)
                c1 = c1 + cw_ref[k:k + 1, :] * tap
                dc0 = dc0 + cw_ref[k:k + 1, :] * _tap(dwin, W - 1 - k, ODD_ROWS, r0, shifts_d)
                dcw_ref[k:k + 1, :] += jnp.sum(dc1 * tap, axis=0, keepdims=True)
            dz_ref[rows, 0:D] = (dy_ref[rows, :] * c1).astype(BF16)
            dz_ref[rows, D:2 * D] = (dc0 * hh_ref[rows, :].astype(F32)).astype(BF16)
            dz_ref[rows, 2 * D:3 * D] = (dc0 * gc_ref[rows, :].astype(F32)).astype(BF16)
            return carry

        lax.fori_loop(0, tm // ODD_ROWS, rows_block, 0)

    cur = lambda c: pl.BlockSpec((tm, D), lambda i, c=c: (i, c))
    prv = lambda c: pl.BlockSpec((HALO_C, D), lambda i, c=c: (jnp.maximum(i * hb - 1, 0), c))
    nxt = pl.BlockSpec((HALO_C, D), lambda i: (jnp.minimum((i + 1) * hb, S // HALO_C - 1), 0))
    return _pallas(body, name="odd_mid_bwd", grid=(nt,),
                   in_specs=[cur(0), nxt, cur(0), nxt, cur(1), cur(2), prv(1), prv(2), pl.BlockSpec((8, D), lambda i: (0, 0))],
                   out_specs=[pl.BlockSpec((tm, 3 * D), lambda i: (i, 0)), pl.BlockSpec((8, D), lambda i: (0, 0))],
                   out_shape=[_sds((S, 3 * D), BF16), _sds((8, D), F32)],
                   scratch_shapes=[pltpu.VMEM((len(shifts_w), tm + HALO_C, D), F32),
                                   pltpu.VMEM((len(shifts_d), tm + HALO_C, D), F32)],
                   compiler_params=_cp(("arbitrary",)))(dy, dy, z, z, z, z, z, z, cw)


def _loss_head(y, tgt):
    S, D = y.shape
    tm = TOK_TILE

    def body(y_ref, t_ref, dy_ref, l_ref):
        @pl.when(pl.program_id(0) == 0)
        def _():
            l_ref[...] = jnp.zeros_like(l_ref)

        e = y_ref[...] - t_ref[...]
        dy_ref[...] = e * (1.0 / D)
        l_ref[...] += jnp.sum(jnp.sum(e * e, axis=-1, keepdims=True), axis=0, keepdims=True) * (0.5 / D)

    row = pl.BlockSpec((tm, D), lambda i: (i, 0))
    return _pallas(body, name="loss_head", grid=(S // tm,), in_specs=[row, row],
                   out_specs=[row, pl.BlockSpec((1, 1), lambda i: (0, 0))],
                   out_shape=[_sds((S, D), F32), _sds((1, 1), F32)],
                   compiler_params=_cp(("arbitrary",)))(y, tgt)


def _pad_rows(a, rows):
    return jnp.pad(a, ((0, rows - a.shape[0]), (0, 0)))


def _local_step(x, tgt, W, need=lambda block, after: None, done=lambda block, block_grads: None):
    S, D = x.shape
    grads = {}
    saved = {}

    def gain_after(gain, token):
        return gain if token is None else gain + token

    def ffn_f(tag, l, xin):
        need((tag, l), xin)
        out, xn, G, U = _ffn_fwd(xin, W[tag + "_norm"][l:l + 1], W[tag + "_w_gate"][l], W[tag + "_w_up"][l],
                                 W[tag + "_w_down"][l])
        saved[(tag, l)] = (xin, xn, G, U)
        return out

    def ffn_b(tag, l, dout):
        xin, xn, G, U = saved[(tag, l)]
        keys = [(tag + "_w_gate", l), (tag + "_w_up", l), (tag + "_w_down", l)]
        *dws, dG, dU = _ffn_bwd_w(dout, xn, G, U, W[tag + "_w_down"][l])
        big = dict(zip(keys, dws))
        grads.update(big)
        token = done((tag, l), big)
        dx, dg = _norm_in_bwd([dG, dU], [W[tag + "_w_gate"][l], W[tag + "_w_up"][l]], xin,
                              gain_after(W[tag + "_norm"][l:l + 1], token), dout, w_rows=True)
        grads[(tag + "_norm", l)] = dg
        return dx

    x0a = ffn_f("ffn1", 0, x)
    need(("ev", 0), x0a)
    w_in = W["ev_w_in"]
    w_main, w_f = w_in[:, :2560], jnp.pad(w_in[:, 2560:], ((0, 0), (0, 120)))
    h0, z0, fl = _norm_proj(x0a, W["mix_norm"][0:1], w_main, w_f)
    cw_a = _pad_rows(W["ev_conv_w"], 32)
    a_act, a1 = _conv_a_fwd(z0, cw_a, W["ev_conv_b"], W["ev_conv_norm"])
    flb, Fc = _forget_scan(fl, jnp.pad(W["ev_b_f"], ((0, 0), (0, 120))))
    qw2, kw2 = jnp.tile(W["ev_q_norm"], (1, 2)), jnp.tile(W["ev_k_norm"], (1, 2))
    q_aug, k_aug, v_aug = _qkv_prep(z0, Fc, qw2, kw2)
    o_aug, q_lse = _fox_fwd(q_aug, k_aug, v_aug)
    w_out_e = W["ev_w_out"]
    w_out_o = jnp.pad(w_out_e[D_CONV:].reshape(N_HEADS, HEAD_DIM, D), ((0, 0), (0, AUG - HEAD_DIM), (0, 0)))
    x0b = _proj_res_heads(a_act, w_out_e[:D_CONV], o_aug, w_out_o, x0a)
    x0c = ffn_f("ffn2", 0, x0b)
    x1a = ffn_f("ffn1", 1, x0c)
    need(("od", 0), x1a)
    h1, z1 = _norm_proj(x1a, W["mix_norm"][1:2], W["od_w_in"])
    cw_c = _pad_rows(W["od_conv_w"], 8)
    y1 = _odd_mid_fwd(z1, cw_c)
    x1b = _proj_res([y1], [W["od_w_out"]], x1a)
    x1c = ffn_f("ffn2", 1, x1b)
    dy, loss = _loss_head(x1c, tgt)

    d = ffn_b("ffn2", 1, dy)
    dy1 = _matmul_nt(d, W["od_w_out"])
    grads[("od_w_out", 0)] = _matmul_tn(y1, d, D)[0]
    dz1, dcw_c = _odd_mid_bwd(dy1, z1, cw_c)
    grads[("od_conv_w", 0)] = dcw_c[:CONV_C_WIDTH]
    grads[("od_w_in", 0)] = _matmul_tn(h1, dz1, 3 * D // 4)
    token = done(("od", 0), {k: grads[k] for k in (("od_w_out", 0), ("od_w_in", 0))})
    d, dg = _norm_in_bwd([dz1[None]], [W["od_w_in"][None]], x1a, gain_after(W["mix_norm"][1:2], token), d)
    grads[("mix_norm", 1)] = dg
    d = ffn_b("ffn1", 1, d)
    d = ffn_b("ffn2", 0, d)
    dcat = _matmul_nt(d, w_out_e)
    grads[("ev_w_out", 0)] = jnp.concatenate([_matmul_tn(a_act, d, D)[0],
                                              _heads_tn(o_aug, d)[:, :HEAD_DIM].reshape(D_ATTN, D)], axis=0)
    duz, dcw_a, dcb, dcn = _conv_a_bwd(dcat, a1, z0, cw_a, W["ev_conv_norm"])
    grads[("ev_conv_w", 0)] = dcw_a[:CONV_A_WIDTH]
    grads[("ev_conv_b", 0)] = dcb
    grads[("ev_conv_norm", 0)] = dcn
    dq_a, dk_a, dv_a = _fox_bwd(q_lse, k_aug, v_aug, _do_prep(dcat, o_aug))
    dqf, dkf, dvf, dF, dqw, dkw = _qkv_bwd(dq_a, dk_a, dv_a, z0, qw2, kw2)
    grads[("ev_q_norm", 0)] = dqw[:, :HEAD_DIM] + dqw[:, HEAD_DIM:]
    grads[("ev_k_norm", 0)] = dkw[:, :HEAD_DIM] + dkw[:, HEAD_DIM:]
    dfl, dbf = _forget_scan_bwd(dF, flb)
    grads[("ev_b_f", 0)] = dbf[:, :N_HEADS]
    dz0 = jnp.concatenate([duz, dqf, dkf, dvf], axis=1)
    dflb = dfl.astype(BF16)
    gmain = _matmul_tn(h0, dz0, 640)
    gmain = gmain.transpose(1, 0, 2).reshape(D, 2560)
    gf = _matmul_tn(h0, dflb, 128)[0][:, :N_HEADS]
    grads[("ev_w_in", 0)] = jnp.concatenate([gmain, gf], axis=1)
    token = done(("ev", 0), {k: grads[k] for k in (("ev_w_out", 0), ("ev_w_in", 0))})
    d, dg = _norm_in_bwd([dz0[None], dflb[None]], [w_main[None], w_f[None]], x0a, gain_after(W["mix_norm"][0:1], token), d)
    grads[("mix_norm", 0)] = dg
    d = ffn_b("ffn1", 0, d)
    return loss, d, grads


def _place():
    x, y, c = lax.axis_index("x"), lax.axis_index("y"), lax.axis_index("c")
    chips = [(1 - x, y), (x, 1 - y), (1 - x, 1 - y)]
    return x, y, c, chips


def _remote(src, dst, send_sem, recv_sem, to):
    return pltpu.make_async_remote_copy(src_ref=src, dst_ref=dst, send_sem=send_sem, recv_sem=recv_sem,
                                        device_id=to, device_id_type=MESH)


HBM = pl.BlockSpec(memory_space=pltpu.HBM)
SEM = pl.BlockSpec(memory_space=pltpu.SEMAPHORE)
EFFECT = pltpu.SideEffectType.DATAFLOW_SIDE_EFFECTING


def _in_hbm(a):
    return pltpu.with_memory_space_constraint(a, pltpu.HBM)


def _ag_start(tag, bufs, with_taps):
    n = len(bufs)
    order = ([n - 1] + list(range(n - 1))) if with_taps else list(range(n))

    def body(*refs):
        send_sems, recv_sems = refs[n], refs[n + 1]
        outs, token = refs[n + 2:2 * n + 2], refs[2 * n + 2]
        x, y, c, chips = _place()
        me = 2 * x + y
        for a in order:
            if with_taps and a == n - 1:
                blk = outs[a].at[me]
            else:
                h = outs[a].shape[1] // 2
                blk = outs[a].at[me, pl.ds(c * h, h)]
            for jj, (px, py) in enumerate(chips):
                _remote(blk, blk, send_sems.at[3 * a + jj], recv_sems.at[3 * a + jj], (px, py, c)).start()
        token[...] = jnp.zeros_like(token)

    return _pallas(
        body, name=f"gather_start_{tag}",
        out_shape=[pltpu.SemaphoreType.DMA((3 * n,)), pltpu.SemaphoreType.DMA((3 * n,))]
        + [pltpu.HBM(b.shape, b.dtype) for b in bufs] + [_sds((8, 128), F32)],
        in_specs=[HBM] * n, out_specs=[SEM, SEM] + [HBM] * n + [pl.BlockSpec(memory_space=pltpu.VMEM)],
        input_output_aliases={a: 2 + a for a in range(n)},
        compiler_params=pltpu.CompilerParams(has_side_effects=EFFECT),
    )(*[_in_hbm(b) for b in bufs])


def _ag_mid(g, ici_send, ici_recv, bufs, idx, taps, n_big, after):
    n = len(bufs)
    arrs = list(bufs) + ([taps] if taps is not None else [])
    m = len(arrs)

    def body(*refs):
        ici_s, ici_r = refs[0], refs[1]
        d_send, d_recv = refs[m + 3], refs[m + 4]
        outs = refs[m + 5:]
        x, y, c, chips = _place()
        me = 2 * x + y
        for i in range(m):
            a = idx[i] if i < n else n_big
            for jj, (px, py) in enumerate(chips):
                k = 3 * a + jj
                if i < n:
                    h = outs[i].shape[1] // 2
                    mine, blk = outs[i].at[me, pl.ds(c * h, h)], outs[i].at[2 * px + py, pl.ds(c * h, h)]
                else:
                    mine, blk = outs[i].at[me], outs[i].at[2 * px + py]
                _remote(mine, mine, ici_s.at[k], ici_r.at[k], (px, py, c)).wait_send()
                _remote(blk, blk, ici_s.at[k], ici_r.at[k], (px, py, c)).wait_recv()
                if i < n:
                    _remote(blk, blk, d_send.at[3 * i + jj], d_recv.at[3 * i + jj], (x, y, 1 - c)).start()

    return _pallas(
        body, name=f"gather_pass_on_{g}",
        out_shape=[pltpu.SemaphoreType.DMA((3 * n,)), pltpu.SemaphoreType.DMA((3 * n,))] + [pltpu.HBM(b.shape, b.dtype) for b in arrs],
        in_specs=[SEM, SEM] + [HBM] * m + [ANY], out_specs=[SEM, SEM] + [HBM] * m,
        input_output_aliases={2 + i: 2 + i for i in range(m)},
        compiler_params=pltpu.CompilerParams(has_side_effects=EFFECT),
    )(ici_send, ici_recv, *arrs, after)


def _ag_wait(g, d_send, d_recv, arrs, n, after):
    m = len(arrs)

    def body(*refs):
        d_s, d_r = refs[0], refs[1]
        outs = refs[m + 3:]
        x, y, c, chips = _place()
        for i in range(n):
            h = outs[i].shape[1] // 2
            for jj, (px, py) in enumerate(chips):
                sent = outs[i].at[2 * px + py, pl.ds(c * h, h)]
                got = outs[i].at[2 * px + py, pl.ds((1 - c) * h, h)]
                _remote(sent, sent, d_s.at[3 * i + jj], d_r.at[3 * i + jj], (x, y, 1 - c)).wait_send()
                _remote(got, got, d_s.at[3 * i + jj], d_r.at[3 * i + jj], (x, y, 1 - c)).wait_recv()

    return _pallas(
        body, name=f"gather_wait_{g}", out_shape=[pltpu.HBM(b.shape, b.dtype) for b in arrs],
        in_specs=[SEM, SEM] + [HBM] * m + [ANY], out_specs=[HBM] * m,
        input_output_aliases={2 + i: i for i in range(m)},
        compiler_params=pltpu.CompilerParams(has_side_effects=EFFECT),
    )(d_send, d_recv, *arrs, after)


def _pair_start(g, gs, after):
    n = len(gs)
    zones = [lax.empty((4, a.shape[1] // 2, a.shape[2]), a.dtype) for a in gs]
    extra = [] if after is None else [after]

    def body(*refs):
        k0 = 2 * n + len(extra)
        send_sems, recv_sems = refs[k0], refs[k0 + 1]
        src, dst = refs[k0 + 2:k0 + 2 + n], refs[k0 + 2 + n:k0 + 2 + 2 * n]
        token = refs[k0 + 2 + 2 * n]
        x, y, c, _ = _place()
        for a in range(n):
            h = src[a].shape[1] // 2
            _remote(src[a].at[:, pl.ds((1 - c) * h, h)], dst[a], send_sems.at[a], recv_sems.at[a], (x, y, 1 - c)).start()
        token[...] = jnp.zeros_like(token)

    return _pallas(
        body, name=f"grad_pair_start_{g}",
        out_shape=[pltpu.SemaphoreType.DMA((n,)), pltpu.SemaphoreType.DMA((n,))]
        + [pltpu.HBM(a.shape, a.dtype) for a in gs + zones] + [_sds((8, 128), F32)],
        in_specs=[HBM] * (2 * n) + [ANY] * len(extra),
        out_specs=[SEM, SEM] + [HBM] * (2 * n) + [pl.BlockSpec(memory_space=pltpu.VMEM)],
        input_output_aliases={i: 2 + i for i in range(2 * n)},
        compiler_params=pltpu.CompilerParams(has_side_effects=EFFECT),
    )(*[_in_hbm(a) for a in gs + zones], *extra)


def _pair_wait(g, send, recv, gs, zones):
    n = len(gs)

    def body(*refs):
        s_ref, r_ref = refs[0], refs[1]
        outs = refs[2 + 2 * n:]
        src, dst = outs[:n], outs[n:]
        x, y, c, _ = _place()
        for a in range(n):
            h = src[a].shape[1] // 2
            _remote(src[a].at[:, pl.ds((1 - c) * h, h)], dst[a], s_ref.at[a], r_ref.at[a], (x, y, 1 - c)).wait()

    return _pallas(
        body, name=f"grad_pair_wait_{g}", out_shape=[pltpu.HBM(a.shape, a.dtype) for a in gs + zones],
        in_specs=[SEM, SEM] + [HBM] * (2 * n), out_specs=[HBM] * (2 * n),
        input_output_aliases={2 + i: i for i in range(2 * n)},
        compiler_params=pltpu.CompilerParams(has_side_effects=EFFECT),
    )(send, recv, *gs, *zones)


def _pair_add(gs, others, c_arr):
    n = len(gs)

    def body(c_ref, *refs):
        for g_ref, o_ref, out_ref in zip(refs[:n], refs[n:2 * n], refs[2 * n:]):
            out_ref[...] = (g_ref[...].astype(F32) + o_ref[...].astype(F32)).astype(BF16)

    half = lambda a: pl.BlockSpec((1, a.shape[1] // 2, a.shape[2]), lambda k, c_ref: (k, c_ref[0], 0))
    whole = lambda a: pl.BlockSpec((1,) + a.shape[1:], lambda k, c_ref: (k, 0, 0))
    grid_spec = pltpu.PrefetchScalarGridSpec(
        num_scalar_prefetch=1, grid=(4,), in_specs=[half(a) for a in gs] + [whole(o) for o in others],
        out_specs=[whole(o) for o in others])
    return _pallas(body, name="grad_pair_add", grid_spec=grid_spec, out_shape=[_sds(o.shape, BF16) for o in others],
                   compiler_params=_cp(("parallel",)))(c_arr, *gs, *others)


def _chip_start(g, ss):
    n = len(ss)
    zones = [lax.empty((3,) + s.shape[1:], s.dtype) for s in ss]

    def body(*refs):
        send_sems, recv_sems = refs[2 * n], refs[2 * n + 1]
        src, dst = refs[2 * n + 2:3 * n + 2], refs[3 * n + 2:4 * n + 2]
        token = refs[4 * n + 2]
        x, y, c, chips = _place()
        for a in range(n):
            for jj, (px, py) in enumerate(chips):
                k = 3 * a + jj
                _remote(src[a].at[2 * px + py], dst[a].at[jj], send_sems.at[k], recv_sems.at[k], (px, py, c)).start()
        token[...] = jnp.zeros_like(token)

    return _pallas(
        body, name=f"grad_chip_start_{g}",
        out_shape=[pltpu.SemaphoreType.DMA((3 * n,)), pltpu.SemaphoreType.DMA((3 * n,))]
        + [pltpu.HBM(a.shape, a.dtype) for a in ss + zones] + [_sds((8, 128), F32)],
        in_specs=[HBM] * (2 * n), out_specs=[SEM, SEM] + [HBM] * (2 * n) + [pl.BlockSpec(memory_space=pltpu.VMEM)],
        input_output_aliases={i: 2 + i for i in range(2 * n)},
        compiler_params=pltpu.CompilerParams(has_side_effects=EFFECT),
    )(*[_in_hbm(a) for a in ss + zones])


def _chip_wait(tag, sends, recvs, counts, ss, zones, after):
    nb, n = len(sends), len(ss)

    def body(*refs):
        s_refs, r_refs = refs[:nb], refs[nb:2 * nb]
        outs = refs[2 * nb + 2 * n + 1:]
        src, dst = outs[:n], outs[n:]
        x, y, c, chips = _place()
        a = 0
        for b in range(nb):
            for i in range(counts[b]):
                for jj, (px, py) in enumerate(chips):
                    k = 3 * i + jj
                    _remote(src[a].at[2 * px + py], dst[a].at[jj], s_refs[b].at[k], r_refs[b].at[k], (px, py, c)).wait()
                a += 1

    return _pallas(
        body, name=f"grad_chip_wait_{tag}", out_shape=[pltpu.HBM(a.shape, a.dtype) for a in ss + zones],
        in_specs=[SEM] * (2 * nb) + [HBM] * (2 * n) + [ANY], out_specs=[HBM] * (2 * n),
        input_output_aliases={2 * nb + i: i for i in range(2 * n)},
        compiler_params=pltpu.CompilerParams(has_side_effects=EFFECT),
    )(*sends, *recvs, *ss, *zones, after)


def _chip_sum(s, r, where, dest, l, L):
    _, h, C = s.shape
    tr = h // 2

    def body(k_ref, s_ref, r_ref, *rest):
        out_ref = rest[-1]
        acc = s_ref[0].astype(F32)
        for jj in range(3):
            acc = acc + r_ref[jj].astype(F32)
        out_ref[...] = acc

    in_specs = [pl.BlockSpec((1, tr, C), lambda i, k_ref: (k_ref[0], i, 0)), pl.BlockSpec((3, tr, C), lambda i, k_ref: (0, i, 0))]
    args = [where, s, r]
    alias = {}
    if dest is not None:
        in_specs.append(ANY)
        args.append(dest)
        alias = {3: 0}
    grid_spec = pltpu.PrefetchScalarGridSpec(
        num_scalar_prefetch=1, grid=(2,), in_specs=in_specs,
        out_specs=pl.BlockSpec((None, tr, C), lambda i, k_ref: (l, 2 * k_ref[1] + i, 0)))
    return _pallas(body, name="grad_chip_sum", grid_spec=grid_spec, out_shape=_sds((L, 2 * h, C), F32),
                   input_output_aliases=alias, compiler_params=_cp(("arbitrary",)))(*args)


def _share_start(tag, bufs, layout):
    n, n_buf = len(layout), len(bufs)

    def body(*refs):
        send_sems, recv_sems = refs[n_buf], refs[n_buf + 1]
        outs = refs[n_buf + 2:]
        x, y, c, _ = _place()
        for a, (o, l) in enumerate(layout):
            h = outs[o].shape[1] // 2
            blk = outs[o].at[l, pl.ds(c * h, h)]
            _remote(blk, blk, send_sems.at[a], recv_sems.at[a], (x, y, 1 - c)).start()

    return _pallas(
        body, name=f"grad_share_start_{tag}",
        out_shape=[pltpu.SemaphoreType.DMA((n,)), pltpu.SemaphoreType.DMA((n,))] + [pltpu.HBM(b.shape, b.dtype) for b in bufs],
        in_specs=[HBM] * n_buf, out_specs=[SEM, SEM] + [HBM] * n_buf, input_output_aliases={o: 2 + o for o in range(n_buf)},
        compiler_params=pltpu.CompilerParams(has_side_effects=EFFECT),
    )(*[_in_hbm(b) for b in bufs])


def _share_wait(tag, send, recv, bufs, layout, after):
    n_buf = len(bufs)

    def body(*refs):
        s_ref, r_ref = refs[0], refs[1]
        outs = refs[n_buf + 3:]
        x, y, c, _ = _place()
        for a, (o, l) in enumerate(layout):
            h = outs[o].shape[1] // 2
            mine, theirs = outs[o].at[l, pl.ds(c * h, h)], outs[o].at[l, pl.ds((1 - c) * h, h)]
            _remote(mine, mine, s_ref.at[a], r_ref.at[a], (x, y, 1 - c)).wait_send()
            _remote(theirs, theirs, s_ref.at[a], r_ref.at[a], (x, y, 1 - c)).wait_recv()

    return _pallas(
        body, name=f"grad_share_wait_{tag}", out_shape=[pltpu.HBM(b.shape, b.dtype) for b in bufs],
        in_specs=[SEM, SEM] + [HBM] * n_buf + [ANY], out_specs=[HBM] * n_buf,
        input_output_aliases={2 + o: o for o in range(n_buf)},
        compiler_params=pltpu.CompilerParams(has_side_effects=EFFECT),
    )(send, recv, *bufs, after)


def _small_all_reduce(packed, after):
    P, L = packed.shape

    def body(in_ref, after_ref, out_ref, slots, send_sems, recv_sems):
        x, y, c, _ = _place()
        me = 4 * x + 2 * y + c
        slots[me] = in_ref[...]
        cps = []
        for r in range(1, 8):
            px = 1 - x if r & 4 else x
            py = 1 - y if r & 2 else y
            pc = 1 - c if r & 1 else c
            cps.append(_remote(in_ref, slots.at[me], send_sems.at[r - 1], recv_sems.at[r - 1], (px, py, pc)))
        for cp in cps:
            cp.start()
        for r in range(1, 8):
            px = 1 - x if r & 4 else x
            py = 1 - y if r & 2 else y
            pc = 1 - c if r & 1 else c
            blk = slots.at[4 * px + 2 * py + pc]
            _remote(blk, blk, send_sems.at[r - 1], recv_sems.at[r - 1], (px, py, pc)).wait_recv()
        for cp in cps:
            cp.wait_send()
        acc = slots[0]
        for k in range(1, 8):
            acc = acc + slots[k]
        out_ref[...] = acc

    vm = pl.BlockSpec(memory_space=pltpu.VMEM)
    return _pallas(body, name="small_all_reduce", in_specs=[vm, ANY], out_specs=vm, out_shape=_sds((P, L), F32),
                   scratch_shapes=[pltpu.VMEM((8, P, L), F32), pltpu.SemaphoreType.DMA((7,)),
                                   pltpu.SemaphoreType.DMA((7,))])(packed, after)


def _adamw_math(w, g, m, v):
    m = ADAM_B1 * m + (1.0 - ADAM_B1) * g
    v = ADAM_B2 * v + (1.0 - ADAM_B2) * (g * g)
    m_hat = m / (1.0 - ADAM_B1 ** ADAM_STEP)
    v_hat = v / (1.0 - ADAM_B2 ** ADAM_STEP)
    delta = -ADAM_LR * (m_hat / (jnp.sqrt(v_hat) + ADAM_EPS) + ADAM_WD * w)
    return delta, m, v


def _adamw(w, g, m, v):
    shape = w.shape
    C = shape[-1]
    rows = math.prod(shape[:-1])
    tr = next(t for t in (512, 352, 256, 128, 64, 32, 16, 8, rows) if rows % t == 0)
    w2, g2, m2, v2 = (a.reshape(rows, C) for a in (w, g, m, v))

    def body(w_ref, g_ref, m_ref, v_ref, go_ref, d_ref, nm_ref, nv_ref):
        gv = g_ref[...]
        d, nm, nv = _adamw_math(w_ref[...], gv, m_ref[...], v_ref[...])
        go_ref[...] = gv
        d_ref[...] = d
        nm_ref[...] = nm
        nv_ref[...] = nv

    blk = pl.BlockSpec((tr, C), lambda i: (i, 0))
    outs = _pallas(body, name="adamw", grid=(rows // tr,), in_specs=[blk] * 4, out_specs=[blk] * 4,
                   out_shape=[_sds((rows, C), F32)] * 4, compiler_params=_cp(("parallel",)))(w2, g2, m2, v2)
    return tuple(o.reshape(shape) for o in outs)


WEIGHTS = ["ffn1_norm", "ffn1_w_gate", "ffn1_w_up", "ffn1_w_down", "mix_norm", "ffn2_norm", "ffn2_w_gate", "ffn2_w_up",
           "ffn2_w_down", "ev_w_in", "ev_b_f", "ev_conv_w", "ev_conv_b", "ev_conv_norm", "ev_q_norm", "ev_k_norm",
           "ev_w_out", "od_w_in", "od_conv_w", "od_w_out"]
BIG = ([("ffn1_w_gate", 0), ("ffn1_w_up", 0), ("ffn1_w_down", 0), ("ev_w_in", 0), ("ev_w_out", 0),
        ("ffn2_w_gate", 0), ("ffn2_w_up", 0), ("ffn2_w_down", 0)]
       + [("ffn1_w_gate", 1), ("ffn1_w_up", 1), ("ffn1_w_down", 1), ("od_w_in", 0), ("od_w_out", 0),
          ("ffn2_w_gate", 1), ("ffn2_w_up", 1), ("ffn2_w_down", 1)])
TRANSPOSED = ("ffn1_w_gate", "ffn1_w_up", "ffn2_w_gate", "ffn2_w_up")
SHARED_LAST = ("ffn1_w_gate", "ffn1_w_up", "ffn1_w_down", "ev_w_in", "ev_w_out")
BLOCKS = [("ffn1", 0), ("ev", 0), ("ffn2", 0), ("ffn1", 1), ("od", 0), ("ffn2", 1)]
BLOCK_OF = {(name, l): (name.split("_w_")[0], l) for name, l in BIG}
BIG_NAMES = ["ffn1_w_gate", "ffn1_w_up", "ffn1_w_down", "ffn2_w_gate", "ffn2_w_up", "ffn2_w_down",
             "ev_w_in", "ev_w_out", "od_w_in", "od_w_out"]
SMALL = [("ffn1_norm", 16), ("mix_norm", 16), ("ffn2_norm", 16), ("ev_b_f", 8), ("ev_conv_w", 128), ("ev_conv_b", 8),
         ("ev_conv_norm", 8), ("ev_q_norm", 8), ("ev_k_norm", 8), ("od_conv_w", 24)]


def _to_lanes(a, rows):
    flat = a.reshape(-1)
    return jnp.pad(flat, (0, rows * 128 - flat.shape[0])).reshape(rows, 128)


def kernel(x, ffn1_norm, ffn1_w_gate, ffn1_w_up, ffn1_w_down, mix_norm, ffn2_norm, ffn2_w_gate, ffn2_w_up, ffn2_w_down, ev_w_in, ev_b_f, ev_conv_w, ev_conv_b, ev_conv_norm, ev_q_norm, ev_k_norm, ev_w_out, od_w_in, od_conv_w, od_w_out, loss_target, m_ffn1_norm, m_ffn1_w_gate, m_ffn1_w_up, m_ffn1_w_down, m_mix_norm, m_ffn2_norm, m_ffn2_w_gate, m_ffn2_w_up, m_ffn2_w_down, m_ev_w_in, m_ev_b_f, m_ev_conv_w, m_ev_conv_b, m_ev_conv_norm, m_ev_q_norm, m_ev_k_norm, m_ev_w_out, m_od_w_in, m_od_conv_w, m_od_w_out, v_ffn1_norm, v_ffn1_w_gate, v_ffn1_w_up, v_ffn1_w_down, v_mix_norm, v_ffn2_norm, v_ffn2_w_gate, v_ffn2_w_up, v_ffn2_w_down, v_ev_w_in, v_ev_b_f, v_ev_conv_w, v_ev_conv_b, v_ev_conv_norm, v_ev_q_norm, v_ev_k_norm, v_ev_w_out, v_od_w_in, v_od_conv_w, v_od_w_out):
    P = dict(ffn1_norm=ffn1_norm, ffn1_w_gate=ffn1_w_gate, ffn1_w_up=ffn1_w_up, ffn1_w_down=ffn1_w_down, mix_norm=mix_norm,
             ffn2_norm=ffn2_norm, ffn2_w_gate=ffn2_w_gate, ffn2_w_up=ffn2_w_up, ffn2_w_down=ffn2_w_down, ev_w_in=ev_w_in,
             ev_b_f=ev_b_f, ev_conv_w=ev_conv_w, ev_conv_b=ev_conv_b, ev_conv_norm=ev_conv_norm, ev_q_norm=ev_q_norm,
             ev_k_norm=ev_k_norm, ev_w_out=ev_w_out, od_w_in=od_w_in, od_conv_w=od_conv_w, od_w_out=od_w_out)
    M = dict(zip(WEIGHTS, [m_ffn1_norm, m_ffn1_w_gate, m_ffn1_w_up, m_ffn1_w_down, m_mix_norm, m_ffn2_norm, m_ffn2_w_gate,
                           m_ffn2_w_up, m_ffn2_w_down, m_ev_w_in, m_ev_b_f, m_ev_conv_w, m_ev_conv_b, m_ev_conv_norm,
                           m_ev_q_norm, m_ev_k_norm, m_ev_w_out, m_od_w_in, m_od_conv_w, m_od_w_out]))
    V = dict(zip(WEIGHTS, [v_ffn1_norm, v_ffn1_w_gate, v_ffn1_w_up, v_ffn1_w_down, v_mix_norm, v_ffn2_norm, v_ffn2_w_gate,
                           v_ffn2_w_up, v_ffn2_w_down, v_ev_w_in, v_ev_b_f, v_ev_conv_w, v_ev_conv_b, v_ev_conv_norm,
                           v_ev_q_norm, v_ev_k_norm, v_ev_w_out, v_od_w_in, v_od_conv_w, v_od_w_out]))
    for name in TRANSPOSED:
        P[name], M[name], V[name] = (jnp.swapaxes(a, 1, 2) for a in (P[name], M[name], V[name]))
    S, D = x.shape[1], x.shape[2]
    chip = 2 * lax.axis_index("x") + lax.axis_index("y")
    core = lax.axis_index("c")

    def own_slot(shard):
        return lax.dynamic_update_slice(lax.empty((4,) + shard.shape, shard.dtype), shard[None], (chip, 0, 0))

    taps = jnp.concatenate([_to_lanes(_pad_rows(ev_conv_w[0], 32), 32), _to_lanes(_pad_rows(od_conv_w[0], 8), 16)], axis=0)
    first = [i for i, k in enumerate(BIG) if BLOCK_OF[k] in BLOCKS[:2]]
    rest = [i for i in range(len(BIG)) if i not in first]
    send0, recv0, *bufs0 = _ag_start("first", [own_slot(P[BIG[i][0]][BIG[i][1]].astype(BF16)) for i in first]
                                     + [own_slot(taps)], True)
    zero = bufs0.pop()[0, 0]
    send1, recv1, *bufs1 = _ag_start("rest", [own_slot((P[BIG[i][0]][BIG[i][1]] + zero).astype(BF16)) for i in rest], False)
    bufs1.pop()
    cols = lambda a: a.transpose(1, 0, 2).reshape(a.shape[1], 4 * a.shape[2])
    W = {k: P[k] for k in ("ffn1_norm", "mix_norm", "ffn2_norm", "ev_b_f", "ev_q_norm", "ev_k_norm")}
    W["ev_conv_b"], W["ev_conv_norm"] = ev_conv_b, ev_conv_norm
    for tag in ("ffn1", "ffn2"):
        for kind in ("_w_gate", "_w_up", "_w_down"):
            W[tag + kind] = [None, None]
    passing = {}

    def pass_on(g, after):
        idx = [i for i, k in enumerate(BIG) if BLOCK_OF[k] == BLOCKS[g]]
        keys = [BIG[i] for i in idx] + (["taps"] if BLOCKS[g] == ("ev", 0) else [])
        send, recv, bufs, members = (send0, recv0, bufs0, first) if g < 2 else (send1, recv1, bufs1, rest)
        local = [members.index(i) for i in idx]
        passing[g] = (keys, _ag_mid(g, send, recv, [bufs[i] for i in local], local,
                                    bufs0[-1] if BLOCKS[g] == ("ev", 0) else None, len(first), after))

    def need(block, after):
        g = BLOCKS.index(block)
        if g not in passing:
            pass_on(g, bufs1[0] if g == 0 else after)
        keys, (d_send, d_recv, *thru) = passing.pop(g)
        got = dict(zip(keys, _ag_wait(g, d_send, d_recv, thru, len(keys) - ("taps" in keys), after)))
        if 1 <= g < len(BLOCKS) - 1:
            pass_on(g + 1, after)
        for key, a in got.items():
            if key == "taps":
                continue
            name, l = key
            if name.startswith("ffn"):
                W[name][l] = a
            elif name.endswith("_w_in"):
                W[name] = cols(a)
            elif name.endswith("_w_out"):
                W[name] = a.reshape(4 * a.shape[1], D)
        if block == ("ev", 0):
            taps_all = got["taps"]
            W["ev_conv_w"] = cols(taps_all[:, :32].reshape(4, 32, 128))[:CONV_A_WIDTH]
            W["od_conv_w"] = cols(taps_all[:, 32:48].reshape(4, 8, 256))[:CONV_C_WIDTH]

    rows = lambda a: a.reshape(4, a.shape[0] // 4, a.shape[1])
    colsh = lambda a: a.reshape(a.shape[0], 4, a.shape[1] // 4).transpose(1, 0, 2)
    c_arr = core.reshape(1).astype(jnp.int32)
    where = jnp.stack([chip, core]).astype(jnp.int32)
    in_flight = []

    def done(block, block_grads):
        g = BLOCKS.index(block)
        keys = list(block_grads)
        gs = []
        for name, l in keys:
            a = block_grads[(name, l)]
            gs.append(colsh(a) if name == "ev_w_in" else rows(a) if name.endswith("_w_out") else a)
        for item in list(pairs):
            to_chips(item)
        send, recv, *rest = _pair_start(g, gs, chained.get("token"))
        n = len(keys)
        pairs.append((g, keys, send, recv, rest[:n], rest[n:2 * n]))
        if g == 0:
            to_chips(pairs[0])
        chained["token"] = rest[-1] if g else chained["token"]
        return chained["token"][0:1, 0:1]

    pairs, chained = [], {}

    def to_chips(item):
        pairs.remove(item)
        g, keys, send, recv, gs, zones = item
        n = len(keys)
        done_ = _pair_wait(g, send, recv, gs, zones)
        sums = list(_pair_add(list(done_[:n]), list(done_[n:]), c_arr))
        send2, recv2, *rest = _chip_start(g, sums)
        in_flight.append((keys, send2, recv2, rest[:n], rest[n:2 * n]))
        chained["token"] = rest[-1]

    loss, grad_x, grads = _local_step(x[0], loss_target[0], W, need, done)

    stacked, shares = {}, []

    def land(tag, flights, after):
        keys = [k for ks, *_ in flights for k in ks]
        landed = _chip_wait(tag, [f[1] for f in flights], [f[2] for f in flights], [len(f[0]) for f in flights],
                            [a for f in flights for a in f[3]], [a for f in flights for a in f[4]], after)
        return list(zip(keys, landed[:len(keys)], landed[len(keys):]))

    def reduce(items, names):
        for (name, l), s, r in items:
            if name in names:
                stacked[name] = _chip_sum(s, r, where, stacked.get(name), l, P[name].shape[0])

    def share(tag, names):
        layout = [(o, l) for o, name in enumerate(names) for l in range(P[name].shape[0])]
        send, recv, *thru = _share_start(tag, [stacked[name] for name in names], layout)
        shares.append((tag, names, send, recv, thru, layout))

    names_a, names_b = [n for n in BIG_NAMES if n not in SHARED_LAST], list(SHARED_LAST)
    early = land("early", in_flight[:-1], grad_x)
    reduce(early, names_a)
    share("a", names_a)
    reduce(early, names_b)
    reduce(land("last", in_flight[-1:], stacked[names_b[-1]]), names_b)
    share("b", names_b)

    def small_grad(name):
        if name.endswith("_norm") and name[:3] in ("ffn", "mix"):
            return jnp.concatenate([grads[(name, 0)], grads[(name, 1)]], axis=0)
        return grads[(name, 0)]

    packed = jnp.concatenate([_to_lanes(small_grad(name), r) for name, r in SMALL], axis=0)
    total = _small_all_reduce(packed, shares[-1][4][0])
    small_grads, at = {}, 0
    for name, r in SMALL:
        part = total[at:at + r].reshape(-1)
        at += r
        if name == "ev_conv_w":
            full_g = part[:CONV_A_WIDTH * D_CONV].reshape(CONV_A_WIDTH, D_CONV)
            small_grads[name] = lax.dynamic_slice_in_dim(full_g, chip * (D_CONV // 4), D_CONV // 4, axis=1)[None]
        elif name == "od_conv_w":
            full_g = part[:CONV_C_WIDTH * D].reshape(CONV_C_WIDTH, D)
            small_grads[name] = lax.dynamic_slice_in_dim(full_g, chip * (D // 4), D // 4, axis=1)[None]
        else:
            small_grads[name] = part[:math.prod(P[name].shape)].reshape(P[name].shape)

    results = {}

    def update(name, g):
        outs = _adamw(P[name], g, M[name], V[name])
        results[name] = tuple(jnp.swapaxes(a, 1, 2) for a in outs) if name in TRANSPOSED else outs

    for name, _ in SMALL:
        update(name, small_grads[name])
    after = results[SMALL[-1][0]][1]
    for tag, names, send, recv, thru, layout in shares:
        for name, g in zip(names, _share_wait(tag, send, recv, thru, layout, after)):
            update(name, g)
        after = results[names[-1]][1]
    loss_all = lax.psum(loss[0, 0], ("x", "y", "c"))
    return (loss_all, grad_x[None], *[results[name][k] for k in range(4) for name in WEIGHTS])
```
